```python
import math
import jax, jax.numpy as jnp
from jax import lax
import numpy as np

D_MODEL = 1024
BATCH = 8
SEQ = 2048
DEPTH = 2

GRID_W = 64
CTX_LEN = 256
RMS_EPS = 1e-6
N_MOD = 9
D_FF = 2816
POOL_WINDOWS = (2, 4, 8, 16)
POOL_GROUPS = len(POOL_WINDOWS)
POOL_DIM = D_MODEL // 2
POOL_GROUP_DIM = POOL_DIM // POOL_GROUPS
MLA_HEADS = D_MODEL // 128
QK_NOPE_DIM = 64
QK_ROPE_DIM = 32
QK_HEAD_DIM = QK_NOPE_DIM + QK_ROPE_DIM
V_HEAD_DIM = 64
Q_LORA_RANK = 768
KV_LORA_RANK = 256
ROPE_AXIS_DIM = QK_ROPE_DIM // 2
ROPE_THETA = 10000.0
ATTN_SCALE = 1.0 / math.sqrt(QK_HEAD_DIM)
QBLOCK = 128
AB_IN_DIM = POOL_DIM + Q_LORA_RANK + KV_LORA_RANK + QK_ROPE_DIM
AB_OUT_DIM = POOL_DIM + MLA_HEADS * V_HEAD_DIM
CONV_DIM = D_MODEL
CONV_WIDTH = 3
N_EVEN = (DEPTH + 1) // 2
N_ODD = DEPTH // 2

kernel_name = 'hybrid_pool_mla_shortconv_dit_block'


def rmsnorm(x, g):
    xf = x.astype(jnp.float32)
    y = xf * lax.rsqrt(jnp.mean(xf * xf, axis=-1, keepdims=True) + RMS_EPS)
    return (y * g.astype(jnp.float32)).astype(x.dtype)


def modulation(cond, w, b):
    m = jax.nn.silu(cond) @ w + b
    return m.reshape(cond.shape[0], 1, N_MOD, cond.shape[-1])


def adaln(s, gain, m, k):
    shift, scale, gate = m[:, :, 3 * k], m[:, :, 3 * k + 1], m[:, :, 3 * k + 2]
    return rmsnorm(s, gain) * (1 + scale) + shift, gate


def swiglu(u, wg, wu, wd):
    return (jax.nn.silu(u @ wg) * (u @ wu)) @ wd


def ffn_half(s, m, k, gain, wg, wu, wd):
    u, gate = adaln(s, gain, m, k)
    return s + 0.5 * gate * swiglu(u, wg, wu, wd)


def multiscale_pool(u, w_grp, scale):
    b, l, _ = u.shape
    uf = u.astype(jnp.float32).reshape(b, l, POOL_GROUPS, POOL_GROUP_DIM)
    cs = jnp.concatenate([jnp.zeros_like(uf[:, :1]), jnp.cumsum(uf, axis=1)], axis=1)
    t = jnp.arange(l)
    outs = []
    for gi, w in enumerate(POOL_WINDOWS):
        lo = jnp.clip(t - w // 2, 0, l - 1)
        hi = jnp.clip(t + (w - w // 2 - 1), 0, l - 1)
        win_sum = cs[:, hi + 1, gi] - cs[:, lo, gi]
        cnt = (hi - lo + 1).astype(jnp.float32)[None, :, None]
        outs.append(win_sum / cnt - uf[:, :, gi])
    p = jnp.stack(outs, axis=2).astype(u.dtype)
    y = jnp.einsum('blgc,gcd->blgd', p, w_grp).reshape(b, l, POOL_DIM)
    return y * scale


def axial_rope_tables(length):
    rows = length // GRID_W
    row = jnp.repeat(jnp.arange(rows), GRID_W).astype(jnp.float32)
    col = jnp.tile(jnp.arange(GRID_W), rows).astype(jnp.float32)
    freqs = jnp.power(ROPE_THETA, -jnp.arange(0, ROPE_AXIS_DIM, 2, dtype=jnp.float32) / ROPE_AXIS_DIM)
    ang_r = (row[:, None] * freqs)[:, None, :]
    ang_c = (col[:, None] * freqs)[:, None, :]
    return (jnp.cos(ang_r), jnp.sin(ang_r), jnp.cos(ang_c), jnp.sin(ang_c))


def rot_half(z, cos, sin):
    h = z.shape[-1] // 2
    z1, z2 = z[..., :h], z[..., h:]
    return jnp.concatenate([z1 * cos - z2 * sin, z2 * cos + z1 * sin], axis=-1)


def apply_axial_rope(z, tabs):
    cos_r, sin_r, cos_c, sin_c = (tb.astype(z.dtype) for tb in tabs)
    return jnp.concatenate([rot_half(z[..., :ROPE_AXIS_DIM], cos_r, sin_r),
                            rot_half(z[..., ROPE_AXIS_DIM:], cos_c, sin_c)], axis=-1)


def mla_queries(cq, q_norm_g, w_uq):
    b, l, _ = cq.shape
    q = (rmsnorm(cq, q_norm_g) @ w_uq).reshape(b, l, MLA_HEADS, QK_HEAD_DIM)
    return q[..., :QK_NOPE_DIM], q[..., QK_NOPE_DIM:]


def mla_keys_values(ckv, kv_norm_g, w_ukv):
    b, l, _ = ckv.shape
    kv = (rmsnorm(ckv, kv_norm_g) @ w_ukv).reshape(b, l, MLA_HEADS, QK_NOPE_DIM + V_HEAD_DIM)
    return kv[..., :QK_NOPE_DIM], kv[..., QK_NOPE_DIM:]


def mla_attention(q_nope, q_rope, k_nope, k_rope, v):
    b, l, h, _ = q_nope.shape
    nb = l // QBLOCK

    def to_blocks(z):
        return z.reshape(b, nb, QBLOCK, h, z.shape[-1]).swapaxes(0, 1)

    def block(args):
        qn, qr = args
        s = (jnp.einsum('bqhd,bkhd->bhqk', qn, k_nope, preferred_element_type=jnp.float32)
             + jnp.einsum('bqhr,bkr->bhqk', qr, k_rope, preferred_element_type=jnp.float32))
        p = jax.nn.softmax(s * ATTN_SCALE, axis=-1).astype(v.dtype)
        return jnp.einsum('bhqk,bkhd->bqhd', p, v)

    o = lax.map(block, (to_blocks(q_nope), to_blocks(q_rope)))
    return o.swapaxes(0, 1).reshape(b, l, h * V_HEAD_DIM)


def pool_mla_mixer(uh, ug, tabs, w_in, pool_w, pool_scale, q_norm_g, w_uq, kv_norm_g, w_ukv, w_out, ctx_out):
    cuts = [POOL_DIM, POOL_DIM + Q_LORA_RANK, POOL_DIM + Q_LORA_RANK + KV_LORA_RANK]
    pool_h, cq_h, ckv_h, kr_h = jnp.split(uh @ w_in, cuts, axis=-1)
    pool_g, cq_g, ckv_g, kr_g = jnp.split(ug @ w_in, cuts, axis=-1)
    qn_h, qr_h = mla_queries(cq_h, q_norm_g, w_uq)
    qr_h = apply_axial_rope(qr_h, tabs)
    kn_h, v_h = mla_keys_values(ckv_h, kv_norm_g, w_ukv)
    kr_h = apply_axial_rope(kr_h[:, :, None, :], tabs)[:, :, 0]
    kn_g, v_g = mla_keys_values(ckv_g, kv_norm_g, w_ukv)
    attn_h = mla_attention(qn_h, qr_h,
                           jnp.concatenate([kn_h, kn_g], axis=1),
                           jnp.concatenate([kr_h, kr_g], axis=1),
                           jnp.concatenate([v_h, v_g], axis=1))
    out_h = jnp.concatenate([multiscale_pool(pool_h, pool_w, pool_scale), attn_h], axis=-1) @ w_out
    out_g = None
    if ctx_out:
        qn_g, qr_g = mla_queries(cq_g, q_norm_g, w_uq)
        attn_g = mla_attention(qn_g, qr_g, kn_g, kr_g, v_g)
        out_g = jnp.concatenate([multiscale_pool(pool_g, pool_w, pool_scale), attn_g], axis=-1) @ w_out
    return out_h, out_g


def short_conv_mixer(u, w_in, conv_w, w_out):
    b_gate, c_gate, val = jnp.split(u @ w_in, 3, axis=-1)
    z = c_gate * val
    z = lax.conv_general_dilated(z, conv_w[:, None, :].astype(z.dtype), window_strides=(1,),
                                 padding=((CONV_WIDTH // 2, CONV_WIDTH // 2),),
                                 dimension_numbers=('NWC', 'WIO', 'NWC'),
                                 feature_group_count=z.shape[-1])
    return (b_gate * z) @ w_out


def _fwd_setup_inputs(seed: int = 0) -> dict:
    key = jax.random.key(seed)
    ks = jax.random.split(key, 24)

    def nrm(k, shape, scale):
        return jax.random.normal(k, shape, jnp.float32) * scale

    D, F = D_MODEL, D_FF
    return {
        'x': nrm(ks[0], (BATCH, SEQ, D), 1.0),
        'c': nrm(ks[1], (BATCH, D), 1.0),
        'ctx': nrm(ks[2], (BATCH, CTX_LEN, D), 1.0),
        'c_ctx': nrm(ks[3], (D,), 1.0),
        'norm_g': 1.0 + nrm(ks[4], (DEPTH, 3, D), 0.02),
        'w_mod': nrm(ks[5], (DEPTH, D, N_MOD * D), 0.5 * D ** -0.5),
        'b_mod': nrm(ks[6], (DEPTH, N_MOD * D), 0.02),
        'ffn_w_gate': nrm(ks[7], (DEPTH, 2, D, F), D ** -0.5),
        'ffn_w_up': nrm(ks[8], (DEPTH, 2, D, F), D ** -0.5),
        'ffn_w_down': nrm(ks[9], (DEPTH, 2, F, D), F ** -0.5),
        'ab_w_in': nrm(ks[10], (N_EVEN, D, AB_IN_DIM), D ** -0.5),
        'pool_w': nrm(ks[11], (N_EVEN, POOL_GROUPS, POOL_GROUP_DIM, POOL_GROUP_DIM), POOL_GROUP_DIM ** -0.5),
        'pool_scale': 1.0 + nrm(ks[12], (N_EVEN, POOL_DIM), 0.1),
        'q_norm_g': 1.0 + nrm(ks[13], (N_EVEN, Q_LORA_RANK), 0.02),
        'w_uq': nrm(ks[14], (N_EVEN, Q_LORA_RANK, MLA_HEADS * QK_HEAD_DIM), Q_LORA_RANK ** -0.5),
        'kv_norm_g': 1.0 + nrm(ks[15], (N_EVEN, KV_LORA_RANK), 0.02),
        'w_ukv': nrm(ks[16], (N_EVEN, KV_LORA_RANK, MLA_HEADS * (QK_NOPE_DIM + V_HEAD_DIM)), KV_LORA_RANK ** -0.5),
        'ab_w_out': nrm(ks[17], (N_EVEN, AB_OUT_DIM, D), AB_OUT_DIM ** -0.5),
        'conv_w_in': nrm(ks[18], (N_ODD, D, 3 * CONV_DIM), D ** -0.5),
        'conv_w': nrm(ks[19], (N_ODD, CONV_WIDTH, CONV_DIM), CONV_WIDTH ** -0.5),
        'conv_w_out': nrm(ks[20], (N_ODD, CONV_DIM, D), CONV_DIM ** -0.5),
        'final_norm_g': 1.0 + nrm(ks[21], (D,), 0.02),
    }


def _fwd_reference(x, c, ctx, c_ctx, norm_g, w_mod, b_mod, ffn_w_gate, ffn_w_up, ffn_w_down,
              ab_w_in, pool_w, pool_scale, q_norm_g, w_uq, kv_norm_g, w_ukv, ab_w_out,
              conv_w_in, conv_w, conv_w_out, final_norm_g):
    tabs = axial_rope_tables(x.shape[1])
    h, g = x, ctx
    cond_g = c_ctx[None, :]
    for i in range(DEPTH):
        last = i == DEPTH - 1
        even = i % 2 == 0
        j = i // 2
        ctx_out = not last
        need_g = even or ctx_out
        m_h = modulation(c, w_mod[i], b_mod[i])
        h = ffn_half(h, m_h, 0, norm_g[i, 0], ffn_w_gate[i, 0], ffn_w_up[i, 0], ffn_w_down[i, 0])
        if need_g:
            m_g = modulation(cond_g, w_mod[i], b_mod[i])
            g = ffn_half(g, m_g, 0, norm_g[i, 0], ffn_w_gate[i, 0], ffn_w_up[i, 0], ffn_w_down[i, 0])
        uh, gate_h = adaln(h, norm_g[i, 1], m_h, 1)
        out_g = None
        if even:
            ug, gate_g = adaln(g, norm_g[i, 1], m_g, 1)
            out_h, out_g = pool_mla_mixer(uh, ug, tabs, ab_w_in[j], pool_w[j], pool_scale[j],
                                          q_norm_g[j], w_uq[j], kv_norm_g[j], w_ukv[j], ab_w_out[j], ctx_out)
        else:
            out_h = short_conv_mixer(uh, conv_w_in[j], conv_w[j], conv_w_out[j])
            if ctx_out:
                ug, gate_g = adaln(g, norm_g[i, 1], m_g, 1)
                out_g = short_conv_mixer(ug, conv_w_in[j], conv_w[j], conv_w_out[j])
        h = h + gate_h * out_h
        h = ffn_half(h, m_h, 2, norm_g[i, 2], ffn_w_gate[i, 1], ffn_w_up[i, 1], ffn_w_down[i, 1])
        if ctx_out:
            g = g + gate_g * out_g
            g = ffn_half(g, m_g, 2, norm_g[i, 2], ffn_w_gate[i, 1], ffn_w_up[i, 1], ffn_w_down[i, 1])
    return rmsnorm(h, final_norm_g)


import jax as _jax
import jax.numpy as _jnp

TWIN_FORMAT = 'train_step'
FWD_PARAMS = ['x', 'c', 'ctx', 'c_ctx', 'norm_g', 'w_mod', 'b_mod', 'ffn_w_gate', 'ffn_w_up', 'ffn_w_down', 'ab_w_in', 'pool_w', 'pool_scale', 'q_norm_g', 'w_uq', 'kv_norm_g', 'w_ukv', 'ab_w_out', 'conv_w_in', 'conv_w', 'conv_w_out', 'final_norm_g']
TWIN_WEIGHTS = ['c_ctx', 'norm_g', 'w_mod', 'b_mod', 'ffn_w_gate', 'ffn_w_up', 'ffn_w_down', 'ab_w_in', 'pool_w', 'pool_scale', 'q_norm_g', 'w_uq', 'kv_norm_g', 'w_ukv', 'ab_w_out', 'conv_w_in', 'conv_w', 'conv_w_out', 'final_norm_g']
TWIN_DIFF_INPUT = 'x'
TWIN_INPUTS = ['x', 'c', 'ctx', 'c_ctx', 'norm_g', 'w_mod', 'b_mod', 'ffn_w_gate', 'ffn_w_up', 'ffn_w_down', 'ab_w_in', 'pool_w', 'pool_scale', 'q_norm_g', 'w_uq', 'kv_norm_g', 'w_ukv', 'ab_w_out', 'conv_w_in', 'conv_w', 'conv_w_out', 'final_norm_g', 'loss_target', 'm_c_ctx', 'm_norm_g', 'm_w_mod', 'm_b_mod', 'm_ffn_w_gate', 'm_ffn_w_up', 'm_ffn_w_down', 'm_ab_w_in', 'm_pool_w', 'm_pool_scale', 'm_q_norm_g', 'm_w_uq', 'm_kv_norm_g', 'm_w_ukv', 'm_ab_w_out', 'm_conv_w_in', 'm_conv_w', 'm_conv_w_out', 'm_final_norm_g', 'v_c_ctx', 'v_norm_g', 'v_w_mod', 'v_b_mod', 'v_ffn_w_gate', 'v_ffn_w_up', 'v_ffn_w_down', 'v_ab_w_in', 'v_pool_w', 'v_pool_scale', 'v_q_norm_g', 'v_w_uq', 'v_kv_norm_g', 'v_w_ukv', 'v_ab_w_out', 'v_conv_w_in', 'v_conv_w', 'v_conv_w_out', 'v_final_norm_g']
TWIN_OUTPUTS = ['loss', 'grad_x', 'grad_c_ctx', 'grad_norm_g', 'grad_w_mod', 'grad_b_mod', 'grad_ffn_w_gate', 'grad_ffn_w_up', 'grad_ffn_w_down', 'grad_ab_w_in', 'grad_pool_w', 'grad_pool_scale', 'grad_q_norm_g', 'grad_w_uq', 'grad_kv_norm_g', 'grad_w_ukv', 'grad_ab_w_out', 'grad_conv_w_in', 'grad_conv_w', 'grad_conv_w_out', 'grad_final_norm_g', 'delta_c_ctx', 'delta_norm_g', 'delta_w_mod', 'delta_b_mod', 'delta_ffn_w_gate', 'delta_ffn_w_up', 'delta_ffn_w_down', 'delta_ab_w_in', 'delta_pool_w', 'delta_pool_scale', 'delta_q_norm_g', 'delta_w_uq', 'delta_kv_norm_g', 'delta_w_ukv', 'delta_ab_w_out', 'delta_conv_w_in', 'delta_conv_w', 'delta_conv_w_out', 'delta_final_norm_g', 'new_m_c_ctx', 'new_m_norm_g', 'new_m_w_mod', 'new_m_b_mod', 'new_m_ffn_w_gate', 'new_m_ffn_w_up', 'new_m_ffn_w_down', 'new_m_ab_w_in', 'new_m_pool_w', 'new_m_pool_scale', 'new_m_q_norm_g', 'new_m_w_uq', 'new_m_kv_norm_g', 'new_m_w_ukv', 'new_m_ab_w_out', 'new_m_conv_w_in', 'new_m_conv_w', 'new_m_conv_w_out', 'new_m_final_norm_g', 'new_v_c_ctx', 'new_v_norm_g', 'new_v_w_mod', 'new_v_b_mod', 'new_v_ffn_w_gate', 'new_v_ffn_w_up', 'new_v_ffn_w_down', 'new_v_ab_w_in', 'new_v_pool_w', 'new_v_pool_scale', 'new_v_q_norm_g', 'new_v_w_uq', 'new_v_kv_norm_g', 'new_v_w_ukv', 'new_v_ab_w_out', 'new_v_conv_w_in', 'new_v_conv_w', 'new_v_conv_w_out', 'new_v_final_norm_g']
TWIN_LEAF_KINDS = {'loss': 'loss', 'grad_x': 'grad_x', 'grad_c_ctx': 'grad_w', 'grad_norm_g': 'grad_w', 'grad_w_mod': 'grad_w', 'grad_b_mod': 'grad_w', 'grad_ffn_w_gate': 'grad_w', 'grad_ffn_w_up': 'grad_w', 'grad_ffn_w_down': 'grad_w', 'grad_ab_w_in': 'grad_w', 'grad_pool_w': 'grad_w', 'grad_pool_scale': 'grad_w', 'grad_q_norm_g': 'grad_w', 'grad_w_uq': 'grad_w', 'grad_kv_norm_g': 'grad_w', 'grad_w_ukv': 'grad_w', 'grad_ab_w_out': 'grad_w', 'grad_conv_w_in': 'grad_w', 'grad_conv_w': 'grad_w', 'grad_conv_w_out': 'grad_w', 'grad_final_norm_g': 'grad_w', 'delta_c_ctx': 'delta_w', 'delta_norm_g': 'delta_w', 'delta_w_mod': 'delta_w', 'delta_b_mod': 'delta_w', 'delta_ffn_w_gate': 'delta_w', 'delta_ffn_w_up': 'delta_w', 'delta_ffn_w_down': 'delta_w', 'delta_ab_w_in': 'delta_w', 'delta_pool_w': 'delta_w', 'delta_pool_scale': 'delta_w', 'delta_q_norm_g': 'delta_w', 'delta_w_uq': 'delta_w', 'delta_kv_norm_g': 'delta_w', 'delta_w_ukv': 'delta_w', 'delta_ab_w_out': 'delta_w', 'delta_conv_w_in': 'delta_w', 'delta_conv_w': 'delta_w', 'delta_conv_w_out': 'delta_w', 'delta_final_norm_g': 'delta_w', 'new_m_c_ctx': 'new_m', 'new_m_norm_g': 'new_m', 'new_m_w_mod': 'new_m', 'new_m_b_mod': 'new_m', 'new_m_ffn_w_gate': 'new_m', 'new_m_ffn_w_up': 'new_m', 'new_m_ffn_w_down': 'new_m', 'new_m_ab_w_in': 'new_m', 'new_m_pool_w': 'new_m', 'new_m_pool_scale': 'new_m', 'new_m_q_norm_g': 'new_m', 'new_m_w_uq': 'new_m', 'new_m_kv_norm_g': 'new_m', 'new_m_w_ukv': 'new_m', 'new_m_ab_w_out': 'new_m', 'new_m_conv_w_in': 'new_m', 'new_m_conv_w': 'new_m', 'new_m_conv_w_out': 'new_m', 'new_m_final_norm_g': 'new_m', 'new_v_c_ctx': 'new_v', 'new_v_norm_g': 'new_v', 'new_v_w_mod': 'new_v', 'new_v_b_mod': 'new_v', 'new_v_ffn_w_gate': 'new_v', 'new_v_ffn_w_up': 'new_v', 'new_v_ffn_w_down': 'new_v', 'new_v_ab_w_in': 'new_v', 'new_v_pool_w': 'new_v', 'new_v_pool_scale': 'new_v', 'new_v_q_norm_g': 'new_v', 'new_v_w_uq': 'new_v', 'new_v_kv_norm_g': 'new_v', 'new_v_w_ukv': 'new_v', 'new_v_ab_w_out': 'new_v', 'new_v_conv_w_in': 'new_v', 'new_v_conv_w': 'new_v', 'new_v_conv_w_out': 'new_v', 'new_v_final_norm_g': 'new_v'}


def _forward(args):
    return _fwd_reference(*[args[k] for k in FWD_PARAMS])


def _output_shape():
    out = _jax.eval_shape(lambda: _forward(_fwd_setup_inputs(0)))
    return out.shape, out.dtype

N_MICROBATCH = 1
ADAM_LR = 0.001
ADAM_B1 = 0.9
ADAM_B2 = 0.999
ADAM_EPS = 1e-08
ADAM_WD = 0.01
ADAM_STEP = 10
PER_EXAMPLE_BATCH_AXIS = {'x': 0, 'c': 0, 'ctx': 0, 'loss_target': 0}
SHARED_INPUTS = []
_WEIGHT_DTYPES = {'c_ctx': _jnp.float32, 'norm_g': _jnp.float32, 'w_mod': _jnp.float32, 'b_mod': _jnp.float32, 'ffn_w_gate': _jnp.float32, 'ffn_w_up': _jnp.float32, 'ffn_w_down': _jnp.float32, 'ab_w_in': _jnp.float32, 'pool_w': _jnp.float32, 'pool_scale': _jnp.float32, 'q_norm_g': _jnp.float32, 'w_uq': _jnp.float32, 'kv_norm_g': _jnp.float32, 'w_ukv': _jnp.float32, 'ab_w_out': _jnp.float32, 'conv_w_in': _jnp.float32, 'conv_w': _jnp.float32, 'conv_w_out': _jnp.float32, 'final_norm_g': _jnp.float32}
MOMENT_SCALE = {'c_ctx': 4.109286e-03, 'norm_g': 3.585826e-02, 'w_mod': 3.539739e-02, 'b_mod': 5.697261e-02, 'ffn_w_gate': 9.122495e-03, 'ffn_w_up': 8.835459e-03, 'ffn_w_down': 1.463230e-02, 'ab_w_in': 2.360231e-02, 'pool_w': 3.972119e-02, 'pool_scale': 3.862030e-02, 'q_norm_g': 3.491966e-03, 'w_uq': 3.447800e-03, 'kv_norm_g': 1.809060e-02, 'w_ukv': 8.193505e-03, 'ab_w_out': 2.898105e-02, 'conv_w_in': 4.365015e-02, 'conv_w': 4.444111e-02, 'conv_w_out': 4.399267e-02, 'final_norm_g': 1.605710e+01}


def _to_microbatches(a, axis):
    t = _jnp.moveaxis(a, axis, 0)
    t = t.reshape((N_MICROBATCH, t.shape[0] // N_MICROBATCH) + t.shape[1:])
    return _jnp.moveaxis(t, 1, axis + 1)


def setup_inputs(seed: int = 0) -> dict:
    inp = _fwd_setup_inputs(seed)
    key = _jax.random.fold_in(_jax.random.key(seed), 7919)
    shape, _ = _output_shape()
    out = dict(inp)
    out["loss_target"] = _jax.random.normal(_jax.random.fold_in(key, 0), shape, _jnp.float32)
    for i, name in enumerate(TWIN_WEIGHTS):
        w = inp[name].astype(_jnp.float32)
        if MOMENT_SCALE is None:
            s = _jnp.sqrt(_jnp.mean(_jnp.square(w)) + 1e-30)
        else:
            s = MOMENT_SCALE[name]
        km, kv = _jax.random.split(_jax.random.fold_in(key, i + 1))
        out[name] = w
        out["m_" + name] = s * _jax.random.normal(km, w.shape, _jnp.float32)
        out["v_" + name] = (s * s) * _jax.random.uniform(kv, w.shape, _jnp.float32, 0.5, 1.5)
    if N_MICROBATCH > 1:
        for name, axis in PER_EXAMPLE_BATCH_AXIS.items():
            out[name] = _to_microbatches(out[name], axis)
    return {'x': out['x'], 'c': out['c'], 'ctx': out['ctx'], 'c_ctx': out['c_ctx'], 'norm_g': out['norm_g'], 'w_mod': out['w_mod'], 'b_mod': out['b_mod'], 'ffn_w_gate': out['ffn_w_gate'], 'ffn_w_up': out['ffn_w_up'], 'ffn_w_down': out['ffn_w_down'], 'ab_w_in': out['ab_w_in'], 'pool_w': out['pool_w'], 'pool_scale': out['pool_scale'], 'q_norm_g': out['q_norm_g'], 'w_uq': out['w_uq'], 'kv_norm_g': out['kv_norm_g'], 'w_ukv': out['w_ukv'], 'ab_w_out': out['ab_w_out'], 'conv_w_in': out['conv_w_in'], 'conv_w': out['conv_w'], 'conv_w_out': out['conv_w_out'], 'final_norm_g': out['final_norm_g'], 'loss_target': out['loss_target'], 'm_c_ctx': out['m_c_ctx'], 'm_norm_g': out['m_norm_g'], 'm_w_mod': out['m_w_mod'], 'm_b_mod': out['m_b_mod'], 'm_ffn_w_gate': out['m_ffn_w_gate'], 'm_ffn_w_up': out['m_ffn_w_up'], 'm_ffn_w_down': out['m_ffn_w_down'], 'm_ab_w_in': out['m_ab_w_in'], 'm_pool_w': out['m_pool_w'], 'm_pool_scale': out['m_pool_scale'], 'm_q_norm_g': out['m_q_norm_g'], 'm_w_uq': out['m_w_uq'], 'm_kv_norm_g': out['m_kv_norm_g'], 'm_w_ukv': out['m_w_ukv'], 'm_ab_w_out': out['m_ab_w_out'], 'm_conv_w_in': out['m_conv_w_in'], 'm_conv_w': out['m_conv_w'], 'm_conv_w_out': out['m_conv_w_out'], 'm_final_norm_g': out['m_final_norm_g'], 'v_c_ctx': out['v_c_ctx'], 'v_norm_g': out['v_norm_g'], 'v_w_mod': out['v_w_mod'], 'v_b_mod': out['v_b_mod'], 'v_ffn_w_gate': out['v_ffn_w_gate'], 'v_ffn_w_up': out['v_ffn_w_up'], 'v_ffn_w_down': out['v_ffn_w_down'], 'v_ab_w_in': out['v_ab_w_in'], 'v_pool_w': out['v_pool_w'], 'v_pool_scale': out['v_pool_scale'], 'v_q_norm_g': out['v_q_norm_g'], 'v_w_uq': out['v_w_uq'], 'v_kv_norm_g': out['v_kv_norm_g'], 'v_w_ukv': out['v_w_ukv'], 'v_ab_w_out': out['v_ab_w_out'], 'v_conv_w_in': out['v_conv_w_in'], 'v_conv_w': out['v_conv_w'], 'v_conv_w_out': out['v_conv_w_out'], 'v_final_norm_g': out['v_final_norm_g']}


def _loss(weights, diff, rest, loss_target):
    with _jax.named_scope("forward"):
        args = {**rest, TWIN_DIFF_INPUT: diff, **{k: w.astype(_WEIGHT_DTYPES[k]) for k, w in weights.items()}}
        y = _forward(args)
    with _jax.named_scope("loss_head"):
        err = _jnp.square(y.astype(_jnp.float32) - loss_target)
        return 0.5 * _jnp.sum(_jnp.mean(err, axis=-1)) if err.ndim else 0.5 * err


def _adamw(w, g, m, v):
    m = ADAM_B1 * m + (1.0 - ADAM_B1) * g
    v = ADAM_B2 * v + (1.0 - ADAM_B2) * _jnp.square(g)
    m_hat = m / (1.0 - ADAM_B1 ** ADAM_STEP)
    v_hat = v / (1.0 - ADAM_B2 ** ADAM_STEP)
    delta = -ADAM_LR * (m_hat / (_jnp.sqrt(v_hat) + ADAM_EPS) + ADAM_WD * w)
    return delta, m, v


def reference(x, c, ctx, c_ctx, norm_g, w_mod, b_mod, ffn_w_gate, ffn_w_up, ffn_w_down, ab_w_in, pool_w, pool_scale, q_norm_g, w_uq, kv_norm_g, w_ukv, ab_w_out, conv_w_in, conv_w, conv_w_out, final_norm_g, loss_target, m_c_ctx, m_norm_g, m_w_mod, m_b_mod, m_ffn_w_gate, m_ffn_w_up, m_ffn_w_down, m_ab_w_in, m_pool_w, m_pool_scale, m_q_norm_g, m_w_uq, m_kv_norm_g, m_w_ukv, m_ab_w_out, m_conv_w_in, m_conv_w, m_conv_w_out, m_final_norm_g, v_c_ctx, v_norm_g, v_w_mod, v_b_mod, v_ffn_w_gate, v_ffn_w_up, v_ffn_w_down, v_ab_w_in, v_pool_w, v_pool_scale, v_q_norm_g, v_w_uq, v_kv_norm_g, v_w_ukv, v_ab_w_out, v_conv_w_in, v_conv_w, v_conv_w_out, v_final_norm_g):
    given = dict(x=x, c=c, ctx=ctx, c_ctx=c_ctx, norm_g=norm_g, w_mod=w_mod, b_mod=b_mod, ffn_w_gate=ffn_w_gate, ffn_w_up=ffn_w_up, ffn_w_down=ffn_w_down, ab_w_in=ab_w_in, pool_w=pool_w, pool_scale=pool_scale, q_norm_g=q_norm_g, w_uq=w_uq, kv_norm_g=kv_norm_g, w_ukv=w_ukv, ab_w_out=ab_w_out, conv_w_in=conv_w_in, conv_w=conv_w, conv_w_out=conv_w_out, final_norm_g=final_norm_g, loss_target=loss_target, m_c_ctx=m_c_ctx, m_norm_g=m_norm_g, m_w_mod=m_w_mod, m_b_mod=m_b_mod, m_ffn_w_gate=m_ffn_w_gate, m_ffn_w_up=m_ffn_w_up, m_ffn_w_down=m_ffn_w_down, m_ab_w_in=m_ab_w_in, m_pool_w=m_pool_w, m_pool_scale=m_pool_scale, m_q_norm_g=m_q_norm_g, m_w_uq=m_w_uq, m_kv_norm_g=m_kv_norm_g, m_w_ukv=m_w_ukv, m_ab_w_out=m_ab_w_out, m_conv_w_in=m_conv_w_in, m_conv_w=m_conv_w, m_conv_w_out=m_conv_w_out, m_final_norm_g=m_final_norm_g, v_c_ctx=v_c_ctx, v_norm_g=v_norm_g, v_w_mod=v_w_mod, v_b_mod=v_b_mod, v_ffn_w_gate=v_ffn_w_gate, v_ffn_w_up=v_ffn_w_up, v_ffn_w_down=v_ffn_w_down, v_ab_w_in=v_ab_w_in, v_pool_w=v_pool_w, v_pool_scale=v_pool_scale, v_q_norm_g=v_q_norm_g, v_w_uq=v_w_uq, v_kv_norm_g=v_kv_norm_g, v_w_ukv=v_w_ukv, v_ab_w_out=v_ab_w_out, v_conv_w_in=v_conv_w_in, v_conv_w=v_conv_w, v_conv_w_out=v_conv_w_out, v_final_norm_g=v_final_norm_g)
    weights = {n: given[n] for n in TWIN_WEIGHTS}
    shared = {n: given[n] for n in SHARED_INPUTS}
    per_example = {n: given[n] for n in ['x', 'c', 'ctx']}
    grad_fn = _jax.value_and_grad(_loss, argnums=(0, 1))

    def one_microbatch(ex, loss_target):
        ex = dict(ex)
        diff = ex.pop(TWIN_DIFF_INPUT)
        return grad_fn(weights, diff, {**shared, **ex}, loss_target)

    if N_MICROBATCH == 1:
        loss, (grad_w, grad_x) = one_microbatch(per_example, given["loss_target"])
    else:
        def body(carry, xs):
            loss_sum, grad_sum = carry
            l_k, (gw_k, gx_k) = one_microbatch(xs[0], xs[1])
            with _jax.named_scope("update"):
                return (loss_sum + l_k, _jax.tree.map(_jnp.add, grad_sum, gw_k)), gx_k

        init = (_jnp.zeros((), _jnp.float32), _jax.tree.map(_jnp.zeros_like, weights))
        (loss, grad_w), grad_x = _jax.lax.scan(body, init, (per_example, given["loss_target"]))
    with _jax.named_scope("update"):
        delta_w, new_m, new_v = {}, {}, {}
        for n in TWIN_WEIGHTS:
            delta_w[n], new_m[n], new_v[n] = _adamw(weights[n], grad_w[n], given["m_" + n], given["v_" + n])
    return (loss, grad_x, *[grad_w[n] for n in TWIN_WEIGHTS], *[delta_w[n] for n in TWIN_WEIGHTS],
            *[new_m[n] for n in TWIN_WEIGHTS], *[new_v[n] for n in TWIN_WEIGHTS])
```

```python
import functools
import math

import jax
import jax.numpy as jnp
from jax import lax
from jax.experimental import pallas as pl
from jax.experimental.pallas import tpu as pltpu

F32 = jnp.float32
BF = jnp.bfloat16
MESH = pl.DeviceIdType.MESH

N_DEV = 8
N_SHARD = 4
RMS_EPS = 1e-6
N_MOD = 9
POOL_WINDOWS = (2, 4, 8, 16)
QK_NOPE = 64
QK_ROPE = 32
V_HEAD = 64
HEAD_PAD = 128
GRID_W = 64
ROPE_THETA = 10000.0
POOL_PAD = 16
ADAM_LR, ADAM_B1, ADAM_B2, ADAM_EPS, ADAM_WD, ADAM_STEP = 0.001, 0.9, 0.999, 1e-08, 0.01, 10
VMEM_LIMIT = 56 * 1024 * 1024


def _pcall(body, **kw):
    return pl.pallas_call(body, **kw)


def _params(sem=None):
    return pltpu.CompilerParams(dimension_semantics=sem, vmem_limit_bytes=VMEM_LIMIT)


def _pick(n, pref, mult=128):
    best = None
    d = mult
    while d <= min(n, pref):
        if n % d == 0:
            best = d
        d += mult
    return best if best is not None else n


def _silu(z):
    return z * jax.nn.sigmoid(z)


def _dsilu(z):
    s = jax.nn.sigmoid(z)
    return s * (1.0 + z * (1.0 - s))


def _dot(a, b, dims):
    return lax.dot_general(a.astype(BF), b.astype(BF), (dims, ((), ())), preferred_element_type=F32)


NN = ((1,), (0,))
NT = ((1,), (1,))
TN = ((0,), (0,))


def _exchange(name, arrays, src_mode, pair=False):
    n = len(arrays)
    flips = [(0, 0, 1)] if pair else [(kx, ky, kc) for kx in (0, 1) for ky in (0, 1) for kc in (0, 1)
                                      if (kx, ky, kc) != (0, 0, 0)]
    nk = len(flips)
    lead = 2 if pair else N_DEV
    blk_shapes = [a.shape if src_mode == 'whole' else a.shape[1:] for a in arrays]

    def body(*refs):
        ins, outs = refs[:n], refs[n:2 * n]
        send_sems, recv_sems, loc_sems = refs[2 * n:]
        x, y, c = lax.axis_index("x"), lax.axis_index("y"), lax.axis_index("c")

        def place(px, py, pc):
            return pc if pair else 4 * px + 2 * py + pc

        me = place(x, y, c)

        def src(a, px, py, pc):
            if src_mode == 'whole':
                return ins[a]
            if src_mode == 'half':
                return ins[a].at[c]
            return ins[a].at[place(px, py, pc)]

        copies = []
        for a in range(n):
            loc = pltpu.make_async_copy(src(a, x, y, c), outs[a].at[me], loc_sems.at[a])
            loc.start()
            copies.append(loc)
            for ki, (kx, ky, kc) in enumerate(flips):
                px = 1 - x if kx else x
                py = 1 - y if ky else y
                pc = 1 - c if kc else c
                cp = pltpu.make_async_remote_copy(
                    src_ref=src(a, px, py, pc), dst_ref=outs[a].at[me],
                    send_sem=send_sems.at[a * nk + ki], recv_sem=recv_sems.at[a * nk + ki],
                    device_id=(px, py, pc), device_id_type=MESH)
                cp.start()
                copies.append(cp)
        for cp in copies:
            cp.wait()

    any_spec = pl.BlockSpec(memory_space=pl.ANY)
    outs = _pcall(
        body, name=name,
        out_shape=[jax.ShapeDtypeStruct((lead,) + tuple(s), a.dtype) for s, a in zip(blk_shapes, arrays)],
        in_specs=[any_spec] * n, out_specs=[any_spec] * n,
        scratch_shapes=[pltpu.SemaphoreType.DMA((n * nk,)), pltpu.SemaphoreType.DMA((n * nk,)),
                        pltpu.SemaphoreType.DMA((n,))],
    )(*arrays)
    return list(outs)


def _sum_lead(name, arr):
    n, r, cdim = arr.shape
    tr = r
    limit = (4 << 20) // (n * cdim * arr.dtype.itemsize)
    if r > limit:
        tr = _pick(r, max(limit, 16), 16)

    def body(x_ref, o_ref):
        acc = x_ref[0].astype(F32)
        for d in range(1, n):
            acc = acc + x_ref[d].astype(F32)
        o_ref[...] = acc

    return _pcall(body, name=name, grid=(r // tr,),
                  in_specs=[pl.BlockSpec((n, tr, cdim), lambda i: (0, i, 0))],
                  out_specs=pl.BlockSpec((tr, cdim), lambda i: (i, 0)),
                  out_shape=jax.ShapeDtypeStruct((r, cdim), F32),
                  compiler_params=_params(("arbitrary",)))(arr)


def _rows_call(name, fn, n_rows, tm, rows, consts, mod, outs, acc_w=None, h_tiles=None):
    nt = n_rows // tm
    ht = nt if h_tiles is None else h_tiles
    ng = 1 if mod is None else mod.shape[0]
    n_r, n_c, n_o = len(rows), len(consts), len(outs)
    has_mod = mod is not None

    def body(*refs):
        i = pl.program_id(0)
        first = (i % ht) == 0
        row_refs, const_refs = refs[:n_r], refs[n_r:n_r + n_c]
        p = n_r + n_c
        mod_tile = refs[p][...] if has_mod else None
        p += int(has_mod)
        out_refs = refs[p:p + n_o]
        o, acc = fn([r[...] for r in row_refs], [r[...] for r in const_refs], mod_tile)
        for r, v in zip(out_refs, o):
            r[...] = v.astype(r.dtype)
        if acc_w is not None:
            acc_ref = refs[p + n_o]

            @pl.when(first)
            def _():
                acc_ref[...] = jnp.zeros_like(acc_ref)

            for k, v in acc.items():
                acc_ref[k:k + 1, :] += v

    in_specs = [pl.BlockSpec((tm, r.shape[1]), lambda i: (i, 0)) for r in rows]
    in_specs += [pl.BlockSpec(cst.shape, lambda i, nd=cst.ndim: (0,) * nd) for cst in consts]
    args = list(rows) + list(consts)
    if has_mod:
        in_specs.append(pl.BlockSpec((None,) + mod.shape[1:], lambda i: (i // ht, 0, 0)))
        args.append(mod)
    out_shape = [jax.ShapeDtypeStruct((n_rows, w), dt) for w, dt in outs]
    out_specs = [pl.BlockSpec((tm, w), lambda i: (i, 0)) for w, _ in outs]
    if acc_w is not None:
        out_shape.append(jax.ShapeDtypeStruct((ng, 8, acc_w), F32))
        out_specs.append(pl.BlockSpec((None, 8, acc_w), lambda i: (i // ht, 0, 0)))
    res = _pcall(body, name=name, grid=(nt,), in_specs=in_specs, out_specs=out_specs, out_shape=out_shape,
                 compiler_params=_params(("arbitrary",)))(*args)
    return list(res)


def _rms(s):
    r = lax.rsqrt(jnp.mean(s * s, axis=1, keepdims=True) + RMS_EPS)
    return s * r, r


def _rms_bwd(dn, n, r):
    return r * (dn - n * jnp.mean(dn * n, axis=1, keepdims=True))


def _adaln_fwd(name, s, gains, gain_row, mod, k, tm, h_tiles):
    def fn(rows, consts, m):
        n, _ = _rms(rows[0])
        y = n * consts[0][gain_row:gain_row + 1, :]
        return [y * (1.0 + m[3 * k + 1:3 * k + 2, :]) + m[3 * k:3 * k + 1, :]], {}

    d = s.shape[1]
    return _rows_call(name, fn, s.shape[0], tm, [s], [gains], mod, [(d, BF)], h_tiles=h_tiles)[0]


def _adaln_bwd(name, s, du, ds_res, gains, gain_row, mod, k, tm, h_tiles):
    def fn(rows, consts, m):
        sv, duv, res = rows
        gain = consts[0][gain_row:gain_row + 1, :]
        n, r = _rms(sv)
        y = n * gain
        dy = duv * (1.0 + m[3 * k + 1:3 * k + 2, :])
        acc = {0: jnp.sum(duv, axis=0, keepdims=True), 1: jnp.sum(duv * y, axis=0, keepdims=True),
               2: jnp.sum(dy * n, axis=0, keepdims=True)}
        return [_rms_bwd(dy * gain, n, r) + res], acc

    d = s.shape[1]
    return _rows_call(name, fn, s.shape[0], tm, [s, du, ds_res], [gains], mod, [(d, F32)], acc_w=d, h_tiles=h_tiles)


def _resid_bwd(name, ds_out, o, mod, k, cst, tm, h_tiles):
    def fn(rows, consts, m):
        dsv, ov = rows
        gate = m[3 * k + 2:3 * k + 3, :]
        return [cst * gate * dsv], {0: jnp.sum(cst * ov * dsv, axis=0, keepdims=True)}

    d = o.shape[1]
    return _rows_call(name, fn, o.shape[0], tm, [ds_out, o], [], mod, [(d, BF)], acc_w=d, h_tiles=h_tiles)


def _mm(name, a, b, mode, tm=256, tn=512, out_dtypes=(F32,), epi=None, epi_args=(), epi_kinds=(), a_pre=None):
    if mode == 'nn':
        (m, kd), nd = a.shape, b.shape[1]
    elif mode == 'nt':
        (m, kd), nd = a.shape, b.shape[0]
    else:
        (kd, m), nd = a.shape, b.shape[1]
    tm = _pick(m, tm, 16) if m % tm else tm
    tn = _pick(nd, tn, 128) if nd % tn else tn
    dims = {'nn': NN, 'nt': NT, 'tn': TN}[mode]
    n_e, n_o = len(epi_args), len(out_dtypes)

    def body(*refs):
        i = pl.program_id(1)
        av = refs[0][...]
        if a_pre is not None:
            av = a_pre(av)
        acc = _dot(av, refs[1][...], dims)
        res = (acc,) if epi is None else epi(acc, i, *[r[...] for r in refs[2:2 + n_e]])
        for r, v in zip(refs[2 + n_e:], res):
            r[...] = v.astype(r.dtype)

    if mode == 'nn':
        specs = [pl.BlockSpec((tm, kd), lambda j, i: (i, 0)), pl.BlockSpec((kd, tn), lambda j, i: (0, j))]
    elif mode == 'nt':
        specs = [pl.BlockSpec((tm, kd), lambda j, i: (i, 0)), pl.BlockSpec((tn, kd), lambda j, i: (j, 0))]
    else:
        specs = [pl.BlockSpec((kd, tm), lambda j, i: (0, i)), pl.BlockSpec((kd, tn), lambda j, i: (0, j))]
    for arr, kind in zip(epi_args, epi_kinds):
        if kind == 'mn':
            specs.append(pl.BlockSpec((tm, tn), lambda j, i: (i, j)))
        elif kind == 'n':
            specs.append(pl.BlockSpec((1, tn), lambda j, i: (0, j)))
        elif kind == 'mt':
            specs.append(pl.BlockSpec((tm, arr.shape[1]), lambda j, i: (i, 0)))
        else:
            specs.append(pl.BlockSpec(arr.shape, lambda j, i, nd_=arr.ndim: (0,) * nd_))
    res = _pcall(body, name=name, grid=(nd // tn, m // tm), in_specs=specs,
                 out_specs=[pl.BlockSpec((tm, tn), lambda j, i: (i, j))] * n_o,
                 out_shape=[jax.ShapeDtypeStruct((m, nd), dt) for dt in out_dtypes],
                 compiler_params=_params(("arbitrary", "arbitrary")))(a, b, *epi_args)
    return res[0] if n_o == 1 else list(res)


def _row_gate(mod, k3, i, tm, n_lat):
    g0 = mod[0, k3:k3 + 1, :]
    if mod.shape[0] == 1:
        return g0
    rid = i * tm + lax.broadcasted_iota(jnp.int32, (tm, 1), 0)
    return jnp.where(rid < n_lat, g0, mod[1, k3:k3 + 1, :])


def _ffn_up(name, u, wgu, base, tm):
    r, d = u.shape
    nch, _, _, fc = wgu.shape

    def body(u_ref, wg_ref, wu_ref, a_ref, b_ref, h_ref):
        uv = u_ref[...]
        a = _dot(uv, wg_ref[...], NN)
        b = _dot(uv, wu_ref[...], NN)
        a_ref[...] = a.astype(BF)
        b_ref[...] = b.astype(BF)
        h_ref[...] = (_silu(a) * b).astype(BF)

    chunk = pl.BlockSpec((None, tm, fc), lambda j, i: (j, i, 0))
    return _pcall(body, name=name, grid=(nch, r // tm),
                  in_specs=[pl.BlockSpec((tm, d), lambda j, i: (i, 0)),
                            pl.BlockSpec((None, None, d, fc), lambda j, i: (j, base, 0, 0)),
                            pl.BlockSpec((None, None, d, fc), lambda j, i: (j, base + 1, 0, 0))],
                  out_specs=[chunk] * 3, out_shape=[jax.ShapeDtypeStruct((nch, r, fc), BF)] * 3,
                  compiler_params=_params(("arbitrary", "arbitrary")))(u, wgu, wgu)


def _ffn_down(name, hid, wd, wd_blk, s, mod, k, n_lat, tm):
    nch, r, fc = hid.shape
    d = wd.shape[2]

    def body(h_ref, w_ref, s_ref, m_ref, so_ref, o_ref, acc_ref):
        i, j = pl.program_id(0), pl.program_id(1)
        part = _dot(h_ref[...], w_ref[...], NN)

        @pl.when(j == 0)
        def _():
            acc_ref[...] = part

        @pl.when(j > 0)
        def _():
            acc_ref[...] += part

        @pl.when(j == nch - 1)
        def _():
            o = acc_ref[...]
            o_ref[...] = o
            so_ref[...] = s_ref[...] + 0.5 * _row_gate(m_ref[...], 3 * k + 2, i, tm, n_lat) * o

    row = pl.BlockSpec((tm, d), lambda i, j: (i, 0))
    return _pcall(body, name=name, grid=(r // tm, nch),
                  in_specs=[pl.BlockSpec((None, tm, fc), lambda i, j: (j, i, 0)),
                            pl.BlockSpec((None, fc, d), lambda i, j: (j, wd_blk, 0)), row,
                            pl.BlockSpec(mod.shape, lambda i, j: (0, 0, 0))],
                  out_specs=[row, row], out_shape=[jax.ShapeDtypeStruct((r, d), F32)] * 2,
                  scratch_shapes=[pltpu.VMEM((tm, d), F32)],
                  compiler_params=_params(("arbitrary", "arbitrary")))(hid, wd, s, mod)


def _ffn_dhid(name, d_o, wd, wd_blk, a, b, tm):
    r, d = d_o.shape
    nch, _, fc = a.shape

    def body(g_ref, w_ref, a_ref, b_ref, da_ref, db_ref):
        dh = _dot(g_ref[...], w_ref[...], NT)
        av, bv = a_ref[...].astype(F32), b_ref[...].astype(F32)
        da_ref[...] = (dh * bv * _dsilu(av)).astype(BF)
        db_ref[...] = (dh * _silu(av)).astype(BF)

    chunk = pl.BlockSpec((None, tm, fc), lambda j, i: (j, i, 0))
    return _pcall(body, name=name, grid=(nch, r // tm),
                  in_specs=[pl.BlockSpec((tm, d), lambda j, i: (i, 0)),
                            pl.BlockSpec((None, fc, d), lambda j, i: (j, wd_blk, 0)), chunk, chunk],
                  out_specs=[chunk] * 2, out_shape=[jax.ShapeDtypeStruct((nch, r, fc), BF)] * 2,
                  compiler_params=_params(("arbitrary", "arbitrary")))(d_o, wd, a, b)


def _ffn_du(name, da, db, wgu, base, tm):
    nch, r, fc = da.shape
    d = wgu.shape[2]

    def body(da_ref, db_ref, wg_ref, wu_ref, o_ref, acc_ref):
        j = pl.program_id(1)
        part = _dot(da_ref[...], wg_ref[...], NT) + _dot(db_ref[...], wu_ref[...], NT)

        @pl.when(j == 0)
        def _():
            acc_ref[...] = part

        @pl.when(j > 0)
        def _():
            acc_ref[...] += part

        @pl.when(j == nch - 1)
        def _():
            o_ref[...] = acc_ref[...]

    chunk = pl.BlockSpec((None, tm, fc), lambda i, j: (j, i, 0))
    return _pcall(body, name=name, grid=(r // tm, nch),
                  in_specs=[chunk, chunk, pl.BlockSpec((None, None, d, fc), lambda i, j: (j, base, 0, 0)),
                            pl.BlockSpec((None, None, d, fc), lambda i, j: (j, base + 1, 0, 0))],
                  out_specs=pl.BlockSpec((tm, d), lambda i, j: (i, 0)),
                  out_shape=jax.ShapeDtypeStruct((r, d), F32), scratch_shapes=[pltpu.VMEM((tm, d), F32)],
                  compiler_params=_params(("arbitrary", "arbitrary")))(da, db, wgu, wgu)


def _ffn_dw_in(name, u, dz, tmm):
    r, d = u.shape
    nch, _, fc = dz.shape

    def body(u_ref, z_ref, o_ref):
        o_ref[...] = _dot(u_ref[...], z_ref[...], TN)

    return _pcall(body, name=name, grid=(nch, d // tmm),
                  in_specs=[pl.BlockSpec((r, tmm), lambda j, mi: (0, mi)),
                            pl.BlockSpec((None, r, fc), lambda j, mi: (j, 0, 0))],
                  out_specs=pl.BlockSpec((None, tmm, fc), lambda j, mi: (j, mi, 0)),
                  out_shape=jax.ShapeDtypeStruct((nch, d, fc), F32),
                  compiler_params=_params(("arbitrary", "arbitrary")))(u, dz)


def _ffn_dw_down(name, hid, d_o, tn):
    nch, r, fc = hid.shape
    d = d_o.shape[1]

    def body(h_ref, g_ref, o_ref):
        o_ref[...] = _dot(h_ref[...], g_ref[...], TN)

    return _pcall(body, name=name, grid=(nch, d // tn),
                  in_specs=[pl.BlockSpec((None, r, fc), lambda j, ni: (j, 0, 0)),
                            pl.BlockSpec((r, tn), lambda j, ni: (0, ni))],
                  out_specs=pl.BlockSpec((None, fc, tn), lambda j, ni: (j, 0, ni)),
                  out_shape=jax.ShapeDtypeStruct((nch, fc, d), F32),
                  compiler_params=_params(("arbitrary", "arbitrary")))(hid, d_o)


def _partner(x):
    n = x.shape[1]
    lane = lax.broadcasted_iota(jnp.int32, x.shape, 1)
    return jnp.where((lane & 15) < 8, pltpu.roll(x, n - 8, 1), pltpu.roll(x, 8, 1))


def _rope(x, ct, st):
    reps = x.shape[1] // ct.shape[1]
    if reps > 1:
        ct, st = jnp.tile(ct, (1, reps)), jnp.tile(st, (1, reps))
    return x * ct + _partner(x) * st


def _rope_t(dy, ct, st):
    reps = dy.shape[1] // ct.shape[1]
    if reps > 1:
        ct, st = jnp.tile(ct, (1, reps)), jnp.tile(st, (1, reps))
    return dy * ct + _partner(dy * st)


def _rope_tables(t_len, g_len, lane0):
    half = QK_ROPE // 4
    pos = jnp.arange(t_len)
    row = (pos // GRID_W).astype(F32)
    col = (pos % GRID_W).astype(F32)
    freqs = jnp.power(ROPE_THETA, -jnp.arange(0, QK_ROPE // 2, 2, dtype=F32) / (QK_ROPE // 2))
    ang_r, ang_c = row[:, None] * freqs, col[:, None] * freqs
    cs = jnp.concatenate([jnp.cos(ang_r)] * 2 + [jnp.cos(ang_c)] * 2, axis=1)
    sn = jnp.concatenate([-jnp.sin(ang_r), jnp.sin(ang_r), -jnp.sin(ang_c), jnp.sin(ang_c)], axis=1)
    assert cs.shape[1] == 4 * half == QK_ROPE
    ct = jnp.ones((t_len + g_len, HEAD_PAD), F32).at[:t_len, lane0:lane0 + QK_ROPE].set(cs)
    st = jnp.zeros((t_len + g_len, HEAD_PAD), F32).at[:t_len, lane0:lane0 + QK_ROPE].set(sn)
    return ct, st


def _attn_fwd(name, q, kp, vp, n_q, q_off, n_k, k_blk, heads, tq, scale):
    qb = q_off // tq

    def body(q_ref, k_ref, v_ref, o_ref, l_ref):
        s = _dot(q_ref[...], k_ref[...], NT) * scale
        m = jnp.max(s, axis=1, keepdims=True)
        p = jnp.exp(s - m)
        l = jnp.sum(p, axis=1, keepdims=True)
        o_ref[...] = (_dot(p, v_ref[...], NN) / l).astype(BF)
        l_ref[...] = jnp.broadcast_to(m + jnp.log(l), l_ref.shape)

    hw = heads * HEAD_PAD
    blk = pl.BlockSpec((tq, HEAD_PAD), lambda h, i: (i, h))
    kv = pl.BlockSpec((n_k, HEAD_PAD), lambda h, i: (k_blk, h))
    return _pcall(body, name=name, grid=(heads, n_q // tq),
                  in_specs=[pl.BlockSpec((tq, HEAD_PAD), lambda h, i: (i + qb, h)), kv, kv],
                  out_specs=[blk, blk],
                  out_shape=[jax.ShapeDtypeStruct((n_q, hw), BF), jax.ShapeDtypeStruct((n_q, hw), F32)],
                  compiler_params=_params(("arbitrary", "arbitrary")))(q, kp, vp)


def _attn_bwd(name, q, kp, vp, cat, dcat, lse, n_q, q_off, n_k, k_blk, heads, tq, scale, col_blk):
    qb = q_off // tq

    def body(q_ref, k_ref, v_ref, o_ref, do_ref, l_ref, dq_ref, dk_ref, dv_ref):
        i = pl.program_id(1)
        qv, kv_, vv = q_ref[...], k_ref[...], v_ref[...]
        dov = do_ref[...]
        s = _dot(qv, kv_, NT) * scale
        p = jnp.exp(s - l_ref[...][:, 0:1])
        dp = _dot(dov, vv, NT)
        delta = jnp.sum(dov * o_ref[...].astype(F32), axis=1, keepdims=True)
        ds = (p * (dp - delta) * scale).astype(BF)
        dq_ref[...] = _dot(ds, kv_, NN)
        dk = _dot(ds, qv, TN)
        dv = _dot(p, dov, TN)

        @pl.when(i == 0)
        def _():
            dk_ref[...] = dk
            dv_ref[...] = dv

        @pl.when(i > 0)
        def _():
            dk_ref[...] += dk
            dv_ref[...] += dv

    hw = heads * HEAD_PAD
    qspec = pl.BlockSpec((tq, HEAD_PAD), lambda h, i: (i + qb, h))
    cspec = pl.BlockSpec((tq, HEAD_PAD), lambda h, i: (i + qb, col_blk + h))
    kv = pl.BlockSpec((n_k, HEAD_PAD), lambda h, i: (k_blk, h))
    acc = pl.BlockSpec((n_k, HEAD_PAD), lambda h, i: (0, h))
    blk = pl.BlockSpec((tq, HEAD_PAD), lambda h, i: (i, h))
    return _pcall(body, name=name, grid=(heads, n_q // tq),
                  in_specs=[qspec, kv, kv, cspec, cspec, blk], out_specs=[blk, acc, acc],
                  out_shape=[jax.ShapeDtypeStruct((n_q, hw), F32), jax.ShapeDtypeStruct((n_k, hw), F32),
                             jax.ShapeDtypeStruct((n_k, hw), F32)],
                  compiler_params=_params(("arbitrary", "arbitrary")))(q, kp, vp, cat, dcat, lse)


def _shift(x, k):
    return pltpu.roll(x, k % x.shape[0], 0)


def _window_sum(v, w, mirrored):
    n, gd = v.shape
    pad = jnp.zeros((POOL_PAD, gd), F32)
    e = jnp.concatenate([pad, v, pad], axis=0)
    acc = e + _shift(e, -1 if mirrored else 1)
    step = 1
    while 2 * step < w:
        acc = _shift(acc, step) + _shift(acc, -step)
        step *= 2
    return acc[POOL_PAD:POOL_PAD + n]


def _window_count(n, w):
    t = lax.broadcasted_iota(jnp.int32, (n, 1), 0)
    lo = jnp.maximum(t - w // 2, 0)
    hi = jnp.minimum(t + (w - w // 2 - 1), n - 1)
    return (hi - lo + 1).astype(F32)


def _pool_fwd(name, u, pool_w, scale):
    n, pd = u.shape
    ng = len(POOL_WINDOWS)
    gd = pd // ng

    def body(u_ref, w_ref, s_ref, y_ref):
        for g, w in enumerate(POOL_WINDOWS):
            sl = slice(g * gd, (g + 1) * gd)
            ug = u_ref[:, sl]
            p = _window_sum(ug, w, False) / _window_count(n, w) - ug
            y_ref[:, sl] = (_dot(p, w_ref[g], NN) * s_ref[:, sl]).astype(BF)

    return _pcall(body, name=name, out_shape=jax.ShapeDtypeStruct((n, pd), BF),
                  compiler_params=_params())(u, pool_w, scale)


def _pool_bwd(name, u, dcat, pool_w, scale, row_off):
    n, pd = u.shape
    ng = len(POOL_WINDOWS)
    gd = pd // ng

    def body(u_ref, dy_ref, w_ref, s_ref, du_ref, dw_ref, ds_ref):
        ds_ref[...] = jnp.zeros_like(ds_ref)
        for g, w in enumerate(POOL_WINDOWS):
            sl = slice(g * gd, (g + 1) * gd)
            ug, dy, wg = u_ref[:, sl], dy_ref[:, sl], w_ref[g]
            cnt = _window_count(n, w)
            p = _window_sum(ug, w, False) / cnt - ug
            ds_ref[0:1, sl] = jnp.sum(dy * _dot(p, wg, NN), axis=0, keepdims=True)
            dys = dy * s_ref[:, sl]
            dw_ref[g] = _dot(p, dys, TN)
            dp = _dot(dys, wg, NT)
            du_ref[:, sl] = (_window_sum(dp / cnt, w, True) - dp).astype(BF)

    rb = row_off // n
    return _pcall(body, name=name, grid=(1,),
                  in_specs=[pl.BlockSpec((n, pd), lambda i: (0, 0)), pl.BlockSpec((n, pd), lambda i: (rb, 0)),
                            pl.BlockSpec(pool_w.shape, lambda i: (0, 0, 0)), pl.BlockSpec(scale.shape, lambda i: (0, 0))],
                  out_specs=[pl.BlockSpec((n, pd), lambda i: (0, 0)), pl.BlockSpec((ng, gd, gd), lambda i: (0, 0, 0)),
                             pl.BlockSpec((8, pd), lambda i: (0, 0))],
                  out_shape=[jax.ShapeDtypeStruct((n, pd), BF), jax.ShapeDtypeStruct((ng, gd, gd), F32),
                             jax.ShapeDtypeStruct((8, pd), F32)],
                  compiler_params=_params(("arbitrary",)))(u, dcat, pool_w, scale)


def _edge_shift(z, k):
    n = z.shape[0]
    t = lax.broadcasted_iota(jnp.int32, (n, 1), 0)
    keep = (t >= k) if k > 0 else (t < n + k)
    return jnp.where(keep, pltpu.roll(z, k % n, 0), 0.0)


def _conv_fwd(name, p3, cw, tc):
    n, cd = p3.shape[0], p3.shape[1] // 3
    nb = cd // tc

    def body(b_ref, c_ref, v_ref, w_ref, y_ref):
        z = c_ref[...] * v_ref[...]
        w = w_ref[...]
        zc = w[0:1] * _edge_shift(z, 1) + w[1:2] * z + w[2:3] * _edge_shift(z, -1)
        y_ref[...] = (b_ref[...] * zc).astype(BF)

    return _pcall(body, name=name, grid=(nb,),
                  in_specs=[pl.BlockSpec((n, tc), lambda j: (0, j)), pl.BlockSpec((n, tc), lambda j: (0, nb + j)),
                            pl.BlockSpec((n, tc), lambda j: (0, 2 * nb + j)), pl.BlockSpec((3, tc), lambda j: (0, j))],
                  out_specs=pl.BlockSpec((n, tc), lambda j: (0, j)), out_shape=jax.ShapeDtypeStruct((n, cd), BF),
                  compiler_params=_params(("arbitrary",)))(p3, p3, p3, cw)


def _conv_bwd(name, p3, cw, dy, tc):
    n, cd = dy.shape
    nb = cd // tc

    def body(b_ref, c_ref, v_ref, w_ref, dy_ref, dp_ref, dw_ref):
        cv, vv, w, dyv = c_ref[...], v_ref[...], w_ref[...], dy_ref[...]
        z = cv * vv
        zl, zr = _edge_shift(z, 1), _edge_shift(z, -1)
        zc = w[0:1] * zl + w[1:2] * z + w[2:3] * zr
        dzc = dyv * b_ref[...]
        dz = w[0:1] * _edge_shift(dzc, -1) + w[1:2] * dzc + w[2:3] * _edge_shift(dzc, 1)
        dp_ref[0] = (dyv * zc).astype(BF)
        dp_ref[1] = (dz * vv).astype(BF)
        dp_ref[2] = (dz * cv).astype(BF)
        dw_ref[...] = jnp.zeros_like(dw_ref)
        dw_ref[0:1, :] = jnp.sum(dzc * zl, axis=0, keepdims=True)
        dw_ref[1:2, :] = jnp.sum(dzc * z, axis=0, keepdims=True)
        dw_ref[2:3, :] = jnp.sum(dzc * zr, axis=0, keepdims=True)

    col = pl.BlockSpec((n, tc), lambda j: (0, j))
    return _pcall(body, name=name, grid=(nb,),
                  in_specs=[col, pl.BlockSpec((n, tc), lambda j: (0, nb + j)),
                            pl.BlockSpec((n, tc), lambda j: (0, 2 * nb + j)), pl.BlockSpec((3, tc), lambda j: (0, j)), col],
                  out_specs=[pl.BlockSpec((3, n, tc), lambda j: (0, 0, j)), pl.BlockSpec((8, tc), lambda j: (0, j))],
                  out_shape=[jax.ShapeDtypeStruct((3, n, cd), BF), jax.ShapeDtypeStruct((8, cd), F32)],
                  compiler_params=_params(("arbitrary",)))(p3, p3, p3, cw, dy)


def _conv_din(name, dp3, w_in, tm):
    _, n, cd = dp3.shape
    d = w_in.shape[0]

    def body(a_ref, w_ref, o_ref, acc_ref):
        j = pl.program_id(1)
        part = _dot(a_ref[...], w_ref[...], NT)

        @pl.when(j == 0)
        def _():
            acc_ref[...] = part

        @pl.when(j > 0)
        def _():
            acc_ref[...] += part

        @pl.when(j == 2)
        def _():
            o_ref[...] = acc_ref[...]

    return _pcall(body, name=name, grid=(n // tm, 3),
                  in_specs=[pl.BlockSpec((None, tm, cd), lambda i, j: (j, i, 0)),
                            pl.BlockSpec((d, cd), lambda i, j: (0, j))],
                  out_specs=pl.BlockSpec((tm, d), lambda i, j: (i, 0)), out_shape=jax.ShapeDtypeStruct((n, d), F32),
                  scratch_shapes=[pltpu.VMEM((tm, d), F32)],
                  compiler_params=_params(("arbitrary", "arbitrary")))(dp3, w_in)


def _conv_dw_in(name, u, dp3, tmm, tn):
    n, d = u.shape
    cd = dp3.shape[2]
    nb = cd // tn

    def body(u_ref, z_ref, o_ref):
        o_ref[...] = _dot(u_ref[...], z_ref[...], TN)

    return _pcall(body, name=name, grid=(3 * nb, d // tmm),
                  in_specs=[pl.BlockSpec((n, tmm), lambda j, mi: (0, mi)),
                            pl.BlockSpec((None, n, tn), lambda j, mi: (j // nb, 0, j % nb))],
                  out_specs=pl.BlockSpec((tmm, tn), lambda j, mi: (mi, j)),
                  out_shape=jax.ShapeDtypeStruct((d, 3 * cd), F32),
                  compiler_params=_params(("arbitrary", "arbitrary")))(u, dp3)


def _loss_head(name, h, target, gain, tm):
    d = h.shape[1]

    def fn(rows, consts, m):
        hv, tv = rows
        g = consts[0][0:1, :]
        n, r = _rms(hv)
        err = n * g - tv
        dy = err / d
        loss = 0.5 * jnp.sum(err * err) / d
        acc = {0: jnp.sum(dy * n, axis=0, keepdims=True), 1: jnp.full((1, d), loss, F32)}
        return [_rms_bwd(dy * g, n, r)], acc

    return _rows_call(name, fn, h.shape[0], tm, [h, target], [gain], None, [(d, F32)], acc_w=d)


def _adamw(name, w, g, m, v):
    shape = w.shape
    cdim = shape[-1]
    r = max(1, math.prod(shape[:-1]))
    tr = r
    if r * cdim * 4 > (3 << 19):
        tr = _pick(r, max(8, (3 << 19) // (cdim * 4)), 8)
    c1 = 1.0 / (1.0 - ADAM_B1 ** ADAM_STEP)
    c2 = 1.0 / (1.0 - ADAM_B2 ** ADAM_STEP)

    def body(w_ref, g_ref, m_ref, v_ref, d_ref, nm_ref, nv_ref):
        gv = g_ref[...]
        nm = ADAM_B1 * m_ref[...] + (1.0 - ADAM_B1) * gv
        nv = ADAM_B2 * v_ref[...] + (1.0 - ADAM_B2) * (gv * gv)
        nm_ref[...] = nm
        nv_ref[...] = nv
        d_ref[...] = -ADAM_LR * ((nm * c1) / (jnp.sqrt(nv * c2) + ADAM_EPS) + ADAM_WD * w_ref[...])

    spec = pl.BlockSpec((tr, cdim), lambda i: (i, 0))
    res = _pcall(body, name=name, grid=(r // tr,), in_specs=[spec] * 4, out_specs=[spec] * 3,
                 out_shape=[jax.ShapeDtypeStruct((r, cdim), F32)] * 3,
                 compiler_params=_params(("arbitrary",)))(*[t.reshape(r, cdim) for t in (w, g, m, v)])
    return [t.reshape(shape) for t in res]


def _ffn_half_fwd(tag, s, gains, mod, k, wts, n_lat, tm, h_tiles, tm_big):
    wgu, base, wd, wd_blk = wts
    u = _adaln_fwd(f"adaln_{tag}", s, gains, k, mod, k, tm, h_tiles)
    a, b, hid = _ffn_up(f"ffn_up_{tag}", u, wgu, base, tm)
    s_out, o = _ffn_down(f"ffn_down_{tag}", hid, wd, wd_blk, s, mod, k, n_lat, tm_big)
    return s_out, (s, u, a, b, hid, o)


def _ffn_half_bwd(tag, ds_out, saved, gains, mod, k, wts, tm, h_tiles, tm_big):
    wgu, base, wd, wd_blk = wts
    s, u, a, b, hid, o = saved
    d_o, acc_g = _resid_bwd(f"resid_bwd_{tag}", ds_out, o, mod, k, 0.5, tm, h_tiles)
    da, db = _ffn_dhid(f"ffn_dhid_{tag}", d_o, wd, wd_blk, a, b, tm)
    du = _ffn_du(f"ffn_du_{tag}", da, db, wgu, base, tm_big)
    d = u.shape[1]
    dwg = _ffn_dw_in(f"ffn_dwg_{tag}", u, da, _pick(d, 256))
    dwu = _ffn_dw_in(f"ffn_dwu_{tag}", u, db, _pick(d, 256))
    dwd = _ffn_dw_down(f"ffn_dwd_{tag}", hid, d_o, _pick(d, 512))
    ds, acc_n = _adaln_bwd(f"adaln_bwd_{tag}", s, du, ds_out, gains, k, mod, k, tm, h_tiles)
    return ds, dwg, dwu, dwd, (acc_n[:, 0], acc_n[:, 1], acc_g[:, 0]), jnp.sum(acc_n[:, 2], axis=0)


def kernel(x, c, ctx, c_ctx, norm_g, w_mod, b_mod, ffn_w_gate, ffn_w_up, ffn_w_down, ab_w_in, pool_w, pool_scale, q_norm_g, w_uq, kv_norm_g, w_ukv, ab_w_out, conv_w_in, conv_w, conv_w_out, final_norm_g, loss_target, m_c_ctx, m_norm_g, m_w_mod, m_b_mod, m_ffn_w_gate, m_ffn_w_up, m_ffn_w_down, m_ab_w_in, m_pool_w, m_pool_scale, m_q_norm_g, m_w_uq, m_kv_norm_g, m_w_ukv, m_ab_w_out, m_conv_w_in, m_conv_w, m_conv_w_out, m_final_norm_g, v_c_ctx, v_norm_g, v_w_mod, v_b_mod, v_ffn_w_gate, v_ffn_w_up, v_ffn_w_down, v_ab_w_in, v_pool_w, v_pool_scale, v_q_norm_g, v_w_uq, v_kv_norm_g, v_w_ukv, v_ab_w_out, v_conv_w_in, v_conv_w, v_conv_w_out, v_final_norm_g):
    weights = dict(c_ctx=c_ctx, norm_g=norm_g, w_mod=w_mod, b_mod=b_mod, ffn_w_gate=ffn_w_gate, ffn_w_up=ffn_w_up,
                   ffn_w_down=ffn_w_down, ab_w_in=ab_w_in, pool_w=pool_w, pool_scale=pool_scale, q_norm_g=q_norm_g,
                   w_uq=w_uq, kv_norm_g=kv_norm_g, w_ukv=w_ukv, ab_w_out=ab_w_out, conv_w_in=conv_w_in, conv_w=conv_w,
                   conv_w_out=conv_w_out, final_norm_g=final_norm_g)
    mom_m = dict(c_ctx=m_c_ctx, norm_g=m_norm_g, w_mod=m_w_mod, b_mod=m_b_mod, ffn_w_gate=m_ffn_w_gate,
                 ffn_w_up=m_ffn_w_up, ffn_w_down=m_ffn_w_down, ab_w_in=m_ab_w_in, pool_w=m_pool_w,
                 pool_scale=m_pool_scale, q_norm_g=m_q_norm_g, w_uq=m_w_uq, kv_norm_g=m_kv_norm_g, w_ukv=m_w_ukv,
                 ab_w_out=m_ab_w_out, conv_w_in=m_conv_w_in, conv_w=m_conv_w, conv_w_out=m_conv_w_out,
                 final_norm_g=m_final_norm_g)
    mom_v = dict(c_ctx=v_c_ctx, norm_g=v_norm_g, w_mod=v_w_mod, b_mod=v_b_mod, ffn_w_gate=v_ffn_w_gate,
                 ffn_w_up=v_ffn_w_up, ffn_w_down=v_ffn_w_down, ab_w_in=v_ab_w_in, pool_w=v_pool_w,
                 pool_scale=v_pool_scale, q_norm_g=v_q_norm_g, w_uq=v_w_uq, kv_norm_g=v_kv_norm_g, w_ukv=v_w_ukv,
                 ab_w_out=v_ab_w_out, conv_w_in=v_conv_w_in, conv_w=v_conv_w, conv_w_out=v_conv_w_out,
                 final_norm_g=v_final_norm_g)

    t_len, d = x.shape[1], x.shape[2]
    g_len = ctx.shape[1]
    r_len = t_len + g_len
    fc = ffn_w_gate.shape[3]
    heads = d // 128
    pool_dim = d // 2
    q_rank, kv_rank = q_norm_g.shape[1], kv_norm_g.shape[1]
    hw = heads * HEAD_PAD
    attn_scale = 1.0 / math.sqrt(QK_NOPE + QK_ROPE)
    kvr_w = kv_rank + HEAD_PAD
    in_w = pool_dim + q_rank + kvr_w
    tm = 256 if g_len % 256 == 0 else g_len
    assert t_len % tm == 0 and g_len % tm == 0 and t_len % g_len == 0 and pool_dim % 128 == 0
    h_tiles = t_len // tm
    tm_l0 = _pick(r_len, 768, tm)
    tm_l1 = _pick(t_len, 1024, tm)

    xi, yi, ci = lax.axis_index("x"), lax.axis_index("y"), lax.axis_index("c")
    me = 4 * xi + 2 * yi + ci
    shard = 2 * xi + yi

    def halves(a2d):
        return a2d.astype(BF).reshape(2, a2d.shape[0] // 2, a2d.shape[1])

    packs = [
        jnp.stack([ffn_w_gate, ffn_w_up], axis=2).reshape(8 * d, fc),
        jnp.concatenate([ffn_w_down.reshape(4 * fc, d), ab_w_out[0], conv_w_out[0]], axis=0),
        w_uq[0], w_ukv[0], ab_w_in[0], conv_w_in[0],
    ]
    gathered = _exchange("gather_weights", [halves(p) for p in packs], 'half')
    gathered = [g.reshape(N_SHARD, p.shape[0], p.shape[1]) for g, p in zip(gathered, packs)]
    wgu_all = gathered[0].reshape(N_SHARD, 8, d, fc)
    ffn_w = [[(wgu_all, 4 * l + 2 * f, gathered[1], 2 * l + f) for f in range(2)] for l in range(2)]
    n_out_rows = ab_w_out.shape[1]
    w_out_full = gathered[1][:, 4 * fc:4 * fc + n_out_rows].reshape(N_SHARD * n_out_rows, d)
    cw_out_full = gathered[1][:, 4 * fc + n_out_rows:].reshape(-1, d)
    w_uq_full = gathered[2].reshape(q_rank, heads * (QK_NOPE + QK_ROPE))
    w_ukv_full = gathered[3].transpose(1, 0, 2).reshape(kv_rank, heads * (QK_NOPE + V_HEAD))
    w_in_full = gathered[4].transpose(1, 0, 2).reshape(d, -1)
    cw_in_full = gathered[5].transpose(1, 0, 2).reshape(d, -1)

    wq_p = jnp.pad(w_uq_full.reshape(q_rank, heads, QK_NOPE + QK_ROPE),
                   ((0, 0), (0, 0), (0, HEAD_PAD - QK_NOPE - QK_ROPE))).reshape(q_rank, hw)
    ukv3 = w_ukv_full.reshape(kv_rank, heads, QK_NOPE + V_HEAD)
    wk_top = jnp.pad(ukv3[..., :QK_NOPE], ((0, 0), (0, 0), (0, HEAD_PAD - QK_NOPE))).reshape(kv_rank, hw)
    wv_top = jnp.pad(ukv3[..., QK_NOPE:], ((0, 0), (0, 0), (0, HEAD_PAD - V_HEAD))).reshape(kv_rank, hw)
    spread = jnp.zeros((HEAD_PAD, heads, HEAD_PAD), BF).at[
        jnp.arange(QK_ROPE)[:, None], jnp.arange(heads)[None, :], QK_NOPE + jnp.arange(QK_ROPE)[:, None]].set(1.0)
    wk_ext = jnp.concatenate([wk_top, spread.reshape(HEAD_PAD, hw)], axis=0)
    wv_ext = jnp.concatenate([wv_top, jnp.zeros((HEAD_PAD, hw), BF)], axis=0)
    w_in_pool = w_in_full[:, :pool_dim]
    w_in_q = w_in_full[:, pool_dim:pool_dim + q_rank]
    w_in_kvr = jnp.pad(w_in_full[:, pool_dim + q_rank:], ((0, 0), (0, HEAD_PAD - QK_ROPE)))
    w_out_attn = jnp.pad(w_out_full[pool_dim:].reshape(heads, V_HEAD, d),
                         ((0, 0), (0, HEAD_PAD - V_HEAD), (0, 0))).reshape(hw, d)
    w_out_p = jnp.concatenate([w_out_full[:pool_dim], w_out_attn], axis=0)

    small = jnp.concatenate([norm_g.reshape(6, -1), conv_w[0]], axis=0)
    small = jnp.pad(small, ((0, 7), (0, 0)))
    c_row = jnp.pad(c, ((0, 7), (0, 0)))
    small_all, c_all = _exchange("gather_small", [small, c_row], 'whole')
    small_full = small_all[::2].transpose(1, 0, 2).reshape(16, d)
    gains = [jnp.pad(small_full[3 * l:3 * l + 3], ((0, 5), (0, 0))) for l in range(2)]
    conv_w_full = small_full[6:9]
    c16 = jnp.concatenate([c_all[:, 0], c_ctx[None], jnp.zeros((7, d), F32)], axis=0)

    n_col = w_mod.shape[2]
    b_sh = lax.dynamic_slice_in_dim(b_mod, shard * n_col, n_col, axis=1)
    m_sh = [_mm(f"mod_fwd_{l}", c16, w_mod[l], 'nn', tm=16, tn=768, a_pre=_silu,
                epi=lambda acc, i, bv: (acc + bv,), epi_args=(b_sh[l:l + 1],), epi_kinds=('n',)) for l in range(2)]
    m_all = _exchange("gather_mod", [jnp.concatenate(m_sh, axis=0)], 'whole')[0]
    m_full = m_all[::2].reshape(N_SHARD, 2, 16, n_col).transpose(1, 2, 0, 3).reshape(2, 16, N_MOD * d)
    mod_h = [jnp.pad(lax.dynamic_index_in_dim(m_full[l], me, 0, keepdims=False).reshape(N_MOD, d), ((0, 7), (0, 0)))
             for l in range(2)]
    mod_g0 = jnp.pad(m_full[0, 8].reshape(N_MOD, d), ((0, 7), (0, 0)))
    mods = [jnp.stack([mod_h[0], mod_g0]), mod_h[1][None]]

    s0 = jnp.concatenate([x[0], ctx[0]], axis=0)
    s1, sav_f00 = _ffn_half_fwd("l0a", s0, gains[0], mods[0], 0, ffn_w[0][0], t_len, tm, h_tiles, tm_l0)
    u_mix = _adaln_fwd("adaln_l0m", s1, gains[0], 1, mods[0], 1, tm, h_tiles)
    p_pool = _mm("in_pool", u_mix, w_in_pool, 'nn', tm=tm, tn=pool_dim)
    p_q = _mm("in_q", u_mix, w_in_q, 'nn', tm=tm, tn=q_rank)
    p_kvr = _mm("in_kvr", u_mix, w_in_kvr, 'nn', tm=tm, tn=kvr_w)
    qg = jnp.pad(q_norm_g, ((0, 7), (0, 0)))
    kvg = jnp.pad(kv_norm_g, ((0, 7), (0, 0)))
    tq_c, tq_s = _rope_tables(t_len, g_len, QK_NOPE)
    tk_c, tk_s = _rope_tables(t_len, g_len, 0)

    def qn_fn(rows, consts, m):
        n, _ = _rms(rows[0])
        return [n * consts[0][0:1, :]], {}

    qn = _rows_call("q_norm", qn_fn, r_len, tm, [p_q], [qg], None, [(q_rank, BF)])[0]
    q_r = _mm("q_up", qn, wq_p, 'nn', tm=tm, tn=hw, out_dtypes=(BF,),
              epi=lambda acc, i, ct, st: (_rope(acc, ct, st),), epi_args=(tq_c, tq_s), epi_kinds=('mt', 'mt'))

    def kvn_fn(rows, consts, m):
        pv, ct, st = rows
        n, _ = _rms(pv[:, :kv_rank])
        return [jnp.concatenate([n * consts[0][0:1, :], _rope(pv[:, kv_rank:], ct, st)], axis=1)], {}

    kvn = _rows_call("kv_norm", kvn_fn, r_len, tm, [p_kvr, tk_c, tk_s], [kvg], None, [(kvr_w, BF)])[0]
    k_p = _mm("k_up", kvn, wk_ext, 'nn', tm=tm, tn=hw, out_dtypes=(BF,))
    v_p = _mm("v_up", kvn, wv_ext, 'nn', tm=tm, tn=hw, out_dtypes=(BF,))
    o_h, lse_h = _attn_fwd("attn_h", q_r, k_p, v_p, t_len, 0, r_len, 0, heads, tm, attn_scale)
    o_g, lse_g = _attn_fwd("attn_g", q_r, k_p, v_p, g_len, t_len, g_len, t_len // g_len, heads, tm, attn_scale)
    y_h = _pool_fwd("pool_h", p_pool[:t_len], pool_w[0], pool_scale)
    y_g = _pool_fwd("pool_g", p_pool[t_len:], pool_w[0], pool_scale)
    cat = jnp.concatenate([jnp.concatenate([y_h, y_g], axis=0), jnp.concatenate([o_h, o_g], axis=0)], axis=1)

    def resid_epi(k3, n_lat, tmr):
        def epi(acc, i, sv, mv):
            return sv + _row_gate(mv, k3, i, tmr, n_lat) * acc, acc
        return epi

    s2, o_mix0 = _mm("mix_out_l0", cat, w_out_p, 'nn', tm=tm, tn=d, out_dtypes=(F32, F32),
                     epi=resid_epi(5, t_len, tm), epi_args=(s1, mods[0]), epi_kinds=('mn', 'w'))
    s3, sav_f01 = _ffn_half_fwd("l0b", s2, gains[0], mods[0], 2, ffn_w[0][1], t_len, tm, h_tiles, tm_l0)

    tml = 256 if t_len % 256 == 0 else tm
    h3 = s3[:t_len]
    h4, sav_f10 = _ffn_half_fwd("l1a", h3, gains[1], mods[1], 0, ffn_w[1][0], t_len, tml, None, tm_l1)
    u_cv = _adaln_fwd("adaln_l1m", h4, gains[1], 1, mods[1], 1, tml, None)
    p3 = _mm("conv_in", u_cv, cw_in_full, 'nn', tm=tml, tn=512)
    cwp = conv_w_full
    tc = _pick(d, 256)
    y_cv = _conv_fwd("conv_fwd", p3, cwp, tc)
    h5, o_mix1 = _mm("mix_out_l1", y_cv, cw_out_full, 'nn', tm=tml, tn=d, out_dtypes=(F32, F32),
                     epi=resid_epi(5, t_len, tml), epi_args=(h4, mods[1]), epi_kinds=('mn', 'w'))
    h6, sav_f11 = _ffn_half_fwd("l1b", h5, gains[1], mods[1], 2, ffn_w[1][1], t_len, tml, None, tm_l1)

    fg = jnp.pad(final_norm_g[None], ((0, 7), (0, 0)))
    dh6, acc_loss = _loss_head("loss_head", h6, loss_target[0], fg, tml)
    loss = lax.psum(acc_loss[0, 1, 0], ("x", "y", "c"))
    d_final_g = acc_loss[0, 0]

    dgain = [[None] * 3 for _ in range(2)]
    dmod = [[None] * N_MOD for _ in range(2)]

    def put(l, k, triple):
        dmod[l][3 * k], dmod[l][3 * k + 1], dmod[l][3 * k + 2] = triple

    dh5, dwg11, dwu11, dwd11, tr, dgain[1][2] = _ffn_half_bwd("l1b", dh6, sav_f11, gains[1], mods[1], 2,
                                                              ffn_w[1][1], tml, None, tm_l1)
    put(1, 2, tr)
    d_o1, acc_g1 = _resid_bwd("resid_bwd_l1m", dh5, o_mix1, mods[1], 1, 1.0, tml, None)
    dy_cv = _mm("mix_out_l1_dx", d_o1, cw_out_full, 'nt', tm=tml, tn=d)
    d_cw_out = _mm("mix_out_l1_dw", y_cv, d_o1, 'tn', tm=256, tn=512)
    dp3, d_cw = _conv_bwd("conv_bwd", p3, cwp, dy_cv, tc)
    du_cv = _conv_din("conv_in_dx", dp3, cw_in_full, tml)
    d_cw_in = _conv_dw_in("conv_in_dw", u_cv, dp3, _pick(d, 256), _pick(d, 512))
    dh4, acc_n1 = _adaln_bwd("adaln_bwd_l1m", h4, du_cv, dh5, gains[1], 1, mods[1], 1, tml, None)
    put(1, 1, (acc_n1[:, 0], acc_n1[:, 1], acc_g1[:, 0]))
    dgain[1][1] = acc_n1[0, 2]
    dh3, dwg10, dwu10, dwd10, tr, dgain[1][0] = _ffn_half_bwd("l1a", dh4, sav_f10, gains[1], mods[1], 0,
                                                              ffn_w[1][0], tml, None, tm_l1)
    put(1, 0, tr)

    ds3 = jnp.concatenate([dh3, jnp.zeros((g_len, d), F32)], axis=0)
    ds2, dwg01, dwu01, dwd01, tr, dgain[0][2] = _ffn_half_bwd("l0b", ds3, sav_f01, gains[0], mods[0], 2,
                                                              ffn_w[0][1], tm, h_tiles, tm_l0)
    put(0, 2, tr)
    d_o0, acc_g0 = _resid_bwd("resid_bwd_l0m", ds2, o_mix0, mods[0], 1, 1.0, tm, h_tiles)
    dcat = _mm("mix_out_l0_dx", d_o0, w_out_p, 'nt', tm=tm, tn=pool_dim + hw)
    d_w_out_p = _mm("mix_out_l0_dw", cat, d_o0, 'tn', tm=256, tn=512)
    col_blk = pool_dim // HEAD_PAD
    dq_h, dk_h, dv_h = _attn_bwd("attn_bwd_h", q_r, k_p, v_p, cat, dcat, lse_h, t_len, 0, r_len, 0, heads, tm,
                                 attn_scale, col_blk)
    dq_g, dk_g, dv_g = _attn_bwd("attn_bwd_g", q_r, k_p, v_p, cat, dcat, lse_g, g_len, t_len, g_len,
                                 t_len // g_len, heads, tm, attn_scale, col_blk)
    dq_all = jnp.concatenate([dq_h, dq_g], axis=0)
    dk_all = dk_h.at[t_len:].add(dk_g)
    dv_all = dv_h.at[t_len:].add(dv_g)
    dkvn = _mm("k_up_dx", dk_all, wk_ext, 'nt', tm=tm, tn=kvr_w)
    dkvn = _mm("v_up_dx", dv_all, wv_ext, 'nt', tm=tm, tn=kvr_w, epi=lambda acc, i, prev: (acc + prev,),
               epi_args=(dkvn,), epi_kinds=('mn',))
    d_wk_ext = _mm("k_up_dw", kvn, dk_all, 'tn', tm=kvr_w, tn=512)
    d_wv_ext = _mm("v_up_dw", kvn, dv_all, 'tn', tm=kvr_w, tn=512)

    def kvn_bwd_fn(rows, consts, m):
        pv, dv_, ct, st = rows
        g = consts[0][0:1, :]
        n, r = _rms(pv[:, :kv_rank])
        dyn = dv_[:, :kv_rank]
        dckv = _rms_bwd(dyn * g, n, r)
        dkr = _rope_t(dv_[:, kv_rank:], ct, st)
        return [jnp.concatenate([dckv, dkr], axis=1)], {0: jnp.sum(dyn * n, axis=0, keepdims=True)}

    dp_kvr, acc_kvg = _rows_call("kv_norm_bwd", kvn_bwd_fn, r_len, tm, [p_kvr, dkvn, tk_c, tk_s], [kvg], None,
                                 [(kvr_w, BF)], acc_w=kv_rank)

    def qrope_bwd_fn(rows, consts, m):
        return [_rope_t(rows[0], rows[1], rows[2])], {}

    dq_pad = _rows_call("q_rope_bwd", qrope_bwd_fn, r_len, tm, [dq_all, tq_c, tq_s], [], None, [(hw, BF)])[0]
    dqn = _mm("q_up_dx", dq_pad, wq_p, 'nt', tm=tm, tn=q_rank)
    d_wq_p = _mm("q_up_dw", qn, dq_pad, 'tn', tm=256, tn=512)

    def qn_bwd_fn(rows, consts, m):
        pv, dv_ = rows
        g = consts[0][0:1, :]
        n, r = _rms(pv)
        return [_rms_bwd(dv_ * g, n, r)], {0: jnp.sum(dv_ * n, axis=0, keepdims=True)}

    dp_q, acc_qg = _rows_call("q_norm_bwd", qn_bwd_fn, r_len, tm, [p_q, dqn], [qg], None, [(q_rank, BF)],
                              acc_w=q_rank)
    dpu_h, dpw_h, dps_h = _pool_bwd("pool_bwd_h", p_pool[:t_len], dcat, pool_w[0], pool_scale, 0)
    dpu_g, dpw_g, dps_g = _pool_bwd("pool_bwd_g", p_pool[t_len:], dcat, pool_w[0], pool_scale, t_len)
    dp_pool = jnp.concatenate([dpu_h, dpu_g], axis=0)
    add_prev = lambda acc, i, prev: (acc + prev,)
    du_mix = _mm("in_pool_dx", dp_pool, w_in_pool, 'nt', tm=tm, tn=d)
    du_mix = _mm("in_q_dx", dp_q, w_in_q, 'nt', tm=tm, tn=d, epi=add_prev, epi_args=(du_mix,), epi_kinds=('mn',))
    du_mix = _mm("in_kvr_dx", dp_kvr, w_in_kvr, 'nt', tm=tm, tn=d, epi=add_prev, epi_args=(du_mix,), epi_kinds=('mn',))
    d_w_in = jnp.concatenate([
        _mm("in_pool_dw", u_mix, dp_pool, 'tn', tm=256, tn=pool_dim),
        _mm("in_q_dw", u_mix, dp_q, 'tn', tm=256, tn=q_rank),
        _mm("in_kvr_dw", u_mix, dp_kvr, 'tn', tm=256, tn=kvr_w)[:, :kv_rank + QK_ROPE]], axis=1)
    ds1, acc_n0 = _adaln_bwd("adaln_bwd_l0m", s1, du_mix, ds2, gains[0], 1, mods[0], 1, tm, h_tiles)
    put(0, 1, (acc_n0[:, 0], acc_n0[:, 1], acc_g0[:, 0]))
    dgain[0][1] = jnp.sum(acc_n0[:, 2], axis=0)
    ds0, dwg00, dwu00, dwd00, tr, dgain[0][0] = _ffn_half_bwd("l0a", ds1, sav_f00, gains[0], mods[0], 0,
                                                              ffn_w[0][0], tm, h_tiles, tm_l0)
    put(0, 0, tr)
    grad_x = ds0[:t_len][None]

    d_w_uq = d_wq_p.reshape(q_rank, heads, HEAD_PAD)[..., :QK_NOPE + QK_ROPE].reshape(q_rank, -1)
    d_w_ukv = jnp.concatenate([d_wk_ext[:kv_rank].reshape(kv_rank, heads, HEAD_PAD)[..., :QK_NOPE],
                               d_wv_ext[:kv_rank].reshape(kv_rank, heads, HEAD_PAD)[..., :V_HEAD]],
                              axis=-1).reshape(kv_rank, -1)
    d_w_out = jnp.concatenate([d_w_out_p[:pool_dim],
                               d_w_out_p[pool_dim:].reshape(heads, HEAD_PAD, d)[:, :V_HEAD].reshape(-1, d)], axis=0)

    dmh = jnp.stack([jnp.stack([dmod[l][k][0] for k in range(N_MOD)]) for l in range(2)])
    dmg0 = jnp.stack([dmod[0][k][1] for k in range(N_MOD)])
    dg_rows = jnp.stack([dgain[l][k] for l in range(2) for k in range(3)])
    pieces = [dmh.reshape(2 * N_MOD, d), dmg0, dg_rows, d_cw[:3], d_final_g[None],
              (dpw_h + dpw_g).reshape(-1, d), jnp.pad((dps_h + dps_g)[0], (0, d - pool_dim))[None],
              jnp.pad(acc_qg[0, 0], (0, d - q_rank))[None], jnp.pad(acc_kvg[0, 0], (0, d - kv_rank))[None]]
    n_piece = [p.shape[0] for p in pieces]
    small_g = jnp.concatenate(pieces, axis=0)
    n_small = small_g.shape[0]
    small_g = jnp.pad(small_g, ((0, (-n_small) % 8), (0, 0)))
    sg_all = _exchange("gather_small_grads", [small_g], 'whole')[0]
    sg_sum = _sum_lead("sum_small_grads", sg_all)
    offs = [0]
    for npc in n_piece:
        offs.append(offs[-1] + npc)
    part = lambda j: sg_sum[offs[j]:offs[j + 1]]
    sum_dmh, sum_dmg0, g_norm_full, g_conv_w_full = part(0).reshape(2, N_MOD * d), part(1).reshape(N_MOD * d), part(2), part(3)
    g_final = part(4)[0]
    g_pool_w = part(5).reshape(pool_w.shape)
    g_pool_scale = part(6)[:, :pool_dim]
    g_q_norm = part(7)[:, :q_rank]
    g_kv_norm = part(8)[:, :kv_rank]
    col0 = shard * (d // N_SHARD)
    g_norm_g = lax.dynamic_slice_in_dim(g_norm_full.reshape(2, 3, d), col0, d // N_SHARD, axis=2)
    g_conv_w = lax.dynamic_slice_in_dim(g_conv_w_full, col0, d // N_SHARD, axis=1)[None]
    g_b_mod = _sum_lead("sum_b_mod", jnp.stack([sum_dmh, jnp.stack([sum_dmg0, jnp.zeros_like(sum_dmg0)])]))

    dm16 = []
    for l in range(2):
        per_dev = sg_all[:, l * N_MOD:(l + 1) * N_MOD].reshape(N_DEV, N_MOD * d)
        row8 = sum_dmg0 if l == 0 else jnp.zeros_like(sum_dmg0)
        full = jnp.concatenate([per_dev, row8[None], jnp.zeros((7, N_MOD * d), F32)], axis=0)
        dm16.append(lax.dynamic_slice_in_dim(full, shard * n_col, n_col, axis=1))
    g_w_mod = jnp.stack([_mm(f"mod_dw_{l}", c16, dm16[l], 'tn', tm=256, tn=768, a_pre=_silu) for l in range(2)])
    dc16 = _mm("mod_dx", dm16[0], w_mod[0], 'nt', tm=16, tn=512, epi=lambda acc, i, cv: (acc * _dsilu(cv),),
               epi_args=(c16,), epi_kinds=('mn',))
    dc_all = _exchange("gather_dc", [dc16], 'whole')[0]
    g_c_ctx = _sum_lead("sum_dc", dc_all[::2])[8]

    def chunks704(l):
        return [[dwg00, dwu00], [dwg01, dwu01]] if l == 0 else [[dwg10, dwu10], [dwg11, dwu11]]

    g704 = jnp.stack([jnp.stack([jnp.stack(chunks704(l)[f]) for f in range(2)]) for l in range(2)])
    g704 = g704.transpose(3, 0, 1, 2, 4, 5).reshape(N_SHARD, 8 * d, fc)
    gdown = jnp.stack([jnp.stack([dwd00, dwd01]), jnp.stack([dwd10, dwd11])])
    gdown = gdown.transpose(2, 0, 1, 3, 4).reshape(N_SHARD, 4 * fc, d)
    g1024 = jnp.concatenate([gdown, d_w_out.reshape(N_SHARD, -1, d), d_cw_out.reshape(N_SHARD, -1, d)], axis=1)
    big = [g704, g1024, d_w_uq.reshape(N_SHARD, -1, d_w_uq.shape[1]),
           d_w_ukv.reshape(kv_rank, N_SHARD, -1).transpose(1, 0, 2),
           d_w_in.reshape(d, N_SHARD, -1).transpose(1, 0, 2),
           d_cw_in.reshape(d, N_SHARD, -1).transpose(1, 0, 2)]
    send = [b.astype(BF).reshape(N_DEV, b.shape[1] // 2, b.shape[2]) for b in big]
    landed = _exchange("scatter_grads", send, 'peer')
    halves_sum = [_sum_lead(f"sum_grads_{j}", l) for j, l in enumerate(landed)]
    swapped = _exchange("swap_halves", halves_sum, 'whole', pair=True)
    gsh = [s.reshape(2 * s.shape[1], s.shape[2]) for s in swapped]
    g_gu = gsh[0].reshape(2, 2, 2, d, fc)
    grads = dict(
        c_ctx=g_c_ctx, norm_g=g_norm_g, w_mod=g_w_mod, b_mod=g_b_mod,
        ffn_w_gate=g_gu[:, :, 0], ffn_w_up=g_gu[:, :, 1], ffn_w_down=gsh[1][:4 * fc].reshape(2, 2, fc, d),
        ab_w_in=gsh[4][None], pool_w=g_pool_w, pool_scale=g_pool_scale, q_norm_g=g_q_norm, w_uq=gsh[2][None],
        kv_norm_g=g_kv_norm, w_ukv=gsh[3][None], ab_w_out=gsh[1][4 * fc:4 * fc + n_out_rows][None],
        conv_w_in=gsh[5][None], conv_w=g_conv_w, conv_w_out=gsh[1][4 * fc + n_out_rows:][None], final_norm_g=g_final)

    names = list(weights)
    upd = {n: _adamw(f"adamw_{n}", weights[n], grads[n].reshape(weights[n].shape), mom_m[n], mom_v[n]) for n in names}
    return (loss, grad_x, *[grads[n].reshape(weights[n].shape) for n in names], *[upd[n][0] for n in names],
            *[upd[n][1] for n in names], *[upd[n][2] for n in names])
```

```python
import functools
import math

import jax
import jax.numpy as jnp
from jax import lax
from jax.experimental import pallas as pl
from jax.experimental.pallas import tpu as pltpu

F32 = jnp.float32
BF = jnp.bfloat16
MESH = pl.DeviceIdType.MESH

N_DEV = 8
N_SHARD = 4
RMS_EPS = 1e-6
N_MOD = 9
POOL_WINDOWS = (2, 4, 8, 16)
QK_NOPE = 64
QK_ROPE = 32
V_HEAD = 64
HEAD_PAD = 128
GRID_W = 64
ROPE_THETA = 10000.0
POOL_PAD = 16
ADAM_LR, ADAM_B1, ADAM_B2, ADAM_EPS, ADAM_WD, ADAM_STEP = 0.001, 0.9, 0.999, 1e-08, 0.01, 10
VMEM_LIMIT = 56 * 1024 * 1024


def _pcall(body, **kw):
    return pl.pallas_call(body, **kw)


def _params(sem=None):
    return pltpu.CompilerParams(dimension_semantics=sem, vmem_limit_bytes=VMEM_LIMIT)


def _pick(n, pref, mult=128):
    best = None
    d = mult
    while d <= min(n, pref):
        if n % d == 0:
            best = d
        d += mult
    return best if best is not None else n


def _silu(z):
    return z * jax.nn.sigmoid(z)


def _dsilu(z):
    s = jax.nn.sigmoid(z)
    return s * (1.0 + z * (1.0 - s))


def _dot(a, b, dims):
    return lax.dot_general(a.astype(BF), b.astype(BF), (dims, ((), ())), preferred_element_type=F32)


NN = ((1,), (0,))
NT = ((1,), (1,))
TN = ((0,), (0,))


ALL_FLIPS = [(kx, ky, kc) for kx in (0, 1) for ky in (0, 1) for kc in (0, 1) if (kx, ky, kc) != (0, 0, 0)]
CHIP_FLIPS = [(1, 0, 0), (0, 1, 0), (1, 1, 0)]
SIBLING = (0, 0, 1)
COMM_SPLIT = 8
SPLIT_MIN_ROWS = 256


def _exchange(name, arrays, plan, lead, whole_src, split=COMM_SPLIT):
    n = len(arrays)
    blk_shapes = [tuple(a.shape) if whole_src else tuple(a.shape[1:]) for a in arrays]
    splits = []
    for shp in blk_shapes:
        s = 1
        while s * 2 <= split and shp[0] % (s * 2) == 0 and (shp[0] // (s * 2)) % 16 == 0 \
                and shp[0] // (s * 2) >= SPLIT_MIN_ROWS:
            s *= 2
        splits.append(s)
    items = plan(0, 0, 0)
    n_items = len(items)
    remote_ids = [k for k, it in enumerate(items) if it[0] is not None]
    local_ids = [k for k, it in enumerate(items) if it[0] is None]
    slots = [(a, s) for s in range(max(splits)) for a in range(n) if s < splits[a]]
    n_slot = len(slots)

    def body(*refs):
        ins, outs = refs[:n], refs[n:2 * n]
        send_sems, recv_sems, loc_sems = refs[2 * n:]
        x, y, c = lax.axis_index("x"), lax.axis_index("y"), lax.axis_index("c")
        plan_here = plan(x, y, c)

        def rows(ref, a, s):
            rc = blk_shapes[a][0] // splits[a]
            return ref.at[pl.ds(s * rc, rc)]

        def make(si, k):
            a, s = slots[si]
            flip, src, dst, _ = plan_here[k]
            base = outs[a] if src[0] == 'out' else ins[a]
            src_ref = rows(base if src[1] is None else base.at[src[1]], a, s)
            dst_ref = rows(outs[a].at[dst], a, s)
            if flip is None:
                return pltpu.make_async_copy(src_ref, dst_ref, loc_sems.at[si * max(1, len(local_ids)) + local_ids.index(k)])
            peer = (1 - x if flip[0] else x, 1 - y if flip[1] else y, 1 - c if flip[2] else c)
            sem = si * len(remote_ids) + remote_ids.index(k)
            return pltpu.make_async_remote_copy(src_ref=src_ref, dst_ref=dst_ref, send_sem=send_sems.at[sem],
                                                recv_sem=recv_sems.at[sem], device_id=peer, device_id_type=MESH)

        copies = {}
        for si in range(n_slot):
            for k in range(n_items):
                if plan_here[k][3] is None:
                    copies[si, k] = make(si, k)
                    copies[si, k].start()
        arrived = set()
        for si in range(n_slot):
            for k in range(n_items):
                after = plan_here[k][3]
                if after is not None:
                    if (si, after) not in arrived:
                        copies[si, after].wait_recv()
                        arrived.add((si, after))
                    copies[si, k] = make(si, k)
                    copies[si, k].start()
        for (si, k), cp in copies.items():
            if plan_here[k][0] is None:
                cp.wait()
            else:
                cp.wait_send()
                if (si, k) not in arrived:
                    cp.wait_recv()

    any_spec = pl.BlockSpec(memory_space=pl.ANY)
    n_rem = max(1, n_slot * len(remote_ids))
    outs = _pcall(
        body, name=name,
        out_shape=[jax.ShapeDtypeStruct((lead,) + s, a.dtype) for s, a in zip(blk_shapes, arrays)],
        in_specs=[any_spec] * n, out_specs=[any_spec] * n,
        scratch_shapes=[pltpu.SemaphoreType.DMA((n_rem,)), pltpu.SemaphoreType.DMA((n_rem,)),
                        pltpu.SemaphoreType.DMA((max(1, n_slot * len(local_ids)),))],
    )(*arrays)
    return list(outs)


def _place(x, y, c):
    return 4 * x + 2 * y + c


def _flip(v, f):
    return 1 - v if f else v


def _gather_all(name, arrays):
    def plan(x, y, c):
        me = _place(x, y, c)
        return [(None, ('in', None), me, None)] + [(f, ('in', None), me, None) for f in ALL_FLIPS]
    return _exchange(name, arrays, plan, N_DEV, True)


def _gather_halves(name, arrays):
    def plan(x, y, c):
        chip = 2 * x + y
        items = [(None, ('in', 0), 2 * chip, None), (None, ('in', 1), 2 * chip + 1, None)]
        items += [(f, ('in', c), 2 * chip + c, None) for f in CHIP_FLIPS]
        for j, f in enumerate(CHIP_FLIPS):
            blk = 2 * (2 * _flip(x, f[0]) + _flip(y, f[1])) + c
            items.append((SIBLING, ('out', blk), blk, 2 + j))
        return items
    return _exchange(name, arrays, plan, N_DEV, False)


def _to_sibling_halves(name, arrays):
    def plan(x, y, c):
        items = [(SIBLING, ('in', 2 * sh + 1 - c), sh, None) for sh in range(N_SHARD)]
        items += [(None, ('in', 2 * sh + c), N_SHARD + sh, None) for sh in range(N_SHARD)]
        return items
    return _exchange(name, arrays, plan, 2 * N_SHARD, False)


def _to_chips(name, arrays):
    def plan(x, y, c):
        chip = 2 * x + y
        items = [(None, ('in', chip), chip, None)]
        items += [(f, ('in', 2 * _flip(x, f[0]) + _flip(y, f[1])), chip, None) for f in CHIP_FLIPS]
        return items
    return _exchange(name, arrays, plan, N_SHARD, False)


def _swap_sibling(name, arrays):
    def plan(x, y, c):
        return [(None, ('in', None), c, None), (SIBLING, ('in', None), c, None)]
    return _exchange(name, arrays, plan, 2, True)


def _sum_lead(name, arr, out_dtype=F32):
    n, r, cdim = arr.shape
    tr = r
    limit = (4 << 20) // (n * cdim * arr.dtype.itemsize)
    if r > limit:
        tr = _pick(r, max(limit, 16), 16)

    def body(x_ref, o_ref):
        acc = x_ref[0].astype(F32)
        for d in range(1, n):
            acc = acc + x_ref[d].astype(F32)
        o_ref[...] = acc.astype(out_dtype)

    return _pcall(body, name=name, grid=(r // tr,),
                  in_specs=[pl.BlockSpec((n, tr, cdim), lambda i: (0, i, 0))],
                  out_specs=pl.BlockSpec((tr, cdim), lambda i: (i, 0)),
                  out_shape=jax.ShapeDtypeStruct((r, cdim), out_dtype),
                  compiler_params=_params(("arbitrary",)))(arr)


def _rows_call(name, fn, n_rows, tm, rows, consts, mod, outs, acc_w=None, h_tiles=None):
    nt = n_rows // tm
    ht = nt if h_tiles is None else h_tiles
    ng = 1 if mod is None else mod.shape[0]
    n_r, n_c, n_o = len(rows), len(consts), len(outs)
    has_mod = mod is not None

    def body(*refs):
        i = pl.program_id(0)
        first = (i % ht) == 0
        row_refs, const_refs = refs[:n_r], refs[n_r:n_r + n_c]
        p = n_r + n_c
        mod_tile = refs[p][...] if has_mod else None
        p += int(has_mod)
        out_refs = refs[p:p + n_o]
        o, acc = fn([r[...] for r in row_refs], [r[...] for r in const_refs], mod_tile)
        for r, v in zip(out_refs, o):
            r[...] = v.astype(r.dtype)
        if acc_w is not None:
            acc_ref = refs[p + n_o]

            @pl.when(first)
            def _():
                acc_ref[...] = jnp.zeros_like(acc_ref)

            for k, v in acc.items():
                acc_ref[k:k + 1, :] += v

    in_specs = [pl.BlockSpec((tm, r.shape[1]), lambda i: (i, 0)) for r in rows]
    in_specs += [pl.BlockSpec(cst.shape, lambda i, nd=cst.ndim: (0,) * nd) for cst in consts]
    args = list(rows) + list(consts)
    if has_mod:
        in_specs.append(pl.BlockSpec((None,) + mod.shape[1:], lambda i: (i // ht, 0, 0)))
        args.append(mod)
    out_shape = [jax.ShapeDtypeStruct((n_rows, w), dt) for w, dt in outs]
    out_specs = [pl.BlockSpec((tm, w), lambda i: (i, 0)) for w, _ in outs]
    if acc_w is not None:
        out_shape.append(jax.ShapeDtypeStruct((ng, 8, acc_w), F32))
        out_specs.append(pl.BlockSpec((None, 8, acc_w), lambda i: (i // ht, 0, 0)))
    res = _pcall(body, name=name, grid=(nt,), in_specs=in_specs, out_specs=out_specs, out_shape=out_shape,
                 compiler_params=_params(("arbitrary",)))(*args)
    return list(res)


def _rms(s):
    r = lax.rsqrt(jnp.mean(s * s, axis=1, keepdims=True) + RMS_EPS)
    return s * r, r


def _rms_bwd(dn, n, r):
    return r * (dn - n * jnp.mean(dn * n, axis=1, keepdims=True))


def _adaln_fwd(name, s, gains, gain_row, mod, k, tm, h_tiles):
    def fn(rows, consts, m):
        n, _ = _rms(rows[0])
        y = n * consts[0][gain_row:gain_row + 1, :]
        return [y * (1.0 + m[3 * k + 1:3 * k + 2, :]) + m[3 * k:3 * k + 1, :]], {}

    d = s.shape[1]
    return _rows_call(name, fn, s.shape[0], tm, [s], [gains], mod, [(d, BF)], h_tiles=h_tiles)[0]


def _adaln_bwd(name, s, du, ds_res, gains, gain_row, mod, k, tm, h_tiles):
    def fn(rows, consts, m):
        sv, duv, res = rows
        gain = consts[0][gain_row:gain_row + 1, :]
        n, r = _rms(sv)
        y = n * gain
        dy = duv * (1.0 + m[3 * k + 1:3 * k + 2, :])
        acc = {0: jnp.sum(duv, axis=0, keepdims=True), 1: jnp.sum(duv * y, axis=0, keepdims=True),
               2: jnp.sum(dy * n, axis=0, keepdims=True)}
        return [_rms_bwd(dy * gain, n, r) + res], acc

    d = s.shape[1]
    return _rows_call(name, fn, s.shape[0], tm, [s, du, ds_res], [gains], mod, [(d, F32)], acc_w=d, h_tiles=h_tiles)


def _resid_bwd(name, ds_out, o, mod, k, cst, tm, h_tiles):
    def fn(rows, consts, m):
        dsv, ov = rows
        gate = m[3 * k + 2:3 * k + 3, :]
        return [cst * gate * dsv], {0: jnp.sum(cst * ov * dsv, axis=0, keepdims=True)}

    d = o.shape[1]
    return _rows_call(name, fn, o.shape[0], tm, [ds_out, o], [], mod, [(d, BF)], acc_w=d, h_tiles=h_tiles)


def _mm(name, a, b, mode, tm=256, tn=512, out_dtypes=(F32,), epi=None, epi_args=(), epi_kinds=(), a_pre=None):
    if mode == 'nn':
        (m, kd), nd = a.shape, b.shape[1]
    elif mode == 'nt':
        (m, kd), nd = a.shape, b.shape[0]
    else:
        (kd, m), nd = a.shape, b.shape[1]
    tm = _pick(m, tm, 16) if m % tm else tm
    tn = _pick(nd, tn, 128) if nd % tn else tn
    dims = {'nn': NN, 'nt': NT, 'tn': TN}[mode]
    n_e, n_o = len(epi_args), len(out_dtypes)

    def body(*refs):
        i = pl.program_id(1)
        av = refs[0][...]
        if a_pre is not None:
            av = a_pre(av)
        acc = _dot(av, refs[1][...], dims)
        res = (acc,) if epi is None else epi(acc, i, *[r[...] for r in refs[2:2 + n_e]])
        for r, v in zip(refs[2 + n_e:], res):
            r[...] = v.astype(r.dtype)

    if mode == 'nn':
        specs = [pl.BlockSpec((tm, kd), lambda j, i: (i, 0)), pl.BlockSpec((kd, tn), lambda j, i: (0, j))]
    elif mode == 'nt':
        specs = [pl.BlockSpec((tm, kd), lambda j, i: (i, 0)), pl.BlockSpec((tn, kd), lambda j, i: (j, 0))]
    else:
        specs = [pl.BlockSpec((kd, tm), lambda j, i: (0, i)), pl.BlockSpec((kd, tn), lambda j, i: (0, j))]
    for arr, kind in zip(epi_args, epi_kinds):
        if kind == 'mn':
            specs.append(pl.BlockSpec((tm, tn), lambda j, i: (i, j)))
        elif kind == 'n':
            specs.append(pl.BlockSpec((1, tn), lambda j, i: (0, j)))
        elif kind == 'mt':
            specs.append(pl.BlockSpec((tm, arr.shape[1]), lambda j, i: (i, 0)))
        else:
            specs.append(pl.BlockSpec(arr.shape, lambda j, i, nd_=arr.ndim: (0,) * nd_))
    res = _pcall(body, name=name, grid=(nd // tn, m // tm), in_specs=specs,
                 out_specs=[pl.BlockSpec((tm, tn), lambda j, i: (i, j))] * n_o,
                 out_shape=[jax.ShapeDtypeStruct((m, nd), dt) for dt in out_dtypes],
                 compiler_params=_params(("arbitrary", "arbitrary")))(a, b, *epi_args)
    return res[0] if n_o == 1 else list(res)


def _row_gate(mod, k3, i, tm, n_lat):
    g0 = mod[0, k3:k3 + 1, :]
    if mod.shape[0] == 1:
        return g0
    rid = i * tm + lax.broadcasted_iota(jnp.int32, (tm, 1), 0)
    return jnp.where(rid < n_lat, g0, mod[1, k3:k3 + 1, :])


def _ffn_up(name, u, wgu, base, tm):
    r, d = u.shape
    nch, _, _, fc = wgu.shape

    def body(u_ref, wg_ref, wu_ref, a_ref, b_ref, h_ref):
        uv = u_ref[...]
        a = _dot(uv, wg_ref[...], NN)
        b = _dot(uv, wu_ref[...], NN)
        a_ref[...] = a.astype(BF)
        b_ref[...] = b.astype(BF)
        h_ref[...] = (_silu(a) * b).astype(BF)

    chunk = pl.BlockSpec((None, tm, fc), lambda j, i: (j, i, 0))
    return _pcall(body, name=name, grid=(nch, r // tm),
                  in_specs=[pl.BlockSpec((tm, d), lambda j, i: (i, 0)),
                            pl.BlockSpec((None, None, d, fc), lambda j, i: (j, base, 0, 0)),
                            pl.BlockSpec((None, None, d, fc), lambda j, i: (j, base + 1, 0, 0))],
                  out_specs=[chunk] * 3, out_shape=[jax.ShapeDtypeStruct((nch, r, fc), BF)] * 3,
                  compiler_params=_params(("arbitrary", "arbitrary")))(u, wgu, wgu)


def _ffn_down(name, hid, wd, wd_blk, s, mod, k, n_lat, tm):
    nch, r, fc = hid.shape
    d = wd.shape[2]

    def body(h_ref, w_ref, s_ref, m_ref, so_ref, o_ref, acc_ref):
        i, j = pl.program_id(0), pl.program_id(1)
        part = _dot(h_ref[...], w_ref[...], NN)

        @pl.when(j == 0)
        def _():
            acc_ref[...] = part

        @pl.when(j > 0)
        def _():
            acc_ref[...] += part

        @pl.when(j == nch - 1)
        def _():
            o = acc_ref[...]
            o_ref[...] = o
            so_ref[...] = s_ref[...] + 0.5 * _row_gate(m_ref[...], 3 * k + 2, i, tm, n_lat) * o

    row = pl.BlockSpec((tm, d), lambda i, j: (i, 0))
    return _pcall(body, name=name, grid=(r // tm, nch),
                  in_specs=[pl.BlockSpec((None, tm, fc), lambda i, j: (j, i, 0)),
                            pl.BlockSpec((None, fc, d), lambda i, j: (j, wd_blk, 0)), row,
                            pl.BlockSpec(mod.shape, lambda i, j: (0, 0, 0))],
                  out_specs=[row, row], out_shape=[jax.ShapeDtypeStruct((r, d), F32)] * 2,
                  scratch_shapes=[pltpu.VMEM((tm, d), F32)],
                  compiler_params=_params(("arbitrary", "arbitrary")))(hid, wd, s, mod)


def _ffn_dhid(name, d_o, wd, wd_blk, a, b, tm):
    r, d = d_o.shape
    nch, _, fc = a.shape

    def body(g_ref, w_ref, a_ref, b_ref, da_ref, db_ref):
        dh = _dot(g_ref[...], w_ref[...], NT)
        av, bv = a_ref[...].astype(F32), b_ref[...].astype(F32)
        da_ref[...] = (dh * bv * _dsilu(av)).astype(BF)
        db_ref[...] = (dh * _silu(av)).astype(BF)

    chunk = pl.BlockSpec((None, tm, fc), lambda j, i: (j, i, 0))
    return _pcall(body, name=name, grid=(nch, r // tm),
                  in_specs=[pl.BlockSpec((tm, d), lambda j, i: (i, 0)),
                            pl.BlockSpec((None, fc, d), lambda j, i: (j, wd_blk, 0)), chunk, chunk],
                  out_specs=[chunk] * 2, out_shape=[jax.ShapeDtypeStruct((nch, r, fc), BF)] * 2,
                  compiler_params=_params(("arbitrary", "arbitrary")))(d_o, wd, a, b)


def _ffn_du(name, da, db, wgu, base, tm):
    nch, r, fc = da.shape
    d = wgu.shape[2]

    def body(da_ref, db_ref, wg_ref, wu_ref, o_ref, acc_ref):
        j = pl.program_id(1)
        part = _dot(da_ref[...], wg_ref[...], NT) + _dot(db_ref[...], wu_ref[...], NT)

        @pl.when(j == 0)
        def _():
            acc_ref[...] = part

        @pl.when(j > 0)
        def _():
            acc_ref[...] += part

        @pl.when(j == nch - 1)
        def _():
            o_ref[...] = acc_ref[...]

    chunk = pl.BlockSpec((None, tm, fc), lambda i, j: (j, i, 0))
    return _pcall(body, name=name, grid=(r // tm, nch),
                  in_specs=[chunk, chunk, pl.BlockSpec((None, None, d, fc), lambda i, j: (j, base, 0, 0)),
                            pl.BlockSpec((None, None, d, fc), lambda i, j: (j, base + 1, 0, 0))],
                  out_specs=pl.BlockSpec((tm, d), lambda i, j: (i, 0)),
                  out_shape=jax.ShapeDtypeStruct((r, d), F32), scratch_shapes=[pltpu.VMEM((tm, d), F32)],
                  compiler_params=_params(("arbitrary", "arbitrary")))(da, db, wgu, wgu)


def _ffn_dw_in(name, u, dz, tmm):
    r, d = u.shape
    nch, _, fc = dz.shape

    def body(u_ref, z_ref, o_ref):
        o_ref[...] = _dot(u_ref[...], z_ref[...], TN)

    return _pcall(body, name=name, grid=(nch, d // tmm),
                  in_specs=[pl.BlockSpec((r, tmm), lambda j, mi: (0, mi)),
                            pl.BlockSpec((None, r, fc), lambda j, mi: (j, 0, 0))],
                  out_specs=pl.BlockSpec((None, tmm, fc), lambda j, mi: (j, mi, 0)),
                  out_shape=jax.ShapeDtypeStruct((nch, d, fc), F32),
                  compiler_params=_params(("arbitrary", "arbitrary")))(u, dz)


def _ffn_dw_down(name, hid, d_o, tn):
    nch, r, fc = hid.shape
    d = d_o.shape[1]

    def body(h_ref, g_ref, o_ref):
        o_ref[...] = _dot(h_ref[...], g_ref[...], TN)

    return _pcall(body, name=name, grid=(nch, d // tn),
                  in_specs=[pl.BlockSpec((None, r, fc), lambda j, ni: (j, 0, 0)),
                            pl.BlockSpec((r, tn), lambda j, ni: (0, ni))],
                  out_specs=pl.BlockSpec((None, fc, tn), lambda j, ni: (j, 0, ni)),
                  out_shape=jax.ShapeDtypeStruct((nch, fc, d), F32),
                  compiler_params=_params(("arbitrary", "arbitrary")))(hid, d_o)


def _partner(x):
    n = x.shape[1]
    lane = lax.broadcasted_iota(jnp.int32, x.shape, 1)
    return jnp.where((lane & 15) < 8, pltpu.roll(x, n - 8, 1), pltpu.roll(x, 8, 1))


def _rope(x, ct, st):
    reps = x.shape[1] // ct.shape[1]
    if reps > 1:
        ct, st = jnp.tile(ct, (1, reps)), jnp.tile(st, (1, reps))
    return x * ct + _partner(x) * st


def _rope_t(dy, ct, st):
    reps = dy.shape[1] // ct.shape[1]
    if reps > 1:
        ct, st = jnp.tile(ct, (1, reps)), jnp.tile(st, (1, reps))
    return dy * ct + _partner(dy * st)


def _rope_tables(t_len, g_len, lane0):
    half = QK_ROPE // 4
    pos = jnp.arange(t_len)
    row = (pos // GRID_W).astype(F32)
    col = (pos % GRID_W).astype(F32)
    freqs = jnp.power(ROPE_THETA, -jnp.arange(0, QK_ROPE // 2, 2, dtype=F32) / (QK_ROPE // 2))
    ang_r, ang_c = row[:, None] * freqs, col[:, None] * freqs
    cs = jnp.concatenate([jnp.cos(ang_r)] * 2 + [jnp.cos(ang_c)] * 2, axis=1)
    sn = jnp.concatenate([-jnp.sin(ang_r), jnp.sin(ang_r), -jnp.sin(ang_c), jnp.sin(ang_c)], axis=1)
    assert cs.shape[1] == 4 * half == QK_ROPE
    ct = jnp.ones((t_len + g_len, HEAD_PAD), F32).at[:t_len, lane0:lane0 + QK_ROPE].set(cs)
    st = jnp.zeros((t_len + g_len, HEAD_PAD), F32).at[:t_len, lane0:lane0 + QK_ROPE].set(sn)
    return ct, st


def _attn_fwd(name, q, kp, vp, n_q, q_off, n_k, k_blk, heads, tq, scale):
    qb = q_off // tq

    def body(q_ref, k_ref, v_ref, o_ref, l_ref):
        s = _dot(q_ref[...], k_ref[...], NT) * scale
        m = jnp.max(s, axis=1, keepdims=True)
        p = jnp.exp(s - m)
        l = jnp.sum(p, axis=1, keepdims=True)
        o_ref[...] = (_dot(p, v_ref[...], NN) / l).astype(BF)
        l_ref[...] = jnp.broadcast_to(m + jnp.log(l), l_ref.shape)

    hw = heads * HEAD_PAD
    blk = pl.BlockSpec((tq, HEAD_PAD), lambda h, i: (i, h))
    kv = pl.BlockSpec((n_k, HEAD_PAD), lambda h, i: (k_blk, h))
    return _pcall(body, name=name, grid=(heads, n_q // tq),
                  in_specs=[pl.BlockSpec((tq, HEAD_PAD), lambda h, i: (i + qb, h)), kv, kv],
                  out_specs=[blk, blk],
                  out_shape=[jax.ShapeDtypeStruct((n_q, hw), BF), jax.ShapeDtypeStruct((n_q, hw), F32)],
                  compiler_params=_params(("arbitrary", "arbitrary")))(q, kp, vp)


def _attn_bwd(name, q, kp, vp, cat, dcat, lse, n_q, q_off, n_k, k_blk, heads, tq, scale, col_blk):
    qb = q_off // tq

    def body(q_ref, k_ref, v_ref, o_ref, do_ref, l_ref, dq_ref, dk_ref, dv_ref):
        i = pl.program_id(1)
        qv, kv_, vv = q_ref[...], k_ref[...], v_ref[...]
        dov = do_ref[...]
        s = _dot(qv, kv_, NT) * scale
        p = jnp.exp(s - l_ref[...][:, 0:1])
        dp = _dot(dov, vv, NT)
        delta = jnp.sum(dov * o_ref[...].astype(F32), axis=1, keepdims=True)
        ds = (p * (dp - delta) * scale).astype(BF)
        dq_ref[...] = _dot(ds, kv_, NN)
        dk = _dot(ds, qv, TN)
        dv = _dot(p, dov, TN)

        @pl.when(i == 0)
        def _():
            dk_ref[...] = dk
            dv_ref[...] = dv

        @pl.when(i > 0)
        def _():
            dk_ref[...] += dk
            dv_ref[...] += dv

    hw = heads * HEAD_PAD
    qspec = pl.BlockSpec((tq, HEAD_PAD), lambda h, i: (i + qb, h))
    cspec = pl.BlockSpec((tq, HEAD_PAD), lambda h, i: (i + qb, col_blk + h))
    kv = pl.BlockSpec((n_k, HEAD_PAD), lambda h, i: (k_blk, h))
    acc = pl.BlockSpec((n_k, HEAD_PAD), lambda h, i: (0, h))
    blk = pl.BlockSpec((tq, HEAD_PAD), lambda h, i: (i, h))
    return _pcall(body, name=name, grid=(heads, n_q // tq),
                  in_specs=[qspec, kv, kv, cspec, cspec, blk], out_specs=[blk, acc, acc],
                  out_shape=[jax.ShapeDtypeStruct((n_q, hw), F32), jax.ShapeDtypeStruct((n_k, hw), F32),
                             jax.ShapeDtypeStruct((n_k, hw), F32)],
                  compiler_params=_params(("arbitrary", "arbitrary")))(q, kp, vp, cat, dcat, lse)


def _shift(x, k):
    return pltpu.roll(x, k % x.shape[0], 0)


def _window_sum(v, w, mirrored):
    n, gd = v.shape
    pad = jnp.zeros((POOL_PAD, gd), F32)
    e = jnp.concatenate([pad, v, pad], axis=0)
    acc = e + _shift(e, -1 if mirrored else 1)
    step = 1
    while 2 * step < w:
        acc = _shift(acc, step) + _shift(acc, -step)
        step *= 2
    return acc[POOL_PAD:POOL_PAD + n]


def _window_count(n, w):
    t = lax.broadcasted_iota(jnp.int32, (n, 1), 0)
    lo = jnp.maximum(t - w // 2, 0)
    hi = jnp.minimum(t + (w - w // 2 - 1), n - 1)
    return (hi - lo + 1).astype(F32)


def _pool_fwd(name, u, pool_w, scale):
    n, pd = u.shape
    ng = len(POOL_WINDOWS)
    gd = pd // ng

    def body(u_ref, w_ref, s_ref, y_ref):
        for g, w in enumerate(POOL_WINDOWS):
            sl = slice(g * gd, (g + 1) * gd)
            ug = u_ref[:, sl]
            p = _window_sum(ug, w, False) / _window_count(n, w) - ug
            y_ref[:, sl] = (_dot(p, w_ref[g], NN) * s_ref[:, sl]).astype(BF)

    return _pcall(body, name=name, out_shape=jax.ShapeDtypeStruct((n, pd), BF),
                  compiler_params=_params())(u, pool_w, scale)


def _pool_bwd(name, u, dcat, pool_w, scale, row_off):
    n, pd = u.shape
    ng = len(POOL_WINDOWS)
    gd = pd // ng

    def body(u_ref, dy_ref, w_ref, s_ref, du_ref, dw_ref, ds_ref):
        ds_ref[...] = jnp.zeros_like(ds_ref)
        for g, w in enumerate(POOL_WINDOWS):
            sl = slice(g * gd, (g + 1) * gd)
            ug, dy, wg = u_ref[:, sl], dy_ref[:, sl], w_ref[g]
            cnt = _window_count(n, w)
            p = _window_sum(ug, w, False) / cnt - ug
            ds_ref[0:1, sl] = jnp.sum(dy * _dot(p, wg, NN), axis=0, keepdims=True)
            dys = dy * s_ref[:, sl]
            dw_ref[g] = _dot(p, dys, TN)
            dp = _dot(dys, wg, NT)
            du_ref[:, sl] = (_window_sum(dp / cnt, w, True) - dp).astype(BF)

    rb = row_off // n
    return _pcall(body, name=name, grid=(1,),
                  in_specs=[pl.BlockSpec((n, pd), lambda i: (0, 0)), pl.BlockSpec((n, pd), lambda i: (rb, 0)),
                            pl.BlockSpec(pool_w.shape, lambda i: (0, 0, 0)), pl.BlockSpec(scale.shape, lambda i: (0, 0))],
                  out_specs=[pl.BlockSpec((n, pd), lambda i: (0, 0)), pl.BlockSpec((ng, gd, gd), lambda i: (0, 0, 0)),
                             pl.BlockSpec((8, pd), lambda i: (0, 0))],
                  out_shape=[jax.ShapeDtypeStruct((n, pd), BF), jax.ShapeDtypeStruct((ng, gd, gd), F32),
                             jax.ShapeDtypeStruct((8, pd), F32)],
                  compiler_params=_params(("arbitrary",)))(u, dcat, pool_w, scale)


def _edge_shift(z, k):
    n = z.shape[0]
    t = lax.broadcasted_iota(jnp.int32, (n, 1), 0)
    keep = (t >= k) if k > 0 else (t < n + k)
    return jnp.where(keep, pltpu.roll(z, k % n, 0), 0.0)


def _conv_fwd(name, p3, cw, tc):
    n, cd = p3.shape[0], p3.shape[1] // 3
    nb = cd // tc

    def body(b_ref, c_ref, v_ref, w_ref, y_ref):
        z = c_ref[...] * v_ref[...]
        w = w_ref[...]
        zc = w[0:1] * _edge_shift(z, 1) + w[1:2] * z + w[2:3] * _edge_shift(z, -1)
        y_ref[...] = (b_ref[...] * zc).astype(BF)

    return _pcall(body, name=name, grid=(nb,),
                  in_specs=[pl.BlockSpec((n, tc), lambda j: (0, j)), pl.BlockSpec((n, tc), lambda j: (0, nb + j)),
                            pl.BlockSpec((n, tc), lambda j: (0, 2 * nb + j)), pl.BlockSpec((3, tc), lambda j: (0, j))],
                  out_specs=pl.BlockSpec((n, tc), lambda j: (0, j)), out_shape=jax.ShapeDtypeStruct((n, cd), BF),
                  compiler_params=_params(("arbitrary",)))(p3, p3, p3, cw)


def _conv_bwd(name, p3, cw, dy, tc):
    n, cd = dy.shape
    nb = cd // tc

    def body(b_ref, c_ref, v_ref, w_ref, dy_ref, dp_ref, dw_ref):
        cv, vv, w, dyv = c_ref[...], v_ref[...], w_ref[...], dy_ref[...]
        z = cv * vv
        zl, zr = _edge_shift(z, 1), _edge_shift(z, -1)
        zc = w[0:1] * zl + w[1:2] * z + w[2:3] * zr
        dzc = dyv * b_ref[...]
        dz = w[0:1] * _edge_shift(dzc, -1) + w[1:2] * dzc + w[2:3] * _edge_shift(dzc, 1)
        dp_ref[0] = (dyv * zc).astype(BF)
        dp_ref[1] = (dz * vv).astype(BF)
        dp_ref[2] = (dz * cv).astype(BF)
        dw_ref[...] = jnp.zeros_like(dw_ref)
        dw_ref[0:1, :] = jnp.sum(dzc * zl, axis=0, keepdims=True)
        dw_ref[1:2, :] = jnp.sum(dzc * z, axis=0, keepdims=True)
        dw_ref[2:3, :] = jnp.sum(dzc * zr, axis=0, keepdims=True)

    col = pl.BlockSpec((n, tc), lambda j: (0, j))
    return _pcall(body, name=name, grid=(nb,),
                  in_specs=[col, pl.BlockSpec((n, tc), lambda j: (0, nb + j)),
                            pl.BlockSpec((n, tc), lambda j: (0, 2 * nb + j)), pl.BlockSpec((3, tc), lambda j: (0, j)), col],
                  out_specs=[pl.BlockSpec((3, n, tc), lambda j: (0, 0, j)), pl.BlockSpec((8, tc), lambda j: (0, j))],
                  out_shape=[jax.ShapeDtypeStruct((3, n, cd), BF), jax.ShapeDtypeStruct((8, cd), F32)],
                  compiler_params=_params(("arbitrary",)))(p3, p3, p3, cw, dy)


def _conv_din(name, dp3, w_in, tm):
    _, n, cd = dp3.shape
    d = w_in.shape[0]

    def body(a_ref, w_ref, o_ref, acc_ref):
        j = pl.program_id(1)
        part = _dot(a_ref[...], w_ref[...], NT)

        @pl.when(j == 0)
        def _():
            acc_ref[...] = part

        @pl.when(j > 0)
        def _():
            acc_ref[...] += part

        @pl.when(j == 2)
        def _():
            o_ref[...] = acc_ref[...]

    return _pcall(body, name=name, grid=(n // tm, 3),
                  in_specs=[pl.BlockSpec((None, tm, cd), lambda i, j: (j, i, 0)),
                            pl.BlockSpec((d, cd), lambda i, j: (0, j))],
                  out_specs=pl.BlockSpec((tm, d), lambda i, j: (i, 0)), out_shape=jax.ShapeDtypeStruct((n, d), F32),
                  scratch_shapes=[pltpu.VMEM((tm, d), F32)],
                  compiler_params=_params(("arbitrary", "arbitrary")))(dp3, w_in)


def _conv_dw_in(name, u, dp3, tmm, tn):
    n, d = u.shape
    cd = dp3.shape[2]
    nb = cd // tn

    def body(u_ref, z_ref, o_ref):
        o_ref[...] = _dot(u_ref[...], z_ref[...], TN)

    return _pcall(body, name=name, grid=(3 * nb, d // tmm),
                  in_specs=[pl.BlockSpec((n, tmm), lambda j, mi: (0, mi)),
                            pl.BlockSpec((None, n, tn), lambda j, mi: (j // nb, 0, j % nb))],
                  out_specs=pl.BlockSpec((tmm, tn), lambda j, mi: (mi, j)),
                  out_shape=jax.ShapeDtypeStruct((d, 3 * cd), F32),
                  compiler_params=_params(("arbitrary", "arbitrary")))(u, dp3)


def _loss_head(name, h, target, gain, tm):
    d = h.shape[1]

    def fn(rows, consts, m):
        hv, tv = rows
        g = consts[0][0:1, :]
        n, r = _rms(hv)
        err = n * g - tv
        dy = err / d
        loss = 0.5 * jnp.sum(err * err) / d
        acc = {0: jnp.sum(dy * n, axis=0, keepdims=True), 1: jnp.full((1, d), loss, F32)}
        return [_rms_bwd(dy * g, n, r)], acc

    return _rows_call(name, fn, h.shape[0], tm, [h, target], [gain], None, [(d, F32)], acc_w=d)


def _adamw(name, w, g, m, v):
    shape = w.shape
    cdim = shape[-1]
    r = max(1, math.prod(shape[:-1]))
    tr = r
    if r * cdim * 4 > (3 << 19):
        tr = _pick(r, max(8, (3 << 19) // (cdim * 4)), 8)
    c1 = 1.0 / (1.0 - ADAM_B1 ** ADAM_STEP)
    c2 = 1.0 / (1.0 - ADAM_B2 ** ADAM_STEP)

    def body(w_ref, g_ref, m_ref, v_ref, d_ref, nm_ref, nv_ref):
        gv = g_ref[...]
        nm = ADAM_B1 * m_ref[...] + (1.0 - ADAM_B1) * gv
        nv = ADAM_B2 * v_ref[...] + (1.0 - ADAM_B2) * (gv * gv)
        nm_ref[...] = nm
        nv_ref[...] = nv
        d_ref[...] = -ADAM_LR * ((nm * c1) / (jnp.sqrt(nv * c2) + ADAM_EPS) + ADAM_WD * w_ref[...])

    spec = pl.BlockSpec((tr, cdim), lambda i: (i, 0))
    res = _pcall(body, name=name, grid=(r // tr,), in_specs=[spec] * 4, out_specs=[spec] * 3,
                 out_shape=[jax.ShapeDtypeStruct((r, cdim), F32)] * 3,
                 compiler_params=_params(("arbitrary",)))(*[t.reshape(r, cdim) for t in (w, g, m, v)])
    return [t.reshape(shape) for t in res]


def _ffn_half_fwd(tag, s, gains, mod, k, wts, n_lat, tm, h_tiles, tm_big):
    wgu, base, wd, wd_blk = wts
    u = _adaln_fwd(f"adaln_{tag}", s, gains, k, mod, k, tm, h_tiles)
    a, b, hid = _ffn_up(f"ffn_up_{tag}", u, wgu, base, tm)
    s_out, o = _ffn_down(f"ffn_down_{tag}", hid, wd, wd_blk, s, mod, k, n_lat, tm_big)
    return s_out, (s, u, a, b, hid, o)


def _ffn_half_bwd(tag, ds_out, saved, gains, mod, k, wts, tm, h_tiles, tm_big):
    wgu, base, wd, wd_blk = wts
    s, u, a, b, hid, o = saved
    d_o, acc_g = _resid_bwd(f"resid_bwd_{tag}", ds_out, o, mod, k, 0.5, tm, h_tiles)
    da, db = _ffn_dhid(f"ffn_dhid_{tag}", d_o, wd, wd_blk, a, b, tm)
    du = _ffn_du(f"ffn_du_{tag}", da, db, wgu, base, tm_big)
    d = u.shape[1]
    dwg = _ffn_dw_in(f"ffn_dwg_{tag}", u, da, _pick(d, 256))
    dwu = _ffn_dw_in(f"ffn_dwu_{tag}", u, db, _pick(d, 256))
    dwd = _ffn_dw_down(f"ffn_dwd_{tag}", hid, d_o, _pick(d, 512))
    ds, acc_n = _adaln_bwd(f"adaln_bwd_{tag}", s, du, ds_out, gains, k, mod, k, tm, h_tiles)
    return ds, dwg, dwu, dwd, (acc_n[:, 0], acc_n[:, 1], acc_g[:, 0]), jnp.sum(acc_n[:, 2], axis=0)


def kernel(x, c, ctx, c_ctx, norm_g, w_mod, b_mod, ffn_w_gate, ffn_w_up, ffn_w_down, ab_w_in, pool_w, pool_scale, q_norm_g, w_uq, kv_norm_g, w_ukv, ab_w_out, conv_w_in, conv_w, conv_w_out, final_norm_g, loss_target, m_c_ctx, m_norm_g, m_w_mod, m_b_mod, m_ffn_w_gate, m_ffn_w_up, m_ffn_w_down, m_ab_w_in, m_pool_w, m_pool_scale, m_q_norm_g, m_w_uq, m_kv_norm_g, m_w_ukv, m_ab_w_out, m_conv_w_in, m_conv_w, m_conv_w_out, m_final_norm_g, v_c_ctx, v_norm_g, v_w_mod, v_b_mod, v_ffn_w_gate, v_ffn_w_up, v_ffn_w_down, v_ab_w_in, v_pool_w, v_pool_scale, v_q_norm_g, v_w_uq, v_kv_norm_g, v_w_ukv, v_ab_w_out, v_conv_w_in, v_conv_w, v_conv_w_out, v_final_norm_g):
    weights = dict(c_ctx=c_ctx, norm_g=norm_g, w_mod=w_mod, b_mod=b_mod, ffn_w_gate=ffn_w_gate, ffn_w_up=ffn_w_up,
                   ffn_w_down=ffn_w_down, ab_w_in=ab_w_in, pool_w=pool_w, pool_scale=pool_scale, q_norm_g=q_norm_g,
                   w_uq=w_uq, kv_norm_g=kv_norm_g, w_ukv=w_ukv, ab_w_out=ab_w_out, conv_w_in=conv_w_in, conv_w=conv_w,
                   conv_w_out=conv_w_out, final_norm_g=final_norm_g)
    mom_m = dict(c_ctx=m_c_ctx, norm_g=m_norm_g, w_mod=m_w_mod, b_mod=m_b_mod, ffn_w_gate=m_ffn_w_gate,
                 ffn_w_up=m_ffn_w_up, ffn_w_down=m_ffn_w_down, ab_w_in=m_ab_w_in, pool_w=m_pool_w,
                 pool_scale=m_pool_scale, q_norm_g=m_q_norm_g, w_uq=m_w_uq, kv_norm_g=m_kv_norm_g, w_ukv=m_w_ukv,
                 ab_w_out=m_ab_w_out, conv_w_in=m_conv_w_in, conv_w=m_conv_w, conv_w_out=m_conv_w_out,
                 final_norm_g=m_final_norm_g)
    mom_v = dict(c_ctx=v_c_ctx, norm_g=v_norm_g, w_mod=v_w_mod, b_mod=v_b_mod, ffn_w_gate=v_ffn_w_gate,
                 ffn_w_up=v_ffn_w_up, ffn_w_down=v_ffn_w_down, ab_w_in=v_ab_w_in, pool_w=v_pool_w,
                 pool_scale=v_pool_scale, q_norm_g=v_q_norm_g, w_uq=v_w_uq, kv_norm_g=v_kv_norm_g, w_ukv=v_w_ukv,
                 ab_w_out=v_ab_w_out, conv_w_in=v_conv_w_in, conv_w=v_conv_w, conv_w_out=v_conv_w_out,
                 final_norm_g=v_final_norm_g)

    t_len, d = x.shape[1], x.shape[2]
    g_len = ctx.shape[1]
    r_len = t_len + g_len
    fc = ffn_w_gate.shape[3]
    heads = d // 128
    pool_dim = d // 2
    q_rank, kv_rank = q_norm_g.shape[1], kv_norm_g.shape[1]
    hw = heads * HEAD_PAD
    attn_scale = 1.0 / math.sqrt(QK_NOPE + QK_ROPE)
    kvr_w = kv_rank + HEAD_PAD
    in_w = pool_dim + q_rank + kvr_w
    tm = 256 if g_len % 256 == 0 else g_len
    assert t_len % tm == 0 and g_len % tm == 0 and t_len % g_len == 0 and pool_dim % 128 == 0
    h_tiles = t_len // tm
    tm_l0 = _pick(r_len, 768, tm)
    tm_l1 = _pick(t_len, 1024, tm)

    xi, yi, ci = lax.axis_index("x"), lax.axis_index("y"), lax.axis_index("c")
    me = 4 * xi + 2 * yi + ci
    shard = 2 * xi + yi

    def halves(a2d):
        return a2d.astype(BF).reshape(2, a2d.shape[0] // 2, a2d.shape[1])

    packs = [
        jnp.stack([ffn_w_gate, ffn_w_up], axis=2).reshape(8 * d, fc),
        jnp.concatenate([ffn_w_down.reshape(4 * fc, d), ab_w_out[0], conv_w_out[0]], axis=0),
        w_uq[0], w_ukv[0], ab_w_in[0], conv_w_in[0],
    ]
    gathered = _gather_halves("gather_weights", [halves(p) for p in packs])
    gathered = [g.reshape(N_SHARD, p.shape[0], p.shape[1]) for g, p in zip(gathered, packs)]
    wgu_all = gathered[0].reshape(N_SHARD, 8, d, fc)
    ffn_w = [[(wgu_all, 4 * l + 2 * f, gathered[1], 2 * l + f) for f in range(2)] for l in range(2)]
    n_out_rows = ab_w_out.shape[1]
    w_out_full = gathered[1][:, 4 * fc:4 * fc + n_out_rows].reshape(N_SHARD * n_out_rows, d)
    cw_out_full = gathered[1][:, 4 * fc + n_out_rows:].reshape(-1, d)
    w_uq_full = gathered[2].reshape(q_rank, heads * (QK_NOPE + QK_ROPE))
    w_ukv_full = gathered[3].transpose(1, 0, 2).reshape(kv_rank, heads * (QK_NOPE + V_HEAD))
    w_in_full = gathered[4].transpose(1, 0, 2).reshape(d, -1)
    cw_in_full = gathered[5].transpose(1, 0, 2).reshape(d, -1)

    wq_p = jnp.pad(w_uq_full.reshape(q_rank, heads, QK_NOPE + QK_ROPE),
                   ((0, 0), (0, 0), (0, HEAD_PAD - QK_NOPE - QK_ROPE))).reshape(q_rank, hw)
    ukv3 = w_ukv_full.reshape(kv_rank, heads, QK_NOPE + V_HEAD)
    wk_top = jnp.pad(ukv3[..., :QK_NOPE], ((0, 0), (0, 0), (0, HEAD_PAD - QK_NOPE))).reshape(kv_rank, hw)
    wv_top = jnp.pad(ukv3[..., QK_NOPE:], ((0, 0), (0, 0), (0, HEAD_PAD - V_HEAD))).reshape(kv_rank, hw)
    spread = jnp.zeros((HEAD_PAD, heads, HEAD_PAD), BF).at[
        jnp.arange(QK_ROPE)[:, None], jnp.arange(heads)[None, :], QK_NOPE + jnp.arange(QK_ROPE)[:, None]].set(1.0)
    wk_ext = jnp.concatenate([wk_top, spread.reshape(HEAD_PAD, hw)], axis=0)
    wv_ext = jnp.concatenate([wv_top, jnp.zeros((HEAD_PAD, hw), BF)], axis=0)
    w_in_pool = w_in_full[:, :pool_dim]
    w_in_q = w_in_full[:, pool_dim:pool_dim + q_rank]
    w_in_kvr = jnp.pad(w_in_full[:, pool_dim + q_rank:], ((0, 0), (0, HEAD_PAD - QK_ROPE)))
    w_out_attn = jnp.pad(w_out_full[pool_dim:].reshape(heads, V_HEAD, d),
                         ((0, 0), (0, HEAD_PAD - V_HEAD), (0, 0))).reshape(hw, d)
    w_out_p = jnp.concatenate([w_out_full[:pool_dim], w_out_attn], axis=0)

    small = jnp.concatenate([norm_g.reshape(6, -1), conv_w[0]], axis=0)
    small = jnp.pad(small, ((0, 7), (0, 0)))
    c_row = jnp.pad(c, ((0, 7), (0, 0)))
    small_all, c_all = _gather_all("gather_small", [small, c_row])
    small_full = small_all[::2].transpose(1, 0, 2).reshape(16, d)
    gains = [jnp.pad(small_full[3 * l:3 * l + 3], ((0, 5), (0, 0))) for l in range(2)]
    conv_w_full = small_full[6:9]
    c16 = jnp.concatenate([c_all[:, 0], c_ctx[None], jnp.zeros((7, d), F32)], axis=0)

    n_col = w_mod.shape[2]
    b_sh = lax.dynamic_slice_in_dim(b_mod, shard * n_col, n_col, axis=1)
    m_sh = [_mm(f"mod_fwd_{l}", c16, w_mod[l], 'nn', tm=16, tn=768, a_pre=_silu,
                epi=lambda acc, i, bv: (acc + bv,), epi_args=(b_sh[l:l + 1],), epi_kinds=('n',)) for l in range(2)]
    m_all = _gather_all("gather_mod", [jnp.concatenate(m_sh, axis=0)])[0]
    m_full = m_all[::2].reshape(N_SHARD, 2, 16, n_col).transpose(1, 2, 0, 3).reshape(2, 16, N_MOD * d)
    mod_h = [jnp.pad(lax.dynamic_index_in_dim(m_full[l], me, 0, keepdims=False).reshape(N_MOD, d), ((0, 7), (0, 0)))
             for l in range(2)]
    mod_g0 = jnp.pad(m_full[0, 8].reshape(N_MOD, d), ((0, 7), (0, 0)))
    mods = [jnp.stack([mod_h[0], mod_g0]), mod_h[1][None]]

    s0 = jnp.concatenate([x[0], ctx[0]], axis=0)
    s1, sav_f00 = _ffn_half_fwd("l0a", s0, gains[0], mods[0], 0, ffn_w[0][0], t_len, tm, h_tiles, tm_l0)
    u_mix = _adaln_fwd("adaln_l0m", s1, gains[0], 1, mods[0], 1, tm, h_tiles)
    p_pool = _mm("in_pool", u_mix, w_in_pool, 'nn', tm=tm, tn=pool_dim)
    p_q = _mm("in_q", u_mix, w_in_q, 'nn', tm=tm, tn=q_rank)
    p_kvr = _mm("in_kvr", u_mix, w_in_kvr, 'nn', tm=tm, tn=kvr_w)
    qg = jnp.pad(q_norm_g, ((0, 7), (0, 0)))
    kvg = jnp.pad(kv_norm_g, ((0, 7), (0, 0)))
    tq_c, tq_s = _rope_tables(t_len, g_len, QK_NOPE)
    tk_c, tk_s = _rope_tables(t_len, g_len, 0)

    def qn_fn(rows, consts, m):
        n, _ = _rms(rows[0])
        return [n * consts[0][0:1, :]], {}

    qn = _rows_call("q_norm", qn_fn, r_len, tm, [p_q], [qg], None, [(q_rank, BF)])[0]
    q_r = _mm("q_up", qn, wq_p, 'nn', tm=tm, tn=hw, out_dtypes=(BF,),
              epi=lambda acc, i, ct, st: (_rope(acc, ct, st),), epi_args=(tq_c, tq_s), epi_kinds=('mt', 'mt'))

    def kvn_fn(rows, consts, m):
        pv, ct, st = rows
        n, _ = _rms(pv[:, :kv_rank])
        return [jnp.concatenate([n * consts[0][0:1, :], _rope(pv[:, kv_rank:], ct, st)], axis=1)], {}

    kvn = _rows_call("kv_norm", kvn_fn, r_len, tm, [p_kvr, tk_c, tk_s], [kvg], None, [(kvr_w, BF)])[0]
    k_p = _mm("k_up", kvn, wk_ext, 'nn', tm=tm, tn=hw, out_dtypes=(BF,))
    v_p = _mm("v_up", kvn, wv_ext, 'nn', tm=tm, tn=hw, out_dtypes=(BF,))
    o_h, lse_h = _attn_fwd("attn_h", q_r, k_p, v_p, t_len, 0, r_len, 0, heads, tm, attn_scale)
    o_g, lse_g = _attn_fwd("attn_g", q_r, k_p, v_p, g_len, t_len, g_len, t_len // g_len, heads, tm, attn_scale)
    y_h = _pool_fwd("pool_h", p_pool[:t_len], pool_w[0], pool_scale)
    y_g = _pool_fwd("pool_g", p_pool[t_len:], pool_w[0], pool_scale)
    cat = jnp.concatenate([jnp.concatenate([y_h, y_g], axis=0), jnp.concatenate([o_h, o_g], axis=0)], axis=1)

    def resid_epi(k3, n_lat, tmr):
        def epi(acc, i, sv, mv):
            return sv + _row_gate(mv, k3, i, tmr, n_lat) * acc, acc
        return epi

    s2, o_mix0 = _mm("mix_out_l0", cat, w_out_p, 'nn', tm=tm, tn=d, out_dtypes=(F32, F32),
                     epi=resid_epi(5, t_len, tm), epi_args=(s1, mods[0]), epi_kinds=('mn', 'w'))
    s3, sav_f01 = _ffn_half_fwd("l0b", s2, gains[0], mods[0], 2, ffn_w[0][1], t_len, tm, h_tiles, tm_l0)

    tml = 256 if t_len % 256 == 0 else tm
    h3 = s3[:t_len]
    h4, sav_f10 = _ffn_half_fwd("l1a", h3, gains[1], mods[1], 0, ffn_w[1][0], t_len, tml, None, tm_l1)
    u_cv = _adaln_fwd("adaln_l1m", h4, gains[1], 1, mods[1], 1, tml, None)
    p3 = _mm("conv_in", u_cv, cw_in_full, 'nn', tm=tml, tn=512)
    cwp = conv_w_full
    tc = _pick(d, 256)
    y_cv = _conv_fwd("conv_fwd", p3, cwp, tc)
    h5, o_mix1 = _mm("mix_out_l1", y_cv, cw_out_full, 'nn', tm=tml, tn=d, out_dtypes=(F32, F32),
                     epi=resid_epi(5, t_len, tml), epi_args=(h4, mods[1]), epi_kinds=('mn', 'w'))
    h6, sav_f11 = _ffn_half_fwd("l1b", h5, gains[1], mods[1], 2, ffn_w[1][1], t_len, tml, None, tm_l1)

    fg = jnp.pad(final_norm_g[None], ((0, 7), (0, 0)))
    dh6, acc_loss = _loss_head("loss_head", h6, loss_target[0], fg, tml)
    loss = lax.psum(acc_loss[0, 1, 0], ("x", "y", "c"))
    d_final_g = acc_loss[0, 0]

    dgain = [[None] * 3 for _ in range(2)]
    dmod = [[None] * N_MOD for _ in range(2)]

    def put(l, k, triple):
        dmod[l][3 * k], dmod[l][3 * k + 1], dmod[l][3 * k + 2] = triple

    dh5, dwg11, dwu11, dwd11, tr, dgain[1][2] = _ffn_half_bwd("l1b", dh6, sav_f11, gains[1], mods[1], 2,
                                                              ffn_w[1][1], tml, None, tm_l1)
    put(1, 2, tr)
    d_o1, acc_g1 = _resid_bwd("resid_bwd_l1m", dh5, o_mix1, mods[1], 1, 1.0, tml, None)
    dy_cv = _mm("mix_out_l1_dx", d_o1, cw_out_full, 'nt', tm=tml, tn=d)
    d_cw_out = _mm("mix_out_l1_dw", y_cv, d_o1, 'tn', tm=256, tn=512)
    dp3, d_cw = _conv_bwd("conv_bwd", p3, cwp, dy_cv, tc)
    du_cv = _conv_din("conv_in_dx", dp3, cw_in_full, tml)
    d_cw_in = _conv_dw_in("conv_in_dw", u_cv, dp3, _pick(d, 256), _pick(d, 512))
    dh4, acc_n1 = _adaln_bwd("adaln_bwd_l1m", h4, du_cv, dh5, gains[1], 1, mods[1], 1, tml, None)
    put(1, 1, (acc_n1[:, 0], acc_n1[:, 1], acc_g1[:, 0]))
    dgain[1][1] = acc_n1[0, 2]
    dh3, dwg10, dwu10, dwd10, tr, dgain[1][0] = _ffn_half_bwd("l1a", dh4, sav_f10, gains[1], mods[1], 0,
                                                              ffn_w[1][0], tml, None, tm_l1)
    put(1, 0, tr)

    ds3 = jnp.concatenate([dh3, jnp.zeros((g_len, d), F32)], axis=0)
    ds2, dwg01, dwu01, dwd01, tr, dgain[0][2] = _ffn_half_bwd("l0b", ds3, sav_f01, gains[0], mods[0], 2,
                                                              ffn_w[0][1], tm, h_tiles, tm_l0)
    put(0, 2, tr)
    d_o0, acc_g0 = _resid_bwd("resid_bwd_l0m", ds2, o_mix0, mods[0], 1, 1.0, tm, h_tiles)
    dcat = _mm("mix_out_l0_dx", d_o0, w_out_p, 'nt', tm=tm, tn=pool_dim + hw)
    d_w_out_p = _mm("mix_out_l0_dw", cat, d_o0, 'tn', tm=256, tn=512)
    col_blk = pool_dim // HEAD_PAD
    dq_h, dk_h, dv_h = _attn_bwd("attn_bwd_h", q_r, k_p, v_p, cat, dcat, lse_h, t_len, 0, r_len, 0, heads, tm,
                                 attn_scale, col_blk)
    dq_g, dk_g, dv_g = _attn_bwd("attn_bwd_g", q_r, k_p, v_p, cat, dcat, lse_g, g_len, t_len, g_len,
                                 t_len // g_len, heads, tm, attn_scale, col_blk)
    dq_all = jnp.concatenate([dq_h, dq_g], axis=0)
    dk_all = dk_h.at[t_len:].add(dk_g)
    dv_all = dv_h.at[t_len:].add(dv_g)
    dkvn = _mm("k_up_dx", dk_all, wk_ext, 'nt', tm=tm, tn=kvr_w)
    dkvn = _mm("v_up_dx", dv_all, wv_ext, 'nt', tm=tm, tn=kvr_w, epi=lambda acc, i, prev: (acc + prev,),
               epi_args=(dkvn,), epi_kinds=('mn',))
    d_wk_ext = _mm("k_up_dw", kvn, dk_all, 'tn', tm=kvr_w, tn=512)
    d_wv_ext = _mm("v_up_dw", kvn, dv_all, 'tn', tm=kvr_w, tn=512)

    def kvn_bwd_fn(rows, consts, m):
        pv, dv_, ct, st = rows
        g = consts[0][0:1, :]
        n, r = _rms(pv[:, :kv_rank])
        dyn = dv_[:, :kv_rank]
        dckv = _rms_bwd(dyn * g, n, r)
        dkr = _rope_t(dv_[:, kv_rank:], ct, st)
        return [jnp.concatenate([dckv, dkr], axis=1)], {0: jnp.sum(dyn * n, axis=0, keepdims=True)}

    dp_kvr, acc_kvg = _rows_call("kv_norm_bwd", kvn_bwd_fn, r_len, tm, [p_kvr, dkvn, tk_c, tk_s], [kvg], None,
                                 [(kvr_w, BF)], acc_w=kv_rank)

    def qrope_bwd_fn(rows, consts, m):
        return [_rope_t(rows[0], rows[1], rows[2])], {}

    dq_pad = _rows_call("q_rope_bwd", qrope_bwd_fn, r_len, tm, [dq_all, tq_c, tq_s], [], None, [(hw, BF)])[0]
    dqn = _mm("q_up_dx", dq_pad, wq_p, 'nt', tm=tm, tn=q_rank)
    d_wq_p = _mm("q_up_dw", qn, dq_pad, 'tn', tm=256, tn=512)

    def qn_bwd_fn(rows, consts, m):
        pv, dv_ = rows
        g = consts[0][0:1, :]
        n, r = _rms(pv)
        return [_rms_bwd(dv_ * g, n, r)], {0: jnp.sum(dv_ * n, axis=0, keepdims=True)}

    dp_q, acc_qg = _rows_call("q_norm_bwd", qn_bwd_fn, r_len, tm, [p_q, dqn], [qg], None, [(q_rank, BF)],
                              acc_w=q_rank)
    dpu_h, dpw_h, dps_h = _pool_bwd("pool_bwd_h", p_pool[:t_len], dcat, pool_w[0], pool_scale, 0)
    dpu_g, dpw_g, dps_g = _pool_bwd("pool_bwd_g", p_pool[t_len:], dcat, pool_w[0], pool_scale, t_len)
    dp_pool = jnp.concatenate([dpu_h, dpu_g], axis=0)
    add_prev = lambda acc, i, prev: (acc + prev,)
    du_mix = _mm("in_pool_dx", dp_pool, w_in_pool, 'nt', tm=tm, tn=d)
    du_mix = _mm("in_q_dx", dp_q, w_in_q, 'nt', tm=tm, tn=d, epi=add_prev, epi_args=(du_mix,), epi_kinds=('mn',))
    du_mix = _mm("in_kvr_dx", dp_kvr, w_in_kvr, 'nt', tm=tm, tn=d, epi=add_prev, epi_args=(du_mix,), epi_kinds=('mn',))
    d_w_in = jnp.concatenate([
        _mm("in_pool_dw", u_mix, dp_pool, 'tn', tm=256, tn=pool_dim),
        _mm("in_q_dw", u_mix, dp_q, 'tn', tm=256, tn=q_rank),
        _mm("in_kvr_dw", u_mix, dp_kvr, 'tn', tm=256, tn=kvr_w)[:, :kv_rank + QK_ROPE]], axis=1)
    ds1, acc_n0 = _adaln_bwd("adaln_bwd_l0m", s1, du_mix, ds2, gains[0], 1, mods[0], 1, tm, h_tiles)
    put(0, 1, (acc_n0[:, 0], acc_n0[:, 1], acc_g0[:, 0]))
    dgain[0][1] = jnp.sum(acc_n0[:, 2], axis=0)
    ds0, dwg00, dwu00, dwd00, tr, dgain[0][0] = _ffn_half_bwd("l0a", ds1, sav_f00, gains[0], mods[0], 0,
                                                              ffn_w[0][0], tm, h_tiles, tm_l0)
    put(0, 0, tr)
    grad_x = ds0[:t_len][None]

    d_w_uq = d_wq_p.reshape(q_rank, heads, HEAD_PAD)[..., :QK_NOPE + QK_ROPE].reshape(q_rank, -1)
    d_w_ukv = jnp.concatenate([d_wk_ext[:kv_rank].reshape(kv_rank, heads, HEAD_PAD)[..., :QK_NOPE],
                               d_wv_ext[:kv_rank].reshape(kv_rank, heads, HEAD_PAD)[..., :V_HEAD]],
                              axis=-1).reshape(kv_rank, -1)
    d_w_out = jnp.concatenate([d_w_out_p[:pool_dim],
                               d_w_out_p[pool_dim:].reshape(heads, HEAD_PAD, d)[:, :V_HEAD].reshape(-1, d)], axis=0)

    dmh = jnp.stack([jnp.stack([dmod[l][k][0] for k in range(N_MOD)]) for l in range(2)])
    dmg0 = jnp.stack([dmod[0][k][1] for k in range(N_MOD)])
    dg_rows = jnp.stack([dgain[l][k] for l in range(2) for k in range(3)])
    pieces = [dmh.reshape(2 * N_MOD, d), dmg0, dg_rows, d_cw[:3], d_final_g[None],
              (dpw_h + dpw_g).reshape(-1, d), jnp.pad((dps_h + dps_g)[0], (0, d - pool_dim))[None],
              jnp.pad(acc_qg[0, 0], (0, d - q_rank))[None], jnp.pad(acc_kvg[0, 0], (0, d - kv_rank))[None]]
    n_piece = [p.shape[0] for p in pieces]
    small_g = jnp.concatenate(pieces, axis=0)
    n_small = small_g.shape[0]
    small_g = jnp.pad(small_g, ((0, (-n_small) % 8), (0, 0)))
    sg_all = _gather_all("gather_small_grads", [small_g])[0]
    sg_sum = _sum_lead("sum_small_grads", sg_all)
    offs = [0]
    for npc in n_piece:
        offs.append(offs[-1] + npc)
    part = lambda j: sg_sum[offs[j]:offs[j + 1]]
    sum_dmh, sum_dmg0, g_norm_full, g_conv_w_full = part(0).reshape(2, N_MOD * d), part(1).reshape(N_MOD * d), part(2), part(3)
    g_final = part(4)[0]
    g_pool_w = part(5).reshape(pool_w.shape)
    g_pool_scale = part(6)[:, :pool_dim]
    g_q_norm = part(7)[:, :q_rank]
    g_kv_norm = part(8)[:, :kv_rank]
    col0 = shard * (d // N_SHARD)
    g_norm_g = lax.dynamic_slice_in_dim(g_norm_full.reshape(2, 3, d), col0, d // N_SHARD, axis=2)
    g_conv_w = lax.dynamic_slice_in_dim(g_conv_w_full, col0, d // N_SHARD, axis=1)[None]
    g_b_mod = _sum_lead("sum_b_mod", jnp.stack([sum_dmh, jnp.stack([sum_dmg0, jnp.zeros_like(sum_dmg0)])]))

    dm16 = []
    for l in range(2):
        per_dev = sg_all[:, l * N_MOD:(l + 1) * N_MOD].reshape(N_DEV, N_MOD * d)
        row8 = sum_dmg0 if l == 0 else jnp.zeros_like(sum_dmg0)
        full = jnp.concatenate([per_dev, row8[None], jnp.zeros((7, N_MOD * d), F32)], axis=0)
        dm16.append(lax.dynamic_slice_in_dim(full, shard * n_col, n_col, axis=1))
    g_w_mod = jnp.stack([_mm(f"mod_dw_{l}", c16, dm16[l], 'tn', tm=256, tn=768, a_pre=_silu) for l in range(2)])
    dc16 = _mm("mod_dx", dm16[0], w_mod[0], 'nt', tm=16, tn=512, epi=lambda acc, i, cv: (acc * _dsilu(cv),),
               epi_args=(c16,), epi_kinds=('mn',))
    dc_all = _gather_all("gather_dc", [dc16])[0]
    g_c_ctx = _sum_lead("sum_dc", dc_all[::2])[8]

    def chunks704(l):
        return [[dwg00, dwu00], [dwg01, dwu01]] if l == 0 else [[dwg10, dwu10], [dwg11, dwu11]]

    g704 = jnp.stack([jnp.stack([jnp.stack(chunks704(l)[f]) for f in range(2)]) for l in range(2)])
    g704 = g704.transpose(3, 0, 1, 2, 4, 5).reshape(N_SHARD, 8 * d, fc)
    gdown = jnp.stack([jnp.stack([dwd00, dwd01]), jnp.stack([dwd10, dwd11])])
    gdown = gdown.transpose(2, 0, 1, 3, 4).reshape(N_SHARD, 4 * fc, d)
    g1024 = jnp.concatenate([gdown, d_w_out.reshape(N_SHARD, -1, d), d_cw_out.reshape(N_SHARD, -1, d)], axis=1)
    big = [g704, g1024, d_w_uq.reshape(N_SHARD, -1, d_w_uq.shape[1]),
           d_w_ukv.reshape(kv_rank, N_SHARD, -1).transpose(1, 0, 2),
           d_w_in.reshape(d, N_SHARD, -1).transpose(1, 0, 2),
           d_cw_in.reshape(d, N_SHARD, -1).transpose(1, 0, 2)]
    send = [b.astype(BF).reshape(N_DEV, b.shape[1] // 2, b.shape[2]) for b in big]
    both = _to_sibling_halves("grads_to_sibling", send)
    pre = [_sum_lead(f"presum_grads_{j}", b.reshape(2, N_SHARD * b.shape[1], b.shape[2]), BF)
           .reshape(N_SHARD, b.shape[1], b.shape[2]) for j, b in enumerate(both)]
    landed = _to_chips("grads_to_chips", pre)
    halves_sum = [_sum_lead(f"sum_grads_{j}", l) for j, l in enumerate(landed)]
    swapped = _swap_sibling("swap_halves", halves_sum)
    gsh = [s.reshape(2 * s.shape[1], s.shape[2]) for s in swapped]
    g_gu = gsh[0].reshape(2, 2, 2, d, fc)
    grads = dict(
        c_ctx=g_c_ctx, norm_g=g_norm_g, w_mod=g_w_mod, b_mod=g_b_mod,
        ffn_w_gate=g_gu[:, :, 0], ffn_w_up=g_gu[:, :, 1], ffn_w_down=gsh[1][:4 * fc].reshape(2, 2, fc, d),
        ab_w_in=gsh[4][None], pool_w=g_pool_w, pool_scale=g_pool_scale, q_norm_g=g_q_norm, w_uq=gsh[2][None],
        kv_norm_g=g_kv_norm, w_ukv=gsh[3][None], ab_w_out=gsh[1][4 * fc:4 * fc + n_out_rows][None],
        conv_w_in=gsh[5][None], conv_w=g_conv_w, conv_w_out=gsh[1][4 * fc + n_out_rows:][None], final_norm_g=g_final)

    names = list(weights)
    upd = {n: _adamw(f"adamw_{n}", weights[n], grads[n].reshape(weights[n].shape), mom_m[n], mom_v[n]) for n in names}
    return (loss, grad_x, *[grads[n].reshape(weights[n].shape) for n in names], *[upd[n][0] for n in names],
            *[upd[n][1] for n in names], *[upd[n][2] for n in names])
```

```python
import functools
import math

import jax
import jax.numpy as jnp
from jax import lax
from jax.experimental import pallas as pl
from jax.experimental.pallas import tpu as pltpu

F32 = jnp.float32
BF = jnp.bfloat16
MESH = pl.DeviceIdType.MESH

N_DEV = 8
N_SHARD = 4
RMS_EPS = 1e-6
N_MOD = 9
POOL_WINDOWS = (2, 4, 8, 16)
QK_NOPE = 64
QK_ROPE = 32
V_HEAD = 64
HEAD_PAD = 128
GRID_W = 64
ROPE_THETA = 10000.0
POOL_PAD = 16
ADAM_LR, ADAM_B1, ADAM_B2, ADAM_EPS, ADAM_WD, ADAM_STEP = 0.001, 0.9, 0.999, 1e-08, 0.01, 10
VMEM_LIMIT = 56 * 1024 * 1024


def _pcall(body, **kw):
    return pl.pallas_call(body, **kw)


def _params(sem=None):
    return pltpu.CompilerParams(dimension_semantics=sem, vmem_limit_bytes=VMEM_LIMIT)


def _pick(n, pref, mult=128):
    best = None
    d = mult
    while d <= min(n, pref):
        if n % d == 0:
            best = d
        d += mult
    return best if best is not None else n


def _silu(z):
    return z * jax.nn.sigmoid(z)


def _dsilu(z):
    s = jax.nn.sigmoid(z)
    return s * (1.0 + z * (1.0 - s))


def _dot(a, b, dims):
    return lax.dot_general(a.astype(BF), b.astype(BF), (dims, ((), ())), preferred_element_type=F32)


NN = ((1,), (0,))
NT = ((1,), (1,))
TN = ((0,), (0,))


ALL_FLIPS = [(kx, ky, kc) for kx in (0, 1) for ky in (0, 1) for kc in (0, 1) if (kx, ky, kc) != (0, 0, 0)]
CHIP_FLIPS = [(1, 0, 0), (0, 1, 0), (1, 1, 0)]
SIBLING = (0, 0, 1)
COMM_SPLIT = 8
SPLIT_MIN_ROWS = 256


def _exchange(name, arrays, plan, lead, whole_src, split=COMM_SPLIT):
    n = len(arrays)
    blk_shapes = [tuple(a.shape) if whole_src else tuple(a.shape[1:]) for a in arrays]
    splits = []
    for shp in blk_shapes:
        s = 1
        while s * 2 <= split and shp[0] % (s * 2) == 0 and (shp[0] // (s * 2)) % 16 == 0 \
                and shp[0] // (s * 2) >= SPLIT_MIN_ROWS:
            s *= 2
        splits.append(s)
    items = plan(0, 0, 0)
    n_items = len(items)
    remote_ids = [k for k, it in enumerate(items) if it[0] is not None]
    local_ids = [k for k, it in enumerate(items) if it[0] is None]
    slots = [(a, s) for s in range(max(splits)) for a in range(n) if s < splits[a]]
    n_slot = len(slots)

    def body(*refs):
        ins, outs = refs[:n], refs[n:2 * n]
        send_sems, recv_sems, loc_sems = refs[2 * n:]
        x, y, c = lax.axis_index("x"), lax.axis_index("y"), lax.axis_index("c")
        plan_here = plan(x, y, c)

        def rows(ref, a, s):
            rc = blk_shapes[a][0] // splits[a]
            return ref.at[pl.ds(s * rc, rc)]

        def make(si, k):
            a, s = slots[si]
            flip, src, dst, _ = plan_here[k]
            base = outs[a] if src[0] == 'out' else ins[a]
            src_ref = rows(base if src[1] is None else base.at[src[1]], a, s)
            dst_ref = rows(outs[a].at[dst], a, s)
            if flip is None:
                return pltpu.make_async_copy(src_ref, dst_ref, loc_sems.at[si * max(1, len(local_ids)) + local_ids.index(k)])
            peer = (1 - x if flip[0] else x, 1 - y if flip[1] else y, 1 - c if flip[2] else c)
            sem = si * len(remote_ids) + remote_ids.index(k)
            return pltpu.make_async_remote_copy(src_ref=src_ref, dst_ref=dst_ref, send_sem=send_sems.at[sem],
                                                recv_sem=recv_sems.at[sem], device_id=peer, device_id_type=MESH)

        copies = {}
        for si in range(n_slot):
            for k in range(n_items):
                if plan_here[k][3] is None:
                    copies[si, k] = make(si, k)
                    copies[si, k].start()
        arrived = set()
        for si in range(n_slot):
            for k in range(n_items):
                after = plan_here[k][3]
                if after is not None:
                    if (si, after) not in arrived:
                        copies[si, after].wait_recv()
                        arrived.add((si, after))
                    copies[si, k] = make(si, k)
                    copies[si, k].start()
        for (si, k), cp in copies.items():
            if plan_here[k][0] is None:
                cp.wait()
            else:
                cp.wait_send()
                if (si, k) not in arrived:
                    cp.wait_recv()

    any_spec = pl.BlockSpec(memory_space=pl.ANY)
    n_rem = max(1, n_slot * len(remote_ids))
    outs = _pcall(
        body, name=name,
        out_shape=[jax.ShapeDtypeStruct((lead,) + s, a.dtype) for s, a in zip(blk_shapes, arrays)],
        in_specs=[any_spec] * n, out_specs=[any_spec] * n,
        scratch_shapes=[pltpu.SemaphoreType.DMA((n_rem,)), pltpu.SemaphoreType.DMA((n_rem,)),
                        pltpu.SemaphoreType.DMA((max(1, n_slot * len(local_ids)),))],
    )(*arrays)
    return list(outs)


def _place(x, y, c):
    return 4 * x + 2 * y + c


def _flip(v, f):
    return 1 - v if f else v


def _gather_all(name, arrays):
    def plan(x, y, c):
        me = _place(x, y, c)
        return [(None, ('in', None), me, None)] + [(f, ('in', None), me, None) for f in ALL_FLIPS]
    return _exchange(name, arrays, plan, N_DEV, True)


STREAM_SLOTS = 4
STREAM_LAG = 2
STREAM_CHUNK_BYTES = 420 * 1024


def _stream_rows(rows, cdim, itemsize):
    best = None
    for rc in range(16, rows + 1, 16):
        if rows % rc == 0 and rc * cdim * itemsize <= STREAM_CHUNK_BYTES:
            best = rc
    return rows if best is None else best


def _stream_scratch(chunk_specs, combine):
    scratch = []
    for rc, cdim, dt in chunk_specs:
        scratch += [pltpu.VMEM((STREAM_SLOTS, rc, cdim), dt), pltpu.VMEM((STREAM_SLOTS, rc, cdim), dt)]
        if combine:
            scratch += [pltpu.VMEM((STREAM_SLOTS, rc, cdim), dt), pltpu.VMEM((STREAM_SLOTS, rc, cdim), BF)]
    n = len(chunk_specs)
    scratch += [pltpu.SemaphoreType.DMA((n * STREAM_SLOTS,)) for _ in range(5)]
    scratch.append(pltpu.SemaphoreType.REGULAR((n,)))
    return scratch


def _run_stream(peer, chunks, bufs, sems, a, combine):
    ns, lag, k_all = STREAM_SLOTS, STREAM_LAG, len(chunks)
    load_s, send_s, recv_s, store_s, own_s, credits = sems
    credit = credits.at[a]
    if combine:
        send_buf, recv_buf, own_buf, res_buf = bufs
    else:
        send_buf, recv_buf = bufs

    def sem(ref, i):
        return ref.at[a * ns + i % ns]

    def load(i):
        return pltpu.make_async_copy(chunks[i]['src'], send_buf.at[i % ns], sem(load_s, i))

    def own_load(i):
        return pltpu.make_async_copy(chunks[i]['own'], own_buf.at[i % ns], sem(own_s, i))

    def remote(i):
        return pltpu.make_async_remote_copy(src_ref=send_buf.at[i % ns], dst_ref=recv_buf.at[i % ns],
                                            send_sem=sem(send_s, i), recv_sem=sem(recv_s, i),
                                            device_id=peer, device_id_type=MESH)

    def store(i):
        return pltpu.make_async_copy((res_buf if combine else recv_buf).at[i % ns], chunks[i]['dst'], sem(store_s, i))

    def before(i):
        if chunks[i].get('pre') is not None:
            chunks[i]['pre']()

    before(0)
    load(0).start()
    if combine:
        own_load(0).start()
    for i in range(k_all + lag):
        if i < k_all:
            load(i).wait()
            if i >= ns:
                pl.semaphore_wait(credit, 1)
            remote(i).start()
            if i + 1 < k_all:
                if i + 1 >= ns:
                    remote(i + 1 - ns).wait_send()
                before(i + 1)
                load(i + 1).start()
        r = i - lag
        if r >= 0:
            remote(r).wait_recv()
            if r >= 1:
                store(r - 1).wait()
                if r - 1 + ns < k_all:
                    pl.semaphore_signal(credit, inc=1, device_id=peer, device_id_type=MESH)
            if combine:
                own_load(r).wait()
                res_buf[r % ns] = (recv_buf[r % ns].astype(F32) + own_buf[r % ns].astype(F32)).astype(BF)
                if r + 1 < k_all:
                    own_load(r + 1).start()
            store(r).start()
    store(k_all - 1).wait()
    for i in range(max(0, k_all - ns), k_all):
        remote(i).wait_send()


def _gather_halves(name, arrays):
    n = len(arrays)
    blk = [tuple(a.shape[1:]) for a in arrays]
    splits = []
    for shp in blk:
        s = 1
        while s * 2 <= COMM_SPLIT and shp[0] % (s * 2) == 0 and (shp[0] // (s * 2)) % 16 == 0 \
                and shp[0] // (s * 2) >= SPLIT_MIN_ROWS:
            s *= 2
        splits.append(s)
    rcs = [_stream_rows(shp[0] // s, shp[1], a.dtype.itemsize) for shp, s, a in zip(blk, splits, arrays)]
    slots = [(a, s) for a in range(n) for s in range(splits[a])]
    nch = len(CHIP_FLIPS)

    def body(*refs):
        ins, outs = refs[:n], refs[n:2 * n]
        bufs = refs[2 * n:4 * n]
        sems = refs[4 * n:4 * n + 6]
        ici_send, ici_recv = refs[4 * n + 6:]
        x, y, c = lax.axis_index("x"), lax.axis_index("y"), lax.axis_index("c")
        chip = 2 * x + y
        ici = {}
        for si, (a, s) in enumerate(slots):
            rows = pl.ds(s * (blk[a][0] // splits[a]), blk[a][0] // splits[a])
            for j, f in enumerate(CHIP_FLIPS):
                cp = pltpu.make_async_remote_copy(
                    src_ref=ins[a].at[c, rows], dst_ref=outs[a].at[2 * chip + c, rows],
                    send_sem=ici_send.at[si * nch + j], recv_sem=ici_recv.at[si * nch + j],
                    device_id=(_flip(x, f[0]), _flip(y, f[1]), c), device_id_type=MESH)
                cp.start()
                ici[a, s, j] = cp
        for a in range(n):
            chunks = []
            per = blk[a][0] // splits[a]
            for s in range(splits[a]):
                for j, f in enumerate(CHIP_FLIPS):
                    other = 2 * (2 * _flip(x, f[0]) + _flip(y, f[1]))
                    for k in range(per // rcs[a]):
                        rows = pl.ds(s * per + k * rcs[a], rcs[a])
                        chunks.append(dict(src=outs[a].at[other + c, rows], dst=outs[a].at[other + 1 - c, rows],
                                           pre=ici[a, s, j].wait_recv if k == 0 else None))
            _run_stream((x, y, 1 - c), chunks, bufs[2 * a:2 * a + 2], sems, a, False)
        for cp in ici.values():
            cp.wait_send()

    any_spec = pl.BlockSpec(memory_space=pl.ANY)
    scratch = _stream_scratch([(rc, shp[1], a.dtype) for rc, shp, a in zip(rcs, blk, arrays)], False)
    scratch += [pltpu.SemaphoreType.DMA((len(slots) * nch,)), pltpu.SemaphoreType.DMA((len(slots) * nch,))]
    outs = _pcall(body, name=name,
                  out_shape=[jax.ShapeDtypeStruct((N_DEV,) + shp, a.dtype) for shp, a in zip(blk, arrays)],
                  in_specs=[any_spec] * n, out_specs=[any_spec] * n, scratch_shapes=scratch,
                  compiler_params=_params())(*arrays)
    first = 2 * (2 * lax.axis_index("x") + lax.axis_index("y"))
    return [lax.dynamic_update_slice_in_dim(o, a, first, 0) for o, a in zip(outs, arrays)]


def _presum_sibling(name, arrays):
    n = len(arrays)
    blk = [tuple(a.shape[1:]) for a in arrays]
    rcs = [_stream_rows(shp[0], shp[1], a.dtype.itemsize) for shp, a in zip(blk, arrays)]

    def body(*refs):
        ins, outs = refs[:n], refs[n:2 * n]
        bufs = refs[2 * n:6 * n]
        sems = refs[6 * n:6 * n + 6]
        x, y, c = lax.axis_index("x"), lax.axis_index("y"), lax.axis_index("c")
        for a in range(n):
            chunks = []
            for sh in range(N_SHARD):
                for k in range(blk[a][0] // rcs[a]):
                    rows = pl.ds(k * rcs[a], rcs[a])
                    chunks.append(dict(src=ins[a].at[2 * sh + 1 - c, rows], own=ins[a].at[2 * sh + c, rows],
                                       dst=outs[a].at[sh, rows]))
            _run_stream((x, y, 1 - c), chunks, bufs[4 * a:4 * a + 4], sems, a, True)

    any_spec = pl.BlockSpec(memory_space=pl.ANY)
    scratch = _stream_scratch([(rc, shp[1], a.dtype) for rc, shp, a in zip(rcs, blk, arrays)], True)
    outs = _pcall(body, name=name,
                  out_shape=[jax.ShapeDtypeStruct((N_SHARD,) + shp, BF) for shp in blk],
                  in_specs=[any_spec] * n, out_specs=[any_spec] * n, scratch_shapes=scratch,
                  compiler_params=_params())(*arrays)
    return list(outs)


def _to_chips(name, arrays):
    def plan(x, y, c):
        return [(f, ('in', 2 * _flip(x, f[0]) + _flip(y, f[1])), 2 * x + y, None) for f in CHIP_FLIPS]
    outs = _exchange(name, arrays, plan, N_SHARD, False)
    chip = 2 * lax.axis_index("x") + lax.axis_index("y")
    return [lax.dynamic_update_slice_in_dim(o, lax.dynamic_slice_in_dim(a, chip, 1, 0), chip, 0)
            for o, a in zip(outs, arrays)]


def _swap_sibling(name, arrays):
    n = len(arrays)
    rcs = [_stream_rows(a.shape[0], a.shape[1], a.dtype.itemsize) for a in arrays]

    def body(*refs):
        ins, outs = refs[:n], refs[n:2 * n]
        bufs = refs[2 * n:4 * n]
        sems = refs[4 * n:4 * n + 6]
        x, y, c = lax.axis_index("x"), lax.axis_index("y"), lax.axis_index("c")
        for a in range(n):
            chunks = [dict(src=ins[a].at[pl.ds(k * rcs[a], rcs[a])], dst=outs[a].at[pl.ds(k * rcs[a], rcs[a])])
                      for k in range(arrays[a].shape[0] // rcs[a])]
            _run_stream((x, y, 1 - c), chunks, bufs[2 * a:2 * a + 2], sems, a, False)

    any_spec = pl.BlockSpec(memory_space=pl.ANY)
    scratch = _stream_scratch([(rc, a.shape[1], a.dtype) for rc, a in zip(rcs, arrays)], False)
    got = _pcall(body, name=name, out_shape=[jax.ShapeDtypeStruct(a.shape, a.dtype) for a in arrays],
                 in_specs=[any_spec] * n, out_specs=[any_spec] * n, scratch_shapes=scratch,
                 compiler_params=_params())(*arrays)
    south = lax.axis_index("c") == 0
    return [jnp.where(south, jnp.stack([a, g]), jnp.stack([g, a])) for a, g in zip(arrays, got)]


def _sum_lead(name, arr, out_dtype=F32):
    n, r, cdim = arr.shape
    tr = r
    limit = (4 << 20) // (n * cdim * arr.dtype.itemsize)
    if r > limit:
        tr = _pick(r, max(limit, 16), 16)

    def body(x_ref, o_ref):
        acc = x_ref[0].astype(F32)
        for d in range(1, n):
            acc = acc + x_ref[d].astype(F32)
        o_ref[...] = acc.astype(out_dtype)

    return _pcall(body, name=name, grid=(r // tr,),
                  in_specs=[pl.BlockSpec((n, tr, cdim), lambda i: (0, i, 0))],
                  out_specs=pl.BlockSpec((tr, cdim), lambda i: (i, 0)),
                  out_shape=jax.ShapeDtypeStruct((r, cdim), out_dtype),
                  compiler_params=_params(("arbitrary",)))(arr)


def _rows_call(name, fn, n_rows, tm, rows, consts, mod, outs, acc_w=None, h_tiles=None):
    nt = n_rows // tm
    ht = nt if h_tiles is None else h_tiles
    ng = 1 if mod is None else mod.shape[0]
    n_r, n_c, n_o = len(rows), len(consts), len(outs)
    has_mod = mod is not None

    def body(*refs):
        i = pl.program_id(0)
        first = (i % ht) == 0
        row_refs, const_refs = refs[:n_r], refs[n_r:n_r + n_c]
        p = n_r + n_c
        mod_tile = refs[p][...] if has_mod else None
        p += int(has_mod)
        out_refs = refs[p:p + n_o]
        o, acc = fn([r[...] for r in row_refs], [r[...] for r in const_refs], mod_tile)
        for r, v in zip(out_refs, o):
            r[...] = v.astype(r.dtype)
        if acc_w is not None:
            acc_ref = refs[p + n_o]

            @pl.when(first)
            def _():
                acc_ref[...] = jnp.zeros_like(acc_ref)

            for k, v in acc.items():
                acc_ref[k:k + 1, :] += v

    in_specs = [pl.BlockSpec((tm, r.shape[1]), lambda i: (i, 0)) for r in rows]
    in_specs += [pl.BlockSpec(cst.shape, lambda i, nd=cst.ndim: (0,) * nd) for cst in consts]
    args = list(rows) + list(consts)
    if has_mod:
        in_specs.append(pl.BlockSpec((None,) + mod.shape[1:], lambda i: (i // ht, 0, 0)))
        args.append(mod)
    out_shape = [jax.ShapeDtypeStruct((n_rows, w), dt) for w, dt in outs]
    out_specs = [pl.BlockSpec((tm, w), lambda i: (i, 0)) for w, _ in outs]
    if acc_w is not None:
        out_shape.append(jax.ShapeDtypeStruct((ng, 8, acc_w), F32))
        out_specs.append(pl.BlockSpec((None, 8, acc_w), lambda i: (i // ht, 0, 0)))
    res = _pcall(body, name=name, grid=(nt,), in_specs=in_specs, out_specs=out_specs, out_shape=out_shape,
                 compiler_params=_params(("arbitrary",)))(*args)
    return list(res)


def _rms(s):
    r = lax.rsqrt(jnp.mean(s * s, axis=1, keepdims=True) + RMS_EPS)
    return s * r, r


def _rms_bwd(dn, n, r):
    return r * (dn - n * jnp.mean(dn * n, axis=1, keepdims=True))


def _adaln_fwd(name, s, gains, gain_row, mod, k, tm, h_tiles):
    def fn(rows, consts, m):
        n, _ = _rms(rows[0])
        y = n * consts[0][gain_row:gain_row + 1, :]
        return [y * (1.0 + m[3 * k + 1:3 * k + 2, :]) + m[3 * k:3 * k + 1, :]], {}

    d = s.shape[1]
    return _rows_call(name, fn, s.shape[0], tm, [s], [gains], mod, [(d, BF)], h_tiles=h_tiles)[0]


def _adaln_bwd(name, s, du, ds_res, gains, gain_row, mod, k, tm, h_tiles):
    def fn(rows, consts, m):
        sv, duv, res = rows
        gain = consts[0][gain_row:gain_row + 1, :]
        n, r = _rms(sv)
        y = n * gain
        dy = duv * (1.0 + m[3 * k + 1:3 * k + 2, :])
        acc = {0: jnp.sum(duv, axis=0, keepdims=True), 1: jnp.sum(duv * y, axis=0, keepdims=True),
               2: jnp.sum(dy * n, axis=0, keepdims=True)}
        return [_rms_bwd(dy * gain, n, r) + res], acc

    d = s.shape[1]
    return _rows_call(name, fn, s.shape[0], tm, [s, du, ds_res], [gains], mod, [(d, F32)], acc_w=d, h_tiles=h_tiles)


def _resid_bwd(name, ds_out, o, mod, k, cst, tm, h_tiles):
    def fn(rows, consts, m):
        dsv, ov = rows
        gate = m[3 * k + 2:3 * k + 3, :]
        return [cst * gate * dsv], {0: jnp.sum(cst * ov * dsv, axis=0, keepdims=True)}

    d = o.shape[1]
    return _rows_call(name, fn, o.shape[0], tm, [ds_out, o], [], mod, [(d, BF)], acc_w=d, h_tiles=h_tiles)


def _mm(name, a, b, mode, tm=256, tn=512, out_dtypes=(F32,), epi=None, epi_args=(), epi_kinds=(), a_pre=None):
    if mode == 'nn':
        (m, kd), nd = a.shape, b.shape[1]
    elif mode == 'nt':
        (m, kd), nd = a.shape, b.shape[0]
    else:
        (kd, m), nd = a.shape, b.shape[1]
    tm = _pick(m, tm, 16) if m % tm else tm
    tn = _pick(nd, tn, 128) if nd % tn else tn
    dims = {'nn': NN, 'nt': NT, 'tn': TN}[mode]
    n_e, n_o = len(epi_args), len(out_dtypes)

    def body(*refs):
        i = pl.program_id(1)
        av = refs[0][...]
        if a_pre is not None:
            av = a_pre(av)
        acc = _dot(av, refs[1][...], dims)
        res = (acc,) if epi is None else epi(acc, i, *[r[...] for r in refs[2:2 + n_e]])
        for r, v in zip(refs[2 + n_e:], res):
            r[...] = v.astype(r.dtype)

    if mode == 'nn':
        specs = [pl.BlockSpec((tm, kd), lambda j, i: (i, 0)), pl.BlockSpec((kd, tn), lambda j, i: (0, j))]
    elif mode == 'nt':
        specs = [pl.BlockSpec((tm, kd), lambda j, i: (i, 0)), pl.BlockSpec((tn, kd), lambda j, i: (j, 0))]
    else:
        specs = [pl.BlockSpec((kd, tm), lambda j, i: (0, i)), pl.BlockSpec((kd, tn), lambda j, i: (0, j))]
    for arr, kind in zip(epi_args, epi_kinds):
        if kind == 'mn':
            specs.append(pl.BlockSpec((tm, tn), lambda j, i: (i, j)))
        elif kind == 'n':
            specs.append(pl.BlockSpec((1, tn), lambda j, i: (0, j)))
        elif kind == 'mt':
            specs.append(pl.BlockSpec((tm, arr.shape[1]), lambda j, i: (i, 0)))
        else:
            specs.append(pl.BlockSpec(arr.shape, lambda j, i, nd_=arr.ndim: (0,) * nd_))
    res = _pcall(body, name=name, grid=(nd // tn, m // tm), in_specs=specs,
                 out_specs=[pl.BlockSpec((tm, tn), lambda j, i: (i, j))] * n_o,
                 out_shape=[jax.ShapeDtypeStruct((m, nd), dt) for dt in out_dtypes],
                 compiler_params=_params(("arbitrary", "arbitrary")))(a, b, *epi_args)
    return res[0] if n_o == 1 else list(res)


def _row_gate(mod, k3, i, tm, n_lat):
    g0 = mod[0, k3:k3 + 1, :]
    if mod.shape[0] == 1:
        return g0
    rid = i * tm + lax.broadcasted_iota(jnp.int32, (tm, 1), 0)
    return jnp.where(rid < n_lat, g0, mod[1, k3:k3 + 1, :])


def _ffn_up(name, u, wgu, base, tm):
    r, d = u.shape
    nch, _, _, fc = wgu.shape

    def body(u_ref, wg_ref, wu_ref, a_ref, b_ref, h_ref):
        uv = u_ref[...]
        a = _dot(uv, wg_ref[...], NN)
        b = _dot(uv, wu_ref[...], NN)
        a_ref[...] = a.astype(BF)
        b_ref[...] = b.astype(BF)
        h_ref[...] = (_silu(a) * b).astype(BF)

    chunk = pl.BlockSpec((None, tm, fc), lambda j, i: (j, i, 0))
    return _pcall(body, name=name, grid=(nch, r // tm),
                  in_specs=[pl.BlockSpec((tm, d), lambda j, i: (i, 0)),
                            pl.BlockSpec((None, None, d, fc), lambda j, i: (j, base, 0, 0)),
                            pl.BlockSpec((None, None, d, fc), lambda j, i: (j, base + 1, 0, 0))],
                  out_specs=[chunk] * 3, out_shape=[jax.ShapeDtypeStruct((nch, r, fc), BF)] * 3,
                  compiler_params=_params(("arbitrary", "arbitrary")))(u, wgu, wgu)


def _ffn_down(name, hid, wd, wd_blk, s, mod, k, n_lat, tm):
    nch, r, fc = hid.shape
    d = wd.shape[2]

    def body(h_ref, w_ref, s_ref, m_ref, so_ref, o_ref, acc_ref):
        i, j = pl.program_id(0), pl.program_id(1)
        part = _dot(h_ref[...], w_ref[...], NN)

        @pl.when(j == 0)
        def _():
            acc_ref[...] = part

        @pl.when(j > 0)
        def _():
            acc_ref[...] += part

        @pl.when(j == nch - 1)
        def _():
            o = acc_ref[...]
            o_ref[...] = o
            so_ref[...] = s_ref[...] + 0.5 * _row_gate(m_ref[...], 3 * k + 2, i, tm, n_lat) * o

    row = pl.BlockSpec((tm, d), lambda i, j: (i, 0))
    return _pcall(body, name=name, grid=(r // tm, nch),
                  in_specs=[pl.BlockSpec((None, tm, fc), lambda i, j: (j, i, 0)),
                            pl.BlockSpec((None, fc, d), lambda i, j: (j, wd_blk, 0)), row,
                            pl.BlockSpec(mod.shape, lambda i, j: (0, 0, 0))],
                  out_specs=[row, row], out_shape=[jax.ShapeDtypeStruct((r, d), F32)] * 2,
                  scratch_shapes=[pltpu.VMEM((tm, d), F32)],
                  compiler_params=_params(("arbitrary", "arbitrary")))(hid, wd, s, mod)


def _ffn_dhid(name, d_o, wd, wd_blk, a, b, tm):
    r, d = d_o.shape
    nch, _, fc = a.shape

    def body(g_ref, w_ref, a_ref, b_ref, da_ref, db_ref):
        dh = _dot(g_ref[...], w_ref[...], NT)
        av, bv = a_ref[...].astype(F32), b_ref[...].astype(F32)
        da_ref[...] = (dh * bv * _dsilu(av)).astype(BF)
        db_ref[...] = (dh * _silu(av)).astype(BF)

    chunk = pl.BlockSpec((None, tm, fc), lambda j, i: (j, i, 0))
    return _pcall(body, name=name, grid=(nch, r // tm),
                  in_specs=[pl.BlockSpec((tm, d), lambda j, i: (i, 0)),
                            pl.BlockSpec((None, fc, d), lambda j, i: (j, wd_blk, 0)), chunk, chunk],
                  out_specs=[chunk] * 2, out_shape=[jax.ShapeDtypeStruct((nch, r, fc), BF)] * 2,
                  compiler_params=_params(("arbitrary", "arbitrary")))(d_o, wd, a, b)


def _ffn_du(name, da, db, wgu, base, tm):
    nch, r, fc = da.shape
    d = wgu.shape[2]

    def body(da_ref, db_ref, wg_ref, wu_ref, o_ref, acc_ref):
        j = pl.program_id(1)
        part = _dot(da_ref[...], wg_ref[...], NT) + _dot(db_ref[...], wu_ref[...], NT)

        @pl.when(j == 0)
        def _():
            acc_ref[...] = part

        @pl.when(j > 0)
        def _():
            acc_ref[...] += part

        @pl.when(j == nch - 1)
        def _():
            o_ref[...] = acc_ref[...]

    chunk = pl.BlockSpec((None, tm, fc), lambda i, j: (j, i, 0))
    return _pcall(body, name=name, grid=(r // tm, nch),
                  in_specs=[chunk, chunk, pl.BlockSpec((None, None, d, fc), lambda i, j: (j, base, 0, 0)),
                            pl.BlockSpec((None, None, d, fc), lambda i, j: (j, base + 1, 0, 0))],
                  out_specs=pl.BlockSpec((tm, d), lambda i, j: (i, 0)),
                  out_shape=jax.ShapeDtypeStruct((r, d), F32), scratch_shapes=[pltpu.VMEM((tm, d), F32)],
                  compiler_params=_params(("arbitrary", "arbitrary")))(da, db, wgu, wgu)


def _ffn_dw_in(name, u, dz, tmm):
    r, d = u.shape
    nch, _, fc = dz.shape

    def body(u_ref, z_ref, o_ref):
        o_ref[...] = _dot(u_ref[...], z_ref[...], TN)

    return _pcall(body, name=name, grid=(nch, d // tmm),
                  in_specs=[pl.BlockSpec((r, tmm), lambda j, mi: (0, mi)),
                            pl.BlockSpec((None, r, fc), lambda j, mi: (j, 0, 0))],
                  out_specs=pl.BlockSpec((None, tmm, fc), lambda j, mi: (j, mi, 0)),
                  out_shape=jax.ShapeDtypeStruct((nch, d, fc), F32),
                  compiler_params=_params(("arbitrary", "arbitrary")))(u, dz)


def _ffn_dw_down(name, hid, d_o, tn):
    nch, r, fc = hid.shape
    d = d_o.shape[1]

    def body(h_ref, g_ref, o_ref):
        o_ref[...] = _dot(h_ref[...], g_ref[...], TN)

    return _pcall(body, name=name, grid=(nch, d // tn),
                  in_specs=[pl.BlockSpec((None, r, fc), lambda j, ni: (j, 0, 0)),
                            pl.BlockSpec((r, tn), lambda j, ni: (0, ni))],
                  out_specs=pl.BlockSpec((None, fc, tn), lambda j, ni: (j, 0, ni)),
                  out_shape=jax.ShapeDtypeStruct((nch, fc, d), F32),
                  compiler_params=_params(("arbitrary", "arbitrary")))(hid, d_o)


def _partner(x):
    n = x.shape[1]
    lane = lax.broadcasted_iota(jnp.int32, x.shape, 1)
    return jnp.where((lane & 15) < 8, pltpu.roll(x, n - 8, 1), pltpu.roll(x, 8, 1))


def _rope(x, ct, st):
    reps = x.shape[1] // ct.shape[1]
    if reps > 1:
        ct, st = jnp.tile(ct, (1, reps)), jnp.tile(st, (1, reps))
    return x * ct + _partner(x) * st


def _rope_t(dy, ct, st):
    reps = dy.shape[1] // ct.shape[1]
    if reps > 1:
        ct, st = jnp.tile(ct, (1, reps)), jnp.tile(st, (1, reps))
    return dy * ct + _partner(dy * st)


def _rope_tables(t_len, g_len, lane0):
    half = QK_ROPE // 4
    pos = jnp.arange(t_len)
    row = (pos // GRID_W).astype(F32)
    col = (pos % GRID_W).astype(F32)
    freqs = jnp.power(ROPE_THETA, -jnp.arange(0, QK_ROPE // 2, 2, dtype=F32) / (QK_ROPE // 2))
    ang_r, ang_c = row[:, None] * freqs, col[:, None] * freqs
    cs = jnp.concatenate([jnp.cos(ang_r)] * 2 + [jnp.cos(ang_c)] * 2, axis=1)
    sn = jnp.concatenate([-jnp.sin(ang_r), jnp.sin(ang_r), -jnp.sin(ang_c), jnp.sin(ang_c)], axis=1)
    assert cs.shape[1] == 4 * half == QK_ROPE
    ct = jnp.ones((t_len + g_len, HEAD_PAD), F32).at[:t_len, lane0:lane0 + QK_ROPE].set(cs)
    st = jnp.zeros((t_len + g_len, HEAD_PAD), F32).at[:t_len, lane0:lane0 + QK_ROPE].set(sn)
    return ct, st


def _attn_fwd(name, q, kp, vp, n_q, q_off, n_k, k_blk, heads, tq, scale):
    qb = q_off // tq

    def body(q_ref, k_ref, v_ref, o_ref, l_ref):
        s = _dot(q_ref[...], k_ref[...], NT) * scale
        m = jnp.max(s, axis=1, keepdims=True)
        p = jnp.exp(s - m)
        l = jnp.sum(p, axis=1, keepdims=True)
        o_ref[...] = (_dot(p, v_ref[...], NN) / l).astype(BF)
        l_ref[...] = jnp.broadcast_to(m + jnp.log(l), l_ref.shape)

    hw = heads * HEAD_PAD
    blk = pl.BlockSpec((tq, HEAD_PAD), lambda h, i: (i, h))
    kv = pl.BlockSpec((n_k, HEAD_PAD), lambda h, i: (k_blk, h))
    return _pcall(body, name=name, grid=(heads, n_q // tq),
                  in_specs=[pl.BlockSpec((tq, HEAD_PAD), lambda h, i: (i + qb, h)), kv, kv],
                  out_specs=[blk, blk],
                  out_shape=[jax.ShapeDtypeStruct((n_q, hw), BF), jax.ShapeDtypeStruct((n_q, hw), F32)],
                  compiler_params=_params(("arbitrary", "arbitrary")))(q, kp, vp)


def _attn_bwd(name, q, kp, vp, cat, dcat, lse, n_q, q_off, n_k, k_blk, heads, tq, scale, col_blk):
    qb = q_off // tq

    def body(q_ref, k_ref, v_ref, o_ref, do_ref, l_ref, dq_ref, dk_ref, dv_ref):
        i = pl.program_id(1)
        qv, kv_, vv = q_ref[...], k_ref[...], v_ref[...]
        dov = do_ref[...]
        s = _dot(qv, kv_, NT) * scale
        p = jnp.exp(s - l_ref[...][:, 0:1])
        dp = _dot(dov, vv, NT)
        delta = jnp.sum(dov * o_ref[...].astype(F32), axis=1, keepdims=True)
        ds = (p * (dp - delta) * scale).astype(BF)
        dq_ref[...] = _dot(ds, kv_, NN)
        dk = _dot(ds, qv, TN)
        dv = _dot(p, dov, TN)

        @pl.when(i == 0)
        def _():
            dk_ref[...] = dk
            dv_ref[...] = dv

        @pl.when(i > 0)
        def _():
            dk_ref[...] += dk
            dv_ref[...] += dv

    hw = heads * HEAD_PAD
    qspec = pl.BlockSpec((tq, HEAD_PAD), lambda h, i: (i + qb, h))
    cspec = pl.BlockSpec((tq, HEAD_PAD), lambda h, i: (i + qb, col_blk + h))
    kv = pl.BlockSpec((n_k, HEAD_PAD), lambda h, i: (k_blk, h))
    acc = pl.BlockSpec((n_k, HEAD_PAD), lambda h, i: (0, h))
    blk = pl.BlockSpec((tq, HEAD_PAD), lambda h, i: (i, h))
    return _pcall(body, name=name, grid=(heads, n_q // tq),
                  in_specs=[qspec, kv, kv, cspec, cspec, blk], out_specs=[blk, acc, acc],
                  out_shape=[jax.ShapeDtypeStruct((n_q, hw), F32), jax.ShapeDtypeStruct((n_k, hw), F32),
                             jax.ShapeDtypeStruct((n_k, hw), F32)],
                  compiler_params=_params(("arbitrary", "arbitrary")))(q, kp, vp, cat, dcat, lse)


def _shift(x, k):
    return pltpu.roll(x, k % x.shape[0], 0)


def _window_sum(v, w, mirrored):
    n, gd = v.shape
    pad = jnp.zeros((POOL_PAD, gd), F32)
    e = jnp.concatenate([pad, v, pad], axis=0)
    acc = e + _shift(e, -1 if mirrored else 1)
    step = 1
    while 2 * step < w:
        acc = _shift(acc, step) + _shift(acc, -step)
        step *= 2
    return acc[POOL_PAD:POOL_PAD + n]


def _window_count(n, w):
    t = lax.broadcasted_iota(jnp.int32, (n, 1), 0)
    lo = jnp.maximum(t - w // 2, 0)
    hi = jnp.minimum(t + (w - w // 2 - 1), n - 1)
    return (hi - lo + 1).astype(F32)


def _pool_fwd(name, u, pool_w, scale):
    n, pd = u.shape
    ng = len(POOL_WINDOWS)
    gd = pd // ng

    def body(u_ref, w_ref, s_ref, y_ref):
        for g, w in enumerate(POOL_WINDOWS):
            sl = slice(g * gd, (g + 1) * gd)
            ug = u_ref[:, sl]
            p = _window_sum(ug, w, False) / _window_count(n, w) - ug
            y_ref[:, sl] = (_dot(p, w_ref[g], NN) * s_ref[:, sl]).astype(BF)

    return _pcall(body, name=name, out_shape=jax.ShapeDtypeStruct((n, pd), BF),
                  compiler_params=_params())(u, pool_w, scale)


def _pool_bwd(name, u, dcat, pool_w, scale, row_off):
    n, pd = u.shape
    ng = len(POOL_WINDOWS)
    gd = pd // ng

    def body(u_ref, dy_ref, w_ref, s_ref, du_ref, dw_ref, ds_ref):
        ds_ref[...] = jnp.zeros_like(ds_ref)
        for g, w in enumerate(POOL_WINDOWS):
            sl = slice(g * gd, (g + 1) * gd)
            ug, dy, wg = u_ref[:, sl], dy_ref[:, sl], w_ref[g]
            cnt = _window_count(n, w)
            p = _window_sum(ug, w, False) / cnt - ug
            ds_ref[0:1, sl] = jnp.sum(dy * _dot(p, wg, NN), axis=0, keepdims=True)
            dys = dy * s_ref[:, sl]
            dw_ref[g] = _dot(p, dys, TN)
            dp = _dot(dys, wg, NT)
            du_ref[:, sl] = (_window_sum(dp / cnt, w, True) - dp).astype(BF)

    rb = row_off // n
    return _pcall(body, name=name, grid=(1,),
                  in_specs=[pl.BlockSpec((n, pd), lambda i: (0, 0)), pl.BlockSpec((n, pd), lambda i: (rb, 0)),
                            pl.BlockSpec(pool_w.shape, lambda i: (0, 0, 0)), pl.BlockSpec(scale.shape, lambda i: (0, 0))],
                  out_specs=[pl.BlockSpec((n, pd), lambda i: (0, 0)), pl.BlockSpec((ng, gd, gd), lambda i: (0, 0, 0)),
                             pl.BlockSpec((8, pd), lambda i: (0, 0))],
                  out_shape=[jax.ShapeDtypeStruct((n, pd), BF), jax.ShapeDtypeStruct((ng, gd, gd), F32),
                             jax.ShapeDtypeStruct((8, pd), F32)],
                  compiler_params=_params(("arbitrary",)))(u, dcat, pool_w, scale)


def _edge_shift(z, k):
    n = z.shape[0]
    t = lax.broadcasted_iota(jnp.int32, (n, 1), 0)
    keep = (t >= k) if k > 0 else (t < n + k)
    return jnp.where(keep, pltpu.roll(z, k % n, 0), 0.0)


def _conv_fwd(name, p3, cw, tc):
    n, cd = p3.shape[0], p3.shape[1] // 3
    nb = cd // tc

    def body(b_ref, c_ref, v_ref, w_ref, y_ref):
        z = c_ref[...] * v_ref[...]
        w = w_ref[...]
        zc = w[0:1] * _edge_shift(z, 1) + w[1:2] * z + w[2:3] * _edge_shift(z, -1)
        y_ref[...] = (b_ref[...] * zc).astype(BF)

    return _pcall(body, name=name, grid=(nb,),
                  in_specs=[pl.BlockSpec((n, tc), lambda j: (0, j)), pl.BlockSpec((n, tc), lambda j: (0, nb + j)),
                            pl.BlockSpec((n, tc), lambda j: (0, 2 * nb + j)), pl.BlockSpec((3, tc), lambda j: (0, j))],
                  out_specs=pl.BlockSpec((n, tc), lambda j: (0, j)), out_shape=jax.ShapeDtypeStruct((n, cd), BF),
                  compiler_params=_params(("arbitrary",)))(p3, p3, p3, cw)


def _conv_bwd(name, p3, cw, dy, tc):
    n, cd = dy.shape
    nb = cd // tc

    def body(b_ref, c_ref, v_ref, w_ref, dy_ref, dp_ref, dw_ref):
        cv, vv, w, dyv = c_ref[...], v_ref[...], w_ref[...], dy_ref[...]
        z = cv * vv
        zl, zr = _edge_shift(z, 1), _edge_shift(z, -1)
        zc = w[0:1] * zl + w[1:2] * z + w[2:3] * zr
        dzc = dyv * b_ref[...]
        dz = w[0:1] * _edge_shift(dzc, -1) + w[1:2] * dzc + w[2:3] * _edge_shift(dzc, 1)
        dp_ref[0] = (dyv * zc).astype(BF)
        dp_ref[1] = (dz * vv).astype(BF)
        dp_ref[2] = (dz * cv).astype(BF)
        dw_ref[...] = jnp.zeros_like(dw_ref)
        dw_ref[0:1, :] = jnp.sum(dzc * zl, axis=0, keepdims=True)
        dw_ref[1:2, :] = jnp.sum(dzc * z, axis=0, keepdims=True)
        dw_ref[2:3, :] = jnp.sum(dzc * zr, axis=0, keepdims=True)

    col = pl.BlockSpec((n, tc), lambda j: (0, j))
    return _pcall(body, name=name, grid=(nb,),
                  in_specs=[col, pl.BlockSpec((n, tc), lambda j: (0, nb + j)),
                            pl.BlockSpec((n, tc), lambda j: (0, 2 * nb + j)), pl.BlockSpec((3, tc), lambda j: (0, j)), col],
                  out_specs=[pl.BlockSpec((3, n, tc), lambda j: (0, 0, j)), pl.BlockSpec((8, tc), lambda j: (0, j))],
                  out_shape=[jax.ShapeDtypeStruct((3, n, cd), BF), jax.ShapeDtypeStruct((8, cd), F32)],
                  compiler_params=_params(("arbitrary",)))(p3, p3, p3, cw, dy)


def _conv_din(name, dp3, w_in, tm):
    _, n, cd = dp3.shape
    d = w_in.shape[0]

    def body(a_ref, w_ref, o_ref, acc_ref):
        j = pl.program_id(1)
        part = _dot(a_ref[...], w_ref[...], NT)

        @pl.when(j == 0)
        def _():
            acc_ref[...] = part

        @pl.when(j > 0)
        def _():
            acc_ref[...] += part

        @pl.when(j == 2)
        def _():
            o_ref[...] = acc_ref[...]

    return _pcall(body, name=name, grid=(n // tm, 3),
                  in_specs=[pl.BlockSpec((None, tm, cd), lambda i, j: (j, i, 0)),
                            pl.BlockSpec((d, cd), lambda i, j: (0, j))],
                  out_specs=pl.BlockSpec((tm, d), lambda i, j: (i, 0)), out_shape=jax.ShapeDtypeStruct((n, d), F32),
                  scratch_shapes=[pltpu.VMEM((tm, d), F32)],
                  compiler_params=_params(("arbitrary", "arbitrary")))(dp3, w_in)


def _conv_dw_in(name, u, dp3, tmm, tn):
    n, d = u.shape
    cd = dp3.shape[2]
    nb = cd // tn

    def body(u_ref, z_ref, o_ref):
        o_ref[...] = _dot(u_ref[...], z_ref[...], TN)

    return _pcall(body, name=name, grid=(3 * nb, d // tmm),
                  in_specs=[pl.BlockSpec((n, tmm), lambda j, mi: (0, mi)),
                            pl.BlockSpec((None, n, tn), lambda j, mi: (j // nb, 0, j % nb))],
                  out_specs=pl.BlockSpec((tmm, tn), lambda j, mi: (mi, j)),
                  out_shape=jax.ShapeDtypeStruct((d, 3 * cd), F32),
                  compiler_params=_params(("arbitrary", "arbitrary")))(u, dp3)


def _loss_head(name, h, target, gain, tm):
    d = h.shape[1]

    def fn(rows, consts, m):
        hv, tv = rows
        g = consts[0][0:1, :]
        n, r = _rms(hv)
        err = n * g - tv
        dy = err / d
        loss = 0.5 * jnp.sum(err * err) / d
        acc = {0: jnp.sum(dy * n, axis=0, keepdims=True), 1: jnp.full((1, d), loss, F32)}
        return [_rms_bwd(dy * g, n, r)], acc

    return _rows_call(name, fn, h.shape[0], tm, [h, target], [gain], None, [(d, F32)], acc_w=d)


def _adamw(name, w, g, m, v):
    shape = w.shape
    cdim = shape[-1]
    r = max(1, math.prod(shape[:-1]))
    tr = r
    if r * cdim * 4 > (3 << 19):
        tr = _pick(r, max(8, (3 << 19) // (cdim * 4)), 8)
    c1 = 1.0 / (1.0 - ADAM_B1 ** ADAM_STEP)
    c2 = 1.0 / (1.0 - ADAM_B2 ** ADAM_STEP)

    def body(w_ref, g_ref, m_ref, v_ref, d_ref, nm_ref, nv_ref):
        gv = g_ref[...]
        nm = ADAM_B1 * m_ref[...] + (1.0 - ADAM_B1) * gv
        nv = ADAM_B2 * v_ref[...] + (1.0 - ADAM_B2) * (gv * gv)
        nm_ref[...] = nm
        nv_ref[...] = nv
        d_ref[...] = -ADAM_LR * ((nm * c1) / (jnp.sqrt(nv * c2) + ADAM_EPS) + ADAM_WD * w_ref[...])

    spec = pl.BlockSpec((tr, cdim), lambda i: (i, 0))
    res = _pcall(body, name=name, grid=(r // tr,), in_specs=[spec] * 4, out_specs=[spec] * 3,
                 out_shape=[jax.ShapeDtypeStruct((r, cdim), F32)] * 3,
                 compiler_params=_params(("arbitrary",)))(*[t.reshape(r, cdim) for t in (w, g, m, v)])
    return [t.reshape(shape) for t in res]


def _ffn_half_fwd(tag, s, gains, mod, k, wts, n_lat, tm, h_tiles, tm_big):
    wgu, base, wd, wd_blk = wts
    u = _adaln_fwd(f"adaln_{tag}", s, gains, k, mod, k, tm, h_tiles)
    a, b, hid = _ffn_up(f"ffn_up_{tag}", u, wgu, base, tm)
    s_out, o = _ffn_down(f"ffn_down_{tag}", hid, wd, wd_blk, s, mod, k, n_lat, tm_big)
    return s_out, (s, u, a, b, hid, o)


def _ffn_half_bwd(tag, ds_out, saved, gains, mod, k, wts, tm, h_tiles, tm_big):
    wgu, base, wd, wd_blk = wts
    s, u, a, b, hid, o = saved
    d_o, acc_g = _resid_bwd(f"resid_bwd_{tag}", ds_out, o, mod, k, 0.5, tm, h_tiles)
    da, db = _ffn_dhid(f"ffn_dhid_{tag}", d_o, wd, wd_blk, a, b, tm)
    du = _ffn_du(f"ffn_du_{tag}", da, db, wgu, base, tm_big)
    d = u.shape[1]
    dwg = _ffn_dw_in(f"ffn_dwg_{tag}", u, da, _pick(d, 256))
    dwu = _ffn_dw_in(f"ffn_dwu_{tag}", u, db, _pick(d, 256))
    dwd = _ffn_dw_down(f"ffn_dwd_{tag}", hid, d_o, _pick(d, 512))
    ds, acc_n = _adaln_bwd(f"adaln_bwd_{tag}", s, du, ds_out, gains, k, mod, k, tm, h_tiles)
    return ds, dwg, dwu, dwd, (acc_n[:, 0], acc_n[:, 1], acc_g[:, 0]), jnp.sum(acc_n[:, 2], axis=0)


def kernel(x, c, ctx, c_ctx, norm_g, w_mod, b_mod, ffn_w_gate, ffn_w_up, ffn_w_down, ab_w_in, pool_w, pool_scale, q_norm_g, w_uq, kv_norm_g, w_ukv, ab_w_out, conv_w_in, conv_w, conv_w_out, final_norm_g, loss_target, m_c_ctx, m_norm_g, m_w_mod, m_b_mod, m_ffn_w_gate, m_ffn_w_up, m_ffn_w_down, m_ab_w_in, m_pool_w, m_pool_scale, m_q_norm_g, m_w_uq, m_kv_norm_g, m_w_ukv, m_ab_w_out, m_conv_w_in, m_conv_w, m_conv_w_out, m_final_norm_g, v_c_ctx, v_norm_g, v_w_mod, v_b_mod, v_ffn_w_gate, v_ffn_w_up, v_ffn_w_down, v_ab_w_in, v_pool_w, v_pool_scale, v_q_norm_g, v_w_uq, v_kv_norm_g, v_w_ukv, v_ab_w_out, v_conv_w_in, v_conv_w, v_conv_w_out, v_final_norm_g):
    weights = dict(c_ctx=c_ctx, norm_g=norm_g, w_mod=w_mod, b_mod=b_mod, ffn_w_gate=ffn_w_gate, ffn_w_up=ffn_w_up,
                   ffn_w_down=ffn_w_down, ab_w_in=ab_w_in, pool_w=pool_w, pool_scale=pool_scale, q_norm_g=q_norm_g,
                   w_uq=w_uq, kv_norm_g=kv_norm_g, w_ukv=w_ukv, ab_w_out=ab_w_out, conv_w_in=conv_w_in, conv_w=conv_w,
                   conv_w_out=conv_w_out, final_norm_g=final_norm_g)
    mom_m = dict(c_ctx=m_c_ctx, norm_g=m_norm_g, w_mod=m_w_mod, b_mod=m_b_mod, ffn_w_gate=m_ffn_w_gate,
                 ffn_w_up=m_ffn_w_up, ffn_w_down=m_ffn_w_down, ab_w_in=m_ab_w_in, pool_w=m_pool_w,
                 pool_scale=m_pool_scale, q_norm_g=m_q_norm_g, w_uq=m_w_uq, kv_norm_g=m_kv_norm_g, w_ukv=m_w_ukv,
                 ab_w_out=m_ab_w_out, conv_w_in=m_conv_w_in, conv_w=m_conv_w, conv_w_out=m_conv_w_out,
                 final_norm_g=m_final_norm_g)
    mom_v = dict(c_ctx=v_c_ctx, norm_g=v_norm_g, w_mod=v_w_mod, b_mod=v_b_mod, ffn_w_gate=v_ffn_w_gate,
                 ffn_w_up=v_ffn_w_up, ffn_w_down=v_ffn_w_down, ab_w_in=v_ab_w_in, pool_w=v_pool_w,
                 pool_scale=v_pool_scale, q_norm_g=v_q_norm_g, w_uq=v_w_uq, kv_norm_g=v_kv_norm_g, w_ukv=v_w_ukv,
                 ab_w_out=v_ab_w_out, conv_w_in=v_conv_w_in, conv_w=v_conv_w, conv_w_out=v_conv_w_out,
                 final_norm_g=v_final_norm_g)

    t_len, d = x.shape[1], x.shape[2]
    g_len = ctx.shape[1]
    r_len = t_len + g_len
    fc = ffn_w_gate.shape[3]
    heads = d // 128
    pool_dim = d // 2
    q_rank, kv_rank = q_norm_g.shape[1], kv_norm_g.shape[1]
    hw = heads * HEAD_PAD
    attn_scale = 1.0 / math.sqrt(QK_NOPE + QK_ROPE)
    kvr_w = kv_rank + HEAD_PAD
    in_w = pool_dim + q_rank + kvr_w
    tm = 256 if g_len % 256 == 0 else g_len
    assert t_len % tm == 0 and g_len % tm == 0 and t_len % g_len == 0 and pool_dim % 128 == 0
    h_tiles = t_len // tm
    tm_l0 = _pick(r_len, 768, tm)
    tm_l1 = _pick(t_len, 1024, tm)

    xi, yi, ci = lax.axis_index("x"), lax.axis_index("y"), lax.axis_index("c")
    me = 4 * xi + 2 * yi + ci
    shard = 2 * xi + yi

    def halves(a2d):
        return a2d.astype(BF).reshape(2, a2d.shape[0] // 2, a2d.shape[1])

    packs = [
        jnp.stack([ffn_w_gate, ffn_w_up], axis=2).reshape(8 * d, fc),
        jnp.concatenate([ffn_w_down.reshape(4 * fc, d), ab_w_out[0], conv_w_out[0]], axis=0),
        w_uq[0], w_ukv[0], ab_w_in[0], conv_w_in[0],
    ]
    gathered = _gather_halves("gather_weights", [halves(p) for p in packs])
    gathered = [g.reshape(N_SHARD, p.shape[0], p.shape[1]) for g, p in zip(gathered, packs)]
    wgu_all = gathered[0].reshape(N_SHARD, 8, d, fc)
    ffn_w = [[(wgu_all, 4 * l + 2 * f, gathered[1], 2 * l + f) for f in range(2)] for l in range(2)]
    n_out_rows = ab_w_out.shape[1]
    w_out_full = gathered[1][:, 4 * fc:4 * fc + n_out_rows].reshape(N_SHARD * n_out_rows, d)
    cw_out_full = gathered[1][:, 4 * fc + n_out_rows:].reshape(-1, d)
    w_uq_full = gathered[2].reshape(q_rank, heads * (QK_NOPE + QK_ROPE))
    w_ukv_full = gathered[3].transpose(1, 0, 2).reshape(kv_rank, heads * (QK_NOPE + V_HEAD))
    w_in_full = gathered[4].transpose(1, 0, 2).reshape(d, -1)
    cw_in_full = gathered[5].transpose(1, 0, 2).reshape(d, -1)

    wq_p = jnp.pad(w_uq_full.reshape(q_rank, heads, QK_NOPE + QK_ROPE),
                   ((0, 0), (0, 0), (0, HEAD_PAD - QK_NOPE - QK_ROPE))).reshape(q_rank, hw)
    ukv3 = w_ukv_full.reshape(kv_rank, heads, QK_NOPE + V_HEAD)
    wk_top = jnp.pad(ukv3[..., :QK_NOPE], ((0, 0), (0, 0), (0, HEAD_PAD - QK_NOPE))).reshape(kv_rank, hw)
    wv_top = jnp.pad(ukv3[..., QK_NOPE:], ((0, 0), (0, 0), (0, HEAD_PAD - V_HEAD))).reshape(kv_rank, hw)
    spread = jnp.zeros((HEAD_PAD, heads, HEAD_PAD), BF).at[
        jnp.arange(QK_ROPE)[:, None], jnp.arange(heads)[None, :], QK_NOPE + jnp.arange(QK_ROPE)[:, None]].set(1.0)
    wk_ext = jnp.concatenate([wk_top, spread.reshape(HEAD_PAD, hw)], axis=0)
    wv_ext = jnp.concatenate([wv_top, jnp.zeros((HEAD_PAD, hw), BF)], axis=0)
    w_in_pool = w_in_full[:, :pool_dim]
    w_in_q = w_in_full[:, pool_dim:pool_dim + q_rank]
    w_in_kvr = jnp.pad(w_in_full[:, pool_dim + q_rank:], ((0, 0), (0, HEAD_PAD - QK_ROPE)))
    w_out_attn = jnp.pad(w_out_full[pool_dim:].reshape(heads, V_HEAD, d),
                         ((0, 0), (0, HEAD_PAD - V_HEAD), (0, 0))).reshape(hw, d)
    w_out_p = jnp.concatenate([w_out_full[:pool_dim], w_out_attn], axis=0)

    small = jnp.concatenate([norm_g.reshape(6, -1), conv_w[0]], axis=0)
    small = jnp.pad(small, ((0, 7), (0, 0)))
    c_row = jnp.pad(c, ((0, 7), (0, 0)))
    small_all, c_all = _gather_all("gather_small", [small, c_row])
    small_full = small_all[::2].transpose(1, 0, 2).reshape(16, d)
    gains = [jnp.pad(small_full[3 * l:3 * l + 3], ((0, 5), (0, 0))) for l in range(2)]
    conv_w_full = small_full[6:9]
    c16 = jnp.concatenate([c_all[:, 0], c_ctx[None], jnp.zeros((7, d), F32)], axis=0)

    n_col = w_mod.shape[2]
    b_sh = lax.dynamic_slice_in_dim(b_mod, shard * n_col, n_col, axis=1)
    m_sh = [_mm(f"mod_fwd_{l}", c16, w_mod[l], 'nn', tm=16, tn=768, a_pre=_silu,
                epi=lambda acc, i, bv: (acc + bv,), epi_args=(b_sh[l:l + 1],), epi_kinds=('n',)) for l in range(2)]
    m_all = _gather_all("gather_mod", [jnp.concatenate(m_sh, axis=0)])[0]
    m_full = m_all[::2].reshape(N_SHARD, 2, 16, n_col).transpose(1, 2, 0, 3).reshape(2, 16, N_MOD * d)
    mod_h = [jnp.pad(lax.dynamic_index_in_dim(m_full[l], me, 0, keepdims=False).reshape(N_MOD, d), ((0, 7), (0, 0)))
             for l in range(2)]
    mod_g0 = jnp.pad(m_full[0, 8].reshape(N_MOD, d), ((0, 7), (0, 0)))
    mods = [jnp.stack([mod_h[0], mod_g0]), mod_h[1][None]]

    s0 = jnp.concatenate([x[0], ctx[0]], axis=0)
    s1, sav_f00 = _ffn_half_fwd("l0a", s0, gains[0], mods[0], 0, ffn_w[0][0], t_len, tm, h_tiles, tm_l0)
    u_mix = _adaln_fwd("adaln_l0m", s1, gains[0], 1, mods[0], 1, tm, h_tiles)
    p_pool = _mm("in_pool", u_mix, w_in_pool, 'nn', tm=tm, tn=pool_dim)
    p_q = _mm("in_q", u_mix, w_in_q, 'nn', tm=tm, tn=q_rank)
    p_kvr = _mm("in_kvr", u_mix, w_in_kvr, 'nn', tm=tm, tn=kvr_w)
    qg = jnp.pad(q_norm_g, ((0, 7), (0, 0)))
    kvg = jnp.pad(kv_norm_g, ((0, 7), (0, 0)))
    tq_c, tq_s = _rope_tables(t_len, g_len, QK_NOPE)
    tk_c, tk_s = _rope_tables(t_len, g_len, 0)

    def qn_fn(rows, consts, m):
        n, _ = _rms(rows[0])
        return [n * consts[0][0:1, :]], {}

    qn = _rows_call("q_norm", qn_fn, r_len, tm, [p_q], [qg], None, [(q_rank, BF)])[0]
    q_r = _mm("q_up", qn, wq_p, 'nn', tm=tm, tn=hw, out_dtypes=(BF,),
              epi=lambda acc, i, ct, st: (_rope(acc, ct, st),), epi_args=(tq_c, tq_s), epi_kinds=('mt', 'mt'))

    def kvn_fn(rows, consts, m):
        pv, ct, st = rows
        n, _ = _rms(pv[:, :kv_rank])
        return [jnp.concatenate([n * consts[0][0:1, :], _rope(pv[:, kv_rank:], ct, st)], axis=1)], {}

    kvn = _rows_call("kv_norm", kvn_fn, r_len, tm, [p_kvr, tk_c, tk_s], [kvg], None, [(kvr_w, BF)])[0]
    k_p = _mm("k_up", kvn, wk_ext, 'nn', tm=tm, tn=hw, out_dtypes=(BF,))
    v_p = _mm("v_up", kvn, wv_ext, 'nn', tm=tm, tn=hw, out_dtypes=(BF,))
    o_h, lse_h = _attn_fwd("attn_h", q_r, k_p, v_p, t_len, 0, r_len, 0, heads, tm, attn_scale)
    o_g, lse_g = _attn_fwd("attn_g", q_r, k_p, v_p, g_len, t_len, g_len, t_len // g_len, heads, tm, attn_scale)
    y_h = _pool_fwd("pool_h", p_pool[:t_len], pool_w[0], pool_scale)
    y_g = _pool_fwd("pool_g", p_pool[t_len:], pool_w[0], pool_scale)
    cat = jnp.concatenate([jnp.concatenate([y_h, y_g], axis=0), jnp.concatenate([o_h, o_g], axis=0)], axis=1)

    def resid_epi(k3, n_lat, tmr):
        def epi(acc, i, sv, mv):
            return sv + _row_gate(mv, k3, i, tmr, n_lat) * acc, acc
        return epi

    s2, o_mix0 = _mm("mix_out_l0", cat, w_out_p, 'nn', tm=tm, tn=d, out_dtypes=(F32, F32),
                     epi=resid_epi(5, t_len, tm), epi_args=(s1, mods[0]), epi_kinds=('mn', 'w'))
    s3, sav_f01 = _ffn_half_fwd("l0b", s2, gains[0], mods[0], 2, ffn_w[0][1], t_len, tm, h_tiles, tm_l0)

    tml = 256 if t_len % 256 == 0 else tm
    h3 = s3[:t_len]
    h4, sav_f10 = _ffn_half_fwd("l1a", h3, gains[1], mods[1], 0, ffn_w[1][0], t_len, tml, None, tm_l1)
    u_cv = _adaln_fwd("adaln_l1m", h4, gains[1], 1, mods[1], 1, tml, None)
    p3 = _mm("conv_in", u_cv, cw_in_full, 'nn', tm=tml, tn=512)
    cwp = conv_w_full
    tc = _pick(d, 256)
    y_cv = _conv_fwd("conv_fwd", p3, cwp, tc)
    h5, o_mix1 = _mm("mix_out_l1", y_cv, cw_out_full, 'nn', tm=tml, tn=d, out_dtypes=(F32, F32),
                     epi=resid_epi(5, t_len, tml), epi_args=(h4, mods[1]), epi_kinds=('mn', 'w'))
    h6, sav_f11 = _ffn_half_fwd("l1b", h5, gains[1], mods[1], 2, ffn_w[1][1], t_len, tml, None, tm_l1)

    fg = jnp.pad(final_norm_g[None], ((0, 7), (0, 0)))
    dh6, acc_loss = _loss_head("loss_head", h6, loss_target[0], fg, tml)
    loss = lax.psum(acc_loss[0, 1, 0], ("x", "y", "c"))
    d_final_g = acc_loss[0, 0]

    dgain = [[None] * 3 for _ in range(2)]
    dmod = [[None] * N_MOD for _ in range(2)]

    def put(l, k, triple):
        dmod[l][3 * k], dmod[l][3 * k + 1], dmod[l][3 * k + 2] = triple

    dh5, dwg11, dwu11, dwd11, tr, dgain[1][2] = _ffn_half_bwd("l1b", dh6, sav_f11, gains[1], mods[1], 2,
                                                              ffn_w[1][1], tml, None, tm_l1)
    put(1, 2, tr)
    d_o1, acc_g1 = _resid_bwd("resid_bwd_l1m", dh5, o_mix1, mods[1], 1, 1.0, tml, None)
    dy_cv = _mm("mix_out_l1_dx", d_o1, cw_out_full, 'nt', tm=tml, tn=d)
    d_cw_out = _mm("mix_out_l1_dw", y_cv, d_o1, 'tn', tm=256, tn=512)
    dp3, d_cw = _conv_bwd("conv_bwd", p3, cwp, dy_cv, tc)
    du_cv = _conv_din("conv_in_dx", dp3, cw_in_full, tml)
    d_cw_in = _conv_dw_in("conv_in_dw", u_cv, dp3, _pick(d, 256), _pick(d, 512))
    dh4, acc_n1 = _adaln_bwd("adaln_bwd_l1m", h4, du_cv, dh5, gains[1], 1, mods[1], 1, tml, None)
    put(1, 1, (acc_n1[:, 0], acc_n1[:, 1], acc_g1[:, 0]))
    dgain[1][1] = acc_n1[0, 2]
    dh3, dwg10, dwu10, dwd10, tr, dgain[1][0] = _ffn_half_bwd("l1a", dh4, sav_f10, gains[1], mods[1], 0,
                                                              ffn_w[1][0], tml, None, tm_l1)
    put(1, 0, tr)

    ds3 = jnp.concatenate([dh3, jnp.zeros((g_len, d), F32)], axis=0)
    ds2, dwg01, dwu01, dwd01, tr, dgain[0][2] = _ffn_half_bwd("l0b", ds3, sav_f01, gains[0], mods[0], 2,
                                                              ffn_w[0][1], tm, h_tiles, tm_l0)
    put(0, 2, tr)
    d_o0, acc_g0 = _resid_bwd("resid_bwd_l0m", ds2, o_mix0, mods[0], 1, 1.0, tm, h_tiles)
    dcat = _mm("mix_out_l0_dx", d_o0, w_out_p, 'nt', tm=tm, tn=pool_dim + hw)
    d_w_out_p = _mm("mix_out_l0_dw", cat, d_o0, 'tn', tm=256, tn=512)
    col_blk = pool_dim // HEAD_PAD
    dq_h, dk_h, dv_h = _attn_bwd("attn_bwd_h", q_r, k_p, v_p, cat, dcat, lse_h, t_len, 0, r_len, 0, heads, tm,
                                 attn_scale, col_blk)
    dq_g, dk_g, dv_g = _attn_bwd("attn_bwd_g", q_r, k_p, v_p, cat, dcat, lse_g, g_len, t_len, g_len,
                                 t_len // g_len, heads, tm, attn_scale, col_blk)
    dq_all = jnp.concatenate([dq_h, dq_g], axis=0)
    dk_all = dk_h.at[t_len:].add(dk_g)
    dv_all = dv_h.at[t_len:].add(dv_g)
    dkvn = _mm("k_up_dx", dk_all, wk_ext, 'nt', tm=tm, tn=kvr_w)
    dkvn = _mm("v_up_dx", dv_all, wv_ext, 'nt', tm=tm, tn=kvr_w, epi=lambda acc, i, prev: (acc + prev,),
               epi_args=(dkvn,), epi_kinds=('mn',))
    d_wk_ext = _mm("k_up_dw", kvn, dk_all, 'tn', tm=kvr_w, tn=512)
    d_wv_ext = _mm("v_up_dw", kvn, dv_all, 'tn', tm=kvr_w, tn=512)

    def kvn_bwd_fn(rows, consts, m):
        pv, dv_, ct, st = rows
        g = consts[0][0:1, :]
        n, r = _rms(pv[:, :kv_rank])
        dyn = dv_[:, :kv_rank]
        dckv = _rms_bwd(dyn * g, n, r)
        dkr = _rope_t(dv_[:, kv_rank:], ct, st)
        return [jnp.concatenate([dckv, dkr], axis=1)], {0: jnp.sum(dyn * n, axis=0, keepdims=True)}

    dp_kvr, acc_kvg = _rows_call("kv_norm_bwd", kvn_bwd_fn, r_len, tm, [p_kvr, dkvn, tk_c, tk_s], [kvg], None,
                                 [(kvr_w, BF)], acc_w=kv_rank)

    def qrope_bwd_fn(rows, consts, m):
        return [_rope_t(rows[0], rows[1], rows[2])], {}

    dq_pad = _rows_call("q_rope_bwd", qrope_bwd_fn, r_len, tm, [dq_all, tq_c, tq_s], [], None, [(hw, BF)])[0]
    dqn = _mm("q_up_dx", dq_pad, wq_p, 'nt', tm=tm, tn=q_rank)
    d_wq_p = _mm("q_up_dw", qn, dq_pad, 'tn', tm=256, tn=512)

    def qn_bwd_fn(rows, consts, m):
        pv, dv_ = rows
        g = consts[0][0:1, :]
        n, r = _rms(pv)
        return [_rms_bwd(dv_ * g, n, r)], {0: jnp.sum(dv_ * n, axis=0, keepdims=True)}

    dp_q, acc_qg = _rows_call("q_norm_bwd", qn_bwd_fn, r_len, tm, [p_q, dqn], [qg], None, [(q_rank, BF)],
                              acc_w=q_rank)
    dpu_h, dpw_h, dps_h = _pool_bwd("pool_bwd_h", p_pool[:t_len], dcat, pool_w[0], pool_scale, 0)
    dpu_g, dpw_g, dps_g = _pool_bwd("pool_bwd_g", p_pool[t_len:], dcat, pool_w[0], pool_scale, t_len)
    dp_pool = jnp.concatenate([dpu_h, dpu_g], axis=0)
    add_prev = lambda acc, i, prev: (acc + prev,)
    du_mix = _mm("in_pool_dx", dp_pool, w_in_pool, 'nt', tm=tm, tn=d)
    du_mix = _mm("in_q_dx", dp_q, w_in_q, 'nt', tm=tm, tn=d, epi=add_prev, epi_args=(du_mix,), epi_kinds=('mn',))
    du_mix = _mm("in_kvr_dx", dp_kvr, w_in_kvr, 'nt', tm=tm, tn=d, epi=add_prev, epi_args=(du_mix,), epi_kinds=('mn',))
    d_w_in = jnp.concatenate([
        _mm("in_pool_dw", u_mix, dp_pool, 'tn', tm=256, tn=pool_dim),
        _mm("in_q_dw", u_mix, dp_q, 'tn', tm=256, tn=q_rank),
        _mm("in_kvr_dw", u_mix, dp_kvr, 'tn', tm=256, tn=kvr_w)[:, :kv_rank + QK_ROPE]], axis=1)
    ds1, acc_n0 = _adaln_bwd("adaln_bwd_l0m", s1, du_mix, ds2, gains[0], 1, mods[0], 1, tm, h_tiles)
    put(0, 1, (acc_n0[:, 0], acc_n0[:, 1], acc_g0[:, 0]))
    dgain[0][1] = jnp.sum(acc_n0[:, 2], axis=0)
    ds0, dwg00, dwu00, dwd00, tr, dgain[0][0] = _ffn_half_bwd("l0a", ds1, sav_f00, gains[0], mods[0], 0,
                                                              ffn_w[0][0], tm, h_tiles, tm_l0)
    put(0, 0, tr)
    grad_x = ds0[:t_len][None]

    d_w_uq = d_wq_p.reshape(q_rank, heads, HEAD_PAD)[..., :QK_NOPE + QK_ROPE].reshape(q_rank, -1)
    d_w_ukv = jnp.concatenate([d_wk_ext[:kv_rank].reshape(kv_rank, heads, HEAD_PAD)[..., :QK_NOPE],
                               d_wv_ext[:kv_rank].reshape(kv_rank, heads, HEAD_PAD)[..., :V_HEAD]],
                              axis=-1).reshape(kv_rank, -1)
    d_w_out = jnp.concatenate([d_w_out_p[:pool_dim],
                               d_w_out_p[pool_dim:].reshape(heads, HEAD_PAD, d)[:, :V_HEAD].reshape(-1, d)], axis=0)

    dmh = jnp.stack([jnp.stack([dmod[l][k][0] for k in range(N_MOD)]) for l in range(2)])
    dmg0 = jnp.stack([dmod[0][k][1] for k in range(N_MOD)])
    dg_rows = jnp.stack([dgain[l][k] for l in range(2) for k in range(3)])
    pieces = [dmh.reshape(2 * N_MOD, d), dmg0, dg_rows, d_cw[:3], d_final_g[None],
              (dpw_h + dpw_g).reshape(-1, d), jnp.pad((dps_h + dps_g)[0], (0, d - pool_dim))[None],
              jnp.pad(acc_qg[0, 0], (0, d - q_rank))[None], jnp.pad(acc_kvg[0, 0], (0, d - kv_rank))[None]]
    n_piece = [p.shape[0] for p in pieces]
    small_g = jnp.concatenate(pieces, axis=0)
    n_small = small_g.shape[0]
    small_g = jnp.pad(small_g, ((0, (-n_small) % 8), (0, 0)))
    sg_all = _gather_all("gather_small_grads", [small_g])[0]
    sg_sum = _sum_lead("sum_small_grads", sg_all)
    offs = [0]
    for npc in n_piece:
        offs.append(offs[-1] + npc)
    part = lambda j: sg_sum[offs[j]:offs[j + 1]]
    sum_dmh, sum_dmg0, g_norm_full, g_conv_w_full = part(0).reshape(2, N_MOD * d), part(1).reshape(N_MOD * d), part(2), part(3)
    g_final = part(4)[0]
    g_pool_w = part(5).reshape(pool_w.shape)
    g_pool_scale = part(6)[:, :pool_dim]
    g_q_norm = part(7)[:, :q_rank]
    g_kv_norm = part(8)[:, :kv_rank]
    col0 = shard * (d // N_SHARD)
    g_norm_g = lax.dynamic_slice_in_dim(g_norm_full.reshape(2, 3, d), col0, d // N_SHARD, axis=2)
    g_conv_w = lax.dynamic_slice_in_dim(g_conv_w_full, col0, d // N_SHARD, axis=1)[None]
    g_b_mod = _sum_lead("sum_b_mod", jnp.stack([sum_dmh, jnp.stack([sum_dmg0, jnp.zeros_like(sum_dmg0)])]))

    dm16 = []
    for l in range(2):
        per_dev = sg_all[:, l * N_MOD:(l + 1) * N_MOD].reshape(N_DEV, N_MOD * d)
        row8 = sum_dmg0 if l == 0 else jnp.zeros_like(sum_dmg0)
        full = jnp.concatenate([per_dev, row8[None], jnp.zeros((7, N_MOD * d), F32)], axis=0)
        dm16.append(lax.dynamic_slice_in_dim(full, shard * n_col, n_col, axis=1))
    g_w_mod = jnp.stack([_mm(f"mod_dw_{l}", c16, dm16[l], 'tn', tm=256, tn=768, a_pre=_silu) for l in range(2)])
    dc16 = _mm("mod_dx", dm16[0], w_mod[0], 'nt', tm=16, tn=512, epi=lambda acc, i, cv: (acc * _dsilu(cv),),
               epi_args=(c16,), epi_kinds=('mn',))
    dc_all = _gather_all("gather_dc", [dc16])[0]
    g_c_ctx = _sum_lead("sum_dc", dc_all[::2])[8]

    def chunks704(l):
        return [[dwg00, dwu00], [dwg01, dwu01]] if l == 0 else [[dwg10, dwu10], [dwg11, dwu11]]

    g704 = jnp.stack([jnp.stack([jnp.stack(chunks704(l)[f]) for f in range(2)]) for l in range(2)])
    g704 = g704.transpose(3, 0, 1, 2, 4, 5).reshape(N_SHARD, 8 * d, fc)
    gdown = jnp.stack([jnp.stack([dwd00, dwd01]), jnp.stack([dwd10, dwd11])])
    gdown = gdown.transpose(2, 0, 1, 3, 4).reshape(N_SHARD, 4 * fc, d)
    g1024 = jnp.concatenate([gdown, d_w_out.reshape(N_SHARD, -1, d), d_cw_out.reshape(N_SHARD, -1, d)], axis=1)
    big = [g704, g1024, d_w_uq.reshape(N_SHARD, -1, d_w_uq.shape[1]),
           d_w_ukv.reshape(kv_rank, N_SHARD, -1).transpose(1, 0, 2),
           d_w_in.reshape(d, N_SHARD, -1).transpose(1, 0, 2),
           d_cw_in.reshape(d, N_SHARD, -1).transpose(1, 0, 2)]
    send = [b.astype(BF).reshape(N_DEV, b.shape[1] // 2, b.shape[2]) for b in big]
    pre = _presum_sibling("grads_presum", send)
    landed = _to_chips("grads_to_chips", pre)
    halves_sum = [_sum_lead(f"sum_grads_{j}", l) for j, l in enumerate(landed)]
    swapped = _swap_sibling("swap_halves", halves_sum)
    gsh = [s.reshape(2 * s.shape[1], s.shape[2]) for s in swapped]
    g_gu = gsh[0].reshape(2, 2, 2, d, fc)
    grads = dict(
        c_ctx=g_c_ctx, norm_g=g_norm_g, w_mod=g_w_mod, b_mod=g_b_mod,
        ffn_w_gate=g_gu[:, :, 0], ffn_w_up=g_gu[:, :, 1], ffn_w_down=gsh[1][:4 * fc].reshape(2, 2, fc, d),
        ab_w_in=gsh[4][None], pool_w=g_pool_w, pool_scale=g_pool_scale, q_norm_g=g_q_norm, w_uq=gsh[2][None],
        kv_norm_g=g_kv_norm, w_ukv=gsh[3][None], ab_w_out=gsh[1][4 * fc:4 * fc + n_out_rows][None],
        conv_w_in=gsh[5][None], conv_w=g_conv_w, conv_w_out=gsh[1][4 * fc + n_out_rows:][None], final_norm_g=g_final)

    names = list(weights)
    upd = {n: _adamw(f"adamw_{n}", weights[n], grads[n].reshape(weights[n].shape), mom_m[n], mom_v[n]) for n in names}
    return (loss, grad_x, *[grads[n].reshape(weights[n].shape) for n in names], *[upd[n][0] for n in names],
            *[upd[n][1] for n in names], *[upd[n][2] for n in names])
```

```python
import functools
import math

import jax
import jax.numpy as jnp
from jax import lax
from jax.experimental import pallas as pl
from jax.experimental.pallas import tpu as pltpu

F32 = jnp.float32
BF = jnp.bfloat16
MESH = pl.DeviceIdType.MESH

N_DEV = 8
N_SHARD = 4
RMS_EPS = 1e-6
N_MOD = 9
POOL_WINDOWS = (2, 4, 8, 16)
QK_NOPE = 64
QK_ROPE = 32
V_HEAD = 64
HEAD_PAD = 128
GRID_W = 64
ROPE_THETA = 10000.0
POOL_PAD = 16
ADAM_LR, ADAM_B1, ADAM_B2, ADAM_EPS, ADAM_WD, ADAM_STEP = 0.001, 0.9, 0.999, 1e-08, 0.01, 10
VMEM_LIMIT = 56 * 1024 * 1024


def _pcall(body, **kw):
    return pl.pallas_call(body, **kw)


def _params(sem=None):
    return pltpu.CompilerParams(dimension_semantics=sem, vmem_limit_bytes=VMEM_LIMIT)


def _pick(n, pref, mult=128):
    best = None
    d = mult
    while d <= min(n, pref):
        if n % d == 0:
            best = d
        d += mult
    return best if best is not None else n


def _silu(z):
    return z * jax.nn.sigmoid(z)


def _dsilu(z):
    s = jax.nn.sigmoid(z)
    return s * (1.0 + z * (1.0 - s))


def _dot(a, b, dims):
    return lax.dot_general(a.astype(BF), b.astype(BF), (dims, ((), ())), preferred_element_type=F32)


NN = ((1,), (0,))
NT = ((1,), (1,))
TN = ((0,), (0,))


ALL_FLIPS = [(kx, ky, kc) for kx in (0, 1) for ky in (0, 1) for kc in (0, 1) if (kx, ky, kc) != (0, 0, 0)]
CHIP_FLIPS = [(1, 0, 0), (0, 1, 0), (1, 1, 0)]
SIBLING = (0, 0, 1)
COMM_SPLIT = 8
SPLIT_MIN_ROWS = 256


def _exchange(name, arrays, plan, lead, whole_src, split=COMM_SPLIT):
    n = len(arrays)
    blk_shapes = [tuple(a.shape) if whole_src else tuple(a.shape[1:]) for a in arrays]
    splits = []
    for shp in blk_shapes:
        s = 1
        while s * 2 <= split and shp[0] % (s * 2) == 0 and (shp[0] // (s * 2)) % 16 == 0 \
                and shp[0] // (s * 2) >= SPLIT_MIN_ROWS:
            s *= 2
        splits.append(s)
    items = plan(0, 0, 0)
    n_items = len(items)
    remote_ids = [k for k, it in enumerate(items) if it[0] is not None]
    local_ids = [k for k, it in enumerate(items) if it[0] is None]
    slots = [(a, s) for s in range(max(splits)) for a in range(n) if s < splits[a]]
    n_slot = len(slots)

    def body(*refs):
        ins, outs = refs[:n], refs[n:2 * n]
        send_sems, recv_sems, loc_sems = refs[2 * n:]
        x, y, c = lax.axis_index("x"), lax.axis_index("y"), lax.axis_index("c")
        plan_here = plan(x, y, c)

        def rows(ref, a, s):
            rc = blk_shapes[a][0] // splits[a]
            return ref.at[pl.ds(s * rc, rc)]

        def make(si, k):
            a, s = slots[si]
            flip, src, dst, _ = plan_here[k]
            base = outs[a] if src[0] == 'out' else ins[a]
            src_ref = rows(base if src[1] is None else base.at[src[1]], a, s)
            dst_ref = rows(outs[a].at[dst], a, s)
            if flip is None:
                return pltpu.make_async_copy(src_ref, dst_ref, loc_sems.at[si * max(1, len(local_ids)) + local_ids.index(k)])
            peer = (1 - x if flip[0] else x, 1 - y if flip[1] else y, 1 - c if flip[2] else c)
            sem = si * len(remote_ids) + remote_ids.index(k)
            return pltpu.make_async_remote_copy(src_ref=src_ref, dst_ref=dst_ref, send_sem=send_sems.at[sem],
                                                recv_sem=recv_sems.at[sem], device_id=peer, device_id_type=MESH)

        copies = {}
        for si in range(n_slot):
            for k in range(n_items):
                if plan_here[k][3] is None:
                    copies[si, k] = make(si, k)
                    copies[si, k].start()
        arrived = set()
        for si in range(n_slot):
            for k in range(n_items):
                after = plan_here[k][3]
                if after is not None:
                    if (si, after) not in arrived:
                        copies[si, after].wait_recv()
                        arrived.add((si, after))
                    copies[si, k] = make(si, k)
                    copies[si, k].start()
        for (si, k), cp in copies.items():
            if plan_here[k][0] is None:
                cp.wait()
            else:
                cp.wait_send()
                if (si, k) not in arrived:
                    cp.wait_recv()

    any_spec = pl.BlockSpec(memory_space=pl.ANY)
    n_rem = max(1, n_slot * len(remote_ids))
    outs = _pcall(
        body, name=name,
        out_shape=[jax.ShapeDtypeStruct((lead,) + s, a.dtype) for s, a in zip(blk_shapes, arrays)],
        in_specs=[any_spec] * n, out_specs=[any_spec] * n,
        scratch_shapes=[pltpu.SemaphoreType.DMA((n_rem,)), pltpu.SemaphoreType.DMA((n_rem,)),
                        pltpu.SemaphoreType.DMA((max(1, n_slot * len(local_ids)),))],
    )(*arrays)
    return list(outs)


def _place(x, y, c):
    return 4 * x + 2 * y + c


def _flip(v, f):
    return 1 - v if f else v


def _gather_all(name, arrays):
    def plan(x, y, c):
        me = _place(x, y, c)
        return [(None, ('in', None), me, None)] + [(f, ('in', None), me, None) for f in ALL_FLIPS]
    return _exchange(name, arrays, plan, N_DEV, True)


STREAM_SLOTS = 4
STREAM_LAG = 2
STREAM_CHUNK_BYTES = 420 * 1024


def _stream_rows(rows, cdim, itemsize):
    best = None
    for rc in range(16, rows + 1, 16):
        if rows % rc == 0 and rc * cdim * itemsize <= STREAM_CHUNK_BYTES:
            best = rc
    return rows if best is None else best


def _stream_scratch(chunk_specs, combine):
    scratch = []
    for rc, cdim, dt in chunk_specs:
        scratch += [pltpu.VMEM((STREAM_SLOTS, rc, cdim), dt), pltpu.VMEM((STREAM_SLOTS, rc, cdim), dt)]
        if combine:
            scratch += [pltpu.VMEM((STREAM_SLOTS, rc, cdim), dt), pltpu.VMEM((STREAM_SLOTS, rc, cdim), BF)]
    n = len(chunk_specs)
    scratch += [pltpu.SemaphoreType.DMA((n * STREAM_SLOTS,)) for _ in range(5)]
    scratch.append(pltpu.SemaphoreType.REGULAR((n,)))
    return scratch


def _run_stream(peer, chunks, bufs, sems, a, combine):
    ns, lag, k_all = STREAM_SLOTS, STREAM_LAG, len(chunks)
    load_s, send_s, recv_s, store_s, own_s, credits = sems
    credit = credits.at[a]
    if combine:
        send_buf, recv_buf, own_buf, res_buf = bufs
    else:
        send_buf, recv_buf = bufs

    def sem(ref, i):
        return ref.at[a * ns + i % ns]

    def load(i):
        return pltpu.make_async_copy(chunks[i]['src'], send_buf.at[i % ns], sem(load_s, i))

    def own_load(i):
        return pltpu.make_async_copy(chunks[i]['own'], own_buf.at[i % ns], sem(own_s, i))

    def remote(i):
        return pltpu.make_async_remote_copy(src_ref=send_buf.at[i % ns], dst_ref=recv_buf.at[i % ns],
                                            send_sem=sem(send_s, i), recv_sem=sem(recv_s, i),
                                            device_id=peer, device_id_type=MESH)

    def store(i):
        return pltpu.make_async_copy((res_buf if combine else recv_buf).at[i % ns], chunks[i]['dst'], sem(store_s, i))

    def before(i):
        if chunks[i].get('pre') is not None:
            chunks[i]['pre']()

    before(0)
    load(0).start()
    if combine:
        own_load(0).start()
    for i in range(k_all + lag):
        if i < k_all:
            load(i).wait()
            if i >= ns:
                pl.semaphore_wait(credit, 1)
            remote(i).start()
            if i + 1 < k_all:
                if i + 1 >= ns:
                    remote(i + 1 - ns).wait_send()
                before(i + 1)
                load(i + 1).start()
        r = i - lag
        if r >= 0:
            remote(r).wait_recv()
            if r >= 1:
                store(r - 1).wait()
                if r - 1 + ns < k_all:
                    pl.semaphore_signal(credit, inc=1, device_id=peer, device_id_type=MESH)
            if combine:
                own_load(r).wait()
                res_buf[r % ns] = (recv_buf[r % ns].astype(F32) + own_buf[r % ns].astype(F32)).astype(BF)
                if r + 1 < k_all:
                    own_load(r + 1).start()
            store(r).start()
    store(k_all - 1).wait()
    for i in range(max(0, k_all - ns), k_all):
        remote(i).wait_send()


def _gather_halves(name, arrays):
    n = len(arrays)
    blk = [tuple(a.shape[1:]) for a in arrays]
    splits = []
    for shp in blk:
        s = 1
        while s * 2 <= COMM_SPLIT and shp[0] % (s * 2) == 0 and (shp[0] // (s * 2)) % 16 == 0 \
                and shp[0] // (s * 2) >= SPLIT_MIN_ROWS:
            s *= 2
        splits.append(s)
    rcs = [_stream_rows(shp[0] // s, shp[1], a.dtype.itemsize) for shp, s, a in zip(blk, splits, arrays)]
    slots = [(a, s) for a in range(n) for s in range(splits[a])]
    nch = len(CHIP_FLIPS)

    def body(*refs):
        ins, outs = refs[:n], refs[n:2 * n]
        bufs = refs[2 * n:4 * n]
        sems = refs[4 * n:4 * n + 6]
        ici_send, ici_recv = refs[4 * n + 6:]
        x, y, c = lax.axis_index("x"), lax.axis_index("y"), lax.axis_index("c")
        chip = 2 * x + y
        ici = {}
        for si, (a, s) in enumerate(slots):
            rows = pl.ds(s * (blk[a][0] // splits[a]), blk[a][0] // splits[a])
            for j, f in enumerate(CHIP_FLIPS):
                cp = pltpu.make_async_remote_copy(
                    src_ref=ins[a].at[c, rows], dst_ref=outs[a].at[2 * chip + c, rows],
                    send_sem=ici_send.at[si * nch + j], recv_sem=ici_recv.at[si * nch + j],
                    device_id=(_flip(x, f[0]), _flip(y, f[1]), c), device_id_type=MESH)
                cp.start()
                ici[a, s, j] = cp
        for a in range(n):
            chunks = []
            per = blk[a][0] // splits[a]
            for s in range(splits[a]):
                for j, f in enumerate(CHIP_FLIPS):
                    other = 2 * (2 * _flip(x, f[0]) + _flip(y, f[1]))
                    for k in range(per // rcs[a]):
                        rows = pl.ds(s * per + k * rcs[a], rcs[a])
                        chunks.append(dict(src=outs[a].at[other + c, rows], dst=outs[a].at[other + 1 - c, rows],
                                           pre=ici[a, s, j].wait_recv if k == 0 else None))
            _run_stream((x, y, 1 - c), chunks, bufs[2 * a:2 * a + 2], sems, a, False)
        for cp in ici.values():
            cp.wait_send()

    any_spec = pl.BlockSpec(memory_space=pl.ANY)
    scratch = _stream_scratch([(rc, shp[1], a.dtype) for rc, shp, a in zip(rcs, blk, arrays)], False)
    scratch += [pltpu.SemaphoreType.DMA((len(slots) * nch,)), pltpu.SemaphoreType.DMA((len(slots) * nch,))]
    outs = _pcall(body, name=name,
                  out_shape=[jax.ShapeDtypeStruct((N_DEV,) + shp, a.dtype) for shp, a in zip(blk, arrays)],
                  in_specs=[any_spec] * n, out_specs=[any_spec] * n, scratch_shapes=scratch,
                  compiler_params=_params())(*arrays)
    first = 2 * (2 * lax.axis_index("x") + lax.axis_index("y"))
    return [lax.dynamic_update_slice_in_dim(o, a, first, 0) for o, a in zip(outs, arrays)]


def _presum_sibling(name, arrays):
    n = len(arrays)
    blk = [tuple(a.shape[1:]) for a in arrays]
    rcs = [_stream_rows(shp[0], shp[1], a.dtype.itemsize) for shp, a in zip(blk, arrays)]

    def body(*refs):
        ins, outs = refs[:n], refs[n:2 * n]
        bufs = refs[2 * n:6 * n]
        sems = refs[6 * n:6 * n + 6]
        x, y, c = lax.axis_index("x"), lax.axis_index("y"), lax.axis_index("c")
        for a in range(n):
            chunks = []
            for sh in range(N_SHARD):
                for k in range(blk[a][0] // rcs[a]):
                    rows = pl.ds(k * rcs[a], rcs[a])
                    chunks.append(dict(src=ins[a].at[2 * sh + 1 - c, rows], own=ins[a].at[2 * sh + c, rows],
                                       dst=outs[a].at[sh, rows]))
            _run_stream((x, y, 1 - c), chunks, bufs[4 * a:4 * a + 4], sems, a, True)

    any_spec = pl.BlockSpec(memory_space=pl.ANY)
    scratch = _stream_scratch([(rc, shp[1], a.dtype) for rc, shp, a in zip(rcs, blk, arrays)], True)
    outs = _pcall(body, name=name,
                  out_shape=[jax.ShapeDtypeStruct((N_SHARD,) + shp, BF) for shp in blk],
                  in_specs=[any_spec] * n, out_specs=[any_spec] * n, scratch_shapes=scratch,
                  compiler_params=_params())(*arrays)
    return list(outs)


def _to_chips(name, arrays):
    def plan(x, y, c):
        return [(f, ('in', 2 * _flip(x, f[0]) + _flip(y, f[1])), 2 * x + y, None) for f in CHIP_FLIPS]
    outs = _exchange(name, arrays, plan, N_SHARD, False)
    chip = 2 * lax.axis_index("x") + lax.axis_index("y")
    return [lax.dynamic_update_slice_in_dim(o, lax.dynamic_slice_in_dim(a, chip, 1, 0), chip, 0)
            for o, a in zip(outs, arrays)]


def _swap_sibling(name, arrays):
    n = len(arrays)
    rcs = [_stream_rows(a.shape[0], a.shape[1], a.dtype.itemsize) for a in arrays]

    def body(*refs):
        ins, outs = refs[:n], refs[n:2 * n]
        bufs = refs[2 * n:4 * n]
        sems = refs[4 * n:4 * n + 6]
        x, y, c = lax.axis_index("x"), lax.axis_index("y"), lax.axis_index("c")
        for a in range(n):
            chunks = [dict(src=ins[a].at[pl.ds(k * rcs[a], rcs[a])], dst=outs[a].at[pl.ds(k * rcs[a], rcs[a])])
                      for k in range(arrays[a].shape[0] // rcs[a])]
            _run_stream((x, y, 1 - c), chunks, bufs[2 * a:2 * a + 2], sems, a, False)

    any_spec = pl.BlockSpec(memory_space=pl.ANY)
    scratch = _stream_scratch([(rc, a.shape[1], a.dtype) for rc, a in zip(rcs, arrays)], False)
    got = _pcall(body, name=name, out_shape=[jax.ShapeDtypeStruct(a.shape, a.dtype) for a in arrays],
                 in_specs=[any_spec] * n, out_specs=[any_spec] * n, scratch_shapes=scratch,
                 compiler_params=_params())(*arrays)
    south = lax.axis_index("c") == 0
    return [jnp.where(south, jnp.stack([a, g]), jnp.stack([g, a])) for a, g in zip(arrays, got)]


def _sum_lead(name, arr, out_dtype=F32):
    n, r, cdim = arr.shape
    tr = r
    limit = (4 << 20) // (n * cdim * arr.dtype.itemsize)
    if r > limit:
        tr = _pick(r, max(limit, 16), 16)

    def body(x_ref, o_ref):
        acc = x_ref[0].astype(F32)
        for d in range(1, n):
            acc = acc + x_ref[d].astype(F32)
        o_ref[...] = acc.astype(out_dtype)

    return _pcall(body, name=name, grid=(r // tr,),
                  in_specs=[pl.BlockSpec((n, tr, cdim), lambda i: (0, i, 0))],
                  out_specs=pl.BlockSpec((tr, cdim), lambda i: (i, 0)),
                  out_shape=jax.ShapeDtypeStruct((r, cdim), out_dtype),
                  compiler_params=_params(("arbitrary",)))(arr)


def _rows_call(name, fn, n_rows, tm, rows, consts, mod, outs, acc_w=None, h_tiles=None):
    nt = n_rows // tm
    ht = nt if h_tiles is None else h_tiles
    ng = 1 if mod is None else mod.shape[0]
    n_r, n_c, n_o = len(rows), len(consts), len(outs)
    has_mod = mod is not None

    def body(*refs):
        i = pl.program_id(0)
        first = (i % ht) == 0
        row_refs, const_refs = refs[:n_r], refs[n_r:n_r + n_c]
        p = n_r + n_c
        mod_tile = refs[p][...] if has_mod else None
        p += int(has_mod)
        out_refs = refs[p:p + n_o]
        o, acc = fn([r[...] for r in row_refs], [r[...] for r in const_refs], mod_tile)
        for r, v in zip(out_refs, o):
            r[...] = v.astype(r.dtype)
        if acc_w is not None:
            acc_ref = refs[p + n_o]

            @pl.when(first)
            def _():
                acc_ref[...] = jnp.zeros_like(acc_ref)

            for k, v in acc.items():
                acc_ref[k:k + 1, :] += v

    in_specs = [pl.BlockSpec((tm, r.shape[1]), lambda i: (i, 0)) for r in rows]
    in_specs += [pl.BlockSpec(cst.shape, lambda i, nd=cst.ndim: (0,) * nd) for cst in consts]
    args = list(rows) + list(consts)
    if has_mod:
        in_specs.append(pl.BlockSpec((None,) + mod.shape[1:], lambda i: (i // ht, 0, 0)))
        args.append(mod)
    out_shape = [jax.ShapeDtypeStruct((n_rows, w), dt) for w, dt in outs]
    out_specs = [pl.BlockSpec((tm, w), lambda i: (i, 0)) for w, _ in outs]
    if acc_w is not None:
        out_shape.append(jax.ShapeDtypeStruct((ng, 8, acc_w), F32))
        out_specs.append(pl.BlockSpec((None, 8, acc_w), lambda i: (i // ht, 0, 0)))
    res = _pcall(body, name=name, grid=(nt,), in_specs=in_specs, out_specs=out_specs, out_shape=out_shape,
                 compiler_params=_params(("arbitrary",)))(*args)
    return list(res)


def _rms(s):
    r = lax.rsqrt(jnp.mean(s * s, axis=1, keepdims=True) + RMS_EPS)
    return s * r, r


def _rms_bwd(dn, n, r):
    return r * (dn - n * jnp.mean(dn * n, axis=1, keepdims=True))


def _adaln_fwd(name, s, gains, gain_row, mod, k, tm, h_tiles):
    def fn(rows, consts, m):
        n, _ = _rms(rows[0])
        y = n * consts[0][gain_row:gain_row + 1, :]
        return [y * (1.0 + m[3 * k + 1:3 * k + 2, :]) + m[3 * k:3 * k + 1, :]], {}

    d = s.shape[1]
    return _rows_call(name, fn, s.shape[0], tm, [s], [gains], mod, [(d, BF)], h_tiles=h_tiles)[0]


def _adaln_bwd(name, s, du, ds_res, gains, gain_row, mod, k, tm, h_tiles):
    def fn(rows, consts, m):
        sv, duv, res = rows
        gain = consts[0][gain_row:gain_row + 1, :]
        n, r = _rms(sv)
        y = n * gain
        dy = duv * (1.0 + m[3 * k + 1:3 * k + 2, :])
        acc = {0: jnp.sum(duv, axis=0, keepdims=True), 1: jnp.sum(duv * y, axis=0, keepdims=True),
               2: jnp.sum(dy * n, axis=0, keepdims=True)}
        return [_rms_bwd(dy * gain, n, r) + res], acc

    d = s.shape[1]
    return _rows_call(name, fn, s.shape[0], tm, [s, du, ds_res], [gains], mod, [(d, F32)], acc_w=d, h_tiles=h_tiles)


def _resid_bwd(name, ds_out, o, mod, k, cst, tm, h_tiles):
    def fn(rows, consts, m):
        dsv, ov = rows
        gate = m[3 * k + 2:3 * k + 3, :]
        return [cst * gate * dsv], {0: jnp.sum(cst * ov * dsv, axis=0, keepdims=True)}

    d = o.shape[1]
    return _rows_call(name, fn, o.shape[0], tm, [ds_out, o], [], mod, [(d, BF)], acc_w=d, h_tiles=h_tiles)


def _mm(name, a, b, mode, tm=256, tn=512, out_dtypes=(F32,), epi=None, epi_args=(), epi_kinds=(), a_pre=None):
    if mode == 'nn':
        (m, kd), nd = a.shape, b.shape[1]
    elif mode == 'nt':
        (m, kd), nd = a.shape, b.shape[0]
    else:
        (kd, m), nd = a.shape, b.shape[1]
    tm = _pick(m, tm, 16) if m % tm else tm
    tn = _pick(nd, tn, 128) if nd % tn else tn
    dims = {'nn': NN, 'nt': NT, 'tn': TN}[mode]
    n_e, n_o = len(epi_args), len(out_dtypes)

    def body(*refs):
        i = pl.program_id(1)
        av = refs[0][...]
        if a_pre is not None:
            av = a_pre(av)
        acc = _dot(av, refs[1][...], dims)
        res = (acc,) if epi is None else epi(acc, i, *[r[...] for r in refs[2:2 + n_e]])
        for r, v in zip(refs[2 + n_e:], res):
            r[...] = v.astype(r.dtype)

    if mode == 'nn':
        specs = [pl.BlockSpec((tm, kd), lambda j, i: (i, 0)), pl.BlockSpec((kd, tn), lambda j, i: (0, j))]
    elif mode == 'nt':
        specs = [pl.BlockSpec((tm, kd), lambda j, i: (i, 0)), pl.BlockSpec((tn, kd), lambda j, i: (j, 0))]
    else:
        specs = [pl.BlockSpec((kd, tm), lambda j, i: (0, i)), pl.BlockSpec((kd, tn), lambda j, i: (0, j))]
    for arr, kind in zip(epi_args, epi_kinds):
        if kind == 'mn':
            specs.append(pl.BlockSpec((tm, tn), lambda j, i: (i, j)))
        elif kind == 'n':
            specs.append(pl.BlockSpec((1, tn), lambda j, i: (0, j)))
        elif kind == 'mt':
            specs.append(pl.BlockSpec((tm, arr.shape[1]), lambda j, i: (i, 0)))
        else:
            specs.append(pl.BlockSpec(arr.shape, lambda j, i, nd_=arr.ndim: (0,) * nd_))
    res = _pcall(body, name=name, grid=(nd // tn, m // tm), in_specs=specs,
                 out_specs=[pl.BlockSpec((tm, tn), lambda j, i: (i, j))] * n_o,
                 out_shape=[jax.ShapeDtypeStruct((m, nd), dt) for dt in out_dtypes],
                 compiler_params=_params(("arbitrary", "arbitrary")))(a, b, *epi_args)
    return res[0] if n_o == 1 else list(res)


def _row_gate(mod, k3, i, tm, n_lat):
    g0 = mod[0, k3:k3 + 1, :]
    if mod.shape[0] == 1:
        return g0
    rid = i * tm + lax.broadcasted_iota(jnp.int32, (tm, 1), 0)
    return jnp.where(rid < n_lat, g0, mod[1, k3:k3 + 1, :])


def _ffn_up(name, u, wg, wu, base, tm):
    r, d = u.shape
    nch, _, _, fc = wg.shape

    def body(u_ref, wg_ref, wu_ref, a_ref, b_ref, h_ref):
        uv = u_ref[...]
        a = _dot(uv, wg_ref[...], NN)
        b = _dot(uv, wu_ref[...], NN)
        a_ref[...] = a.astype(BF)
        b_ref[...] = b.astype(BF)
        h_ref[...] = (_silu(a) * b).astype(BF)

    chunk = pl.BlockSpec((None, tm, fc), lambda j, i: (j, i, 0))
    return _pcall(body, name=name, grid=(nch, r // tm),
                  in_specs=[pl.BlockSpec((tm, d), lambda j, i: (i, 0)),
                            pl.BlockSpec((None, None, d, fc), lambda j, i: (j, base, 0, 0)),
                            pl.BlockSpec((None, None, d, fc), lambda j, i: (j, base, 0, 0))],
                  out_specs=[chunk] * 3, out_shape=[jax.ShapeDtypeStruct((nch, r, fc), BF)] * 3,
                  compiler_params=_params(("arbitrary", "arbitrary")))(u, wg, wu)


def _ffn_down(name, hid, wd, wd_blk, s, mod, k, n_lat, tm):
    nch, r, fc = hid.shape
    d = wd.shape[2]

    def body(h_ref, w_ref, s_ref, m_ref, so_ref, o_ref, acc_ref):
        i, j = pl.program_id(0), pl.program_id(1)
        part = _dot(h_ref[...], w_ref[...], NN)

        @pl.when(j == 0)
        def _():
            acc_ref[...] = part

        @pl.when(j > 0)
        def _():
            acc_ref[...] += part

        @pl.when(j == nch - 1)
        def _():
            o = acc_ref[...]
            o_ref[...] = o
            so_ref[...] = s_ref[...] + 0.5 * _row_gate(m_ref[...], 3 * k + 2, i, tm, n_lat) * o

    row = pl.BlockSpec((tm, d), lambda i, j: (i, 0))
    return _pcall(body, name=name, grid=(r // tm, nch),
                  in_specs=[pl.BlockSpec((None, tm, fc), lambda i, j: (j, i, 0)),
                            pl.BlockSpec((None, fc, d), lambda i, j: (j, wd_blk, 0)), row,
                            pl.BlockSpec(mod.shape, lambda i, j: (0, 0, 0))],
                  out_specs=[row, row], out_shape=[jax.ShapeDtypeStruct((r, d), F32)] * 2,
                  scratch_shapes=[pltpu.VMEM((tm, d), F32)],
                  compiler_params=_params(("arbitrary", "arbitrary")))(hid, wd, s, mod)


def _ffn_dhid(name, d_o, wd, wd_blk, a, b, tm):
    r, d = d_o.shape
    nch, _, fc = a.shape

    def body(g_ref, w_ref, a_ref, b_ref, da_ref, db_ref):
        dh = _dot(g_ref[...], w_ref[...], NT)
        av, bv = a_ref[...].astype(F32), b_ref[...].astype(F32)
        da_ref[...] = (dh * bv * _dsilu(av)).astype(BF)
        db_ref[...] = (dh * _silu(av)).astype(BF)

    chunk = pl.BlockSpec((None, tm, fc), lambda j, i: (j, i, 0))
    return _pcall(body, name=name, grid=(nch, r // tm),
                  in_specs=[pl.BlockSpec((tm, d), lambda j, i: (i, 0)),
                            pl.BlockSpec((None, fc, d), lambda j, i: (j, wd_blk, 0)), chunk, chunk],
                  out_specs=[chunk] * 2, out_shape=[jax.ShapeDtypeStruct((nch, r, fc), BF)] * 2,
                  compiler_params=_params(("arbitrary", "arbitrary")))(d_o, wd, a, b)


def _ffn_du(name, da, db, wg, wu, base, tm):
    nch, r, fc = da.shape
    d = wg.shape[2]

    def body(da_ref, db_ref, wg_ref, wu_ref, o_ref, acc_ref):
        j = pl.program_id(1)
        part = _dot(da_ref[...], wg_ref[...], NT) + _dot(db_ref[...], wu_ref[...], NT)

        @pl.when(j == 0)
        def _():
            acc_ref[...] = part

        @pl.when(j > 0)
        def _():
            acc_ref[...] += part

        @pl.when(j == nch - 1)
        def _():
            o_ref[...] = acc_ref[...]

    chunk = pl.BlockSpec((None, tm, fc), lambda i, j: (j, i, 0))
    return _pcall(body, name=name, grid=(r // tm, nch),
                  in_specs=[chunk, chunk, pl.BlockSpec((None, None, d, fc), lambda i, j: (j, base, 0, 0)),
                            pl.BlockSpec((None, None, d, fc), lambda i, j: (j, base, 0, 0))],
                  out_specs=pl.BlockSpec((tm, d), lambda i, j: (i, 0)),
                  out_shape=jax.ShapeDtypeStruct((r, d), F32), scratch_shapes=[pltpu.VMEM((tm, d), F32)],
                  compiler_params=_params(("arbitrary", "arbitrary")))(da, db, wg, wu)


def _ffn_dw_in(name, u, dz, tmm, grads, idx):
    r, d = u.shape
    nch, _, fc = dz.shape
    nb = d // tmm

    def body(u_ref, z_ref, g_ref, o_ref):
        o_ref[...] = _dot(u_ref[...], z_ref[...], TN).astype(o_ref.dtype)

    return _pcall(body, name=name, grid=(nch, nb),
                  in_specs=[pl.BlockSpec((r, tmm), lambda j, mi: (0, mi)),
                            pl.BlockSpec((None, r, fc), lambda j, mi: (j, 0, 0)),
                            pl.BlockSpec(memory_space=pl.ANY)],
                  out_specs=pl.BlockSpec((None, tmm, fc), lambda j, mi: (j, idx * nb + mi, 0)),
                  out_shape=jax.ShapeDtypeStruct(grads.shape, grads.dtype), input_output_aliases={2: 0},
                  compiler_params=_params(("arbitrary", "arbitrary")))(u, dz, grads)


def _ffn_dw_down(name, hid, d_o, tn, grads, idx):
    nch, r, fc = hid.shape
    d = d_o.shape[1]

    def body(h_ref, g_ref, acc_ref, o_ref):
        o_ref[...] = _dot(h_ref[...], g_ref[...], TN).astype(o_ref.dtype)

    return _pcall(body, name=name, grid=(nch, d // tn),
                  in_specs=[pl.BlockSpec((None, r, fc), lambda j, ni: (j, 0, 0)),
                            pl.BlockSpec((r, tn), lambda j, ni: (0, ni)),
                            pl.BlockSpec(memory_space=pl.ANY)],
                  out_specs=pl.BlockSpec((None, fc, tn), lambda j, ni: (j, idx, ni)),
                  out_shape=jax.ShapeDtypeStruct(grads.shape, grads.dtype), input_output_aliases={2: 0},
                  compiler_params=_params(("arbitrary", "arbitrary")))(hid, d_o, grads)


def _partner(x):
    n = x.shape[1]
    lane = lax.broadcasted_iota(jnp.int32, x.shape, 1)
    return jnp.where((lane & 15) < 8, pltpu.roll(x, n - 8, 1), pltpu.roll(x, 8, 1))


def _rope(x, ct, st):
    reps = x.shape[1] // ct.shape[1]
    if reps > 1:
        ct, st = jnp.tile(ct, (1, reps)), jnp.tile(st, (1, reps))
    return x * ct + _partner(x) * st


def _rope_t(dy, ct, st):
    reps = dy.shape[1] // ct.shape[1]
    if reps > 1:
        ct, st = jnp.tile(ct, (1, reps)), jnp.tile(st, (1, reps))
    return dy * ct + _partner(dy * st)


def _rope_tables(t_len, g_len, lane0):
    half = QK_ROPE // 4
    pos = jnp.arange(t_len)
    row = (pos // GRID_W).astype(F32)
    col = (pos % GRID_W).astype(F32)
    freqs = jnp.power(ROPE_THETA, -jnp.arange(0, QK_ROPE // 2, 2, dtype=F32) / (QK_ROPE // 2))
    ang_r, ang_c = row[:, None] * freqs, col[:, None] * freqs
    cs = jnp.concatenate([jnp.cos(ang_r)] * 2 + [jnp.cos(ang_c)] * 2, axis=1)
    sn = jnp.concatenate([-jnp.sin(ang_r), jnp.sin(ang_r), -jnp.sin(ang_c), jnp.sin(ang_c)], axis=1)
    assert cs.shape[1] == 4 * half == QK_ROPE
    ct = jnp.ones((t_len + g_len, HEAD_PAD), F32).at[:t_len, lane0:lane0 + QK_ROPE].set(cs)
    st = jnp.zeros((t_len + g_len, HEAD_PAD), F32).at[:t_len, lane0:lane0 + QK_ROPE].set(sn)
    return ct, st


def _attn_fwd(name, q, kp, vp, n_q, q_off, n_k, k_blk, heads, tq, scale):
    qb = q_off // tq

    def body(q_ref, k_ref, v_ref, o_ref, l_ref):
        s = _dot(q_ref[...], k_ref[...], NT) * scale
        m = jnp.max(s, axis=1, keepdims=True)
        p = jnp.exp(s - m)
        l = jnp.sum(p, axis=1, keepdims=True)
        o_ref[...] = (_dot(p, v_ref[...], NN) / l).astype(BF)
        l_ref[...] = jnp.broadcast_to(m + jnp.log(l), l_ref.shape)

    hw = heads * HEAD_PAD
    blk = pl.BlockSpec((tq, HEAD_PAD), lambda h, i: (i, h))
    kv = pl.BlockSpec((n_k, HEAD_PAD), lambda h, i: (k_blk, h))
    return _pcall(body, name=name, grid=(heads, n_q // tq),
                  in_specs=[pl.BlockSpec((tq, HEAD_PAD), lambda h, i: (i + qb, h)), kv, kv],
                  out_specs=[blk, blk],
                  out_shape=[jax.ShapeDtypeStruct((n_q, hw), BF), jax.ShapeDtypeStruct((n_q, hw), F32)],
                  compiler_params=_params(("arbitrary", "arbitrary")))(q, kp, vp)


def _attn_bwd(name, q, kp, vp, cat, dcat, lse, n_q, q_off, n_k, k_blk, heads, tq, scale, col_blk):
    qb = q_off // tq

    def body(q_ref, k_ref, v_ref, o_ref, do_ref, l_ref, dq_ref, dk_ref, dv_ref):
        i = pl.program_id(1)
        qv, kv_, vv = q_ref[...], k_ref[...], v_ref[...]
        dov = do_ref[...]
        s = _dot(qv, kv_, NT) * scale
        p = jnp.exp(s - l_ref[...][:, 0:1])
        dp = _dot(dov, vv, NT)
        delta = jnp.sum(dov * o_ref[...].astype(F32), axis=1, keepdims=True)
        ds = (p * (dp - delta) * scale).astype(BF)
        dq_ref[...] = _dot(ds, kv_, NN)
        dk = _dot(ds, qv, TN)
        dv = _dot(p, dov, TN)

        @pl.when(i == 0)
        def _():
            dk_ref[...] = dk
            dv_ref[...] = dv

        @pl.when(i > 0)
        def _():
            dk_ref[...] += dk
            dv_ref[...] += dv

    hw = heads * HEAD_PAD
    qspec = pl.BlockSpec((tq, HEAD_PAD), lambda h, i: (i + qb, h))
    cspec = pl.BlockSpec((tq, HEAD_PAD), lambda h, i: (i + qb, col_blk + h))
    kv = pl.BlockSpec((n_k, HEAD_PAD), lambda h, i: (k_blk, h))
    acc = pl.BlockSpec((n_k, HEAD_PAD), lambda h, i: (0, h))
    blk = pl.BlockSpec((tq, HEAD_PAD), lambda h, i: (i, h))
    return _pcall(body, name=name, grid=(heads, n_q // tq),
                  in_specs=[qspec, kv, kv, cspec, cspec, blk], out_specs=[blk, acc, acc],
                  out_shape=[jax.ShapeDtypeStruct((n_q, hw), F32), jax.ShapeDtypeStruct((n_k, hw), F32),
                             jax.ShapeDtypeStruct((n_k, hw), F32)],
                  compiler_params=_params(("arbitrary", "arbitrary")))(q, kp, vp, cat, dcat, lse)


def _shift(x, k):
    return pltpu.roll(x, k % x.shape[0], 0)


def _window_sum(v, w, mirrored):
    n, gd = v.shape
    pad = jnp.zeros((POOL_PAD, gd), F32)
    e = jnp.concatenate([pad, v, pad], axis=0)
    acc = e + _shift(e, -1 if mirrored else 1)
    step = 1
    while 2 * step < w:
        acc = _shift(acc, step) + _shift(acc, -step)
        step *= 2
    return acc[POOL_PAD:POOL_PAD + n]


def _window_count(n, w):
    t = lax.broadcasted_iota(jnp.int32, (n, 1), 0)
    lo = jnp.maximum(t - w // 2, 0)
    hi = jnp.minimum(t + (w - w // 2 - 1), n - 1)
    return (hi - lo + 1).astype(F32)


def _pool_fwd(name, u, pool_w, scale):
    n, pd = u.shape
    ng = len(POOL_WINDOWS)
    gd = pd // ng

    def body(u_ref, w_ref, s_ref, y_ref):
        for g, w in enumerate(POOL_WINDOWS):
            sl = slice(g * gd, (g + 1) * gd)
            ug = u_ref[:, sl]
            p = _window_sum(ug, w, False) / _window_count(n, w) - ug
            y_ref[:, sl] = (_dot(p, w_ref[g], NN) * s_ref[:, sl]).astype(BF)

    return _pcall(body, name=name, out_shape=jax.ShapeDtypeStruct((n, pd), BF),
                  compiler_params=_params())(u, pool_w, scale)


def _pool_bwd(name, u, dcat, pool_w, scale, row_off):
    n, pd = u.shape
    ng = len(POOL_WINDOWS)
    gd = pd // ng

    def body(u_ref, dy_ref, w_ref, s_ref, du_ref, dw_ref, ds_ref):
        ds_ref[...] = jnp.zeros_like(ds_ref)
        for g, w in enumerate(POOL_WINDOWS):
            sl = slice(g * gd, (g + 1) * gd)
            ug, dy, wg = u_ref[:, sl], dy_ref[:, sl], w_ref[g]
            cnt = _window_count(n, w)
            p = _window_sum(ug, w, False) / cnt - ug
            ds_ref[0:1, sl] = jnp.sum(dy * _dot(p, wg, NN), axis=0, keepdims=True)
            dys = dy * s_ref[:, sl]
            dw_ref[g] = _dot(p, dys, TN)
            dp = _dot(dys, wg, NT)
            du_ref[:, sl] = (_window_sum(dp / cnt, w, True) - dp).astype(BF)

    rb = row_off // n
    return _pcall(body, name=name, grid=(1,),
                  in_specs=[pl.BlockSpec((n, pd), lambda i: (0, 0)), pl.BlockSpec((n, pd), lambda i: (rb, 0)),
                            pl.BlockSpec(pool_w.shape, lambda i: (0, 0, 0)), pl.BlockSpec(scale.shape, lambda i: (0, 0))],
                  out_specs=[pl.BlockSpec((n, pd), lambda i: (0, 0)), pl.BlockSpec((ng, gd, gd), lambda i: (0, 0, 0)),
                             pl.BlockSpec((8, pd), lambda i: (0, 0))],
                  out_shape=[jax.ShapeDtypeStruct((n, pd), BF), jax.ShapeDtypeStruct((ng, gd, gd), F32),
                             jax.ShapeDtypeStruct((8, pd), F32)],
                  compiler_params=_params(("arbitrary",)))(u, dcat, pool_w, scale)


def _edge_shift(z, k):
    n = z.shape[0]
    t = lax.broadcasted_iota(jnp.int32, (n, 1), 0)
    keep = (t >= k) if k > 0 else (t < n + k)
    return jnp.where(keep, pltpu.roll(z, k % n, 0), 0.0)


def _conv_fwd(name, p3, cw, tc):
    n, cd = p3.shape[0], p3.shape[1] // 3
    nb = cd // tc

    def body(b_ref, c_ref, v_ref, w_ref, y_ref):
        z = c_ref[...] * v_ref[...]
        w = w_ref[...]
        zc = w[0:1] * _edge_shift(z, 1) + w[1:2] * z + w[2:3] * _edge_shift(z, -1)
        y_ref[...] = (b_ref[...] * zc).astype(BF)

    return _pcall(body, name=name, grid=(nb,),
                  in_specs=[pl.BlockSpec((n, tc), lambda j: (0, j)), pl.BlockSpec((n, tc), lambda j: (0, nb + j)),
                            pl.BlockSpec((n, tc), lambda j: (0, 2 * nb + j)), pl.BlockSpec((3, tc), lambda j: (0, j))],
                  out_specs=pl.BlockSpec((n, tc), lambda j: (0, j)), out_shape=jax.ShapeDtypeStruct((n, cd), BF),
                  compiler_params=_params(("arbitrary",)))(p3, p3, p3, cw)


def _conv_bwd(name, p3, cw, dy, tc):
    n, cd = dy.shape
    nb = cd // tc

    def body(b_ref, c_ref, v_ref, w_ref, dy_ref, dp_ref, dw_ref):
        cv, vv, w, dyv = c_ref[...], v_ref[...], w_ref[...], dy_ref[...]
        z = cv * vv
        zl, zr = _edge_shift(z, 1), _edge_shift(z, -1)
        zc = w[0:1] * zl + w[1:2] * z + w[2:3] * zr
        dzc = dyv * b_ref[...]
        dz = w[0:1] * _edge_shift(dzc, -1) + w[1:2] * dzc + w[2:3] * _edge_shift(dzc, 1)
        dp_ref[0] = (dyv * zc).astype(BF)
        dp_ref[1] = (dz * vv).astype(BF)
        dp_ref[2] = (dz * cv).astype(BF)
        dw_ref[...] = jnp.zeros_like(dw_ref)
        dw_ref[0:1, :] = jnp.sum(dzc * zl, axis=0, keepdims=True)
        dw_ref[1:2, :] = jnp.sum(dzc * z, axis=0, keepdims=True)
        dw_ref[2:3, :] = jnp.sum(dzc * zr, axis=0, keepdims=True)

    col = pl.BlockSpec((n, tc), lambda j: (0, j))
    return _pcall(body, name=name, grid=(nb,),
                  in_specs=[col, pl.BlockSpec((n, tc), lambda j: (0, nb + j)),
                            pl.BlockSpec((n, tc), lambda j: (0, 2 * nb + j)), pl.BlockSpec((3, tc), lambda j: (0, j)), col],
                  out_specs=[pl.BlockSpec((3, n, tc), lambda j: (0, 0, j)), pl.BlockSpec((8, tc), lambda j: (0, j))],
                  out_shape=[jax.ShapeDtypeStruct((3, n, cd), BF), jax.ShapeDtypeStruct((8, cd), F32)],
                  compiler_params=_params(("arbitrary",)))(p3, p3, p3, cw, dy)


def _conv_din(name, dp3, w_in, tm):
    _, n, cd = dp3.shape
    d = w_in.shape[0]

    def body(a_ref, w_ref, o_ref, acc_ref):
        j = pl.program_id(1)
        part = _dot(a_ref[...], w_ref[...], NT)

        @pl.when(j == 0)
        def _():
            acc_ref[...] = part

        @pl.when(j > 0)
        def _():
            acc_ref[...] += part

        @pl.when(j == 2)
        def _():
            o_ref[...] = acc_ref[...]

    return _pcall(body, name=name, grid=(n // tm, 3),
                  in_specs=[pl.BlockSpec((None, tm, cd), lambda i, j: (j, i, 0)),
                            pl.BlockSpec((d, cd), lambda i, j: (0, j))],
                  out_specs=pl.BlockSpec((tm, d), lambda i, j: (i, 0)), out_shape=jax.ShapeDtypeStruct((n, d), F32),
                  scratch_shapes=[pltpu.VMEM((tm, d), F32)],
                  compiler_params=_params(("arbitrary", "arbitrary")))(dp3, w_in)


def _conv_dw_in(name, u, dp3, tmm, tn):
    n, d = u.shape
    cd = dp3.shape[2]
    nb = cd // tn

    def body(u_ref, z_ref, o_ref):
        o_ref[...] = _dot(u_ref[...], z_ref[...], TN)

    return _pcall(body, name=name, grid=(3 * nb, d // tmm),
                  in_specs=[pl.BlockSpec((n, tmm), lambda j, mi: (0, mi)),
                            pl.BlockSpec((None, n, tn), lambda j, mi: (j // nb, 0, j % nb))],
                  out_specs=pl.BlockSpec((tmm, tn), lambda j, mi: (mi, j)),
                  out_shape=jax.ShapeDtypeStruct((d, 3 * cd), F32),
                  compiler_params=_params(("arbitrary", "arbitrary")))(u, dp3)


def _loss_head(name, h, target, gain, tm):
    d = h.shape[1]

    def fn(rows, consts, m):
        hv, tv = rows
        g = consts[0][0:1, :]
        n, r = _rms(hv)
        err = n * g - tv
        dy = err / d
        loss = 0.5 * jnp.sum(err * err) / d
        acc = {0: jnp.sum(dy * n, axis=0, keepdims=True), 1: jnp.full((1, d), loss, F32)}
        return [_rms_bwd(dy * g, n, r)], acc

    return _rows_call(name, fn, h.shape[0], tm, [h, target], [gain], None, [(d, F32)], acc_w=d)


def _adamw(name, w, g, m, v):
    shape = w.shape
    cdim = shape[-1]
    r = max(1, math.prod(shape[:-1]))
    tr = r
    if r * cdim * 4 > (3 << 19):
        tr = _pick(r, max(8, (3 << 19) // (cdim * 4)), 8)
    c1 = 1.0 / (1.0 - ADAM_B1 ** ADAM_STEP)
    c2 = 1.0 / (1.0 - ADAM_B2 ** ADAM_STEP)

    def body(w_ref, g_ref, m_ref, v_ref, d_ref, nm_ref, nv_ref):
        gv = g_ref[...]
        nm = ADAM_B1 * m_ref[...] + (1.0 - ADAM_B1) * gv
        nv = ADAM_B2 * v_ref[...] + (1.0 - ADAM_B2) * (gv * gv)
        nm_ref[...] = nm
        nv_ref[...] = nv
        d_ref[...] = -ADAM_LR * ((nm * c1) / (jnp.sqrt(nv * c2) + ADAM_EPS) + ADAM_WD * w_ref[...])

    spec = pl.BlockSpec((tr, cdim), lambda i: (i, 0))
    res = _pcall(body, name=name, grid=(r // tr,), in_specs=[spec] * 4, out_specs=[spec] * 3,
                 out_shape=[jax.ShapeDtypeStruct((r, cdim), F32)] * 3,
                 compiler_params=_params(("arbitrary",)))(*[t.reshape(r, cdim) for t in (w, g, m, v)])
    return [t.reshape(shape) for t in res]


def _ffn_half_fwd(tag, s, gains, mod, k, wts, n_lat, tm, h_tiles, tm_big):
    wg, wu, wd, idx = wts
    u = _adaln_fwd(f"adaln_{tag}", s, gains, k, mod, k, tm, h_tiles)
    a, b, hid = _ffn_up(f"ffn_up_{tag}", u, wg, wu, idx, tm)
    s_out, o = _ffn_down(f"ffn_down_{tag}", hid, wd, idx, s, mod, k, n_lat, tm_big)
    return s_out, (s, u, a, b, hid, o)


def _ffn_half_bwd(tag, ds_out, saved, gains, mod, k, wts, big_grads, tm, h_tiles, tm_big):
    wg, wu, wd, idx = wts
    g_gate, g_up, g_down = big_grads
    s, u, a, b, hid, o = saved
    d_o, acc_g = _resid_bwd(f"resid_bwd_{tag}", ds_out, o, mod, k, 0.5, tm, h_tiles)
    da, db = _ffn_dhid(f"ffn_dhid_{tag}", d_o, wd, idx, a, b, tm)
    du = _ffn_du(f"ffn_du_{tag}", da, db, wg, wu, idx, tm_big)
    d = u.shape[1]
    g_gate = _ffn_dw_in(f"ffn_dwg_{tag}", u, da, _pick(d, 256), g_gate, idx)
    g_up = _ffn_dw_in(f"ffn_dwu_{tag}", u, db, _pick(d, 256), g_up, idx)
    g_down = _ffn_dw_down(f"ffn_dwd_{tag}", hid, d_o, _pick(d, 512), g_down, idx)
    ds, acc_n = _adaln_bwd(f"adaln_bwd_{tag}", s, du, ds_out, gains, k, mod, k, tm, h_tiles)
    return ds, (g_gate, g_up, g_down), (acc_n[:, 0], acc_n[:, 1], acc_g[:, 0]), jnp.sum(acc_n[:, 2], axis=0)


def kernel(x, c, ctx, c_ctx, norm_g, w_mod, b_mod, ffn_w_gate, ffn_w_up, ffn_w_down, ab_w_in, pool_w, pool_scale, q_norm_g, w_uq, kv_norm_g, w_ukv, ab_w_out, conv_w_in, conv_w, conv_w_out, final_norm_g, loss_target, m_c_ctx, m_norm_g, m_w_mod, m_b_mod, m_ffn_w_gate, m_ffn_w_up, m_ffn_w_down, m_ab_w_in, m_pool_w, m_pool_scale, m_q_norm_g, m_w_uq, m_kv_norm_g, m_w_ukv, m_ab_w_out, m_conv_w_in, m_conv_w, m_conv_w_out, m_final_norm_g, v_c_ctx, v_norm_g, v_w_mod, v_b_mod, v_ffn_w_gate, v_ffn_w_up, v_ffn_w_down, v_ab_w_in, v_pool_w, v_pool_scale, v_q_norm_g, v_w_uq, v_kv_norm_g, v_w_ukv, v_ab_w_out, v_conv_w_in, v_conv_w, v_conv_w_out, v_final_norm_g):
    weights = dict(c_ctx=c_ctx, norm_g=norm_g, w_mod=w_mod, b_mod=b_mod, ffn_w_gate=ffn_w_gate, ffn_w_up=ffn_w_up,
                   ffn_w_down=ffn_w_down, ab_w_in=ab_w_in, pool_w=pool_w, pool_scale=pool_scale, q_norm_g=q_norm_g,
                   w_uq=w_uq, kv_norm_g=kv_norm_g, w_ukv=w_ukv, ab_w_out=ab_w_out, conv_w_in=conv_w_in, conv_w=conv_w,
                   conv_w_out=conv_w_out, final_norm_g=final_norm_g)
    mom_m = dict(c_ctx=m_c_ctx, norm_g=m_norm_g, w_mod=m_w_mod, b_mod=m_b_mod, ffn_w_gate=m_ffn_w_gate,
                 ffn_w_up=m_ffn_w_up, ffn_w_down=m_ffn_w_down, ab_w_in=m_ab_w_in, pool_w=m_pool_w,
                 pool_scale=m_pool_scale, q_norm_g=m_q_norm_g, w_uq=m_w_uq, kv_norm_g=m_kv_norm_g, w_ukv=m_w_ukv,
                 ab_w_out=m_ab_w_out, conv_w_in=m_conv_w_in, conv_w=m_conv_w, conv_w_out=m_conv_w_out,
                 final_norm_g=m_final_norm_g)
    mom_v = dict(c_ctx=v_c_ctx, norm_g=v_norm_g, w_mod=v_w_mod, b_mod=v_b_mod, ffn_w_gate=v_ffn_w_gate,
                 ffn_w_up=v_ffn_w_up, ffn_w_down=v_ffn_w_down, ab_w_in=v_ab_w_in, pool_w=v_pool_w,
                 pool_scale=v_pool_scale, q_norm_g=v_q_norm_g, w_uq=v_w_uq, kv_norm_g=v_kv_norm_g, w_ukv=v_w_ukv,
                 ab_w_out=v_ab_w_out, conv_w_in=v_conv_w_in, conv_w=v_conv_w, conv_w_out=v_conv_w_out,
                 final_norm_g=v_final_norm_g)

    t_len, d = x.shape[1], x.shape[2]
    g_len = ctx.shape[1]
    r_len = t_len + g_len
    fc = ffn_w_gate.shape[3]
    heads = d // 128
    pool_dim = d // 2
    q_rank, kv_rank = q_norm_g.shape[1], kv_norm_g.shape[1]
    hw = heads * HEAD_PAD
    attn_scale = 1.0 / math.sqrt(QK_NOPE + QK_ROPE)
    kvr_w = kv_rank + HEAD_PAD
    in_w = pool_dim + q_rank + kvr_w
    tm = 256 if g_len % 256 == 0 else g_len
    assert t_len % tm == 0 and g_len % tm == 0 and t_len % g_len == 0 and pool_dim % 128 == 0
    h_tiles = t_len // tm
    tm_l0 = _pick(r_len, 768, tm)
    tm_l1 = _pick(t_len, 1024, tm)

    xi, yi, ci = lax.axis_index("x"), lax.axis_index("y"), lax.axis_index("c")
    me = 4 * xi + 2 * yi + ci
    shard = 2 * xi + yi

    def halves(w):
        return w.astype(BF).reshape(2, -1, w.shape[-1])

    big_names = ["ffn_w_gate", "ffn_w_up", "ffn_w_down", "ab_w_out", "conv_w_out", "w_uq", "w_ukv", "ab_w_in",
                 "conv_w_in"]
    gathered = _gather_halves("gather_weights", [halves(weights[nm]) for nm in big_names])
    gw = {nm: g.reshape(N_SHARD, 2 * g.shape[1], g.shape[2]) for nm, g in zip(big_names, gathered)}
    wg_all = gw["ffn_w_gate"].reshape(N_SHARD, 4, d, fc)
    wu_all = gw["ffn_w_up"].reshape(N_SHARD, 4, d, fc)
    ffn_w = [[(wg_all, wu_all, gw["ffn_w_down"], 2 * l + f) for f in range(2)] for l in range(2)]
    w_out_full = gw["ab_w_out"].reshape(-1, d)
    cw_out_full = gw["conv_w_out"].reshape(-1, d)
    w_uq_full = gw["w_uq"].reshape(q_rank, heads * (QK_NOPE + QK_ROPE))
    w_ukv_full = gw["w_ukv"].transpose(1, 0, 2).reshape(kv_rank, heads * (QK_NOPE + V_HEAD))
    w_in_full = gw["ab_w_in"].transpose(1, 0, 2).reshape(d, -1)
    cw_in_full = gw["conv_w_in"].transpose(1, 0, 2).reshape(d, -1)

    wq_p = jnp.pad(w_uq_full.reshape(q_rank, heads, QK_NOPE + QK_ROPE),
                   ((0, 0), (0, 0), (0, HEAD_PAD - QK_NOPE - QK_ROPE))).reshape(q_rank, hw)
    ukv3 = w_ukv_full.reshape(kv_rank, heads, QK_NOPE + V_HEAD)
    wk_top = jnp.pad(ukv3[..., :QK_NOPE], ((0, 0), (0, 0), (0, HEAD_PAD - QK_NOPE))).reshape(kv_rank, hw)
    wv_top = jnp.pad(ukv3[..., QK_NOPE:], ((0, 0), (0, 0), (0, HEAD_PAD - V_HEAD))).reshape(kv_rank, hw)
    spread = jnp.zeros((HEAD_PAD, heads, HEAD_PAD), BF).at[
        jnp.arange(QK_ROPE)[:, None], jnp.arange(heads)[None, :], QK_NOPE + jnp.arange(QK_ROPE)[:, None]].set(1.0)
    wk_ext = jnp.concatenate([wk_top, spread.reshape(HEAD_PAD, hw)], axis=0)
    wv_ext = jnp.concatenate([wv_top, jnp.zeros((HEAD_PAD, hw), BF)], axis=0)
    w_in_pool = w_in_full[:, :pool_dim]
    w_in_q = w_in_full[:, pool_dim:pool_dim + q_rank]
    w_in_kvr = jnp.pad(w_in_full[:, pool_dim + q_rank:], ((0, 0), (0, HEAD_PAD - QK_ROPE)))
    w_out_attn = jnp.pad(w_out_full[pool_dim:].reshape(heads, V_HEAD, d),
                         ((0, 0), (0, HEAD_PAD - V_HEAD), (0, 0))).reshape(hw, d)
    w_out_p = jnp.concatenate([w_out_full[:pool_dim], w_out_attn], axis=0)

    small = jnp.concatenate([norm_g.reshape(6, -1), conv_w[0]], axis=0)
    small = jnp.pad(small, ((0, 7), (0, 0)))
    c_row = jnp.pad(c, ((0, 7), (0, 0)))
    small_all, c_all = _gather_all("gather_small", [small, c_row])
    small_full = small_all[::2].transpose(1, 0, 2).reshape(16, d)
    gains = [jnp.pad(small_full[3 * l:3 * l + 3], ((0, 5), (0, 0))) for l in range(2)]
    conv_w_full = small_full[6:9]
    c16 = jnp.concatenate([c_all[:, 0], c_ctx[None], jnp.zeros((7, d), F32)], axis=0)

    n_col = w_mod.shape[2]
    b_sh = lax.dynamic_slice_in_dim(b_mod, shard * n_col, n_col, axis=1)
    m_sh = [_mm(f"mod_fwd_{l}", c16, w_mod[l], 'nn', tm=16, tn=768, a_pre=_silu,
                epi=lambda acc, i, bv: (acc + bv,), epi_args=(b_sh[l:l + 1],), epi_kinds=('n',)) for l in range(2)]
    m_all = _gather_all("gather_mod", [jnp.concatenate(m_sh, axis=0)])[0]
    m_full = m_all[::2].reshape(N_SHARD, 2, 16, n_col).transpose(1, 2, 0, 3).reshape(2, 16, N_MOD * d)
    mod_h = [jnp.pad(lax.dynamic_index_in_dim(m_full[l], me, 0, keepdims=False).reshape(N_MOD, d), ((0, 7), (0, 0)))
             for l in range(2)]
    mod_g0 = jnp.pad(m_full[0, 8].reshape(N_MOD, d), ((0, 7), (0, 0)))
    mods = [jnp.stack([mod_h[0], mod_g0]), mod_h[1][None]]

    s0 = jnp.concatenate([x[0], ctx[0]], axis=0)
    s1, sav_f00 = _ffn_half_fwd("l0a", s0, gains[0], mods[0], 0, ffn_w[0][0], t_len, tm, h_tiles, tm_l0)
    u_mix = _adaln_fwd("adaln_l0m", s1, gains[0], 1, mods[0], 1, tm, h_tiles)
    p_pool = _mm("in_pool", u_mix, w_in_pool, 'nn', tm=tm, tn=pool_dim)
    p_q = _mm("in_q", u_mix, w_in_q, 'nn', tm=tm, tn=q_rank)
    p_kvr = _mm("in_kvr", u_mix, w_in_kvr, 'nn', tm=tm, tn=kvr_w)
    qg = jnp.pad(q_norm_g, ((0, 7), (0, 0)))
    kvg = jnp.pad(kv_norm_g, ((0, 7), (0, 0)))
    tq_c, tq_s = _rope_tables(t_len, g_len, QK_NOPE)
    tk_c, tk_s = _rope_tables(t_len, g_len, 0)

    def qn_fn(rows, consts, m):
        n, _ = _rms(rows[0])
        return [n * consts[0][0:1, :]], {}

    qn = _rows_call("q_norm", qn_fn, r_len, tm, [p_q], [qg], None, [(q_rank, BF)])[0]
    q_r = _mm("q_up", qn, wq_p, 'nn', tm=tm, tn=hw, out_dtypes=(BF,),
              epi=lambda acc, i, ct, st: (_rope(acc, ct, st),), epi_args=(tq_c, tq_s), epi_kinds=('mt', 'mt'))

    def kvn_fn(rows, consts, m):
        pv, ct, st = rows
        n, _ = _rms(pv[:, :kv_rank])
        return [jnp.concatenate([n * consts[0][0:1, :], _rope(pv[:, kv_rank:], ct, st)], axis=1)], {}

    kvn = _rows_call("kv_norm", kvn_fn, r_len, tm, [p_kvr, tk_c, tk_s], [kvg], None, [(kvr_w, BF)])[0]
    k_p = _mm("k_up", kvn, wk_ext, 'nn', tm=tm, tn=hw, out_dtypes=(BF,))
    v_p = _mm("v_up", kvn, wv_ext, 'nn', tm=tm, tn=hw, out_dtypes=(BF,))
    o_h, lse_h = _attn_fwd("attn_h", q_r, k_p, v_p, t_len, 0, r_len, 0, heads, tm, attn_scale)
    o_g, lse_g = _attn_fwd("attn_g", q_r, k_p, v_p, g_len, t_len, g_len, t_len // g_len, heads, tm, attn_scale)
    y_h = _pool_fwd("pool_h", p_pool[:t_len], pool_w[0], pool_scale)
    y_g = _pool_fwd("pool_g", p_pool[t_len:], pool_w[0], pool_scale)
    cat = jnp.concatenate([jnp.concatenate([y_h, y_g], axis=0), jnp.concatenate([o_h, o_g], axis=0)], axis=1)

    def resid_epi(k3, n_lat, tmr):
        def epi(acc, i, sv, mv):
            return sv + _row_gate(mv, k3, i, tmr, n_lat) * acc, acc
        return epi

    s2, o_mix0 = _mm("mix_out_l0", cat, w_out_p, 'nn', tm=tm, tn=d, out_dtypes=(F32, F32),
                     epi=resid_epi(5, t_len, tm), epi_args=(s1, mods[0]), epi_kinds=('mn', 'w'))
    s3, sav_f01 = _ffn_half_fwd("l0b", s2, gains[0], mods[0], 2, ffn_w[0][1], t_len, tm, h_tiles, tm_l0)

    tml = 256 if t_len % 256 == 0 else tm
    h3 = s3[:t_len]
    h4, sav_f10 = _ffn_half_fwd("l1a", h3, gains[1], mods[1], 0, ffn_w[1][0], t_len, tml, None, tm_l1)
    u_cv = _adaln_fwd("adaln_l1m", h4, gains[1], 1, mods[1], 1, tml, None)
    p3 = _mm("conv_in", u_cv, cw_in_full, 'nn', tm=tml, tn=512)
    cwp = conv_w_full
    tc = _pick(d, 256)
    y_cv = _conv_fwd("conv_fwd", p3, cwp, tc)
    h5, o_mix1 = _mm("mix_out_l1", y_cv, cw_out_full, 'nn', tm=tml, tn=d, out_dtypes=(F32, F32),
                     epi=resid_epi(5, t_len, tml), epi_args=(h4, mods[1]), epi_kinds=('mn', 'w'))
    h6, sav_f11 = _ffn_half_fwd("l1b", h5, gains[1], mods[1], 2, ffn_w[1][1], t_len, tml, None, tm_l1)

    fg = jnp.pad(final_norm_g[None], ((0, 7), (0, 0)))
    dh6, acc_loss = _loss_head("loss_head", h6, loss_target[0], fg, tml)
    loss = lax.psum(acc_loss[0, 1, 0], ("x", "y", "c"))
    d_final_g = acc_loss[0, 0]

    dgain = [[None] * 3 for _ in range(2)]
    dmod = [[None] * N_MOD for _ in range(2)]

    def put(l, k, triple):
        dmod[l][3 * k], dmod[l][3 * k + 1], dmod[l][3 * k + 2] = triple

    ffn_g = (lax.empty((N_SHARD, 4 * d, fc), BF), lax.empty((N_SHARD, 4 * d, fc), BF),
             lax.empty((N_SHARD, 4 * fc, d), BF))
    dh5, ffn_g, tr, dgain[1][2] = _ffn_half_bwd("l1b", dh6, sav_f11, gains[1], mods[1], 2, ffn_w[1][1], ffn_g,
                                                tml, None, tm_l1)
    put(1, 2, tr)
    d_o1, acc_g1 = _resid_bwd("resid_bwd_l1m", dh5, o_mix1, mods[1], 1, 1.0, tml, None)
    dy_cv = _mm("mix_out_l1_dx", d_o1, cw_out_full, 'nt', tm=tml, tn=d)
    d_cw_out = _mm("mix_out_l1_dw", y_cv, d_o1, 'tn', tm=256, tn=512)
    dp3, d_cw = _conv_bwd("conv_bwd", p3, cwp, dy_cv, tc)
    du_cv = _conv_din("conv_in_dx", dp3, cw_in_full, tml)
    d_cw_in = _conv_dw_in("conv_in_dw", u_cv, dp3, _pick(d, 256), _pick(d, 512))
    dh4, acc_n1 = _adaln_bwd("adaln_bwd_l1m", h4, du_cv, dh5, gains[1], 1, mods[1], 1, tml, None)
    put(1, 1, (acc_n1[:, 0], acc_n1[:, 1], acc_g1[:, 0]))
    dgain[1][1] = acc_n1[0, 2]
    dh3, ffn_g, tr, dgain[1][0] = _ffn_half_bwd("l1a", dh4, sav_f10, gains[1], mods[1], 0, ffn_w[1][0], ffn_g,
                                                tml, None, tm_l1)
    put(1, 0, tr)

    ds3 = jnp.concatenate([dh3, jnp.zeros((g_len, d), F32)], axis=0)
    ds2, ffn_g, tr, dgain[0][2] = _ffn_half_bwd("l0b", ds3, sav_f01, gains[0], mods[0], 2, ffn_w[0][1], ffn_g,
                                                tm, h_tiles, tm_l0)
    put(0, 2, tr)
    d_o0, acc_g0 = _resid_bwd("resid_bwd_l0m", ds2, o_mix0, mods[0], 1, 1.0, tm, h_tiles)
    dcat = _mm("mix_out_l0_dx", d_o0, w_out_p, 'nt', tm=tm, tn=pool_dim + hw)
    d_w_out_p = _mm("mix_out_l0_dw", cat, d_o0, 'tn', tm=256, tn=512)
    col_blk = pool_dim // HEAD_PAD
    dq_h, dk_h, dv_h = _attn_bwd("attn_bwd_h", q_r, k_p, v_p, cat, dcat, lse_h, t_len, 0, r_len, 0, heads, tm,
                                 attn_scale, col_blk)
    dq_g, dk_g, dv_g = _attn_bwd("attn_bwd_g", q_r, k_p, v_p, cat, dcat, lse_g, g_len, t_len, g_len,
                                 t_len // g_len, heads, tm, attn_scale, col_blk)
    dq_all = jnp.concatenate([dq_h, dq_g], axis=0)
    dk_all = dk_h.at[t_len:].add(dk_g)
    dv_all = dv_h.at[t_len:].add(dv_g)
    dkvn = _mm("k_up_dx", dk_all, wk_ext, 'nt', tm=tm, tn=kvr_w)
    dkvn = _mm("v_up_dx", dv_all, wv_ext, 'nt', tm=tm, tn=kvr_w, epi=lambda acc, i, prev: (acc + prev,),
               epi_args=(dkvn,), epi_kinds=('mn',))
    d_wk_ext = _mm("k_up_dw", kvn, dk_all, 'tn', tm=kvr_w, tn=512)
    d_wv_ext = _mm("v_up_dw", kvn, dv_all, 'tn', tm=kvr_w, tn=512)

    def kvn_bwd_fn(rows, consts, m):
        pv, dv_, ct, st = rows
        g = consts[0][0:1, :]
        n, r = _rms(pv[:, :kv_rank])
        dyn = dv_[:, :kv_rank]
        dckv = _rms_bwd(dyn * g, n, r)
        dkr = _rope_t(dv_[:, kv_rank:], ct, st)
        return [jnp.concatenate([dckv, dkr], axis=1)], {0: jnp.sum(dyn * n, axis=0, keepdims=True)}

    dp_kvr, acc_kvg = _rows_call("kv_norm_bwd", kvn_bwd_fn, r_len, tm, [p_kvr, dkvn, tk_c, tk_s], [kvg], None,
                                 [(kvr_w, BF)], acc_w=kv_rank)

    def qrope_bwd_fn(rows, consts, m):
        return [_rope_t(rows[0], rows[1], rows[2])], {}

    dq_pad = _rows_call("q_rope_bwd", qrope_bwd_fn, r_len, tm, [dq_all, tq_c, tq_s], [], None, [(hw, BF)])[0]
    dqn = _mm("q_up_dx", dq_pad, wq_p, 'nt', tm=tm, tn=q_rank)
    d_wq_p = _mm("q_up_dw", qn, dq_pad, 'tn', tm=256, tn=512)

    def qn_bwd_fn(rows, consts, m):
        pv, dv_ = rows
        g = consts[0][0:1, :]
        n, r = _rms(pv)
        return [_rms_bwd(dv_ * g, n, r)], {0: jnp.sum(dv_ * n, axis=0, keepdims=True)}

    dp_q, acc_qg = _rows_call("q_norm_bwd", qn_bwd_fn, r_len, tm, [p_q, dqn], [qg], None, [(q_rank, BF)],
                              acc_w=q_rank)
    dpu_h, dpw_h, dps_h = _pool_bwd("pool_bwd_h", p_pool[:t_len], dcat, pool_w[0], pool_scale, 0)
    dpu_g, dpw_g, dps_g = _pool_bwd("pool_bwd_g", p_pool[t_len:], dcat, pool_w[0], pool_scale, t_len)
    dp_pool = jnp.concatenate([dpu_h, dpu_g], axis=0)
    add_prev = lambda acc, i, prev: (acc + prev,)
    du_mix = _mm("in_pool_dx", dp_pool, w_in_pool, 'nt', tm=tm, tn=d)
    du_mix = _mm("in_q_dx", dp_q, w_in_q, 'nt', tm=tm, tn=d, epi=add_prev, epi_args=(du_mix,), epi_kinds=('mn',))
    du_mix = _mm("in_kvr_dx", dp_kvr, w_in_kvr, 'nt', tm=tm, tn=d, epi=add_prev, epi_args=(du_mix,), epi_kinds=('mn',))
    d_w_in = jnp.concatenate([
        _mm("in_pool_dw", u_mix, dp_pool, 'tn', tm=256, tn=pool_dim),
        _mm("in_q_dw", u_mix, dp_q, 'tn', tm=256, tn=q_rank),
        _mm("in_kvr_dw", u_mix, dp_kvr, 'tn', tm=256, tn=kvr_w)[:, :kv_rank + QK_ROPE]], axis=1)
    ds1, acc_n0 = _adaln_bwd("adaln_bwd_l0m", s1, du_mix, ds2, gains[0], 1, mods[0], 1, tm, h_tiles)
    put(0, 1, (acc_n0[:, 0], acc_n0[:, 1], acc_g0[:, 0]))
    dgain[0][1] = jnp.sum(acc_n0[:, 2], axis=0)
    ds0, ffn_g, tr, dgain[0][0] = _ffn_half_bwd("l0a", ds1, sav_f00, gains[0], mods[0], 0, ffn_w[0][0], ffn_g,
                                                tm, h_tiles, tm_l0)
    put(0, 0, tr)
    grad_x = ds0[:t_len][None]

    d_w_uq = d_wq_p.reshape(q_rank, heads, HEAD_PAD)[..., :QK_NOPE + QK_ROPE].reshape(q_rank, -1)
    d_w_ukv = jnp.concatenate([d_wk_ext[:kv_rank].reshape(kv_rank, heads, HEAD_PAD)[..., :QK_NOPE],
                               d_wv_ext[:kv_rank].reshape(kv_rank, heads, HEAD_PAD)[..., :V_HEAD]],
                              axis=-1).reshape(kv_rank, -1)
    d_w_out = jnp.concatenate([d_w_out_p[:pool_dim],
                               d_w_out_p[pool_dim:].reshape(heads, HEAD_PAD, d)[:, :V_HEAD].reshape(-1, d)], axis=0)

    dmh = jnp.stack([jnp.stack([dmod[l][k][0] for k in range(N_MOD)]) for l in range(2)])
    dmg0 = jnp.stack([dmod[0][k][1] for k in range(N_MOD)])
    dg_rows = jnp.stack([dgain[l][k] for l in range(2) for k in range(3)])
    pieces = [dmh.reshape(2 * N_MOD, d), dmg0, dg_rows, d_cw[:3], d_final_g[None],
              (dpw_h + dpw_g).reshape(-1, d), jnp.pad((dps_h + dps_g)[0], (0, d - pool_dim))[None],
              jnp.pad(acc_qg[0, 0], (0, d - q_rank))[None], jnp.pad(acc_kvg[0, 0], (0, d - kv_rank))[None]]
    n_piece = [p.shape[0] for p in pieces]
    pieces = [jnp.pad(p, ((0, (-p.shape[0]) % 8), (0, 0))) for p in pieces]
    small_g = jnp.concatenate(pieces, axis=0)
    sg_all = _gather_all("gather_small_grads", [small_g])[0]
    sg_sum = _sum_lead("sum_small_grads", sg_all)
    offs = [0]
    for p in pieces:
        offs.append(offs[-1] + p.shape[0])
    part = lambda j: sg_sum[offs[j]:offs[j] + n_piece[j]]
    sum_dmh, sum_dmg0, g_norm_full, g_conv_w_full = part(0).reshape(2, N_MOD * d), part(1).reshape(N_MOD * d), part(2), part(3)
    g_final = part(4)[0]
    g_pool_w = part(5).reshape(pool_w.shape)
    g_pool_scale = part(6)[:, :pool_dim]
    g_q_norm = part(7)[:, :q_rank]
    g_kv_norm = part(8)[:, :kv_rank]
    col0 = shard * (d // N_SHARD)
    g_norm_g = lax.dynamic_slice_in_dim(g_norm_full.reshape(2, 3, d), col0, d // N_SHARD, axis=2)
    g_conv_w = lax.dynamic_slice_in_dim(g_conv_w_full, col0, d // N_SHARD, axis=1)[None]
    g_b_mod = _sum_lead("sum_b_mod", jnp.stack([sum_dmh, jnp.stack([sum_dmg0, jnp.zeros_like(sum_dmg0)])]))

    dm16 = []
    for l in range(2):
        per_dev = sg_all[:, l * N_MOD:(l + 1) * N_MOD].reshape(N_DEV, N_MOD * d)
        row8 = sum_dmg0 if l == 0 else jnp.zeros_like(sum_dmg0)
        full = jnp.concatenate([per_dev, row8[None], jnp.zeros((7, N_MOD * d), F32)], axis=0)
        dm16.append(lax.dynamic_slice_in_dim(full, shard * n_col, n_col, axis=1))
    g_w_mod = jnp.stack([_mm(f"mod_dw_{l}", c16, dm16[l], 'tn', tm=256, tn=768, a_pre=_silu) for l in range(2)])
    dc16 = _mm("mod_dx", dm16[0], w_mod[0], 'nt', tm=16, tn=512, epi=lambda acc, i, cv: (acc * _dsilu(cv),),
               epi_args=(c16,), epi_kinds=('mn',))
    dc_all = _gather_all("gather_dc", [dc16])[0]
    g_c_ctx = _sum_lead("sum_dc", dc_all[::2])[8]

    def by_shard_rows(g):
        return g.reshape(N_SHARD, -1, g.shape[-1])

    def by_shard_cols(g):
        return g.reshape(g.shape[0], N_SHARD, -1).transpose(1, 0, 2)

    big = dict(ffn_w_gate=ffn_g[0], ffn_w_up=ffn_g[1], ffn_w_down=ffn_g[2], ab_w_out=by_shard_rows(d_w_out),
               conv_w_out=by_shard_rows(d_cw_out), w_uq=by_shard_rows(d_w_uq), w_ukv=by_shard_cols(d_w_ukv),
               ab_w_in=by_shard_cols(d_w_in), conv_w_in=by_shard_cols(d_cw_in))
    send = [big[nm].astype(BF).reshape(N_DEV, big[nm].shape[1] // 2, big[nm].shape[2]) for nm in big_names]
    pre = _presum_sibling("grads_presum", send)
    landed = _to_chips("grads_to_chips", pre)
    halves_sum = [_sum_lead(f"sum_grads_{nm}", l) for nm, l in zip(big_names, landed)]
    swapped = _swap_sibling("swap_halves", halves_sum)
    grads = dict(c_ctx=g_c_ctx, norm_g=g_norm_g, w_mod=g_w_mod, b_mod=g_b_mod, pool_w=g_pool_w,
                 pool_scale=g_pool_scale, q_norm_g=g_q_norm, kv_norm_g=g_kv_norm, conv_w=g_conv_w, final_norm_g=g_final)
    grads.update({nm: s.reshape(weights[nm].shape) for nm, s in zip(big_names, swapped)})

    names = list(weights)
    upd = {n: _adamw(f"adamw_{n}", weights[n], grads[n].reshape(weights[n].shape), mom_m[n], mom_v[n]) for n in names}
    return (loss, grad_x, *[grads[n].reshape(weights[n].shape) for n in names], *[upd[n][0] for n in names],
            *[upd[n][1] for n in names], *[upd[n][2] for n in names])
```

```python
import functools
import math

import jax
import jax.numpy as jnp
from jax import lax
from jax.experimental import pallas as pl
from jax.experimental.pallas import tpu as pltpu

F32 = jnp.float32
BF = jnp.bfloat16
MESH = pl.DeviceIdType.MESH

N_DEV = 8
N_SHARD = 4
RMS_EPS = 1e-6
N_MOD = 9
POOL_WINDOWS = (2, 4, 8, 16)
QK_NOPE = 64
QK_ROPE = 32
V_HEAD = 64
HEAD_PAD = 128
GRID_W = 64
ROPE_THETA = 10000.0
POOL_PAD = 16
ADAM_LR, ADAM_B1, ADAM_B2, ADAM_EPS, ADAM_WD, ADAM_STEP = 0.001, 0.9, 0.999, 1e-08, 0.01, 10
VMEM_LIMIT = 56 * 1024 * 1024


def _pcall(body, **kw):
    return pl.pallas_call(body, **kw)


def _params(sem=None):
    return pltpu.CompilerParams(dimension_semantics=sem, vmem_limit_bytes=VMEM_LIMIT)


def _pick(n, pref, mult=128):
    best = None
    d = mult
    while d <= min(n, pref):
        if n % d == 0:
            best = d
        d += mult
    return best if best is not None else n


def _silu(z):
    return z * jax.nn.sigmoid(z)


def _dsilu(z):
    s = jax.nn.sigmoid(z)
    return s * (1.0 + z * (1.0 - s))


def _dot(a, b, dims):
    return lax.dot_general(a.astype(BF), b.astype(BF), (dims, ((), ())), preferred_element_type=F32)


NN = ((1,), (0,))
NT = ((1,), (1,))
TN = ((0,), (0,))


ALL_FLIPS = [(kx, ky, kc) for kx in (0, 1) for ky in (0, 1) for kc in (0, 1) if (kx, ky, kc) != (0, 0, 0)]
CHIP_FLIPS = [(1, 0, 0), (0, 1, 0), (1, 1, 0)]
SIBLING = (0, 0, 1)
COMM_SPLIT = 8
SPLIT_MIN_ROWS = 256


def _exchange(name, arrays, plan, lead, whole_src, split=COMM_SPLIT):
    n = len(arrays)
    blk_shapes = [tuple(a.shape) if whole_src else tuple(a.shape[1:]) for a in arrays]
    splits = []
    for shp in blk_shapes:
        s = 1
        while s * 2 <= split and shp[0] % (s * 2) == 0 and (shp[0] // (s * 2)) % 16 == 0 \
                and shp[0] // (s * 2) >= SPLIT_MIN_ROWS:
            s *= 2
        splits.append(s)
    items = plan(0, 0, 0)
    n_items = len(items)
    remote_ids = [k for k, it in enumerate(items) if it[0] is not None]
    local_ids = [k for k, it in enumerate(items) if it[0] is None]
    slots = [(a, s) for s in range(max(splits)) for a in range(n) if s < splits[a]]
    n_slot = len(slots)

    def body(*refs):
        ins, outs = refs[:n], refs[n:2 * n]
        send_sems, recv_sems, loc_sems = refs[2 * n:]
        x, y, c = lax.axis_index("x"), lax.axis_index("y"), lax.axis_index("c")
        plan_here = plan(x, y, c)

        def rows(ref, a, s):
            rc = blk_shapes[a][0] // splits[a]
            return ref.at[pl.ds(s * rc, rc)]

        def make(si, k):
            a, s = slots[si]
            flip, src, dst, _ = plan_here[k]
            base = outs[a] if src[0] == 'out' else ins[a]
            src_ref = rows(base if src[1] is None else base.at[src[1]], a, s)
            dst_ref = rows(outs[a].at[dst], a, s)
            if flip is None:
                return pltpu.make_async_copy(src_ref, dst_ref, loc_sems.at[si * max(1, len(local_ids)) + local_ids.index(k)])
            peer = (1 - x if flip[0] else x, 1 - y if flip[1] else y, 1 - c if flip[2] else c)
            sem = si * len(remote_ids) + remote_ids.index(k)
            return pltpu.make_async_remote_copy(src_ref=src_ref, dst_ref=dst_ref, send_sem=send_sems.at[sem],
                                                recv_sem=recv_sems.at[sem], device_id=peer, device_id_type=MESH)

        copies = {}
        for si in range(n_slot):
            for k in range(n_items):
                if plan_here[k][3] is None:
                    copies[si, k] = make(si, k)
                    copies[si, k].start()
        arrived = set()
        for si in range(n_slot):
            for k in range(n_items):
                after = plan_here[k][3]
                if after is not None:
                    if (si, after) not in arrived:
                        copies[si, after].wait_recv()
                        arrived.add((si, after))
                    copies[si, k] = make(si, k)
                    copies[si, k].start()
        for (si, k), cp in copies.items():
            if plan_here[k][0] is None:
                cp.wait()
            else:
                cp.wait_send()
                if (si, k) not in arrived:
                    cp.wait_recv()

    any_spec = pl.BlockSpec(memory_space=pl.ANY)
    n_rem = max(1, n_slot * len(remote_ids))
    outs = _pcall(
        body, name=name,
        out_shape=[jax.ShapeDtypeStruct((lead,) + s, a.dtype) for s, a in zip(blk_shapes, arrays)],
        in_specs=[any_spec] * n, out_specs=[any_spec] * n,
        scratch_shapes=[pltpu.SemaphoreType.DMA((n_rem,)), pltpu.SemaphoreType.DMA((n_rem,)),
                        pltpu.SemaphoreType.DMA((max(1, n_slot * len(local_ids)),))],
    )(*arrays)
    return list(outs)


def _place(x, y, c):
    return 4 * x + 2 * y + c


def _flip(v, f):
    return 1 - v if f else v


def _gather_all(name, arrays):
    def plan(x, y, c):
        me = _place(x, y, c)
        return [(None, ('in', None), me, None)] + [(f, ('in', None), me, None) for f in ALL_FLIPS]
    return _exchange(name, arrays, plan, N_DEV, True)


STREAM_SLOTS = 4
STREAM_LAG = 2
STREAM_CHUNK_BYTES = 420 * 1024


def _stream_rows(rows, cdim, itemsize):
    best = None
    for rc in range(16, rows + 1, 16):
        if rows % rc == 0 and rc * cdim * itemsize <= STREAM_CHUNK_BYTES:
            best = rc
    return rows if best is None else best


def _stream_scratch(chunk_specs, combine):
    scratch = []
    for rc, cdim, dt in chunk_specs:
        scratch += [pltpu.VMEM((STREAM_SLOTS, rc, cdim), dt), pltpu.VMEM((STREAM_SLOTS, rc, cdim), dt)]
        if combine:
            scratch += [pltpu.VMEM((STREAM_SLOTS, rc, cdim), dt), pltpu.VMEM((STREAM_SLOTS, rc, cdim), BF)]
    n = len(chunk_specs)
    scratch += [pltpu.SemaphoreType.DMA((n * STREAM_SLOTS,)) for _ in range(5)]
    scratch.append(pltpu.SemaphoreType.REGULAR((n,)))
    return scratch


def _run_stream(peer, chunks, bufs, sems, a, combine):
    ns, lag, k_all = STREAM_SLOTS, STREAM_LAG, len(chunks)
    load_s, send_s, recv_s, store_s, own_s, credits = sems
    credit = credits.at[a]
    if combine:
        send_buf, recv_buf, own_buf, res_buf = bufs
    else:
        send_buf, recv_buf = bufs

    def sem(ref, i):
        return ref.at[a * ns + i % ns]

    def load(i):
        return pltpu.make_async_copy(chunks[i]['src'], send_buf.at[i % ns], sem(load_s, i))

    def own_load(i):
        return pltpu.make_async_copy(chunks[i]['own'], own_buf.at[i % ns], sem(own_s, i))

    def remote(i):
        return pltpu.make_async_remote_copy(src_ref=send_buf.at[i % ns], dst_ref=recv_buf.at[i % ns],
                                            send_sem=sem(send_s, i), recv_sem=sem(recv_s, i),
                                            device_id=peer, device_id_type=MESH)

    def store(i):
        return pltpu.make_async_copy((res_buf if combine else recv_buf).at[i % ns], chunks[i]['dst'], sem(store_s, i))

    def before(i):
        if chunks[i].get('pre') is not None:
            chunks[i]['pre']()

    before(0)
    load(0).start()
    if combine:
        own_load(0).start()
    for i in range(k_all + lag):
        if i < k_all:
            load(i).wait()
            if i >= ns:
                pl.semaphore_wait(credit, 1)
            remote(i).start()
            if i + 1 < k_all:
                if i + 1 >= ns:
                    remote(i + 1 - ns).wait_send()
                before(i + 1)
                load(i + 1).start()
        r = i - lag
        if r >= 0:
            remote(r).wait_recv()
            if r >= 1:
                store(r - 1).wait()
                if r - 1 + ns < k_all:
                    pl.semaphore_signal(credit, inc=1, device_id=peer, device_id_type=MESH)
            if combine:
                own_load(r).wait()
                res_buf[r % ns] = (recv_buf[r % ns].astype(F32) + own_buf[r % ns].astype(F32)).astype(BF)
                if r + 1 < k_all:
                    own_load(r + 1).start()
            store(r).start()
    store(k_all - 1).wait()
    for i in range(max(0, k_all - ns), k_all):
        remote(i).wait_send()


def _gather_halves(name, arrays):
    n = len(arrays)
    blk = [tuple(a.shape[1:]) for a in arrays]
    splits = []
    for shp in blk:
        s = 1
        while s * 2 <= COMM_SPLIT and shp[0] % (s * 2) == 0 and (shp[0] // (s * 2)) % 16 == 0 \
                and shp[0] // (s * 2) >= SPLIT_MIN_ROWS:
            s *= 2
        splits.append(s)
    rcs = [_stream_rows(shp[0] // s, shp[1], a.dtype.itemsize) for shp, s, a in zip(blk, splits, arrays)]
    slots = [(a, s) for a in range(n) for s in range(splits[a])]
    nch = len(CHIP_FLIPS)

    def body(*refs):
        ins, outs = refs[:n], refs[n:2 * n]
        bufs = refs[2 * n:4 * n]
        sems = refs[4 * n:4 * n + 6]
        ici_send, ici_recv = refs[4 * n + 6:]
        x, y, c = lax.axis_index("x"), lax.axis_index("y"), lax.axis_index("c")
        chip = 2 * x + y
        ici = {}
        for si, (a, s) in enumerate(slots):
            rows = pl.ds(s * (blk[a][0] // splits[a]), blk[a][0] // splits[a])
            for j, f in enumerate(CHIP_FLIPS):
                cp = pltpu.make_async_remote_copy(
                    src_ref=ins[a].at[c, rows], dst_ref=outs[a].at[2 * chip + c, rows],
                    send_sem=ici_send.at[si * nch + j], recv_sem=ici_recv.at[si * nch + j],
                    device_id=(_flip(x, f[0]), _flip(y, f[1]), c), device_id_type=MESH)
                cp.start()
                ici[a, s, j] = cp
        for a in range(n):
            chunks = []
            per = blk[a][0] // splits[a]
            for s in range(splits[a]):
                for j, f in enumerate(CHIP_FLIPS):
                    other = 2 * (2 * _flip(x, f[0]) + _flip(y, f[1]))
                    for k in range(per // rcs[a]):
                        rows = pl.ds(s * per + k * rcs[a], rcs[a])
                        chunks.append(dict(src=outs[a].at[other + c, rows], dst=outs[a].at[other + 1 - c, rows],
                                           pre=ici[a, s, j].wait_recv if k == 0 else None))
            _run_stream((x, y, 1 - c), chunks, bufs[2 * a:2 * a + 2], sems, a, False)
        for cp in ici.values():
            cp.wait_send()

    any_spec = pl.BlockSpec(memory_space=pl.ANY)
    scratch = _stream_scratch([(rc, shp[1], a.dtype) for rc, shp, a in zip(rcs, blk, arrays)], False)
    scratch += [pltpu.SemaphoreType.DMA((len(slots) * nch,)), pltpu.SemaphoreType.DMA((len(slots) * nch,))]
    outs = _pcall(body, name=name,
                  out_shape=[jax.ShapeDtypeStruct((N_DEV,) + shp, a.dtype) for shp, a in zip(blk, arrays)],
                  in_specs=[any_spec] * n, out_specs=[any_spec] * n, scratch_shapes=scratch,
                  compiler_params=_params())(*arrays)
    first = 2 * (2 * lax.axis_index("x") + lax.axis_index("y"))
    return [lax.dynamic_update_slice_in_dim(o, a, first, 0) for o, a in zip(outs, arrays)]


HBM_SPEC = pl.BlockSpec(memory_space=pltpu.HBM)
SEM_SPEC = pl.BlockSpec(memory_space=pltpu.SEMAPHORE)
SIDE_EFFECT = pltpu.SideEffectType.DATAFLOW_SIDE_EFFECTING


def _chip_copies(srcs, lands, send_sems, recv_sems, gather):
    x, y, c = lax.axis_index("x"), lax.axis_index("y"), lax.axis_index("c")
    chip = 2 * x + y
    copies = []
    for a in range(len(srcs)):
        for j, f in enumerate(CHIP_FLIPS):
            px, py = _flip(x, f[0]), _flip(y, f[1])
            k = a * len(CHIP_FLIPS) + j
            src = srcs[a].at[c] if gather else srcs[a].at[2 * px + py]
            dst = lands[a].at[2 * chip + c] if gather else lands[a].at[chip]
            copies.append(pltpu.make_async_remote_copy(src_ref=src, dst_ref=dst, send_sem=send_sems.at[k],
                                                       recv_sem=recv_sems.at[k], device_id=(px, py, c),
                                                       device_id_type=MESH))
    return copies


def _chips_start(name, arrays, gather):
    n = len(arrays)
    lead = N_DEV if gather else N_SHARD
    lands = [pltpu.with_memory_space_constraint(lax.empty((lead,) + tuple(a.shape[1:]), a.dtype), pltpu.HBM)
             for a in arrays]
    ncp = n * len(CHIP_FLIPS)

    def body(*refs):
        srcs, lnd = refs[:n], refs[n:2 * n]
        for cp in _chip_copies(srcs, lnd, refs[2 * n], refs[2 * n + 1], gather):
            cp.start()
        token = refs[-1]
        token[...] = jnp.zeros_like(token)

    res = _pcall(
        body, name=name,
        out_shape=(pltpu.SemaphoreType.DMA((ncp,)), pltpu.SemaphoreType.DMA((ncp,)),
                   *[pltpu.HBM(a.shape, a.dtype) for a in arrays], *[pltpu.HBM(l.shape, l.dtype) for l in lands],
                   jax.ShapeDtypeStruct((8, 128), F32)),
        in_specs=[HBM_SPEC] * (2 * n),
        out_specs=(SEM_SPEC, SEM_SPEC, *[HBM_SPEC] * (2 * n), pl.BlockSpec(memory_space=pltpu.VMEM)),
        input_output_aliases={i: 2 + i for i in range(2 * n)},
        compiler_params=pltpu.CompilerParams(has_side_effects=SIDE_EFFECT),
    )(*[pltpu.with_memory_space_constraint(a, pltpu.HBM) for a in arrays], *lands)
    return dict(send=res[0], recv=res[1], srcs=list(res[2:2 + n]), lands=list(res[2 + n:2 + 2 * n]), token=res[-1],
                gather=gather)


def _chips_wait(name, handle, after):
    n = len(handle['srcs'])
    gather = handle['gather']

    def body(*refs):
        srcs, lnd = refs[:n], refs[n:2 * n]
        for cp in _chip_copies(srcs, lnd, refs[2 * n], refs[2 * n + 1], gather):
            cp.wait_send()
            cp.wait_recv()

    both = handle['srcs'] + handle['lands']
    res = _pcall(
        body, name=name, out_shape=tuple(pltpu.HBM(b.shape, b.dtype) for b in both),
        in_specs=[HBM_SPEC] * (2 * n) + [SEM_SPEC, SEM_SPEC, pl.BlockSpec(memory_space=pl.ANY)],
        out_specs=tuple([HBM_SPEC] * (2 * n)), input_output_aliases={i: i for i in range(2 * n)},
        compiler_params=pltpu.CompilerParams(has_side_effects=SIDE_EFFECT),
    )(*both, handle['send'], handle['recv'], after)
    return list(res[:n]), list(res[n:])


def _forward_sibling(name, landed, own):
    n = len(landed)
    blk = [tuple(a.shape[1:]) for a in landed]
    rcs = [_stream_rows(shp[0], shp[1], a.dtype.itemsize) for shp, a in zip(blk, landed)]

    def body(*refs):
        ins, outs = refs[:n], refs[n:2 * n]
        bufs = refs[2 * n:4 * n]
        sems = refs[4 * n:4 * n + 6]
        x, y, c = lax.axis_index("x"), lax.axis_index("y"), lax.axis_index("c")
        for a in range(n):
            chunks = []
            for f in CHIP_FLIPS:
                other = 2 * (2 * _flip(x, f[0]) + _flip(y, f[1]))
                for k in range(blk[a][0] // rcs[a]):
                    rows = pl.ds(k * rcs[a], rcs[a])
                    chunks.append(dict(src=ins[a].at[other + c, rows], dst=outs[a].at[other + 1 - c, rows]))
            _run_stream((x, y, 1 - c), chunks, bufs[2 * a:2 * a + 2], sems, a, False)

    any_spec = pl.BlockSpec(memory_space=pl.ANY)
    scratch = _stream_scratch([(rc, shp[1], a.dtype) for rc, shp, a in zip(rcs, blk, landed)], False)
    outs = _pcall(body, name=name, out_shape=[jax.ShapeDtypeStruct(a.shape, a.dtype) for a in landed],
                  in_specs=[any_spec] * n, out_specs=[any_spec] * n, scratch_shapes=scratch,
                  input_output_aliases={a: a for a in range(n)}, compiler_params=_params())(*landed)
    first = 2 * (2 * lax.axis_index("x") + lax.axis_index("y"))
    return [lax.dynamic_update_slice_in_dim(o, a, first, 0) for o, a in zip(outs, own)]


def _presum_sibling(name, arrays):
    n = len(arrays)
    blk = [tuple(a.shape[1:]) for a in arrays]
    rcs = [_stream_rows(shp[0], shp[1], a.dtype.itemsize) for shp, a in zip(blk, arrays)]

    def body(*refs):
        ins, outs = refs[:n], refs[n:2 * n]
        bufs = refs[2 * n:6 * n]
        sems = refs[6 * n:6 * n + 6]
        x, y, c = lax.axis_index("x"), lax.axis_index("y"), lax.axis_index("c")
        for a in range(n):
            chunks = []
            for sh in range(N_SHARD):
                for k in range(blk[a][0] // rcs[a]):
                    rows = pl.ds(k * rcs[a], rcs[a])
                    chunks.append(dict(src=ins[a].at[2 * sh + 1 - c, rows], own=ins[a].at[2 * sh + c, rows],
                                       dst=outs[a].at[sh, rows]))
            _run_stream((x, y, 1 - c), chunks, bufs[4 * a:4 * a + 4], sems, a, True)

    any_spec = pl.BlockSpec(memory_space=pl.ANY)
    scratch = _stream_scratch([(rc, shp[1], a.dtype) for rc, shp, a in zip(rcs, blk, arrays)], True)
    outs = _pcall(body, name=name,
                  out_shape=[jax.ShapeDtypeStruct((N_SHARD,) + shp, BF) for shp in blk],
                  in_specs=[any_spec] * n, out_specs=[any_spec] * n, scratch_shapes=scratch,
                  compiler_params=_params())(*arrays)
    return list(outs)


def _to_chips(name, arrays):
    def plan(x, y, c):
        return [(f, ('in', 2 * _flip(x, f[0]) + _flip(y, f[1])), 2 * x + y, None) for f in CHIP_FLIPS]
    outs = _exchange(name, arrays, plan, N_SHARD, False)
    chip = 2 * lax.axis_index("x") + lax.axis_index("y")
    return [lax.dynamic_update_slice_in_dim(o, lax.dynamic_slice_in_dim(a, chip, 1, 0), chip, 0)
            for o, a in zip(outs, arrays)]


def _swap_sibling(name, arrays):
    n = len(arrays)
    rcs = [_stream_rows(a.shape[0], a.shape[1], a.dtype.itemsize) for a in arrays]

    def body(*refs):
        ins, outs = refs[:n], refs[n:2 * n]
        bufs = refs[2 * n:4 * n]
        sems = refs[4 * n:4 * n + 6]
        x, y, c = lax.axis_index("x"), lax.axis_index("y"), lax.axis_index("c")
        for a in range(n):
            chunks = [dict(src=ins[a].at[pl.ds(k * rcs[a], rcs[a])], dst=outs[a].at[pl.ds(k * rcs[a], rcs[a])])
                      for k in range(arrays[a].shape[0] // rcs[a])]
            _run_stream((x, y, 1 - c), chunks, bufs[2 * a:2 * a + 2], sems, a, False)

    any_spec = pl.BlockSpec(memory_space=pl.ANY)
    scratch = _stream_scratch([(rc, a.shape[1], a.dtype) for rc, a in zip(rcs, arrays)], False)
    got = _pcall(body, name=name, out_shape=[jax.ShapeDtypeStruct(a.shape, a.dtype) for a in arrays],
                 in_specs=[any_spec] * n, out_specs=[any_spec] * n, scratch_shapes=scratch,
                 compiler_params=_params())(*arrays)
    south = lax.axis_index("c") == 0
    return [jnp.where(south, jnp.stack([a, g]), jnp.stack([g, a])) for a, g in zip(arrays, got)]


def _sum_lead(name, arr, out_dtype=F32):
    n, r, cdim = arr.shape
    tr = r
    limit = (4 << 20) // (n * cdim * arr.dtype.itemsize)
    if r > limit:
        tr = _pick(r, max(limit, 16), 16)

    def body(x_ref, o_ref):
        acc = x_ref[0].astype(F32)
        for d in range(1, n):
            acc = acc + x_ref[d].astype(F32)
        o_ref[...] = acc.astype(out_dtype)

    return _pcall(body, name=name, grid=(r // tr,),
                  in_specs=[pl.BlockSpec((n, tr, cdim), lambda i: (0, i, 0))],
                  out_specs=pl.BlockSpec((tr, cdim), lambda i: (i, 0)),
                  out_shape=jax.ShapeDtypeStruct((r, cdim), out_dtype),
                  compiler_params=_params(("arbitrary",)))(arr)


def _rows_call(name, fn, n_rows, tm, rows, consts, mod, outs, acc_w=None, h_tiles=None):
    nt = n_rows // tm
    ht = nt if h_tiles is None else h_tiles
    ng = 1 if mod is None else mod.shape[0]
    n_r, n_c, n_o = len(rows), len(consts), len(outs)
    has_mod = mod is not None

    def body(*refs):
        i = pl.program_id(0)
        first = (i % ht) == 0
        row_refs, const_refs = refs[:n_r], refs[n_r:n_r + n_c]
        p = n_r + n_c
        mod_tile = refs[p][...] if has_mod else None
        p += int(has_mod)
        out_refs = refs[p:p + n_o]
        o, acc = fn([r[...] for r in row_refs], [r[...] for r in const_refs], mod_tile)
        for r, v in zip(out_refs, o):
            r[...] = v.astype(r.dtype)
        if acc_w is not None:
            acc_ref = refs[p + n_o]

            @pl.when(first)
            def _():
                acc_ref[...] = jnp.zeros_like(acc_ref)

            for k, v in acc.items():
                acc_ref[k:k + 1, :] += v

    in_specs = [pl.BlockSpec((tm, r.shape[1]), lambda i: (i, 0)) for r in rows]
    in_specs += [pl.BlockSpec(cst.shape, lambda i, nd=cst.ndim: (0,) * nd) for cst in consts]
    args = list(rows) + list(consts)
    if has_mod:
        in_specs.append(pl.BlockSpec((None,) + mod.shape[1:], lambda i: (i // ht, 0, 0)))
        args.append(mod)
    out_shape = [jax.ShapeDtypeStruct((n_rows, w), dt) for w, dt in outs]
    out_specs = [pl.BlockSpec((tm, w), lambda i: (i, 0)) for w, _ in outs]
    if acc_w is not None:
        out_shape.append(jax.ShapeDtypeStruct((ng, 8, acc_w), F32))
        out_specs.append(pl.BlockSpec((None, 8, acc_w), lambda i: (i // ht, 0, 0)))
    res = _pcall(body, name=name, grid=(nt,), in_specs=in_specs, out_specs=out_specs, out_shape=out_shape,
                 compiler_params=_params(("arbitrary",)))(*args)
    return list(res)


def _rms(s):
    r = lax.rsqrt(jnp.mean(s * s, axis=1, keepdims=True) + RMS_EPS)
    return s * r, r


def _rms_bwd(dn, n, r):
    return r * (dn - n * jnp.mean(dn * n, axis=1, keepdims=True))


def _adaln_fwd(name, s, gains, gain_row, mod, k, tm, h_tiles):
    def fn(rows, consts, m):
        n, _ = _rms(rows[0])
        y = n * consts[0][gain_row:gain_row + 1, :]
        return [y * (1.0 + m[3 * k + 1:3 * k + 2, :]) + m[3 * k:3 * k + 1, :]], {}

    d = s.shape[1]
    return _rows_call(name, fn, s.shape[0], tm, [s], [gains], mod, [(d, BF)], h_tiles=h_tiles)[0]


def _adaln_bwd(name, s, du, ds_res, gains, gain_row, mod, k, tm, h_tiles):
    def fn(rows, consts, m):
        sv, duv, res = rows
        gain = consts[0][gain_row:gain_row + 1, :]
        n, r = _rms(sv)
        y = n * gain
        dy = duv * (1.0 + m[3 * k + 1:3 * k + 2, :])
        acc = {0: jnp.sum(duv, axis=0, keepdims=True), 1: jnp.sum(duv * y, axis=0, keepdims=True),
               2: jnp.sum(dy * n, axis=0, keepdims=True)}
        return [_rms_bwd(dy * gain, n, r) + res], acc

    d = s.shape[1]
    return _rows_call(name, fn, s.shape[0], tm, [s, du, ds_res], [gains], mod, [(d, F32)], acc_w=d, h_tiles=h_tiles)


def _resid_bwd(name, ds_out, o, mod, k, cst, tm, h_tiles):
    def fn(rows, consts, m):
        dsv, ov = rows
        gate = m[3 * k + 2:3 * k + 3, :]
        return [cst * gate * dsv], {0: jnp.sum(cst * ov * dsv, axis=0, keepdims=True)}

    d = o.shape[1]
    return _rows_call(name, fn, o.shape[0], tm, [ds_out, o], [], mod, [(d, BF)], acc_w=d, h_tiles=h_tiles)


def _mm(name, a, b, mode, tm=256, tn=512, out_dtypes=(F32,), epi=None, epi_args=(), epi_kinds=(), a_pre=None):
    if mode == 'nn':
        (m, kd), nd = a.shape, b.shape[1]
    elif mode == 'nt':
        (m, kd), nd = a.shape, b.shape[0]
    else:
        (kd, m), nd = a.shape, b.shape[1]
    tm = _pick(m, tm, 16) if m % tm else tm
    tn = _pick(nd, tn, 128) if nd % tn else tn
    dims = {'nn': NN, 'nt': NT, 'tn': TN}[mode]
    n_e, n_o = len(epi_args), len(out_dtypes)

    def body(*refs):
        i = pl.program_id(1)
        av = refs[0][...]
        if a_pre is not None:
            av = a_pre(av)
        acc = _dot(av, refs[1][...], dims)
        res = (acc,) if epi is None else epi(acc, i, *[r[...] for r in refs[2:2 + n_e]])
        for r, v in zip(refs[2 + n_e:], res):
            r[...] = v.astype(r.dtype)

    if mode == 'nn':
        specs = [pl.BlockSpec((tm, kd), lambda j, i: (i, 0)), pl.BlockSpec((kd, tn), lambda j, i: (0, j))]
    elif mode == 'nt':
        specs = [pl.BlockSpec((tm, kd), lambda j, i: (i, 0)), pl.BlockSpec((tn, kd), lambda j, i: (j, 0))]
    else:
        specs = [pl.BlockSpec((kd, tm), lambda j, i: (0, i)), pl.BlockSpec((kd, tn), lambda j, i: (0, j))]
    for arr, kind in zip(epi_args, epi_kinds):
        if kind == 'mn':
            specs.append(pl.BlockSpec((tm, tn), lambda j, i: (i, j)))
        elif kind == 'n':
            specs.append(pl.BlockSpec((1, tn), lambda j, i: (0, j)))
        elif kind == 'mt':
            specs.append(pl.BlockSpec((tm, arr.shape[1]), lambda j, i: (i, 0)))
        else:
            specs.append(pl.BlockSpec(arr.shape, lambda j, i, nd_=arr.ndim: (0,) * nd_))
    res = _pcall(body, name=name, grid=(nd // tn, m // tm), in_specs=specs,
                 out_specs=[pl.BlockSpec((tm, tn), lambda j, i: (i, j))] * n_o,
                 out_shape=[jax.ShapeDtypeStruct((m, nd), dt) for dt in out_dtypes],
                 compiler_params=_params(("arbitrary", "arbitrary")))(a, b, *epi_args)
    return res[0] if n_o == 1 else list(res)


def _row_gate(mod, k3, i, tm, n_lat):
    g0 = mod[0, k3:k3 + 1, :]
    if mod.shape[0] == 1:
        return g0
    rid = i * tm + lax.broadcasted_iota(jnp.int32, (tm, 1), 0)
    return jnp.where(rid < n_lat, g0, mod[1, k3:k3 + 1, :])


def _ffn_up(name, u, wg, wu, base, tm):
    r, d = u.shape
    nch, _, _, fc = wg.shape

    def body(u_ref, wg_ref, wu_ref, a_ref, b_ref, h_ref):
        uv = u_ref[...]
        a = _dot(uv, wg_ref[...], NN)
        b = _dot(uv, wu_ref[...], NN)
        a_ref[...] = a.astype(BF)
        b_ref[...] = b.astype(BF)
        h_ref[...] = (_silu(a) * b).astype(BF)

    chunk = pl.BlockSpec((None, tm, fc), lambda j, i: (j, i, 0))
    return _pcall(body, name=name, grid=(nch, r // tm),
                  in_specs=[pl.BlockSpec((tm, d), lambda j, i: (i, 0)),
                            pl.BlockSpec((None, None, d, fc), lambda j, i: (j, base, 0, 0)),
                            pl.BlockSpec((None, None, d, fc), lambda j, i: (j, base, 0, 0))],
                  out_specs=[chunk] * 3, out_shape=[jax.ShapeDtypeStruct((nch, r, fc), BF)] * 3,
                  compiler_params=_params(("arbitrary", "arbitrary")))(u, wg, wu)


def _ffn_down(name, hid, wd, wd_blk, s, mod, k, n_lat, tm):
    nch, r, fc = hid.shape
    d = wd.shape[2]

    def body(h_ref, w_ref, s_ref, m_ref, so_ref, o_ref, acc_ref):
        i, j = pl.program_id(0), pl.program_id(1)
        part = _dot(h_ref[...], w_ref[...], NN)

        @pl.when(j == 0)
        def _():
            acc_ref[...] = part

        @pl.when(j > 0)
        def _():
            acc_ref[...] += part

        @pl.when(j == nch - 1)
        def _():
            o = acc_ref[...]
            o_ref[...] = o
            so_ref[...] = s_ref[...] + 0.5 * _row_gate(m_ref[...], 3 * k + 2, i, tm, n_lat) * o

    row = pl.BlockSpec((tm, d), lambda i, j: (i, 0))
    return _pcall(body, name=name, grid=(r // tm, nch),
                  in_specs=[pl.BlockSpec((None, tm, fc), lambda i, j: (j, i, 0)),
                            pl.BlockSpec((None, fc, d), lambda i, j: (j, wd_blk, 0)), row,
                            pl.BlockSpec(mod.shape, lambda i, j: (0, 0, 0))],
                  out_specs=[row, row], out_shape=[jax.ShapeDtypeStruct((r, d), F32)] * 2,
                  scratch_shapes=[pltpu.VMEM((tm, d), F32)],
                  compiler_params=_params(("arbitrary", "arbitrary")))(hid, wd, s, mod)


def _ffn_dhid(name, d_o, wd, wd_blk, a, b, tm):
    r, d = d_o.shape
    nch, _, fc = a.shape

    def body(g_ref, w_ref, a_ref, b_ref, da_ref, db_ref):
        dh = _dot(g_ref[...], w_ref[...], NT)
        av, bv = a_ref[...].astype(F32), b_ref[...].astype(F32)
        da_ref[...] = (dh * bv * _dsilu(av)).astype(BF)
        db_ref[...] = (dh * _silu(av)).astype(BF)

    chunk = pl.BlockSpec((None, tm, fc), lambda j, i: (j, i, 0))
    return _pcall(body, name=name, grid=(nch, r // tm),
                  in_specs=[pl.BlockSpec((tm, d), lambda j, i: (i, 0)),
                            pl.BlockSpec((None, fc, d), lambda j, i: (j, wd_blk, 0)), chunk, chunk],
                  out_specs=[chunk] * 2, out_shape=[jax.ShapeDtypeStruct((nch, r, fc), BF)] * 2,
                  compiler_params=_params(("arbitrary", "arbitrary")))(d_o, wd, a, b)


def _ffn_du(name, da, db, wg, wu, base, tm):
    nch, r, fc = da.shape
    d = wg.shape[2]

    def body(da_ref, db_ref, wg_ref, wu_ref, o_ref, acc_ref):
        j = pl.program_id(1)
        part = _dot(da_ref[...], wg_ref[...], NT) + _dot(db_ref[...], wu_ref[...], NT)

        @pl.when(j == 0)
        def _():
            acc_ref[...] = part

        @pl.when(j > 0)
        def _():
            acc_ref[...] += part

        @pl.when(j == nch - 1)
        def _():
            o_ref[...] = acc_ref[...]

    chunk = pl.BlockSpec((None, tm, fc), lambda i, j: (j, i, 0))
    return _pcall(body, name=name, grid=(r // tm, nch),
                  in_specs=[chunk, chunk, pl.BlockSpec((None, None, d, fc), lambda i, j: (j, base, 0, 0)),
                            pl.BlockSpec((None, None, d, fc), lambda i, j: (j, base, 0, 0))],
                  out_specs=pl.BlockSpec((tm, d), lambda i, j: (i, 0)),
                  out_shape=jax.ShapeDtypeStruct((r, d), F32), scratch_shapes=[pltpu.VMEM((tm, d), F32)],
                  compiler_params=_params(("arbitrary", "arbitrary")))(da, db, wg, wu)


def _ffn_dw_in(name, u, dz, tmm, grads, idx):
    r, d = u.shape
    nch, _, fc = dz.shape
    nb = d // tmm

    def body(u_ref, z_ref, g_ref, o_ref):
        o_ref[...] = _dot(u_ref[...], z_ref[...], TN).astype(o_ref.dtype)

    return _pcall(body, name=name, grid=(nch, nb),
                  in_specs=[pl.BlockSpec((r, tmm), lambda j, mi: (0, mi)),
                            pl.BlockSpec((None, r, fc), lambda j, mi: (j, 0, 0)),
                            pl.BlockSpec(memory_space=pl.ANY)],
                  out_specs=pl.BlockSpec((None, tmm, fc), lambda j, mi: (j, idx * nb + mi, 0)),
                  out_shape=jax.ShapeDtypeStruct(grads.shape, grads.dtype), input_output_aliases={2: 0},
                  compiler_params=_params(("arbitrary", "arbitrary")))(u, dz, grads)


def _ffn_dw_down(name, hid, d_o, tn, grads, idx):
    nch, r, fc = hid.shape
    d = d_o.shape[1]

    def body(h_ref, g_ref, acc_ref, o_ref):
        o_ref[...] = _dot(h_ref[...], g_ref[...], TN).astype(o_ref.dtype)

    return _pcall(body, name=name, grid=(nch, d // tn),
                  in_specs=[pl.BlockSpec((None, r, fc), lambda j, ni: (j, 0, 0)),
                            pl.BlockSpec((r, tn), lambda j, ni: (0, ni)),
                            pl.BlockSpec(memory_space=pl.ANY)],
                  out_specs=pl.BlockSpec((None, fc, tn), lambda j, ni: (j, idx, ni)),
                  out_shape=jax.ShapeDtypeStruct(grads.shape, grads.dtype), input_output_aliases={2: 0},
                  compiler_params=_params(("arbitrary", "arbitrary")))(hid, d_o, grads)


def _partner(x):
    n = x.shape[1]
    lane = lax.broadcasted_iota(jnp.int32, x.shape, 1)
    return jnp.where((lane & 15) < 8, pltpu.roll(x, n - 8, 1), pltpu.roll(x, 8, 1))


def _rope(x, ct, st):
    reps = x.shape[1] // ct.shape[1]
    if reps > 1:
        ct, st = jnp.tile(ct, (1, reps)), jnp.tile(st, (1, reps))
    return x * ct + _partner(x) * st


def _rope_t(dy, ct, st):
    reps = dy.shape[1] // ct.shape[1]
    if reps > 1:
        ct, st = jnp.tile(ct, (1, reps)), jnp.tile(st, (1, reps))
    return dy * ct + _partner(dy * st)


def _rope_tables(t_len, g_len, lane0):
    half = QK_ROPE // 4
    pos = jnp.arange(t_len)
    row = (pos // GRID_W).astype(F32)
    col = (pos % GRID_W).astype(F32)
    freqs = jnp.power(ROPE_THETA, -jnp.arange(0, QK_ROPE // 2, 2, dtype=F32) / (QK_ROPE // 2))
    ang_r, ang_c = row[:, None] * freqs, col[:, None] * freqs
    cs = jnp.concatenate([jnp.cos(ang_r)] * 2 + [jnp.cos(ang_c)] * 2, axis=1)
    sn = jnp.concatenate([-jnp.sin(ang_r), jnp.sin(ang_r), -jnp.sin(ang_c), jnp.sin(ang_c)], axis=1)
    assert cs.shape[1] == 4 * half == QK_ROPE
    ct = jnp.ones((t_len + g_len, HEAD_PAD), F32).at[:t_len, lane0:lane0 + QK_ROPE].set(cs)
    st = jnp.zeros((t_len + g_len, HEAD_PAD), F32).at[:t_len, lane0:lane0 + QK_ROPE].set(sn)
    return ct, st


def _attn_fwd(name, q, kp, vp, n_q, q_off, n_k, k_blk, heads, tq, scale):
    qb = q_off // tq

    def body(q_ref, k_ref, v_ref, o_ref, l_ref):
        s = _dot(q_ref[...], k_ref[...], NT) * scale
        m = jnp.max(s, axis=1, keepdims=True)
        p = jnp.exp(s - m)
        l = jnp.sum(p, axis=1, keepdims=True)
        o_ref[...] = (_dot(p, v_ref[...], NN) / l).astype(BF)
        l_ref[...] = jnp.broadcast_to(m + jnp.log(l), l_ref.shape)

    hw = heads * HEAD_PAD
    blk = pl.BlockSpec((tq, HEAD_PAD), lambda h, i: (i, h))
    kv = pl.BlockSpec((n_k, HEAD_PAD), lambda h, i: (k_blk, h))
    return _pcall(body, name=name, grid=(heads, n_q // tq),
                  in_specs=[pl.BlockSpec((tq, HEAD_PAD), lambda h, i: (i + qb, h)), kv, kv],
                  out_specs=[blk, blk],
                  out_shape=[jax.ShapeDtypeStruct((n_q, hw), BF), jax.ShapeDtypeStruct((n_q, hw), F32)],
                  compiler_params=_params(("arbitrary", "arbitrary")))(q, kp, vp)


def _attn_bwd(name, q, kp, vp, cat, dcat, lse, n_q, q_off, n_k, k_blk, heads, tq, scale, col_blk):
    qb = q_off // tq

    def body(q_ref, k_ref, v_ref, o_ref, do_ref, l_ref, dq_ref, dk_ref, dv_ref):
        i = pl.program_id(1)
        qv, kv_, vv = q_ref[...], k_ref[...], v_ref[...]
        dov = do_ref[...]
        s = _dot(qv, kv_, NT) * scale
        p = jnp.exp(s - l_ref[...][:, 0:1])
        dp = _dot(dov, vv, NT)
        delta = jnp.sum(dov * o_ref[...].astype(F32), axis=1, keepdims=True)
        ds = (p * (dp - delta) * scale).astype(BF)
        dq_ref[...] = _dot(ds, kv_, NN)
        dk = _dot(ds, qv, TN)
        dv = _dot(p, dov, TN)

        @pl.when(i == 0)
        def _():
            dk_ref[...] = dk
            dv_ref[...] = dv

        @pl.when(i > 0)
        def _():
            dk_ref[...] += dk
            dv_ref[...] += dv

    hw = heads * HEAD_PAD
    qspec = pl.BlockSpec((tq, HEAD_PAD), lambda h, i: (i + qb, h))
    cspec = pl.BlockSpec((tq, HEAD_PAD), lambda h, i: (i + qb, col_blk + h))
    kv = pl.BlockSpec((n_k, HEAD_PAD), lambda h, i: (k_blk, h))
    acc = pl.BlockSpec((n_k, HEAD_PAD), lambda h, i: (0, h))
    blk = pl.BlockSpec((tq, HEAD_PAD), lambda h, i: (i, h))
    return _pcall(body, name=name, grid=(heads, n_q // tq),
                  in_specs=[qspec, kv, kv, cspec, cspec, blk], out_specs=[blk, acc, acc],
                  out_shape=[jax.ShapeDtypeStruct((n_q, hw), F32), jax.ShapeDtypeStruct((n_k, hw), F32),
                             jax.ShapeDtypeStruct((n_k, hw), F32)],
                  compiler_params=_params(("arbitrary", "arbitrary")))(q, kp, vp, cat, dcat, lse)


def _shift(x, k):
    return pltpu.roll(x, k % x.shape[0], 0)


def _window_sum(v, w, mirrored):
    n, gd = v.shape
    pad = jnp.zeros((POOL_PAD, gd), F32)
    e = jnp.concatenate([pad, v, pad], axis=0)
    acc = e + _shift(e, -1 if mirrored else 1)
    step = 1
    while 2 * step < w:
        acc = _shift(acc, step) + _shift(acc, -step)
        step *= 2
    return acc[POOL_PAD:POOL_PAD + n]


def _window_count(n, w):
    t = lax.broadcasted_iota(jnp.int32, (n, 1), 0)
    lo = jnp.maximum(t - w // 2, 0)
    hi = jnp.minimum(t + (w - w // 2 - 1), n - 1)
    return (hi - lo + 1).astype(F32)


def _pool_fwd(name, u, pool_w, scale):
    n, pd = u.shape
    ng = len(POOL_WINDOWS)
    gd = pd // ng

    def body(u_ref, w_ref, s_ref, y_ref):
        for g, w in enumerate(POOL_WINDOWS):
            sl = slice(g * gd, (g + 1) * gd)
            ug = u_ref[:, sl]
            p = _window_sum(ug, w, False) / _window_count(n, w) - ug
            y_ref[:, sl] = (_dot(p, w_ref[g], NN) * s_ref[:, sl]).astype(BF)

    return _pcall(body, name=name, out_shape=jax.ShapeDtypeStruct((n, pd), BF),
                  compiler_params=_params())(u, pool_w, scale)


def _pool_bwd(name, u, dcat, pool_w, scale, row_off):
    n, pd = u.shape
    ng = len(POOL_WINDOWS)
    gd = pd // ng

    def body(u_ref, dy_ref, w_ref, s_ref, du_ref, dw_ref, ds_ref):
        ds_ref[...] = jnp.zeros_like(ds_ref)
        for g, w in enumerate(POOL_WINDOWS):
            sl = slice(g * gd, (g + 1) * gd)
            ug, dy, wg = u_ref[:, sl], dy_ref[:, sl], w_ref[g]
            cnt = _window_count(n, w)
            p = _window_sum(ug, w, False) / cnt - ug
            ds_ref[0:1, sl] = jnp.sum(dy * _dot(p, wg, NN), axis=0, keepdims=True)
            dys = dy * s_ref[:, sl]
            dw_ref[g] = _dot(p, dys, TN)
            dp = _dot(dys, wg, NT)
            du_ref[:, sl] = (_window_sum(dp / cnt, w, True) - dp).astype(BF)

    rb = row_off // n
    return _pcall(body, name=name, grid=(1,),
                  in_specs=[pl.BlockSpec((n, pd), lambda i: (0, 0)), pl.BlockSpec((n, pd), lambda i: (rb, 0)),
                            pl.BlockSpec(pool_w.shape, lambda i: (0, 0, 0)), pl.BlockSpec(scale.shape, lambda i: (0, 0))],
                  out_specs=[pl.BlockSpec((n, pd), lambda i: (0, 0)), pl.BlockSpec((ng, gd, gd), lambda i: (0, 0, 0)),
                             pl.BlockSpec((8, pd), lambda i: (0, 0))],
                  out_shape=[jax.ShapeDtypeStruct((n, pd), BF), jax.ShapeDtypeStruct((ng, gd, gd), F32),
                             jax.ShapeDtypeStruct((8, pd), F32)],
                  compiler_params=_params(("arbitrary",)))(u, dcat, pool_w, scale)


def _edge_shift(z, k):
    n = z.shape[0]
    t = lax.broadcasted_iota(jnp.int32, (n, 1), 0)
    keep = (t >= k) if k > 0 else (t < n + k)
    return jnp.where(keep, pltpu.roll(z, k % n, 0), 0.0)


def _conv_fwd(name, p3, cw, tc):
    n, cd = p3.shape[0], p3.shape[1] // 3
    nb = cd // tc

    def body(b_ref, c_ref, v_ref, w_ref, y_ref):
        z = c_ref[...] * v_ref[...]
        w = w_ref[...]
        zc = w[0:1] * _edge_shift(z, 1) + w[1:2] * z + w[2:3] * _edge_shift(z, -1)
        y_ref[...] = (b_ref[...] * zc).astype(BF)

    return _pcall(body, name=name, grid=(nb,),
                  in_specs=[pl.BlockSpec((n, tc), lambda j: (0, j)), pl.BlockSpec((n, tc), lambda j: (0, nb + j)),
                            pl.BlockSpec((n, tc), lambda j: (0, 2 * nb + j)), pl.BlockSpec((3, tc), lambda j: (0, j))],
                  out_specs=pl.BlockSpec((n, tc), lambda j: (0, j)), out_shape=jax.ShapeDtypeStruct((n, cd), BF),
                  compiler_params=_params(("arbitrary",)))(p3, p3, p3, cw)


def _conv_bwd(name, p3, cw, dy, tc):
    n, cd = dy.shape
    nb = cd // tc

    def body(b_ref, c_ref, v_ref, w_ref, dy_ref, dp_ref, dw_ref):
        cv, vv, w, dyv = c_ref[...], v_ref[...], w_ref[...], dy_ref[...]
        z = cv * vv
        zl, zr = _edge_shift(z, 1), _edge_shift(z, -1)
        zc = w[0:1] * zl + w[1:2] * z + w[2:3] * zr
        dzc = dyv * b_ref[...]
        dz = w[0:1] * _edge_shift(dzc, -1) + w[1:2] * dzc + w[2:3] * _edge_shift(dzc, 1)
        dp_ref[0] = (dyv * zc).astype(BF)
        dp_ref[1] = (dz * vv).astype(BF)
        dp_ref[2] = (dz * cv).astype(BF)
        dw_ref[...] = jnp.zeros_like(dw_ref)
        dw_ref[0:1, :] = jnp.sum(dzc * zl, axis=0, keepdims=True)
        dw_ref[1:2, :] = jnp.sum(dzc * z, axis=0, keepdims=True)
        dw_ref[2:3, :] = jnp.sum(dzc * zr, axis=0, keepdims=True)

    col = pl.BlockSpec((n, tc), lambda j: (0, j))
    return _pcall(body, name=name, grid=(nb,),
                  in_specs=[col, pl.BlockSpec((n, tc), lambda j: (0, nb + j)),
                            pl.BlockSpec((n, tc), lambda j: (0, 2 * nb + j)), pl.BlockSpec((3, tc), lambda j: (0, j)), col],
                  out_specs=[pl.BlockSpec((3, n, tc), lambda j: (0, 0, j)), pl.BlockSpec((8, tc), lambda j: (0, j))],
                  out_shape=[jax.ShapeDtypeStruct((3, n, cd), BF), jax.ShapeDtypeStruct((8, cd), F32)],
                  compiler_params=_params(("arbitrary",)))(p3, p3, p3, cw, dy)


def _conv_din(name, dp3, w_in, tm):
    _, n, cd = dp3.shape
    d = w_in.shape[0]

    def body(a_ref, w_ref, o_ref, acc_ref):
        j = pl.program_id(1)
        part = _dot(a_ref[...], w_ref[...], NT)

        @pl.when(j == 0)
        def _():
            acc_ref[...] = part

        @pl.when(j > 0)
        def _():
            acc_ref[...] += part

        @pl.when(j == 2)
        def _():
            o_ref[...] = acc_ref[...]

    return _pcall(body, name=name, grid=(n // tm, 3),
                  in_specs=[pl.BlockSpec((None, tm, cd), lambda i, j: (j, i, 0)),
                            pl.BlockSpec((d, cd), lambda i, j: (0, j))],
                  out_specs=pl.BlockSpec((tm, d), lambda i, j: (i, 0)), out_shape=jax.ShapeDtypeStruct((n, d), F32),
                  scratch_shapes=[pltpu.VMEM((tm, d), F32)],
                  compiler_params=_params(("arbitrary", "arbitrary")))(dp3, w_in)


def _conv_dw_in(name, u, dp3, tmm, tn):
    n, d = u.shape
    cd = dp3.shape[2]
    nb = cd // tn

    def body(u_ref, z_ref, o_ref):
        o_ref[...] = _dot(u_ref[...], z_ref[...], TN)

    return _pcall(body, name=name, grid=(3 * nb, d // tmm),
                  in_specs=[pl.BlockSpec((n, tmm), lambda j, mi: (0, mi)),
                            pl.BlockSpec((None, n, tn), lambda j, mi: (j // nb, 0, j % nb))],
                  out_specs=pl.BlockSpec((tmm, tn), lambda j, mi: (mi, j)),
                  out_shape=jax.ShapeDtypeStruct((d, 3 * cd), F32),
                  compiler_params=_params(("arbitrary", "arbitrary")))(u, dp3)


def _loss_head(name, h, target, gain, tm):
    d = h.shape[1]

    def fn(rows, consts, m):
        hv, tv = rows
        g = consts[0][0:1, :]
        n, r = _rms(hv)
        err = n * g - tv
        dy = err / d
        loss = 0.5 * jnp.sum(err * err) / d
        acc = {0: jnp.sum(dy * n, axis=0, keepdims=True), 1: jnp.full((1, d), loss, F32)}
        return [_rms_bwd(dy * g, n, r)], acc

    return _rows_call(name, fn, h.shape[0], tm, [h, target], [gain], None, [(d, F32)], acc_w=d)


def _adamw(name, w, g, m, v):
    shape = w.shape
    cdim = shape[-1]
    r = max(1, math.prod(shape[:-1]))
    tr = r
    if r * cdim * 4 > (3 << 19):
        tr = _pick(r, max(8, (3 << 19) // (cdim * 4)), 8)
    c1 = 1.0 / (1.0 - ADAM_B1 ** ADAM_STEP)
    c2 = 1.0 / (1.0 - ADAM_B2 ** ADAM_STEP)

    def body(w_ref, g_ref, m_ref, v_ref, d_ref, nm_ref, nv_ref):
        gv = g_ref[...]
        nm = ADAM_B1 * m_ref[...] + (1.0 - ADAM_B1) * gv
        nv = ADAM_B2 * v_ref[...] + (1.0 - ADAM_B2) * (gv * gv)
        nm_ref[...] = nm
        nv_ref[...] = nv
        d_ref[...] = -ADAM_LR * ((nm * c1) / (jnp.sqrt(nv * c2) + ADAM_EPS) + ADAM_WD * w_ref[...])

    spec = pl.BlockSpec((tr, cdim), lambda i: (i, 0))
    res = _pcall(body, name=name, grid=(r // tr,), in_specs=[spec] * 4, out_specs=[spec] * 3,
                 out_shape=[jax.ShapeDtypeStruct((r, cdim), F32)] * 3,
                 compiler_params=_params(("arbitrary",)))(*[t.reshape(r, cdim) for t in (w, g, m, v)])
    return [t.reshape(shape) for t in res]


def _ffn_half_fwd(tag, s, gains, mod, k, wts, n_lat, tm, h_tiles, tm_big):
    wg, wu, wd, idx = wts
    u = _adaln_fwd(f"adaln_{tag}", s, gains, k, mod, k, tm, h_tiles)
    a, b, hid = _ffn_up(f"ffn_up_{tag}", u, wg, wu, idx, tm)
    s_out, o = _ffn_down(f"ffn_down_{tag}", hid, wd, idx, s, mod, k, n_lat, tm_big)
    return s_out, (s, u, a, b, hid, o)


def _ffn_half_bwd(tag, ds_out, saved, gains, mod, k, wts, big_grads, tm, h_tiles, tm_big):
    wg, wu, wd, idx = wts
    g_gate, g_up, g_down = big_grads
    s, u, a, b, hid, o = saved
    d_o, acc_g = _resid_bwd(f"resid_bwd_{tag}", ds_out, o, mod, k, 0.5, tm, h_tiles)
    da, db = _ffn_dhid(f"ffn_dhid_{tag}", d_o, wd, idx, a, b, tm)
    du = _ffn_du(f"ffn_du_{tag}", da, db, wg, wu, idx, tm_big)
    d = u.shape[1]
    g_gate = _ffn_dw_in(f"ffn_dwg_{tag}", u, da, _pick(d, 256), g_gate, idx)
    g_up = _ffn_dw_in(f"ffn_dwu_{tag}", u, db, _pick(d, 256), g_up, idx)
    g_down = _ffn_dw_down(f"ffn_dwd_{tag}", hid, d_o, _pick(d, 512), g_down, idx)
    ds, acc_n = _adaln_bwd(f"adaln_bwd_{tag}", s, du, ds_out, gains, k, mod, k, tm, h_tiles)
    return ds, (g_gate, g_up, g_down), (acc_n[:, 0], acc_n[:, 1], acc_g[:, 0]), jnp.sum(acc_n[:, 2], axis=0)


def kernel(x, c, ctx, c_ctx, norm_g, w_mod, b_mod, ffn_w_gate, ffn_w_up, ffn_w_down, ab_w_in, pool_w, pool_scale, q_norm_g, w_uq, kv_norm_g, w_ukv, ab_w_out, conv_w_in, conv_w, conv_w_out, final_norm_g, loss_target, m_c_ctx, m_norm_g, m_w_mod, m_b_mod, m_ffn_w_gate, m_ffn_w_up, m_ffn_w_down, m_ab_w_in, m_pool_w, m_pool_scale, m_q_norm_g, m_w_uq, m_kv_norm_g, m_w_ukv, m_ab_w_out, m_conv_w_in, m_conv_w, m_conv_w_out, m_final_norm_g, v_c_ctx, v_norm_g, v_w_mod, v_b_mod, v_ffn_w_gate, v_ffn_w_up, v_ffn_w_down, v_ab_w_in, v_pool_w, v_pool_scale, v_q_norm_g, v_w_uq, v_kv_norm_g, v_w_ukv, v_ab_w_out, v_conv_w_in, v_conv_w, v_conv_w_out, v_final_norm_g):
    weights = dict(c_ctx=c_ctx, norm_g=norm_g, w_mod=w_mod, b_mod=b_mod, ffn_w_gate=ffn_w_gate, ffn_w_up=ffn_w_up,
                   ffn_w_down=ffn_w_down, ab_w_in=ab_w_in, pool_w=pool_w, pool_scale=pool_scale, q_norm_g=q_norm_g,
                   w_uq=w_uq, kv_norm_g=kv_norm_g, w_ukv=w_ukv, ab_w_out=ab_w_out, conv_w_in=conv_w_in, conv_w=conv_w,
                   conv_w_out=conv_w_out, final_norm_g=final_norm_g)
    mom_m = dict(c_ctx=m_c_ctx, norm_g=m_norm_g, w_mod=m_w_mod, b_mod=m_b_mod, ffn_w_gate=m_ffn_w_gate,
                 ffn_w_up=m_ffn_w_up, ffn_w_down=m_ffn_w_down, ab_w_in=m_ab_w_in, pool_w=m_pool_w,
                 pool_scale=m_pool_scale, q_norm_g=m_q_norm_g, w_uq=m_w_uq, kv_norm_g=m_kv_norm_g, w_ukv=m_w_ukv,
                 ab_w_out=m_ab_w_out, conv_w_in=m_conv_w_in, conv_w=m_conv_w, conv_w_out=m_conv_w_out,
                 final_norm_g=m_final_norm_g)
    mom_v = dict(c_ctx=v_c_ctx, norm_g=v_norm_g, w_mod=v_w_mod, b_mod=v_b_mod, ffn_w_gate=v_ffn_w_gate,
                 ffn_w_up=v_ffn_w_up, ffn_w_down=v_ffn_w_down, ab_w_in=v_ab_w_in, pool_w=v_pool_w,
                 pool_scale=v_pool_scale, q_norm_g=v_q_norm_g, w_uq=v_w_uq, kv_norm_g=v_kv_norm_g, w_ukv=v_w_ukv,
                 ab_w_out=v_ab_w_out, conv_w_in=v_conv_w_in, conv_w=v_conv_w, conv_w_out=v_conv_w_out,
                 final_norm_g=v_final_norm_g)

    t_len, d = x.shape[1], x.shape[2]
    g_len = ctx.shape[1]
    r_len = t_len + g_len
    fc = ffn_w_gate.shape[3]
    heads = d // 128
    pool_dim = d // 2
    q_rank, kv_rank = q_norm_g.shape[1], kv_norm_g.shape[1]
    hw = heads * HEAD_PAD
    attn_scale = 1.0 / math.sqrt(QK_NOPE + QK_ROPE)
    kvr_w = kv_rank + HEAD_PAD
    in_w = pool_dim + q_rank + kvr_w
    tm = 256 if g_len % 256 == 0 else g_len
    assert t_len % tm == 0 and g_len % tm == 0 and t_len % g_len == 0 and pool_dim % 128 == 0
    h_tiles = t_len // tm
    tm_l0 = _pick(r_len, 768, tm)
    tm_l1 = _pick(t_len, 1024, tm)

    xi, yi, ci = lax.axis_index("x"), lax.axis_index("y"), lax.axis_index("c")
    me = 4 * xi + 2 * yi + ci
    shard = 2 * xi + yi

    def halves(w):
        return w.astype(BF).reshape(2, -1, w.shape[-1])

    big_names = ["ffn_w_gate", "ffn_w_up", "ffn_w_down", "ab_w_out", "conv_w_out", "w_uq", "w_ukv", "ab_w_in",
                 "conv_w_in"]
    own_halves = [halves(weights[nm]) for nm in big_names]
    gather = _chips_start("gather_start", own_halves, True)

    small = jnp.concatenate([norm_g.reshape(6, -1), conv_w[0]], axis=0)
    small = jnp.pad(small, ((0, 7), (0, 0)))
    c_row = jnp.pad(c, ((0, 7), (0, 0))) + gather['token'][0, 0]
    small_all, c_all = _gather_all("gather_small", [small, c_row])
    small_full = small_all[::2].transpose(1, 0, 2).reshape(16, d)
    gains = [jnp.pad(small_full[3 * l:3 * l + 3], ((0, 5), (0, 0))) for l in range(2)]
    conv_w_full = small_full[6:9]
    c16 = jnp.concatenate([c_all[:, 0], c_ctx[None], jnp.zeros((7, d), F32)], axis=0)

    n_col = w_mod.shape[2]
    b_sh = lax.dynamic_slice_in_dim(b_mod, shard * n_col, n_col, axis=1)
    m_sh = [_mm(f"mod_fwd_{l}", c16, w_mod[l], 'nn', tm=16, tn=768, a_pre=_silu,
                epi=lambda acc, i, bv: (acc + bv,), epi_args=(b_sh[l:l + 1],), epi_kinds=('n',)) for l in range(2)]
    m_all = _gather_all("gather_mod", [jnp.concatenate(m_sh, axis=0)])[0]
    m_full = m_all[::2].reshape(N_SHARD, 2, 16, n_col).transpose(1, 2, 0, 3).reshape(2, 16, N_MOD * d)
    mod_h = [jnp.pad(lax.dynamic_index_in_dim(m_full[l], me, 0, keepdims=False).reshape(N_MOD, d), ((0, 7), (0, 0)))
             for l in range(2)]
    mod_g0 = jnp.pad(m_full[0, 8].reshape(N_MOD, d), ((0, 7), (0, 0)))
    mods = [jnp.stack([mod_h[0], mod_g0]), mod_h[1][None]]

    own_halves, landed_w = _chips_wait("gather_wait", gather, m_all)
    gathered = _forward_sibling("gather_forward", landed_w, own_halves)
    gw = {nm: g.reshape(N_SHARD, 2 * g.shape[1], g.shape[2]) for nm, g in zip(big_names, gathered)}
    wg_all = gw["ffn_w_gate"].reshape(N_SHARD, 4, d, fc)
    wu_all = gw["ffn_w_up"].reshape(N_SHARD, 4, d, fc)
    ffn_w = [[(wg_all, wu_all, gw["ffn_w_down"], 2 * l + f) for f in range(2)] for l in range(2)]
    w_out_full = gw["ab_w_out"].reshape(-1, d)
    cw_out_full = gw["conv_w_out"].reshape(-1, d)
    w_uq_full = gw["w_uq"].reshape(q_rank, heads * (QK_NOPE + QK_ROPE))
    w_ukv_full = gw["w_ukv"].transpose(1, 0, 2).reshape(kv_rank, heads * (QK_NOPE + V_HEAD))
    w_in_full = gw["ab_w_in"].transpose(1, 0, 2).reshape(d, -1)
    cw_in_full = gw["conv_w_in"].transpose(1, 0, 2).reshape(d, -1)

    wq_p = jnp.pad(w_uq_full.reshape(q_rank, heads, QK_NOPE + QK_ROPE),
                   ((0, 0), (0, 0), (0, HEAD_PAD - QK_NOPE - QK_ROPE))).reshape(q_rank, hw)
    ukv3 = w_ukv_full.reshape(kv_rank, heads, QK_NOPE + V_HEAD)
    wk_top = jnp.pad(ukv3[..., :QK_NOPE], ((0, 0), (0, 0), (0, HEAD_PAD - QK_NOPE))).reshape(kv_rank, hw)
    wv_top = jnp.pad(ukv3[..., QK_NOPE:], ((0, 0), (0, 0), (0, HEAD_PAD - V_HEAD))).reshape(kv_rank, hw)
    spread = jnp.zeros((HEAD_PAD, heads, HEAD_PAD), BF).at[
        jnp.arange(QK_ROPE)[:, None], jnp.arange(heads)[None, :], QK_NOPE + jnp.arange(QK_ROPE)[:, None]].set(1.0)
    wk_ext = jnp.concatenate([wk_top, spread.reshape(HEAD_PAD, hw)], axis=0)
    wv_ext = jnp.concatenate([wv_top, jnp.zeros((HEAD_PAD, hw), BF)], axis=0)
    w_in_pool = w_in_full[:, :pool_dim]
    w_in_q = w_in_full[:, pool_dim:pool_dim + q_rank]
    w_in_kvr = jnp.pad(w_in_full[:, pool_dim + q_rank:], ((0, 0), (0, HEAD_PAD - QK_ROPE)))
    w_out_attn = jnp.pad(w_out_full[pool_dim:].reshape(heads, V_HEAD, d),
                         ((0, 0), (0, HEAD_PAD - V_HEAD), (0, 0))).reshape(hw, d)
    w_out_p = jnp.concatenate([w_out_full[:pool_dim], w_out_attn], axis=0)

    s0 = jnp.concatenate([x[0], ctx[0]], axis=0)
    s1, sav_f00 = _ffn_half_fwd("l0a", s0, gains[0], mods[0], 0, ffn_w[0][0], t_len, tm, h_tiles, tm_l0)
    u_mix = _adaln_fwd("adaln_l0m", s1, gains[0], 1, mods[0], 1, tm, h_tiles)
    p_pool = _mm("in_pool", u_mix, w_in_pool, 'nn', tm=tm, tn=pool_dim)
    p_q = _mm("in_q", u_mix, w_in_q, 'nn', tm=tm, tn=q_rank)
    p_kvr = _mm("in_kvr", u_mix, w_in_kvr, 'nn', tm=tm, tn=kvr_w)
    qg = jnp.pad(q_norm_g, ((0, 7), (0, 0)))
    kvg = jnp.pad(kv_norm_g, ((0, 7), (0, 0)))
    tq_c, tq_s = _rope_tables(t_len, g_len, QK_NOPE)
    tk_c, tk_s = _rope_tables(t_len, g_len, 0)

    def qn_fn(rows, consts, m):
        n, _ = _rms(rows[0])
        return [n * consts[0][0:1, :]], {}

    qn = _rows_call("q_norm", qn_fn, r_len, tm, [p_q], [qg], None, [(q_rank, BF)])[0]
    q_r = _mm("q_up", qn, wq_p, 'nn', tm=tm, tn=hw, out_dtypes=(BF,),
              epi=lambda acc, i, ct, st: (_rope(acc, ct, st),), epi_args=(tq_c, tq_s), epi_kinds=('mt', 'mt'))

    def kvn_fn(rows, consts, m):
        pv, ct, st = rows
        n, _ = _rms(pv[:, :kv_rank])
        return [jnp.concatenate([n * consts[0][0:1, :], _rope(pv[:, kv_rank:], ct, st)], axis=1)], {}

    kvn = _rows_call("kv_norm", kvn_fn, r_len, tm, [p_kvr, tk_c, tk_s], [kvg], None, [(kvr_w, BF)])[0]
    k_p = _mm("k_up", kvn, wk_ext, 'nn', tm=tm, tn=hw, out_dtypes=(BF,))
    v_p = _mm("v_up", kvn, wv_ext, 'nn', tm=tm, tn=hw, out_dtypes=(BF,))
    o_h, lse_h = _attn_fwd("attn_h", q_r, k_p, v_p, t_len, 0, r_len, 0, heads, tm, attn_scale)
    o_g, lse_g = _attn_fwd("attn_g", q_r, k_p, v_p, g_len, t_len, g_len, t_len // g_len, heads, tm, attn_scale)
    y_h = _pool_fwd("pool_h", p_pool[:t_len], pool_w[0], pool_scale)
    y_g = _pool_fwd("pool_g", p_pool[t_len:], pool_w[0], pool_scale)
    cat = jnp.concatenate([jnp.concatenate([y_h, y_g], axis=0), jnp.concatenate([o_h, o_g], axis=0)], axis=1)

    def resid_epi(k3, n_lat, tmr):
        def epi(acc, i, sv, mv):
            return sv + _row_gate(mv, k3, i, tmr, n_lat) * acc, acc
        return epi

    s2, o_mix0 = _mm("mix_out_l0", cat, w_out_p, 'nn', tm=tm, tn=d, out_dtypes=(F32, F32),
                     epi=resid_epi(5, t_len, tm), epi_args=(s1, mods[0]), epi_kinds=('mn', 'w'))
    s3, sav_f01 = _ffn_half_fwd("l0b", s2, gains[0], mods[0], 2, ffn_w[0][1], t_len, tm, h_tiles, tm_l0)

    tml = 256 if t_len % 256 == 0 else tm
    h3 = s3[:t_len]
    h4, sav_f10 = _ffn_half_fwd("l1a", h3, gains[1], mods[1], 0, ffn_w[1][0], t_len, tml, None, tm_l1)
    u_cv = _adaln_fwd("adaln_l1m", h4, gains[1], 1, mods[1], 1, tml, None)
    p3 = _mm("conv_in", u_cv, cw_in_full, 'nn', tm=tml, tn=512)
    cwp = conv_w_full
    tc = _pick(d, 256)
    y_cv = _conv_fwd("conv_fwd", p3, cwp, tc)
    h5, o_mix1 = _mm("mix_out_l1", y_cv, cw_out_full, 'nn', tm=tml, tn=d, out_dtypes=(F32, F32),
                     epi=resid_epi(5, t_len, tml), epi_args=(h4, mods[1]), epi_kinds=('mn', 'w'))
    h6, sav_f11 = _ffn_half_fwd("l1b", h5, gains[1], mods[1], 2, ffn_w[1][1], t_len, tml, None, tm_l1)

    fg = jnp.pad(final_norm_g[None], ((0, 7), (0, 0)))
    dh6, acc_loss = _loss_head("loss_head", h6, loss_target[0], fg, tml)
    loss = lax.psum(acc_loss[0, 1, 0], ("x", "y", "c"))
    d_final_g = acc_loss[0, 0]

    dgain = [[None] * 3 for _ in range(2)]
    dmod = [[None] * N_MOD for _ in range(2)]

    def put(l, k, triple):
        dmod[l][3 * k], dmod[l][3 * k + 1], dmod[l][3 * k + 2] = triple

    ffn_g = (lax.empty((N_SHARD, 4 * d, fc), BF), lax.empty((N_SHARD, 4 * d, fc), BF),
             lax.empty((N_SHARD, 4 * fc, d), BF))
    dh5, ffn_g, tr, dgain[1][2] = _ffn_half_bwd("l1b", dh6, sav_f11, gains[1], mods[1], 2, ffn_w[1][1], ffn_g,
                                                tml, None, tm_l1)
    put(1, 2, tr)
    d_o1, acc_g1 = _resid_bwd("resid_bwd_l1m", dh5, o_mix1, mods[1], 1, 1.0, tml, None)
    dy_cv = _mm("mix_out_l1_dx", d_o1, cw_out_full, 'nt', tm=tml, tn=d)
    d_cw_out = _mm("mix_out_l1_dw", y_cv, d_o1, 'tn', tm=256, tn=512)
    dp3, d_cw = _conv_bwd("conv_bwd", p3, cwp, dy_cv, tc)
    du_cv = _conv_din("conv_in_dx", dp3, cw_in_full, tml)
    d_cw_in = _conv_dw_in("conv_in_dw", u_cv, dp3, _pick(d, 256), _pick(d, 512))
    dh4, acc_n1 = _adaln_bwd("adaln_bwd_l1m", h4, du_cv, dh5, gains[1], 1, mods[1], 1, tml, None)
    put(1, 1, (acc_n1[:, 0], acc_n1[:, 1], acc_g1[:, 0]))
    dgain[1][1] = acc_n1[0, 2]
    dh3, ffn_g, tr, dgain[1][0] = _ffn_half_bwd("l1a", dh4, sav_f10, gains[1], mods[1], 0, ffn_w[1][0], ffn_g,
                                                tml, None, tm_l1)
    put(1, 0, tr)

    ds3 = jnp.concatenate([dh3, jnp.zeros((g_len, d), F32)], axis=0)
    ds2, ffn_g, tr, dgain[0][2] = _ffn_half_bwd("l0b", ds3, sav_f01, gains[0], mods[0], 2, ffn_w[0][1], ffn_g,
                                                tm, h_tiles, tm_l0)
    put(0, 2, tr)
    d_o0, acc_g0 = _resid_bwd("resid_bwd_l0m", ds2, o_mix0, mods[0], 1, 1.0, tm, h_tiles)
    dcat = _mm("mix_out_l0_dx", d_o0, w_out_p, 'nt', tm=tm, tn=pool_dim + hw)
    d_w_out_p = _mm("mix_out_l0_dw", cat, d_o0, 'tn', tm=256, tn=512)
    col_blk = pool_dim // HEAD_PAD
    dq_h, dk_h, dv_h = _attn_bwd("attn_bwd_h", q_r, k_p, v_p, cat, dcat, lse_h, t_len, 0, r_len, 0, heads, tm,
                                 attn_scale, col_blk)
    dq_g, dk_g, dv_g = _attn_bwd("attn_bwd_g", q_r, k_p, v_p, cat, dcat, lse_g, g_len, t_len, g_len,
                                 t_len // g_len, heads, tm, attn_scale, col_blk)
    dq_all = jnp.concatenate([dq_h, dq_g], axis=0)
    dk_all = dk_h.at[t_len:].add(dk_g)
    dv_all = dv_h.at[t_len:].add(dv_g)
    dkvn = _mm("k_up_dx", dk_all, wk_ext, 'nt', tm=tm, tn=kvr_w)
    dkvn = _mm("v_up_dx", dv_all, wv_ext, 'nt', tm=tm, tn=kvr_w, epi=lambda acc, i, prev: (acc + prev,),
               epi_args=(dkvn,), epi_kinds=('mn',))
    d_wk_ext = _mm("k_up_dw", kvn, dk_all, 'tn', tm=kvr_w, tn=512)
    d_wv_ext = _mm("v_up_dw", kvn, dv_all, 'tn', tm=kvr_w, tn=512)

    def kvn_bwd_fn(rows, consts, m):
        pv, dv_, ct, st = rows
        g = consts[0][0:1, :]
        n, r = _rms(pv[:, :kv_rank])
        dyn = dv_[:, :kv_rank]
        dckv = _rms_bwd(dyn * g, n, r)
        dkr = _rope_t(dv_[:, kv_rank:], ct, st)
        return [jnp.concatenate([dckv, dkr], axis=1)], {0: jnp.sum(dyn * n, axis=0, keepdims=True)}

    dp_kvr, acc_kvg = _rows_call("kv_norm_bwd", kvn_bwd_fn, r_len, tm, [p_kvr, dkvn, tk_c, tk_s], [kvg], None,
                                 [(kvr_w, BF)], acc_w=kv_rank)

    def qrope_bwd_fn(rows, consts, m):
        return [_rope_t(rows[0], rows[1], rows[2])], {}

    dq_pad = _rows_call("q_rope_bwd", qrope_bwd_fn, r_len, tm, [dq_all, tq_c, tq_s], [], None, [(hw, BF)])[0]
    dqn = _mm("q_up_dx", dq_pad, wq_p, 'nt', tm=tm, tn=q_rank)
    d_wq_p = _mm("q_up_dw", qn, dq_pad, 'tn', tm=256, tn=512)

    def qn_bwd_fn(rows, consts, m):
        pv, dv_ = rows
        g = consts[0][0:1, :]
        n, r = _rms(pv)
        return [_rms_bwd(dv_ * g, n, r)], {0: jnp.sum(dv_ * n, axis=0, keepdims=True)}

    dp_q, acc_qg = _rows_call("q_norm_bwd", qn_bwd_fn, r_len, tm, [p_q, dqn], [qg], None, [(q_rank, BF)],
                              acc_w=q_rank)
    dpu_h, dpw_h, dps_h = _pool_bwd("pool_bwd_h", p_pool[:t_len], dcat, pool_w[0], pool_scale, 0)
    dpu_g, dpw_g, dps_g = _pool_bwd("pool_bwd_g", p_pool[t_len:], dcat, pool_w[0], pool_scale, t_len)
    dp_pool = jnp.concatenate([dpu_h, dpu_g], axis=0)
    add_prev = lambda acc, i, prev: (acc + prev,)
    du_mix = _mm("in_pool_dx", dp_pool, w_in_pool, 'nt', tm=tm, tn=d)
    du_mix = _mm("in_q_dx", dp_q, w_in_q, 'nt', tm=tm, tn=d, epi=add_prev, epi_args=(du_mix,), epi_kinds=('mn',))
    du_mix = _mm("in_kvr_dx", dp_kvr, w_in_kvr, 'nt', tm=tm, tn=d, epi=add_prev, epi_args=(du_mix,), epi_kinds=('mn',))
    d_w_in = jnp.concatenate([
        _mm("in_pool_dw", u_mix, dp_pool, 'tn', tm=256, tn=pool_dim),
        _mm("in_q_dw", u_mix, dp_q, 'tn', tm=256, tn=q_rank),
        _mm("in_kvr_dw", u_mix, dp_kvr, 'tn', tm=256, tn=kvr_w)[:, :kv_rank + QK_ROPE]], axis=1)
    ds1, acc_n0 = _adaln_bwd("adaln_bwd_l0m", s1, du_mix, ds2, gains[0], 1, mods[0], 1, tm, h_tiles)
    put(0, 1, (acc_n0[:, 0], acc_n0[:, 1], acc_g0[:, 0]))
    dgain[0][1] = jnp.sum(acc_n0[:, 2], axis=0)
    ds0, ffn_g, tr, dgain[0][0] = _ffn_half_bwd("l0a", ds1, sav_f00, gains[0], mods[0], 0, ffn_w[0][0], ffn_g,
                                                tm, h_tiles, tm_l0)
    put(0, 0, tr)
    grad_x = ds0[:t_len][None]

    d_w_uq = d_wq_p.reshape(q_rank, heads, HEAD_PAD)[..., :QK_NOPE + QK_ROPE].reshape(q_rank, -1)
    d_w_ukv = jnp.concatenate([d_wk_ext[:kv_rank].reshape(kv_rank, heads, HEAD_PAD)[..., :QK_NOPE],
                               d_wv_ext[:kv_rank].reshape(kv_rank, heads, HEAD_PAD)[..., :V_HEAD]],
                              axis=-1).reshape(kv_rank, -1)
    d_w_out = jnp.concatenate([d_w_out_p[:pool_dim],
                               d_w_out_p[pool_dim:].reshape(heads, HEAD_PAD, d)[:, :V_HEAD].reshape(-1, d)], axis=0)

    def by_shard_rows(g):
        return g.reshape(N_SHARD, -1, g.shape[-1])

    def by_shard_cols(g):
        return g.reshape(g.shape[0], N_SHARD, -1).transpose(1, 0, 2)

    big = dict(ffn_w_gate=ffn_g[0], ffn_w_up=ffn_g[1], ffn_w_down=ffn_g[2], ab_w_out=by_shard_rows(d_w_out),
               conv_w_out=by_shard_rows(d_cw_out), w_uq=by_shard_rows(d_w_uq), w_ukv=by_shard_cols(d_w_ukv),
               ab_w_in=by_shard_cols(d_w_in), conv_w_in=by_shard_cols(d_cw_in))
    send = [big[nm].astype(BF).reshape(N_DEV, big[nm].shape[1] // 2, big[nm].shape[2]) for nm in big_names]
    pre = _presum_sibling("grads_presum", send)
    scatter = _chips_start("grads_start", pre, False)

    dmh = jnp.stack([jnp.stack([dmod[l][k][0] for k in range(N_MOD)]) for l in range(2)])
    dmg0 = jnp.stack([dmod[0][k][1] for k in range(N_MOD)])
    dg_rows = jnp.stack([dgain[l][k] for l in range(2) for k in range(3)])
    pieces = [dmh.reshape(2 * N_MOD, d), dmg0, dg_rows, d_cw[:3], d_final_g[None],
              (dpw_h + dpw_g).reshape(-1, d), jnp.pad((dps_h + dps_g)[0], (0, d - pool_dim))[None],
              jnp.pad(acc_qg[0, 0], (0, d - q_rank))[None], jnp.pad(acc_kvg[0, 0], (0, d - kv_rank))[None]]
    n_piece = [p.shape[0] for p in pieces]
    pieces = [jnp.pad(p, ((0, (-p.shape[0]) % 8), (0, 0))) for p in pieces]
    small_g = jnp.concatenate(pieces, axis=0) + scatter['token'][0, 0]
    sg_all = _gather_all("gather_small_grads", [small_g])[0]
    sg_sum = _sum_lead("sum_small_grads", sg_all)
    offs = [0]
    for p in pieces:
        offs.append(offs[-1] + p.shape[0])
    part = lambda j: sg_sum[offs[j]:offs[j] + n_piece[j]]
    sum_dmh, sum_dmg0, g_norm_full, g_conv_w_full = part(0).reshape(2, N_MOD * d), part(1).reshape(N_MOD * d), part(2), part(3)
    g_final = part(4)[0]
    g_pool_w = part(5).reshape(pool_w.shape)
    g_pool_scale = part(6)[:, :pool_dim]
    g_q_norm = part(7)[:, :q_rank]
    g_kv_norm = part(8)[:, :kv_rank]
    col0 = shard * (d // N_SHARD)
    g_norm_g = lax.dynamic_slice_in_dim(g_norm_full.reshape(2, 3, d), col0, d // N_SHARD, axis=2)
    g_conv_w = lax.dynamic_slice_in_dim(g_conv_w_full, col0, d // N_SHARD, axis=1)[None]
    g_b_mod = _sum_lead("sum_b_mod", jnp.stack([sum_dmh, jnp.stack([sum_dmg0, jnp.zeros_like(sum_dmg0)])]))

    dm16 = []
    for l in range(2):
        per_dev = sg_all[:, l * N_MOD:(l + 1) * N_MOD].reshape(N_DEV, N_MOD * d)
        row8 = sum_dmg0 if l == 0 else jnp.zeros_like(sum_dmg0)
        full = jnp.concatenate([per_dev, row8[None], jnp.zeros((7, N_MOD * d), F32)], axis=0)
        dm16.append(lax.dynamic_slice_in_dim(full, shard * n_col, n_col, axis=1))
    g_w_mod = jnp.stack([_mm(f"mod_dw_{l}", c16, dm16[l], 'tn', tm=256, tn=768, a_pre=_silu) for l in range(2)])
    dc16 = _mm("mod_dx", dm16[0], w_mod[0], 'nt', tm=16, tn=512, epi=lambda acc, i, cv: (acc * _dsilu(cv),),
               epi_args=(c16,), epi_kinds=('mn',))
    dc_all = _gather_all("gather_dc", [dc16])[0]
    g_c_ctx = _sum_lead("sum_dc", dc_all[::2])[8]

    grads = dict(c_ctx=g_c_ctx, norm_g=g_norm_g, w_mod=g_w_mod, b_mod=g_b_mod, pool_w=g_pool_w,
                 pool_scale=g_pool_scale, q_norm_g=g_q_norm, kv_norm_g=g_kv_norm, conv_w=g_conv_w, final_norm_g=g_final)
    names = list(weights)
    upd = {n: _adamw(f"adamw_{n}", weights[n], grads[n].reshape(weights[n].shape), mom_m[n], mom_v[n])
           for n in names if n not in big_names}

    chip = 2 * xi + yi
    pre, landed = _chips_wait("grads_wait", scatter, upd["w_mod"][0])
    landed = [lax.dynamic_update_slice_in_dim(l, lax.dynamic_slice_in_dim(p, chip, 1, 0), chip, 0)
              for l, p in zip(landed, pre)]
    halves_sum = [_sum_lead(f"sum_grads_{nm}", l) for nm, l in zip(big_names, landed)]
    swapped = _swap_sibling("swap_halves", halves_sum)
    grads.update({nm: s.reshape(weights[nm].shape) for nm, s in zip(big_names, swapped)})
    upd.update({n: _adamw(f"adamw_{n}", weights[n], grads[n], mom_m[n], mom_v[n]) for n in big_names})
    return (loss, grad_x, *[grads[n].reshape(weights[n].shape) for n in names], *[upd[n][0] for n in names],
            *[upd[n][1] for n in names], *[upd[n][2] for n in names])
```

```python
import functools
import math

import jax
import jax.numpy as jnp
from jax import lax
from jax.experimental import pallas as pl
from jax.experimental.pallas import tpu as pltpu

F32 = jnp.float32
BF = jnp.bfloat16
MESH = pl.DeviceIdType.MESH

N_DEV = 8
N_SHARD = 4
RMS_EPS = 1e-6
N_MOD = 9
POOL_WINDOWS = (2, 4, 8, 16)
QK_NOPE = 64
QK_ROPE = 32
V_HEAD = 64
HEAD_PAD = 128
GRID_W = 64
ROPE_THETA = 10000.0
POOL_PAD = 16
ADAM_LR, ADAM_B1, ADAM_B2, ADAM_EPS, ADAM_WD, ADAM_STEP = 0.001, 0.9, 0.999, 1e-08, 0.01, 10
VMEM_LIMIT = 56 * 1024 * 1024


def _pcall(body, **kw):
    return pl.pallas_call(body, **kw)


def _params(sem=None):
    return pltpu.CompilerParams(dimension_semantics=sem, vmem_limit_bytes=VMEM_LIMIT)


def _pick(n, pref, mult=128):
    best = None
    d = mult
    while d <= min(n, pref):
        if n % d == 0:
            best = d
        d += mult
    return best if best is not None else n


def _silu(z):
    return z * jax.nn.sigmoid(z)


def _dsilu(z):
    s = jax.nn.sigmoid(z)
    return s * (1.0 + z * (1.0 - s))


def _dot(a, b, dims):
    return lax.dot_general(a.astype(BF), b.astype(BF), (dims, ((), ())), preferred_element_type=F32)


NN = ((1,), (0,))
NT = ((1,), (1,))
TN = ((0,), (0,))


ALL_FLIPS = [(kx, ky, kc) for kx in (0, 1) for ky in (0, 1) for kc in (0, 1) if (kx, ky, kc) != (0, 0, 0)]
CHIP_FLIPS = [(1, 0, 0), (0, 1, 0), (1, 1, 0)]
SIBLING = (0, 0, 1)
COMM_SPLIT = 8
SPLIT_MIN_ROWS = 256


def _exchange(name, arrays, plan, lead, whole_src, split=COMM_SPLIT):
    n = len(arrays)
    blk_shapes = [tuple(a.shape) if whole_src else tuple(a.shape[1:]) for a in arrays]
    splits = []
    for shp in blk_shapes:
        s = 1
        while s * 2 <= split and shp[0] % (s * 2) == 0 and (shp[0] // (s * 2)) % 16 == 0 \
                and shp[0] // (s * 2) >= SPLIT_MIN_ROWS:
            s *= 2
        splits.append(s)
    items = plan(0, 0, 0)
    n_items = len(items)
    remote_ids = [k for k, it in enumerate(items) if it[0] is not None]
    local_ids = [k for k, it in enumerate(items) if it[0] is None]
    slots = [(a, s) for s in range(max(splits)) for a in range(n) if s < splits[a]]
    n_slot = len(slots)

    def body(*refs):
        ins, outs = refs[:n], refs[n:2 * n]
        send_sems, recv_sems, loc_sems = refs[2 * n:]
        x, y, c = lax.axis_index("x"), lax.axis_index("y"), lax.axis_index("c")
        plan_here = plan(x, y, c)

        def rows(ref, a, s):
            rc = blk_shapes[a][0] // splits[a]
            return ref.at[pl.ds(s * rc, rc)]

        def make(si, k):
            a, s = slots[si]
            flip, src, dst, _ = plan_here[k]
            base = outs[a] if src[0] == 'out' else ins[a]
            src_ref = rows(base if src[1] is None else base.at[src[1]], a, s)
            dst_ref = rows(outs[a].at[dst], a, s)
            if flip is None:
                return pltpu.make_async_copy(src_ref, dst_ref, loc_sems.at[si * max(1, len(local_ids)) + local_ids.index(k)])
            peer = (1 - x if flip[0] else x, 1 - y if flip[1] else y, 1 - c if flip[2] else c)
            sem = si * len(remote_ids) + remote_ids.index(k)
            return pltpu.make_async_remote_copy(src_ref=src_ref, dst_ref=dst_ref, send_sem=send_sems.at[sem],
                                                recv_sem=recv_sems.at[sem], device_id=peer, device_id_type=MESH)

        copies = {}
        for si in range(n_slot):
            for k in range(n_items):
                if plan_here[k][3] is None:
                    copies[si, k] = make(si, k)
                    copies[si, k].start()
        arrived = set()
        for si in range(n_slot):
            for k in range(n_items):
                after = plan_here[k][3]
                if after is not None:
                    if (si, after) not in arrived:
                        copies[si, after].wait_recv()
                        arrived.add((si, after))
                    copies[si, k] = make(si, k)
                    copies[si, k].start()
        for (si, k), cp in copies.items():
            if plan_here[k][0] is None:
                cp.wait()
            else:
                cp.wait_send()
                if (si, k) not in arrived:
                    cp.wait_recv()

    any_spec = pl.BlockSpec(memory_space=pl.ANY)
    n_rem = max(1, n_slot * len(remote_ids))
    outs = _pcall(
        body, name=name,
        out_shape=[jax.ShapeDtypeStruct((lead,) + s, a.dtype) for s, a in zip(blk_shapes, arrays)],
        in_specs=[any_spec] * n, out_specs=[any_spec] * n,
        scratch_shapes=[pltpu.SemaphoreType.DMA((n_rem,)), pltpu.SemaphoreType.DMA((n_rem,)),
                        pltpu.SemaphoreType.DMA((max(1, n_slot * len(local_ids)),))],
    )(*arrays)
    return list(outs)


def _place(x, y, c):
    return 4 * x + 2 * y + c


def _flip(v, f):
    return 1 - v if f else v


def _gather_all(name, arrays):
    def plan(x, y, c):
        me = _place(x, y, c)
        return [(None, ('in', None), me, None)] + [(f, ('in', None), me, None) for f in ALL_FLIPS]
    return _exchange(name, arrays, plan, N_DEV, True)


STREAM_SLOTS = 4
STREAM_LAG = 2
STREAM_CHUNK_BYTES = 420 * 1024


def _stream_rows(rows, cdim, itemsize):
    best = None
    for rc in range(16, rows + 1, 16):
        if rows % rc == 0 and rc * cdim * itemsize <= STREAM_CHUNK_BYTES:
            best = rc
    return rows if best is None else best


def _stream_scratch(chunk_specs, combine):
    scratch = []
    for rc, cdim, dt in chunk_specs:
        scratch += [pltpu.VMEM((STREAM_SLOTS, rc, cdim), dt), pltpu.VMEM((STREAM_SLOTS, rc, cdim), dt)]
        if combine:
            scratch += [pltpu.VMEM((STREAM_SLOTS, rc, cdim), dt), pltpu.VMEM((STREAM_SLOTS, rc, cdim), BF)]
    n = len(chunk_specs)
    scratch += [pltpu.SemaphoreType.DMA((n * STREAM_SLOTS,)) for _ in range(5)]
    scratch.append(pltpu.SemaphoreType.REGULAR((n,)))
    return scratch


def _run_stream(peer, chunks, bufs, sems, a, combine):
    ns, lag, k_all = STREAM_SLOTS, STREAM_LAG, len(chunks)
    load_s, send_s, recv_s, store_s, own_s, credits = sems
    credit = credits.at[a]
    if combine:
        send_buf, recv_buf, own_buf, res_buf = bufs
    else:
        send_buf, recv_buf = bufs

    def sem(ref, i):
        return ref.at[a * ns + i % ns]

    def load(i):
        return pltpu.make_async_copy(chunks[i]['src'], send_buf.at[i % ns], sem(load_s, i))

    def own_load(i):
        return pltpu.make_async_copy(chunks[i]['own'], own_buf.at[i % ns], sem(own_s, i))

    def remote(i):
        return pltpu.make_async_remote_copy(src_ref=send_buf.at[i % ns], dst_ref=recv_buf.at[i % ns],
                                            send_sem=sem(send_s, i), recv_sem=sem(recv_s, i),
                                            device_id=peer, device_id_type=MESH)

    def store(i):
        return pltpu.make_async_copy((res_buf if combine else recv_buf).at[i % ns], chunks[i]['dst'], sem(store_s, i))

    def before(i):
        if chunks[i].get('pre') is not None:
            chunks[i]['pre']()

    before(0)
    load(0).start()
    if combine:
        own_load(0).start()
    for i in range(k_all + lag):
        if i < k_all:
            load(i).wait()
            if i >= ns:
                pl.semaphore_wait(credit, 1)
            remote(i).start()
            if i + 1 < k_all:
                if i + 1 >= ns:
                    remote(i + 1 - ns).wait_send()
                before(i + 1)
                load(i + 1).start()
        r = i - lag
        if r >= 0:
            remote(r).wait_recv()
            if r >= 1:
                store(r - 1).wait()
                if r - 1 + ns < k_all:
                    pl.semaphore_signal(credit, inc=1, device_id=peer, device_id_type=MESH)
            if combine:
                own_load(r).wait()
                res_buf[r % ns] = (recv_buf[r % ns].astype(F32) + own_buf[r % ns].astype(F32)).astype(BF)
                if r + 1 < k_all:
                    own_load(r + 1).start()
            store(r).start()
    store(k_all - 1).wait()
    for i in range(max(0, k_all - ns), k_all):
        remote(i).wait_send()


def _gather_halves(name, arrays):
    n = len(arrays)
    blk = [tuple(a.shape[1:]) for a in arrays]
    splits = []
    for shp in blk:
        s = 1
        while s * 2 <= COMM_SPLIT and shp[0] % (s * 2) == 0 and (shp[0] // (s * 2)) % 16 == 0 \
                and shp[0] // (s * 2) >= SPLIT_MIN_ROWS:
            s *= 2
        splits.append(s)
    rcs = [_stream_rows(shp[0] // s, shp[1], a.dtype.itemsize) for shp, s, a in zip(blk, splits, arrays)]
    slots = [(a, s) for a in range(n) for s in range(splits[a])]
    nch = len(CHIP_FLIPS)

    def body(*refs):
        ins, outs = refs[:n], refs[n:2 * n]
        bufs = refs[2 * n:4 * n]
        sems = refs[4 * n:4 * n + 6]
        ici_send, ici_recv = refs[4 * n + 6:]
        x, y, c = lax.axis_index("x"), lax.axis_index("y"), lax.axis_index("c")
        chip = 2 * x + y
        ici = {}
        for si, (a, s) in enumerate(slots):
            rows = pl.ds(s * (blk[a][0] // splits[a]), blk[a][0] // splits[a])
            for j, f in enumerate(CHIP_FLIPS):
                cp = pltpu.make_async_remote_copy(
                    src_ref=ins[a].at[c, rows], dst_ref=outs[a].at[2 * chip + c, rows],
                    send_sem=ici_send.at[si * nch + j], recv_sem=ici_recv.at[si * nch + j],
                    device_id=(_flip(x, f[0]), _flip(y, f[1]), c), device_id_type=MESH)
                cp.start()
                ici[a, s, j] = cp
        for a in range(n):
            chunks = []
            per = blk[a][0] // splits[a]
            for s in range(splits[a]):
                for j, f in enumerate(CHIP_FLIPS):
                    other = 2 * (2 * _flip(x, f[0]) + _flip(y, f[1]))
                    for k in range(per // rcs[a]):
                        rows = pl.ds(s * per + k * rcs[a], rcs[a])
                        chunks.append(dict(src=outs[a].at[other + c, rows], dst=outs[a].at[other + 1 - c, rows],
                                           pre=ici[a, s, j].wait_recv if k == 0 else None))
            _run_stream((x, y, 1 - c), chunks, bufs[2 * a:2 * a + 2], sems, a, False)
        for cp in ici.values():
            cp.wait_send()

    any_spec = pl.BlockSpec(memory_space=pl.ANY)
    scratch = _stream_scratch([(rc, shp[1], a.dtype) for rc, shp, a in zip(rcs, blk, arrays)], False)
    scratch += [pltpu.SemaphoreType.DMA((len(slots) * nch,)), pltpu.SemaphoreType.DMA((len(slots) * nch,))]
    outs = _pcall(body, name=name,
                  out_shape=[jax.ShapeDtypeStruct((N_DEV,) + shp, a.dtype) for shp, a in zip(blk, arrays)],
                  in_specs=[any_spec] * n, out_specs=[any_spec] * n, scratch_shapes=scratch,
                  compiler_params=_params())(*arrays)
    first = 2 * (2 * lax.axis_index("x") + lax.axis_index("y"))
    return [lax.dynamic_update_slice_in_dim(o, a, first, 0) for o, a in zip(outs, arrays)]


HBM_SPEC = pl.BlockSpec(memory_space=pltpu.HBM)
SEM_SPEC = pl.BlockSpec(memory_space=pltpu.SEMAPHORE)
SIDE_EFFECT = pltpu.SideEffectType.DATAFLOW_SIDE_EFFECTING


def _chip_copies(srcs, lands, send_sems, recv_sems, gather):
    x, y, c = lax.axis_index("x"), lax.axis_index("y"), lax.axis_index("c")
    chip = 2 * x + y
    copies = []
    for a in range(len(srcs)):
        for j, f in enumerate(CHIP_FLIPS):
            px, py = _flip(x, f[0]), _flip(y, f[1])
            k = a * len(CHIP_FLIPS) + j
            src = srcs[a].at[c] if gather else srcs[a].at[2 * px + py]
            dst = lands[a].at[2 * chip + c] if gather else lands[a].at[chip]
            copies.append(pltpu.make_async_remote_copy(src_ref=src, dst_ref=dst, send_sem=send_sems.at[k],
                                                       recv_sem=recv_sems.at[k], device_id=(px, py, c),
                                                       device_id_type=MESH))
    return copies


def _chips_start(name, groups, gather):
    arrays = [a for g in groups for a in g]
    n, ng = len(arrays), len(groups)
    lead = N_DEV if gather else N_SHARD
    lands = [pltpu.with_memory_space_constraint(lax.empty((lead,) + tuple(a.shape[1:]), a.dtype), pltpu.HBM)
             for a in arrays]
    offs = [sum(len(g) for g in groups[:gi]) for gi in range(ng + 1)]

    def body(*refs):
        srcs, lnd = refs[:n], refs[n:2 * n]
        for gi in range(ng):
            part = slice(offs[gi], offs[gi + 1])
            for cp in _chip_copies(srcs[part], lnd[part], refs[2 * n + 2 * gi], refs[2 * n + 2 * gi + 1], gather):
                cp.start()
        token = refs[-1]
        token[...] = jnp.zeros_like(token)

    sems = [pltpu.SemaphoreType.DMA((len(g) * len(CHIP_FLIPS),)) for g in groups for _ in range(2)]
    res = _pcall(
        body, name=name,
        out_shape=(*sems, *[pltpu.HBM(a.shape, a.dtype) for a in arrays], *[pltpu.HBM(l.shape, l.dtype) for l in lands],
                   jax.ShapeDtypeStruct((8, 128), F32)),
        in_specs=[HBM_SPEC] * (2 * n),
        out_specs=(*[SEM_SPEC] * (2 * ng), *[HBM_SPEC] * (2 * n), pl.BlockSpec(memory_space=pltpu.VMEM)),
        input_output_aliases={i: 2 * ng + i for i in range(2 * n)},
        compiler_params=pltpu.CompilerParams(has_side_effects=SIDE_EFFECT),
    )(*[pltpu.with_memory_space_constraint(a, pltpu.HBM) for a in arrays], *lands)
    srcs_thru, lands_thru = res[2 * ng:2 * ng + n], res[2 * ng + n:2 * ng + 2 * n]
    return [dict(send=res[2 * gi], recv=res[2 * gi + 1], srcs=list(srcs_thru[offs[gi]:offs[gi + 1]]),
                 lands=list(lands_thru[offs[gi]:offs[gi + 1]]), token=res[-1], gather=gather) for gi in range(ng)]


def _chips_wait(name, handle, after):
    n = len(handle['srcs'])
    gather = handle['gather']

    def body(*refs):
        srcs, lnd = refs[:n], refs[n:2 * n]
        for cp in _chip_copies(srcs, lnd, refs[2 * n], refs[2 * n + 1], gather):
            cp.wait_send()
            cp.wait_recv()

    both = handle['srcs'] + handle['lands']
    res = _pcall(
        body, name=name, out_shape=tuple(pltpu.HBM(b.shape, b.dtype) for b in both),
        in_specs=[HBM_SPEC] * (2 * n) + [SEM_SPEC, SEM_SPEC, pl.BlockSpec(memory_space=pl.ANY)],
        out_specs=tuple([HBM_SPEC] * (2 * n)), input_output_aliases={i: i for i in range(2 * n)},
        compiler_params=pltpu.CompilerParams(has_side_effects=SIDE_EFFECT),
    )(*both, handle['send'], handle['recv'], after)
    return list(res[:n]), list(res[n:])


def _forward_sibling(name, landed, own):
    n = len(landed)
    blk = [tuple(a.shape[1:]) for a in landed]
    rcs = [_stream_rows(shp[0], shp[1], a.dtype.itemsize) for shp, a in zip(blk, landed)]

    def body(*refs):
        ins, outs = refs[:n], refs[n:2 * n]
        bufs = refs[2 * n:4 * n]
        sems = refs[4 * n:4 * n + 6]
        x, y, c = lax.axis_index("x"), lax.axis_index("y"), lax.axis_index("c")
        for a in range(n):
            chunks = []
            for f in CHIP_FLIPS:
                other = 2 * (2 * _flip(x, f[0]) + _flip(y, f[1]))
                for k in range(blk[a][0] // rcs[a]):
                    rows = pl.ds(k * rcs[a], rcs[a])
                    chunks.append(dict(src=ins[a].at[other + c, rows], dst=outs[a].at[other + 1 - c, rows]))
            _run_stream((x, y, 1 - c), chunks, bufs[2 * a:2 * a + 2], sems, a, False)

    any_spec = pl.BlockSpec(memory_space=pl.ANY)
    scratch = _stream_scratch([(rc, shp[1], a.dtype) for rc, shp, a in zip(rcs, blk, landed)], False)
    outs = _pcall(body, name=name, out_shape=[jax.ShapeDtypeStruct(a.shape, a.dtype) for a in landed],
                  in_specs=[any_spec] * n, out_specs=[any_spec] * n, scratch_shapes=scratch,
                  input_output_aliases={a: a for a in range(n)}, compiler_params=_params())(*landed)
    first = 2 * (2 * lax.axis_index("x") + lax.axis_index("y"))
    return [lax.dynamic_update_slice_in_dim(o, a, first, 0) for o, a in zip(outs, own)]


def _presum_sibling(name, arrays):
    n = len(arrays)
    blk = [tuple(a.shape[1:]) for a in arrays]
    rcs = [_stream_rows(shp[0], shp[1], a.dtype.itemsize) for shp, a in zip(blk, arrays)]

    def body(*refs):
        ins, outs = refs[:n], refs[n:2 * n]
        bufs = refs[2 * n:6 * n]
        sems = refs[6 * n:6 * n + 6]
        x, y, c = lax.axis_index("x"), lax.axis_index("y"), lax.axis_index("c")
        for a in range(n):
            chunks = []
            for sh in range(N_SHARD):
                for k in range(blk[a][0] // rcs[a]):
                    rows = pl.ds(k * rcs[a], rcs[a])
                    chunks.append(dict(src=ins[a].at[2 * sh + 1 - c, rows], own=ins[a].at[2 * sh + c, rows],
                                       dst=outs[a].at[sh, rows]))
            _run_stream((x, y, 1 - c), chunks, bufs[4 * a:4 * a + 4], sems, a, True)

    any_spec = pl.BlockSpec(memory_space=pl.ANY)
    scratch = _stream_scratch([(rc, shp[1], a.dtype) for rc, shp, a in zip(rcs, blk, arrays)], True)
    outs = _pcall(body, name=name,
                  out_shape=[jax.ShapeDtypeStruct((N_SHARD,) + shp, BF) for shp in blk],
                  in_specs=[any_spec] * n, out_specs=[any_spec] * n, scratch_shapes=scratch,
                  compiler_params=_params())(*arrays)
    return list(outs)


def _to_chips(name, arrays):
    def plan(x, y, c):
        return [(f, ('in', 2 * _flip(x, f[0]) + _flip(y, f[1])), 2 * x + y, None) for f in CHIP_FLIPS]
    outs = _exchange(name, arrays, plan, N_SHARD, False)
    chip = 2 * lax.axis_index("x") + lax.axis_index("y")
    return [lax.dynamic_update_slice_in_dim(o, lax.dynamic_slice_in_dim(a, chip, 1, 0), chip, 0)
            for o, a in zip(outs, arrays)]


def _swap_sibling(name, arrays):
    n = len(arrays)
    rcs = [_stream_rows(a.shape[0], a.shape[1], a.dtype.itemsize) for a in arrays]

    def body(*refs):
        ins, outs = refs[:n], refs[n:2 * n]
        bufs = refs[2 * n:4 * n]
        sems = refs[4 * n:4 * n + 6]
        x, y, c = lax.axis_index("x"), lax.axis_index("y"), lax.axis_index("c")
        for a in range(n):
            chunks = [dict(src=ins[a].at[pl.ds(k * rcs[a], rcs[a])], dst=outs[a].at[pl.ds(k * rcs[a], rcs[a])])
                      for k in range(arrays[a].shape[0] // rcs[a])]
            _run_stream((x, y, 1 - c), chunks, bufs[2 * a:2 * a + 2], sems, a, False)

    any_spec = pl.BlockSpec(memory_space=pl.ANY)
    scratch = _stream_scratch([(rc, a.shape[1], a.dtype) for rc, a in zip(rcs, arrays)], False)
    got = _pcall(body, name=name, out_shape=[jax.ShapeDtypeStruct(a.shape, a.dtype) for a in arrays],
                 in_specs=[any_spec] * n, out_specs=[any_spec] * n, scratch_shapes=scratch,
                 compiler_params=_params())(*arrays)
    south = lax.axis_index("c") == 0
    return [jnp.where(south, jnp.stack([a, g]), jnp.stack([g, a])) for a, g in zip(arrays, got)]


def _sum_lead(name, arr, out_dtype=F32):
    n, r, cdim = arr.shape
    tr = r
    limit = (4 << 20) // (n * cdim * arr.dtype.itemsize)
    if r > limit:
        tr = _pick(r, max(limit, 16), 16)

    def body(x_ref, o_ref):
        acc = x_ref[0].astype(F32)
        for d in range(1, n):
            acc = acc + x_ref[d].astype(F32)
        o_ref[...] = acc.astype(out_dtype)

    return _pcall(body, name=name, grid=(r // tr,),
                  in_specs=[pl.BlockSpec((n, tr, cdim), lambda i: (0, i, 0))],
                  out_specs=pl.BlockSpec((tr, cdim), lambda i: (i, 0)),
                  out_shape=jax.ShapeDtypeStruct((r, cdim), out_dtype),
                  compiler_params=_params(("arbitrary",)))(arr)


def _rows_call(name, fn, n_rows, tm, rows, consts, mod, outs, acc_w=None, h_tiles=None):
    nt = n_rows // tm
    ht = nt if h_tiles is None else h_tiles
    ng = 1 if mod is None else mod.shape[0]
    n_r, n_c, n_o = len(rows), len(consts), len(outs)
    has_mod = mod is not None

    def body(*refs):
        i = pl.program_id(0)
        first = (i % ht) == 0
        row_refs, const_refs = refs[:n_r], refs[n_r:n_r + n_c]
        p = n_r + n_c
        mod_tile = refs[p][...] if has_mod else None
        p += int(has_mod)
        out_refs = refs[p:p + n_o]
        o, acc = fn([r[...] for r in row_refs], [r[...] for r in const_refs], mod_tile)
        for r, v in zip(out_refs, o):
            r[...] = v.astype(r.dtype)
        if acc_w is not None:
            acc_ref = refs[p + n_o]

            @pl.when(first)
            def _():
                acc_ref[...] = jnp.zeros_like(acc_ref)

            for k, v in acc.items():
                acc_ref[k:k + 1, :] += v

    in_specs = [pl.BlockSpec((tm, r.shape[1]), lambda i: (i, 0)) for r in rows]
    in_specs += [pl.BlockSpec(cst.shape, lambda i, nd=cst.ndim: (0,) * nd) for cst in consts]
    args = list(rows) + list(consts)
    if has_mod:
        in_specs.append(pl.BlockSpec((None,) + mod.shape[1:], lambda i: (i // ht, 0, 0)))
        args.append(mod)
    out_shape = [jax.ShapeDtypeStruct((n_rows, w), dt) for w, dt in outs]
    out_specs = [pl.BlockSpec((tm, w), lambda i: (i, 0)) for w, _ in outs]
    if acc_w is not None:
        out_shape.append(jax.ShapeDtypeStruct((ng, 8, acc_w), F32))
        out_specs.append(pl.BlockSpec((None, 8, acc_w), lambda i: (i // ht, 0, 0)))
    res = _pcall(body, name=name, grid=(nt,), in_specs=in_specs, out_specs=out_specs, out_shape=out_shape,
                 compiler_params=_params(("arbitrary",)))(*args)
    return list(res)


def _rms(s):
    r = lax.rsqrt(jnp.mean(s * s, axis=1, keepdims=True) + RMS_EPS)
    return s * r, r


def _rms_bwd(dn, n, r):
    return r * (dn - n * jnp.mean(dn * n, axis=1, keepdims=True))


def _adaln_fwd(name, s, gains, gain_row, mod, k, tm, h_tiles):
    def fn(rows, consts, m):
        n, _ = _rms(rows[0])
        y = n * consts[0][gain_row:gain_row + 1, :]
        return [y * (1.0 + m[3 * k + 1:3 * k + 2, :]) + m[3 * k:3 * k + 1, :]], {}

    d = s.shape[1]
    return _rows_call(name, fn, s.shape[0], tm, [s], [gains], mod, [(d, BF)], h_tiles=h_tiles)[0]


def _adaln_bwd(name, s, du, ds_res, gains, gain_row, mod, k, tm, h_tiles):
    def fn(rows, consts, m):
        sv, duv, res = rows
        gain = consts[0][gain_row:gain_row + 1, :]
        n, r = _rms(sv)
        y = n * gain
        dy = duv * (1.0 + m[3 * k + 1:3 * k + 2, :])
        acc = {0: jnp.sum(duv, axis=0, keepdims=True), 1: jnp.sum(duv * y, axis=0, keepdims=True),
               2: jnp.sum(dy * n, axis=0, keepdims=True)}
        return [_rms_bwd(dy * gain, n, r) + res], acc

    d = s.shape[1]
    return _rows_call(name, fn, s.shape[0], tm, [s, du, ds_res], [gains], mod, [(d, F32)], acc_w=d, h_tiles=h_tiles)


def _resid_bwd(name, ds_out, o, mod, k, cst, tm, h_tiles):
    def fn(rows, consts, m):
        dsv, ov = rows
        gate = m[3 * k + 2:3 * k + 3, :]
        return [cst * gate * dsv], {0: jnp.sum(cst * ov * dsv, axis=0, keepdims=True)}

    d = o.shape[1]
    return _rows_call(name, fn, o.shape[0], tm, [ds_out, o], [], mod, [(d, BF)], acc_w=d, h_tiles=h_tiles)


def _mm(name, a, b, mode, tm=256, tn=512, out_dtypes=(F32,), epi=None, epi_args=(), epi_kinds=(), a_pre=None):
    if mode == 'nn':
        (m, kd), nd = a.shape, b.shape[1]
    elif mode == 'nt':
        (m, kd), nd = a.shape, b.shape[0]
    else:
        (kd, m), nd = a.shape, b.shape[1]
    tm = _pick(m, tm, 16) if m % tm else tm
    tn = _pick(nd, tn, 128) if nd % tn else tn
    dims = {'nn': NN, 'nt': NT, 'tn': TN}[mode]
    n_e, n_o = len(epi_args), len(out_dtypes)

    def body(*refs):
        i = pl.program_id(1)
        av = refs[0][...]
        if a_pre is not None:
            av = a_pre(av)
        acc = _dot(av, refs[1][...], dims)
        res = (acc,) if epi is None else epi(acc, i, *[r[...] for r in refs[2:2 + n_e]])
        for r, v in zip(refs[2 + n_e:], res):
            r[...] = v.astype(r.dtype)

    if mode == 'nn':
        specs = [pl.BlockSpec((tm, kd), lambda j, i: (i, 0)), pl.BlockSpec((kd, tn), lambda j, i: (0, j))]
    elif mode == 'nt':
        specs = [pl.BlockSpec((tm, kd), lambda j, i: (i, 0)), pl.BlockSpec((tn, kd), lambda j, i: (j, 0))]
    else:
        specs = [pl.BlockSpec((kd, tm), lambda j, i: (0, i)), pl.BlockSpec((kd, tn), lambda j, i: (0, j))]
    for arr, kind in zip(epi_args, epi_kinds):
        if kind == 'mn':
            specs.append(pl.BlockSpec((tm, tn), lambda j, i: (i, j)))
        elif kind == 'n':
            specs.append(pl.BlockSpec((1, tn), lambda j, i: (0, j)))
        elif kind == 'mt':
            specs.append(pl.BlockSpec((tm, arr.shape[1]), lambda j, i: (i, 0)))
        else:
            specs.append(pl.BlockSpec(arr.shape, lambda j, i, nd_=arr.ndim: (0,) * nd_))
    res = _pcall(body, name=name, grid=(nd // tn, m // tm), in_specs=specs,
                 out_specs=[pl.BlockSpec((tm, tn), lambda j, i: (i, j))] * n_o,
                 out_shape=[jax.ShapeDtypeStruct((m, nd), dt) for dt in out_dtypes],
                 compiler_params=_params(("arbitrary", "arbitrary")))(a, b, *epi_args)
    return res[0] if n_o == 1 else list(res)


def _row_gate(mod, k3, i, tm, n_lat):
    g0 = mod[0, k3:k3 + 1, :]
    if mod.shape[0] == 1:
        return g0
    rid = i * tm + lax.broadcasted_iota(jnp.int32, (tm, 1), 0)
    return jnp.where(rid < n_lat, g0, mod[1, k3:k3 + 1, :])


def _ffn_up(name, u, wg, wu, base, tm):
    r, d = u.shape
    nch, _, _, fc = wg.shape

    def body(u_ref, wg_ref, wu_ref, a_ref, b_ref, h_ref):
        uv = u_ref[...]
        a = _dot(uv, wg_ref[...], NN)
        b = _dot(uv, wu_ref[...], NN)
        a_ref[...] = a.astype(BF)
        b_ref[...] = b.astype(BF)
        h_ref[...] = (_silu(a) * b).astype(BF)

    chunk = pl.BlockSpec((None, tm, fc), lambda j, i: (j, i, 0))
    return _pcall(body, name=name, grid=(nch, r // tm),
                  in_specs=[pl.BlockSpec((tm, d), lambda j, i: (i, 0)),
                            pl.BlockSpec((None, None, d, fc), lambda j, i: (j, base, 0, 0)),
                            pl.BlockSpec((None, None, d, fc), lambda j, i: (j, base, 0, 0))],
                  out_specs=[chunk] * 3, out_shape=[jax.ShapeDtypeStruct((nch, r, fc), BF)] * 3,
                  compiler_params=_params(("arbitrary", "arbitrary")))(u, wg, wu)


def _ffn_down(name, hid, wd, wd_blk, s, mod, k, n_lat, tm):
    nch, r, fc = hid.shape
    d = wd.shape[2]

    def body(h_ref, w_ref, s_ref, m_ref, so_ref, o_ref, acc_ref):
        i, j = pl.program_id(0), pl.program_id(1)
        part = _dot(h_ref[...], w_ref[...], NN)

        @pl.when(j == 0)
        def _():
            acc_ref[...] = part

        @pl.when(j > 0)
        def _():
            acc_ref[...] += part

        @pl.when(j == nch - 1)
        def _():
            o = acc_ref[...]
            o_ref[...] = o
            so_ref[...] = s_ref[...] + 0.5 * _row_gate(m_ref[...], 3 * k + 2, i, tm, n_lat) * o

    row = pl.BlockSpec((tm, d), lambda i, j: (i, 0))
    return _pcall(body, name=name, grid=(r // tm, nch),
                  in_specs=[pl.BlockSpec((None, tm, fc), lambda i, j: (j, i, 0)),
                            pl.BlockSpec((None, fc, d), lambda i, j: (j, wd_blk, 0)), row,
                            pl.BlockSpec(mod.shape, lambda i, j: (0, 0, 0))],
                  out_specs=[row, row], out_shape=[jax.ShapeDtypeStruct((r, d), F32)] * 2,
                  scratch_shapes=[pltpu.VMEM((tm, d), F32)],
                  compiler_params=_params(("arbitrary", "arbitrary")))(hid, wd, s, mod)


def _ffn_dhid(name, d_o, wd, wd_blk, a, b, tm):
    r, d = d_o.shape
    nch, _, fc = a.shape

    def body(g_ref, w_ref, a_ref, b_ref, da_ref, db_ref):
        dh = _dot(g_ref[...], w_ref[...], NT)
        av, bv = a_ref[...].astype(F32), b_ref[...].astype(F32)
        da_ref[...] = (dh * bv * _dsilu(av)).astype(BF)
        db_ref[...] = (dh * _silu(av)).astype(BF)

    chunk = pl.BlockSpec((None, tm, fc), lambda j, i: (j, i, 0))
    return _pcall(body, name=name, grid=(nch, r // tm),
                  in_specs=[pl.BlockSpec((tm, d), lambda j, i: (i, 0)),
                            pl.BlockSpec((None, fc, d), lambda j, i: (j, wd_blk, 0)), chunk, chunk],
                  out_specs=[chunk] * 2, out_shape=[jax.ShapeDtypeStruct((nch, r, fc), BF)] * 2,
                  compiler_params=_params(("arbitrary", "arbitrary")))(d_o, wd, a, b)


def _ffn_du(name, da, db, wg, wu, base, tm):
    nch, r, fc = da.shape
    d = wg.shape[2]

    def body(da_ref, db_ref, wg_ref, wu_ref, o_ref, acc_ref):
        j = pl.program_id(1)
        part = _dot(da_ref[...], wg_ref[...], NT) + _dot(db_ref[...], wu_ref[...], NT)

        @pl.when(j == 0)
        def _():
            acc_ref[...] = part

        @pl.when(j > 0)
        def _():
            acc_ref[...] += part

        @pl.when(j == nch - 1)
        def _():
            o_ref[...] = acc_ref[...]

    chunk = pl.BlockSpec((None, tm, fc), lambda i, j: (j, i, 0))
    return _pcall(body, name=name, grid=(r // tm, nch),
                  in_specs=[chunk, chunk, pl.BlockSpec((None, None, d, fc), lambda i, j: (j, base, 0, 0)),
                            pl.BlockSpec((None, None, d, fc), lambda i, j: (j, base, 0, 0))],
                  out_specs=pl.BlockSpec((tm, d), lambda i, j: (i, 0)),
                  out_shape=jax.ShapeDtypeStruct((r, d), F32), scratch_shapes=[pltpu.VMEM((tm, d), F32)],
                  compiler_params=_params(("arbitrary", "arbitrary")))(da, db, wg, wu)


def _ffn_dw_in(name, u, dz, tmm, grads, idx):
    r, d = u.shape
    nch, _, fc = dz.shape
    nb = d // tmm

    def body(u_ref, z_ref, g_ref, o_ref):
        o_ref[...] = _dot(u_ref[...], z_ref[...], TN).astype(o_ref.dtype)

    return _pcall(body, name=name, grid=(nch, nb),
                  in_specs=[pl.BlockSpec((r, tmm), lambda j, mi: (0, mi)),
                            pl.BlockSpec((None, r, fc), lambda j, mi: (j, 0, 0)),
                            pl.BlockSpec(memory_space=pl.ANY)],
                  out_specs=pl.BlockSpec((None, tmm, fc), lambda j, mi: (j, idx * nb + mi, 0)),
                  out_shape=jax.ShapeDtypeStruct(grads.shape, grads.dtype), input_output_aliases={2: 0},
                  compiler_params=_params(("arbitrary", "arbitrary")))(u, dz, grads)


def _ffn_dw_down(name, hid, d_o, tn, grads, idx):
    nch, r, fc = hid.shape
    d = d_o.shape[1]

    def body(h_ref, g_ref, acc_ref, o_ref):
        o_ref[...] = _dot(h_ref[...], g_ref[...], TN).astype(o_ref.dtype)

    return _pcall(body, name=name, grid=(nch, d // tn),
                  in_specs=[pl.BlockSpec((None, r, fc), lambda j, ni: (j, 0, 0)),
                            pl.BlockSpec((r, tn), lambda j, ni: (0, ni)),
                            pl.BlockSpec(memory_space=pl.ANY)],
                  out_specs=pl.BlockSpec((None, fc, tn), lambda j, ni: (j, idx, ni)),
                  out_shape=jax.ShapeDtypeStruct(grads.shape, grads.dtype), input_output_aliases={2: 0},
                  compiler_params=_params(("arbitrary", "arbitrary")))(hid, d_o, grads)


def _partner(x):
    n = x.shape[1]
    lane = lax.broadcasted_iota(jnp.int32, x.shape, 1)
    return jnp.where((lane & 15) < 8, pltpu.roll(x, n - 8, 1), pltpu.roll(x, 8, 1))


def _rope(x, ct, st):
    reps = x.shape[1] // ct.shape[1]
    if reps > 1:
        ct, st = jnp.tile(ct, (1, reps)), jnp.tile(st, (1, reps))
    return x * ct + _partner(x) * st


def _rope_t(dy, ct, st):
    reps = dy.shape[1] // ct.shape[1]
    if reps > 1:
        ct, st = jnp.tile(ct, (1, reps)), jnp.tile(st, (1, reps))
    return dy * ct + _partner(dy * st)


def _rope_tables(t_len, g_len, lane0):
    half = QK_ROPE // 4
    pos = jnp.arange(t_len)
    row = (pos // GRID_W).astype(F32)
    col = (pos % GRID_W).astype(F32)
    freqs = jnp.power(ROPE_THETA, -jnp.arange(0, QK_ROPE // 2, 2, dtype=F32) / (QK_ROPE // 2))
    ang_r, ang_c = row[:, None] * freqs, col[:, None] * freqs
    cs = jnp.concatenate([jnp.cos(ang_r)] * 2 + [jnp.cos(ang_c)] * 2, axis=1)
    sn = jnp.concatenate([-jnp.sin(ang_r), jnp.sin(ang_r), -jnp.sin(ang_c), jnp.sin(ang_c)], axis=1)
    assert cs.shape[1] == 4 * half == QK_ROPE
    ct = jnp.ones((t_len + g_len, HEAD_PAD), F32).at[:t_len, lane0:lane0 + QK_ROPE].set(cs)
    st = jnp.zeros((t_len + g_len, HEAD_PAD), F32).at[:t_len, lane0:lane0 + QK_ROPE].set(sn)
    return ct, st


def _attn_fwd(name, q, kp, vp, n_q, q_off, n_k, k_blk, heads, tq, scale):
    qb = q_off // tq

    def body(q_ref, k_ref, v_ref, o_ref, l_ref):
        s = _dot(q_ref[...], k_ref[...], NT) * scale
        m = jnp.max(s, axis=1, keepdims=True)
        p = jnp.exp(s - m)
        l = jnp.sum(p, axis=1, keepdims=True)
        o_ref[...] = (_dot(p, v_ref[...], NN) / l).astype(BF)
        l_ref[...] = jnp.broadcast_to(m + jnp.log(l), l_ref.shape)

    hw = heads * HEAD_PAD
    blk = pl.BlockSpec((tq, HEAD_PAD), lambda h, i: (i, h))
    kv = pl.BlockSpec((n_k, HEAD_PAD), lambda h, i: (k_blk, h))
    return _pcall(body, name=name, grid=(heads, n_q // tq),
                  in_specs=[pl.BlockSpec((tq, HEAD_PAD), lambda h, i: (i + qb, h)), kv, kv],
                  out_specs=[blk, blk],
                  out_shape=[jax.ShapeDtypeStruct((n_q, hw), BF), jax.ShapeDtypeStruct((n_q, hw), F32)],
                  compiler_params=_params(("arbitrary", "arbitrary")))(q, kp, vp)


def _attn_bwd(name, q, kp, vp, cat, dcat, lse, n_q, q_off, n_k, k_blk, heads, tq, scale, col_blk):
    qb = q_off // tq

    def body(q_ref, k_ref, v_ref, o_ref, do_ref, l_ref, dq_ref, dk_ref, dv_ref):
        i = pl.program_id(1)
        qv, kv_, vv = q_ref[...], k_ref[...], v_ref[...]
        dov = do_ref[...]
        s = _dot(qv, kv_, NT) * scale
        p = jnp.exp(s - l_ref[...][:, 0:1])
        dp = _dot(dov, vv, NT)
        delta = jnp.sum(dov * o_ref[...].astype(F32), axis=1, keepdims=True)
        ds = (p * (dp - delta) * scale).astype(BF)
        dq_ref[...] = _dot(ds, kv_, NN)
        dk = _dot(ds, qv, TN)
        dv = _dot(p, dov, TN)

        @pl.when(i == 0)
        def _():
            dk_ref[...] = dk
            dv_ref[...] = dv

        @pl.when(i > 0)
        def _():
            dk_ref[...] += dk
            dv_ref[...] += dv

    hw = heads * HEAD_PAD
    qspec = pl.BlockSpec((tq, HEAD_PAD), lambda h, i: (i + qb, h))
    cspec = pl.BlockSpec((tq, HEAD_PAD), lambda h, i: (i + qb, col_blk + h))
    kv = pl.BlockSpec((n_k, HEAD_PAD), lambda h, i: (k_blk, h))
    acc = pl.BlockSpec((n_k, HEAD_PAD), lambda h, i: (0, h))
    blk = pl.BlockSpec((tq, HEAD_PAD), lambda h, i: (i, h))
    return _pcall(body, name=name, grid=(heads, n_q // tq),
                  in_specs=[qspec, kv, kv, cspec, cspec, blk], out_specs=[blk, acc, acc],
                  out_shape=[jax.ShapeDtypeStruct((n_q, hw), F32), jax.ShapeDtypeStruct((n_k, hw), F32),
                             jax.ShapeDtypeStruct((n_k, hw), F32)],
                  compiler_params=_params(("arbitrary", "arbitrary")))(q, kp, vp, cat, dcat, lse)


def _shift(x, k):
    return pltpu.roll(x, k % x.shape[0], 0)


def _window_sum(v, w, mirrored):
    n, gd = v.shape
    pad = jnp.zeros((POOL_PAD, gd), F32)
    e = jnp.concatenate([pad, v, pad], axis=0)
    acc = e + _shift(e, -1 if mirrored else 1)
    step = 1
    while 2 * step < w:
        acc = _shift(acc, step) + _shift(acc, -step)
        step *= 2
    return acc[POOL_PAD:POOL_PAD + n]


def _window_count(n, w):
    t = lax.broadcasted_iota(jnp.int32, (n, 1), 0)
    lo = jnp.maximum(t - w // 2, 0)
    hi = jnp.minimum(t + (w - w // 2 - 1), n - 1)
    return (hi - lo + 1).astype(F32)


def _pool_fwd(name, u, pool_w, scale):
    n, pd = u.shape
    ng = len(POOL_WINDOWS)
    gd = pd // ng

    def body(u_ref, w_ref, s_ref, y_ref):
        for g, w in enumerate(POOL_WINDOWS):
            sl = slice(g * gd, (g + 1) * gd)
            ug = u_ref[:, sl]
            p = _window_sum(ug, w, False) / _window_count(n, w) - ug
            y_ref[:, sl] = (_dot(p, w_ref[g], NN) * s_ref[:, sl]).astype(BF)

    return _pcall(body, name=name, out_shape=jax.ShapeDtypeStruct((n, pd), BF),
                  compiler_params=_params())(u, pool_w, scale)


def _pool_bwd(name, u, dcat, pool_w, scale, row_off):
    n, pd = u.shape
    ng = len(POOL_WINDOWS)
    gd = pd // ng

    def body(u_ref, dy_ref, w_ref, s_ref, du_ref, dw_ref, ds_ref):
        ds_ref[...] = jnp.zeros_like(ds_ref)
        for g, w in enumerate(POOL_WINDOWS):
            sl = slice(g * gd, (g + 1) * gd)
            ug, dy, wg = u_ref[:, sl], dy_ref[:, sl], w_ref[g]
            cnt = _window_count(n, w)
            p = _window_sum(ug, w, False) / cnt - ug
            ds_ref[0:1, sl] = jnp.sum(dy * _dot(p, wg, NN), axis=0, keepdims=True)
            dys = dy * s_ref[:, sl]
            dw_ref[g] = _dot(p, dys, TN)
            dp = _dot(dys, wg, NT)
            du_ref[:, sl] = (_window_sum(dp / cnt, w, True) - dp).astype(BF)

    rb = row_off // n
    return _pcall(body, name=name, grid=(1,),
                  in_specs=[pl.BlockSpec((n, pd), lambda i: (0, 0)), pl.BlockSpec((n, pd), lambda i: (rb, 0)),
                            pl.BlockSpec(pool_w.shape, lambda i: (0, 0, 0)), pl.BlockSpec(scale.shape, lambda i: (0, 0))],
                  out_specs=[pl.BlockSpec((n, pd), lambda i: (0, 0)), pl.BlockSpec((ng, gd, gd), lambda i: (0, 0, 0)),
                             pl.BlockSpec((8, pd), lambda i: (0, 0))],
                  out_shape=[jax.ShapeDtypeStruct((n, pd), BF), jax.ShapeDtypeStruct((ng, gd, gd), F32),
                             jax.ShapeDtypeStruct((8, pd), F32)],
                  compiler_params=_params(("arbitrary",)))(u, dcat, pool_w, scale)


def _edge_shift(z, k):
    n = z.shape[0]
    t = lax.broadcasted_iota(jnp.int32, (n, 1), 0)
    keep = (t >= k) if k > 0 else (t < n + k)
    return jnp.where(keep, pltpu.roll(z, k % n, 0), 0.0)


def _conv_fwd(name, p3, cw, tc):
    n, cd = p3.shape[0], p3.shape[1] // 3
    nb = cd // tc

    def body(b_ref, c_ref, v_ref, w_ref, y_ref):
        z = c_ref[...] * v_ref[...]
        w = w_ref[...]
        zc = w[0:1] * _edge_shift(z, 1) + w[1:2] * z + w[2:3] * _edge_shift(z, -1)
        y_ref[...] = (b_ref[...] * zc).astype(BF)

    return _pcall(body, name=name, grid=(nb,),
                  in_specs=[pl.BlockSpec((n, tc), lambda j: (0, j)), pl.BlockSpec((n, tc), lambda j: (0, nb + j)),
                            pl.BlockSpec((n, tc), lambda j: (0, 2 * nb + j)), pl.BlockSpec((3, tc), lambda j: (0, j))],
                  out_specs=pl.BlockSpec((n, tc), lambda j: (0, j)), out_shape=jax.ShapeDtypeStruct((n, cd), BF),
                  compiler_params=_params(("arbitrary",)))(p3, p3, p3, cw)


def _conv_bwd(name, p3, cw, dy, tc):
    n, cd = dy.shape
    nb = cd // tc

    def body(b_ref, c_ref, v_ref, w_ref, dy_ref, dp_ref, dw_ref):
        cv, vv, w, dyv = c_ref[...], v_ref[...], w_ref[...], dy_ref[...]
        z = cv * vv
        zl, zr = _edge_shift(z, 1), _edge_shift(z, -1)
        zc = w[0:1] * zl + w[1:2] * z + w[2:3] * zr
        dzc = dyv * b_ref[...]
        dz = w[0:1] * _edge_shift(dzc, -1) + w[1:2] * dzc + w[2:3] * _edge_shift(dzc, 1)
        dp_ref[0] = (dyv * zc).astype(BF)
        dp_ref[1] = (dz * vv).astype(BF)
        dp_ref[2] = (dz * cv).astype(BF)
        dw_ref[...] = jnp.zeros_like(dw_ref)
        dw_ref[0:1, :] = jnp.sum(dzc * zl, axis=0, keepdims=True)
        dw_ref[1:2, :] = jnp.sum(dzc * z, axis=0, keepdims=True)
        dw_ref[2:3, :] = jnp.sum(dzc * zr, axis=0, keepdims=True)

    col = pl.BlockSpec((n, tc), lambda j: (0, j))
    return _pcall(body, name=name, grid=(nb,),
                  in_specs=[col, pl.BlockSpec((n, tc), lambda j: (0, nb + j)),
                            pl.BlockSpec((n, tc), lambda j: (0, 2 * nb + j)), pl.BlockSpec((3, tc), lambda j: (0, j)), col],
                  out_specs=[pl.BlockSpec((3, n, tc), lambda j: (0, 0, j)), pl.BlockSpec((8, tc), lambda j: (0, j))],
                  out_shape=[jax.ShapeDtypeStruct((3, n, cd), BF), jax.ShapeDtypeStruct((8, cd), F32)],
                  compiler_params=_params(("arbitrary",)))(p3, p3, p3, cw, dy)


def _conv_din(name, dp3, w_in, tm):
    _, n, cd = dp3.shape
    d = w_in.shape[0]

    def body(a_ref, w_ref, o_ref, acc_ref):
        j = pl.program_id(1)
        part = _dot(a_ref[...], w_ref[...], NT)

        @pl.when(j == 0)
        def _():
            acc_ref[...] = part

        @pl.when(j > 0)
        def _():
            acc_ref[...] += part

        @pl.when(j == 2)
        def _():
            o_ref[...] = acc_ref[...]

    return _pcall(body, name=name, grid=(n // tm, 3),
                  in_specs=[pl.BlockSpec((None, tm, cd), lambda i, j: (j, i, 0)),
                            pl.BlockSpec((d, cd), lambda i, j: (0, j))],
                  out_specs=pl.BlockSpec((tm, d), lambda i, j: (i, 0)), out_shape=jax.ShapeDtypeStruct((n, d), F32),
                  scratch_shapes=[pltpu.VMEM((tm, d), F32)],
                  compiler_params=_params(("arbitrary", "arbitrary")))(dp3, w_in)


def _conv_dw_in(name, u, dp3, tmm, tn):
    n, d = u.shape
    cd = dp3.shape[2]
    nb = cd // tn

    def body(u_ref, z_ref, o_ref):
        o_ref[...] = _dot(u_ref[...], z_ref[...], TN)

    return _pcall(body, name=name, grid=(3 * nb, d // tmm),
                  in_specs=[pl.BlockSpec((n, tmm), lambda j, mi: (0, mi)),
                            pl.BlockSpec((None, n, tn), lambda j, mi: (j // nb, 0, j % nb))],
                  out_specs=pl.BlockSpec((tmm, tn), lambda j, mi: (mi, j)),
                  out_shape=jax.ShapeDtypeStruct((d, 3 * cd), F32),
                  compiler_params=_params(("arbitrary", "arbitrary")))(u, dp3)


def _loss_head(name, h, target, gain, tm):
    d = h.shape[1]

    def fn(rows, consts, m):
        hv, tv = rows
        g = consts[0][0:1, :]
        n, r = _rms(hv)
        err = n * g - tv
        dy = err / d
        loss = 0.5 * jnp.sum(err * err) / d
        acc = {0: jnp.sum(dy * n, axis=0, keepdims=True), 1: jnp.full((1, d), loss, F32)}
        return [_rms_bwd(dy * g, n, r)], acc

    return _rows_call(name, fn, h.shape[0], tm, [h, target], [gain], None, [(d, F32)], acc_w=d)


def _adamw(name, w, g, m, v, after=None):
    shape = w.shape
    cdim = shape[-1]
    r = max(1, math.prod(shape[:-1]))
    tr = r
    if r * cdim * 4 > (3 << 19):
        tr = _pick(r, max(8, (3 << 19) // (cdim * 4)), 8)
    c1 = 1.0 / (1.0 - ADAM_B1 ** ADAM_STEP)
    c2 = 1.0 / (1.0 - ADAM_B2 ** ADAM_STEP)

    def body(w_ref, g_ref, m_ref, v_ref, *rest):
        d_ref, nm_ref, nv_ref = rest[-3:]
        gv = g_ref[...]
        nm = ADAM_B1 * m_ref[...] + (1.0 - ADAM_B1) * gv
        nv = ADAM_B2 * v_ref[...] + (1.0 - ADAM_B2) * (gv * gv)
        nm_ref[...] = nm
        nv_ref[...] = nv
        d_ref[...] = -ADAM_LR * ((nm * c1) / (jnp.sqrt(nv * c2) + ADAM_EPS) + ADAM_WD * w_ref[...])

    spec = pl.BlockSpec((tr, cdim), lambda i: (i, 0))
    extra = [] if after is None else [after]
    res = _pcall(body, name=name, grid=(r // tr,),
                 in_specs=[spec] * 4 + [pl.BlockSpec(memory_space=pl.ANY)] * len(extra), out_specs=[spec] * 3,
                 out_shape=[jax.ShapeDtypeStruct((r, cdim), F32)] * 3,
                 compiler_params=_params(("arbitrary",)))(*[t.reshape(r, cdim) for t in (w, g, m, v)], *extra)
    return [t.reshape(shape) for t in res]


def _ffn_half_fwd(tag, s, gains, mod, k, wts, n_lat, tm, h_tiles, tm_big):
    wg, wu, wd, idx = wts
    u = _adaln_fwd(f"adaln_{tag}", s, gains, k, mod, k, tm, h_tiles)
    a, b, hid = _ffn_up(f"ffn_up_{tag}", u, wg, wu, idx, tm)
    s_out, o = _ffn_down(f"ffn_down_{tag}", hid, wd, idx, s, mod, k, n_lat, tm_big)
    return s_out, (s, u, a, b, hid, o)


def _ffn_half_bwd(tag, ds_out, saved, gains, mod, k, wts, big_grads, tm, h_tiles, tm_big):
    wg, wu, wd, idx = wts
    g_gate, g_up, g_down = big_grads
    s, u, a, b, hid, o = saved
    d_o, acc_g = _resid_bwd(f"resid_bwd_{tag}", ds_out, o, mod, k, 0.5, tm, h_tiles)
    da, db = _ffn_dhid(f"ffn_dhid_{tag}", d_o, wd, idx, a, b, tm)
    du = _ffn_du(f"ffn_du_{tag}", da, db, wg, wu, idx, tm_big)
    d = u.shape[1]
    g_gate = _ffn_dw_in(f"ffn_dwg_{tag}", u, da, _pick(d, 256), g_gate, idx)
    g_up = _ffn_dw_in(f"ffn_dwu_{tag}", u, db, _pick(d, 256), g_up, idx)
    g_down = _ffn_dw_down(f"ffn_dwd_{tag}", hid, d_o, _pick(d, 512), g_down, idx)
    ds, acc_n = _adaln_bwd(f"adaln_bwd_{tag}", s, du, ds_out, gains, k, mod, k, tm, h_tiles)
    return ds, (g_gate, g_up, g_down), (acc_n[:, 0], acc_n[:, 1], acc_g[:, 0]), jnp.sum(acc_n[:, 2], axis=0)


def kernel(x, c, ctx, c_ctx, norm_g, w_mod, b_mod, ffn_w_gate, ffn_w_up, ffn_w_down, ab_w_in, pool_w, pool_scale, q_norm_g, w_uq, kv_norm_g, w_ukv, ab_w_out, conv_w_in, conv_w, conv_w_out, final_norm_g, loss_target, m_c_ctx, m_norm_g, m_w_mod, m_b_mod, m_ffn_w_gate, m_ffn_w_up, m_ffn_w_down, m_ab_w_in, m_pool_w, m_pool_scale, m_q_norm_g, m_w_uq, m_kv_norm_g, m_w_ukv, m_ab_w_out, m_conv_w_in, m_conv_w, m_conv_w_out, m_final_norm_g, v_c_ctx, v_norm_g, v_w_mod, v_b_mod, v_ffn_w_gate, v_ffn_w_up, v_ffn_w_down, v_ab_w_in, v_pool_w, v_pool_scale, v_q_norm_g, v_w_uq, v_kv_norm_g, v_w_ukv, v_ab_w_out, v_conv_w_in, v_conv_w, v_conv_w_out, v_final_norm_g):
    weights = dict(c_ctx=c_ctx, norm_g=norm_g, w_mod=w_mod, b_mod=b_mod, ffn_w_gate=ffn_w_gate, ffn_w_up=ffn_w_up,
                   ffn_w_down=ffn_w_down, ab_w_in=ab_w_in, pool_w=pool_w, pool_scale=pool_scale, q_norm_g=q_norm_g,
                   w_uq=w_uq, kv_norm_g=kv_norm_g, w_ukv=w_ukv, ab_w_out=ab_w_out, conv_w_in=conv_w_in, conv_w=conv_w,
                   conv_w_out=conv_w_out, final_norm_g=final_norm_g)
    mom_m = dict(c_ctx=m_c_ctx, norm_g=m_norm_g, w_mod=m_w_mod, b_mod=m_b_mod, ffn_w_gate=m_ffn_w_gate,
                 ffn_w_up=m_ffn_w_up, ffn_w_down=m_ffn_w_down, ab_w_in=m_ab_w_in, pool_w=m_pool_w,
                 pool_scale=m_pool_scale, q_norm_g=m_q_norm_g, w_uq=m_w_uq, kv_norm_g=m_kv_norm_g, w_ukv=m_w_ukv,
                 ab_w_out=m_ab_w_out, conv_w_in=m_conv_w_in, conv_w=m_conv_w, conv_w_out=m_conv_w_out,
                 final_norm_g=m_final_norm_g)
    mom_v = dict(c_ctx=v_c_ctx, norm_g=v_norm_g, w_mod=v_w_mod, b_mod=v_b_mod, ffn_w_gate=v_ffn_w_gate,
                 ffn_w_up=v_ffn_w_up, ffn_w_down=v_ffn_w_down, ab_w_in=v_ab_w_in, pool_w=v_pool_w,
                 pool_scale=v_pool_scale, q_norm_g=v_q_norm_g, w_uq=v_w_uq, kv_norm_g=v_kv_norm_g, w_ukv=v_w_ukv,
                 ab_w_out=v_ab_w_out, conv_w_in=v_conv_w_in, conv_w=v_conv_w, conv_w_out=v_conv_w_out,
                 final_norm_g=v_final_norm_g)

    t_len, d = x.shape[1], x.shape[2]
    g_len = ctx.shape[1]
    r_len = t_len + g_len
    fc = ffn_w_gate.shape[3]
    heads = d // 128
    pool_dim = d // 2
    q_rank, kv_rank = q_norm_g.shape[1], kv_norm_g.shape[1]
    hw = heads * HEAD_PAD
    attn_scale = 1.0 / math.sqrt(QK_NOPE + QK_ROPE)
    kvr_w = kv_rank + HEAD_PAD
    in_w = pool_dim + q_rank + kvr_w
    tm = 256 if g_len % 256 == 0 else g_len
    assert t_len % tm == 0 and g_len % tm == 0 and t_len % g_len == 0 and pool_dim % 128 == 0
    h_tiles = t_len // tm
    tm_l0 = _pick(r_len, 768, tm)
    tm_l1 = _pick(t_len, 1024, tm)

    xi, yi, ci = lax.axis_index("x"), lax.axis_index("y"), lax.axis_index("c")
    me = 4 * xi + 2 * yi + ci
    shard = 2 * xi + yi

    def halves(w):
        return w.astype(BF).reshape(2, -1, w.shape[-1])

    ffn_names = ["ffn_w_gate", "ffn_w_up", "ffn_w_down"]
    mixer_names = [["ab_w_in", "w_uq", "w_ukv", "ab_w_out"], ["conv_w_in", "conv_w_out"]]
    big_names = ffn_names + mixer_names[0] + mixer_names[1]

    def layer_halves(l):
        return [halves(weights[nm][l]) for nm in ffn_names] + [halves(weights[nm]) for nm in mixer_names[l]]

    small = jnp.concatenate([norm_g.reshape(6, -1), conv_w[0]], axis=0)
    small = jnp.pad(small, ((0, 7), (0, 0)))
    c_row = jnp.pad(c, ((0, 7), (0, 0)))
    small_all, c_all = _gather_all("gather_small", [small, c_row])
    small_full = small_all[::2].transpose(1, 0, 2).reshape(16, d)
    gains = [jnp.pad(small_full[3 * l:3 * l + 3], ((0, 5), (0, 0))) for l in range(2)]
    conv_w_full = small_full[6:9]
    c16 = jnp.concatenate([c_all[:, 0], c_ctx[None], jnp.zeros((7, d), F32)], axis=0)

    n_col = w_mod.shape[2]
    b_sh = lax.dynamic_slice_in_dim(b_mod, shard * n_col, n_col, axis=1)
    m_sh = [_mm(f"mod_fwd_{l}", c16, w_mod[l], 'nn', tm=16, tn=768, a_pre=_silu,
                epi=lambda acc, i, bv: (acc + bv,), epi_args=(b_sh[l:l + 1],), epi_kinds=('n',)) for l in range(2)]
    m_all = _gather_all("gather_mod", [jnp.concatenate(m_sh, axis=0)])[0]
    m_full = m_all[::2].reshape(N_SHARD, 2, 16, n_col).transpose(1, 2, 0, 3).reshape(2, 16, N_MOD * d)
    mod_h = [jnp.pad(lax.dynamic_index_in_dim(m_full[l], me, 0, keepdims=False).reshape(N_MOD, d), ((0, 7), (0, 0)))
             for l in range(2)]
    mod_g0 = jnp.pad(m_full[0, 8].reshape(N_MOD, d), ((0, 7), (0, 0)))
    mods = [jnp.stack([mod_h[0], mod_g0]), mod_h[1][None]]

    gather = _chips_start("gather_start", [layer_halves(0), layer_halves(1)], True)

    def gathered_layer(l, after):
        own, landed = _chips_wait(f"gather_wait_l{l}", gather[l], after)
        full = _forward_sibling(f"gather_forward_l{l}", landed, own)
        return {nm: g.reshape(N_SHARD, 2 * g.shape[1], g.shape[2])
                for nm, g in zip(ffn_names + mixer_names[l], full)}

    def ffn_weights(gw):
        wg, wu = gw["ffn_w_gate"].reshape(N_SHARD, 2, d, fc), gw["ffn_w_up"].reshape(N_SHARD, 2, d, fc)
        return [(wg, wu, gw["ffn_w_down"], f) for f in range(2)]

    gw0 = gathered_layer(0, gather[0]['token'])
    ffn_w = [ffn_weights(gw0), None]
    w_out_full = gw0["ab_w_out"].reshape(-1, d)
    w_uq_full = gw0["w_uq"].reshape(q_rank, heads * (QK_NOPE + QK_ROPE))
    w_ukv_full = gw0["w_ukv"].transpose(1, 0, 2).reshape(kv_rank, heads * (QK_NOPE + V_HEAD))
    w_in_full = gw0["ab_w_in"].transpose(1, 0, 2).reshape(d, -1)

    wq_p = jnp.pad(w_uq_full.reshape(q_rank, heads, QK_NOPE + QK_ROPE),
                   ((0, 0), (0, 0), (0, HEAD_PAD - QK_NOPE - QK_ROPE))).reshape(q_rank, hw)
    ukv3 = w_ukv_full.reshape(kv_rank, heads, QK_NOPE + V_HEAD)
    wk_top = jnp.pad(ukv3[..., :QK_NOPE], ((0, 0), (0, 0), (0, HEAD_PAD - QK_NOPE))).reshape(kv_rank, hw)
    wv_top = jnp.pad(ukv3[..., QK_NOPE:], ((0, 0), (0, 0), (0, HEAD_PAD - V_HEAD))).reshape(kv_rank, hw)
    spread = jnp.zeros((HEAD_PAD, heads, HEAD_PAD), BF).at[
        jnp.arange(QK_ROPE)[:, None], jnp.arange(heads)[None, :], QK_NOPE + jnp.arange(QK_ROPE)[:, None]].set(1.0)
    wk_ext = jnp.concatenate([wk_top, spread.reshape(HEAD_PAD, hw)], axis=0)
    wv_ext = jnp.concatenate([wv_top, jnp.zeros((HEAD_PAD, hw), BF)], axis=0)
    w_in_pool = w_in_full[:, :pool_dim]
    w_in_q = w_in_full[:, pool_dim:pool_dim + q_rank]
    w_in_kvr = jnp.pad(w_in_full[:, pool_dim + q_rank:], ((0, 0), (0, HEAD_PAD - QK_ROPE)))
    w_out_attn = jnp.pad(w_out_full[pool_dim:].reshape(heads, V_HEAD, d),
                         ((0, 0), (0, HEAD_PAD - V_HEAD), (0, 0))).reshape(hw, d)
    w_out_p = jnp.concatenate([w_out_full[:pool_dim], w_out_attn], axis=0)

    s0 = jnp.concatenate([x[0], ctx[0]], axis=0)
    s1, sav_f00 = _ffn_half_fwd("l0a", s0, gains[0], mods[0], 0, ffn_w[0][0], t_len, tm, h_tiles, tm_l0)
    u_mix = _adaln_fwd("adaln_l0m", s1, gains[0], 1, mods[0], 1, tm, h_tiles)
    p_pool = _mm("in_pool", u_mix, w_in_pool, 'nn', tm=tm, tn=pool_dim)
    p_q = _mm("in_q", u_mix, w_in_q, 'nn', tm=tm, tn=q_rank)
    p_kvr = _mm("in_kvr", u_mix, w_in_kvr, 'nn', tm=tm, tn=kvr_w)
    qg = jnp.pad(q_norm_g, ((0, 7), (0, 0)))
    kvg = jnp.pad(kv_norm_g, ((0, 7), (0, 0)))
    tq_c, tq_s = _rope_tables(t_len, g_len, QK_NOPE)
    tk_c, tk_s = _rope_tables(t_len, g_len, 0)

    def qn_fn(rows, consts, m):
        n, _ = _rms(rows[0])
        return [n * consts[0][0:1, :]], {}

    qn = _rows_call("q_norm", qn_fn, r_len, tm, [p_q], [qg], None, [(q_rank, BF)])[0]
    q_r = _mm("q_up", qn, wq_p, 'nn', tm=tm, tn=hw, out_dtypes=(BF,),
              epi=lambda acc, i, ct, st: (_rope(acc, ct, st),), epi_args=(tq_c, tq_s), epi_kinds=('mt', 'mt'))

    def kvn_fn(rows, consts, m):
        pv, ct, st = rows
        n, _ = _rms(pv[:, :kv_rank])
        return [jnp.concatenate([n * consts[0][0:1, :], _rope(pv[:, kv_rank:], ct, st)], axis=1)], {}

    kvn = _rows_call("kv_norm", kvn_fn, r_len, tm, [p_kvr, tk_c, tk_s], [kvg], None, [(kvr_w, BF)])[0]
    k_p = _mm("k_up", kvn, wk_ext, 'nn', tm=tm, tn=hw, out_dtypes=(BF,))
    v_p = _mm("v_up", kvn, wv_ext, 'nn', tm=tm, tn=hw, out_dtypes=(BF,))
    o_h, lse_h = _attn_fwd("attn_h", q_r, k_p, v_p, t_len, 0, r_len, 0, heads, tm, attn_scale)
    o_g, lse_g = _attn_fwd("attn_g", q_r, k_p, v_p, g_len, t_len, g_len, t_len // g_len, heads, tm, attn_scale)
    y_h = _pool_fwd("pool_h", p_pool[:t_len], pool_w[0], pool_scale)
    y_g = _pool_fwd("pool_g", p_pool[t_len:], pool_w[0], pool_scale)
    cat = jnp.concatenate([jnp.concatenate([y_h, y_g], axis=0), jnp.concatenate([o_h, o_g], axis=0)], axis=1)

    def resid_epi(k3, n_lat, tmr):
        def epi(acc, i, sv, mv):
            return sv + _row_gate(mv, k3, i, tmr, n_lat) * acc, acc
        return epi

    s2, o_mix0 = _mm("mix_out_l0", cat, w_out_p, 'nn', tm=tm, tn=d, out_dtypes=(F32, F32),
                     epi=resid_epi(5, t_len, tm), epi_args=(s1, mods[0]), epi_kinds=('mn', 'w'))
    s3, sav_f01 = _ffn_half_fwd("l0b", s2, gains[0], mods[0], 2, ffn_w[0][1], t_len, tm, h_tiles, tm_l0)

    gw1 = gathered_layer(1, s3)
    ffn_w[1] = ffn_weights(gw1)
    cw_out_full = gw1["conv_w_out"].reshape(-1, d)
    cw_in_full = gw1["conv_w_in"].transpose(1, 0, 2).reshape(d, -1)
    tml = 256 if t_len % 256 == 0 else tm
    h3 = s3[:t_len]
    h4, sav_f10 = _ffn_half_fwd("l1a", h3, gains[1], mods[1], 0, ffn_w[1][0], t_len, tml, None, tm_l1)
    u_cv = _adaln_fwd("adaln_l1m", h4, gains[1], 1, mods[1], 1, tml, None)
    p3 = _mm("conv_in", u_cv, cw_in_full, 'nn', tm=tml, tn=512)
    cwp = conv_w_full
    tc = _pick(d, 256)
    y_cv = _conv_fwd("conv_fwd", p3, cwp, tc)
    h5, o_mix1 = _mm("mix_out_l1", y_cv, cw_out_full, 'nn', tm=tml, tn=d, out_dtypes=(F32, F32),
                     epi=resid_epi(5, t_len, tml), epi_args=(h4, mods[1]), epi_kinds=('mn', 'w'))
    h6, sav_f11 = _ffn_half_fwd("l1b", h5, gains[1], mods[1], 2, ffn_w[1][1], t_len, tml, None, tm_l1)

    fg = jnp.pad(final_norm_g[None], ((0, 7), (0, 0)))
    dh6, acc_loss = _loss_head("loss_head", h6, loss_target[0], fg, tml)
    loss = lax.psum(acc_loss[0, 1, 0], ("x", "y", "c"))
    d_final_g = acc_loss[0, 0]

    dgain = [[None] * 3 for _ in range(2)]
    dmod = [[None] * N_MOD for _ in range(2)]

    def put(l, k, triple):
        dmod[l][3 * k], dmod[l][3 * k + 1], dmod[l][3 * k + 2] = triple

    def empty_ffn_grads():
        return (lax.empty((N_SHARD, 2 * d, fc), BF), lax.empty((N_SHARD, 2 * d, fc), BF),
                lax.empty((N_SHARD, 2 * fc, d), BF))

    def by_shard_rows(g):
        return g.reshape(N_SHARD, -1, g.shape[-1])

    def by_shard_cols(g):
        return g.reshape(g.shape[0], N_SHARD, -1).transpose(1, 0, 2)

    def presum_and_start(l, big):
        send = [b.astype(BF).reshape(N_DEV, b.shape[1] // 2, b.shape[2]) for b in big]
        pre = _presum_sibling(f"grads_presum_l{l}", send)
        return _chips_start(f"grads_start_l{l}", [pre], False)[0]

    def landed_sums(l, handle, after):
        pre, landed = _chips_wait(f"grads_wait_l{l}", handle, after)
        chip = 2 * xi + yi
        landed = [lax.dynamic_update_slice_in_dim(z, lax.dynamic_slice_in_dim(p, chip, 1, 0), chip, 0)
                  for z, p in zip(landed, pre)]
        return [_sum_lead(f"sum_grads_l{l}_{nm}", z) for nm, z in zip(ffn_names + mixer_names[l], landed)]

    ffn_g = empty_ffn_grads()
    dh5, ffn_g, tr, dgain[1][2] = _ffn_half_bwd("l1b", dh6, sav_f11, gains[1], mods[1], 2, ffn_w[1][1], ffn_g,
                                                tml, None, tm_l1)
    put(1, 2, tr)
    d_o1, acc_g1 = _resid_bwd("resid_bwd_l1m", dh5, o_mix1, mods[1], 1, 1.0, tml, None)
    dy_cv = _mm("mix_out_l1_dx", d_o1, cw_out_full, 'nt', tm=tml, tn=d)
    d_cw_out = _mm("mix_out_l1_dw", y_cv, d_o1, 'tn', tm=256, tn=512)
    dp3, d_cw = _conv_bwd("conv_bwd", p3, cwp, dy_cv, tc)
    du_cv = _conv_din("conv_in_dx", dp3, cw_in_full, tml)
    d_cw_in = _conv_dw_in("conv_in_dw", u_cv, dp3, _pick(d, 256), _pick(d, 512))
    dh4, acc_n1 = _adaln_bwd("adaln_bwd_l1m", h4, du_cv, dh5, gains[1], 1, mods[1], 1, tml, None)
    put(1, 1, (acc_n1[:, 0], acc_n1[:, 1], acc_g1[:, 0]))
    dgain[1][1] = acc_n1[0, 2]
    dh3, ffn_g, tr, dgain[1][0] = _ffn_half_bwd("l1a", dh4, sav_f10, gains[1], mods[1], 0, ffn_w[1][0], ffn_g,
                                                tml, None, tm_l1)
    put(1, 0, tr)

    scatter1 = presum_and_start(1, list(ffn_g) + [by_shard_cols(d_cw_in), by_shard_rows(d_cw_out)])

    ds3 = jnp.concatenate([dh3, jnp.zeros((g_len, d), F32)], axis=0) + scatter1['token'][0, 0]
    ffn_g = empty_ffn_grads()
    ds2, ffn_g, tr, dgain[0][2] = _ffn_half_bwd("l0b", ds3, sav_f01, gains[0], mods[0], 2, ffn_w[0][1], ffn_g,
                                                tm, h_tiles, tm_l0)
    put(0, 2, tr)
    d_o0, acc_g0 = _resid_bwd("resid_bwd_l0m", ds2, o_mix0, mods[0], 1, 1.0, tm, h_tiles)
    dcat = _mm("mix_out_l0_dx", d_o0, w_out_p, 'nt', tm=tm, tn=pool_dim + hw)
    d_w_out_p = _mm("mix_out_l0_dw", cat, d_o0, 'tn', tm=256, tn=512)
    col_blk = pool_dim // HEAD_PAD
    dq_h, dk_h, dv_h = _attn_bwd("attn_bwd_h", q_r, k_p, v_p, cat, dcat, lse_h, t_len, 0, r_len, 0, heads, tm,
                                 attn_scale, col_blk)
    dq_g, dk_g, dv_g = _attn_bwd("attn_bwd_g", q_r, k_p, v_p, cat, dcat, lse_g, g_len, t_len, g_len,
                                 t_len // g_len, heads, tm, attn_scale, col_blk)
    dq_all = jnp.concatenate([dq_h, dq_g], axis=0)
    dk_all = dk_h.at[t_len:].add(dk_g)
    dv_all = dv_h.at[t_len:].add(dv_g)
    dkvn = _mm("k_up_dx", dk_all, wk_ext, 'nt', tm=tm, tn=kvr_w)
    dkvn = _mm("v_up_dx", dv_all, wv_ext, 'nt', tm=tm, tn=kvr_w, epi=lambda acc, i, prev: (acc + prev,),
               epi_args=(dkvn,), epi_kinds=('mn',))
    d_wk_ext = _mm("k_up_dw", kvn, dk_all, 'tn', tm=kvr_w, tn=512)
    d_wv_ext = _mm("v_up_dw", kvn, dv_all, 'tn', tm=kvr_w, tn=512)

    def kvn_bwd_fn(rows, consts, m):
        pv, dv_, ct, st = rows
        g = consts[0][0:1, :]
        n, r = _rms(pv[:, :kv_rank])
        dyn = dv_[:, :kv_rank]
        dckv = _rms_bwd(dyn * g, n, r)
        dkr = _rope_t(dv_[:, kv_rank:], ct, st)
        return [jnp.concatenate([dckv, dkr], axis=1)], {0: jnp.sum(dyn * n, axis=0, keepdims=True)}

    dp_kvr, acc_kvg = _rows_call("kv_norm_bwd", kvn_bwd_fn, r_len, tm, [p_kvr, dkvn, tk_c, tk_s], [kvg], None,
                                 [(kvr_w, BF)], acc_w=kv_rank)

    def qrope_bwd_fn(rows, consts, m):
        return [_rope_t(rows[0], rows[1], rows[2])], {}

    dq_pad = _rows_call("q_rope_bwd", qrope_bwd_fn, r_len, tm, [dq_all, tq_c, tq_s], [], None, [(hw, BF)])[0]
    dqn = _mm("q_up_dx", dq_pad, wq_p, 'nt', tm=tm, tn=q_rank)
    d_wq_p = _mm("q_up_dw", qn, dq_pad, 'tn', tm=256, tn=512)

    def qn_bwd_fn(rows, consts, m):
        pv, dv_ = rows
        g = consts[0][0:1, :]
        n, r = _rms(pv)
        return [_rms_bwd(dv_ * g, n, r)], {0: jnp.sum(dv_ * n, axis=0, keepdims=True)}

    dp_q, acc_qg = _rows_call("q_norm_bwd", qn_bwd_fn, r_len, tm, [p_q, dqn], [qg], None, [(q_rank, BF)],
                              acc_w=q_rank)
    dpu_h, dpw_h, dps_h = _pool_bwd("pool_bwd_h", p_pool[:t_len], dcat, pool_w[0], pool_scale, 0)
    dpu_g, dpw_g, dps_g = _pool_bwd("pool_bwd_g", p_pool[t_len:], dcat, pool_w[0], pool_scale, t_len)
    dp_pool = jnp.concatenate([dpu_h, dpu_g], axis=0)
    add_prev = lambda acc, i, prev: (acc + prev,)
    du_mix = _mm("in_pool_dx", dp_pool, w_in_pool, 'nt', tm=tm, tn=d)
    du_mix = _mm("in_q_dx", dp_q, w_in_q, 'nt', tm=tm, tn=d, epi=add_prev, epi_args=(du_mix,), epi_kinds=('mn',))
    du_mix = _mm("in_kvr_dx", dp_kvr, w_in_kvr, 'nt', tm=tm, tn=d, epi=add_prev, epi_args=(du_mix,), epi_kinds=('mn',))
    d_w_in = jnp.concatenate([
        _mm("in_pool_dw", u_mix, dp_pool, 'tn', tm=256, tn=pool_dim),
        _mm("in_q_dw", u_mix, dp_q, 'tn', tm=256, tn=q_rank),
        _mm("in_kvr_dw", u_mix, dp_kvr, 'tn', tm=256, tn=kvr_w)[:, :kv_rank + QK_ROPE]], axis=1)
    ds1, acc_n0 = _adaln_bwd("adaln_bwd_l0m", s1, du_mix, ds2, gains[0], 1, mods[0], 1, tm, h_tiles)
    put(0, 1, (acc_n0[:, 0], acc_n0[:, 1], acc_g0[:, 0]))
    dgain[0][1] = jnp.sum(acc_n0[:, 2], axis=0)
    ds0, ffn_g, tr, dgain[0][0] = _ffn_half_bwd("l0a", ds1, sav_f00, gains[0], mods[0], 0, ffn_w[0][0], ffn_g,
                                                tm, h_tiles, tm_l0)
    put(0, 0, tr)
    grad_x = ds0[:t_len][None]

    d_w_uq = d_wq_p.reshape(q_rank, heads, HEAD_PAD)[..., :QK_NOPE + QK_ROPE].reshape(q_rank, -1)
    d_w_ukv = jnp.concatenate([d_wk_ext[:kv_rank].reshape(kv_rank, heads, HEAD_PAD)[..., :QK_NOPE],
                               d_wv_ext[:kv_rank].reshape(kv_rank, heads, HEAD_PAD)[..., :V_HEAD]],
                              axis=-1).reshape(kv_rank, -1)
    d_w_out = jnp.concatenate([d_w_out_p[:pool_dim],
                               d_w_out_p[pool_dim:].reshape(heads, HEAD_PAD, d)[:, :V_HEAD].reshape(-1, d)], axis=0)

    sums1 = landed_sums(1, scatter1, ds0)

    dmh = jnp.stack([jnp.stack([dmod[l][k][0] for k in range(N_MOD)]) for l in range(2)])
    dmg0 = jnp.stack([dmod[0][k][1] for k in range(N_MOD)])
    dg_rows = jnp.stack([dgain[l][k] for l in range(2) for k in range(3)])
    pieces = [dmh.reshape(2 * N_MOD, d), dmg0, dg_rows, d_cw[:3], d_final_g[None],
              (dpw_h + dpw_g).reshape(-1, d), jnp.pad((dps_h + dps_g)[0], (0, d - pool_dim))[None],
              jnp.pad(acc_qg[0, 0], (0, d - q_rank))[None], jnp.pad(acc_kvg[0, 0], (0, d - kv_rank))[None]]
    n_piece = [p.shape[0] for p in pieces]
    pieces = [jnp.pad(p, ((0, (-p.shape[0]) % 8), (0, 0))) for p in pieces]
    small_g = jnp.concatenate(pieces, axis=0)
    sg_all = _gather_all("gather_small_grads", [small_g])[0]
    sg_sum = _sum_lead("sum_small_grads", sg_all)
    offs = [0]
    for p in pieces:
        offs.append(offs[-1] + p.shape[0])
    part = lambda j: sg_sum[offs[j]:offs[j] + n_piece[j]]
    sum_dmh, sum_dmg0, g_norm_full, g_conv_w_full = part(0).reshape(2, N_MOD * d), part(1).reshape(N_MOD * d), part(2), part(3)
    g_final = part(4)[0]
    g_pool_w = part(5).reshape(pool_w.shape)
    g_pool_scale = part(6)[:, :pool_dim]
    g_q_norm = part(7)[:, :q_rank]
    g_kv_norm = part(8)[:, :kv_rank]
    col0 = shard * (d // N_SHARD)
    g_norm_g = lax.dynamic_slice_in_dim(g_norm_full.reshape(2, 3, d), col0, d // N_SHARD, axis=2)
    g_conv_w = lax.dynamic_slice_in_dim(g_conv_w_full, col0, d // N_SHARD, axis=1)[None]
    g_b_mod = _sum_lead("sum_b_mod", jnp.stack([sum_dmh, jnp.stack([sum_dmg0, jnp.zeros_like(sum_dmg0)])]))

    dm16 = []
    for l in range(2):
        per_dev = sg_all[:, l * N_MOD:(l + 1) * N_MOD].reshape(N_DEV, N_MOD * d)
        row8 = sum_dmg0 if l == 0 else jnp.zeros_like(sum_dmg0)
        full = jnp.concatenate([per_dev, row8[None], jnp.zeros((7, N_MOD * d), F32)], axis=0)
        dm16.append(lax.dynamic_slice_in_dim(full, shard * n_col, n_col, axis=1))
    g_w_mod = jnp.stack([_mm(f"mod_dw_{l}", c16, dm16[l], 'tn', tm=256, tn=768, a_pre=_silu) for l in range(2)])
    dc16 = _mm("mod_dx", dm16[0], w_mod[0], 'nt', tm=16, tn=512, epi=lambda acc, i, cv: (acc * _dsilu(cv),),
               epi_args=(c16,), epi_kinds=('mn',))
    dc_all = _gather_all("gather_dc", [dc16])[0]
    g_c_ctx = _sum_lead("sum_dc", dc_all[::2])[8]

    grads = dict(c_ctx=g_c_ctx, norm_g=g_norm_g, w_mod=g_w_mod, b_mod=g_b_mod, pool_w=g_pool_w,
                 pool_scale=g_pool_scale, q_norm_g=g_q_norm, kv_norm_g=g_kv_norm, conv_w=g_conv_w, final_norm_g=g_final)
    names = list(weights)

    scatter0 = presum_and_start(0, list(ffn_g) + [by_shard_cols(d_w_in), by_shard_rows(d_w_uq),
                                                  by_shard_cols(d_w_ukv), by_shard_rows(d_w_out)])
    upd = {n: _adamw(f"adamw_{n}", weights[n], grads[n].reshape(weights[n].shape), mom_m[n], mom_v[n],
                     scatter0['token']) for n in names if n not in big_names}
    sums0 = landed_sums(0, scatter0, upd["w_mod"][0])

    swapped = _swap_sibling("swap_halves", sums0 + sums1)
    n0 = len(sums0)
    for l, part_l in enumerate((swapped[:n0], swapped[n0:])):
        for nm, s in zip(ffn_names + mixer_names[l], part_l):
            grads[(nm, l)] = s
    for nm in ffn_names:
        grads[nm] = jnp.stack([grads.pop((nm, l)).reshape(weights[nm].shape[1:]) for l in range(2)])
    for l in range(2):
        for nm in mixer_names[l]:
            grads[nm] = grads.pop((nm, l)).reshape(weights[nm].shape)
    upd.update({n: _adamw(f"adamw_{n}", weights[n], grads[n], mom_m[n], mom_v[n]) for n in big_names})
    return (loss, grad_x, *[grads[n].reshape(weights[n].shape) for n in names], *[upd[n][0] for n in names],
            *[upd[n][1] for n in names], *[upd[n][2] for n in names])
```

```python
import functools
import math

import jax
import jax.numpy as jnp
from jax import lax
from jax.experimental import pallas as pl
from jax.experimental.pallas import tpu as pltpu

F32 = jnp.float32
BF = jnp.bfloat16
MESH = pl.DeviceIdType.MESH

N_DEV = 8
N_SHARD = 4
RMS_EPS = 1e-6
N_MOD = 9
POOL_WINDOWS = (2, 4, 8, 16)
QK_NOPE = 64
QK_ROPE = 32
V_HEAD = 64
HEAD_PAD = 128
GRID_W = 64
ROPE_THETA = 10000.0
POOL_PAD = 16
ADAM_LR, ADAM_B1, ADAM_B2, ADAM_EPS, ADAM_WD, ADAM_STEP = 0.001, 0.9, 0.999, 1e-08, 0.01, 10
VMEM_LIMIT = 56 * 1024 * 1024


def _pcall(body, **kw):
    return pl.pallas_call(body, **kw)


def _params(sem=None):
    return pltpu.CompilerParams(dimension_semantics=sem, vmem_limit_bytes=VMEM_LIMIT)


def _pick(n, pref, mult=128):
    best = None
    d = mult
    while d <= min(n, pref):
        if n % d == 0:
            best = d
        d += mult
    return best if best is not None else n


def _silu(z):
    return z * jax.nn.sigmoid(z)


def _dsilu(z):
    s = jax.nn.sigmoid(z)
    return s * (1.0 + z * (1.0 - s))


def _dot(a, b, dims):
    return lax.dot_general(a.astype(BF), b.astype(BF), (dims, ((), ())), preferred_element_type=F32)


NN = ((1,), (0,))
NT = ((1,), (1,))
TN = ((0,), (0,))


ALL_FLIPS = [(kx, ky, kc) for kx in (0, 1) for ky in (0, 1) for kc in (0, 1) if (kx, ky, kc) != (0, 0, 0)]
CHIP_FLIPS = [(1, 0, 0), (0, 1, 0), (1, 1, 0)]
SIBLING = (0, 0, 1)
COMM_SPLIT = 8
SPLIT_MIN_ROWS = 256


def _exchange(name, arrays, plan, lead, whole_src, split=COMM_SPLIT):
    n = len(arrays)
    blk_shapes = [tuple(a.shape) if whole_src else tuple(a.shape[1:]) for a in arrays]
    splits = []
    for shp in blk_shapes:
        s = 1
        while s * 2 <= split and shp[0] % (s * 2) == 0 and (shp[0] // (s * 2)) % 16 == 0 \
                and shp[0] // (s * 2) >= SPLIT_MIN_ROWS:
            s *= 2
        splits.append(s)
    items = plan(0, 0, 0)
    n_items = len(items)
    remote_ids = [k for k, it in enumerate(items) if it[0] is not None]
    local_ids = [k for k, it in enumerate(items) if it[0] is None]
    slots = [(a, s) for s in range(max(splits)) for a in range(n) if s < splits[a]]
    n_slot = len(slots)

    def body(*refs):
        ins, outs = refs[:n], refs[n:2 * n]
        send_sems, recv_sems, loc_sems = refs[2 * n:]
        x, y, c = lax.axis_index("x"), lax.axis_index("y"), lax.axis_index("c")
        plan_here = plan(x, y, c)

        def rows(ref, a, s):
            rc = blk_shapes[a][0] // splits[a]
            return ref.at[pl.ds(s * rc, rc)]

        def make(si, k):
            a, s = slots[si]
            flip, src, dst, _ = plan_here[k]
            base = outs[a] if src[0] == 'out' else ins[a]
            src_ref = rows(base if src[1] is None else base.at[src[1]], a, s)
            dst_ref = rows(outs[a].at[dst], a, s)
            if flip is None:
                return pltpu.make_async_copy(src_ref, dst_ref, loc_sems.at[si * max(1, len(local_ids)) + local_ids.index(k)])
            peer = (1 - x if flip[0] else x, 1 - y if flip[1] else y, 1 - c if flip[2] else c)
            sem = si * len(remote_ids) + remote_ids.index(k)
            return pltpu.make_async_remote_copy(src_ref=src_ref, dst_ref=dst_ref, send_sem=send_sems.at[sem],
                                                recv_sem=recv_sems.at[sem], device_id=peer, device_id_type=MESH)

        copies = {}
        for si in range(n_slot):
            for k in range(n_items):
                if plan_here[k][3] is None:
                    copies[si, k] = make(si, k)
                    copies[si, k].start()
        arrived = set()
        for si in range(n_slot):
            for k in range(n_items):
                after = plan_here[k][3]
                if after is not None:
                    if (si, after) not in arrived:
                        copies[si, after].wait_recv()
                        arrived.add((si, after))
                    copies[si, k] = make(si, k)
                    copies[si, k].start()
        for (si, k), cp in copies.items():
            if plan_here[k][0] is None:
                cp.wait()
            else:
                cp.wait_send()
                if (si, k) not in arrived:
                    cp.wait_recv()

    any_spec = pl.BlockSpec(memory_space=pl.ANY)
    n_rem = max(1, n_slot * len(remote_ids))
    outs = _pcall(
        body, name=name,
        out_shape=[jax.ShapeDtypeStruct((lead,) + s, a.dtype) for s, a in zip(blk_shapes, arrays)],
        in_specs=[any_spec] * n, out_specs=[any_spec] * n,
        scratch_shapes=[pltpu.SemaphoreType.DMA((n_rem,)), pltpu.SemaphoreType.DMA((n_rem,)),
                        pltpu.SemaphoreType.DMA((max(1, n_slot * len(local_ids)),))],
    )(*arrays)
    return list(outs)


def _place(x, y, c):
    return 4 * x + 2 * y + c


def _flip(v, f):
    return 1 - v if f else v


def _gather_all(name, arrays):
    def plan(x, y, c):
        me = _place(x, y, c)
        return [(None, ('in', None), me, None)] + [(f, ('in', None), me, None) for f in ALL_FLIPS]
    return _exchange(name, arrays, plan, N_DEV, True)


HBM_SPEC = pl.BlockSpec(memory_space=pltpu.HBM)
SEM_SPEC = pl.BlockSpec(memory_space=pltpu.SEMAPHORE)
SIDE_EFFECT = pltpu.SideEffectType.DATAFLOW_SIDE_EFFECTING


def _split_start(name, bufs, n_copies, build):
    n = len(bufs)

    def body(*refs):
        for cp in build(refs[:n], refs[n], refs[n + 1]):
            cp.start()
        token = refs[-1]
        token[...] = jnp.zeros_like(token)

    res = _pcall(
        body, name=name,
        out_shape=(pltpu.SemaphoreType.DMA((n_copies,)), pltpu.SemaphoreType.DMA((n_copies,)),
                   *[pltpu.HBM(b.shape, b.dtype) for b in bufs], jax.ShapeDtypeStruct((8, 128), F32)),
        in_specs=[HBM_SPEC] * n,
        out_specs=(SEM_SPEC, SEM_SPEC, *[HBM_SPEC] * n, pl.BlockSpec(memory_space=pltpu.VMEM)),
        input_output_aliases={i: 2 + i for i in range(n)},
        compiler_params=pltpu.CompilerParams(has_side_effects=SIDE_EFFECT),
    )(*[pltpu.with_memory_space_constraint(b, pltpu.HBM) for b in bufs])
    return dict(send=res[0], recv=res[1], bufs=list(res[2:2 + n]), token=res[-1], build=build)


def _split_wait(name, handle, after):
    n = len(handle['bufs'])
    build = handle['build']

    def body(*refs):
        for cp in build(refs[:n], refs[n], refs[n + 1]):
            cp.wait_send()
            cp.wait_recv()

    res = _pcall(
        body, name=name, out_shape=tuple(pltpu.HBM(b.shape, b.dtype) for b in handle['bufs']),
        in_specs=[HBM_SPEC] * n + [SEM_SPEC, SEM_SPEC, pl.BlockSpec(memory_space=pl.ANY)],
        out_specs=tuple([HBM_SPEC] * n), input_output_aliases={i: i for i in range(n)},
        compiler_params=pltpu.CompilerParams(has_side_effects=SIDE_EFFECT),
    )(*handle['bufs'], handle['send'], handle['recv'], after)
    return list(res)


def _landing(lead, arrays):
    return [pltpu.with_memory_space_constraint(lax.empty((lead,) + tuple(a.shape[1:]), a.dtype), pltpu.HBM)
            for a in arrays]


def _copy_list(n, per_array, make):
    def build(refs, send_sems, recv_sems):
        copies = []
        for a in range(n):
            for j in range(per_array):
                src, dst, peer = make(refs, a, j)
                k = a * per_array + j
                copies.append(pltpu.make_async_remote_copy(src_ref=src, dst_ref=dst, send_sem=send_sems.at[k],
                                                           recv_sem=recv_sems.at[k], device_id=peer,
                                                           device_id_type=MESH))
        return copies
    return build


def _mesh_place():
    x, y, c = lax.axis_index("x"), lax.axis_index("y"), lax.axis_index("c")
    return x, y, c, 2 * x + y


def _chips_gather_build(n):
    def make(refs, a, j):
        x, y, c, chip = _mesh_place()
        px, py = _flip(x, CHIP_FLIPS[j][0]), _flip(y, CHIP_FLIPS[j][1])
        return refs[a].at[c], refs[n + a].at[2 * chip + c], (px, py, c)
    return _copy_list(n, len(CHIP_FLIPS), make)


def _chips_scatter_build(n):
    def make(refs, a, j):
        x, y, c, chip = _mesh_place()
        px, py = _flip(x, CHIP_FLIPS[j][0]), _flip(y, CHIP_FLIPS[j][1])
        return refs[a].at[2 * px + py], refs[n + a].at[chip], (px, py, c)
    return _copy_list(n, len(CHIP_FLIPS), make)


def _sibling_forward_build(n):
    def make(refs, a, j):
        x, y, c, _ = _mesh_place()
        blk = 2 * (2 * _flip(x, CHIP_FLIPS[j][0]) + _flip(y, CHIP_FLIPS[j][1])) + c
        return refs[a].at[blk], refs[a].at[blk], (x, y, 1 - c)
    return _copy_list(n, len(CHIP_FLIPS), make)


def _sibling_halves_build(n):
    def make(refs, a, j):
        x, y, c, _ = _mesh_place()
        return refs[a].at[2 * j + 1 - c], refs[n + a].at[j], (x, y, 1 - c)
    return _copy_list(n, N_SHARD, make)


def _sibling_whole_build(n):
    def make(refs, a, j):
        x, y, c, _ = _mesh_place()
        return refs[a], refs[n + a], (x, y, 1 - c)
    return _copy_list(n, 1, make)


def _add_halves(name, send, land):
    _, r, cdim = send.shape
    tr = _pick(r, max(16, (1 << 20) // (cdim * 2)), 16)

    def body(c_ref, own_ref, got_ref, o_ref):
        o_ref[...] = (own_ref[...].astype(F32) + got_ref[...].astype(F32)).astype(BF)

    grid_spec = pltpu.PrefetchScalarGridSpec(
        num_scalar_prefetch=1, grid=(N_SHARD, r // tr),
        in_specs=[pl.BlockSpec((None, tr, cdim), lambda sh, i, cr: (2 * sh + cr[0], i, 0)),
                  pl.BlockSpec((None, tr, cdim), lambda sh, i, cr: (sh, i, 0))],
        out_specs=pl.BlockSpec((None, tr, cdim), lambda sh, i, cr: (sh, i, 0)))
    core = lax.axis_index("c").astype(jnp.int32).reshape(1)
    return _pcall(body, name=name, grid_spec=grid_spec, out_shape=jax.ShapeDtypeStruct((N_SHARD, r, cdim), BF),
                  compiler_params=_params(("arbitrary", "arbitrary")))(core, send, land)


def _sum_lead(name, arr, out_dtype=F32):
    n, r, cdim = arr.shape
    tr = r
    limit = (4 << 20) // (n * cdim * arr.dtype.itemsize)
    if r > limit:
        tr = _pick(r, max(limit, 16), 16)

    def body(x_ref, o_ref):
        acc = x_ref[0].astype(F32)
        for d in range(1, n):
            acc = acc + x_ref[d].astype(F32)
        o_ref[...] = acc.astype(out_dtype)

    return _pcall(body, name=name, grid=(r // tr,),
                  in_specs=[pl.BlockSpec((n, tr, cdim), lambda i: (0, i, 0))],
                  out_specs=pl.BlockSpec((tr, cdim), lambda i: (i, 0)),
                  out_shape=jax.ShapeDtypeStruct((r, cdim), out_dtype),
                  compiler_params=_params(("arbitrary",)))(arr)


def _rows_call(name, fn, n_rows, tm, rows, consts, mod, outs, acc_w=None, h_tiles=None):
    nt = n_rows // tm
    ht = nt if h_tiles is None else h_tiles
    ng = 1 if mod is None else mod.shape[0]
    n_r, n_c, n_o = len(rows), len(consts), len(outs)
    has_mod = mod is not None

    def body(*refs):
        i = pl.program_id(0)
        first = (i % ht) == 0
        row_refs, const_refs = refs[:n_r], refs[n_r:n_r + n_c]
        p = n_r + n_c
        mod_tile = refs[p][...] if has_mod else None
        p += int(has_mod)
        out_refs = refs[p:p + n_o]
        o, acc = fn([r[...] for r in row_refs], [r[...] for r in const_refs], mod_tile)
        for r, v in zip(out_refs, o):
            r[...] = v.astype(r.dtype)
        if acc_w is not None:
            acc_ref = refs[p + n_o]

            @pl.when(first)
            def _():
                acc_ref[...] = jnp.zeros_like(acc_ref)

            for k, v in acc.items():
                acc_ref[k:k + 1, :] += v

    in_specs = [pl.BlockSpec((tm, r.shape[1]), lambda i: (i, 0)) for r in rows]
    in_specs += [pl.BlockSpec(cst.shape, lambda i, nd=cst.ndim: (0,) * nd) for cst in consts]
    args = list(rows) + list(consts)
    if has_mod:
        in_specs.append(pl.BlockSpec((None,) + mod.shape[1:], lambda i: (i // ht, 0, 0)))
        args.append(mod)
    out_shape = [jax.ShapeDtypeStruct((n_rows, w), dt) for w, dt in outs]
    out_specs = [pl.BlockSpec((tm, w), lambda i: (i, 0)) for w, _ in outs]
    if acc_w is not None:
        out_shape.append(jax.ShapeDtypeStruct((ng, 8, acc_w), F32))
        out_specs.append(pl.BlockSpec((None, 8, acc_w), lambda i: (i // ht, 0, 0)))
    res = _pcall(body, name=name, grid=(nt,), in_specs=in_specs, out_specs=out_specs, out_shape=out_shape,
                 compiler_params=_params(("arbitrary",)))(*args)
    return list(res)


def _rms(s):
    r = lax.rsqrt(jnp.mean(s * s, axis=1, keepdims=True) + RMS_EPS)
    return s * r, r


def _rms_bwd(dn, n, r):
    return r * (dn - n * jnp.mean(dn * n, axis=1, keepdims=True))


def _adaln_fwd(name, s, gains, gain_row, mod, k, tm, h_tiles, after=None):
    def fn(rows, consts, m):
        n, _ = _rms(rows[0])
        y = n * consts[0][gain_row:gain_row + 1, :]
        return [y * (1.0 + m[3 * k + 1:3 * k + 2, :]) + m[3 * k:3 * k + 1, :]], {}

    d = s.shape[1]
    consts = [gains] if after is None else [gains, after]
    return _rows_call(name, fn, s.shape[0], tm, [s], consts, mod, [(d, BF)], h_tiles=h_tiles)[0]


def _adaln_bwd(name, s, du, ds_res, gains, gain_row, mod, k, tm, h_tiles):
    def fn(rows, consts, m):
        sv, duv, res = rows
        gain = consts[0][gain_row:gain_row + 1, :]
        n, r = _rms(sv)
        y = n * gain
        dy = duv * (1.0 + m[3 * k + 1:3 * k + 2, :])
        acc = {0: jnp.sum(duv, axis=0, keepdims=True), 1: jnp.sum(duv * y, axis=0, keepdims=True),
               2: jnp.sum(dy * n, axis=0, keepdims=True)}
        return [_rms_bwd(dy * gain, n, r) + res], acc

    d = s.shape[1]
    return _rows_call(name, fn, s.shape[0], tm, [s, du, ds_res], [gains], mod, [(d, F32)], acc_w=d, h_tiles=h_tiles)


def _resid_bwd(name, ds_out, o, mod, k, cst, tm, h_tiles, after=None):
    def fn(rows, consts, m):
        dsv, ov = rows
        gate = m[3 * k + 2:3 * k + 3, :]
        return [cst * gate * dsv], {0: jnp.sum(cst * ov * dsv, axis=0, keepdims=True)}

    d = o.shape[1]
    consts = [] if after is None else [after]
    return _rows_call(name, fn, o.shape[0], tm, [ds_out, o], consts, mod, [(d, BF)], acc_w=d, h_tiles=h_tiles)


def _mm(name, a, b, mode, tm=256, tn=512, out_dtypes=(F32,), epi=None, epi_args=(), epi_kinds=(), a_pre=None,
        b_lead=None):
    bshape = b.shape if b_lead is None else b.shape[1:]
    if mode == 'nn':
        (m, kd), nd = a.shape, bshape[1]
    elif mode == 'nt':
        (m, kd), nd = a.shape, bshape[0]
    else:
        (kd, m), nd = a.shape, bshape[1]
    tm = _pick(m, tm, 16) if m % tm else tm
    tn = _pick(nd, tn, 128) if nd % tn else tn
    dims = {'nn': NN, 'nt': NT, 'tn': TN}[mode]
    n_e, n_o = len(epi_args), len(out_dtypes)

    def body(*refs):
        i = pl.program_id(1)
        av = refs[0][...]
        if a_pre is not None:
            av = a_pre(av)
        acc = _dot(av, refs[1][...], dims)
        res = (acc,) if epi is None else epi(acc, i, *[r[...] for r in refs[2:2 + n_e]])
        for r, v in zip(refs[2 + n_e:], res):
            r[...] = v.astype(r.dtype)

    if mode == 'nn':
        specs = [pl.BlockSpec((tm, kd), lambda j, i: (i, 0)), pl.BlockSpec((kd, tn), lambda j, i: (0, j))]
    elif mode == 'nt':
        specs = [pl.BlockSpec((tm, kd), lambda j, i: (i, 0)), pl.BlockSpec((tn, kd), lambda j, i: (j, 0))]
    else:
        specs = [pl.BlockSpec((kd, tm), lambda j, i: (0, i)), pl.BlockSpec((kd, tn), lambda j, i: (0, j))]
    if b_lead is not None:
        shape2, at2 = specs[1].block_shape, specs[1].index_map
        specs[1] = pl.BlockSpec((None,) + tuple(shape2), lambda j, i: (b_lead,) + tuple(at2(j, i)))
    for arr, kind in zip(epi_args, epi_kinds):
        if kind == 'mn':
            specs.append(pl.BlockSpec((tm, tn), lambda j, i: (i, j)))
        elif kind == 'n':
            specs.append(pl.BlockSpec((1, tn), lambda j, i: (0, j)))
        elif kind == 'mt':
            specs.append(pl.BlockSpec((tm, arr.shape[1]), lambda j, i: (i, 0)))
        else:
            specs.append(pl.BlockSpec(arr.shape, lambda j, i, nd_=arr.ndim: (0,) * nd_))
    res = _pcall(body, name=name, grid=(nd // tn, m // tm), in_specs=specs,
                 out_specs=[pl.BlockSpec((tm, tn), lambda j, i: (i, j))] * n_o,
                 out_shape=[jax.ShapeDtypeStruct((m, nd), dt) for dt in out_dtypes],
                 compiler_params=_params(("arbitrary", "arbitrary")))(a, b, *epi_args)
    return res[0] if n_o == 1 else list(res)


def _row_gate(mod, k3, i, tm, n_lat):
    g0 = mod[0, k3:k3 + 1, :]
    if mod.shape[0] == 1:
        return g0
    rid = i * tm + lax.broadcasted_iota(jnp.int32, (tm, 1), 0)
    return jnp.where(rid < n_lat, g0, mod[1, k3:k3 + 1, :])


def _ffn_up(name, u, wg, wu, base, tm):
    r, d = u.shape
    nch, _, _, fc = wg.shape

    def body(u_ref, wg_ref, wu_ref, a_ref, b_ref, h_ref):
        uv = u_ref[...]
        a = _dot(uv, wg_ref[...], NN)
        b = _dot(uv, wu_ref[...], NN)
        a_ref[...] = a.astype(BF)
        b_ref[...] = b.astype(BF)
        h_ref[...] = (_silu(a) * b).astype(BF)

    chunk = pl.BlockSpec((None, tm, fc), lambda j, i: (j, i, 0))
    return _pcall(body, name=name, grid=(nch, r // tm),
                  in_specs=[pl.BlockSpec((tm, d), lambda j, i: (i, 0)),
                            pl.BlockSpec((None, None, d, fc), lambda j, i: (j, base, 0, 0)),
                            pl.BlockSpec((None, None, d, fc), lambda j, i: (j, base, 0, 0))],
                  out_specs=[chunk] * 3, out_shape=[jax.ShapeDtypeStruct((nch, r, fc), BF)] * 3,
                  compiler_params=_params(("arbitrary", "arbitrary")))(u, wg, wu)


def _ffn_down(name, hid, wd, wd_blk, s, mod, k, n_lat, tm):
    nch, r, fc = hid.shape
    d = wd.shape[2]

    def body(h_ref, w_ref, s_ref, m_ref, so_ref, o_ref, acc_ref):
        i, j = pl.program_id(0), pl.program_id(1)
        part = _dot(h_ref[...], w_ref[...], NN)

        @pl.when(j == 0)
        def _():
            acc_ref[...] = part

        @pl.when(j > 0)
        def _():
            acc_ref[...] += part

        @pl.when(j == nch - 1)
        def _():
            o = acc_ref[...]
            o_ref[...] = o
            so_ref[...] = s_ref[...] + 0.5 * _row_gate(m_ref[...], 3 * k + 2, i, tm, n_lat) * o

    row = pl.BlockSpec((tm, d), lambda i, j: (i, 0))
    return _pcall(body, name=name, grid=(r // tm, nch),
                  in_specs=[pl.BlockSpec((None, tm, fc), lambda i, j: (j, i, 0)),
                            pl.BlockSpec((None, fc, d), lambda i, j: (j, wd_blk, 0)), row,
                            pl.BlockSpec(mod.shape, lambda i, j: (0, 0, 0))],
                  out_specs=[row, row], out_shape=[jax.ShapeDtypeStruct((r, d), F32)] * 2,
                  scratch_shapes=[pltpu.VMEM((tm, d), F32)],
                  compiler_params=_params(("arbitrary", "arbitrary")))(hid, wd, s, mod)


def _ffn_dhid(name, d_o, wd, wd_blk, a, b, tm):
    r, d = d_o.shape
    nch, _, fc = a.shape

    def body(g_ref, w_ref, a_ref, b_ref, da_ref, db_ref):
        dh = _dot(g_ref[...], w_ref[...], NT)
        av, bv = a_ref[...].astype(F32), b_ref[...].astype(F32)
        da_ref[...] = (dh * bv * _dsilu(av)).astype(BF)
        db_ref[...] = (dh * _silu(av)).astype(BF)

    chunk = pl.BlockSpec((None, tm, fc), lambda j, i: (j, i, 0))
    return _pcall(body, name=name, grid=(nch, r // tm),
                  in_specs=[pl.BlockSpec((tm, d), lambda j, i: (i, 0)),
                            pl.BlockSpec((None, fc, d), lambda j, i: (j, wd_blk, 0)), chunk, chunk],
                  out_specs=[chunk] * 2, out_shape=[jax.ShapeDtypeStruct((nch, r, fc), BF)] * 2,
                  compiler_params=_params(("arbitrary", "arbitrary")))(d_o, wd, a, b)


def _ffn_du(name, da, db, wg, wu, base, tm):
    nch, r, fc = da.shape
    d = wg.shape[2]

    def body(da_ref, db_ref, wg_ref, wu_ref, o_ref, acc_ref):
        j = pl.program_id(1)
        part = _dot(da_ref[...], wg_ref[...], NT) + _dot(db_ref[...], wu_ref[...], NT)

        @pl.when(j == 0)
        def _():
            acc_ref[...] = part

        @pl.when(j > 0)
        def _():
            acc_ref[...] += part

        @pl.when(j == nch - 1)
        def _():
            o_ref[...] = acc_ref[...]

    chunk = pl.BlockSpec((None, tm, fc), lambda i, j: (j, i, 0))
    return _pcall(body, name=name, grid=(r // tm, nch),
                  in_specs=[chunk, chunk, pl.BlockSpec((None, None, d, fc), lambda i, j: (j, base, 0, 0)),
                            pl.BlockSpec((None, None, d, fc), lambda i, j: (j, base, 0, 0))],
                  out_specs=pl.BlockSpec((tm, d), lambda i, j: (i, 0)),
                  out_shape=jax.ShapeDtypeStruct((r, d), F32), scratch_shapes=[pltpu.VMEM((tm, d), F32)],
                  compiler_params=_params(("arbitrary", "arbitrary")))(da, db, wg, wu)


def _ffn_dw_in(name, u, dz, tmm, grads, idx):
    r, d = u.shape
    nch, _, fc = dz.shape
    nb = d // tmm

    def body(u_ref, z_ref, g_ref, o_ref):
        o_ref[...] = _dot(u_ref[...], z_ref[...], TN).astype(o_ref.dtype)

    return _pcall(body, name=name, grid=(nch, nb),
                  in_specs=[pl.BlockSpec((r, tmm), lambda j, mi: (0, mi)),
                            pl.BlockSpec((None, r, fc), lambda j, mi: (j, 0, 0)),
                            pl.BlockSpec(memory_space=pl.ANY)],
                  out_specs=pl.BlockSpec((None, tmm, fc), lambda j, mi: (j, idx * nb + mi, 0)),
                  out_shape=jax.ShapeDtypeStruct(grads.shape, grads.dtype), input_output_aliases={2: 0},
                  compiler_params=_params(("arbitrary", "arbitrary")))(u, dz, grads)


def _ffn_dw_down(name, hid, d_o, tn, grads, idx):
    nch, r, fc = hid.shape
    d = d_o.shape[1]

    def body(h_ref, g_ref, acc_ref, o_ref):
        o_ref[...] = _dot(h_ref[...], g_ref[...], TN).astype(o_ref.dtype)

    return _pcall(body, name=name, grid=(nch, d // tn),
                  in_specs=[pl.BlockSpec((None, r, fc), lambda j, ni: (j, 0, 0)),
                            pl.BlockSpec((r, tn), lambda j, ni: (0, ni)),
                            pl.BlockSpec(memory_space=pl.ANY)],
                  out_specs=pl.BlockSpec((None, fc, tn), lambda j, ni: (j, idx, ni)),
                  out_shape=jax.ShapeDtypeStruct(grads.shape, grads.dtype), input_output_aliases={2: 0},
                  compiler_params=_params(("arbitrary", "arbitrary")))(hid, d_o, grads)


def _partner(x):
    n = x.shape[1]
    lane = lax.broadcasted_iota(jnp.int32, x.shape, 1)
    return jnp.where((lane & 15) < 8, pltpu.roll(x, n - 8, 1), pltpu.roll(x, 8, 1))


def _rope(x, ct, st):
    reps = x.shape[1] // ct.shape[1]
    if reps > 1:
        ct, st = jnp.tile(ct, (1, reps)), jnp.tile(st, (1, reps))
    return x * ct + _partner(x) * st


def _rope_t(dy, ct, st):
    reps = dy.shape[1] // ct.shape[1]
    if reps > 1:
        ct, st = jnp.tile(ct, (1, reps)), jnp.tile(st, (1, reps))
    return dy * ct + _partner(dy * st)


def _rope_tables(t_len, g_len, lane0):
    half = QK_ROPE // 4
    pos = jnp.arange(t_len)
    row = (pos // GRID_W).astype(F32)
    col = (pos % GRID_W).astype(F32)
    freqs = jnp.power(ROPE_THETA, -jnp.arange(0, QK_ROPE // 2, 2, dtype=F32) / (QK_ROPE // 2))
    ang_r, ang_c = row[:, None] * freqs, col[:, None] * freqs
    cs = jnp.concatenate([jnp.cos(ang_r)] * 2 + [jnp.cos(ang_c)] * 2, axis=1)
    sn = jnp.concatenate([-jnp.sin(ang_r), jnp.sin(ang_r), -jnp.sin(ang_c), jnp.sin(ang_c)], axis=1)
    assert cs.shape[1] == 4 * half == QK_ROPE
    ct = jnp.ones((t_len + g_len, HEAD_PAD), F32).at[:t_len, lane0:lane0 + QK_ROPE].set(cs)
    st = jnp.zeros((t_len + g_len, HEAD_PAD), F32).at[:t_len, lane0:lane0 + QK_ROPE].set(sn)
    return ct, st


def _attn_fwd(name, q, kp, vp, n_q, q_off, n_k, k_blk, heads, tq, scale):
    qb = q_off // tq

    def body(q_ref, k_ref, v_ref, o_ref, l_ref):
        s = _dot(q_ref[...], k_ref[...], NT) * scale
        m = jnp.max(s, axis=1, keepdims=True)
        p = jnp.exp(s - m)
        l = jnp.sum(p, axis=1, keepdims=True)
        o_ref[...] = (_dot(p, v_ref[...], NN) / l).astype(BF)
        l_ref[...] = jnp.broadcast_to(m + jnp.log(l), l_ref.shape)

    hw = heads * HEAD_PAD
    blk = pl.BlockSpec((tq, HEAD_PAD), lambda h, i: (i, h))
    kv = pl.BlockSpec((n_k, HEAD_PAD), lambda h, i: (k_blk, h))
    return _pcall(body, name=name, grid=(heads, n_q // tq),
                  in_specs=[pl.BlockSpec((tq, HEAD_PAD), lambda h, i: (i + qb, h)), kv, kv],
                  out_specs=[blk, blk],
                  out_shape=[jax.ShapeDtypeStruct((n_q, hw), BF), jax.ShapeDtypeStruct((n_q, hw), F32)],
                  compiler_params=_params(("arbitrary", "arbitrary")))(q, kp, vp)


def _attn_bwd(name, q, kp, vp, cat, dcat, lse, n_q, q_off, n_k, k_blk, heads, tq, scale, col_blk):
    qb = q_off // tq

    def body(q_ref, k_ref, v_ref, o_ref, do_ref, l_ref, dq_ref, dk_ref, dv_ref):
        i = pl.program_id(1)
        qv, kv_, vv = q_ref[...], k_ref[...], v_ref[...]
        dov = do_ref[...]
        s = _dot(qv, kv_, NT) * scale
        p = jnp.exp(s - l_ref[...][:, 0:1])
        dp = _dot(dov, vv, NT)
        delta = jnp.sum(dov * o_ref[...].astype(F32), axis=1, keepdims=True)
        ds = (p * (dp - delta) * scale).astype(BF)
        dq_ref[...] = _dot(ds, kv_, NN)
        dk = _dot(ds, qv, TN)
        dv = _dot(p, dov, TN)

        @pl.when(i == 0)
        def _():
            dk_ref[...] = dk
            dv_ref[...] = dv

        @pl.when(i > 0)
        def _():
            dk_ref[...] += dk
            dv_ref[...] += dv

    hw = heads * HEAD_PAD
    qspec = pl.BlockSpec((tq, HEAD_PAD), lambda h, i: (i + qb, h))
    cspec = pl.BlockSpec((tq, HEAD_PAD), lambda h, i: (i + qb, col_blk + h))
    kv = pl.BlockSpec((n_k, HEAD_PAD), lambda h, i: (k_blk, h))
    acc = pl.BlockSpec((n_k, HEAD_PAD), lambda h, i: (0, h))
    blk = pl.BlockSpec((tq, HEAD_PAD), lambda h, i: (i, h))
    return _pcall(body, name=name, grid=(heads, n_q // tq),
                  in_specs=[qspec, kv, kv, cspec, cspec, blk], out_specs=[blk, acc, acc],
                  out_shape=[jax.ShapeDtypeStruct((n_q, hw), F32), jax.ShapeDtypeStruct((n_k, hw), F32),
                             jax.ShapeDtypeStruct((n_k, hw), F32)],
                  compiler_params=_params(("arbitrary", "arbitrary")))(q, kp, vp, cat, dcat, lse)


def _shift(x, k):
    return pltpu.roll(x, k % x.shape[0], 0)


def _window_sum(v, w, mirrored):
    n, gd = v.shape
    pad = jnp.zeros((POOL_PAD, gd), F32)
    e = jnp.concatenate([pad, v, pad], axis=0)
    acc = e + _shift(e, -1 if mirrored else 1)
    step = 1
    while 2 * step < w:
        acc = _shift(acc, step) + _shift(acc, -step)
        step *= 2
    return acc[POOL_PAD:POOL_PAD + n]


def _window_count(n, w):
    t = lax.broadcasted_iota(jnp.int32, (n, 1), 0)
    lo = jnp.maximum(t - w // 2, 0)
    hi = jnp.minimum(t + (w - w // 2 - 1), n - 1)
    return (hi - lo + 1).astype(F32)


def _pool_fwd(name, u, pool_w, scale):
    n, pd = u.shape
    ng = len(POOL_WINDOWS)
    gd = pd // ng

    def body(u_ref, w_ref, s_ref, y_ref):
        for g, w in enumerate(POOL_WINDOWS):
            sl = slice(g * gd, (g + 1) * gd)
            ug = u_ref[:, sl]
            p = _window_sum(ug, w, False) / _window_count(n, w) - ug
            y_ref[:, sl] = (_dot(p, w_ref[g], NN) * s_ref[:, sl]).astype(BF)

    return _pcall(body, name=name, out_shape=jax.ShapeDtypeStruct((n, pd), BF),
                  compiler_params=_params())(u, pool_w, scale)


def _pool_bwd(name, u, dcat, pool_w, scale, row_off):
    n, pd = u.shape
    ng = len(POOL_WINDOWS)
    gd = pd // ng

    def body(u_ref, dy_ref, w_ref, s_ref, du_ref, dw_ref, ds_ref):
        ds_ref[...] = jnp.zeros_like(ds_ref)
        for g, w in enumerate(POOL_WINDOWS):
            sl = slice(g * gd, (g + 1) * gd)
            ug, dy, wg = u_ref[:, sl], dy_ref[:, sl], w_ref[g]
            cnt = _window_count(n, w)
            p = _window_sum(ug, w, False) / cnt - ug
            ds_ref[0:1, sl] = jnp.sum(dy * _dot(p, wg, NN), axis=0, keepdims=True)
            dys = dy * s_ref[:, sl]
            dw_ref[g] = _dot(p, dys, TN)
            dp = _dot(dys, wg, NT)
            du_ref[:, sl] = (_window_sum(dp / cnt, w, True) - dp).astype(BF)

    rb = row_off // n
    return _pcall(body, name=name, grid=(1,),
                  in_specs=[pl.BlockSpec((n, pd), lambda i: (0, 0)), pl.BlockSpec((n, pd), lambda i: (rb, 0)),
                            pl.BlockSpec(pool_w.shape, lambda i: (0, 0, 0)), pl.BlockSpec(scale.shape, lambda i: (0, 0))],
                  out_specs=[pl.BlockSpec((n, pd), lambda i: (0, 0)), pl.BlockSpec((ng, gd, gd), lambda i: (0, 0, 0)),
                             pl.BlockSpec((8, pd), lambda i: (0, 0))],
                  out_shape=[jax.ShapeDtypeStruct((n, pd), BF), jax.ShapeDtypeStruct((ng, gd, gd), F32),
                             jax.ShapeDtypeStruct((8, pd), F32)],
                  compiler_params=_params(("arbitrary",)))(u, dcat, pool_w, scale)


def _edge_shift(z, k):
    n = z.shape[0]
    t = lax.broadcasted_iota(jnp.int32, (n, 1), 0)
    keep = (t >= k) if k > 0 else (t < n + k)
    return jnp.where(keep, pltpu.roll(z, k % n, 0), 0.0)


def _conv_fwd(name, p3, cw, tc):
    n, cd = p3.shape[0], p3.shape[1] // 3
    nb = cd // tc

    def body(b_ref, c_ref, v_ref, w_ref, y_ref):
        z = c_ref[...] * v_ref[...]
        w = w_ref[...]
        zc = w[0:1] * _edge_shift(z, 1) + w[1:2] * z + w[2:3] * _edge_shift(z, -1)
        y_ref[...] = (b_ref[...] * zc).astype(BF)

    return _pcall(body, name=name, grid=(nb,),
                  in_specs=[pl.BlockSpec((n, tc), lambda j: (0, j)), pl.BlockSpec((n, tc), lambda j: (0, nb + j)),
                            pl.BlockSpec((n, tc), lambda j: (0, 2 * nb + j)), pl.BlockSpec((3, tc), lambda j: (0, j))],
                  out_specs=pl.BlockSpec((n, tc), lambda j: (0, j)), out_shape=jax.ShapeDtypeStruct((n, cd), BF),
                  compiler_params=_params(("arbitrary",)))(p3, p3, p3, cw)


def _conv_bwd(name, p3, cw, dy, tc):
    n, cd = dy.shape
    nb = cd // tc

    def body(b_ref, c_ref, v_ref, w_ref, dy_ref, dp_ref, dw_ref):
        cv, vv, w, dyv = c_ref[...], v_ref[...], w_ref[...], dy_ref[...]
        z = cv * vv
        zl, zr = _edge_shift(z, 1), _edge_shift(z, -1)
        zc = w[0:1] * zl + w[1:2] * z + w[2:3] * zr
        dzc = dyv * b_ref[...]
        dz = w[0:1] * _edge_shift(dzc, -1) + w[1:2] * dzc + w[2:3] * _edge_shift(dzc, 1)
        dp_ref[0] = (dyv * zc).astype(BF)
        dp_ref[1] = (dz * vv).astype(BF)
        dp_ref[2] = (dz * cv).astype(BF)
        dw_ref[...] = jnp.zeros_like(dw_ref)
        dw_ref[0:1, :] = jnp.sum(dzc * zl, axis=0, keepdims=True)
        dw_ref[1:2, :] = jnp.sum(dzc * z, axis=0, keepdims=True)
        dw_ref[2:3, :] = jnp.sum(dzc * zr, axis=0, keepdims=True)

    col = pl.BlockSpec((n, tc), lambda j: (0, j))
    return _pcall(body, name=name, grid=(nb,),
                  in_specs=[col, pl.BlockSpec((n, tc), lambda j: (0, nb + j)),
                            pl.BlockSpec((n, tc), lambda j: (0, 2 * nb + j)), pl.BlockSpec((3, tc), lambda j: (0, j)), col],
                  out_specs=[pl.BlockSpec((3, n, tc), lambda j: (0, 0, j)), pl.BlockSpec((8, tc), lambda j: (0, j))],
                  out_shape=[jax.ShapeDtypeStruct((3, n, cd), BF), jax.ShapeDtypeStruct((8, cd), F32)],
                  compiler_params=_params(("arbitrary",)))(p3, p3, p3, cw, dy)


def _conv_din(name, dp3, w_in, tm):
    _, n, cd = dp3.shape
    d = w_in.shape[0]

    def body(a_ref, w_ref, o_ref, acc_ref):
        j = pl.program_id(1)
        part = _dot(a_ref[...], w_ref[...], NT)

        @pl.when(j == 0)
        def _():
            acc_ref[...] = part

        @pl.when(j > 0)
        def _():
            acc_ref[...] += part

        @pl.when(j == 2)
        def _():
            o_ref[...] = acc_ref[...]

    return _pcall(body, name=name, grid=(n // tm, 3),
                  in_specs=[pl.BlockSpec((None, tm, cd), lambda i, j: (j, i, 0)),
                            pl.BlockSpec((d, cd), lambda i, j: (0, j))],
                  out_specs=pl.BlockSpec((tm, d), lambda i, j: (i, 0)), out_shape=jax.ShapeDtypeStruct((n, d), F32),
                  scratch_shapes=[pltpu.VMEM((tm, d), F32)],
                  compiler_params=_params(("arbitrary", "arbitrary")))(dp3, w_in)


def _conv_dw_in(name, u, dp3, tmm, tn):
    n, d = u.shape
    cd = dp3.shape[2]
    nb = cd // tn

    def body(u_ref, z_ref, o_ref):
        o_ref[...] = _dot(u_ref[...], z_ref[...], TN)

    return _pcall(body, name=name, grid=(3 * nb, d // tmm),
                  in_specs=[pl.BlockSpec((n, tmm), lambda j, mi: (0, mi)),
                            pl.BlockSpec((None, n, tn), lambda j, mi: (j // nb, 0, j % nb))],
                  out_specs=pl.BlockSpec((tmm, tn), lambda j, mi: (mi, j)),
                  out_shape=jax.ShapeDtypeStruct((d, 3 * cd), F32),
                  compiler_params=_params(("arbitrary", "arbitrary")))(u, dp3)


def _loss_head(name, h, target, gain, tm):
    d = h.shape[1]

    def fn(rows, consts, m):
        hv, tv = rows
        g = consts[0][0:1, :]
        n, r = _rms(hv)
        err = n * g - tv
        dy = err / d
        loss = 0.5 * jnp.sum(err * err) / d
        acc = {0: jnp.sum(dy * n, axis=0, keepdims=True), 1: jnp.full((1, d), loss, F32)}
        return [_rms_bwd(dy * g, n, r)], acc

    return _rows_call(name, fn, h.shape[0], tm, [h, target], [gain], None, [(d, F32)], acc_w=d)


def _adamw(name, w, g, m, v, after=None):
    shape = w.shape
    if w.ndim == 1:
        shape2 = (1,) + shape
        res = _adamw(name, *[t.reshape(shape2) for t in (w, g, m, v)], after=after)
        return [t.reshape(shape) for t in res]
    lead, (r, cdim) = shape[:-2], shape[-2:]
    tr = r
    if r * cdim * 4 > (3 << 19):
        tr = _pick(r, max(8, (3 << 19) // (cdim * 4)), 8)
    c1 = 1.0 / (1.0 - ADAM_B1 ** ADAM_STEP)
    c2 = 1.0 / (1.0 - ADAM_B2 ** ADAM_STEP)
    nl = len(lead)

    def body(w_ref, g_ref, m_ref, v_ref, *rest):
        d_ref, nm_ref, nv_ref = rest[-3:]
        gv = g_ref[...]
        nm = ADAM_B1 * m_ref[...] + (1.0 - ADAM_B1) * gv
        nv = ADAM_B2 * v_ref[...] + (1.0 - ADAM_B2) * (gv * gv)
        nm_ref[...] = nm
        nv_ref[...] = nv
        d_ref[...] = -ADAM_LR * ((nm * c1) / (jnp.sqrt(nv * c2) + ADAM_EPS) + ADAM_WD * w_ref[...])

    spec = pl.BlockSpec((None,) * nl + (tr, cdim), lambda *idx: idx + (0,))
    extra = [] if after is None else [after]
    res = _pcall(body, name=name, grid=lead + (r // tr,),
                 in_specs=[spec] * 4 + [pl.BlockSpec(memory_space=pl.ANY)] * len(extra), out_specs=[spec] * 3,
                 out_shape=[jax.ShapeDtypeStruct(shape, F32)] * 3,
                 compiler_params=_params(("arbitrary",) * (nl + 1)))(w, g, m, v, *extra)
    return list(res)


def _ffn_half_fwd(tag, s, gains, mod, k, wts, n_lat, tm, h_tiles, tm_big, after=None):
    wg, wu, wd, idx = wts
    u = _adaln_fwd(f"adaln_{tag}", s, gains, k, mod, k, tm, h_tiles, after)
    a, b, hid = _ffn_up(f"ffn_up_{tag}", u, wg, wu, idx, tm)
    s_out, o = _ffn_down(f"ffn_down_{tag}", hid, wd, idx, s, mod, k, n_lat, tm_big)
    return s_out, (s, u, a, b, hid, o)


def _ffn_half_bwd(tag, ds_out, saved, gains, mod, k, wts, big_grads, tm, h_tiles, tm_big):
    wg, wu, wd, idx = wts
    g_gate, g_up, g_down = big_grads
    s, u, a, b, hid, o = saved
    d_o, acc_g = _resid_bwd(f"resid_bwd_{tag}", ds_out, o, mod, k, 0.5, tm, h_tiles)
    da, db = _ffn_dhid(f"ffn_dhid_{tag}", d_o, wd, idx, a, b, tm)
    du = _ffn_du(f"ffn_du_{tag}", da, db, wg, wu, idx, tm_big)
    d = u.shape[1]
    g_gate = _ffn_dw_in(f"ffn_dwg_{tag}", u, da, _pick(d, 256), g_gate, idx)
    g_up = _ffn_dw_in(f"ffn_dwu_{tag}", u, db, _pick(d, 256), g_up, idx)
    g_down = _ffn_dw_down(f"ffn_dwd_{tag}", hid, d_o, _pick(d, 512), g_down, idx)
    ds, acc_n = _adaln_bwd(f"adaln_bwd_{tag}", s, du, ds_out, gains, k, mod, k, tm, h_tiles)
    return ds, (g_gate, g_up, g_down), (acc_n[:, 0], acc_n[:, 1], acc_g[:, 0]), jnp.sum(acc_n[:, 2], axis=0)


def kernel(x, c, ctx, c_ctx, norm_g, w_mod, b_mod, ffn_w_gate, ffn_w_up, ffn_w_down, ab_w_in, pool_w, pool_scale, q_norm_g, w_uq, kv_norm_g, w_ukv, ab_w_out, conv_w_in, conv_w, conv_w_out, final_norm_g, loss_target, m_c_ctx, m_norm_g, m_w_mod, m_b_mod, m_ffn_w_gate, m_ffn_w_up, m_ffn_w_down, m_ab_w_in, m_pool_w, m_pool_scale, m_q_norm_g, m_w_uq, m_kv_norm_g, m_w_ukv, m_ab_w_out, m_conv_w_in, m_conv_w, m_conv_w_out, m_final_norm_g, v_c_ctx, v_norm_g, v_w_mod, v_b_mod, v_ffn_w_gate, v_ffn_w_up, v_ffn_w_down, v_ab_w_in, v_pool_w, v_pool_scale, v_q_norm_g, v_w_uq, v_kv_norm_g, v_w_ukv, v_ab_w_out, v_conv_w_in, v_conv_w, v_conv_w_out, v_final_norm_g):
    weights = dict(c_ctx=c_ctx, norm_g=norm_g, w_mod=w_mod, b_mod=b_mod, ffn_w_gate=ffn_w_gate, ffn_w_up=ffn_w_up,
                   ffn_w_down=ffn_w_down, ab_w_in=ab_w_in, pool_w=pool_w, pool_scale=pool_scale, q_norm_g=q_norm_g,
                   w_uq=w_uq, kv_norm_g=kv_norm_g, w_ukv=w_ukv, ab_w_out=ab_w_out, conv_w_in=conv_w_in, conv_w=conv_w,
                   conv_w_out=conv_w_out, final_norm_g=final_norm_g)
    mom_m = dict(c_ctx=m_c_ctx, norm_g=m_norm_g, w_mod=m_w_mod, b_mod=m_b_mod, ffn_w_gate=m_ffn_w_gate,
                 ffn_w_up=m_ffn_w_up, ffn_w_down=m_ffn_w_down, ab_w_in=m_ab_w_in, pool_w=m_pool_w,
                 pool_scale=m_pool_scale, q_norm_g=m_q_norm_g, w_uq=m_w_uq, kv_norm_g=m_kv_norm_g, w_ukv=m_w_ukv,
                 ab_w_out=m_ab_w_out, conv_w_in=m_conv_w_in, conv_w=m_conv_w, conv_w_out=m_conv_w_out,
                 final_norm_g=m_final_norm_g)
    mom_v = dict(c_ctx=v_c_ctx, norm_g=v_norm_g, w_mod=v_w_mod, b_mod=v_b_mod, ffn_w_gate=v_ffn_w_gate,
                 ffn_w_up=v_ffn_w_up, ffn_w_down=v_ffn_w_down, ab_w_in=v_ab_w_in, pool_w=v_pool_w,
                 pool_scale=v_pool_scale, q_norm_g=v_q_norm_g, w_uq=v_w_uq, kv_norm_g=v_kv_norm_g, w_ukv=v_w_ukv,
                 ab_w_out=v_ab_w_out, conv_w_in=v_conv_w_in, conv_w=v_conv_w, conv_w_out=v_conv_w_out,
                 final_norm_g=v_final_norm_g)

    t_len, d = x.shape[1], x.shape[2]
    g_len = ctx.shape[1]
    r_len = t_len + g_len
    fc = ffn_w_gate.shape[3]
    heads = d // 128
    pool_dim = d // 2
    q_rank, kv_rank = q_norm_g.shape[1], kv_norm_g.shape[1]
    hw = heads * HEAD_PAD
    attn_scale = 1.0 / math.sqrt(QK_NOPE + QK_ROPE)
    kvr_w = kv_rank + HEAD_PAD
    in_w = pool_dim + q_rank + kvr_w
    tm = 256 if g_len % 256 == 0 else g_len
    assert t_len % tm == 0 and g_len % tm == 0 and t_len % g_len == 0 and pool_dim % 128 == 0
    h_tiles = t_len // tm
    tm_l0 = _pick(r_len, 768, tm)
    tm_l1 = _pick(t_len, 1024, tm)

    xi, yi, ci = lax.axis_index("x"), lax.axis_index("y"), lax.axis_index("c")
    me = 4 * xi + 2 * yi + ci
    shard = 2 * xi + yi

    def halves(w):
        return w.astype(BF).reshape(2, -1, w.shape[-1])

    ffn_names = ["ffn_w_gate", "ffn_w_up", "ffn_w_down"]
    mixer_names = [["ab_w_in", "w_uq", "w_ukv", "ab_w_out"], ["conv_w_in", "conv_w_out"]]
    big_names = ffn_names + mixer_names[0] + mixer_names[1]

    def layer_halves(l):
        return [halves(weights[nm][l]) for nm in ffn_names] + [halves(weights[nm]) for nm in mixer_names[l]]

    small = jnp.concatenate([norm_g.reshape(6, -1), conv_w[0]], axis=0)
    small = jnp.pad(small, ((0, 7), (0, 0)))
    c_row = jnp.pad(c, ((0, 7), (0, 0)))
    small_all, c_all = _gather_all("gather_small", [small, c_row])
    small_full = small_all[::2].transpose(1, 0, 2).reshape(16, d)
    gains = [jnp.pad(small_full[3 * l:3 * l + 3], ((0, 5), (0, 0))) for l in range(2)]
    conv_w_full = small_full[6:9]
    c16 = jnp.concatenate([c_all[:, 0], c_ctx[None], jnp.zeros((7, d), F32)], axis=0)

    n_col = w_mod.shape[2]
    b_sh = lax.dynamic_slice_in_dim(b_mod, shard * n_col, n_col, axis=1)
    m_sh = [_mm(f"mod_fwd_{l}", c16, w_mod, 'nn', tm=16, tn=768, a_pre=_silu, b_lead=l,
                epi=lambda acc, i, bv: (acc + bv,), epi_args=(b_sh[l:l + 1],), epi_kinds=('n',)) for l in range(2)]
    m_all = _gather_all("gather_mod", [jnp.concatenate(m_sh, axis=0)])[0]
    m_full = m_all[::2].reshape(N_SHARD, 2, 16, n_col).transpose(1, 2, 0, 3).reshape(2, 16, N_MOD * d)
    mod_h = [jnp.pad(lax.dynamic_index_in_dim(m_full[l], me, 0, keepdims=False).reshape(N_MOD, d), ((0, 7), (0, 0)))
             for l in range(2)]
    mod_g0 = jnp.pad(m_full[0, 8].reshape(N_MOD, d), ((0, 7), (0, 0)))
    mods = [jnp.stack([mod_h[0], mod_g0]), mod_h[1][None]]

    def gather_start(l, dep):
        own = lax.optimization_barrier((tuple(layer_halves(l)), dep))[0]
        n = len(own)
        return _split_start(f"gather_start_l{l}", list(own) + _landing(N_DEV, own), n * len(CHIP_FLIPS),
                            _chips_gather_build(n))

    def forward_start(l, handle, after):
        bufs = _split_wait(f"gather_wait_l{l}", handle, after)
        n = len(bufs) // 2
        return bufs[:n], _split_start(f"forward_start_l{l}", bufs[n:], n * len(CHIP_FLIPS), _sibling_forward_build(n))

    def gathered_layer(l, own, handle, after):
        landed = _split_wait(f"forward_wait_l{l}", handle, after)
        full = [lax.dynamic_update_slice_in_dim(z, a, 2 * shard, 0) for z, a in zip(landed, own)]
        return {nm: g.reshape(N_SHARD, 2 * g.shape[1], g.shape[2])
                for nm, g in zip(ffn_names + mixer_names[l], full)}

    def ffn_weights(gw):
        wg, wu = gw["ffn_w_gate"].reshape(N_SHARD, 2, d, fc), gw["ffn_w_up"].reshape(N_SHARD, 2, d, fc)
        return [(wg, wu, gw["ffn_w_down"], f) for f in range(2)]

    gather0 = gather_start(0, m_all)
    own0, forward0 = forward_start(0, gather0, gather0['token'])
    gather1 = gather_start(1, forward0['token'])
    gw0 = gathered_layer(0, own0, forward0, forward0['token'])
    ffn_w = [ffn_weights(gw0), None]
    w_out_full = gw0["ab_w_out"].reshape(-1, d)
    w_uq_full = gw0["w_uq"].reshape(q_rank, heads * (QK_NOPE + QK_ROPE))
    w_ukv_full = gw0["w_ukv"].transpose(1, 0, 2).reshape(kv_rank, heads * (QK_NOPE + V_HEAD))
    w_in_full = gw0["ab_w_in"].transpose(1, 0, 2).reshape(d, -1)

    wq_p = jnp.pad(w_uq_full.reshape(q_rank, heads, QK_NOPE + QK_ROPE),
                   ((0, 0), (0, 0), (0, HEAD_PAD - QK_NOPE - QK_ROPE))).reshape(q_rank, hw)
    ukv3 = w_ukv_full.reshape(kv_rank, heads, QK_NOPE + V_HEAD)
    wk_top = jnp.pad(ukv3[..., :QK_NOPE], ((0, 0), (0, 0), (0, HEAD_PAD - QK_NOPE))).reshape(kv_rank, hw)
    wv_top = jnp.pad(ukv3[..., QK_NOPE:], ((0, 0), (0, 0), (0, HEAD_PAD - V_HEAD))).reshape(kv_rank, hw)
    spread = jnp.zeros((HEAD_PAD, heads, HEAD_PAD), BF).at[
        jnp.arange(QK_ROPE)[:, None], jnp.arange(heads)[None, :], QK_NOPE + jnp.arange(QK_ROPE)[:, None]].set(1.0)
    wk_ext = jnp.concatenate([wk_top, spread.reshape(HEAD_PAD, hw)], axis=0)
    wv_ext = jnp.concatenate([wv_top, jnp.zeros((HEAD_PAD, hw), BF)], axis=0)
    w_in_pool = w_in_full[:, :pool_dim]
    w_in_q = w_in_full[:, pool_dim:pool_dim + q_rank]
    w_in_kvr = jnp.pad(w_in_full[:, pool_dim + q_rank:], ((0, 0), (0, HEAD_PAD - QK_ROPE)))
    w_out_attn = jnp.pad(w_out_full[pool_dim:].reshape(heads, V_HEAD, d),
                         ((0, 0), (0, HEAD_PAD - V_HEAD), (0, 0))).reshape(hw, d)
    w_out_p = jnp.concatenate([w_out_full[:pool_dim], w_out_attn], axis=0)

    s0 = jnp.concatenate([x[0], ctx[0]], axis=0)
    s1, sav_f00 = _ffn_half_fwd("l0a", s0, gains[0], mods[0], 0, ffn_w[0][0], t_len, tm, h_tiles, tm_l0)
    u_mix = _adaln_fwd("adaln_l0m", s1, gains[0], 1, mods[0], 1, tm, h_tiles)
    p_pool = _mm("in_pool", u_mix, w_in_pool, 'nn', tm=tm, tn=pool_dim)
    p_q = _mm("in_q", u_mix, w_in_q, 'nn', tm=tm, tn=q_rank)
    p_kvr = _mm("in_kvr", u_mix, w_in_kvr, 'nn', tm=tm, tn=kvr_w)
    qg = jnp.pad(q_norm_g, ((0, 7), (0, 0)))
    kvg = jnp.pad(kv_norm_g, ((0, 7), (0, 0)))
    tq_c, tq_s = _rope_tables(t_len, g_len, QK_NOPE)
    tk_c, tk_s = _rope_tables(t_len, g_len, 0)

    def qn_fn(rows, consts, m):
        n, _ = _rms(rows[0])
        return [n * consts[0][0:1, :]], {}

    qn = _rows_call("q_norm", qn_fn, r_len, tm, [p_q], [qg], None, [(q_rank, BF)])[0]
    q_r = _mm("q_up", qn, wq_p, 'nn', tm=tm, tn=hw, out_dtypes=(BF,),
              epi=lambda acc, i, ct, st: (_rope(acc, ct, st),), epi_args=(tq_c, tq_s), epi_kinds=('mt', 'mt'))

    def kvn_fn(rows, consts, m):
        pv, ct, st = rows
        n, _ = _rms(pv[:, :kv_rank])
        return [jnp.concatenate([n * consts[0][0:1, :], _rope(pv[:, kv_rank:], ct, st)], axis=1)], {}

    kvn = _rows_call("kv_norm", kvn_fn, r_len, tm, [p_kvr, tk_c, tk_s], [kvg], None, [(kvr_w, BF)])[0]
    k_p = _mm("k_up", kvn, wk_ext, 'nn', tm=tm, tn=hw, out_dtypes=(BF,))
    v_p = _mm("v_up", kvn, wv_ext, 'nn', tm=tm, tn=hw, out_dtypes=(BF,))
    o_h, lse_h = _attn_fwd("attn_h", q_r, k_p, v_p, t_len, 0, r_len, 0, heads, tm, attn_scale)
    o_g, lse_g = _attn_fwd("attn_g", q_r, k_p, v_p, g_len, t_len, g_len, t_len // g_len, heads, tm, attn_scale)
    y_h = _pool_fwd("pool_h", p_pool[:t_len], pool_w[0], pool_scale)
    y_g = _pool_fwd("pool_g", p_pool[t_len:], pool_w[0], pool_scale)
    cat = jnp.concatenate([jnp.concatenate([y_h, y_g], axis=0), jnp.concatenate([o_h, o_g], axis=0)], axis=1)

    def resid_epi(k3, n_lat, tmr):
        def epi(acc, i, sv, mv):
            return sv + _row_gate(mv, k3, i, tmr, n_lat) * acc, acc
        return epi

    s2, o_mix0 = _mm("mix_out_l0", cat, w_out_p, 'nn', tm=tm, tn=d, out_dtypes=(F32, F32),
                     epi=resid_epi(5, t_len, tm), epi_args=(s1, mods[0]), epi_kinds=('mn', 'w'))
    own1, forward1 = forward_start(1, gather1, s2)
    s3, sav_f01 = _ffn_half_fwd("l0b", s2, gains[0], mods[0], 2, ffn_w[0][1], t_len, tm, h_tiles, tm_l0,
                                forward1['token'])

    gw1 = gathered_layer(1, own1, forward1, s3)
    ffn_w[1] = ffn_weights(gw1)
    cw_out_full = gw1["conv_w_out"].reshape(-1, d)
    cw_in_full = gw1["conv_w_in"].transpose(1, 0, 2).reshape(d, -1)
    tml = 256 if t_len % 256 == 0 else tm
    h3 = s3[:t_len]
    h4, sav_f10 = _ffn_half_fwd("l1a", h3, gains[1], mods[1], 0, ffn_w[1][0], t_len, tml, None, tm_l1)
    u_cv = _adaln_fwd("adaln_l1m", h4, gains[1], 1, mods[1], 1, tml, None)
    p3 = _mm("conv_in", u_cv, cw_in_full, 'nn', tm=tml, tn=512)
    cwp = conv_w_full
    tc = _pick(d, 256)
    y_cv = _conv_fwd("conv_fwd", p3, cwp, tc)
    h5, o_mix1 = _mm("mix_out_l1", y_cv, cw_out_full, 'nn', tm=tml, tn=d, out_dtypes=(F32, F32),
                     epi=resid_epi(5, t_len, tml), epi_args=(h4, mods[1]), epi_kinds=('mn', 'w'))
    h6, sav_f11 = _ffn_half_fwd("l1b", h5, gains[1], mods[1], 2, ffn_w[1][1], t_len, tml, None, tm_l1)

    fg = jnp.pad(final_norm_g[None], ((0, 7), (0, 0)))
    dh6, acc_loss = _loss_head("loss_head", h6, loss_target[0], fg, tml)
    loss = lax.psum(acc_loss[0, 1, 0], ("x", "y", "c"))
    d_final_g = acc_loss[0, 0]

    dgain = [[None] * 3 for _ in range(2)]
    dmod = [[None] * N_MOD for _ in range(2)]

    def put(l, k, triple):
        dmod[l][3 * k], dmod[l][3 * k + 1], dmod[l][3 * k + 2] = triple

    def empty_ffn_grads():
        return (lax.empty((N_SHARD, 2 * d, fc), BF), lax.empty((N_SHARD, 2 * d, fc), BF),
                lax.empty((N_SHARD, 2 * fc, d), BF))

    def by_shard_rows(g):
        return g.reshape(N_SHARD, -1, g.shape[-1])

    def by_shard_cols(g):
        return g.reshape(g.shape[0], N_SHARD, -1).transpose(1, 0, 2)

    def pair_start(l, big):
        send = [b.astype(BF).reshape(N_DEV, b.shape[1] // 2, b.shape[2]) for b in big]
        n = len(send)
        return _split_start(f"grads_pair_start_l{l}", send + _landing(N_SHARD, send), n * N_SHARD,
                            _sibling_halves_build(n))

    def chips_start(l, handle, after):
        bufs = _split_wait(f"grads_pair_wait_l{l}", handle, after)
        n = len(bufs) // 2
        pre = [_add_halves(f"grads_add_l{l}_{nm}", s, z)
               for nm, s, z in zip(ffn_names + mixer_names[l], bufs[:n], bufs[n:])]
        return _split_start(f"grads_start_l{l}", pre + _landing(N_SHARD, pre), n * len(CHIP_FLIPS),
                            _chips_scatter_build(n))

    def landed_sums(l, handle, after):
        bufs = _split_wait(f"grads_wait_l{l}", handle, after)
        n = len(bufs) // 2
        landed = [lax.dynamic_update_slice_in_dim(z, lax.dynamic_slice_in_dim(p, shard, 1, 0), shard, 0)
                  for p, z in zip(bufs[:n], bufs[n:])]
        return [_sum_lead(f"sum_grads_l{l}_{nm}", z) for nm, z in zip(ffn_names + mixer_names[l], landed)]

    ffn_g = empty_ffn_grads()
    dh5, ffn_g, tr, dgain[1][2] = _ffn_half_bwd("l1b", dh6, sav_f11, gains[1], mods[1], 2, ffn_w[1][1], ffn_g,
                                                tml, None, tm_l1)
    put(1, 2, tr)
    d_o1, acc_g1 = _resid_bwd("resid_bwd_l1m", dh5, o_mix1, mods[1], 1, 1.0, tml, None)
    dy_cv = _mm("mix_out_l1_dx", d_o1, cw_out_full, 'nt', tm=tml, tn=d)
    d_cw_out = _mm("mix_out_l1_dw", y_cv, d_o1, 'tn', tm=256, tn=512)
    dp3, d_cw = _conv_bwd("conv_bwd", p3, cwp, dy_cv, tc)
    du_cv = _conv_din("conv_in_dx", dp3, cw_in_full, tml)
    d_cw_in = _conv_dw_in("conv_in_dw", u_cv, dp3, _pick(d, 256), _pick(d, 512))
    dh4, acc_n1 = _adaln_bwd("adaln_bwd_l1m", h4, du_cv, dh5, gains[1], 1, mods[1], 1, tml, None)
    put(1, 1, (acc_n1[:, 0], acc_n1[:, 1], acc_g1[:, 0]))
    dgain[1][1] = acc_n1[0, 2]
    dh3, ffn_g, tr, dgain[1][0] = _ffn_half_bwd("l1a", dh4, sav_f10, gains[1], mods[1], 0, ffn_w[1][0], ffn_g,
                                                tml, None, tm_l1)
    put(1, 0, tr)

    pair1 = pair_start(1, list(ffn_g) + [by_shard_cols(d_cw_in), by_shard_rows(d_cw_out)])

    ds3 = jnp.concatenate([dh3, jnp.zeros((g_len, d), F32)], axis=0) + pair1['token'][0, 0]
    ffn_g = empty_ffn_grads()
    ds2, ffn_g, tr, dgain[0][2] = _ffn_half_bwd("l0b", ds3, sav_f01, gains[0], mods[0], 2, ffn_w[0][1], ffn_g,
                                                tm, h_tiles, tm_l0)
    put(0, 2, tr)
    scatter1 = chips_start(1, pair1, ds2)
    d_o0, acc_g0 = _resid_bwd("resid_bwd_l0m", ds2, o_mix0, mods[0], 1, 1.0, tm, h_tiles, scatter1['token'])
    dcat = _mm("mix_out_l0_dx", d_o0, w_out_p, 'nt', tm=tm, tn=pool_dim + hw)
    d_w_out_p = _mm("mix_out_l0_dw", cat, d_o0, 'tn', tm=256, tn=512)
    col_blk = pool_dim // HEAD_PAD
    dq_h, dk_h, dv_h = _attn_bwd("attn_bwd_h", q_r, k_p, v_p, cat, dcat, lse_h, t_len, 0, r_len, 0, heads, tm,
                                 attn_scale, col_blk)
    dq_g, dk_g, dv_g = _attn_bwd("attn_bwd_g", q_r, k_p, v_p, cat, dcat, lse_g, g_len, t_len, g_len,
                                 t_len // g_len, heads, tm, attn_scale, col_blk)
    dq_all = jnp.concatenate([dq_h, dq_g], axis=0)
    dk_all = dk_h.at[t_len:].add(dk_g)
    dv_all = dv_h.at[t_len:].add(dv_g)
    dkvn = _mm("k_up_dx", dk_all, wk_ext, 'nt', tm=tm, tn=kvr_w)
    dkvn = _mm("v_up_dx", dv_all, wv_ext, 'nt', tm=tm, tn=kvr_w, epi=lambda acc, i, prev: (acc + prev,),
               epi_args=(dkvn,), epi_kinds=('mn',))
    d_wk_ext = _mm("k_up_dw", kvn, dk_all, 'tn', tm=kvr_w, tn=512)
    d_wv_ext = _mm("v_up_dw", kvn, dv_all, 'tn', tm=kvr_w, tn=512)

    def kvn_bwd_fn(rows, consts, m):
        pv, dv_, ct, st = rows
        g = consts[0][0:1, :]
        n, r = _rms(pv[:, :kv_rank])
        dyn = dv_[:, :kv_rank]
        dckv = _rms_bwd(dyn * g, n, r)
        dkr = _rope_t(dv_[:, kv_rank:], ct, st)
        return [jnp.concatenate([dckv, dkr], axis=1)], {0: jnp.sum(dyn * n, axis=0, keepdims=True)}

    dp_kvr, acc_kvg = _rows_call("kv_norm_bwd", kvn_bwd_fn, r_len, tm, [p_kvr, dkvn, tk_c, tk_s], [kvg], None,
                                 [(kvr_w, BF)], acc_w=kv_rank)

    def qrope_bwd_fn(rows, consts, m):
        return [_rope_t(rows[0], rows[1], rows[2])], {}

    dq_pad = _rows_call("q_rope_bwd", qrope_bwd_fn, r_len, tm, [dq_all, tq_c, tq_s], [], None, [(hw, BF)])[0]
    dqn = _mm("q_up_dx", dq_pad, wq_p, 'nt', tm=tm, tn=q_rank)
    d_wq_p = _mm("q_up_dw", qn, dq_pad, 'tn', tm=256, tn=512)

    def qn_bwd_fn(rows, consts, m):
        pv, dv_ = rows
        g = consts[0][0:1, :]
        n, r = _rms(pv)
        return [_rms_bwd(dv_ * g, n, r)], {0: jnp.sum(dv_ * n, axis=0, keepdims=True)}

    dp_q, acc_qg = _rows_call("q_norm_bwd", qn_bwd_fn, r_len, tm, [p_q, dqn], [qg], None, [(q_rank, BF)],
                              acc_w=q_rank)
    dpu_h, dpw_h, dps_h = _pool_bwd("pool_bwd_h", p_pool[:t_len], dcat, pool_w[0], pool_scale, 0)
    dpu_g, dpw_g, dps_g = _pool_bwd("pool_bwd_g", p_pool[t_len:], dcat, pool_w[0], pool_scale, t_len)
    dp_pool = jnp.concatenate([dpu_h, dpu_g], axis=0)
    add_prev = lambda acc, i, prev: (acc + prev,)
    du_mix = _mm("in_pool_dx", dp_pool, w_in_pool, 'nt', tm=tm, tn=d)
    du_mix = _mm("in_q_dx", dp_q, w_in_q, 'nt', tm=tm, tn=d, epi=add_prev, epi_args=(du_mix,), epi_kinds=('mn',))
    du_mix = _mm("in_kvr_dx", dp_kvr, w_in_kvr, 'nt', tm=tm, tn=d, epi=add_prev, epi_args=(du_mix,), epi_kinds=('mn',))
    d_w_in = jnp.concatenate([
        _mm("in_pool_dw", u_mix, dp_pool, 'tn', tm=256, tn=pool_dim),
        _mm("in_q_dw", u_mix, dp_q, 'tn', tm=256, tn=q_rank),
        _mm("in_kvr_dw", u_mix, dp_kvr, 'tn', tm=256, tn=kvr_w)[:, :kv_rank + QK_ROPE]], axis=1)
    ds1, acc_n0 = _adaln_bwd("adaln_bwd_l0m", s1, du_mix, ds2, gains[0], 1, mods[0], 1, tm, h_tiles)
    put(0, 1, (acc_n0[:, 0], acc_n0[:, 1], acc_g0[:, 0]))
    dgain[0][1] = jnp.sum(acc_n0[:, 2], axis=0)
    ds0, ffn_g, tr, dgain[0][0] = _ffn_half_bwd("l0a", ds1, sav_f00, gains[0], mods[0], 0, ffn_w[0][0], ffn_g,
                                                tm, h_tiles, tm_l0)
    put(0, 0, tr)
    grad_x = ds0[:t_len][None]

    d_w_uq = d_wq_p.reshape(q_rank, heads, HEAD_PAD)[..., :QK_NOPE + QK_ROPE].reshape(q_rank, -1)
    d_w_ukv = jnp.concatenate([d_wk_ext[:kv_rank].reshape(kv_rank, heads, HEAD_PAD)[..., :QK_NOPE],
                               d_wv_ext[:kv_rank].reshape(kv_rank, heads, HEAD_PAD)[..., :V_HEAD]],
                              axis=-1).reshape(kv_rank, -1)
    d_w_out = jnp.concatenate([d_w_out_p[:pool_dim],
                               d_w_out_p[pool_dim:].reshape(heads, HEAD_PAD, d)[:, :V_HEAD].reshape(-1, d)], axis=0)

    sums1 = landed_sums(1, scatter1, ds0)

    pair0 = pair_start(0, list(ffn_g) + [by_shard_cols(d_w_in), by_shard_rows(d_w_uq), by_shard_cols(d_w_ukv),
                                         by_shard_rows(d_w_out)])

    dmh = jnp.stack([jnp.stack([dmod[l][k][0] for k in range(N_MOD)]) for l in range(2)])
    dmg0 = jnp.stack([dmod[0][k][1] for k in range(N_MOD)])
    dg_rows = jnp.stack([dgain[l][k] for l in range(2) for k in range(3)])
    pieces = [dmh.reshape(2 * N_MOD, d), dmg0, dg_rows, d_cw[:3], d_final_g[None],
              (dpw_h + dpw_g).reshape(-1, d), jnp.pad((dps_h + dps_g)[0], (0, d - pool_dim))[None],
              jnp.pad(acc_qg[0, 0], (0, d - q_rank))[None], jnp.pad(acc_kvg[0, 0], (0, d - kv_rank))[None]]
    n_piece = [p.shape[0] for p in pieces]
    pieces = [jnp.pad(p, ((0, (-p.shape[0]) % 8), (0, 0))) for p in pieces]
    small_g = jnp.concatenate(pieces, axis=0) + pair0['token'][0, 0]
    sg_all = _gather_all("gather_small_grads", [small_g])[0]
    sg_sum = _sum_lead("sum_small_grads", sg_all)
    offs = [0]
    for p in pieces:
        offs.append(offs[-1] + p.shape[0])
    part = lambda j: sg_sum[offs[j]:offs[j] + n_piece[j]]
    sum_dmh, sum_dmg0, g_norm_full, g_conv_w_full = part(0).reshape(2, N_MOD * d), part(1).reshape(N_MOD * d), part(2), part(3)
    g_final = part(4)[0]
    g_pool_w = part(5).reshape(pool_w.shape)
    g_pool_scale = part(6)[:, :pool_dim]
    g_q_norm = part(7)[:, :q_rank]
    g_kv_norm = part(8)[:, :kv_rank]
    col0 = shard * (d // N_SHARD)
    g_norm_g = lax.dynamic_slice_in_dim(g_norm_full.reshape(2, 3, d), col0, d // N_SHARD, axis=2)
    g_conv_w = lax.dynamic_slice_in_dim(g_conv_w_full, col0, d // N_SHARD, axis=1)[None]
    g_b_mod = _sum_lead("sum_b_mod", jnp.stack([sum_dmh, jnp.stack([sum_dmg0, jnp.zeros_like(sum_dmg0)])]))

    dm16 = []
    for l in range(2):
        per_dev = sg_all[:, l * N_MOD:(l + 1) * N_MOD].reshape(N_DEV, N_MOD * d)
        row8 = sum_dmg0 if l == 0 else jnp.zeros_like(sum_dmg0)
        full = jnp.concatenate([per_dev, row8[None], jnp.zeros((7, N_MOD * d), F32)], axis=0)
        dm16.append(lax.dynamic_slice_in_dim(full, shard * n_col, n_col, axis=1))
    g_w_mod = jnp.stack([_mm(f"mod_dw_{l}", c16, dm16[l], 'tn', tm=256, tn=768, a_pre=_silu) for l in range(2)])
    dc16 = _mm("mod_dx", dm16[0], w_mod, 'nt', tm=16, tn=512, b_lead=0, epi=lambda acc, i, cv: (acc * _dsilu(cv),),
               epi_args=(c16,), epi_kinds=('mn',))
    dc_all = _gather_all("gather_dc", [dc16])[0]
    g_c_ctx = _sum_lead("sum_dc", dc_all[::2])[8]

    grads = dict(c_ctx=g_c_ctx, norm_g=g_norm_g, w_mod=g_w_mod, b_mod=g_b_mod, pool_w=g_pool_w,
                 pool_scale=g_pool_scale, q_norm_g=g_q_norm, kv_norm_g=g_kv_norm, conv_w=g_conv_w, final_norm_g=g_final)
    names = list(weights)

    scatter0 = chips_start(0, pair0, g_c_ctx)
    upd = {n: _adamw(f"adamw_{n}", weights[n], grads[n].reshape(weights[n].shape), mom_m[n], mom_v[n],
                     scatter0['token']) for n in names if n not in big_names}
    sums0 = landed_sums(0, scatter0, upd["w_mod"][0])

    halves_sum = sums0 + sums1
    swap = _split_start("swap_start", halves_sum + [pltpu.with_memory_space_constraint(lax.empty(s.shape, s.dtype),
                                                                                        pltpu.HBM) for s in halves_sum],
                        len(halves_sum), _sibling_whole_build(len(halves_sum)))
    both = _split_wait("swap_wait", swap, swap['token'])
    south = ci == 0
    swapped = [jnp.where(south, jnp.stack([a, g]), jnp.stack([g, a]))
               for a, g in zip(both[:len(halves_sum)], both[len(halves_sum):])]
    n0 = len(sums0)
    for l, part_l in enumerate((swapped[:n0], swapped[n0:])):
        for nm, s in zip(ffn_names + mixer_names[l], part_l):
            grads[(nm, l)] = s
    for nm in ffn_names:
        grads[nm] = jnp.stack([grads.pop((nm, l)).reshape(weights[nm].shape[1:]) for l in range(2)])
    for l in range(2):
        for nm in mixer_names[l]:
            grads[nm] = grads.pop((nm, l)).reshape(weights[nm].shape)
    upd.update({n: _adamw(f"adamw_{n}", weights[n], grads[n], mom_m[n], mom_v[n]) for n in big_names})
    return (loss, grad_x, *[grads[n].reshape(weights[n].shape) for n in names], *[upd[n][0] for n in names],
            *[upd[n][1] for n in names], *[upd[n][2] for n in names])
```

```python
import functools
import math

import jax
import jax.numpy as jnp
from jax import lax
from jax.experimental import pallas as pl
from jax.experimental.pallas import tpu as pltpu

F32 = jnp.float32
BF = jnp.bfloat16
MESH = pl.DeviceIdType.MESH

N_DEV = 8
N_SHARD = 4
RMS_EPS = 1e-6
N_MOD = 9
POOL_WINDOWS = (2, 4, 8, 16)
QK_NOPE = 64
QK_ROPE = 32
V_HEAD = 64
HEAD_PAD = 128
GRID_W = 64
ROPE_THETA = 10000.0
POOL_PAD = 16
ADAM_LR, ADAM_B1, ADAM_B2, ADAM_EPS, ADAM_WD, ADAM_STEP = 0.001, 0.9, 0.999, 1e-08, 0.01, 10
VMEM_LIMIT = 56 * 1024 * 1024


def _pcall(body, **kw):
    return pl.pallas_call(body, **kw)


def _params(sem=None):
    return pltpu.CompilerParams(dimension_semantics=sem, vmem_limit_bytes=VMEM_LIMIT)


def _pick(n, pref, mult=128):
    best = None
    d = mult
    while d <= min(n, pref):
        if n % d == 0:
            best = d
        d += mult
    return best if best is not None else n


def _silu(z):
    return z * jax.nn.sigmoid(z)


def _dsilu(z):
    s = jax.nn.sigmoid(z)
    return s * (1.0 + z * (1.0 - s))


def _dot(a, b, dims):
    return lax.dot_general(a.astype(BF), b.astype(BF), (dims, ((), ())), preferred_element_type=F32)


NN = ((1,), (0,))
NT = ((1,), (1,))
TN = ((0,), (0,))


ALL_FLIPS = [(kx, ky, kc) for kx in (0, 1) for ky in (0, 1) for kc in (0, 1) if (kx, ky, kc) != (0, 0, 0)]
CHIP_FLIPS = [(1, 0, 0), (0, 1, 0), (1, 1, 0)]
SIBLING = (0, 0, 1)
COMM_SPLIT = 8
SPLIT_MIN_ROWS = 256


def _exchange(name, arrays, plan, lead, whole_src, split=COMM_SPLIT):
    n = len(arrays)
    blk_shapes = [tuple(a.shape) if whole_src else tuple(a.shape[1:]) for a in arrays]
    splits = []
    for shp in blk_shapes:
        s = 1
        while s * 2 <= split and shp[0] % (s * 2) == 0 and (shp[0] // (s * 2)) % 16 == 0 \
                and shp[0] // (s * 2) >= SPLIT_MIN_ROWS:
            s *= 2
        splits.append(s)
    items = plan(0, 0, 0)
    n_items = len(items)
    remote_ids = [k for k, it in enumerate(items) if it[0] is not None]
    local_ids = [k for k, it in enumerate(items) if it[0] is None]
    slots = [(a, s) for s in range(max(splits)) for a in range(n) if s < splits[a]]
    n_slot = len(slots)

    def body(*refs):
        ins, outs = refs[:n], refs[n:2 * n]
        send_sems, recv_sems, loc_sems = refs[2 * n:]
        x, y, c = lax.axis_index("x"), lax.axis_index("y"), lax.axis_index("c")
        plan_here = plan(x, y, c)

        def rows(ref, a, s):
            rc = blk_shapes[a][0] // splits[a]
            return ref.at[pl.ds(s * rc, rc)]

        def make(si, k):
            a, s = slots[si]
            flip, src, dst, _ = plan_here[k]
            base = outs[a] if src[0] == 'out' else ins[a]
            src_ref = rows(base if src[1] is None else base.at[src[1]], a, s)
            dst_ref = rows(outs[a].at[dst], a, s)
            if flip is None:
                return pltpu.make_async_copy(src_ref, dst_ref, loc_sems.at[si * max(1, len(local_ids)) + local_ids.index(k)])
            peer = (1 - x if flip[0] else x, 1 - y if flip[1] else y, 1 - c if flip[2] else c)
            sem = si * len(remote_ids) + remote_ids.index(k)
            return pltpu.make_async_remote_copy(src_ref=src_ref, dst_ref=dst_ref, send_sem=send_sems.at[sem],
                                                recv_sem=recv_sems.at[sem], device_id=peer, device_id_type=MESH)

        copies = {}
        for si in range(n_slot):
            for k in range(n_items):
                if plan_here[k][3] is None:
                    copies[si, k] = make(si, k)
                    copies[si, k].start()
        arrived = set()
        for si in range(n_slot):
            for k in range(n_items):
                after = plan_here[k][3]
                if after is not None:
                    if (si, after) not in arrived:
                        copies[si, after].wait_recv()
                        arrived.add((si, after))
                    copies[si, k] = make(si, k)
                    copies[si, k].start()
        for (si, k), cp in copies.items():
            if plan_here[k][0] is None:
                cp.wait()
            else:
                cp.wait_send()
                if (si, k) not in arrived:
                    cp.wait_recv()

    any_spec = pl.BlockSpec(memory_space=pl.ANY)
    n_rem = max(1, n_slot * len(remote_ids))
    outs = _pcall(
        body, name=name,
        out_shape=[jax.ShapeDtypeStruct((lead,) + s, a.dtype) for s, a in zip(blk_shapes, arrays)],
        in_specs=[any_spec] * n, out_specs=[any_spec] * n,
        scratch_shapes=[pltpu.SemaphoreType.DMA((n_rem,)), pltpu.SemaphoreType.DMA((n_rem,)),
                        pltpu.SemaphoreType.DMA((max(1, n_slot * len(local_ids)),))],
    )(*arrays)
    return list(outs)


def _place(x, y, c):
    return 4 * x + 2 * y + c


def _flip(v, f):
    return 1 - v if f else v


def _gather_all(name, arrays):
    def plan(x, y, c):
        me = _place(x, y, c)
        return [(None, ('in', None), me, None)] + [(f, ('in', None), me, None) for f in ALL_FLIPS]
    return _exchange(name, arrays, plan, N_DEV, True)


HBM_SPEC = pl.BlockSpec(memory_space=pltpu.HBM)
SEM_SPEC = pl.BlockSpec(memory_space=pltpu.SEMAPHORE)
SIDE_EFFECT = pltpu.SideEffectType.DATAFLOW_SIDE_EFFECTING


def _split_start(name, bufs, n_copies, build):
    n = len(bufs)

    def body(*refs):
        for cp in build(refs[:n], refs[n], refs[n + 1]):
            cp.start()
        token = refs[-1]
        token[...] = jnp.zeros_like(token)

    res = _pcall(
        body, name=name,
        out_shape=(pltpu.SemaphoreType.DMA((n_copies,)), pltpu.SemaphoreType.DMA((n_copies,)),
                   *[pltpu.HBM(b.shape, b.dtype) for b in bufs], jax.ShapeDtypeStruct((8, 128), F32)),
        in_specs=[HBM_SPEC] * n,
        out_specs=(SEM_SPEC, SEM_SPEC, *[HBM_SPEC] * n, pl.BlockSpec(memory_space=pltpu.VMEM)),
        input_output_aliases={i: 2 + i for i in range(n)},
        compiler_params=pltpu.CompilerParams(has_side_effects=SIDE_EFFECT),
    )(*[pltpu.with_memory_space_constraint(b, pltpu.HBM) for b in bufs])
    return dict(send=res[0], recv=res[1], bufs=list(res[2:2 + n]), token=res[-1], build=build)


def _split_wait(name, handle, after):
    n = len(handle['bufs'])
    build = handle['build']

    def body(*refs):
        for cp in build(refs[:n], refs[n], refs[n + 1]):
            cp.wait_send()
            cp.wait_recv()

    res = _pcall(
        body, name=name, out_shape=tuple(pltpu.HBM(b.shape, b.dtype) for b in handle['bufs']),
        in_specs=[HBM_SPEC] * n + [SEM_SPEC, SEM_SPEC, pl.BlockSpec(memory_space=pl.ANY)],
        out_specs=tuple([HBM_SPEC] * n), input_output_aliases={i: i for i in range(n)},
        compiler_params=pltpu.CompilerParams(has_side_effects=SIDE_EFFECT),
    )(*handle['bufs'], handle['send'], handle['recv'], after)
    return list(res)


def _landing(lead, arrays):
    return [pltpu.with_memory_space_constraint(lax.empty((lead,) + tuple(a.shape[1:]), a.dtype), pltpu.HBM)
            for a in arrays]


def _copy_list(n, per_array, make):
    def build(refs, send_sems, recv_sems):
        copies = []
        for a in range(n):
            for j in range(per_array):
                src, dst, peer = make(refs, a, j)
                k = a * per_array + j
                copies.append(pltpu.make_async_remote_copy(src_ref=src, dst_ref=dst, send_sem=send_sems.at[k],
                                                           recv_sem=recv_sems.at[k], device_id=peer,
                                                           device_id_type=MESH))
        return copies
    return build


def _mesh_place():
    x, y, c = lax.axis_index("x"), lax.axis_index("y"), lax.axis_index("c")
    return x, y, c, 2 * x + y


def _chips_gather_build(n):
    def make(refs, a, j):
        x, y, c, chip = _mesh_place()
        px, py = _flip(x, CHIP_FLIPS[j][0]), _flip(y, CHIP_FLIPS[j][1])
        return refs[a].at[c], refs[n + a].at[2 * chip + c], (px, py, c)
    return _copy_list(n, len(CHIP_FLIPS), make)


def _chips_scatter_build(n):
    def make(refs, a, j):
        x, y, c, chip = _mesh_place()
        px, py = _flip(x, CHIP_FLIPS[j][0]), _flip(y, CHIP_FLIPS[j][1])
        return refs[a].at[2 * px + py], refs[n + a].at[chip], (px, py, c)
    return _copy_list(n, len(CHIP_FLIPS), make)


def _sibling_forward_build(n):
    def make(refs, a, j):
        x, y, c, _ = _mesh_place()
        blk = 2 * (2 * _flip(x, CHIP_FLIPS[j][0]) + _flip(y, CHIP_FLIPS[j][1])) + c
        return refs[a].at[blk], refs[a].at[blk], (x, y, 1 - c)
    return _copy_list(n, len(CHIP_FLIPS), make)


def _sibling_halves_build(n):
    def make(refs, a, j):
        x, y, c, _ = _mesh_place()
        return refs[a].at[2 * j + 1 - c], refs[n + a].at[j], (x, y, 1 - c)
    return _copy_list(n, N_SHARD, make)


def _sibling_whole_build(n):
    def make(refs, a, j):
        x, y, c, _ = _mesh_place()
        return refs[a], refs[n + a], (x, y, 1 - c)
    return _copy_list(n, 1, make)


def _add_halves(name, send, land):
    _, r, cdim = send.shape
    tr = _pick(r, max(16, (1 << 20) // (cdim * 2)), 16)

    def body(c_ref, own_ref, got_ref, o_ref):
        o_ref[...] = (own_ref[...].astype(F32) + got_ref[...].astype(F32)).astype(BF)

    grid_spec = pltpu.PrefetchScalarGridSpec(
        num_scalar_prefetch=1, grid=(N_SHARD, r // tr),
        in_specs=[pl.BlockSpec((None, tr, cdim), lambda sh, i, cr: (2 * sh + cr[0], i, 0)),
                  pl.BlockSpec((None, tr, cdim), lambda sh, i, cr: (sh, i, 0))],
        out_specs=pl.BlockSpec((None, tr, cdim), lambda sh, i, cr: (sh, i, 0)))
    core = lax.axis_index("c").astype(jnp.int32).reshape(1)
    return _pcall(body, name=name, grid_spec=grid_spec, out_shape=jax.ShapeDtypeStruct((N_SHARD, r, cdim), BF),
                  compiler_params=_params(("arbitrary", "arbitrary")))(core, send, land)


def _sum_lead(name, arr, out_dtype=F32):
    n, r, cdim = arr.shape
    tr = r
    limit = (4 << 20) // (n * cdim * arr.dtype.itemsize)
    if r > limit:
        tr = _pick(r, max(limit, 16), 16)

    def body(x_ref, o_ref):
        acc = x_ref[0].astype(F32)
        for d in range(1, n):
            acc = acc + x_ref[d].astype(F32)
        o_ref[...] = acc.astype(out_dtype)

    return _pcall(body, name=name, grid=(r // tr,),
                  in_specs=[pl.BlockSpec((n, tr, cdim), lambda i: (0, i, 0))],
                  out_specs=pl.BlockSpec((tr, cdim), lambda i: (i, 0)),
                  out_shape=jax.ShapeDtypeStruct((r, cdim), out_dtype),
                  compiler_params=_params(("arbitrary",)))(arr)


def _rows_call(name, fn, n_rows, tm, rows, consts, mod, outs, acc_w=None, h_tiles=None):
    nt = n_rows // tm
    ht = nt if h_tiles is None else h_tiles
    ng = 1 if mod is None else mod.shape[0]
    n_r, n_c, n_o = len(rows), len(consts), len(outs)
    has_mod = mod is not None

    def body(*refs):
        i = pl.program_id(0)
        first = (i % ht) == 0
        row_refs, const_refs = refs[:n_r], refs[n_r:n_r + n_c]
        p = n_r + n_c
        mod_tile = refs[p][...] if has_mod else None
        p += int(has_mod)
        out_refs = refs[p:p + n_o]
        o, acc = fn([r[...] for r in row_refs], [r[...] for r in const_refs], mod_tile)
        for r, v in zip(out_refs, o):
            r[...] = v.astype(r.dtype)
        if acc_w is not None:
            acc_ref = refs[p + n_o]

            @pl.when(first)
            def _():
                acc_ref[...] = jnp.zeros_like(acc_ref)

            for k, v in acc.items():
                acc_ref[k:k + 1, :] += v

    in_specs = [pl.BlockSpec((tm, r.shape[1]), lambda i: (i, 0)) for r in rows]
    in_specs += [pl.BlockSpec(cst.shape, lambda i, nd=cst.ndim: (0,) * nd) for cst in consts]
    args = list(rows) + list(consts)
    if has_mod:
        in_specs.append(pl.BlockSpec((None,) + mod.shape[1:], lambda i: (i // ht, 0, 0)))
        args.append(mod)
    out_shape = [jax.ShapeDtypeStruct((n_rows, w), dt) for w, dt in outs]
    out_specs = [pl.BlockSpec((tm, w), lambda i: (i, 0)) for w, _ in outs]
    if acc_w is not None:
        out_shape.append(jax.ShapeDtypeStruct((ng, 8, acc_w), F32))
        out_specs.append(pl.BlockSpec((None, 8, acc_w), lambda i: (i // ht, 0, 0)))
    res = _pcall(body, name=name, grid=(nt,), in_specs=in_specs, out_specs=out_specs, out_shape=out_shape,
                 compiler_params=_params(("arbitrary",)))(*args)
    return list(res)


def _rms(s):
    r = lax.rsqrt(jnp.mean(s * s, axis=1, keepdims=True) + RMS_EPS)
    return s * r, r


def _rms_bwd(dn, n, r):
    return r * (dn - n * jnp.mean(dn * n, axis=1, keepdims=True))


def _adaln_fwd(name, s, gains, gain_row, mod, k, tm, h_tiles, after=None):
    def fn(rows, consts, m):
        n, _ = _rms(rows[0])
        y = n * consts[0][gain_row:gain_row + 1, :]
        return [y * (1.0 + m[3 * k + 1:3 * k + 2, :]) + m[3 * k:3 * k + 1, :]], {}

    d = s.shape[1]
    consts = [gains] if after is None else [gains, after]
    return _rows_call(name, fn, s.shape[0], tm, [s], consts, mod, [(d, BF)], h_tiles=h_tiles)[0]


def _adaln_bwd(name, s, du, ds_res, gains, gain_row, mod, k, tm, h_tiles):
    def fn(rows, consts, m):
        sv, duv, res = rows
        gain = consts[0][gain_row:gain_row + 1, :]
        n, r = _rms(sv)
        y = n * gain
        dy = duv * (1.0 + m[3 * k + 1:3 * k + 2, :])
        acc = {0: jnp.sum(duv, axis=0, keepdims=True), 1: jnp.sum(duv * y, axis=0, keepdims=True),
               2: jnp.sum(dy * n, axis=0, keepdims=True)}
        return [_rms_bwd(dy * gain, n, r) + res], acc

    d = s.shape[1]
    return _rows_call(name, fn, s.shape[0], tm, [s, du, ds_res], [gains], mod, [(d, F32)], acc_w=d, h_tiles=h_tiles)


def _resid_bwd(name, ds_out, o, mod, k, cst, tm, h_tiles, after=None):
    def fn(rows, consts, m):
        dsv, ov = rows
        gate = m[3 * k + 2:3 * k + 3, :]
        return [cst * gate * dsv], {0: jnp.sum(cst * ov * dsv, axis=0, keepdims=True)}

    d = o.shape[1]
    consts = [] if after is None else [after]
    return _rows_call(name, fn, o.shape[0], tm, [ds_out, o], consts, mod, [(d, BF)], acc_w=d, h_tiles=h_tiles)


def _mm(name, a, b, mode, tm=256, tn=512, out_dtypes=(F32,), epi=None, epi_args=(), epi_kinds=(), a_pre=None,
        b_lead=None):
    bshape = b.shape if b_lead is None else b.shape[1:]
    if mode == 'nn':
        (m, kd), nd = a.shape, bshape[1]
    elif mode == 'nt':
        (m, kd), nd = a.shape, bshape[0]
    else:
        (kd, m), nd = a.shape, bshape[1]
    tm = _pick(m, tm, 16) if m % tm else tm
    tn = _pick(nd, tn, 128) if nd % tn else tn
    dims = {'nn': NN, 'nt': NT, 'tn': TN}[mode]
    n_e, n_o = len(epi_args), len(out_dtypes)

    def body(*refs):
        i = pl.program_id(1)
        av = refs[0][...]
        if a_pre is not None:
            av = a_pre(av)
        acc = _dot(av, refs[1][...], dims)
        res = (acc,) if epi is None else epi(acc, i, *[r[...] for r in refs[2:2 + n_e]])
        for r, v in zip(refs[2 + n_e:], res):
            r[...] = v.astype(r.dtype)

    if mode == 'nn':
        specs = [pl.BlockSpec((tm, kd), lambda j, i: (i, 0)), pl.BlockSpec((kd, tn), lambda j, i: (0, j))]
    elif mode == 'nt':
        specs = [pl.BlockSpec((tm, kd), lambda j, i: (i, 0)), pl.BlockSpec((tn, kd), lambda j, i: (j, 0))]
    else:
        specs = [pl.BlockSpec((kd, tm), lambda j, i: (0, i)), pl.BlockSpec((kd, tn), lambda j, i: (0, j))]
    if b_lead is not None:
        shape2, at2 = specs[1].block_shape, specs[1].index_map
        specs[1] = pl.BlockSpec((None,) + tuple(shape2), lambda j, i: (b_lead,) + tuple(at2(j, i)))
    for arr, kind in zip(epi_args, epi_kinds):
        if kind == 'mn':
            specs.append(pl.BlockSpec((tm, tn), lambda j, i: (i, j)))
        elif kind == 'n':
            specs.append(pl.BlockSpec((1, tn), lambda j, i: (0, j)))
        elif kind == 'mt':
            specs.append(pl.BlockSpec((tm, arr.shape[1]), lambda j, i: (i, 0)))
        else:
            specs.append(pl.BlockSpec(arr.shape, lambda j, i, nd_=arr.ndim: (0,) * nd_))
    res = _pcall(body, name=name, grid=(nd // tn, m // tm), in_specs=specs,
                 out_specs=[pl.BlockSpec((tm, tn), lambda j, i: (i, j))] * n_o,
                 out_shape=[jax.ShapeDtypeStruct((m, nd), dt) for dt in out_dtypes],
                 compiler_params=_params(("arbitrary", "arbitrary")))(a, b, *epi_args)
    return res[0] if n_o == 1 else list(res)


def _row_gate(mod, k3, i, tm, n_lat):
    g0 = mod[0, k3:k3 + 1, :]
    if mod.shape[0] == 1:
        return g0
    rid = i * tm + lax.broadcasted_iota(jnp.int32, (tm, 1), 0)
    return jnp.where(rid < n_lat, g0, mod[1, k3:k3 + 1, :])


def _ffn_up(name, u, wg, wu, base, tm):
    r, d = u.shape
    nch, _, _, fc = wg.shape

    def body(u_ref, wg_ref, wu_ref, a_ref, b_ref, h_ref):
        uv = u_ref[...]
        a = _dot(uv, wg_ref[...], NN)
        b = _dot(uv, wu_ref[...], NN)
        a_ref[...] = a.astype(BF)
        b_ref[...] = b.astype(BF)
        h_ref[...] = (_silu(a) * b).astype(BF)

    chunk = pl.BlockSpec((None, tm, fc), lambda j, i: (j, i, 0))
    return _pcall(body, name=name, grid=(nch, r // tm),
                  in_specs=[pl.BlockSpec((tm, d), lambda j, i: (i, 0)),
                            pl.BlockSpec((None, None, d, fc), lambda j, i: (j, base, 0, 0)),
                            pl.BlockSpec((None, None, d, fc), lambda j, i: (j, base, 0, 0))],
                  out_specs=[chunk] * 3, out_shape=[jax.ShapeDtypeStruct((nch, r, fc), BF)] * 3,
                  compiler_params=_params(("arbitrary", "arbitrary")))(u, wg, wu)


def _ffn_down(name, hid, wd, wd_blk, s, mod, k, n_lat, tm):
    nch, r, fc = hid.shape
    d = wd.shape[2]

    def body(h_ref, w_ref, s_ref, m_ref, so_ref, o_ref, acc_ref):
        i, j = pl.program_id(0), pl.program_id(1)
        part = _dot(h_ref[...], w_ref[...], NN)

        @pl.when(j == 0)
        def _():
            acc_ref[...] = part

        @pl.when(j > 0)
        def _():
            acc_ref[...] += part

        @pl.when(j == nch - 1)
        def _():
            o = acc_ref[...]
            o_ref[...] = o
            so_ref[...] = s_ref[...] + 0.5 * _row_gate(m_ref[...], 3 * k + 2, i, tm, n_lat) * o

    row = pl.BlockSpec((tm, d), lambda i, j: (i, 0))
    return _pcall(body, name=name, grid=(r // tm, nch),
                  in_specs=[pl.BlockSpec((None, tm, fc), lambda i, j: (j, i, 0)),
                            pl.BlockSpec((None, fc, d), lambda i, j: (j, wd_blk, 0)), row,
                            pl.BlockSpec(mod.shape, lambda i, j: (0, 0, 0))],
                  out_specs=[row, row], out_shape=[jax.ShapeDtypeStruct((r, d), F32)] * 2,
                  scratch_shapes=[pltpu.VMEM((tm, d), F32)],
                  compiler_params=_params(("arbitrary", "arbitrary")))(hid, wd, s, mod)


def _ffn_dhid(name, d_o, wd, wd_blk, a, b, tm):
    r, d = d_o.shape
    nch, _, fc = a.shape

    def body(g_ref, w_ref, a_ref, b_ref, da_ref, db_ref):
        dh = _dot(g_ref[...], w_ref[...], NT)
        av, bv = a_ref[...].astype(F32), b_ref[...].astype(F32)
        da_ref[...] = (dh * bv * _dsilu(av)).astype(BF)
        db_ref[...] = (dh * _silu(av)).astype(BF)

    chunk = pl.BlockSpec((None, tm, fc), lambda j, i: (j, i, 0))
    return _pcall(body, name=name, grid=(nch, r // tm),
                  in_specs=[pl.BlockSpec((tm, d), lambda j, i: (i, 0)),
                            pl.BlockSpec((None, fc, d), lambda j, i: (j, wd_blk, 0)), chunk, chunk],
                  out_specs=[chunk] * 2, out_shape=[jax.ShapeDtypeStruct((nch, r, fc), BF)] * 2,
                  compiler_params=_params(("arbitrary", "arbitrary")))(d_o, wd, a, b)


def _ffn_du(name, da, db, wg, wu, base, tm):
    nch, r, fc = da.shape
    d = wg.shape[2]

    def body(da_ref, db_ref, wg_ref, wu_ref, o_ref, acc_ref):
        j = pl.program_id(1)
        part = _dot(da_ref[...], wg_ref[...], NT) + _dot(db_ref[...], wu_ref[...], NT)

        @pl.when(j == 0)
        def _():
            acc_ref[...] = part

        @pl.when(j > 0)
        def _():
            acc_ref[...] += part

        @pl.when(j == nch - 1)
        def _():
            o_ref[...] = acc_ref[...]

    chunk = pl.BlockSpec((None, tm, fc), lambda i, j: (j, i, 0))
    return _pcall(body, name=name, grid=(r // tm, nch),
                  in_specs=[chunk, chunk, pl.BlockSpec((None, None, d, fc), lambda i, j: (j, base, 0, 0)),
                            pl.BlockSpec((None, None, d, fc), lambda i, j: (j, base, 0, 0))],
                  out_specs=pl.BlockSpec((tm, d), lambda i, j: (i, 0)),
                  out_shape=jax.ShapeDtypeStruct((r, d), F32), scratch_shapes=[pltpu.VMEM((tm, d), F32)],
                  compiler_params=_params(("arbitrary", "arbitrary")))(da, db, wg, wu)


def _ffn_dw_in(name, u, dz, tmm, grads, idx):
    r, d = u.shape
    nch, _, fc = dz.shape
    nb = d // tmm

    def body(u_ref, z_ref, g_ref, o_ref):
        o_ref[...] = _dot(u_ref[...], z_ref[...], TN).astype(o_ref.dtype)

    return _pcall(body, name=name, grid=(nch, nb),
                  in_specs=[pl.BlockSpec((r, tmm), lambda j, mi: (0, mi)),
                            pl.BlockSpec((None, r, fc), lambda j, mi: (j, 0, 0)),
                            pl.BlockSpec(memory_space=pl.ANY)],
                  out_specs=pl.BlockSpec((None, tmm, fc), lambda j, mi: (j, idx * nb + mi, 0)),
                  out_shape=jax.ShapeDtypeStruct(grads.shape, grads.dtype), input_output_aliases={2: 0},
                  compiler_params=_params(("arbitrary", "arbitrary")))(u, dz, grads)


def _ffn_dw_down(name, hid, d_o, tn, grads, idx):
    nch, r, fc = hid.shape
    d = d_o.shape[1]

    def body(h_ref, g_ref, acc_ref, o_ref):
        o_ref[...] = _dot(h_ref[...], g_ref[...], TN).astype(o_ref.dtype)

    return _pcall(body, name=name, grid=(nch, d // tn),
                  in_specs=[pl.BlockSpec((None, r, fc), lambda j, ni: (j, 0, 0)),
                            pl.BlockSpec((r, tn), lambda j, ni: (0, ni)),
                            pl.BlockSpec(memory_space=pl.ANY)],
                  out_specs=pl.BlockSpec((None, fc, tn), lambda j, ni: (j, idx, ni)),
                  out_shape=jax.ShapeDtypeStruct(grads.shape, grads.dtype), input_output_aliases={2: 0},
                  compiler_params=_params(("arbitrary", "arbitrary")))(hid, d_o, grads)


def _partner(x):
    n = x.shape[1]
    lane = lax.broadcasted_iota(jnp.int32, x.shape, 1)
    return jnp.where((lane & 15) < 8, pltpu.roll(x, n - 8, 1), pltpu.roll(x, 8, 1))


def _rope(x, ct, st):
    reps = x.shape[1] // ct.shape[1]
    if reps > 1:
        ct, st = jnp.tile(ct, (1, reps)), jnp.tile(st, (1, reps))
    return x * ct + _partner(x) * st


def _rope_t(dy, ct, st):
    reps = dy.shape[1] // ct.shape[1]
    if reps > 1:
        ct, st = jnp.tile(ct, (1, reps)), jnp.tile(st, (1, reps))
    return dy * ct + _partner(dy * st)


def _rope_tables(t_len, g_len, lane0):
    half = QK_ROPE // 4
    pos = jnp.arange(t_len)
    row = (pos // GRID_W).astype(F32)
    col = (pos % GRID_W).astype(F32)
    freqs = jnp.power(ROPE_THETA, -jnp.arange(0, QK_ROPE // 2, 2, dtype=F32) / (QK_ROPE // 2))
    ang_r, ang_c = row[:, None] * freqs, col[:, None] * freqs
    cs = jnp.concatenate([jnp.cos(ang_r)] * 2 + [jnp.cos(ang_c)] * 2, axis=1)
    sn = jnp.concatenate([-jnp.sin(ang_r), jnp.sin(ang_r), -jnp.sin(ang_c), jnp.sin(ang_c)], axis=1)
    assert cs.shape[1] == 4 * half == QK_ROPE
    ct = jnp.ones((t_len + g_len, HEAD_PAD), F32).at[:t_len, lane0:lane0 + QK_ROPE].set(cs)
    st = jnp.zeros((t_len + g_len, HEAD_PAD), F32).at[:t_len, lane0:lane0 + QK_ROPE].set(sn)
    return ct, st


def _attn_fwd(name, q, kp, vp, n_q, q_off, n_k, k_blk, heads, tq, scale):
    qb = q_off // tq

    def body(q_ref, k_ref, v_ref, o_ref, l_ref):
        s = _dot(q_ref[...], k_ref[...], NT) * scale
        m = jnp.max(s, axis=1, keepdims=True)
        p = jnp.exp(s - m)
        l = jnp.sum(p, axis=1, keepdims=True)
        o_ref[...] = (_dot(p, v_ref[...], NN) / l).astype(BF)
        l_ref[...] = jnp.broadcast_to(m + jnp.log(l), l_ref.shape)

    hw = heads * HEAD_PAD
    blk = pl.BlockSpec((tq, HEAD_PAD), lambda h, i: (i, h))
    kv = pl.BlockSpec((n_k, HEAD_PAD), lambda h, i: (k_blk, h))
    return _pcall(body, name=name, grid=(heads, n_q // tq),
                  in_specs=[pl.BlockSpec((tq, HEAD_PAD), lambda h, i: (i + qb, h)), kv, kv],
                  out_specs=[blk, blk],
                  out_shape=[jax.ShapeDtypeStruct((n_q, hw), BF), jax.ShapeDtypeStruct((n_q, hw), F32)],
                  compiler_params=_params(("arbitrary", "arbitrary")))(q, kp, vp)


def _attn_bwd(name, q, kp, vp, cat, dcat, lse, n_q, q_off, n_k, k_blk, heads, tq, scale, col_blk, onto=None):
    qb = q_off // tq

    def body(q_ref, k_ref, v_ref, o_ref, do_ref, l_ref, *rest):
        dq_ref, dk_ref, dv_ref = rest[-3:]
        i = pl.program_id(1)
        qv, kv_, vv = q_ref[...], k_ref[...], v_ref[...]
        dov = do_ref[...]
        s = _dot(qv, kv_, NT) * scale
        p = jnp.exp(s - l_ref[...][:, 0:1])
        dp = _dot(dov, vv, NT)
        delta = jnp.sum(dov * o_ref[...].astype(F32), axis=1, keepdims=True)
        ds = (p * (dp - delta) * scale).astype(BF)
        dq_ref[...] = _dot(ds, kv_, NN)
        dk = _dot(ds, qv, TN)
        dv = _dot(p, dov, TN)

        @pl.when(i == 0)
        def _():
            if onto is None:
                dk_ref[...] = dk
                dv_ref[...] = dv
            else:
                dk_ref[...] = rest[0][...] + dk
                dv_ref[...] = rest[1][...] + dv

        @pl.when(i > 0)
        def _():
            dk_ref[...] += dk
            dv_ref[...] += dv

    hw = heads * HEAD_PAD
    qspec = pl.BlockSpec((tq, HEAD_PAD), lambda h, i: (i + qb, h))
    cspec = pl.BlockSpec((tq, HEAD_PAD), lambda h, i: (i + qb, col_blk + h))
    kv = pl.BlockSpec((n_k, HEAD_PAD), lambda h, i: (k_blk, h))
    blk = pl.BlockSpec((tq, HEAD_PAD), lambda h, i: (i, h))
    if onto is None:
        acc = pl.BlockSpec((n_k, HEAD_PAD), lambda h, i: (0, h))
        return _pcall(body, name=name, grid=(heads, n_q // tq),
                      in_specs=[qspec, kv, kv, cspec, cspec, blk], out_specs=[blk, acc, acc],
                      out_shape=[jax.ShapeDtypeStruct((n_q, hw), F32), jax.ShapeDtypeStruct((n_k, hw), F32),
                                 jax.ShapeDtypeStruct((n_k, hw), F32)],
                      compiler_params=_params(("arbitrary", "arbitrary")))(q, kp, vp, cat, dcat, lse)
    return _pcall(body, name=name, grid=(heads, n_q // tq),
                  in_specs=[qspec, kv, kv, cspec, cspec, blk, kv, kv], out_specs=[blk, kv, kv],
                  out_shape=[jax.ShapeDtypeStruct((n_q, hw), F32)] + [jax.ShapeDtypeStruct(t.shape, F32) for t in onto],
                  input_output_aliases={6: 1, 7: 2},
                  compiler_params=_params(("arbitrary", "arbitrary")))(q, kp, vp, cat, dcat, lse, *onto)


def _shift(x, k):
    return pltpu.roll(x, k % x.shape[0], 0)


def _window_sum(v, w, mirrored):
    n, gd = v.shape
    pad = jnp.zeros((POOL_PAD, gd), F32)
    e = jnp.concatenate([pad, v, pad], axis=0)
    acc = e + _shift(e, -1 if mirrored else 1)
    step = 1
    while 2 * step < w:
        acc = _shift(acc, step) + _shift(acc, -step)
        step *= 2
    return acc[POOL_PAD:POOL_PAD + n]


def _window_count(n, w):
    t = lax.broadcasted_iota(jnp.int32, (n, 1), 0)
    lo = jnp.maximum(t - w // 2, 0)
    hi = jnp.minimum(t + (w - w // 2 - 1), n - 1)
    return (hi - lo + 1).astype(F32)


def _pool_fwd(name, u, pool_w, scale):
    n, pd = u.shape
    ng = len(POOL_WINDOWS)
    gd = pd // ng

    def body(u_ref, w_ref, s_ref, y_ref):
        for g, w in enumerate(POOL_WINDOWS):
            sl = slice(g * gd, (g + 1) * gd)
            ug = u_ref[:, sl]
            p = _window_sum(ug, w, False) / _window_count(n, w) - ug
            y_ref[:, sl] = (_dot(p, w_ref[g], NN) * s_ref[:, sl]).astype(BF)

    return _pcall(body, name=name, out_shape=jax.ShapeDtypeStruct((n, pd), BF),
                  compiler_params=_params())(u, pool_w, scale)


def _pool_bwd(name, u, dcat, pool_w, scale, row_off):
    n, pd = u.shape
    ng = len(POOL_WINDOWS)
    gd = pd // ng

    def body(u_ref, dy_ref, w_ref, s_ref, du_ref, dw_ref, ds_ref):
        ds_ref[...] = jnp.zeros_like(ds_ref)
        for g, w in enumerate(POOL_WINDOWS):
            sl = slice(g * gd, (g + 1) * gd)
            ug, dy, wg = u_ref[:, sl], dy_ref[:, sl], w_ref[g]
            cnt = _window_count(n, w)
            p = _window_sum(ug, w, False) / cnt - ug
            ds_ref[0:1, sl] = jnp.sum(dy * _dot(p, wg, NN), axis=0, keepdims=True)
            dys = dy * s_ref[:, sl]
            dw_ref[g] = _dot(p, dys, TN)
            dp = _dot(dys, wg, NT)
            du_ref[:, sl] = (_window_sum(dp / cnt, w, True) - dp).astype(BF)

    rb = row_off // n
    return _pcall(body, name=name, grid=(1,),
                  in_specs=[pl.BlockSpec((n, pd), lambda i: (0, 0)), pl.BlockSpec((n, pd), lambda i: (rb, 0)),
                            pl.BlockSpec(pool_w.shape, lambda i: (0, 0, 0)), pl.BlockSpec(scale.shape, lambda i: (0, 0))],
                  out_specs=[pl.BlockSpec((n, pd), lambda i: (0, 0)), pl.BlockSpec((ng, gd, gd), lambda i: (0, 0, 0)),
                             pl.BlockSpec((8, pd), lambda i: (0, 0))],
                  out_shape=[jax.ShapeDtypeStruct((n, pd), BF), jax.ShapeDtypeStruct((ng, gd, gd), F32),
                             jax.ShapeDtypeStruct((8, pd), F32)],
                  compiler_params=_params(("arbitrary",)))(u, dcat, pool_w, scale)


def _edge_shift(z, k):
    n = z.shape[0]
    t = lax.broadcasted_iota(jnp.int32, (n, 1), 0)
    keep = (t >= k) if k > 0 else (t < n + k)
    return jnp.where(keep, pltpu.roll(z, k % n, 0), 0.0)


def _conv_fwd(name, p3, cw, tc):
    n, cd = p3.shape[0], p3.shape[1] // 3
    nb = cd // tc

    def body(b_ref, c_ref, v_ref, w_ref, y_ref):
        z = c_ref[...] * v_ref[...]
        w = w_ref[...]
        zc = w[0:1] * _edge_shift(z, 1) + w[1:2] * z + w[2:3] * _edge_shift(z, -1)
        y_ref[...] = (b_ref[...] * zc).astype(BF)

    return _pcall(body, name=name, grid=(nb,),
                  in_specs=[pl.BlockSpec((n, tc), lambda j: (0, j)), pl.BlockSpec((n, tc), lambda j: (0, nb + j)),
                            pl.BlockSpec((n, tc), lambda j: (0, 2 * nb + j)), pl.BlockSpec((3, tc), lambda j: (0, j))],
                  out_specs=pl.BlockSpec((n, tc), lambda j: (0, j)), out_shape=jax.ShapeDtypeStruct((n, cd), BF),
                  compiler_params=_params(("arbitrary",)))(p3, p3, p3, cw)


def _conv_bwd(name, p3, cw, dy, tc):
    n, cd = dy.shape
    nb = cd // tc

    def body(b_ref, c_ref, v_ref, w_ref, dy_ref, dp_ref, dw_ref):
        cv, vv, w, dyv = c_ref[...], v_ref[...], w_ref[...], dy_ref[...]
        z = cv * vv
        zl, zr = _edge_shift(z, 1), _edge_shift(z, -1)
        zc = w[0:1] * zl + w[1:2] * z + w[2:3] * zr
        dzc = dyv * b_ref[...]
        dz = w[0:1] * _edge_shift(dzc, -1) + w[1:2] * dzc + w[2:3] * _edge_shift(dzc, 1)
        dp_ref[0] = (dyv * zc).astype(BF)
        dp_ref[1] = (dz * vv).astype(BF)
        dp_ref[2] = (dz * cv).astype(BF)
        dw_ref[...] = jnp.zeros_like(dw_ref)
        dw_ref[0:1, :] = jnp.sum(dzc * zl, axis=0, keepdims=True)
        dw_ref[1:2, :] = jnp.sum(dzc * z, axis=0, keepdims=True)
        dw_ref[2:3, :] = jnp.sum(dzc * zr, axis=0, keepdims=True)

    col = pl.BlockSpec((n, tc), lambda j: (0, j))
    return _pcall(body, name=name, grid=(nb,),
                  in_specs=[col, pl.BlockSpec((n, tc), lambda j: (0, nb + j)),
                            pl.BlockSpec((n, tc), lambda j: (0, 2 * nb + j)), pl.BlockSpec((3, tc), lambda j: (0, j)), col],
                  out_specs=[pl.BlockSpec((3, n, tc), lambda j: (0, 0, j)), pl.BlockSpec((8, tc), lambda j: (0, j))],
                  out_shape=[jax.ShapeDtypeStruct((3, n, cd), BF), jax.ShapeDtypeStruct((8, cd), F32)],
                  compiler_params=_params(("arbitrary",)))(p3, p3, p3, cw, dy)


def _conv_din(name, dp3, w_in, tm):
    _, n, cd = dp3.shape
    d = w_in.shape[0]

    def body(a_ref, w_ref, o_ref, acc_ref):
        j = pl.program_id(1)
        part = _dot(a_ref[...], w_ref[...], NT)

        @pl.when(j == 0)
        def _():
            acc_ref[...] = part

        @pl.when(j > 0)
        def _():
            acc_ref[...] += part

        @pl.when(j == 2)
        def _():
            o_ref[...] = acc_ref[...]

    return _pcall(body, name=name, grid=(n // tm, 3),
                  in_specs=[pl.BlockSpec((None, tm, cd), lambda i, j: (j, i, 0)),
                            pl.BlockSpec((d, cd), lambda i, j: (0, j))],
                  out_specs=pl.BlockSpec((tm, d), lambda i, j: (i, 0)), out_shape=jax.ShapeDtypeStruct((n, d), F32),
                  scratch_shapes=[pltpu.VMEM((tm, d), F32)],
                  compiler_params=_params(("arbitrary", "arbitrary")))(dp3, w_in)


def _conv_dw_in(name, u, dp3, tmm, tn):
    n, d = u.shape
    cd = dp3.shape[2]
    nb = cd // tn

    def body(u_ref, z_ref, o_ref):
        o_ref[...] = _dot(u_ref[...], z_ref[...], TN)

    return _pcall(body, name=name, grid=(3 * nb, d // tmm),
                  in_specs=[pl.BlockSpec((n, tmm), lambda j, mi: (0, mi)),
                            pl.BlockSpec((None, n, tn), lambda j, mi: (j // nb, 0, j % nb))],
                  out_specs=pl.BlockSpec((tmm, tn), lambda j, mi: (mi, j)),
                  out_shape=jax.ShapeDtypeStruct((d, 3 * cd), F32),
                  compiler_params=_params(("arbitrary", "arbitrary")))(u, dp3)


def _loss_head(name, h, target, gain, tm):
    d = h.shape[1]

    def fn(rows, consts, m):
        hv, tv = rows
        g = consts[0][0:1, :]
        n, r = _rms(hv)
        err = n * g - tv
        dy = err / d
        loss = 0.5 * jnp.sum(err * err) / d
        acc = {0: jnp.sum(dy * n, axis=0, keepdims=True), 1: jnp.full((1, d), loss, F32)}
        return [_rms_bwd(dy * g, n, r)], acc

    return _rows_call(name, fn, h.shape[0], tm, [h, target], [gain], None, [(d, F32)], acc_w=d)


def _adamw(name, w, g, m, v, after=None):
    shape = w.shape
    if w.ndim == 1:
        shape2 = (1,) + shape
        res = _adamw(name, *[t.reshape(shape2) for t in (w, g, m, v)], after=after)
        return [t.reshape(shape) for t in res]
    if shape[-1] % 128 and shape[-2] % 128 == 0:
        res = _adamw(name, *[jnp.swapaxes(t, -1, -2) for t in (w, g, m, v)], after=after)
        return [jnp.swapaxes(t, -1, -2) for t in res]
    lead, (r, cdim) = shape[:-2], shape[-2:]
    tr = r
    if r * cdim * 4 > (3 << 19):
        tr = _pick(r, max(8, (3 << 19) // (cdim * 4)), 8)
    c1 = 1.0 / (1.0 - ADAM_B1 ** ADAM_STEP)
    c2 = 1.0 / (1.0 - ADAM_B2 ** ADAM_STEP)
    nl = len(lead)

    def body(w_ref, g_ref, m_ref, v_ref, *rest):
        d_ref, nm_ref, nv_ref = rest[-3:]
        gv = g_ref[...]
        nm = ADAM_B1 * m_ref[...] + (1.0 - ADAM_B1) * gv
        nv = ADAM_B2 * v_ref[...] + (1.0 - ADAM_B2) * (gv * gv)
        nm_ref[...] = nm
        nv_ref[...] = nv
        d_ref[...] = -ADAM_LR * ((nm * c1) / (jnp.sqrt(nv * c2) + ADAM_EPS) + ADAM_WD * w_ref[...])

    spec = pl.BlockSpec((None,) * nl + (tr, cdim), lambda *idx: idx + (0,))
    extra = [] if after is None else [after]
    res = _pcall(body, name=name, grid=lead + (r // tr,),
                 in_specs=[spec] * 4 + [pl.BlockSpec(memory_space=pl.ANY)] * len(extra), out_specs=[spec] * 3,
                 out_shape=[jax.ShapeDtypeStruct(shape, F32)] * 3,
                 compiler_params=_params(("arbitrary",) * (nl + 1)))(w, g, m, v, *extra)
    return list(res)


def _ffn_half_fwd(tag, s, gains, mod, k, wts, n_lat, tm, h_tiles, tm_big, after=None):
    wg, wu, wd, idx = wts
    u = _adaln_fwd(f"adaln_{tag}", s, gains, k, mod, k, tm, h_tiles, after)
    a, b, hid = _ffn_up(f"ffn_up_{tag}", u, wg, wu, idx, tm)
    s_out, o = _ffn_down(f"ffn_down_{tag}", hid, wd, idx, s, mod, k, n_lat, tm_big)
    return s_out, (s, u, a, b, hid, o)


def _ffn_half_bwd(tag, ds_out, saved, gains, mod, k, wts, big_grads, tm, h_tiles, tm_big):
    wg, wu, wd, idx = wts
    g_gate, g_up, g_down = big_grads
    s, u, a, b, hid, o = saved
    d_o, acc_g = _resid_bwd(f"resid_bwd_{tag}", ds_out, o, mod, k, 0.5, tm, h_tiles)
    da, db = _ffn_dhid(f"ffn_dhid_{tag}", d_o, wd, idx, a, b, tm)
    du = _ffn_du(f"ffn_du_{tag}", da, db, wg, wu, idx, tm_big)
    d = u.shape[1]
    g_gate = _ffn_dw_in(f"ffn_dwg_{tag}", u, da, _pick(d, 256), g_gate, idx)
    g_up = _ffn_dw_in(f"ffn_dwu_{tag}", u, db, _pick(d, 256), g_up, idx)
    g_down = _ffn_dw_down(f"ffn_dwd_{tag}", hid, d_o, _pick(d, 512), g_down, idx)
    ds, acc_n = _adaln_bwd(f"adaln_bwd_{tag}", s, du, ds_out, gains, k, mod, k, tm, h_tiles)
    return ds, (g_gate, g_up, g_down), (acc_n[:, 0], acc_n[:, 1], acc_g[:, 0]), jnp.sum(acc_n[:, 2], axis=0)


def kernel(x, c, ctx, c_ctx, norm_g, w_mod, b_mod, ffn_w_gate, ffn_w_up, ffn_w_down, ab_w_in, pool_w, pool_scale, q_norm_g, w_uq, kv_norm_g, w_ukv, ab_w_out, conv_w_in, conv_w, conv_w_out, final_norm_g, loss_target, m_c_ctx, m_norm_g, m_w_mod, m_b_mod, m_ffn_w_gate, m_ffn_w_up, m_ffn_w_down, m_ab_w_in, m_pool_w, m_pool_scale, m_q_norm_g, m_w_uq, m_kv_norm_g, m_w_ukv, m_ab_w_out, m_conv_w_in, m_conv_w, m_conv_w_out, m_final_norm_g, v_c_ctx, v_norm_g, v_w_mod, v_b_mod, v_ffn_w_gate, v_ffn_w_up, v_ffn_w_down, v_ab_w_in, v_pool_w, v_pool_scale, v_q_norm_g, v_w_uq, v_kv_norm_g, v_w_ukv, v_ab_w_out, v_conv_w_in, v_conv_w, v_conv_w_out, v_final_norm_g):
    weights = dict(c_ctx=c_ctx, norm_g=norm_g, w_mod=w_mod, b_mod=b_mod, ffn_w_gate=ffn_w_gate, ffn_w_up=ffn_w_up,
                   ffn_w_down=ffn_w_down, ab_w_in=ab_w_in, pool_w=pool_w, pool_scale=pool_scale, q_norm_g=q_norm_g,
                   w_uq=w_uq, kv_norm_g=kv_norm_g, w_ukv=w_ukv, ab_w_out=ab_w_out, conv_w_in=conv_w_in, conv_w=conv_w,
                   conv_w_out=conv_w_out, final_norm_g=final_norm_g)
    mom_m = dict(c_ctx=m_c_ctx, norm_g=m_norm_g, w_mod=m_w_mod, b_mod=m_b_mod, ffn_w_gate=m_ffn_w_gate,
                 ffn_w_up=m_ffn_w_up, ffn_w_down=m_ffn_w_down, ab_w_in=m_ab_w_in, pool_w=m_pool_w,
                 pool_scale=m_pool_scale, q_norm_g=m_q_norm_g, w_uq=m_w_uq, kv_norm_g=m_kv_norm_g, w_ukv=m_w_ukv,
                 ab_w_out=m_ab_w_out, conv_w_in=m_conv_w_in, conv_w=m_conv_w, conv_w_out=m_conv_w_out,
                 final_norm_g=m_final_norm_g)
    mom_v = dict(c_ctx=v_c_ctx, norm_g=v_norm_g, w_mod=v_w_mod, b_mod=v_b_mod, ffn_w_gate=v_ffn_w_gate,
                 ffn_w_up=v_ffn_w_up, ffn_w_down=v_ffn_w_down, ab_w_in=v_ab_w_in, pool_w=v_pool_w,
                 pool_scale=v_pool_scale, q_norm_g=v_q_norm_g, w_uq=v_w_uq, kv_norm_g=v_kv_norm_g, w_ukv=v_w_ukv,
                 ab_w_out=v_ab_w_out, conv_w_in=v_conv_w_in, conv_w=v_conv_w, conv_w_out=v_conv_w_out,
                 final_norm_g=v_final_norm_g)

    t_len, d = x.shape[1], x.shape[2]
    g_len = ctx.shape[1]
    r_len = t_len + g_len
    fc = ffn_w_gate.shape[3]
    heads = d // 128
    pool_dim = d // 2
    q_rank, kv_rank = q_norm_g.shape[1], kv_norm_g.shape[1]
    hw = heads * HEAD_PAD
    attn_scale = 1.0 / math.sqrt(QK_NOPE + QK_ROPE)
    kvr_w = kv_rank + HEAD_PAD
    in_w = pool_dim + q_rank + kvr_w
    tm = 256 if g_len % 256 == 0 else g_len
    assert t_len % tm == 0 and g_len % tm == 0 and t_len % g_len == 0 and pool_dim % 128 == 0
    h_tiles = t_len // tm
    tm_l0 = _pick(r_len, 768, tm)
    tm_l1 = _pick(t_len, 1024, tm)

    xi, yi, ci = lax.axis_index("x"), lax.axis_index("y"), lax.axis_index("c")
    me = 4 * xi + 2 * yi + ci
    shard = 2 * xi + yi

    def halves(w):
        return w.astype(BF).reshape(2, -1, w.shape[-1])

    ffn_names = ["ffn_w_gate", "ffn_w_up", "ffn_w_down"]
    mixer_names = [["ab_w_in", "w_uq", "w_ukv", "ab_w_out"], ["conv_w_in", "conv_w_out"]]
    big_names = ffn_names + mixer_names[0] + mixer_names[1]

    def layer_halves(l):
        return [halves(weights[nm][l]) for nm in ffn_names] + [halves(weights[nm]) for nm in mixer_names[l]]

    small = jnp.concatenate([norm_g.reshape(6, -1), conv_w[0]], axis=0)
    small = jnp.pad(small, ((0, 7), (0, 0)))
    c_row = jnp.pad(c, ((0, 7), (0, 0)))
    small_all, c_all = _gather_all("gather_small", [small, c_row])
    small_full = small_all[::2].transpose(1, 0, 2).reshape(16, d)
    gains = [jnp.pad(small_full[3 * l:3 * l + 3], ((0, 5), (0, 0))) for l in range(2)]
    conv_w_full = small_full[6:9]
    c16 = jnp.concatenate([c_all[:, 0], c_ctx[None], jnp.zeros((7, d), F32)], axis=0)

    n_col = w_mod.shape[2]
    b_sh = lax.dynamic_slice_in_dim(b_mod, shard * n_col, n_col, axis=1)
    m_sh = [_mm(f"mod_fwd_{l}", c16, w_mod, 'nn', tm=16, tn=768, a_pre=_silu, b_lead=l,
                epi=lambda acc, i, bv: (acc + bv,), epi_args=(b_sh[l:l + 1],), epi_kinds=('n',)) for l in range(2)]
    m_all = _gather_all("gather_mod", [jnp.concatenate(m_sh, axis=0)])[0]
    m_full = m_all[::2].reshape(N_SHARD, 2, 16, n_col).transpose(1, 2, 0, 3).reshape(2, 16, N_MOD * d)
    mod_h = [jnp.pad(lax.dynamic_index_in_dim(m_full[l], me, 0, keepdims=False).reshape(N_MOD, d), ((0, 7), (0, 0)))
             for l in range(2)]
    mod_g0 = jnp.pad(m_full[0, 8].reshape(N_MOD, d), ((0, 7), (0, 0)))
    mods = [jnp.stack([mod_h[0], mod_g0]), mod_h[1][None]]

    def gather_start(l, dep):
        own = lax.optimization_barrier((tuple(layer_halves(l)), dep))[0]
        n = len(own)
        return _split_start(f"gather_start_l{l}", list(own) + _landing(N_DEV, own), n * len(CHIP_FLIPS),
                            _chips_gather_build(n))

    def forward_start(l, handle, after):
        bufs = _split_wait(f"gather_wait_l{l}", handle, after)
        n = len(bufs) // 2
        return bufs[:n], _split_start(f"forward_start_l{l}", bufs[n:], n * len(CHIP_FLIPS), _sibling_forward_build(n))

    def gathered_layer(l, own, handle, after):
        landed = _split_wait(f"forward_wait_l{l}", handle, after)
        full = [lax.dynamic_update_slice_in_dim(z, a, 2 * shard, 0) for z, a in zip(landed, own)]
        return {nm: g.reshape(N_SHARD, 2 * g.shape[1], g.shape[2])
                for nm, g in zip(ffn_names + mixer_names[l], full)}

    def ffn_weights(gw):
        wg, wu = gw["ffn_w_gate"].reshape(N_SHARD, 2, d, fc), gw["ffn_w_up"].reshape(N_SHARD, 2, d, fc)
        return [(wg, wu, gw["ffn_w_down"], f) for f in range(2)]

    gather0 = gather_start(0, m_all)
    own0, forward0 = forward_start(0, gather0, gather0['token'])
    gather1 = gather_start(1, forward0['token'])
    gw0 = gathered_layer(0, own0, forward0, forward0['token'])
    ffn_w = [ffn_weights(gw0), None]
    w_out_full = gw0["ab_w_out"].reshape(-1, d)
    w_uq_full = gw0["w_uq"].reshape(q_rank, heads * (QK_NOPE + QK_ROPE))
    w_ukv_full = gw0["w_ukv"].transpose(1, 0, 2).reshape(kv_rank, heads * (QK_NOPE + V_HEAD))
    w_in_full = gw0["ab_w_in"].transpose(1, 0, 2).reshape(d, -1)

    wq_p = jnp.pad(w_uq_full.reshape(q_rank, heads, QK_NOPE + QK_ROPE),
                   ((0, 0), (0, 0), (0, HEAD_PAD - QK_NOPE - QK_ROPE))).reshape(q_rank, hw)
    ukv3 = w_ukv_full.reshape(kv_rank, heads, QK_NOPE + V_HEAD)
    wk_top = jnp.pad(ukv3[..., :QK_NOPE], ((0, 0), (0, 0), (0, HEAD_PAD - QK_NOPE))).reshape(kv_rank, hw)
    wv_top = jnp.pad(ukv3[..., QK_NOPE:], ((0, 0), (0, 0), (0, HEAD_PAD - V_HEAD))).reshape(kv_rank, hw)
    src_row = lax.broadcasted_iota(jnp.int32, (HEAD_PAD, hw), 0)
    dst_lane = lax.broadcasted_iota(jnp.int32, (HEAD_PAD, hw), 1) % HEAD_PAD
    spread = ((src_row < QK_ROPE) & (dst_lane == src_row + QK_NOPE)).astype(BF)
    wk_ext = jnp.concatenate([wk_top, spread], axis=0)
    wv_ext = jnp.concatenate([wv_top, jnp.zeros((HEAD_PAD, hw), BF)], axis=0)
    w_in_pool = w_in_full[:, :pool_dim]
    w_in_q = w_in_full[:, pool_dim:pool_dim + q_rank]
    w_in_kvr = jnp.pad(w_in_full[:, pool_dim + q_rank:], ((0, 0), (0, HEAD_PAD - QK_ROPE)))
    w_out_attn = jnp.pad(w_out_full[pool_dim:].reshape(heads, V_HEAD, d),
                         ((0, 0), (0, HEAD_PAD - V_HEAD), (0, 0))).reshape(hw, d)
    w_out_p = jnp.concatenate([w_out_full[:pool_dim], w_out_attn], axis=0)

    s0 = jnp.concatenate([x[0], ctx[0]], axis=0)
    s1, sav_f00 = _ffn_half_fwd("l0a", s0, gains[0], mods[0], 0, ffn_w[0][0], t_len, tm, h_tiles, tm_l0,
                                gather1['token'])
    u_mix = _adaln_fwd("adaln_l0m", s1, gains[0], 1, mods[0], 1, tm, h_tiles)
    p_pool = _mm("in_pool", u_mix, w_in_pool, 'nn', tm=tm, tn=pool_dim)
    p_q = _mm("in_q", u_mix, w_in_q, 'nn', tm=tm, tn=q_rank)
    p_kvr = _mm("in_kvr", u_mix, w_in_kvr, 'nn', tm=tm, tn=kvr_w)
    qg = jnp.pad(q_norm_g, ((0, 7), (0, 0)))
    kvg = jnp.pad(kv_norm_g, ((0, 7), (0, 0)))
    tq_c, tq_s = _rope_tables(t_len, g_len, QK_NOPE)
    tk_c, tk_s = _rope_tables(t_len, g_len, 0)

    def qn_fn(rows, consts, m):
        n, _ = _rms(rows[0])
        return [n * consts[0][0:1, :]], {}

    qn = _rows_call("q_norm", qn_fn, r_len, tm, [p_q], [qg], None, [(q_rank, BF)])[0]
    q_r = _mm("q_up", qn, wq_p, 'nn', tm=tm, tn=hw, out_dtypes=(BF,),
              epi=lambda acc, i, ct, st: (_rope(acc, ct, st),), epi_args=(tq_c, tq_s), epi_kinds=('mt', 'mt'))

    def kvn_fn(rows, consts, m):
        pv, ct, st = rows
        n, _ = _rms(pv[:, :kv_rank])
        return [jnp.concatenate([n * consts[0][0:1, :], _rope(pv[:, kv_rank:], ct, st)], axis=1)], {}

    kvn = _rows_call("kv_norm", kvn_fn, r_len, tm, [p_kvr, tk_c, tk_s], [kvg], None, [(kvr_w, BF)])[0]
    k_p = _mm("k_up", kvn, wk_ext, 'nn', tm=tm, tn=hw, out_dtypes=(BF,))
    v_p = _mm("v_up", kvn, wv_ext, 'nn', tm=tm, tn=hw, out_dtypes=(BF,))
    o_h, lse_h = _attn_fwd("attn_h", q_r, k_p, v_p, t_len, 0, r_len, 0, heads, tm, attn_scale)
    o_g, lse_g = _attn_fwd("attn_g", q_r, k_p, v_p, g_len, t_len, g_len, t_len // g_len, heads, tm, attn_scale)
    y_h = _pool_fwd("pool_h", p_pool[:t_len], pool_w[0], pool_scale)
    y_g = _pool_fwd("pool_g", p_pool[t_len:], pool_w[0], pool_scale)
    cat = jnp.concatenate([jnp.concatenate([y_h, y_g], axis=0), jnp.concatenate([o_h, o_g], axis=0)], axis=1)

    def resid_epi(k3, n_lat, tmr):
        def epi(acc, i, sv, mv):
            return sv + _row_gate(mv, k3, i, tmr, n_lat) * acc, acc
        return epi

    s2, o_mix0 = _mm("mix_out_l0", cat, w_out_p, 'nn', tm=tm, tn=d, out_dtypes=(F32, F32),
                     epi=resid_epi(5, t_len, tm), epi_args=(s1, mods[0]), epi_kinds=('mn', 'w'))
    own1, forward1 = forward_start(1, gather1, s2)
    s3, sav_f01 = _ffn_half_fwd("l0b", s2, gains[0], mods[0], 2, ffn_w[0][1], t_len, tm, h_tiles, tm_l0,
                                forward1['token'])

    gw1 = gathered_layer(1, own1, forward1, s3)
    ffn_w[1] = ffn_weights(gw1)
    cw_out_full = gw1["conv_w_out"].reshape(-1, d)
    cw_in_full = gw1["conv_w_in"].transpose(1, 0, 2).reshape(d, -1)
    tml = 256 if t_len % 256 == 0 else tm
    h3 = s3[:t_len]
    h4, sav_f10 = _ffn_half_fwd("l1a", h3, gains[1], mods[1], 0, ffn_w[1][0], t_len, tml, None, tm_l1)
    u_cv = _adaln_fwd("adaln_l1m", h4, gains[1], 1, mods[1], 1, tml, None)
    p3 = _mm("conv_in", u_cv, cw_in_full, 'nn', tm=tml, tn=512)
    cwp = conv_w_full
    tc = _pick(d, 256)
    y_cv = _conv_fwd("conv_fwd", p3, cwp, tc)
    h5, o_mix1 = _mm("mix_out_l1", y_cv, cw_out_full, 'nn', tm=tml, tn=d, out_dtypes=(F32, F32),
                     epi=resid_epi(5, t_len, tml), epi_args=(h4, mods[1]), epi_kinds=('mn', 'w'))
    h6, sav_f11 = _ffn_half_fwd("l1b", h5, gains[1], mods[1], 2, ffn_w[1][1], t_len, tml, None, tm_l1)

    fg = jnp.pad(final_norm_g[None], ((0, 7), (0, 0)))
    dh6, acc_loss = _loss_head("loss_head", h6, loss_target[0], fg, tml)
    d_final_g = acc_loss[0, 0]

    dgain = [[None] * 3 for _ in range(2)]
    dmod = [[None] * N_MOD for _ in range(2)]

    def put(l, k, triple):
        dmod[l][3 * k], dmod[l][3 * k + 1], dmod[l][3 * k + 2] = triple

    def empty_ffn_grads():
        return (lax.empty((N_SHARD, 2 * d, fc), BF), lax.empty((N_SHARD, 2 * d, fc), BF),
                lax.empty((N_SHARD, 2 * fc, d), BF))

    def by_shard_rows(g):
        return g.reshape(N_SHARD, -1, g.shape[-1])

    def by_shard_cols(g):
        return g.reshape(g.shape[0], N_SHARD, -1).transpose(1, 0, 2)

    def pair_start(l, big):
        send = [b.astype(BF).reshape(N_DEV, b.shape[1] // 2, b.shape[2]) for b in big]
        n = len(send)
        return _split_start(f"grads_pair_start_l{l}", send + _landing(N_SHARD, send), n * N_SHARD,
                            _sibling_halves_build(n))

    def chips_start(l, handle, after):
        bufs = _split_wait(f"grads_pair_wait_l{l}", handle, after)
        n = len(bufs) // 2
        pre = [_add_halves(f"grads_add_l{l}_{nm}", s, z)
               for nm, s, z in zip(ffn_names + mixer_names[l], bufs[:n], bufs[n:])]
        return _split_start(f"grads_start_l{l}", pre + _landing(N_SHARD, pre), n * len(CHIP_FLIPS),
                            _chips_scatter_build(n))

    def landed_sums(l, handle, after):
        bufs = _split_wait(f"grads_wait_l{l}", handle, after)
        n = len(bufs) // 2
        landed = [lax.dynamic_update_slice_in_dim(z, lax.dynamic_slice_in_dim(p, shard, 1, 0), shard, 0)
                  for p, z in zip(bufs[:n], bufs[n:])]
        return [_sum_lead(f"sum_grads_l{l}_{nm}", z) for nm, z in zip(ffn_names + mixer_names[l], landed)]

    ffn_g = empty_ffn_grads()
    dh5, ffn_g, tr, dgain[1][2] = _ffn_half_bwd("l1b", dh6, sav_f11, gains[1], mods[1], 2, ffn_w[1][1], ffn_g,
                                                tml, None, tm_l1)
    put(1, 2, tr)
    d_o1, acc_g1 = _resid_bwd("resid_bwd_l1m", dh5, o_mix1, mods[1], 1, 1.0, tml, None)
    dy_cv = _mm("mix_out_l1_dx", d_o1, cw_out_full, 'nt', tm=tml, tn=d)
    d_cw_out = _mm("mix_out_l1_dw", y_cv, d_o1, 'tn', tm=256, tn=512)
    dp3, d_cw = _conv_bwd("conv_bwd", p3, cwp, dy_cv, tc)
    du_cv = _conv_din("conv_in_dx", dp3, cw_in_full, tml)
    d_cw_in = _conv_dw_in("conv_in_dw", u_cv, dp3, _pick(d, 256), _pick(d, 512))
    dh4, acc_n1 = _adaln_bwd("adaln_bwd_l1m", h4, du_cv, dh5, gains[1], 1, mods[1], 1, tml, None)
    put(1, 1, (acc_n1[:, 0], acc_n1[:, 1], acc_g1[:, 0]))
    dgain[1][1] = acc_n1[0, 2]
    dh3, ffn_g, tr, dgain[1][0] = _ffn_half_bwd("l1a", dh4, sav_f10, gains[1], mods[1], 0, ffn_w[1][0], ffn_g,
                                                tml, None, tm_l1)
    put(1, 0, tr)

    pair1 = pair_start(1, list(ffn_g) + [by_shard_cols(d_cw_in), by_shard_rows(d_cw_out)])

    ds3 = jnp.concatenate([dh3, jnp.zeros((g_len, d), F32)], axis=0) + pair1['token'][0, 0]
    ffn_g = empty_ffn_grads()
    ds2, ffn_g, tr, dgain[0][2] = _ffn_half_bwd("l0b", ds3, sav_f01, gains[0], mods[0], 2, ffn_w[0][1], ffn_g,
                                                tm, h_tiles, tm_l0)
    put(0, 2, tr)
    scatter1 = chips_start(1, pair1, ds2)
    d_o0, acc_g0 = _resid_bwd("resid_bwd_l0m", ds2, o_mix0, mods[0], 1, 1.0, tm, h_tiles, scatter1['token'])
    dcat = _mm("mix_out_l0_dx", d_o0, w_out_p, 'nt', tm=tm, tn=pool_dim + hw)
    d_w_out_p = _mm("mix_out_l0_dw", cat, d_o0, 'tn', tm=256, tn=512)
    col_blk = pool_dim // HEAD_PAD
    dq_h, dk_h, dv_h = _attn_bwd("attn_bwd_h", q_r, k_p, v_p, cat, dcat, lse_h, t_len, 0, r_len, 0, heads, tm,
                                 attn_scale, col_blk)
    dq_g, dk_all, dv_all = _attn_bwd("attn_bwd_g", q_r, k_p, v_p, cat, dcat, lse_g, g_len, t_len, g_len,
                                     t_len // g_len, heads, tm, attn_scale, col_blk, onto=(dk_h, dv_h))
    dq_all = jnp.concatenate([dq_h, dq_g], axis=0)
    dkvn = _mm("k_up_dx", dk_all, wk_ext, 'nt', tm=tm, tn=kvr_w)
    dkvn = _mm("v_up_dx", dv_all, wv_ext, 'nt', tm=tm, tn=kvr_w, epi=lambda acc, i, prev: (acc + prev,),
               epi_args=(dkvn,), epi_kinds=('mn',))
    d_wk_ext = _mm("k_up_dw", kvn, dk_all, 'tn', tm=kvr_w, tn=512)
    d_wv_ext = _mm("v_up_dw", kvn, dv_all, 'tn', tm=kvr_w, tn=512)

    def kvn_bwd_fn(rows, consts, m):
        pv, dv_, ct, st = rows
        g = consts[0][0:1, :]
        n, r = _rms(pv[:, :kv_rank])
        dyn = dv_[:, :kv_rank]
        dckv = _rms_bwd(dyn * g, n, r)
        dkr = _rope_t(dv_[:, kv_rank:], ct, st)
        return [jnp.concatenate([dckv, dkr], axis=1)], {0: jnp.sum(dyn * n, axis=0, keepdims=True)}

    dp_kvr, acc_kvg = _rows_call("kv_norm_bwd", kvn_bwd_fn, r_len, tm, [p_kvr, dkvn, tk_c, tk_s], [kvg], None,
                                 [(kvr_w, BF)], acc_w=kv_rank)

    def qrope_bwd_fn(rows, consts, m):
        return [_rope_t(rows[0], rows[1], rows[2])], {}

    dq_pad = _rows_call("q_rope_bwd", qrope_bwd_fn, r_len, tm, [dq_all, tq_c, tq_s], [], None, [(hw, BF)])[0]
    dqn = _mm("q_up_dx", dq_pad, wq_p, 'nt', tm=tm, tn=q_rank)
    d_wq_p = _mm("q_up_dw", qn, dq_pad, 'tn', tm=256, tn=512)

    def qn_bwd_fn(rows, consts, m):
        pv, dv_ = rows
        g = consts[0][0:1, :]
        n, r = _rms(pv)
        return [_rms_bwd(dv_ * g, n, r)], {0: jnp.sum(dv_ * n, axis=0, keepdims=True)}

    dp_q, acc_qg = _rows_call("q_norm_bwd", qn_bwd_fn, r_len, tm, [p_q, dqn], [qg], None, [(q_rank, BF)],
                              acc_w=q_rank)
    dpu_h, dpw_h, dps_h = _pool_bwd("pool_bwd_h", p_pool[:t_len], dcat, pool_w[0], pool_scale, 0)
    dpu_g, dpw_g, dps_g = _pool_bwd("pool_bwd_g", p_pool[t_len:], dcat, pool_w[0], pool_scale, t_len)
    dp_pool = jnp.concatenate([dpu_h, dpu_g], axis=0)
    add_prev = lambda acc, i, prev: (acc + prev,)
    du_mix = _mm("in_pool_dx", dp_pool, w_in_pool, 'nt', tm=tm, tn=d)
    du_mix = _mm("in_q_dx", dp_q, w_in_q, 'nt', tm=tm, tn=d, epi=add_prev, epi_args=(du_mix,), epi_kinds=('mn',))
    du_mix = _mm("in_kvr_dx", dp_kvr, w_in_kvr, 'nt', tm=tm, tn=d, epi=add_prev, epi_args=(du_mix,), epi_kinds=('mn',))
    d_w_in = jnp.concatenate([
        _mm("in_pool_dw", u_mix, dp_pool, 'tn', tm=256, tn=pool_dim),
        _mm("in_q_dw", u_mix, dp_q, 'tn', tm=256, tn=q_rank),
        _mm("in_kvr_dw", u_mix, dp_kvr, 'tn', tm=256, tn=kvr_w)[:, :kv_rank + QK_ROPE]], axis=1)
    ds1, acc_n0 = _adaln_bwd("adaln_bwd_l0m", s1, du_mix, ds2, gains[0], 1, mods[0], 1, tm, h_tiles)
    put(0, 1, (acc_n0[:, 0], acc_n0[:, 1], acc_g0[:, 0]))
    dgain[0][1] = jnp.sum(acc_n0[:, 2], axis=0)
    ds0, ffn_g, tr, dgain[0][0] = _ffn_half_bwd("l0a", ds1, sav_f00, gains[0], mods[0], 0, ffn_w[0][0], ffn_g,
                                                tm, h_tiles, tm_l0)
    put(0, 0, tr)
    grad_x = ds0[:t_len][None]

    d_w_uq = d_wq_p.reshape(q_rank, heads, HEAD_PAD)[..., :QK_NOPE + QK_ROPE].reshape(q_rank, -1)
    d_w_ukv = jnp.concatenate([d_wk_ext[:kv_rank].reshape(kv_rank, heads, HEAD_PAD)[..., :QK_NOPE],
                               d_wv_ext[:kv_rank].reshape(kv_rank, heads, HEAD_PAD)[..., :V_HEAD]],
                              axis=-1).reshape(kv_rank, -1)
    d_w_out = jnp.concatenate([d_w_out_p[:pool_dim],
                               d_w_out_p[pool_dim:].reshape(heads, HEAD_PAD, d)[:, :V_HEAD].reshape(-1, d)], axis=0)

    sums1 = landed_sums(1, scatter1, ds0)

    pair0 = pair_start(0, list(ffn_g) + [by_shard_cols(d_w_in), by_shard_rows(d_w_uq), by_shard_cols(d_w_ukv),
                                         by_shard_rows(d_w_out)])

    dmh = jnp.stack([jnp.stack([dmod[l][k][0] for k in range(N_MOD)]) for l in range(2)])
    dmg0 = jnp.stack([dmod[0][k][1] for k in range(N_MOD)])
    dg_rows = jnp.stack([dgain[l][k] for l in range(2) for k in range(3)])
    pieces = [dmh.reshape(2 * N_MOD, d), dmg0, dg_rows, d_cw[:3], d_final_g[None],
              (dpw_h + dpw_g).reshape(-1, d), jnp.pad((dps_h + dps_g)[0], (0, d - pool_dim))[None],
              jnp.pad(acc_qg[0, 0], (0, d - q_rank))[None], jnp.pad(acc_kvg[0, 0], (0, d - kv_rank))[None],
              acc_loss[0, 1][None]]
    n_piece = [p.shape[0] for p in pieces]
    pieces = [jnp.pad(p, ((0, (-p.shape[0]) % 8), (0, 0))) for p in pieces]
    small_g = jnp.concatenate(pieces, axis=0) + pair0['token'][0, 0]
    sg_all = _gather_all("gather_small_grads", [small_g])[0]
    sg_sum = _sum_lead("sum_small_grads", sg_all)
    offs = [0]
    for p in pieces:
        offs.append(offs[-1] + p.shape[0])
    part = lambda j: sg_sum[offs[j]:offs[j] + n_piece[j]]
    sum_dmh, sum_dmg0, g_norm_full, g_conv_w_full = part(0).reshape(2, N_MOD * d), part(1).reshape(N_MOD * d), part(2), part(3)
    g_final = part(4)[0]
    loss = part(9)[0, 0]
    g_pool_w = part(5).reshape(pool_w.shape)
    g_pool_scale = part(6)[:, :pool_dim]
    g_q_norm = part(7)[:, :q_rank]
    g_kv_norm = part(8)[:, :kv_rank]
    col0 = shard * (d // N_SHARD)
    g_norm_g = lax.dynamic_slice_in_dim(g_norm_full.reshape(2, 3, d), col0, d // N_SHARD, axis=2)
    g_conv_w = lax.dynamic_slice_in_dim(g_conv_w_full, col0, d // N_SHARD, axis=1)[None]
    g_b_mod = _sum_lead("sum_b_mod", jnp.stack([sum_dmh, jnp.stack([sum_dmg0, jnp.zeros_like(sum_dmg0)])]))

    dm16 = []
    for l in range(2):
        per_dev = sg_all[:, l * N_MOD:(l + 1) * N_MOD].reshape(N_DEV, N_MOD * d)
        row8 = sum_dmg0 if l == 0 else jnp.zeros_like(sum_dmg0)
        full = jnp.concatenate([per_dev, row8[None], jnp.zeros((7, N_MOD * d), F32)], axis=0)
        dm16.append(lax.dynamic_slice_in_dim(full, shard * n_col, n_col, axis=1))
    g_w_mod = jnp.stack([_mm(f"mod_dw_{l}", c16, dm16[l], 'tn', tm=256, tn=768, a_pre=_silu) for l in range(2)])
    dc16 = _mm("mod_dx", dm16[0], w_mod, 'nt', tm=16, tn=512, b_lead=0, epi=lambda acc, i, cv: (acc * _dsilu(cv),),
               epi_args=(c16,), epi_kinds=('mn',))
    dc_all = _gather_all("gather_dc", [dc16])[0]
    g_c_ctx = _sum_lead("sum_dc", dc_all[::2])[8]

    grads = dict(c_ctx=g_c_ctx, norm_g=g_norm_g, w_mod=g_w_mod, b_mod=g_b_mod, pool_w=g_pool_w,
                 pool_scale=g_pool_scale, q_norm_g=g_q_norm, kv_norm_g=g_kv_norm, conv_w=g_conv_w, final_norm_g=g_final)
    names = list(weights)

    scatter0 = chips_start(0, pair0, g_c_ctx)
    upd = {n: _adamw(f"adamw_{n}", weights[n], grads[n].reshape(weights[n].shape), mom_m[n], mom_v[n],
                     scatter0['token']) for n in names if n not in big_names}
    sums0 = landed_sums(0, scatter0, upd["w_mod"][0])

    halves_sum = sums0 + sums1
    swap = _split_start("swap_start", halves_sum + [pltpu.with_memory_space_constraint(lax.empty(s.shape, s.dtype),
                                                                                        pltpu.HBM) for s in halves_sum],
                        len(halves_sum), _sibling_whole_build(len(halves_sum)))
    both = _split_wait("swap_wait", swap, swap['token'])
    south = ci == 0
    swapped = [jnp.where(south, jnp.stack([a, g]), jnp.stack([g, a]))
               for a, g in zip(both[:len(halves_sum)], both[len(halves_sum):])]
    n0 = len(sums0)
    for l, part_l in enumerate((swapped[:n0], swapped[n0:])):
        for nm, s in zip(ffn_names + mixer_names[l], part_l):
            grads[(nm, l)] = s
    for nm in ffn_names:
        grads[nm] = jnp.stack([grads.pop((nm, l)).reshape(weights[nm].shape[1:]) for l in range(2)])
    for l in range(2):
        for nm in mixer_names[l]:
            grads[nm] = grads.pop((nm, l)).reshape(weights[nm].shape)
    upd.update({n: _adamw(f"adamw_{n}", weights[n], grads[n], mom_m[n], mom_v[n]) for n in big_names})
    return (loss, grad_x, *[grads[n].reshape(weights[n].shape) for n in names], *[upd[n][0] for n in names],
            *[upd[n][1] for n in names], *[upd[n][2] for n in names])
```

```python
import functools
import math

import jax
import jax.numpy as jnp
from jax import lax
from jax.experimental import pallas as pl
from jax.experimental.pallas import tpu as pltpu

F32 = jnp.float32
BF = jnp.bfloat16
MESH = pl.DeviceIdType.MESH

N_DEV = 8
N_SHARD = 4
RMS_EPS = 1e-6
N_MOD = 9
POOL_WINDOWS = (2, 4, 8, 16)
QK_NOPE = 64
QK_ROPE = 32
V_HEAD = 64
HEAD_PAD = 128
GRID_W = 64
ROPE_THETA = 10000.0
POOL_PAD = 16
ADAM_LR, ADAM_B1, ADAM_B2, ADAM_EPS, ADAM_WD, ADAM_STEP = 0.001, 0.9, 0.999, 1e-08, 0.01, 10
VMEM_LIMIT = 56 * 1024 * 1024


def _pcall(body, **kw):
    return pl.pallas_call(body, **kw)


def _params(sem=None):
    return pltpu.CompilerParams(dimension_semantics=sem, vmem_limit_bytes=VMEM_LIMIT)


def _pick(n, pref, mult=128):
    best = None
    d = mult
    while d <= min(n, pref):
        if n % d == 0:
            best = d
        d += mult
    return best if best is not None else n


def _silu(z):
    return z * jax.nn.sigmoid(z)


def _dsilu(z):
    s = jax.nn.sigmoid(z)
    return s * (1.0 + z * (1.0 - s))


def _dot(a, b, dims):
    return lax.dot_general(a.astype(BF), b.astype(BF), (dims, ((), ())), preferred_element_type=F32)


NN = ((1,), (0,))
NT = ((1,), (1,))
TN = ((0,), (0,))


ALL_FLIPS = [(kx, ky, kc) for kx in (0, 1) for ky in (0, 1) for kc in (0, 1) if (kx, ky, kc) != (0, 0, 0)]
CHIP_FLIPS = [(1, 0, 0), (0, 1, 0), (1, 1, 0)]
SIBLING = (0, 0, 1)
COMM_SPLIT = 8
SPLIT_MIN_ROWS = 256


def _exchange(name, arrays, plan, lead, whole_src, split=COMM_SPLIT):
    n = len(arrays)
    blk_shapes = [tuple(a.shape) if whole_src else tuple(a.shape[1:]) for a in arrays]
    splits = []
    for shp in blk_shapes:
        s = 1
        while s * 2 <= split and shp[0] % (s * 2) == 0 and (shp[0] // (s * 2)) % 16 == 0 \
                and shp[0] // (s * 2) >= SPLIT_MIN_ROWS:
            s *= 2
        splits.append(s)
    items = plan(0, 0, 0)
    n_items = len(items)
    remote_ids = [k for k, it in enumerate(items) if it[0] is not None]
    local_ids = [k for k, it in enumerate(items) if it[0] is None]
    slots = [(a, s) for s in range(max(splits)) for a in range(n) if s < splits[a]]
    n_slot = len(slots)

    def body(*refs):
        ins, outs = refs[:n], refs[n:2 * n]
        send_sems, recv_sems, loc_sems = refs[2 * n:]
        x, y, c = lax.axis_index("x"), lax.axis_index("y"), lax.axis_index("c")
        plan_here = plan(x, y, c)

        def rows(ref, a, s):
            rc = blk_shapes[a][0] // splits[a]
            return ref.at[pl.ds(s * rc, rc)]

        def make(si, k):
            a, s = slots[si]
            flip, src, dst, _ = plan_here[k]
            base = outs[a] if src[0] == 'out' else ins[a]
            src_ref = rows(base if src[1] is None else base.at[src[1]], a, s)
            dst_ref = rows(outs[a].at[dst], a, s)
            if flip is None:
                return pltpu.make_async_copy(src_ref, dst_ref, loc_sems.at[si * max(1, len(local_ids)) + local_ids.index(k)])
            peer = (1 - x if flip[0] else x, 1 - y if flip[1] else y, 1 - c if flip[2] else c)
            sem = si * len(remote_ids) + remote_ids.index(k)
            return pltpu.make_async_remote_copy(src_ref=src_ref, dst_ref=dst_ref, send_sem=send_sems.at[sem],
                                                recv_sem=recv_sems.at[sem], device_id=peer, device_id_type=MESH)

        copies = {}
        for si in range(n_slot):
            for k in range(n_items):
                if plan_here[k][3] is None:
                    copies[si, k] = make(si, k)
                    copies[si, k].start()
        arrived = set()
        for si in range(n_slot):
            for k in range(n_items):
                after = plan_here[k][3]
                if after is not None:
                    if (si, after) not in arrived:
                        copies[si, after].wait_recv()
                        arrived.add((si, after))
                    copies[si, k] = make(si, k)
                    copies[si, k].start()
        for (si, k), cp in copies.items():
            if plan_here[k][0] is None:
                cp.wait()
            else:
                cp.wait_send()
                if (si, k) not in arrived:
                    cp.wait_recv()

    any_spec = pl.BlockSpec(memory_space=pl.ANY)
    n_rem = max(1, n_slot * len(remote_ids))
    outs = _pcall(
        body, name=name,
        out_shape=[jax.ShapeDtypeStruct((lead,) + s, a.dtype) for s, a in zip(blk_shapes, arrays)],
        in_specs=[any_spec] * n, out_specs=[any_spec] * n,
        scratch_shapes=[pltpu.SemaphoreType.DMA((n_rem,)), pltpu.SemaphoreType.DMA((n_rem,)),
                        pltpu.SemaphoreType.DMA((max(1, n_slot * len(local_ids)),))],
    )(*arrays)
    return list(outs)


def _place(x, y, c):
    return 4 * x + 2 * y + c


def _flip(v, f):
    return 1 - v if f else v


def _gather_all(name, arrays):
    def plan(x, y, c):
        me = _place(x, y, c)
        return [(None, ('in', None), me, None)] + [(f, ('in', None), me, None) for f in ALL_FLIPS]
    return _exchange(name, arrays, plan, N_DEV, True)


HBM_SPEC = pl.BlockSpec(memory_space=pltpu.HBM)
SEM_SPEC = pl.BlockSpec(memory_space=pltpu.SEMAPHORE)
SIDE_EFFECT = pltpu.SideEffectType.DATAFLOW_SIDE_EFFECTING


def _split_start(name, bufs, n_copies, build):
    n = len(bufs)

    def body(*refs):
        for cp in build(refs[:n], refs[n], refs[n + 1]):
            cp.start()
        token = refs[-1]
        token[...] = jnp.zeros_like(token)

    res = _pcall(
        body, name=name,
        out_shape=(pltpu.SemaphoreType.DMA((n_copies,)), pltpu.SemaphoreType.DMA((n_copies,)),
                   *[pltpu.HBM(b.shape, b.dtype) for b in bufs], jax.ShapeDtypeStruct((8, 128), F32)),
        in_specs=[HBM_SPEC] * n,
        out_specs=(SEM_SPEC, SEM_SPEC, *[HBM_SPEC] * n, pl.BlockSpec(memory_space=pltpu.VMEM)),
        input_output_aliases={i: 2 + i for i in range(n)},
        compiler_params=pltpu.CompilerParams(has_side_effects=SIDE_EFFECT),
    )(*[pltpu.with_memory_space_constraint(b, pltpu.HBM) for b in bufs])
    return dict(send=res[0], recv=res[1], bufs=list(res[2:2 + n]), token=res[-1], build=build)


def _split_wait(name, handle, after):
    n = len(handle['bufs'])
    build = handle['build']

    def body(*refs):
        for cp in build(refs[:n], refs[n], refs[n + 1]):
            cp.wait_send()
            cp.wait_recv()

    res = _pcall(
        body, name=name, out_shape=tuple(pltpu.HBM(b.shape, b.dtype) for b in handle['bufs']),
        in_specs=[HBM_SPEC] * n + [SEM_SPEC, SEM_SPEC, pl.BlockSpec(memory_space=pl.ANY)],
        out_specs=tuple([HBM_SPEC] * n), input_output_aliases={i: i for i in range(n)},
        compiler_params=pltpu.CompilerParams(has_side_effects=SIDE_EFFECT),
    )(*handle['bufs'], handle['send'], handle['recv'], after)
    return list(res)


def _landing(lead, arrays):
    return [pltpu.with_memory_space_constraint(lax.empty((lead,) + tuple(a.shape[1:]), a.dtype), pltpu.HBM)
            for a in arrays]


def _copy_list(n, per_array, make):
    def build(refs, send_sems, recv_sems):
        copies = []
        for a in range(n):
            for j in range(per_array):
                src, dst, peer = make(refs, a, j)
                k = a * per_array + j
                copies.append(pltpu.make_async_remote_copy(src_ref=src, dst_ref=dst, send_sem=send_sems.at[k],
                                                           recv_sem=recv_sems.at[k], device_id=peer,
                                                           device_id_type=MESH))
        return copies
    return build


def _mesh_place():
    x, y, c = lax.axis_index("x"), lax.axis_index("y"), lax.axis_index("c")
    return x, y, c, 2 * x + y


def _chips_gather_build(n):
    def make(refs, a, j):
        x, y, c, chip = _mesh_place()
        px, py = _flip(x, CHIP_FLIPS[j][0]), _flip(y, CHIP_FLIPS[j][1])
        return refs[a].at[c], refs[n + a].at[2 * chip + c], (px, py, c)
    return _copy_list(n, len(CHIP_FLIPS), make)


def _chips_scatter_build(n):
    def make(refs, a, j):
        x, y, c, chip = _mesh_place()
        px, py = _flip(x, CHIP_FLIPS[j][0]), _flip(y, CHIP_FLIPS[j][1])
        return refs[a].at[2 * px + py], refs[n + a].at[chip], (px, py, c)
    return _copy_list(n, len(CHIP_FLIPS), make)


def _sibling_forward_build(n):
    def make(refs, a, j):
        x, y, c, _ = _mesh_place()
        blk = 2 * (2 * _flip(x, CHIP_FLIPS[j][0]) + _flip(y, CHIP_FLIPS[j][1])) + c
        return refs[a].at[blk], refs[a].at[blk], (x, y, 1 - c)
    return _copy_list(n, len(CHIP_FLIPS), make)


def _sibling_halves_build(n):
    def make(refs, a, j):
        x, y, c, _ = _mesh_place()
        return refs[a].at[2 * j + 1 - c], refs[n + a].at[j], (x, y, 1 - c)
    return _copy_list(n, N_SHARD, make)


def _sibling_whole_build(n):
    def make(refs, a, j):
        x, y, c, _ = _mesh_place()
        return refs[a], refs[n + a], (x, y, 1 - c)
    return _copy_list(n, 1, make)


def _add_halves(name, send, land):
    _, r, cdim = send.shape
    tr = _pick(r, max(16, (1 << 20) // (cdim * 2)), 16)

    def body(c_ref, own_ref, got_ref, o_ref):
        o_ref[...] = (own_ref[...].astype(F32) + got_ref[...].astype(F32)).astype(BF)

    grid_spec = pltpu.PrefetchScalarGridSpec(
        num_scalar_prefetch=1, grid=(N_SHARD, r // tr),
        in_specs=[pl.BlockSpec((None, tr, cdim), lambda sh, i, cr: (2 * sh + cr[0], i, 0)),
                  pl.BlockSpec((None, tr, cdim), lambda sh, i, cr: (sh, i, 0))],
        out_specs=pl.BlockSpec((None, tr, cdim), lambda sh, i, cr: (sh, i, 0)))
    core = lax.axis_index("c").astype(jnp.int32).reshape(1)
    return _pcall(body, name=name, grid_spec=grid_spec, out_shape=jax.ShapeDtypeStruct((N_SHARD, r, cdim), BF),
                  compiler_params=_params(("arbitrary", "arbitrary")))(core, send, land)


def _sum_lead(name, arr, out_dtype=F32):
    n, r, cdim = arr.shape
    tr = r
    limit = (4 << 20) // (n * cdim * arr.dtype.itemsize)
    if r > limit:
        tr = _pick(r, max(limit, 16), 16)

    def body(x_ref, o_ref):
        acc = x_ref[0].astype(F32)
        for d in range(1, n):
            acc = acc + x_ref[d].astype(F32)
        o_ref[...] = acc.astype(out_dtype)

    return _pcall(body, name=name, grid=(r // tr,),
                  in_specs=[pl.BlockSpec((n, tr, cdim), lambda i: (0, i, 0))],
                  out_specs=pl.BlockSpec((tr, cdim), lambda i: (i, 0)),
                  out_shape=jax.ShapeDtypeStruct((r, cdim), out_dtype),
                  compiler_params=_params(("arbitrary",)))(arr)


def _rows_call(name, fn, n_rows, tm, rows, consts, mod, outs, acc_w=None, h_tiles=None):
    nt = n_rows // tm
    ht = nt if h_tiles is None else h_tiles
    ng = 1 if mod is None else mod.shape[0]
    n_r, n_c, n_o = len(rows), len(consts), len(outs)
    has_mod = mod is not None

    def body(*refs):
        i = pl.program_id(0)
        first = (i % ht) == 0
        row_refs, const_refs = refs[:n_r], refs[n_r:n_r + n_c]
        p = n_r + n_c
        mod_tile = refs[p][...] if has_mod else None
        p += int(has_mod)
        out_refs = refs[p:p + n_o]
        o, acc = fn([r[...] for r in row_refs], [r[...] for r in const_refs], mod_tile)
        for r, v in zip(out_refs, o):
            r[...] = v.astype(r.dtype)
        if acc_w is not None:
            acc_ref = refs[p + n_o]

            @pl.when(first)
            def _():
                acc_ref[...] = jnp.zeros_like(acc_ref)

            for k, v in acc.items():
                acc_ref[k:k + 1, :] += v

    in_specs = [pl.BlockSpec((tm, r.shape[1]), lambda i: (i, 0)) for r in rows]
    in_specs += [pl.BlockSpec(cst.shape, lambda i, nd=cst.ndim: (0,) * nd) for cst in consts]
    args = list(rows) + list(consts)
    if has_mod:
        in_specs.append(pl.BlockSpec((None,) + mod.shape[1:], lambda i: (i // ht, 0, 0)))
        args.append(mod)
    out_shape = [jax.ShapeDtypeStruct((n_rows, w), dt) for w, dt in outs]
    out_specs = [pl.BlockSpec((tm, w), lambda i: (i, 0)) for w, _ in outs]
    if acc_w is not None:
        out_shape.append(jax.ShapeDtypeStruct((ng, 8, acc_w), F32))
        out_specs.append(pl.BlockSpec((None, 8, acc_w), lambda i: (i // ht, 0, 0)))
    res = _pcall(body, name=name, grid=(nt,), in_specs=in_specs, out_specs=out_specs, out_shape=out_shape,
                 compiler_params=_params(("arbitrary",)))(*args)
    return list(res)


def _rms(s):
    r = lax.rsqrt(jnp.mean(s * s, axis=1, keepdims=True) + RMS_EPS)
    return s * r, r


def _rms_bwd(dn, n, r):
    return r * (dn - n * jnp.mean(dn * n, axis=1, keepdims=True))


def _adaln_fwd(name, s, gains, gain_row, mod, k, tm, h_tiles, after=None):
    def fn(rows, consts, m):
        n, _ = _rms(rows[0])
        y = n * consts[0][gain_row:gain_row + 1, :]
        return [y * (1.0 + m[3 * k + 1:3 * k + 2, :]) + m[3 * k:3 * k + 1, :]], {}

    d = s.shape[1]
    consts = [gains] if after is None else [gains, after]
    return _rows_call(name, fn, s.shape[0], tm, [s], consts, mod, [(d, BF)], h_tiles=h_tiles)[0]


def _adaln_bwd(name, s, du, ds_res, gains, gain_row, mod, k, tm, h_tiles):
    def fn(rows, consts, m):
        sv, duv, res = rows
        gain = consts[0][gain_row:gain_row + 1, :]
        n, r = _rms(sv)
        y = n * gain
        dy = duv * (1.0 + m[3 * k + 1:3 * k + 2, :])
        acc = {0: jnp.sum(duv, axis=0, keepdims=True), 1: jnp.sum(duv * y, axis=0, keepdims=True),
               2: jnp.sum(dy * n, axis=0, keepdims=True)}
        return [_rms_bwd(dy * gain, n, r) + res], acc

    d = s.shape[1]
    return _rows_call(name, fn, s.shape[0], tm, [s, du, ds_res], [gains], mod, [(d, F32)], acc_w=d, h_tiles=h_tiles)


def _resid_bwd(name, ds_out, o, mod, k, cst, tm, h_tiles, after=None):
    def fn(rows, consts, m):
        dsv, ov = rows
        gate = m[3 * k + 2:3 * k + 3, :]
        return [cst * gate * dsv], {0: jnp.sum(cst * ov * dsv, axis=0, keepdims=True)}

    d = o.shape[1]
    consts = [] if after is None else [after]
    return _rows_call(name, fn, o.shape[0], tm, [ds_out, o], consts, mod, [(d, BF)], acc_w=d, h_tiles=h_tiles)


def _mm(name, a, b, mode, tm=256, tn=512, out_dtypes=(F32,), epi=None, epi_args=(), epi_kinds=(), a_pre=None,
        b_lead=None):
    bshape = b.shape if b_lead is None else b.shape[1:]
    if mode == 'nn':
        (m, kd), nd = a.shape, bshape[1]
    elif mode == 'nt':
        (m, kd), nd = a.shape, bshape[0]
    else:
        (kd, m), nd = a.shape, bshape[1]
    tm = _pick(m, tm, 16) if m % tm else tm
    tn = _pick(nd, tn, 128) if nd % tn else tn
    dims = {'nn': NN, 'nt': NT, 'tn': TN}[mode]
    n_e, n_o = len(epi_args), len(out_dtypes)

    def body(*refs):
        i = pl.program_id(1)
        av = refs[0][...]
        if a_pre is not None:
            av = a_pre(av)
        acc = _dot(av, refs[1][...], dims)
        res = (acc,) if epi is None else epi(acc, i, *[r[...] for r in refs[2:2 + n_e]])
        for r, v in zip(refs[2 + n_e:], res):
            r[...] = v.astype(r.dtype)

    if mode == 'nn':
        specs = [pl.BlockSpec((tm, kd), lambda j, i: (i, 0)), pl.BlockSpec((kd, tn), lambda j, i: (0, j))]
    elif mode == 'nt':
        specs = [pl.BlockSpec((tm, kd), lambda j, i: (i, 0)), pl.BlockSpec((tn, kd), lambda j, i: (j, 0))]
    else:
        specs = [pl.BlockSpec((kd, tm), lambda j, i: (0, i)), pl.BlockSpec((kd, tn), lambda j, i: (0, j))]
    if b_lead is not None:
        shape2, at2 = specs[1].block_shape, specs[1].index_map
        specs[1] = pl.BlockSpec((None,) + tuple(shape2), lambda j, i: (b_lead,) + tuple(at2(j, i)))
    for arr, kind in zip(epi_args, epi_kinds):
        if kind == 'mn':
            specs.append(pl.BlockSpec((tm, tn), lambda j, i: (i, j)))
        elif kind == 'n':
            specs.append(pl.BlockSpec((1, tn), lambda j, i: (0, j)))
        elif kind == 'mt':
            specs.append(pl.BlockSpec((tm, arr.shape[1]), lambda j, i: (i, 0)))
        else:
            specs.append(pl.BlockSpec(arr.shape, lambda j, i, nd_=arr.ndim: (0,) * nd_))
    res = _pcall(body, name=name, grid=(nd // tn, m // tm), in_specs=specs,
                 out_specs=[pl.BlockSpec((tm, tn), lambda j, i: (i, j))] * n_o,
                 out_shape=[jax.ShapeDtypeStruct((m, nd), dt) for dt in out_dtypes],
                 compiler_params=_params(("arbitrary", "arbitrary")))(a, b, *epi_args)
    return res[0] if n_o == 1 else list(res)


def _row_gate(mod, k3, i, tm, n_lat):
    g0 = mod[0, k3:k3 + 1, :]
    if mod.shape[0] == 1:
        return g0
    rid = i * tm + lax.broadcasted_iota(jnp.int32, (tm, 1), 0)
    return jnp.where(rid < n_lat, g0, mod[1, k3:k3 + 1, :])


def _ffn_up(name, u, wg, wu, base, tm):
    r, d = u.shape
    nch, _, _, fc = wg.shape

    def body(u_ref, wg_ref, wu_ref, a_ref, b_ref, h_ref):
        uv = u_ref[...]
        a = _dot(uv, wg_ref[...], NN)
        b = _dot(uv, wu_ref[...], NN)
        a_ref[...] = a.astype(BF)
        b_ref[...] = b.astype(BF)
        h_ref[...] = (_silu(a) * b).astype(BF)

    chunk = pl.BlockSpec((None, tm, fc), lambda j, i: (j, i, 0))
    return _pcall(body, name=name, grid=(nch, r // tm),
                  in_specs=[pl.BlockSpec((tm, d), lambda j, i: (i, 0)),
                            pl.BlockSpec((None, None, d, fc), lambda j, i: (j, base, 0, 0)),
                            pl.BlockSpec((None, None, d, fc), lambda j, i: (j, base, 0, 0))],
                  out_specs=[chunk] * 3, out_shape=[jax.ShapeDtypeStruct((nch, r, fc), BF)] * 3,
                  compiler_params=_params(("arbitrary", "arbitrary")))(u, wg, wu)


def _ffn_down(name, hid, wd, wd_blk, s, mod, k, n_lat, tm):
    nch, r, fc = hid.shape
    d = wd.shape[2]

    def body(h_ref, w_ref, s_ref, m_ref, so_ref, o_ref, acc_ref):
        i, j = pl.program_id(0), pl.program_id(1)
        part = _dot(h_ref[...], w_ref[...], NN)

        @pl.when(j == 0)
        def _():
            acc_ref[...] = part

        @pl.when(j > 0)
        def _():
            acc_ref[...] += part

        @pl.when(j == nch - 1)
        def _():
            o = acc_ref[...]
            o_ref[...] = o
            so_ref[...] = s_ref[...] + 0.5 * _row_gate(m_ref[...], 3 * k + 2, i, tm, n_lat) * o

    row = pl.BlockSpec((tm, d), lambda i, j: (i, 0))
    return _pcall(body, name=name, grid=(r // tm, nch),
                  in_specs=[pl.BlockSpec((None, tm, fc), lambda i, j: (j, i, 0)),
                            pl.BlockSpec((None, fc, d), lambda i, j: (j, wd_blk, 0)), row,
                            pl.BlockSpec(mod.shape, lambda i, j: (0, 0, 0))],
                  out_specs=[row, row], out_shape=[jax.ShapeDtypeStruct((r, d), F32)] * 2,
                  scratch_shapes=[pltpu.VMEM((tm, d), F32)],
                  compiler_params=_params(("arbitrary", "arbitrary")))(hid, wd, s, mod)


def _ffn_dhid(name, d_o, wd, wd_blk, a, b, tm):
    r, d = d_o.shape
    nch, _, fc = a.shape

    def body(g_ref, w_ref, a_ref, b_ref, da_ref, db_ref):
        dh = _dot(g_ref[...], w_ref[...], NT)
        av, bv = a_ref[...].astype(F32), b_ref[...].astype(F32)
        da_ref[...] = (dh * bv * _dsilu(av)).astype(BF)
        db_ref[...] = (dh * _silu(av)).astype(BF)

    chunk = pl.BlockSpec((None, tm, fc), lambda j, i: (j, i, 0))
    return _pcall(body, name=name, grid=(nch, r // tm),
                  in_specs=[pl.BlockSpec((tm, d), lambda j, i: (i, 0)),
                            pl.BlockSpec((None, fc, d), lambda j, i: (j, wd_blk, 0)), chunk, chunk],
                  out_specs=[chunk] * 2, out_shape=[jax.ShapeDtypeStruct((nch, r, fc), BF)] * 2,
                  compiler_params=_params(("arbitrary", "arbitrary")))(d_o, wd, a, b)


def _ffn_du(name, da, db, wg, wu, base, tm):
    nch, r, fc = da.shape
    d = wg.shape[2]

    def body(da_ref, db_ref, wg_ref, wu_ref, o_ref, acc_ref):
        j = pl.program_id(1)
        part = _dot(da_ref[...], wg_ref[...], NT) + _dot(db_ref[...], wu_ref[...], NT)

        @pl.when(j == 0)
        def _():
            acc_ref[...] = part

        @pl.when(j > 0)
        def _():
            acc_ref[...] += part

        @pl.when(j == nch - 1)
        def _():
            o_ref[...] = acc_ref[...]

    chunk = pl.BlockSpec((None, tm, fc), lambda i, j: (j, i, 0))
    return _pcall(body, name=name, grid=(r // tm, nch),
                  in_specs=[chunk, chunk, pl.BlockSpec((None, None, d, fc), lambda i, j: (j, base, 0, 0)),
                            pl.BlockSpec((None, None, d, fc), lambda i, j: (j, base, 0, 0))],
                  out_specs=pl.BlockSpec((tm, d), lambda i, j: (i, 0)),
                  out_shape=jax.ShapeDtypeStruct((r, d), F32), scratch_shapes=[pltpu.VMEM((tm, d), F32)],
                  compiler_params=_params(("arbitrary", "arbitrary")))(da, db, wg, wu)


def _ffn_dw_in(name, u, dz, tmm, grads, idx):
    r, d = u.shape
    nch, _, fc = dz.shape
    nb = d // tmm

    def body(u_ref, z_ref, g_ref, o_ref):
        o_ref[...] = _dot(u_ref[...], z_ref[...], TN).astype(o_ref.dtype)

    return _pcall(body, name=name, grid=(nch, nb),
                  in_specs=[pl.BlockSpec((r, tmm), lambda j, mi: (0, mi)),
                            pl.BlockSpec((None, r, fc), lambda j, mi: (j, 0, 0)),
                            pl.BlockSpec(memory_space=pl.ANY)],
                  out_specs=pl.BlockSpec((None, tmm, fc), lambda j, mi: (j, idx * nb + mi, 0)),
                  out_shape=jax.ShapeDtypeStruct(grads.shape, grads.dtype), input_output_aliases={2: 0},
                  compiler_params=_params(("arbitrary", "arbitrary")))(u, dz, grads)


def _ffn_dw_down(name, hid, d_o, tn, grads, idx):
    nch, r, fc = hid.shape
    d = d_o.shape[1]

    def body(h_ref, g_ref, acc_ref, o_ref):
        o_ref[...] = _dot(h_ref[...], g_ref[...], TN).astype(o_ref.dtype)

    return _pcall(body, name=name, grid=(nch, d // tn),
                  in_specs=[pl.BlockSpec((None, r, fc), lambda j, ni: (j, 0, 0)),
                            pl.BlockSpec((r, tn), lambda j, ni: (0, ni)),
                            pl.BlockSpec(memory_space=pl.ANY)],
                  out_specs=pl.BlockSpec((None, fc, tn), lambda j, ni: (j, idx, ni)),
                  out_shape=jax.ShapeDtypeStruct(grads.shape, grads.dtype), input_output_aliases={2: 0},
                  compiler_params=_params(("arbitrary", "arbitrary")))(hid, d_o, grads)


def _partner(x):
    n = x.shape[1]
    lane = lax.broadcasted_iota(jnp.int32, x.shape, 1)
    return jnp.where((lane & 15) < 8, pltpu.roll(x, n - 8, 1), pltpu.roll(x, 8, 1))


def _rope(x, ct, st):
    reps = x.shape[1] // ct.shape[1]
    if reps > 1:
        ct, st = jnp.tile(ct, (1, reps)), jnp.tile(st, (1, reps))
    return x * ct + _partner(x) * st


def _rope_t(dy, ct, st):
    reps = dy.shape[1] // ct.shape[1]
    if reps > 1:
        ct, st = jnp.tile(ct, (1, reps)), jnp.tile(st, (1, reps))
    return dy * ct + _partner(dy * st)


def _rope_tables(t_len, g_len, lane0):
    half = QK_ROPE // 4
    pos = jnp.arange(t_len)
    row = (pos // GRID_W).astype(F32)
    col = (pos % GRID_W).astype(F32)
    freqs = jnp.power(ROPE_THETA, -jnp.arange(0, QK_ROPE // 2, 2, dtype=F32) / (QK_ROPE // 2))
    ang_r, ang_c = row[:, None] * freqs, col[:, None] * freqs
    cs = jnp.concatenate([jnp.cos(ang_r)] * 2 + [jnp.cos(ang_c)] * 2, axis=1)
    sn = jnp.concatenate([-jnp.sin(ang_r), jnp.sin(ang_r), -jnp.sin(ang_c), jnp.sin(ang_c)], axis=1)
    assert cs.shape[1] == 4 * half == QK_ROPE
    ct = jnp.ones((t_len + g_len, HEAD_PAD), F32).at[:t_len, lane0:lane0 + QK_ROPE].set(cs)
    st = jnp.zeros((t_len + g_len, HEAD_PAD), F32).at[:t_len, lane0:lane0 + QK_ROPE].set(sn)
    return ct, st


def _attn_fwd(name, q, kp, vp, n_q, q_off, n_k, k_blk, heads, tq, scale):
    qb = q_off // tq

    def body(q_ref, k_ref, v_ref, o_ref, l_ref):
        s = _dot(q_ref[...], k_ref[...], NT) * scale
        m = jnp.max(s, axis=1, keepdims=True)
        p = jnp.exp(s - m)
        l = jnp.sum(p, axis=1, keepdims=True)
        o_ref[...] = (_dot(p, v_ref[...], NN) / l).astype(BF)
        l_ref[...] = jnp.broadcast_to(m + jnp.log(l), l_ref.shape)

    hw = heads * HEAD_PAD
    blk = pl.BlockSpec((tq, HEAD_PAD), lambda h, i: (i, h))
    kv = pl.BlockSpec((n_k, HEAD_PAD), lambda h, i: (k_blk, h))
    return _pcall(body, name=name, grid=(heads, n_q // tq),
                  in_specs=[pl.BlockSpec((tq, HEAD_PAD), lambda h, i: (i + qb, h)), kv, kv],
                  out_specs=[blk, blk],
                  out_shape=[jax.ShapeDtypeStruct((n_q, hw), BF), jax.ShapeDtypeStruct((n_q, hw), F32)],
                  compiler_params=_params(("arbitrary", "arbitrary")))(q, kp, vp)


def _attn_bwd(name, q, kp, vp, cat, dcat, lse, n_q, q_off, n_k, k_blk, heads, tq, scale, col_blk, onto=None):
    qb = q_off // tq

    def body(q_ref, k_ref, v_ref, o_ref, do_ref, l_ref, *rest):
        dq_ref, dk_ref, dv_ref = rest[-3:]
        i = pl.program_id(1)
        qv, kv_, vv = q_ref[...], k_ref[...], v_ref[...]
        dov = do_ref[...]
        s = _dot(qv, kv_, NT) * scale
        p = jnp.exp(s - l_ref[...][:, 0:1])
        dp = _dot(dov, vv, NT)
        delta = jnp.sum(dov * o_ref[...].astype(F32), axis=1, keepdims=True)
        ds = (p * (dp - delta) * scale).astype(BF)
        dq_ref[...] = _dot(ds, kv_, NN)
        dk = _dot(ds, qv, TN)
        dv = _dot(p, dov, TN)

        @pl.when(i == 0)
        def _():
            if onto is None:
                dk_ref[...] = dk
                dv_ref[...] = dv
            else:
                dk_ref[...] = rest[0][...] + dk
                dv_ref[...] = rest[1][...] + dv

        @pl.when(i > 0)
        def _():
            dk_ref[...] += dk
            dv_ref[...] += dv

    hw = heads * HEAD_PAD
    qspec = pl.BlockSpec((tq, HEAD_PAD), lambda h, i: (i + qb, h))
    cspec = pl.BlockSpec((tq, HEAD_PAD), lambda h, i: (i + qb, col_blk + h))
    kv = pl.BlockSpec((n_k, HEAD_PAD), lambda h, i: (k_blk, h))
    blk = pl.BlockSpec((tq, HEAD_PAD), lambda h, i: (i, h))
    if onto is None:
        acc = pl.BlockSpec((n_k, HEAD_PAD), lambda h, i: (0, h))
        return _pcall(body, name=name, grid=(heads, n_q // tq),
                      in_specs=[qspec, kv, kv, cspec, cspec, blk], out_specs=[blk, acc, acc],
                      out_shape=[jax.ShapeDtypeStruct((n_q, hw), F32), jax.ShapeDtypeStruct((n_k, hw), F32),
                                 jax.ShapeDtypeStruct((n_k, hw), F32)],
                      compiler_params=_params(("arbitrary", "arbitrary")))(q, kp, vp, cat, dcat, lse)
    return _pcall(body, name=name, grid=(heads, n_q // tq),
                  in_specs=[qspec, kv, kv, cspec, cspec, blk, kv, kv], out_specs=[blk, kv, kv],
                  out_shape=[jax.ShapeDtypeStruct((n_q, hw), F32)] + [jax.ShapeDtypeStruct(t.shape, F32) for t in onto],
                  input_output_aliases={6: 1, 7: 2},
                  compiler_params=_params(("arbitrary", "arbitrary")))(q, kp, vp, cat, dcat, lse, *onto)


def _shift(x, k):
    return pltpu.roll(x, k % x.shape[0], 0)


def _window_sum(v, w, mirrored):
    n, gd = v.shape
    pad = jnp.zeros((POOL_PAD, gd), F32)
    e = jnp.concatenate([pad, v, pad], axis=0)
    acc = e + _shift(e, -1 if mirrored else 1)
    step = 1
    while 2 * step < w:
        acc = _shift(acc, step) + _shift(acc, -step)
        step *= 2
    return acc[POOL_PAD:POOL_PAD + n]


def _window_count(n, w):
    t = lax.broadcasted_iota(jnp.int32, (n, 1), 0)
    lo = jnp.maximum(t - w // 2, 0)
    hi = jnp.minimum(t + (w - w // 2 - 1), n - 1)
    return (hi - lo + 1).astype(F32)


def _pool_fwd(name, u, pool_w, scale):
    n, pd = u.shape
    ng = len(POOL_WINDOWS)
    gd = pd // ng

    def body(u_ref, w_ref, s_ref, y_ref):
        for g, w in enumerate(POOL_WINDOWS):
            sl = slice(g * gd, (g + 1) * gd)
            ug = u_ref[:, sl]
            p = _window_sum(ug, w, False) / _window_count(n, w) - ug
            y_ref[:, sl] = (_dot(p, w_ref[g], NN) * s_ref[:, sl]).astype(BF)

    return _pcall(body, name=name, out_shape=jax.ShapeDtypeStruct((n, pd), BF),
                  compiler_params=_params())(u, pool_w, scale)


def _pool_bwd(name, u, dcat, pool_w, scale, row_off):
    n, pd = u.shape
    ng = len(POOL_WINDOWS)
    gd = pd // ng

    def body(u_ref, dy_ref, w_ref, s_ref, du_ref, dw_ref, ds_ref):
        ds_ref[...] = jnp.zeros_like(ds_ref)
        for g, w in enumerate(POOL_WINDOWS):
            sl = slice(g * gd, (g + 1) * gd)
            ug, dy, wg = u_ref[:, sl], dy_ref[:, sl], w_ref[g]
            cnt = _window_count(n, w)
            p = _window_sum(ug, w, False) / cnt - ug
            ds_ref[0:1, sl] = jnp.sum(dy * _dot(p, wg, NN), axis=0, keepdims=True)
            dys = dy * s_ref[:, sl]
            dw_ref[g] = _dot(p, dys, TN)
            dp = _dot(dys, wg, NT)
            du_ref[:, sl] = (_window_sum(dp / cnt, w, True) - dp).astype(BF)

    rb = row_off // n
    return _pcall(body, name=name, grid=(1,),
                  in_specs=[pl.BlockSpec((n, pd), lambda i: (0, 0)), pl.BlockSpec((n, pd), lambda i: (rb, 0)),
                            pl.BlockSpec(pool_w.shape, lambda i: (0, 0, 0)), pl.BlockSpec(scale.shape, lambda i: (0, 0))],
                  out_specs=[pl.BlockSpec((n, pd), lambda i: (0, 0)), pl.BlockSpec((ng, gd, gd), lambda i: (0, 0, 0)),
                             pl.BlockSpec((8, pd), lambda i: (0, 0))],
                  out_shape=[jax.ShapeDtypeStruct((n, pd), BF), jax.ShapeDtypeStruct((ng, gd, gd), F32),
                             jax.ShapeDtypeStruct((8, pd), F32)],
                  compiler_params=_params(("arbitrary",)))(u, dcat, pool_w, scale)


def _edge_shift(z, k):
    n = z.shape[0]
    t = lax.broadcasted_iota(jnp.int32, (n, 1), 0)
    keep = (t >= k) if k > 0 else (t < n + k)
    return jnp.where(keep, pltpu.roll(z, k % n, 0), 0.0)


def _conv_fwd(name, p3, cw, tc):
    n, cd = p3.shape[0], p3.shape[1] // 3
    nb = cd // tc

    def body(b_ref, c_ref, v_ref, w_ref, y_ref):
        z = c_ref[...] * v_ref[...]
        w = w_ref[...]
        zc = w[0:1] * _edge_shift(z, 1) + w[1:2] * z + w[2:3] * _edge_shift(z, -1)
        y_ref[...] = (b_ref[...] * zc).astype(BF)

    return _pcall(body, name=name, grid=(nb,),
                  in_specs=[pl.BlockSpec((n, tc), lambda j: (0, j)), pl.BlockSpec((n, tc), lambda j: (0, nb + j)),
                            pl.BlockSpec((n, tc), lambda j: (0, 2 * nb + j)), pl.BlockSpec((3, tc), lambda j: (0, j))],
                  out_specs=pl.BlockSpec((n, tc), lambda j: (0, j)), out_shape=jax.ShapeDtypeStruct((n, cd), BF),
                  compiler_params=_params(("arbitrary",)))(p3, p3, p3, cw)


def _conv_bwd(name, p3, cw, dy, tc):
    n, cd = dy.shape
    nb = cd // tc

    def body(b_ref, c_ref, v_ref, w_ref, dy_ref, dp_ref, dw_ref):
        cv, vv, w, dyv = c_ref[...], v_ref[...], w_ref[...], dy_ref[...]
        z = cv * vv
        zl, zr = _edge_shift(z, 1), _edge_shift(z, -1)
        zc = w[0:1] * zl + w[1:2] * z + w[2:3] * zr
        dzc = dyv * b_ref[...]
        dz = w[0:1] * _edge_shift(dzc, -1) + w[1:2] * dzc + w[2:3] * _edge_shift(dzc, 1)
        dp_ref[0] = (dyv * zc).astype(BF)
        dp_ref[1] = (dz * vv).astype(BF)
        dp_ref[2] = (dz * cv).astype(BF)
        dw_ref[...] = jnp.zeros_like(dw_ref)
        dw_ref[0:1, :] = jnp.sum(dzc * zl, axis=0, keepdims=True)
        dw_ref[1:2, :] = jnp.sum(dzc * z, axis=0, keepdims=True)
        dw_ref[2:3, :] = jnp.sum(dzc * zr, axis=0, keepdims=True)

    col = pl.BlockSpec((n, tc), lambda j: (0, j))
    return _pcall(body, name=name, grid=(nb,),
                  in_specs=[col, pl.BlockSpec((n, tc), lambda j: (0, nb + j)),
                            pl.BlockSpec((n, tc), lambda j: (0, 2 * nb + j)), pl.BlockSpec((3, tc), lambda j: (0, j)), col],
                  out_specs=[pl.BlockSpec((3, n, tc), lambda j: (0, 0, j)), pl.BlockSpec((8, tc), lambda j: (0, j))],
                  out_shape=[jax.ShapeDtypeStruct((3, n, cd), BF), jax.ShapeDtypeStruct((8, cd), F32)],
                  compiler_params=_params(("arbitrary",)))(p3, p3, p3, cw, dy)


def _conv_din(name, dp3, w_in, tm):
    _, n, cd = dp3.shape
    d = w_in.shape[0]

    def body(a_ref, w_ref, o_ref, acc_ref):
        j = pl.program_id(1)
        part = _dot(a_ref[...], w_ref[...], NT)

        @pl.when(j == 0)
        def _():
            acc_ref[...] = part

        @pl.when(j > 0)
        def _():
            acc_ref[...] += part

        @pl.when(j == 2)
        def _():
            o_ref[...] = acc_ref[...]

    return _pcall(body, name=name, grid=(n // tm, 3),
                  in_specs=[pl.BlockSpec((None, tm, cd), lambda i, j: (j, i, 0)),
                            pl.BlockSpec((d, cd), lambda i, j: (0, j))],
                  out_specs=pl.BlockSpec((tm, d), lambda i, j: (i, 0)), out_shape=jax.ShapeDtypeStruct((n, d), F32),
                  scratch_shapes=[pltpu.VMEM((tm, d), F32)],
                  compiler_params=_params(("arbitrary", "arbitrary")))(dp3, w_in)


def _conv_dw_in(name, u, dp3, tmm, tn):
    n, d = u.shape
    cd = dp3.shape[2]
    nb = cd // tn

    def body(u_ref, z_ref, o_ref):
        o_ref[...] = _dot(u_ref[...], z_ref[...], TN)

    return _pcall(body, name=name, grid=(3 * nb, d // tmm),
                  in_specs=[pl.BlockSpec((n, tmm), lambda j, mi: (0, mi)),
                            pl.BlockSpec((None, n, tn), lambda j, mi: (j // nb, 0, j % nb))],
                  out_specs=pl.BlockSpec((tmm, tn), lambda j, mi: (mi, j)),
                  out_shape=jax.ShapeDtypeStruct((d, 3 * cd), F32),
                  compiler_params=_params(("arbitrary", "arbitrary")))(u, dp3)


def _loss_head(name, h, target, gain, tm):
    d = h.shape[1]

    def fn(rows, consts, m):
        hv, tv = rows
        g = consts[0][0:1, :]
        n, r = _rms(hv)
        err = n * g - tv
        dy = err / d
        loss = 0.5 * jnp.sum(err * err) / d
        acc = {0: jnp.sum(dy * n, axis=0, keepdims=True), 1: jnp.full((1, d), loss, F32)}
        return [_rms_bwd(dy * g, n, r)], acc

    return _rows_call(name, fn, h.shape[0], tm, [h, target], [gain], None, [(d, F32)], acc_w=d)


def _adamw(name, w, g, m, v, after=None):
    shape = w.shape
    if w.ndim == 1:
        shape2 = (1,) + shape
        res = _adamw(name, *[t.reshape(shape2) for t in (w, g, m, v)], after=after)
        return [t.reshape(shape) for t in res]
    if shape[-1] % 128 and shape[-2] % 128 == 0:
        res = _adamw(name, *[jnp.swapaxes(t, -1, -2) for t in (w, g, m, v)], after=after)
        return [jnp.swapaxes(t, -1, -2) for t in res]
    lead, (r, cdim) = shape[:-2], shape[-2:]
    tr = r
    if r * cdim * 4 > (3 << 19):
        tr = _pick(r, max(8, (3 << 19) // (cdim * 4)), 8)
    c1 = 1.0 / (1.0 - ADAM_B1 ** ADAM_STEP)
    c2 = 1.0 / (1.0 - ADAM_B2 ** ADAM_STEP)
    nl = len(lead)

    def body(w_ref, g_ref, m_ref, v_ref, *rest):
        d_ref, nm_ref, nv_ref = rest[-3:]
        gv = g_ref[...]
        nm = ADAM_B1 * m_ref[...] + (1.0 - ADAM_B1) * gv
        nv = ADAM_B2 * v_ref[...] + (1.0 - ADAM_B2) * (gv * gv)
        nm_ref[...] = nm
        nv_ref[...] = nv
        d_ref[...] = -ADAM_LR * ((nm * c1) / (jnp.sqrt(nv * c2) + ADAM_EPS) + ADAM_WD * w_ref[...])

    spec = pl.BlockSpec((None,) * nl + (tr, cdim), lambda *idx: idx + (0,))
    extra = [] if after is None else [after]
    res = _pcall(body, name=name, grid=lead + (r // tr,),
                 in_specs=[spec] * 4 + [pl.BlockSpec(memory_space=pl.ANY)] * len(extra), out_specs=[spec] * 3,
                 out_shape=[jax.ShapeDtypeStruct(shape, F32)] * 3,
                 compiler_params=_params(("arbitrary",) * (nl + 1)))(w, g, m, v, *extra)
    return list(res)


def _ffn_half_fwd(tag, s, gains, mod, k, wts, n_lat, tm, h_tiles, tm_big, after=None):
    wg, wu, wd, idx = wts
    u = _adaln_fwd(f"adaln_{tag}", s, gains, k, mod, k, tm, h_tiles, after)
    a, b, hid = _ffn_up(f"ffn_up_{tag}", u, wg, wu, idx, tm)
    s_out, o = _ffn_down(f"ffn_down_{tag}", hid, wd, idx, s, mod, k, n_lat, tm_big)
    return s_out, (s, u, a, b, hid, o)


def _ffn_half_bwd(tag, ds_out, saved, gains, mod, k, wts, big_grads, tm, h_tiles, tm_big, after=None):
    wg, wu, wd, idx = wts
    g_gate, g_up, g_down = big_grads
    s, u, a, b, hid, o = saved
    d_o, acc_g = _resid_bwd(f"resid_bwd_{tag}", ds_out, o, mod, k, 0.5, tm, h_tiles, after)
    da, db = _ffn_dhid(f"ffn_dhid_{tag}", d_o, wd, idx, a, b, tm)
    du = _ffn_du(f"ffn_du_{tag}", da, db, wg, wu, idx, tm_big)
    d = u.shape[1]
    g_gate = _ffn_dw_in(f"ffn_dwg_{tag}", u, da, _pick(d, 256), g_gate, idx)
    g_up = _ffn_dw_in(f"ffn_dwu_{tag}", u, db, _pick(d, 256), g_up, idx)
    g_down = _ffn_dw_down(f"ffn_dwd_{tag}", hid, d_o, _pick(d, 512), g_down, idx)
    ds, acc_n = _adaln_bwd(f"adaln_bwd_{tag}", s, du, ds_out, gains, k, mod, k, tm, h_tiles)
    return ds, (g_gate, g_up, g_down), (acc_n[:, 0], acc_n[:, 1], acc_g[:, 0]), jnp.sum(acc_n[:, 2], axis=0)


def kernel(x, c, ctx, c_ctx, norm_g, w_mod, b_mod, ffn_w_gate, ffn_w_up, ffn_w_down, ab_w_in, pool_w, pool_scale, q_norm_g, w_uq, kv_norm_g, w_ukv, ab_w_out, conv_w_in, conv_w, conv_w_out, final_norm_g, loss_target, m_c_ctx, m_norm_g, m_w_mod, m_b_mod, m_ffn_w_gate, m_ffn_w_up, m_ffn_w_down, m_ab_w_in, m_pool_w, m_pool_scale, m_q_norm_g, m_w_uq, m_kv_norm_g, m_w_ukv, m_ab_w_out, m_conv_w_in, m_conv_w, m_conv_w_out, m_final_norm_g, v_c_ctx, v_norm_g, v_w_mod, v_b_mod, v_ffn_w_gate, v_ffn_w_up, v_ffn_w_down, v_ab_w_in, v_pool_w, v_pool_scale, v_q_norm_g, v_w_uq, v_kv_norm_g, v_w_ukv, v_ab_w_out, v_conv_w_in, v_conv_w, v_conv_w_out, v_final_norm_g):
    weights = dict(c_ctx=c_ctx, norm_g=norm_g, w_mod=w_mod, b_mod=b_mod, ffn_w_gate=ffn_w_gate, ffn_w_up=ffn_w_up,
                   ffn_w_down=ffn_w_down, ab_w_in=ab_w_in, pool_w=pool_w, pool_scale=pool_scale, q_norm_g=q_norm_g,
                   w_uq=w_uq, kv_norm_g=kv_norm_g, w_ukv=w_ukv, ab_w_out=ab_w_out, conv_w_in=conv_w_in, conv_w=conv_w,
                   conv_w_out=conv_w_out, final_norm_g=final_norm_g)
    mom_m = dict(c_ctx=m_c_ctx, norm_g=m_norm_g, w_mod=m_w_mod, b_mod=m_b_mod, ffn_w_gate=m_ffn_w_gate,
                 ffn_w_up=m_ffn_w_up, ffn_w_down=m_ffn_w_down, ab_w_in=m_ab_w_in, pool_w=m_pool_w,
                 pool_scale=m_pool_scale, q_norm_g=m_q_norm_g, w_uq=m_w_uq, kv_norm_g=m_kv_norm_g, w_ukv=m_w_ukv,
                 ab_w_out=m_ab_w_out, conv_w_in=m_conv_w_in, conv_w=m_conv_w, conv_w_out=m_conv_w_out,
                 final_norm_g=m_final_norm_g)
    mom_v = dict(c_ctx=v_c_ctx, norm_g=v_norm_g, w_mod=v_w_mod, b_mod=v_b_mod, ffn_w_gate=v_ffn_w_gate,
                 ffn_w_up=v_ffn_w_up, ffn_w_down=v_ffn_w_down, ab_w_in=v_ab_w_in, pool_w=v_pool_w,
                 pool_scale=v_pool_scale, q_norm_g=v_q_norm_g, w_uq=v_w_uq, kv_norm_g=v_kv_norm_g, w_ukv=v_w_ukv,
                 ab_w_out=v_ab_w_out, conv_w_in=v_conv_w_in, conv_w=v_conv_w, conv_w_out=v_conv_w_out,
                 final_norm_g=v_final_norm_g)

    t_len, d = x.shape[1], x.shape[2]
    g_len = ctx.shape[1]
    r_len = t_len + g_len
    fc = ffn_w_gate.shape[3]
    heads = d // 128
    pool_dim = d // 2
    q_rank, kv_rank = q_norm_g.shape[1], kv_norm_g.shape[1]
    hw = heads * HEAD_PAD
    attn_scale = 1.0 / math.sqrt(QK_NOPE + QK_ROPE)
    kvr_w = kv_rank + HEAD_PAD
    in_w = pool_dim + q_rank + kvr_w
    tm = 256 if g_len % 256 == 0 else g_len
    assert t_len % tm == 0 and g_len % tm == 0 and t_len % g_len == 0 and pool_dim % 128 == 0
    h_tiles = t_len // tm
    tm_l0 = _pick(r_len, 768, tm)
    tm_l1 = _pick(t_len, 1024, tm)

    xi, yi, ci = lax.axis_index("x"), lax.axis_index("y"), lax.axis_index("c")
    me = 4 * xi + 2 * yi + ci
    shard = 2 * xi + yi

    def halves(w):
        return w.astype(BF).reshape(2, -1, w.shape[-1])

    ffn_names = ["ffn_w_gate", "ffn_w_up", "ffn_w_down"]
    mixer_names = [["ab_w_in", "w_uq", "w_ukv", "ab_w_out"], ["conv_w_in", "conv_w_out"]]
    big_names = ffn_names + mixer_names[0] + mixer_names[1]

    def stage_names(k):
        return mixer_names[k // 3] if k % 3 == 1 else ffn_names

    def stage_halves(k):
        l, f = k // 3, (k % 3) // 2
        if k % 3 == 1:
            return [halves(weights[nm]) for nm in mixer_names[l]]
        return [halves(weights[nm][l, f]) for nm in ffn_names]

    small = jnp.concatenate([norm_g.reshape(6, -1), conv_w[0]], axis=0)
    small = jnp.pad(small, ((0, 7), (0, 0)))
    c_row = jnp.pad(c, ((0, 7), (0, 0)))
    small_all, c_all = _gather_all("gather_small", [small, c_row])
    small_full = small_all[::2].transpose(1, 0, 2).reshape(16, d)
    gains = [jnp.pad(small_full[3 * l:3 * l + 3], ((0, 5), (0, 0))) for l in range(2)]
    conv_w_full = small_full[6:9]
    c16 = jnp.concatenate([c_all[:, 0], c_ctx[None], jnp.zeros((7, d), F32)], axis=0)

    n_col = w_mod.shape[2]
    b_sh = lax.dynamic_slice_in_dim(b_mod, shard * n_col, n_col, axis=1)
    m_sh = [_mm(f"mod_fwd_{l}", c16, w_mod, 'nn', tm=16, tn=768, a_pre=_silu, b_lead=l,
                epi=lambda acc, i, bv: (acc + bv,), epi_args=(b_sh[l:l + 1],), epi_kinds=('n',)) for l in range(2)]
    m_all = _gather_all("gather_mod", [jnp.concatenate(m_sh, axis=0)])[0]
    m_full = m_all[::2].reshape(N_SHARD, 2, 16, n_col).transpose(1, 2, 0, 3).reshape(2, 16, N_MOD * d)
    mod_h = [jnp.pad(lax.dynamic_index_in_dim(m_full[l], me, 0, keepdims=False).reshape(N_MOD, d), ((0, 7), (0, 0)))
             for l in range(2)]
    mod_g0 = jnp.pad(m_full[0, 8].reshape(N_MOD, d), ((0, 7), (0, 0)))
    mods = [jnp.stack([mod_h[0], mod_g0]), mod_h[1][None]]

    def gather_start(k, dep):
        own = lax.optimization_barrier((tuple(stage_halves(k)), dep))[0]
        n = len(own)
        return _split_start(f"gather_start_s{k}", list(own) + _landing(N_DEV, own), n * len(CHIP_FLIPS),
                            _chips_gather_build(n))

    def stage_weights(k, handle, after):
        bufs = _split_wait(f"gather_wait_s{k}", handle, after)
        n = len(bufs) // 2
        own = bufs[:n]
        fwd = _split_start(f"forward_start_s{k}", bufs[n:], n * len(CHIP_FLIPS), _sibling_forward_build(n))
        nxt = gather_start(k + 1, fwd['token']) if k + 1 < 6 else None
        landed = _split_wait(f"forward_wait_s{k}", fwd, fwd['token'])
        full = [lax.dynamic_update_slice_in_dim(z, a, 2 * shard, 0) for z, a in zip(landed, own)]
        gw = {nm: g.reshape(N_SHARD, 2 * g.shape[1], g.shape[2]) for nm, g in zip(stage_names(k), full)}
        return gw, nxt, (fwd['token'] if nxt is None else nxt['token'])

    def ffn_weights(gw):
        return gw["ffn_w_gate"].reshape(N_SHARD, 1, d, fc), gw["ffn_w_up"].reshape(N_SHARD, 1, d, fc), \
            gw["ffn_w_down"], 0

    gather0 = gather_start(0, m_all)
    gw_s0, gather1, tok0 = stage_weights(0, gather0, gather0['token'])
    ffn_w = [[ffn_weights(gw_s0), None], [None, None]]

    s0 = jnp.concatenate([x[0], ctx[0]], axis=0)
    s1, sav_f00 = _ffn_half_fwd("l0a", s0, gains[0], mods[0], 0, ffn_w[0][0], t_len, tm, h_tiles, tm_l0, tok0)

    gw_s1, gather2, tok1 = stage_weights(1, gather1, s1)
    w_out_full = gw_s1["ab_w_out"].reshape(-1, d)
    w_uq_full = gw_s1["w_uq"].reshape(q_rank, heads * (QK_NOPE + QK_ROPE))
    w_ukv_full = gw_s1["w_ukv"].transpose(1, 0, 2).reshape(kv_rank, heads * (QK_NOPE + V_HEAD))
    w_in_full = gw_s1["ab_w_in"].transpose(1, 0, 2).reshape(d, -1)

    wq_p = jnp.pad(w_uq_full.reshape(q_rank, heads, QK_NOPE + QK_ROPE),
                   ((0, 0), (0, 0), (0, HEAD_PAD - QK_NOPE - QK_ROPE))).reshape(q_rank, hw)
    ukv3 = w_ukv_full.reshape(kv_rank, heads, QK_NOPE + V_HEAD)
    wk_top = jnp.pad(ukv3[..., :QK_NOPE], ((0, 0), (0, 0), (0, HEAD_PAD - QK_NOPE))).reshape(kv_rank, hw)
    wv_top = jnp.pad(ukv3[..., QK_NOPE:], ((0, 0), (0, 0), (0, HEAD_PAD - V_HEAD))).reshape(kv_rank, hw)
    src_row = lax.broadcasted_iota(jnp.int32, (HEAD_PAD, hw), 0)
    dst_lane = lax.broadcasted_iota(jnp.int32, (HEAD_PAD, hw), 1) % HEAD_PAD
    spread = ((src_row < QK_ROPE) & (dst_lane == src_row + QK_NOPE)).astype(BF)
    wk_ext = jnp.concatenate([wk_top, spread], axis=0)
    wv_ext = jnp.concatenate([wv_top, jnp.zeros((HEAD_PAD, hw), BF)], axis=0)
    w_in_pool = w_in_full[:, :pool_dim]
    w_in_q = w_in_full[:, pool_dim:pool_dim + q_rank]
    w_in_kvr = jnp.pad(w_in_full[:, pool_dim + q_rank:], ((0, 0), (0, HEAD_PAD - QK_ROPE)))
    w_out_attn = jnp.pad(w_out_full[pool_dim:].reshape(heads, V_HEAD, d),
                         ((0, 0), (0, HEAD_PAD - V_HEAD), (0, 0))).reshape(hw, d)
    w_out_p = jnp.concatenate([w_out_full[:pool_dim], w_out_attn], axis=0)

    u_mix = _adaln_fwd("adaln_l0m", s1, gains[0], 1, mods[0], 1, tm, h_tiles, tok1)
    p_pool = _mm("in_pool", u_mix, w_in_pool, 'nn', tm=tm, tn=pool_dim)
    p_q = _mm("in_q", u_mix, w_in_q, 'nn', tm=tm, tn=q_rank)
    p_kvr = _mm("in_kvr", u_mix, w_in_kvr, 'nn', tm=tm, tn=kvr_w)
    qg = jnp.pad(q_norm_g, ((0, 7), (0, 0)))
    kvg = jnp.pad(kv_norm_g, ((0, 7), (0, 0)))
    tq_c, tq_s = _rope_tables(t_len, g_len, QK_NOPE)
    tk_c, tk_s = _rope_tables(t_len, g_len, 0)

    def qn_fn(rows, consts, m):
        n, _ = _rms(rows[0])
        return [n * consts[0][0:1, :]], {}

    qn = _rows_call("q_norm", qn_fn, r_len, tm, [p_q], [qg], None, [(q_rank, BF)])[0]
    q_r = _mm("q_up", qn, wq_p, 'nn', tm=tm, tn=hw, out_dtypes=(BF,),
              epi=lambda acc, i, ct, st: (_rope(acc, ct, st),), epi_args=(tq_c, tq_s), epi_kinds=('mt', 'mt'))

    def kvn_fn(rows, consts, m):
        pv, ct, st = rows
        n, _ = _rms(pv[:, :kv_rank])
        return [jnp.concatenate([n * consts[0][0:1, :], _rope(pv[:, kv_rank:], ct, st)], axis=1)], {}

    kvn = _rows_call("kv_norm", kvn_fn, r_len, tm, [p_kvr, tk_c, tk_s], [kvg], None, [(kvr_w, BF)])[0]
    k_p = _mm("k_up", kvn, wk_ext, 'nn', tm=tm, tn=hw, out_dtypes=(BF,))
    v_p = _mm("v_up", kvn, wv_ext, 'nn', tm=tm, tn=hw, out_dtypes=(BF,))
    o_h, lse_h = _attn_fwd("attn_h", q_r, k_p, v_p, t_len, 0, r_len, 0, heads, tm, attn_scale)
    o_g, lse_g = _attn_fwd("attn_g", q_r, k_p, v_p, g_len, t_len, g_len, t_len // g_len, heads, tm, attn_scale)
    y_h = _pool_fwd("pool_h", p_pool[:t_len], pool_w[0], pool_scale)
    y_g = _pool_fwd("pool_g", p_pool[t_len:], pool_w[0], pool_scale)
    cat = jnp.concatenate([jnp.concatenate([y_h, y_g], axis=0), jnp.concatenate([o_h, o_g], axis=0)], axis=1)

    def resid_epi(k3, n_lat, tmr):
        def epi(acc, i, sv, mv):
            return sv + _row_gate(mv, k3, i, tmr, n_lat) * acc, acc
        return epi

    s2, o_mix0 = _mm("mix_out_l0", cat, w_out_p, 'nn', tm=tm, tn=d, out_dtypes=(F32, F32),
                     epi=resid_epi(5, t_len, tm), epi_args=(s1, mods[0]), epi_kinds=('mn', 'w'))
    gw_s2, gather3, tok2 = stage_weights(2, gather2, s2)
    ffn_w[0][1] = ffn_weights(gw_s2)
    s3, sav_f01 = _ffn_half_fwd("l0b", s2, gains[0], mods[0], 2, ffn_w[0][1], t_len, tm, h_tiles, tm_l0, tok2)

    gw_s3, gather4, tok3 = stage_weights(3, gather3, s3)
    ffn_w[1][0] = ffn_weights(gw_s3)
    tml = 256 if t_len % 256 == 0 else tm
    h3 = s3[:t_len]
    h4, sav_f10 = _ffn_half_fwd("l1a", h3, gains[1], mods[1], 0, ffn_w[1][0], t_len, tml, None, tm_l1, tok3)
    gw_s4, gather5, tok4 = stage_weights(4, gather4, h4)
    cw_out_full = gw_s4["conv_w_out"].reshape(-1, d)
    cw_in_full = gw_s4["conv_w_in"].transpose(1, 0, 2).reshape(d, -1)
    u_cv = _adaln_fwd("adaln_l1m", h4, gains[1], 1, mods[1], 1, tml, None, tok4)
    p3 = _mm("conv_in", u_cv, cw_in_full, 'nn', tm=tml, tn=512)
    cwp = conv_w_full
    tc = _pick(d, 256)
    y_cv = _conv_fwd("conv_fwd", p3, cwp, tc)
    h5, o_mix1 = _mm("mix_out_l1", y_cv, cw_out_full, 'nn', tm=tml, tn=d, out_dtypes=(F32, F32),
                     epi=resid_epi(5, t_len, tml), epi_args=(h4, mods[1]), epi_kinds=('mn', 'w'))
    gw_s5, _, tok5 = stage_weights(5, gather5, h5)
    ffn_w[1][1] = ffn_weights(gw_s5)
    h6, sav_f11 = _ffn_half_fwd("l1b", h5, gains[1], mods[1], 2, ffn_w[1][1], t_len, tml, None, tm_l1, tok5)

    fg = jnp.pad(final_norm_g[None], ((0, 7), (0, 0)))
    dh6, acc_loss = _loss_head("loss_head", h6, loss_target[0], fg, tml)
    d_final_g = acc_loss[0, 0]

    dgain = [[None] * 3 for _ in range(2)]
    dmod = [[None] * N_MOD for _ in range(2)]

    def put(l, k, triple):
        dmod[l][3 * k], dmod[l][3 * k + 1], dmod[l][3 * k + 2] = triple

    def empty_ffn_grads():
        return (lax.empty((N_SHARD, d, fc), BF), lax.empty((N_SHARD, d, fc), BF), lax.empty((N_SHARD, fc, d), BF))

    def by_shard_rows(g):
        return g.reshape(N_SHARD, -1, g.shape[-1])

    def by_shard_cols(g):
        return g.reshape(g.shape[0], N_SHARD, -1).transpose(1, 0, 2)

    def pair_start(k, big):
        send = [b.astype(BF).reshape(N_DEV, b.shape[1] // 2, b.shape[2]) for b in big]
        n = len(send)
        return _split_start(f"grads_pair_start_s{k}", send + _landing(N_SHARD, send), n * N_SHARD,
                            _sibling_halves_build(n))

    def chips_start(k, handle, after):
        bufs = _split_wait(f"grads_pair_wait_s{k}", handle, after)
        n = len(bufs) // 2
        pre = [_add_halves(f"grads_add_s{k}_{nm}", s, z) for nm, s, z in zip(stage_names(k), bufs[:n], bufs[n:])]
        return _split_start(f"grads_start_s{k}", pre + _landing(N_SHARD, pre), n * len(CHIP_FLIPS),
                            _chips_scatter_build(n))

    def landed_sums(k, handle, after):
        bufs = _split_wait(f"grads_wait_s{k}", handle, after)
        n = len(bufs) // 2
        landed = [lax.dynamic_update_slice_in_dim(z, lax.dynamic_slice_in_dim(p, shard, 1, 0), shard, 0)
                  for p, z in zip(bufs[:n], bufs[n:])]
        return [_sum_lead(f"sum_grads_s{k}_{nm}", z) for nm, z in zip(stage_names(k), landed)]

    pair, scatter, sums = [None] * 6, [None] * 6, [None] * 6
    dh5, ffn_g, tr, dgain[1][2] = _ffn_half_bwd("l1b", dh6, sav_f11, gains[1], mods[1], 2, ffn_w[1][1],
                                                empty_ffn_grads(), tml, None, tm_l1)
    put(1, 2, tr)
    pair[5] = pair_start(5, list(ffn_g))
    d_o1, acc_g1 = _resid_bwd("resid_bwd_l1m", dh5, o_mix1, mods[1], 1, 1.0, tml, None, pair[5]['token'])
    dy_cv = _mm("mix_out_l1_dx", d_o1, cw_out_full, 'nt', tm=tml, tn=d)
    d_cw_out = _mm("mix_out_l1_dw", y_cv, d_o1, 'tn', tm=256, tn=512)
    dp3, d_cw = _conv_bwd("conv_bwd", p3, cwp, dy_cv, tc)
    du_cv = _conv_din("conv_in_dx", dp3, cw_in_full, tml)
    d_cw_in = _conv_dw_in("conv_in_dw", u_cv, dp3, _pick(d, 256), _pick(d, 512))
    dh4, acc_n1 = _adaln_bwd("adaln_bwd_l1m", h4, du_cv, dh5, gains[1], 1, mods[1], 1, tml, None)
    put(1, 1, (acc_n1[:, 0], acc_n1[:, 1], acc_g1[:, 0]))
    dgain[1][1] = acc_n1[0, 2]
    scatter[5] = chips_start(5, pair[5], dh4)
    pair[4] = pair_start(4, [by_shard_cols(d_cw_in), by_shard_rows(d_cw_out)])
    dh3, ffn_g, tr, dgain[1][0] = _ffn_half_bwd("l1a", dh4, sav_f10, gains[1], mods[1], 0, ffn_w[1][0],
                                                empty_ffn_grads(), tml, None, tm_l1,
                                                scatter[5]['token'] + pair[4]['token'])
    put(1, 0, tr)
    sums[5] = landed_sums(5, scatter[5], dh3)
    scatter[4] = chips_start(4, pair[4], dh3)
    pair[3] = pair_start(3, list(ffn_g))

    ds3 = jnp.concatenate([dh3, jnp.zeros((g_len, d), F32)], axis=0) \
        + (scatter[4]['token'][0, 0] + pair[3]['token'][0, 0])
    ds2, ffn_g, tr, dgain[0][2] = _ffn_half_bwd("l0b", ds3, sav_f01, gains[0], mods[0], 2, ffn_w[0][1],
                                                empty_ffn_grads(), tm, h_tiles, tm_l0)
    put(0, 2, tr)
    sums[4] = landed_sums(4, scatter[4], ds2)
    scatter[3] = chips_start(3, pair[3], ds2)
    pair[2] = pair_start(2, list(ffn_g))
    d_o0, acc_g0 = _resid_bwd("resid_bwd_l0m", ds2, o_mix0, mods[0], 1, 1.0, tm, h_tiles,
                              scatter[3]['token'] + pair[2]['token'])
    dcat = _mm("mix_out_l0_dx", d_o0, w_out_p, 'nt', tm=tm, tn=pool_dim + hw)
    d_w_out_p = _mm("mix_out_l0_dw", cat, d_o0, 'tn', tm=256, tn=512)
    col_blk = pool_dim // HEAD_PAD
    dq_h, dk_h, dv_h = _attn_bwd("attn_bwd_h", q_r, k_p, v_p, cat, dcat, lse_h, t_len, 0, r_len, 0, heads, tm,
                                 attn_scale, col_blk)
    dq_g, dk_all, dv_all = _attn_bwd("attn_bwd_g", q_r, k_p, v_p, cat, dcat, lse_g, g_len, t_len, g_len,
                                     t_len // g_len, heads, tm, attn_scale, col_blk, onto=(dk_h, dv_h))
    dq_all = jnp.concatenate([dq_h, dq_g], axis=0)
    dkvn = _mm("k_up_dx", dk_all, wk_ext, 'nt', tm=tm, tn=kvr_w)
    dkvn = _mm("v_up_dx", dv_all, wv_ext, 'nt', tm=tm, tn=kvr_w, epi=lambda acc, i, prev: (acc + prev,),
               epi_args=(dkvn,), epi_kinds=('mn',))
    d_wk_ext = _mm("k_up_dw", kvn, dk_all, 'tn', tm=kvr_w, tn=512)
    d_wv_ext = _mm("v_up_dw", kvn, dv_all, 'tn', tm=kvr_w, tn=512)

    def kvn_bwd_fn(rows, consts, m):
        pv, dv_, ct, st = rows
        g = consts[0][0:1, :]
        n, r = _rms(pv[:, :kv_rank])
        dyn = dv_[:, :kv_rank]
        dckv = _rms_bwd(dyn * g, n, r)
        dkr = _rope_t(dv_[:, kv_rank:], ct, st)
        return [jnp.concatenate([dckv, dkr], axis=1)], {0: jnp.sum(dyn * n, axis=0, keepdims=True)}

    dp_kvr, acc_kvg = _rows_call("kv_norm_bwd", kvn_bwd_fn, r_len, tm, [p_kvr, dkvn, tk_c, tk_s], [kvg], None,
                                 [(kvr_w, BF)], acc_w=kv_rank)

    def qrope_bwd_fn(rows, consts, m):
        return [_rope_t(rows[0], rows[1], rows[2])], {}

    dq_pad = _rows_call("q_rope_bwd", qrope_bwd_fn, r_len, tm, [dq_all, tq_c, tq_s], [], None, [(hw, BF)])[0]
    dqn = _mm("q_up_dx", dq_pad, wq_p, 'nt', tm=tm, tn=q_rank)
    d_wq_p = _mm("q_up_dw", qn, dq_pad, 'tn', tm=256, tn=512)

    def qn_bwd_fn(rows, consts, m):
        pv, dv_ = rows
        g = consts[0][0:1, :]
        n, r = _rms(pv)
        return [_rms_bwd(dv_ * g, n, r)], {0: jnp.sum(dv_ * n, axis=0, keepdims=True)}

    dp_q, acc_qg = _rows_call("q_norm_bwd", qn_bwd_fn, r_len, tm, [p_q, dqn], [qg], None, [(q_rank, BF)],
                              acc_w=q_rank)
    dpu_h, dpw_h, dps_h = _pool_bwd("pool_bwd_h", p_pool[:t_len], dcat, pool_w[0], pool_scale, 0)
    dpu_g, dpw_g, dps_g = _pool_bwd("pool_bwd_g", p_pool[t_len:], dcat, pool_w[0], pool_scale, t_len)
    dp_pool = jnp.concatenate([dpu_h, dpu_g], axis=0)
    add_prev = lambda acc, i, prev: (acc + prev,)
    du_mix = _mm("in_pool_dx", dp_pool, w_in_pool, 'nt', tm=tm, tn=d)
    du_mix = _mm("in_q_dx", dp_q, w_in_q, 'nt', tm=tm, tn=d, epi=add_prev, epi_args=(du_mix,), epi_kinds=('mn',))
    du_mix = _mm("in_kvr_dx", dp_kvr, w_in_kvr, 'nt', tm=tm, tn=d, epi=add_prev, epi_args=(du_mix,), epi_kinds=('mn',))
    d_w_in = jnp.concatenate([
        _mm("in_pool_dw", u_mix, dp_pool, 'tn', tm=256, tn=pool_dim),
        _mm("in_q_dw", u_mix, dp_q, 'tn', tm=256, tn=q_rank),
        _mm("in_kvr_dw", u_mix, dp_kvr, 'tn', tm=256, tn=kvr_w)[:, :kv_rank + QK_ROPE]], axis=1)
    ds1, acc_n0 = _adaln_bwd("adaln_bwd_l0m", s1, du_mix, ds2, gains[0], 1, mods[0], 1, tm, h_tiles)
    put(0, 1, (acc_n0[:, 0], acc_n0[:, 1], acc_g0[:, 0]))
    dgain[0][1] = jnp.sum(acc_n0[:, 2], axis=0)
    d_w_uq = d_wq_p.reshape(q_rank, heads, HEAD_PAD)[..., :QK_NOPE + QK_ROPE].reshape(q_rank, -1)
    d_w_ukv = jnp.concatenate([d_wk_ext[:kv_rank].reshape(kv_rank, heads, HEAD_PAD)[..., :QK_NOPE],
                               d_wv_ext[:kv_rank].reshape(kv_rank, heads, HEAD_PAD)[..., :V_HEAD]],
                              axis=-1).reshape(kv_rank, -1)
    d_w_out = jnp.concatenate([d_w_out_p[:pool_dim],
                               d_w_out_p[pool_dim:].reshape(heads, HEAD_PAD, d)[:, :V_HEAD].reshape(-1, d)], axis=0)
    sums[3] = landed_sums(3, scatter[3], ds1)
    scatter[2] = chips_start(2, pair[2], ds1)
    pair[1] = pair_start(1, [by_shard_cols(d_w_in), by_shard_rows(d_w_uq), by_shard_cols(d_w_ukv),
                             by_shard_rows(d_w_out)])
    ds0, ffn_g, tr, dgain[0][0] = _ffn_half_bwd("l0a", ds1, sav_f00, gains[0], mods[0], 0, ffn_w[0][0],
                                                empty_ffn_grads(), tm, h_tiles, tm_l0,
                                                scatter[2]['token'] + pair[1]['token'])
    put(0, 0, tr)
    grad_x = ds0[:t_len][None]
    sums[2] = landed_sums(2, scatter[2], ds0)
    scatter[1] = chips_start(1, pair[1], ds0)
    pair[0] = pair_start(0, list(ffn_g))

    dmh = jnp.stack([jnp.stack([dmod[l][k][0] for k in range(N_MOD)]) for l in range(2)])
    dmg0 = jnp.stack([dmod[0][k][1] for k in range(N_MOD)])
    dg_rows = jnp.stack([dgain[l][k] for l in range(2) for k in range(3)])
    pieces = [dmh.reshape(2 * N_MOD, d), dmg0, dg_rows, d_cw[:3], d_final_g[None],
              (dpw_h + dpw_g).reshape(-1, d), jnp.pad((dps_h + dps_g)[0], (0, d - pool_dim))[None],
              jnp.pad(acc_qg[0, 0], (0, d - q_rank))[None], jnp.pad(acc_kvg[0, 0], (0, d - kv_rank))[None],
              acc_loss[0, 1][None]]
    n_piece = [p.shape[0] for p in pieces]
    pieces = [jnp.pad(p, ((0, (-p.shape[0]) % 8), (0, 0))) for p in pieces]
    small_g = jnp.concatenate(pieces, axis=0) + (pair[0]['token'][0, 0] + scatter[1]['token'][0, 0])
    sg_all = _gather_all("gather_small_grads", [small_g])[0]
    sg_sum = _sum_lead("sum_small_grads", sg_all)
    offs = [0]
    for p in pieces:
        offs.append(offs[-1] + p.shape[0])
    part = lambda j: sg_sum[offs[j]:offs[j] + n_piece[j]]
    sum_dmh, sum_dmg0, g_norm_full, g_conv_w_full = part(0).reshape(2, N_MOD * d), part(1).reshape(N_MOD * d), part(2), part(3)
    g_final = part(4)[0]
    loss = part(9)[0, 0]
    g_pool_w = part(5).reshape(pool_w.shape)
    g_pool_scale = part(6)[:, :pool_dim]
    g_q_norm = part(7)[:, :q_rank]
    g_kv_norm = part(8)[:, :kv_rank]
    col0 = shard * (d // N_SHARD)
    g_norm_g = lax.dynamic_slice_in_dim(g_norm_full.reshape(2, 3, d), col0, d // N_SHARD, axis=2)
    g_conv_w = lax.dynamic_slice_in_dim(g_conv_w_full, col0, d // N_SHARD, axis=1)[None]
    g_b_mod = _sum_lead("sum_b_mod", jnp.stack([sum_dmh, jnp.stack([sum_dmg0, jnp.zeros_like(sum_dmg0)])]))

    dm16 = []
    for l in range(2):
        per_dev = sg_all[:, l * N_MOD:(l + 1) * N_MOD].reshape(N_DEV, N_MOD * d)
        row8 = sum_dmg0 if l == 0 else jnp.zeros_like(sum_dmg0)
        full = jnp.concatenate([per_dev, row8[None], jnp.zeros((7, N_MOD * d), F32)], axis=0)
        dm16.append(lax.dynamic_slice_in_dim(full, shard * n_col, n_col, axis=1))
    g_w_mod = jnp.stack([_mm(f"mod_dw_{l}", c16, dm16[l], 'tn', tm=256, tn=768, a_pre=_silu) for l in range(2)])
    dc16 = _mm("mod_dx", dm16[0], w_mod, 'nt', tm=16, tn=512, b_lead=0, epi=lambda acc, i, cv: (acc * _dsilu(cv),),
               epi_args=(c16,), epi_kinds=('mn',))
    dc_all = _gather_all("gather_dc", [dc16])[0]
    g_c_ctx = _sum_lead("sum_dc", dc_all[::2])[8]

    grads = dict(c_ctx=g_c_ctx, norm_g=g_norm_g, w_mod=g_w_mod, b_mod=g_b_mod, pool_w=g_pool_w,
                 pool_scale=g_pool_scale, q_norm_g=g_q_norm, kv_norm_g=g_kv_norm, conv_w=g_conv_w, final_norm_g=g_final)
    names = list(weights)

    sums[1] = landed_sums(1, scatter[1], g_c_ctx)
    scatter[0] = chips_start(0, pair[0], g_c_ctx)
    upd = {n: _adamw(f"adamw_{n}", weights[n], grads[n].reshape(weights[n].shape), mom_m[n], mom_v[n],
                     scatter[0]['token']) for n in names if n not in big_names}
    sums[0] = landed_sums(0, scatter[0], upd["w_mod"][0])

    halves_sum = [s for k in range(6) for s in sums[k]]
    swap = _split_start("swap_start", halves_sum + [pltpu.with_memory_space_constraint(lax.empty(s.shape, s.dtype),
                                                                                        pltpu.HBM) for s in halves_sum],
                        len(halves_sum), _sibling_whole_build(len(halves_sum)))
    both = _split_wait("swap_wait", swap, swap['token'])
    south = ci == 0
    swapped = iter([jnp.where(south, jnp.stack([a, g]), jnp.stack([g, a]))
                    for a, g in zip(both[:len(halves_sum)], both[len(halves_sum):])])
    by_stage = [{nm: next(swapped) for nm in stage_names(k)} for k in range(6)]
    for nm in ffn_names:
        grads[nm] = jnp.stack([jnp.stack([by_stage[3 * l + 2 * f][nm].reshape(weights[nm].shape[2:])
                                          for f in range(2)]) for l in range(2)])
    for l in range(2):
        for nm in mixer_names[l]:
            grads[nm] = by_stage[3 * l + 1][nm].reshape(weights[nm].shape)
    upd.update({n: _adamw(f"adamw_{n}", weights[n], grads[n], mom_m[n], mom_v[n]) for n in big_names})
    return (loss, grad_x, *[grads[n].reshape(weights[n].shape) for n in names], *[upd[n][0] for n in names],
            *[upd[n][1] for n in names], *[upd[n][2] for n in names])
```

```python
import functools
import math

import jax
import jax.numpy as jnp
from jax import lax
from jax.experimental import pallas as pl
from jax.experimental.pallas import tpu as pltpu

F32 = jnp.float32
BF = jnp.bfloat16
MESH = pl.DeviceIdType.MESH

N_DEV = 8
N_SHARD = 4
RMS_EPS = 1e-6
N_MOD = 9
POOL_WINDOWS = (2, 4, 8, 16)
QK_NOPE = 64
QK_ROPE = 32
V_HEAD = 64
HEAD_PAD = 128
GRID_W = 64
ROPE_THETA = 10000.0
POOL_PAD = 16
ADAM_LR, ADAM_B1, ADAM_B2, ADAM_EPS, ADAM_WD, ADAM_STEP = 0.001, 0.9, 0.999, 1e-08, 0.01, 10
VMEM_LIMIT = 56 * 1024 * 1024
MM_ROWS = 1024


def _pcall(body, **kw):
    return pl.pallas_call(body, **kw)


def _params(sem=None):
    return pltpu.CompilerParams(dimension_semantics=sem, vmem_limit_bytes=VMEM_LIMIT)


def _pick(n, pref, mult=128):
    best = None
    d = mult
    while d <= min(n, pref):
        if n % d == 0:
            best = d
        d += mult
    return best if best is not None else n


def _silu(z):
    return z * jax.nn.sigmoid(z)


def _dsilu(z):
    s = jax.nn.sigmoid(z)
    return s * (1.0 + z * (1.0 - s))


def _dot(a, b, dims):
    return lax.dot_general(a.astype(BF), b.astype(BF), (dims, ((), ())), preferred_element_type=F32)


NN = ((1,), (0,))
NT = ((1,), (1,))
TN = ((0,), (0,))


ALL_FLIPS = [(kx, ky, kc) for kx in (0, 1) for ky in (0, 1) for kc in (0, 1) if (kx, ky, kc) != (0, 0, 0)]
CHIP_FLIPS = [(1, 0, 0), (0, 1, 0), (1, 1, 0)]
SIBLING = (0, 0, 1)
COMM_SPLIT = 8
SPLIT_MIN_ROWS = 256


def _exchange(name, arrays, plan, lead, whole_src, split=COMM_SPLIT):
    n = len(arrays)
    blk_shapes = [tuple(a.shape) if whole_src else tuple(a.shape[1:]) for a in arrays]
    splits = []
    for shp in blk_shapes:
        s = 1
        while s * 2 <= split and shp[0] % (s * 2) == 0 and (shp[0] // (s * 2)) % 16 == 0 \
                and shp[0] // (s * 2) >= SPLIT_MIN_ROWS:
            s *= 2
        splits.append(s)
    items = plan(0, 0, 0)
    n_items = len(items)
    remote_ids = [k for k, it in enumerate(items) if it[0] is not None]
    local_ids = [k for k, it in enumerate(items) if it[0] is None]
    slots = [(a, s) for s in range(max(splits)) for a in range(n) if s < splits[a]]
    n_slot = len(slots)

    def body(*refs):
        ins, outs = refs[:n], refs[n:2 * n]
        send_sems, recv_sems, loc_sems = refs[2 * n:]
        x, y, c = lax.axis_index("x"), lax.axis_index("y"), lax.axis_index("c")
        plan_here = plan(x, y, c)

        def rows(ref, a, s):
            rc = blk_shapes[a][0] // splits[a]
            return ref.at[pl.ds(s * rc, rc)]

        def make(si, k):
            a, s = slots[si]
            flip, src, dst, _ = plan_here[k]
            base = outs[a] if src[0] == 'out' else ins[a]
            src_ref = rows(base if src[1] is None else base.at[src[1]], a, s)
            dst_ref = rows(outs[a].at[dst], a, s)
            if flip is None:
                return pltpu.make_async_copy(src_ref, dst_ref, loc_sems.at[si * max(1, len(local_ids)) + local_ids.index(k)])
            peer = (1 - x if flip[0] else x, 1 - y if flip[1] else y, 1 - c if flip[2] else c)
            sem = si * len(remote_ids) + remote_ids.index(k)
            return pltpu.make_async_remote_copy(src_ref=src_ref, dst_ref=dst_ref, send_sem=send_sems.at[sem],
                                                recv_sem=recv_sems.at[sem], device_id=peer, device_id_type=MESH)

        copies = {}
        for si in range(n_slot):
            for k in range(n_items):
                if plan_here[k][3] is None:
                    copies[si, k] = make(si, k)
                    copies[si, k].start()
        arrived = set()
        for si in range(n_slot):
            for k in range(n_items):
                after = plan_here[k][3]
                if after is not None:
                    if (si, after) not in arrived:
                        copies[si, after].wait_recv()
                        arrived.add((si, after))
                    copies[si, k] = make(si, k)
                    copies[si, k].start()
        for (si, k), cp in copies.items():
            if plan_here[k][0] is None:
                cp.wait()
            else:
                cp.wait_send()
                if (si, k) not in arrived:
                    cp.wait_recv()

    any_spec = pl.BlockSpec(memory_space=pl.ANY)
    n_rem = max(1, n_slot * len(remote_ids))
    outs = _pcall(
        body, name=name,
        out_shape=[jax.ShapeDtypeStruct((lead,) + s, a.dtype) for s, a in zip(blk_shapes, arrays)],
        in_specs=[any_spec] * n, out_specs=[any_spec] * n,
        scratch_shapes=[pltpu.SemaphoreType.DMA((n_rem,)), pltpu.SemaphoreType.DMA((n_rem,)),
                        pltpu.SemaphoreType.DMA((max(1, n_slot * len(local_ids)),))],
    )(*arrays)
    return list(outs)


def _place(x, y, c):
    return 4 * x + 2 * y + c


def _flip(v, f):
    return 1 - v if f else v


def _gather_all(name, arrays):
    def plan(x, y, c):
        me = _place(x, y, c)
        return [(None, ('in', None), me, None)] + [(f, ('in', None), me, None) for f in ALL_FLIPS]
    return _exchange(name, arrays, plan, N_DEV, True)


HBM_SPEC = pl.BlockSpec(memory_space=pltpu.HBM)
SEM_SPEC = pl.BlockSpec(memory_space=pltpu.SEMAPHORE)
SIDE_EFFECT = pltpu.SideEffectType.DATAFLOW_SIDE_EFFECTING


def _split_start(name, bufs, n_copies, build):
    n = len(bufs)

    def body(*refs):
        for cp in build(refs[:n], refs[n], refs[n + 1]):
            cp.start()
        token = refs[-1]
        token[...] = jnp.zeros_like(token)

    res = _pcall(
        body, name=name,
        out_shape=(pltpu.SemaphoreType.DMA((n_copies,)), pltpu.SemaphoreType.DMA((n_copies,)),
                   *[pltpu.HBM(b.shape, b.dtype) for b in bufs], jax.ShapeDtypeStruct((8, 128), F32)),
        in_specs=[HBM_SPEC] * n,
        out_specs=(SEM_SPEC, SEM_SPEC, *[HBM_SPEC] * n, pl.BlockSpec(memory_space=pltpu.VMEM)),
        input_output_aliases={i: 2 + i for i in range(n)},
        compiler_params=pltpu.CompilerParams(has_side_effects=SIDE_EFFECT),
    )(*[pltpu.with_memory_space_constraint(b, pltpu.HBM) for b in bufs])
    return dict(send=res[0], recv=res[1], bufs=list(res[2:2 + n]), token=res[-1], build=build)


def _split_wait(name, handle, after):
    n = len(handle['bufs'])
    build = handle['build']

    def body(*refs):
        for cp in build(refs[:n], refs[n], refs[n + 1]):
            cp.wait_send()
            cp.wait_recv()

    res = _pcall(
        body, name=name, out_shape=tuple(pltpu.HBM(b.shape, b.dtype) for b in handle['bufs']),
        in_specs=[HBM_SPEC] * n + [SEM_SPEC, SEM_SPEC, pl.BlockSpec(memory_space=pl.ANY)],
        out_specs=tuple([HBM_SPEC] * n), input_output_aliases={i: i for i in range(n)},
        compiler_params=pltpu.CompilerParams(has_side_effects=SIDE_EFFECT),
    )(*handle['bufs'], handle['send'], handle['recv'], after)
    return list(res)


def _landing(lead, arrays):
    return [pltpu.with_memory_space_constraint(lax.empty((lead,) + tuple(a.shape[1:]), a.dtype), pltpu.HBM)
            for a in arrays]


def _copy_list(n, per_array, make):
    def build(refs, send_sems, recv_sems):
        copies = []
        for a in range(n):
            for j in range(per_array):
                src, dst, peer = make(refs, a, j)
                k = a * per_array + j
                copies.append(pltpu.make_async_remote_copy(src_ref=src, dst_ref=dst, send_sem=send_sems.at[k],
                                                           recv_sem=recv_sems.at[k], device_id=peer,
                                                           device_id_type=MESH))
        return copies
    return build


def _mesh_place():
    x, y, c = lax.axis_index("x"), lax.axis_index("y"), lax.axis_index("c")
    return x, y, c, 2 * x + y


def _chips_gather_build(n):
    def make(refs, a, j):
        x, y, c, chip = _mesh_place()
        px, py = _flip(x, CHIP_FLIPS[j][0]), _flip(y, CHIP_FLIPS[j][1])
        return refs[a].at[c], refs[n + a].at[2 * chip + c], (px, py, c)
    return _copy_list(n, len(CHIP_FLIPS), make)


def _chips_scatter_build(n):
    def make(refs, a, j):
        x, y, c, chip = _mesh_place()
        px, py = _flip(x, CHIP_FLIPS[j][0]), _flip(y, CHIP_FLIPS[j][1])
        return refs[a].at[2 * px + py], refs[n + a].at[chip], (px, py, c)
    return _copy_list(n, len(CHIP_FLIPS), make)


def _sibling_forward_build(n):
    def make(refs, a, j):
        x, y, c, _ = _mesh_place()
        blk = 2 * (2 * _flip(x, CHIP_FLIPS[j][0]) + _flip(y, CHIP_FLIPS[j][1])) + c
        return refs[a].at[blk], refs[a].at[blk], (x, y, 1 - c)
    return _copy_list(n, len(CHIP_FLIPS), make)


def _sibling_halves_build(n):
    def make(refs, a, j):
        x, y, c, _ = _mesh_place()
        return refs[a].at[2 * j + 1 - c], refs[n + a].at[j], (x, y, 1 - c)
    return _copy_list(n, N_SHARD, make)


def _sibling_whole_build(n):
    def make(refs, a, j):
        x, y, c, _ = _mesh_place()
        return refs[a], refs[n + a], (x, y, 1 - c)
    return _copy_list(n, 1, make)


def _add_halves(name, send, land):
    _, r, cdim = send.shape
    tr = _pick(r, max(16, (1 << 20) // (cdim * 2)), 16)

    def body(c_ref, own_ref, got_ref, o_ref):
        o_ref[...] = (own_ref[...].astype(F32) + got_ref[...].astype(F32)).astype(BF)

    grid_spec = pltpu.PrefetchScalarGridSpec(
        num_scalar_prefetch=1, grid=(N_SHARD, r // tr),
        in_specs=[pl.BlockSpec((None, tr, cdim), lambda sh, i, cr: (2 * sh + cr[0], i, 0)),
                  pl.BlockSpec((None, tr, cdim), lambda sh, i, cr: (sh, i, 0))],
        out_specs=pl.BlockSpec((None, tr, cdim), lambda sh, i, cr: (sh, i, 0)))
    core = lax.axis_index("c").astype(jnp.int32).reshape(1)
    return _pcall(body, name=name, grid_spec=grid_spec, out_shape=jax.ShapeDtypeStruct((N_SHARD, r, cdim), BF),
                  compiler_params=_params(("arbitrary", "arbitrary")))(core, send, land)


def _sum_lead(name, arr, out_dtype=F32):
    n, r, cdim = arr.shape
    tr = r
    limit = (4 << 20) // (n * cdim * arr.dtype.itemsize)
    if r > limit:
        tr = _pick(r, max(limit, 16), 16)

    def body(x_ref, o_ref):
        acc = x_ref[0].astype(F32)
        for d in range(1, n):
            acc = acc + x_ref[d].astype(F32)
        o_ref[...] = acc.astype(out_dtype)

    return _pcall(body, name=name, grid=(r // tr,),
                  in_specs=[pl.BlockSpec((n, tr, cdim), lambda i: (0, i, 0))],
                  out_specs=pl.BlockSpec((tr, cdim), lambda i: (i, 0)),
                  out_shape=jax.ShapeDtypeStruct((r, cdim), out_dtype),
                  compiler_params=_params(("arbitrary",)))(arr)


def _rows_call(name, fn, n_rows, tm, rows, consts, mod, outs, acc_w=None, h_tiles=None):
    nt = n_rows // tm
    ht = nt if h_tiles is None else h_tiles
    ng = 1 if mod is None else mod.shape[0]
    n_r, n_c, n_o = len(rows), len(consts), len(outs)
    has_mod = mod is not None

    def body(*refs):
        i = pl.program_id(0)
        first = (i % ht) == 0
        row_refs, const_refs = refs[:n_r], refs[n_r:n_r + n_c]
        p = n_r + n_c
        mod_tile = refs[p][...] if has_mod else None
        p += int(has_mod)
        out_refs = refs[p:p + n_o]
        o, acc = fn([r[...] for r in row_refs], [r[...] for r in const_refs], mod_tile)
        for r, v in zip(out_refs, o):
            r[...] = v.astype(r.dtype)
        if acc_w is not None:
            acc_ref = refs[p + n_o]

            @pl.when(first)
            def _():
                acc_ref[...] = jnp.zeros_like(acc_ref)

            for k, v in acc.items():
                acc_ref[k:k + 1, :] += v

    in_specs = [pl.BlockSpec((tm, r.shape[1]), lambda i: (i, 0)) for r in rows]
    in_specs += [pl.BlockSpec(cst.shape, lambda i, nd=cst.ndim: (0,) * nd) for cst in consts]
    args = list(rows) + list(consts)
    if has_mod:
        in_specs.append(pl.BlockSpec((None,) + mod.shape[1:], lambda i: (i // ht, 0, 0)))
        args.append(mod)
    out_shape = [jax.ShapeDtypeStruct((n_rows, w), dt) for w, dt in outs]
    out_specs = [pl.BlockSpec((tm, w), lambda i: (i, 0)) for w, _ in outs]
    if acc_w is not None:
        out_shape.append(jax.ShapeDtypeStruct((ng, 8, acc_w), F32))
        out_specs.append(pl.BlockSpec((None, 8, acc_w), lambda i: (i // ht, 0, 0)))
    res = _pcall(body, name=name, grid=(nt,), in_specs=in_specs, out_specs=out_specs, out_shape=out_shape,
                 compiler_params=_params(("arbitrary",)))(*args)
    return list(res)


def _rms(s):
    r = lax.rsqrt(jnp.mean(s * s, axis=1, keepdims=True) + RMS_EPS)
    return s * r, r


def _rms_bwd(dn, n, r):
    return r * (dn - n * jnp.mean(dn * n, axis=1, keepdims=True))


def _adaln_fwd(name, s, gains, gain_row, mod, k, tm, h_tiles, after=None):
    def fn(rows, consts, m):
        n, _ = _rms(rows[0])
        y = n * consts[0][gain_row:gain_row + 1, :]
        return [y * (1.0 + m[3 * k + 1:3 * k + 2, :]) + m[3 * k:3 * k + 1, :]], {}

    d = s.shape[1]
    consts = [gains] if after is None else [gains, after]
    return _rows_call(name, fn, s.shape[0], tm, [s], consts, mod, [(d, BF)], h_tiles=h_tiles)[0]


def _adaln_bwd(name, s, du, ds_res, gains, gain_row, mod, k, tm, h_tiles):
    def fn(rows, consts, m):
        sv, duv, res = rows
        gain = consts[0][gain_row:gain_row + 1, :]
        n, r = _rms(sv)
        y = n * gain
        dy = duv * (1.0 + m[3 * k + 1:3 * k + 2, :])
        acc = {0: jnp.sum(duv, axis=0, keepdims=True), 1: jnp.sum(duv * y, axis=0, keepdims=True),
               2: jnp.sum(dy * n, axis=0, keepdims=True)}
        return [_rms_bwd(dy * gain, n, r) + res], acc

    d = s.shape[1]
    return _rows_call(name, fn, s.shape[0], tm, [s, du, ds_res], [gains], mod, [(d, F32)], acc_w=d, h_tiles=h_tiles)


def _resid_bwd(name, ds_out, o, mod, k, cst, tm, h_tiles, after=None):
    def fn(rows, consts, m):
        dsv, ov = rows
        gate = m[3 * k + 2:3 * k + 3, :]
        return [cst * gate * dsv], {0: jnp.sum(cst * ov * dsv, axis=0, keepdims=True)}

    d = o.shape[1]
    consts = [] if after is None else [after]
    return _rows_call(name, fn, o.shape[0], tm, [ds_out, o], consts, mod, [(d, BF)], acc_w=d, h_tiles=h_tiles)


def _mm(name, a, b, mode, tm=256, tn=512, out_dtypes=(F32,), epi=None, epi_args=(), epi_kinds=(), a_pre=None,
        b_lead=None):
    bshape = b.shape if b_lead is None else b.shape[1:]
    if mode == 'nn':
        (m, kd), nd = a.shape, bshape[1]
    elif mode == 'nt':
        (m, kd), nd = a.shape, bshape[0]
    else:
        (kd, m), nd = a.shape, bshape[1]
    tm = _pick(m, tm, 16) if m % tm else tm
    tn = _pick(nd, tn, 128) if nd % tn else tn
    dims = {'nn': NN, 'nt': NT, 'tn': TN}[mode]
    n_e, n_o = len(epi_args), len(out_dtypes)

    def body(*refs):
        i = pl.program_id(1)
        av = refs[0][...]
        if a_pre is not None:
            av = a_pre(av)
        acc = _dot(av, refs[1][...], dims)
        res = (acc,) if epi is None else epi(acc, i, *[r[...] for r in refs[2:2 + n_e]])
        for r, v in zip(refs[2 + n_e:], res):
            r[...] = v.astype(r.dtype)

    if mode == 'nn':
        specs = [pl.BlockSpec((tm, kd), lambda j, i: (i, 0)), pl.BlockSpec((kd, tn), lambda j, i: (0, j))]
    elif mode == 'nt':
        specs = [pl.BlockSpec((tm, kd), lambda j, i: (i, 0)), pl.BlockSpec((tn, kd), lambda j, i: (j, 0))]
    else:
        specs = [pl.BlockSpec((kd, tm), lambda j, i: (0, i)), pl.BlockSpec((kd, tn), lambda j, i: (0, j))]
    if b_lead is not None:
        shape2, at2 = specs[1].block_shape, specs[1].index_map
        specs[1] = pl.BlockSpec((None,) + tuple(shape2), lambda j, i: (b_lead,) + tuple(at2(j, i)))
    for arr, kind in zip(epi_args, epi_kinds):
        if kind == 'mn':
            specs.append(pl.BlockSpec((tm, tn), lambda j, i: (i, j)))
        elif kind == 'n':
            specs.append(pl.BlockSpec((1, tn), lambda j, i: (0, j)))
        elif kind == 'mt':
            specs.append(pl.BlockSpec((tm, arr.shape[1]), lambda j, i: (i, 0)))
        else:
            specs.append(pl.BlockSpec(arr.shape, lambda j, i, nd_=arr.ndim: (0,) * nd_))
    res = _pcall(body, name=name, grid=(nd // tn, m // tm), in_specs=specs,
                 out_specs=[pl.BlockSpec((tm, tn), lambda j, i: (i, j))] * n_o,
                 out_shape=[jax.ShapeDtypeStruct((m, nd), dt) for dt in out_dtypes],
                 compiler_params=_params(("arbitrary", "arbitrary")))(a, b, *epi_args)
    return res[0] if n_o == 1 else list(res)


def _row_gate(mod, k3, i, tm, n_lat):
    g0 = mod[0, k3:k3 + 1, :]
    if mod.shape[0] == 1:
        return g0
    rid = i * tm + lax.broadcasted_iota(jnp.int32, (tm, 1), 0)
    return jnp.where(rid < n_lat, g0, mod[1, k3:k3 + 1, :])


def _ffn_up(name, u, wg, wu, base, tm):
    r, d = u.shape
    nch, _, _, fc = wg.shape

    def body(u_ref, wg_ref, wu_ref, a_ref, b_ref, h_ref):
        uv = u_ref[...]
        a = _dot(uv, wg_ref[...], NN)
        b = _dot(uv, wu_ref[...], NN)
        a_ref[...] = a.astype(BF)
        b_ref[...] = b.astype(BF)
        h_ref[...] = (_silu(a) * b).astype(BF)

    chunk = pl.BlockSpec((None, tm, fc), lambda j, i: (j, i, 0))
    return _pcall(body, name=name, grid=(nch, r // tm),
                  in_specs=[pl.BlockSpec((tm, d), lambda j, i: (i, 0)),
                            pl.BlockSpec((None, None, d, fc), lambda j, i: (j, base, 0, 0)),
                            pl.BlockSpec((None, None, d, fc), lambda j, i: (j, base, 0, 0))],
                  out_specs=[chunk] * 3, out_shape=[jax.ShapeDtypeStruct((nch, r, fc), BF)] * 3,
                  compiler_params=_params(("arbitrary", "arbitrary")))(u, wg, wu)


def _ffn_down(name, hid, wd, wd_blk, s, mod, k, n_lat, tm):
    nch, r, fc = hid.shape
    d = wd.shape[2]

    def body(h_ref, w_ref, s_ref, m_ref, so_ref, o_ref, acc_ref):
        i, j = pl.program_id(0), pl.program_id(1)
        part = _dot(h_ref[...], w_ref[...], NN)

        @pl.when(j == 0)
        def _():
            acc_ref[...] = part

        @pl.when(j > 0)
        def _():
            acc_ref[...] += part

        @pl.when(j == nch - 1)
        def _():
            o = acc_ref[...]
            o_ref[...] = o
            so_ref[...] = s_ref[...] + 0.5 * _row_gate(m_ref[...], 3 * k + 2, i, tm, n_lat) * o

    row = pl.BlockSpec((tm, d), lambda i, j: (i, 0))
    return _pcall(body, name=name, grid=(r // tm, nch),
                  in_specs=[pl.BlockSpec((None, tm, fc), lambda i, j: (j, i, 0)),
                            pl.BlockSpec((None, fc, d), lambda i, j: (j, wd_blk, 0)), row,
                            pl.BlockSpec(mod.shape, lambda i, j: (0, 0, 0))],
                  out_specs=[row, row], out_shape=[jax.ShapeDtypeStruct((r, d), F32)] * 2,
                  scratch_shapes=[pltpu.VMEM((tm, d), F32)],
                  compiler_params=_params(("arbitrary", "arbitrary")))(hid, wd, s, mod)


def _ffn_dhid(name, d_o, wd, wd_blk, a, b, tm):
    r, d = d_o.shape
    nch, _, fc = a.shape

    def body(g_ref, w_ref, a_ref, b_ref, da_ref, db_ref):
        dh = _dot(g_ref[...], w_ref[...], NT)
        av, bv = a_ref[...].astype(F32), b_ref[...].astype(F32)
        da_ref[...] = (dh * bv * _dsilu(av)).astype(BF)
        db_ref[...] = (dh * _silu(av)).astype(BF)

    chunk = pl.BlockSpec((None, tm, fc), lambda j, i: (j, i, 0))
    return _pcall(body, name=name, grid=(nch, r // tm),
                  in_specs=[pl.BlockSpec((tm, d), lambda j, i: (i, 0)),
                            pl.BlockSpec((None, fc, d), lambda j, i: (j, wd_blk, 0)), chunk, chunk],
                  out_specs=[chunk] * 2, out_shape=[jax.ShapeDtypeStruct((nch, r, fc), BF)] * 2,
                  compiler_params=_params(("arbitrary", "arbitrary")))(d_o, wd, a, b)


def _ffn_du(name, da, db, wg, wu, base, tm):
    nch, r, fc = da.shape
    d = wg.shape[2]

    def body(da_ref, db_ref, wg_ref, wu_ref, o_ref, acc_ref):
        j = pl.program_id(1)
        part = _dot(da_ref[...], wg_ref[...], NT) + _dot(db_ref[...], wu_ref[...], NT)

        @pl.when(j == 0)
        def _():
            acc_ref[...] = part

        @pl.when(j > 0)
        def _():
            acc_ref[...] += part

        @pl.when(j == nch - 1)
        def _():
            o_ref[...] = acc_ref[...]

    chunk = pl.BlockSpec((None, tm, fc), lambda i, j: (j, i, 0))
    return _pcall(body, name=name, grid=(r // tm, nch),
                  in_specs=[chunk, chunk, pl.BlockSpec((None, None, d, fc), lambda i, j: (j, base, 0, 0)),
                            pl.BlockSpec((None, None, d, fc), lambda i, j: (j, base, 0, 0))],
                  out_specs=pl.BlockSpec((tm, d), lambda i, j: (i, 0)),
                  out_shape=jax.ShapeDtypeStruct((r, d), F32), scratch_shapes=[pltpu.VMEM((tm, d), F32)],
                  compiler_params=_params(("arbitrary", "arbitrary")))(da, db, wg, wu)


def _ffn_dw_in(name, u, dz, tmm, grads, idx):
    r, d = u.shape
    nch, _, fc = dz.shape
    nb = d // tmm

    def body(u_ref, z_ref, g_ref, o_ref):
        o_ref[...] = _dot(u_ref[...], z_ref[...], TN).astype(o_ref.dtype)

    return _pcall(body, name=name, grid=(nch, nb),
                  in_specs=[pl.BlockSpec((r, tmm), lambda j, mi: (0, mi)),
                            pl.BlockSpec((None, r, fc), lambda j, mi: (j, 0, 0)),
                            pl.BlockSpec(memory_space=pl.ANY)],
                  out_specs=pl.BlockSpec((None, tmm, fc), lambda j, mi: (j, idx * nb + mi, 0)),
                  out_shape=jax.ShapeDtypeStruct(grads.shape, grads.dtype), input_output_aliases={2: 0},
                  compiler_params=_params(("arbitrary", "arbitrary")))(u, dz, grads)


def _ffn_dw_down(name, hid, d_o, tn, grads, idx):
    nch, r, fc = hid.shape
    d = d_o.shape[1]

    def body(h_ref, g_ref, acc_ref, o_ref):
        o_ref[...] = _dot(h_ref[...], g_ref[...], TN).astype(o_ref.dtype)

    return _pcall(body, name=name, grid=(nch, d // tn),
                  in_specs=[pl.BlockSpec((None, r, fc), lambda j, ni: (j, 0, 0)),
                            pl.BlockSpec((r, tn), lambda j, ni: (0, ni)),
                            pl.BlockSpec(memory_space=pl.ANY)],
                  out_specs=pl.BlockSpec((None, fc, tn), lambda j, ni: (j, idx, ni)),
                  out_shape=jax.ShapeDtypeStruct(grads.shape, grads.dtype), input_output_aliases={2: 0},
                  compiler_params=_params(("arbitrary", "arbitrary")))(hid, d_o, grads)


def _partner(x):
    n = x.shape[1]
    lane = lax.broadcasted_iota(jnp.int32, x.shape, 1)
    return jnp.where((lane & 15) < 8, pltpu.roll(x, n - 8, 1), pltpu.roll(x, 8, 1))


def _rope(x, ct, st):
    reps = x.shape[1] // ct.shape[1]
    if reps > 1:
        ct, st = jnp.tile(ct, (1, reps)), jnp.tile(st, (1, reps))
    return x * ct + _partner(x) * st


def _rope_t(dy, ct, st):
    reps = dy.shape[1] // ct.shape[1]
    if reps > 1:
        ct, st = jnp.tile(ct, (1, reps)), jnp.tile(st, (1, reps))
    return dy * ct + _partner(dy * st)


def _rope_tables(t_len, g_len, lane0):
    half = QK_ROPE // 4
    pos = jnp.arange(t_len)
    row = (pos // GRID_W).astype(F32)
    col = (pos % GRID_W).astype(F32)
    freqs = jnp.power(ROPE_THETA, -jnp.arange(0, QK_ROPE // 2, 2, dtype=F32) / (QK_ROPE // 2))
    ang_r, ang_c = row[:, None] * freqs, col[:, None] * freqs
    cs = jnp.concatenate([jnp.cos(ang_r)] * 2 + [jnp.cos(ang_c)] * 2, axis=1)
    sn = jnp.concatenate([-jnp.sin(ang_r), jnp.sin(ang_r), -jnp.sin(ang_c), jnp.sin(ang_c)], axis=1)
    assert cs.shape[1] == 4 * half == QK_ROPE
    def place(tab, fill):
        rest = HEAD_PAD - lane0 - QK_ROPE
        rows = jnp.concatenate([jnp.full((t_len, lane0), fill, F32), tab, jnp.full((t_len, rest), fill, F32)], axis=1)
        return jnp.concatenate([rows, jnp.full((g_len, HEAD_PAD), fill, F32)], axis=0)

    return place(cs, 1.0), place(sn, 0.0)


def _attn_fwd(name, q, kp, vp, n_q, q_off, n_k, k_blk, heads, tq, scale):
    qb = q_off // tq

    def body(q_ref, k_ref, v_ref, o_ref, l_ref):
        s = _dot(q_ref[...], k_ref[...], NT) * scale
        m = jnp.max(s, axis=1, keepdims=True)
        p = jnp.exp(s - m)
        l = jnp.sum(p, axis=1, keepdims=True)
        o_ref[...] = (_dot(p, v_ref[...], NN) / l).astype(BF)
        l_ref[...] = jnp.broadcast_to(m + jnp.log(l), l_ref.shape)

    hw = heads * HEAD_PAD
    blk = pl.BlockSpec((tq, HEAD_PAD), lambda h, i: (i, h))
    kv = pl.BlockSpec((n_k, HEAD_PAD), lambda h, i: (k_blk, h))
    return _pcall(body, name=name, grid=(heads, n_q // tq),
                  in_specs=[pl.BlockSpec((tq, HEAD_PAD), lambda h, i: (i + qb, h)), kv, kv],
                  out_specs=[blk, blk],
                  out_shape=[jax.ShapeDtypeStruct((n_q, hw), BF), jax.ShapeDtypeStruct((n_q, hw), F32)],
                  compiler_params=_params(("arbitrary", "arbitrary")))(q, kp, vp)


def _attn_bwd(name, q, kp, vp, cat, dcat, lse, n_q, q_off, n_k, k_blk, heads, tq, scale, col_blk, onto=None):
    qb = q_off // tq

    def body(q_ref, k_ref, v_ref, o_ref, do_ref, l_ref, *rest):
        dq_ref, dk_ref, dv_ref = rest[-3:]
        i = pl.program_id(1)
        qv, kv_, vv = q_ref[...], k_ref[...], v_ref[...]
        dov = do_ref[...]
        s = _dot(qv, kv_, NT) * scale
        p = jnp.exp(s - l_ref[...][:, 0:1])
        dp = _dot(dov, vv, NT)
        delta = jnp.sum(dov * o_ref[...].astype(F32), axis=1, keepdims=True)
        ds = (p * (dp - delta) * scale).astype(BF)
        dq_ref[...] = _dot(ds, kv_, NN)
        dk = _dot(ds, qv, TN)
        dv = _dot(p, dov, TN)

        @pl.when(i == 0)
        def _():
            if onto is None:
                dk_ref[...] = dk
                dv_ref[...] = dv
            else:
                dk_ref[...] = rest[0][...] + dk
                dv_ref[...] = rest[1][...] + dv

        @pl.when(i > 0)
        def _():
            dk_ref[...] += dk
            dv_ref[...] += dv

    hw = heads * HEAD_PAD
    qspec = pl.BlockSpec((tq, HEAD_PAD), lambda h, i: (i + qb, h))
    cspec = pl.BlockSpec((tq, HEAD_PAD), lambda h, i: (i + qb, col_blk + h))
    kv = pl.BlockSpec((n_k, HEAD_PAD), lambda h, i: (k_blk, h))
    blk = pl.BlockSpec((tq, HEAD_PAD), lambda h, i: (i, h))
    if onto is None:
        acc = pl.BlockSpec((n_k, HEAD_PAD), lambda h, i: (0, h))
        return _pcall(body, name=name, grid=(heads, n_q // tq),
                      in_specs=[qspec, kv, kv, cspec, cspec, blk], out_specs=[blk, acc, acc],
                      out_shape=[jax.ShapeDtypeStruct((n_q, hw), F32), jax.ShapeDtypeStruct((n_k, hw), F32),
                                 jax.ShapeDtypeStruct((n_k, hw), F32)],
                      compiler_params=_params(("arbitrary", "arbitrary")))(q, kp, vp, cat, dcat, lse)
    return _pcall(body, name=name, grid=(heads, n_q // tq),
                  in_specs=[qspec, kv, kv, cspec, cspec, blk, kv, kv], out_specs=[blk, kv, kv],
                  out_shape=[jax.ShapeDtypeStruct((n_q, hw), F32)] + [jax.ShapeDtypeStruct(t.shape, F32) for t in onto],
                  input_output_aliases={6: 1, 7: 2},
                  compiler_params=_params(("arbitrary", "arbitrary")))(q, kp, vp, cat, dcat, lse, *onto)


def _shift(x, k):
    return pltpu.roll(x, k % x.shape[0], 0)


def _window_sum(v, w, mirrored):
    n, gd = v.shape
    pad = jnp.zeros((POOL_PAD, gd), F32)
    e = jnp.concatenate([pad, v, pad], axis=0)
    acc = e + _shift(e, -1 if mirrored else 1)
    step = 1
    while 2 * step < w:
        acc = _shift(acc, step) + _shift(acc, -step)
        step *= 2
    return acc[POOL_PAD:POOL_PAD + n]


def _window_count(n, w):
    t = lax.broadcasted_iota(jnp.int32, (n, 1), 0)
    lo = jnp.maximum(t - w // 2, 0)
    hi = jnp.minimum(t + (w - w // 2 - 1), n - 1)
    return (hi - lo + 1).astype(F32)


def _pool_fwd(name, u, pool_w, scale):
    n, pd = u.shape
    ng = len(POOL_WINDOWS)
    gd = pd // ng

    def body(u_ref, w_ref, s_ref, y_ref):
        for g, w in enumerate(POOL_WINDOWS):
            sl = slice(g * gd, (g + 1) * gd)
            ug = u_ref[:, sl]
            p = _window_sum(ug, w, False) / _window_count(n, w) - ug
            y_ref[:, sl] = (_dot(p, w_ref[g], NN) * s_ref[:, sl]).astype(BF)

    return _pcall(body, name=name, out_shape=jax.ShapeDtypeStruct((n, pd), BF),
                  compiler_params=_params())(u, pool_w, scale)


def _pool_bwd(name, u, dcat, pool_w, scale, row_off):
    n, pd = u.shape
    ng = len(POOL_WINDOWS)
    gd = pd // ng

    def body(u_ref, dy_ref, w_ref, s_ref, du_ref, dw_ref, ds_ref):
        ds_ref[...] = jnp.zeros_like(ds_ref)
        for g, w in enumerate(POOL_WINDOWS):
            sl = slice(g * gd, (g + 1) * gd)
            ug, dy, wg = u_ref[:, sl], dy_ref[:, sl], w_ref[g]
            cnt = _window_count(n, w)
            p = _window_sum(ug, w, False) / cnt - ug
            ds_ref[0:1, sl] = jnp.sum(dy * _dot(p, wg, NN), axis=0, keepdims=True)
            dys = dy * s_ref[:, sl]
            dw_ref[g] = _dot(p, dys, TN)
            dp = _dot(dys, wg, NT)
            du_ref[:, sl] = (_window_sum(dp / cnt, w, True) - dp).astype(BF)

    rb = row_off // n
    return _pcall(body, name=name, grid=(1,),
                  in_specs=[pl.BlockSpec((n, pd), lambda i: (0, 0)), pl.BlockSpec((n, pd), lambda i: (rb, 0)),
                            pl.BlockSpec(pool_w.shape, lambda i: (0, 0, 0)), pl.BlockSpec(scale.shape, lambda i: (0, 0))],
                  out_specs=[pl.BlockSpec((n, pd), lambda i: (0, 0)), pl.BlockSpec((ng, gd, gd), lambda i: (0, 0, 0)),
                             pl.BlockSpec((8, pd), lambda i: (0, 0))],
                  out_shape=[jax.ShapeDtypeStruct((n, pd), BF), jax.ShapeDtypeStruct((ng, gd, gd), F32),
                             jax.ShapeDtypeStruct((8, pd), F32)],
                  compiler_params=_params(("arbitrary",)))(u, dcat, pool_w, scale)


def _edge_shift(z, k):
    n = z.shape[0]
    t = lax.broadcasted_iota(jnp.int32, (n, 1), 0)
    keep = (t >= k) if k > 0 else (t < n + k)
    return jnp.where(keep, pltpu.roll(z, k % n, 0), 0.0)


def _conv_fwd(name, p3, cw, tc):
    n, cd = p3.shape[0], p3.shape[1] // 3
    nb = cd // tc

    def body(b_ref, c_ref, v_ref, w_ref, y_ref):
        z = c_ref[...] * v_ref[...]
        w = w_ref[...]
        zc = w[0:1] * _edge_shift(z, 1) + w[1:2] * z + w[2:3] * _edge_shift(z, -1)
        y_ref[...] = (b_ref[...] * zc).astype(BF)

    return _pcall(body, name=name, grid=(nb,),
                  in_specs=[pl.BlockSpec((n, tc), lambda j: (0, j)), pl.BlockSpec((n, tc), lambda j: (0, nb + j)),
                            pl.BlockSpec((n, tc), lambda j: (0, 2 * nb + j)), pl.BlockSpec((3, tc), lambda j: (0, j))],
                  out_specs=pl.BlockSpec((n, tc), lambda j: (0, j)), out_shape=jax.ShapeDtypeStruct((n, cd), BF),
                  compiler_params=_params(("arbitrary",)))(p3, p3, p3, cw)


def _conv_bwd(name, p3, cw, dy, tc):
    n, cd = dy.shape
    nb = cd // tc

    def body(b_ref, c_ref, v_ref, w_ref, dy_ref, dp_ref, dw_ref):
        cv, vv, w, dyv = c_ref[...], v_ref[...], w_ref[...], dy_ref[...]
        z = cv * vv
        zl, zr = _edge_shift(z, 1), _edge_shift(z, -1)
        zc = w[0:1] * zl + w[1:2] * z + w[2:3] * zr
        dzc = dyv * b_ref[...]
        dz = w[0:1] * _edge_shift(dzc, -1) + w[1:2] * dzc + w[2:3] * _edge_shift(dzc, 1)
        dp_ref[0] = (dyv * zc).astype(BF)
        dp_ref[1] = (dz * vv).astype(BF)
        dp_ref[2] = (dz * cv).astype(BF)
        dw_ref[...] = jnp.zeros_like(dw_ref)
        dw_ref[0:1, :] = jnp.sum(dzc * zl, axis=0, keepdims=True)
        dw_ref[1:2, :] = jnp.sum(dzc * z, axis=0, keepdims=True)
        dw_ref[2:3, :] = jnp.sum(dzc * zr, axis=0, keepdims=True)

    col = pl.BlockSpec((n, tc), lambda j: (0, j))
    return _pcall(body, name=name, grid=(nb,),
                  in_specs=[col, pl.BlockSpec((n, tc), lambda j: (0, nb + j)),
                            pl.BlockSpec((n, tc), lambda j: (0, 2 * nb + j)), pl.BlockSpec((3, tc), lambda j: (0, j)), col],
                  out_specs=[pl.BlockSpec((3, n, tc), lambda j: (0, 0, j)), pl.BlockSpec((8, tc), lambda j: (0, j))],
                  out_shape=[jax.ShapeDtypeStruct((3, n, cd), BF), jax.ShapeDtypeStruct((8, cd), F32)],
                  compiler_params=_params(("arbitrary",)))(p3, p3, p3, cw, dy)


def _conv_din(name, dp3, w_in, tm):
    _, n, cd = dp3.shape
    d = w_in.shape[0]

    def body(a_ref, w_ref, o_ref, acc_ref):
        j = pl.program_id(1)
        part = _dot(a_ref[...], w_ref[...], NT)

        @pl.when(j == 0)
        def _():
            acc_ref[...] = part

        @pl.when(j > 0)
        def _():
            acc_ref[...] += part

        @pl.when(j == 2)
        def _():
            o_ref[...] = acc_ref[...]

    return _pcall(body, name=name, grid=(n // tm, 3),
                  in_specs=[pl.BlockSpec((None, tm, cd), lambda i, j: (j, i, 0)),
                            pl.BlockSpec((d, cd), lambda i, j: (0, j))],
                  out_specs=pl.BlockSpec((tm, d), lambda i, j: (i, 0)), out_shape=jax.ShapeDtypeStruct((n, d), F32),
                  scratch_shapes=[pltpu.VMEM((tm, d), F32)],
                  compiler_params=_params(("arbitrary", "arbitrary")))(dp3, w_in)


def _conv_dw_in(name, u, dp3, tmm, tn):
    n, d = u.shape
    cd = dp3.shape[2]
    nb = cd // tn

    def body(u_ref, z_ref, o_ref):
        o_ref[...] = _dot(u_ref[...], z_ref[...], TN)

    return _pcall(body, name=name, grid=(3 * nb, d // tmm),
                  in_specs=[pl.BlockSpec((n, tmm), lambda j, mi: (0, mi)),
                            pl.BlockSpec((None, n, tn), lambda j, mi: (j // nb, 0, j % nb))],
                  out_specs=pl.BlockSpec((tmm, tn), lambda j, mi: (mi, j)),
                  out_shape=jax.ShapeDtypeStruct((d, 3 * cd), F32),
                  compiler_params=_params(("arbitrary", "arbitrary")))(u, dp3)


def _loss_head(name, h, target, gain, tm):
    d = h.shape[1]

    def fn(rows, consts, m):
        hv, tv = rows
        g = consts[0][0:1, :]
        n, r = _rms(hv)
        err = n * g - tv
        dy = err / d
        loss = 0.5 * jnp.sum(err * err) / d
        acc = {0: jnp.sum(dy * n, axis=0, keepdims=True), 1: jnp.full((1, d), loss, F32)}
        return [_rms_bwd(dy * g, n, r)], acc

    return _rows_call(name, fn, h.shape[0], tm, [h, target], [gain], None, [(d, F32)], acc_w=d)


def _adamw(name, w, g, m, v, after=None):
    shape = w.shape
    if w.ndim == 1:
        shape2 = (1,) + shape
        res = _adamw(name, *[t.reshape(shape2) for t in (w, g, m, v)], after=after)
        return [t.reshape(shape) for t in res]
    if shape[-1] % 128 and shape[-2] % 128 == 0:
        res = _adamw(name, *[jnp.swapaxes(t, -1, -2) for t in (w, g, m, v)], after=after)
        return [jnp.swapaxes(t, -1, -2) for t in res]
    lead, (r, cdim) = shape[:-2], shape[-2:]
    tr = r
    if r * cdim * 4 > (3 << 19):
        tr = _pick(r, max(8, (3 << 19) // (cdim * 4)), 8)
    c1 = 1.0 / (1.0 - ADAM_B1 ** ADAM_STEP)
    c2 = 1.0 / (1.0 - ADAM_B2 ** ADAM_STEP)
    nl = len(lead)

    def body(w_ref, g_ref, m_ref, v_ref, *rest):
        d_ref, nm_ref, nv_ref = rest[-3:]
        gv = g_ref[...]
        nm = ADAM_B1 * m_ref[...] + (1.0 - ADAM_B1) * gv
        nv = ADAM_B2 * v_ref[...] + (1.0 - ADAM_B2) * (gv * gv)
        nm_ref[...] = nm
        nv_ref[...] = nv
        d_ref[...] = -ADAM_LR * ((nm * c1) / (jnp.sqrt(nv * c2) + ADAM_EPS) + ADAM_WD * w_ref[...])

    spec = pl.BlockSpec((None,) * nl + (tr, cdim), lambda *idx: idx + (0,))
    extra = [] if after is None else [after]
    res = _pcall(body, name=name, grid=lead + (r // tr,),
                 in_specs=[spec] * 4 + [pl.BlockSpec(memory_space=pl.ANY)] * len(extra), out_specs=[spec] * 3,
                 out_shape=[jax.ShapeDtypeStruct(shape, F32)] * 3,
                 compiler_params=_params(("arbitrary",) * (nl + 1)))(w, g, m, v, *extra)
    return list(res)


def _ffn_half_fwd(tag, s, gains, mod, k, wts, n_lat, tm, h_tiles, tm_big, after=None):
    wg, wu, wd, idx = wts
    u = _adaln_fwd(f"adaln_{tag}", s, gains, k, mod, k, tm, h_tiles, after)
    a, b, hid = _ffn_up(f"ffn_up_{tag}", u, wg, wu, idx, tm_big)
    s_out, o = _ffn_down(f"ffn_down_{tag}", hid, wd, idx, s, mod, k, n_lat, tm_big)
    return s_out, (s, u, a, b, hid, o)


def _ffn_half_bwd(tag, ds_out, saved, gains, mod, k, wts, big_grads, tm, h_tiles, tm_big, after=None):
    wg, wu, wd, idx = wts
    g_gate, g_up, g_down = big_grads
    s, u, a, b, hid, o = saved
    d_o, acc_g = _resid_bwd(f"resid_bwd_{tag}", ds_out, o, mod, k, 0.5, tm, h_tiles, after)
    da, db = _ffn_dhid(f"ffn_dhid_{tag}", d_o, wd, idx, a, b, tm_big)
    du = _ffn_du(f"ffn_du_{tag}", da, db, wg, wu, idx, tm_big)
    d = u.shape[1]
    g_gate = _ffn_dw_in(f"ffn_dwg_{tag}", u, da, _pick(d, MM_ROWS), g_gate, idx)
    g_up = _ffn_dw_in(f"ffn_dwu_{tag}", u, db, _pick(d, MM_ROWS), g_up, idx)
    g_down = _ffn_dw_down(f"ffn_dwd_{tag}", hid, d_o, _pick(d, 512), g_down, idx)
    ds, acc_n = _adaln_bwd(f"adaln_bwd_{tag}", s, du, ds_out, gains, k, mod, k, tm, h_tiles)
    return ds, (g_gate, g_up, g_down), (acc_n[:, 0], acc_n[:, 1], acc_g[:, 0]), jnp.sum(acc_n[:, 2], axis=0)


def kernel(x, c, ctx, c_ctx, norm_g, w_mod, b_mod, ffn_w_gate, ffn_w_up, ffn_w_down, ab_w_in, pool_w, pool_scale, q_norm_g, w_uq, kv_norm_g, w_ukv, ab_w_out, conv_w_in, conv_w, conv_w_out, final_norm_g, loss_target, m_c_ctx, m_norm_g, m_w_mod, m_b_mod, m_ffn_w_gate, m_ffn_w_up, m_ffn_w_down, m_ab_w_in, m_pool_w, m_pool_scale, m_q_norm_g, m_w_uq, m_kv_norm_g, m_w_ukv, m_ab_w_out, m_conv_w_in, m_conv_w, m_conv_w_out, m_final_norm_g, v_c_ctx, v_norm_g, v_w_mod, v_b_mod, v_ffn_w_gate, v_ffn_w_up, v_ffn_w_down, v_ab_w_in, v_pool_w, v_pool_scale, v_q_norm_g, v_w_uq, v_kv_norm_g, v_w_ukv, v_ab_w_out, v_conv_w_in, v_conv_w, v_conv_w_out, v_final_norm_g):
    weights = dict(c_ctx=c_ctx, norm_g=norm_g, w_mod=w_mod, b_mod=b_mod, ffn_w_gate=ffn_w_gate, ffn_w_up=ffn_w_up,
                   ffn_w_down=ffn_w_down, ab_w_in=ab_w_in, pool_w=pool_w, pool_scale=pool_scale, q_norm_g=q_norm_g,
                   w_uq=w_uq, kv_norm_g=kv_norm_g, w_ukv=w_ukv, ab_w_out=ab_w_out, conv_w_in=conv_w_in, conv_w=conv_w,
                   conv_w_out=conv_w_out, final_norm_g=final_norm_g)
    mom_m = dict(c_ctx=m_c_ctx, norm_g=m_norm_g, w_mod=m_w_mod, b_mod=m_b_mod, ffn_w_gate=m_ffn_w_gate,
                 ffn_w_up=m_ffn_w_up, ffn_w_down=m_ffn_w_down, ab_w_in=m_ab_w_in, pool_w=m_pool_w,
                 pool_scale=m_pool_scale, q_norm_g=m_q_norm_g, w_uq=m_w_uq, kv_norm_g=m_kv_norm_g, w_ukv=m_w_ukv,
                 ab_w_out=m_ab_w_out, conv_w_in=m_conv_w_in, conv_w=m_conv_w, conv_w_out=m_conv_w_out,
                 final_norm_g=m_final_norm_g)
    mom_v = dict(c_ctx=v_c_ctx, norm_g=v_norm_g, w_mod=v_w_mod, b_mod=v_b_mod, ffn_w_gate=v_ffn_w_gate,
                 ffn_w_up=v_ffn_w_up, ffn_w_down=v_ffn_w_down, ab_w_in=v_ab_w_in, pool_w=v_pool_w,
                 pool_scale=v_pool_scale, q_norm_g=v_q_norm_g, w_uq=v_w_uq, kv_norm_g=v_kv_norm_g, w_ukv=v_w_ukv,
                 ab_w_out=v_ab_w_out, conv_w_in=v_conv_w_in, conv_w=v_conv_w, conv_w_out=v_conv_w_out,
                 final_norm_g=v_final_norm_g)

    t_len, d = x.shape[1], x.shape[2]
    g_len = ctx.shape[1]
    r_len = t_len + g_len
    fc = ffn_w_gate.shape[3]
    heads = d // 128
    pool_dim = d // 2
    q_rank, kv_rank = q_norm_g.shape[1], kv_norm_g.shape[1]
    hw = heads * HEAD_PAD
    attn_scale = 1.0 / math.sqrt(QK_NOPE + QK_ROPE)
    kvr_w = kv_rank + HEAD_PAD
    in_w = pool_dim + q_rank + kvr_w
    tm = 256 if g_len % 256 == 0 else g_len
    assert t_len % tm == 0 and g_len % tm == 0 and t_len % g_len == 0 and pool_dim % 128 == 0
    h_tiles = t_len // tm
    tm_l0 = _pick(r_len, 768, tm)
    tm_l1 = _pick(t_len, 1024, tm)

    xi, yi, ci = lax.axis_index("x"), lax.axis_index("y"), lax.axis_index("c")
    me = 4 * xi + 2 * yi + ci
    shard = 2 * xi + yi

    def halves(w):
        return w.astype(BF).reshape(2, -1, w.shape[-1])

    ffn_names = ["ffn_w_gate", "ffn_w_up", "ffn_w_down"]
    mixer_names = [["ab_w_in", "w_uq", "w_ukv", "ab_w_out"], ["conv_w_in", "conv_w_out"]]
    big_names = ffn_names + mixer_names[0] + mixer_names[1]

    def stage_names(k):
        return mixer_names[k // 3] if k % 3 == 1 else ffn_names

    def stage_halves(k):
        l, f = k // 3, (k % 3) // 2
        if k % 3 == 1:
            return [halves(weights[nm]) for nm in mixer_names[l]]
        return [halves(weights[nm][l, f]) for nm in ffn_names]

    small = jnp.concatenate([norm_g.reshape(6, -1), conv_w[0]], axis=0)
    small = jnp.pad(small, ((0, 7), (0, 0)))
    c_row = jnp.pad(c, ((0, 7), (0, 0)))
    small_all, c_all = _gather_all("gather_small", [small, c_row])
    small_full = small_all[::2].transpose(1, 0, 2).reshape(16, d)
    gains = [jnp.pad(small_full[3 * l:3 * l + 3], ((0, 5), (0, 0))) for l in range(2)]
    conv_w_full = small_full[6:9]
    c16 = jnp.concatenate([c_all[:, 0], c_ctx[None], jnp.zeros((7, d), F32)], axis=0)

    n_col = w_mod.shape[2]
    b_sh = lax.dynamic_slice_in_dim(b_mod, shard * n_col, n_col, axis=1)
    m_sh = [_mm(f"mod_fwd_{l}", c16, w_mod, 'nn', tm=16, tn=768, a_pre=_silu, b_lead=l,
                epi=lambda acc, i, bv: (acc + bv,), epi_args=(b_sh[l:l + 1],), epi_kinds=('n',)) for l in range(2)]
    m_all = _gather_all("gather_mod", [jnp.concatenate(m_sh, axis=0)])[0]
    m_full = m_all[::2].reshape(N_SHARD, 2, 16, n_col).transpose(1, 2, 0, 3).reshape(2, 16, N_MOD * d)
    mod_h = [jnp.pad(lax.dynamic_index_in_dim(m_full[l], me, 0, keepdims=False).reshape(N_MOD, d), ((0, 7), (0, 0)))
             for l in range(2)]
    mod_g0 = jnp.pad(m_full[0, 8].reshape(N_MOD, d), ((0, 7), (0, 0)))
    mods = [jnp.stack([mod_h[0], mod_g0]), mod_h[1][None]]

    def gather_start(k, dep):
        own = lax.optimization_barrier((tuple(stage_halves(k)), dep))[0]
        n = len(own)
        return _split_start(f"gather_start_s{k}", list(own) + _landing(N_DEV, own), n * len(CHIP_FLIPS),
                            _chips_gather_build(n))

    def stage_weights(k, handle, after):
        bufs = _split_wait(f"gather_wait_s{k}", handle, after)
        n = len(bufs) // 2
        own = bufs[:n]
        fwd = _split_start(f"forward_start_s{k}", bufs[n:], n * len(CHIP_FLIPS), _sibling_forward_build(n))
        nxt = gather_start(k + 1, fwd['token']) if k + 1 < 6 else None
        landed = _split_wait(f"forward_wait_s{k}", fwd, fwd['token'])
        full = [lax.dynamic_update_slice_in_dim(z, a, 2 * shard, 0) for z, a in zip(landed, own)]
        gw = {nm: g.reshape(N_SHARD, 2 * g.shape[1], g.shape[2]) for nm, g in zip(stage_names(k), full)}
        return gw, nxt, (fwd['token'] if nxt is None else nxt['token'])

    def ffn_weights(gw):
        return gw["ffn_w_gate"].reshape(N_SHARD, 1, d, fc), gw["ffn_w_up"].reshape(N_SHARD, 1, d, fc), \
            gw["ffn_w_down"], 0

    gather0 = gather_start(0, m_all)
    gw_s0, gather1, tok0 = stage_weights(0, gather0, gather0['token'])
    ffn_w = [[ffn_weights(gw_s0), None], [None, None]]

    s0 = jnp.concatenate([x[0], ctx[0]], axis=0)
    s1, sav_f00 = _ffn_half_fwd("l0a", s0, gains[0], mods[0], 0, ffn_w[0][0], t_len, tm, h_tiles, tm_l0, tok0)

    gw_s1, gather2, tok1 = stage_weights(1, gather1, s1)
    w_out_full = gw_s1["ab_w_out"].reshape(-1, d)
    w_uq_full = gw_s1["w_uq"].reshape(q_rank, heads * (QK_NOPE + QK_ROPE))
    w_ukv_full = gw_s1["w_ukv"].transpose(1, 0, 2).reshape(kv_rank, heads * (QK_NOPE + V_HEAD))
    w_in_full = gw_s1["ab_w_in"].transpose(1, 0, 2).reshape(d, -1)

    wq_p = jnp.pad(w_uq_full.reshape(q_rank, heads, QK_NOPE + QK_ROPE),
                   ((0, 0), (0, 0), (0, HEAD_PAD - QK_NOPE - QK_ROPE))).reshape(q_rank, hw)
    ukv3 = w_ukv_full.reshape(kv_rank, heads, QK_NOPE + V_HEAD)
    wk_top = jnp.pad(ukv3[..., :QK_NOPE], ((0, 0), (0, 0), (0, HEAD_PAD - QK_NOPE))).reshape(kv_rank, hw)
    wv_top = jnp.pad(ukv3[..., QK_NOPE:], ((0, 0), (0, 0), (0, HEAD_PAD - V_HEAD))).reshape(kv_rank, hw)
    src_row = lax.broadcasted_iota(jnp.int32, (HEAD_PAD, hw), 0)
    dst_lane = lax.broadcasted_iota(jnp.int32, (HEAD_PAD, hw), 1) % HEAD_PAD
    spread = ((src_row < QK_ROPE) & (dst_lane == src_row + QK_NOPE)).astype(BF)
    wk_ext = jnp.concatenate([wk_top, spread], axis=0)
    wv_ext = jnp.concatenate([wv_top, jnp.zeros((HEAD_PAD, hw), BF)], axis=0)
    w_in_pool = w_in_full[:, :pool_dim]
    w_in_q = w_in_full[:, pool_dim:pool_dim + q_rank]
    w_in_kvr = jnp.pad(w_in_full[:, pool_dim + q_rank:], ((0, 0), (0, HEAD_PAD - QK_ROPE)))
    w_out_attn = jnp.pad(w_out_full[pool_dim:].reshape(heads, V_HEAD, d),
                         ((0, 0), (0, HEAD_PAD - V_HEAD), (0, 0))).reshape(hw, d)
    w_out_p = jnp.concatenate([w_out_full[:pool_dim], w_out_attn], axis=0)

    u_mix = _adaln_fwd("adaln_l0m", s1, gains[0], 1, mods[0], 1, tm, h_tiles, tok1)
    p_pool = _mm("in_pool", u_mix, w_in_pool, 'nn', tm=tm_l0, tn=pool_dim)
    p_q = _mm("in_q", u_mix, w_in_q, 'nn', tm=tm_l0, tn=q_rank)
    p_kvr = _mm("in_kvr", u_mix, w_in_kvr, 'nn', tm=tm_l0, tn=kvr_w)
    qg = jnp.pad(q_norm_g, ((0, 7), (0, 0)))
    kvg = jnp.pad(kv_norm_g, ((0, 7), (0, 0)))
    tq_c, tq_s = _rope_tables(t_len, g_len, QK_NOPE)
    tk_c, tk_s = _rope_tables(t_len, g_len, 0)

    def qn_fn(rows, consts, m):
        n, _ = _rms(rows[0])
        return [n * consts[0][0:1, :]], {}

    qn = _rows_call("q_norm", qn_fn, r_len, tm, [p_q], [qg], None, [(q_rank, BF)])[0]
    q_r = _mm("q_up", qn, wq_p, 'nn', tm=tm_l0, tn=hw, out_dtypes=(BF,),
              epi=lambda acc, i, ct, st: (_rope(acc, ct, st),), epi_args=(tq_c, tq_s), epi_kinds=('mt', 'mt'))

    def kvn_fn(rows, consts, m):
        pv, ct, st = rows
        n, _ = _rms(pv[:, :kv_rank])
        return [jnp.concatenate([n * consts[0][0:1, :], _rope(pv[:, kv_rank:], ct, st)], axis=1)], {}

    kvn = _rows_call("kv_norm", kvn_fn, r_len, tm, [p_kvr, tk_c, tk_s], [kvg], None, [(kvr_w, BF)])[0]
    k_p = _mm("k_up", kvn, wk_ext, 'nn', tm=tm_l0, tn=hw, out_dtypes=(BF,))
    v_p = _mm("v_up", kvn, wv_ext, 'nn', tm=tm_l0, tn=hw, out_dtypes=(BF,))
    tq_h = _pick(t_len, 512, tm)
    o_h, lse_h = _attn_fwd("attn_h", q_r, k_p, v_p, t_len, 0, r_len, 0, heads, tm, attn_scale)
    o_g, lse_g = _attn_fwd("attn_g", q_r, k_p, v_p, g_len, t_len, g_len, t_len // g_len, heads, tm, attn_scale)
    y_h = _pool_fwd("pool_h", p_pool[:t_len], pool_w[0], pool_scale)
    y_g = _pool_fwd("pool_g", p_pool[t_len:], pool_w[0], pool_scale)
    cat = jnp.concatenate([jnp.concatenate([y_h, y_g], axis=0), jnp.concatenate([o_h, o_g], axis=0)], axis=1)

    def resid_epi(k3, n_lat, tmr):
        def epi(acc, i, sv, mv):
            return sv + _row_gate(mv, k3, i, tmr, n_lat) * acc, acc
        return epi

    s2, o_mix0 = _mm("mix_out_l0", cat, w_out_p, 'nn', tm=tm_l0, tn=d, out_dtypes=(F32, F32),
                     epi=resid_epi(5, t_len, tm_l0), epi_args=(s1, mods[0]), epi_kinds=('mn', 'w'))
    gw_s2, gather3, tok2 = stage_weights(2, gather2, s2)
    ffn_w[0][1] = ffn_weights(gw_s2)
    s3, sav_f01 = _ffn_half_fwd("l0b", s2, gains[0], mods[0], 2, ffn_w[0][1], t_len, tm, h_tiles, tm_l0, tok2)

    gw_s3, gather4, tok3 = stage_weights(3, gather3, s3)
    ffn_w[1][0] = ffn_weights(gw_s3)
    tml = 256 if t_len % 256 == 0 else tm
    h3 = s3[:t_len]
    h4, sav_f10 = _ffn_half_fwd("l1a", h3, gains[1], mods[1], 0, ffn_w[1][0], t_len, tml, None, tm_l1, tok3)
    gw_s4, gather5, tok4 = stage_weights(4, gather4, h4)
    cw_out_full = gw_s4["conv_w_out"].reshape(-1, d)
    cw_in_full = gw_s4["conv_w_in"].transpose(1, 0, 2).reshape(d, -1)
    u_cv = _adaln_fwd("adaln_l1m", h4, gains[1], 1, mods[1], 1, tml, None, tok4)
    p3 = _mm("conv_in", u_cv, cw_in_full, 'nn', tm=tm_l1, tn=512)
    cwp = conv_w_full
    tc = _pick(d, 256)
    y_cv = _conv_fwd("conv_fwd", p3, cwp, tc)
    h5, o_mix1 = _mm("mix_out_l1", y_cv, cw_out_full, 'nn', tm=tm_l1, tn=d, out_dtypes=(F32, F32),
                     epi=resid_epi(5, t_len, tm_l1), epi_args=(h4, mods[1]), epi_kinds=('mn', 'w'))
    gw_s5, _, tok5 = stage_weights(5, gather5, h5)
    ffn_w[1][1] = ffn_weights(gw_s5)
    h6, sav_f11 = _ffn_half_fwd("l1b", h5, gains[1], mods[1], 2, ffn_w[1][1], t_len, tml, None, tm_l1, tok5)

    fg = jnp.pad(final_norm_g[None], ((0, 7), (0, 0)))
    dh6, acc_loss = _loss_head("loss_head", h6, loss_target[0], fg, tml)
    d_final_g = acc_loss[0, 0]

    dgain = [[None] * 3 for _ in range(2)]
    dmod = [[None] * N_MOD for _ in range(2)]

    def put(l, k, triple):
        dmod[l][3 * k], dmod[l][3 * k + 1], dmod[l][3 * k + 2] = triple

    def empty_ffn_grads():
        return (lax.empty((N_SHARD, d, fc), BF), lax.empty((N_SHARD, d, fc), BF), lax.empty((N_SHARD, fc, d), BF))

    def by_shard_rows(g):
        return g.reshape(N_SHARD, -1, g.shape[-1])

    def by_shard_cols(g):
        return g.reshape(g.shape[0], N_SHARD, -1).transpose(1, 0, 2)

    def pair_start(k, big):
        send = [b.astype(BF).reshape(N_DEV, b.shape[1] // 2, b.shape[2]) for b in big]
        n = len(send)
        return _split_start(f"grads_pair_start_s{k}", send + _landing(N_SHARD, send), n * N_SHARD,
                            _sibling_halves_build(n))

    def chips_start(k, handle, after):
        bufs = _split_wait(f"grads_pair_wait_s{k}", handle, after)
        n = len(bufs) // 2
        pre = [_add_halves(f"grads_add_s{k}_{nm}", s, z) for nm, s, z in zip(stage_names(k), bufs[:n], bufs[n:])]
        return _split_start(f"grads_start_s{k}", pre + _landing(N_SHARD, pre), n * len(CHIP_FLIPS),
                            _chips_scatter_build(n))

    def landed_sums(k, handle, after):
        bufs = _split_wait(f"grads_wait_s{k}", handle, after)
        n = len(bufs) // 2
        landed = [lax.dynamic_update_slice_in_dim(z, lax.dynamic_slice_in_dim(p, shard, 1, 0), shard, 0)
                  for p, z in zip(bufs[:n], bufs[n:])]
        return [_sum_lead(f"sum_grads_s{k}_{nm}", z) for nm, z in zip(stage_names(k), landed)]

    pair, scatter, sums = [None] * 6, [None] * 6, [None] * 6
    dh5, ffn_g, tr, dgain[1][2] = _ffn_half_bwd("l1b", dh6, sav_f11, gains[1], mods[1], 2, ffn_w[1][1],
                                                empty_ffn_grads(), tml, None, tm_l1)
    put(1, 2, tr)
    pair[5] = pair_start(5, list(ffn_g))
    d_o1, acc_g1 = _resid_bwd("resid_bwd_l1m", dh5, o_mix1, mods[1], 1, 1.0, tml, None, pair[5]['token'])
    dy_cv = _mm("mix_out_l1_dx", d_o1, cw_out_full, 'nt', tm=tm_l1, tn=d)
    d_cw_out = _mm("mix_out_l1_dw", y_cv, d_o1, 'tn', tm=512, tn=512)
    dp3, d_cw = _conv_bwd("conv_bwd", p3, cwp, dy_cv, tc)
    du_cv = _conv_din("conv_in_dx", dp3, cw_in_full, tml)
    d_cw_in = _conv_dw_in("conv_in_dw", u_cv, dp3, _pick(d, 256), _pick(d, 512))
    dh4, acc_n1 = _adaln_bwd("adaln_bwd_l1m", h4, du_cv, dh5, gains[1], 1, mods[1], 1, tml, None)
    put(1, 1, (acc_n1[:, 0], acc_n1[:, 1], acc_g1[:, 0]))
    dgain[1][1] = acc_n1[0, 2]
    scatter[5] = chips_start(5, pair[5], dh4)
    pair[4] = pair_start(4, [by_shard_cols(d_cw_in), by_shard_rows(d_cw_out)])
    dh3, ffn_g, tr, dgain[1][0] = _ffn_half_bwd("l1a", dh4, sav_f10, gains[1], mods[1], 0, ffn_w[1][0],
                                                empty_ffn_grads(), tml, None, tm_l1,
                                                scatter[5]['token'] + pair[4]['token'])
    put(1, 0, tr)
    sums[5] = landed_sums(5, scatter[5], dh3)
    scatter[4] = chips_start(4, pair[4], dh3)
    pair[3] = pair_start(3, list(ffn_g))

    ds3 = jnp.concatenate([dh3, jnp.zeros((g_len, d), F32)], axis=0) \
        + (scatter[4]['token'][0, 0] + pair[3]['token'][0, 0])
    ds2, ffn_g, tr, dgain[0][2] = _ffn_half_bwd("l0b", ds3, sav_f01, gains[0], mods[0], 2, ffn_w[0][1],
                                                empty_ffn_grads(), tm, h_tiles, tm_l0)
    put(0, 2, tr)
    sums[4] = landed_sums(4, scatter[4], ds2)
    scatter[3] = chips_start(3, pair[3], ds2)
    pair[2] = pair_start(2, list(ffn_g))
    d_o0, acc_g0 = _resid_bwd("resid_bwd_l0m", ds2, o_mix0, mods[0], 1, 1.0, tm, h_tiles,
                              scatter[3]['token'] + pair[2]['token'])
    dcat = _mm("mix_out_l0_dx", d_o0, w_out_p, 'nt', tm=tm_l0, tn=pool_dim + hw)
    d_w_out_p = _mm("mix_out_l0_dw", cat, d_o0, 'tn', tm=512, tn=512)
    col_blk = pool_dim // HEAD_PAD
    dq_h, dk_h, dv_h = _attn_bwd("attn_bwd_h", q_r, k_p, v_p, cat, dcat, lse_h, t_len, 0, r_len, 0, heads, tq_h,
                                 attn_scale, col_blk)
    dq_g, dk_all, dv_all = _attn_bwd("attn_bwd_g", q_r, k_p, v_p, cat, dcat, lse_g, g_len, t_len, g_len,
                                     t_len // g_len, heads, tm, attn_scale, col_blk, onto=(dk_h, dv_h))
    dq_all = jnp.concatenate([dq_h, dq_g], axis=0)
    dkvn = _mm("k_up_dx", dk_all, wk_ext, 'nt', tm=tm_l0, tn=kvr_w)
    dkvn = _mm("v_up_dx", dv_all, wv_ext, 'nt', tm=tm_l0, tn=kvr_w, epi=lambda acc, i, prev: (acc + prev,),
               epi_args=(dkvn,), epi_kinds=('mn',))
    d_wk_ext = _mm("k_up_dw", kvn, dk_all, 'tn', tm=kvr_w, tn=512)
    d_wv_ext = _mm("v_up_dw", kvn, dv_all, 'tn', tm=kvr_w, tn=512)

    def kvn_bwd_fn(rows, consts, m):
        pv, dv_, ct, st = rows
        g = consts[0][0:1, :]
        n, r = _rms(pv[:, :kv_rank])
        dyn = dv_[:, :kv_rank]
        dckv = _rms_bwd(dyn * g, n, r)
        dkr = _rope_t(dv_[:, kv_rank:], ct, st)
        return [jnp.concatenate([dckv, dkr], axis=1)], {0: jnp.sum(dyn * n, axis=0, keepdims=True)}

    dp_kvr, acc_kvg = _rows_call("kv_norm_bwd", kvn_bwd_fn, r_len, tm, [p_kvr, dkvn, tk_c, tk_s], [kvg], None,
                                 [(kvr_w, BF)], acc_w=kv_rank)

    def qrope_bwd_fn(rows, consts, m):
        return [_rope_t(rows[0], rows[1], rows[2])], {}

    dq_pad = _rows_call("q_rope_bwd", qrope_bwd_fn, r_len, tm, [dq_all, tq_c, tq_s], [], None, [(hw, BF)])[0]
    dqn = _mm("q_up_dx", dq_pad, wq_p, 'nt', tm=tm_l0, tn=q_rank)
    d_wq_p = _mm("q_up_dw", qn, dq_pad, 'tn', tm=512, tn=512)

    def qn_bwd_fn(rows, consts, m):
        pv, dv_ = rows
        g = consts[0][0:1, :]
        n, r = _rms(pv)
        return [_rms_bwd(dv_ * g, n, r)], {0: jnp.sum(dv_ * n, axis=0, keepdims=True)}

    dp_q, acc_qg = _rows_call("q_norm_bwd", qn_bwd_fn, r_len, tm, [p_q, dqn], [qg], None, [(q_rank, BF)],
                              acc_w=q_rank)
    dpu_h, dpw_h, dps_h = _pool_bwd("pool_bwd_h", p_pool[:t_len], dcat, pool_w[0], pool_scale, 0)
    dpu_g, dpw_g, dps_g = _pool_bwd("pool_bwd_g", p_pool[t_len:], dcat, pool_w[0], pool_scale, t_len)
    dp_pool = jnp.concatenate([dpu_h, dpu_g], axis=0)
    add_prev = lambda acc, i, prev: (acc + prev,)
    du_mix = _mm("in_pool_dx", dp_pool, w_in_pool, 'nt', tm=tm_l0, tn=d)
    du_mix = _mm("in_q_dx", dp_q, w_in_q, 'nt', tm=tm_l0, tn=d, epi=add_prev, epi_args=(du_mix,), epi_kinds=('mn',))
    du_mix = _mm("in_kvr_dx", dp_kvr, w_in_kvr, 'nt', tm=tm_l0, tn=d, epi=add_prev, epi_args=(du_mix,), epi_kinds=('mn',))
    d_w_in = jnp.concatenate([
        _mm("in_pool_dw", u_mix, dp_pool, 'tn', tm=512, tn=pool_dim),
        _mm("in_q_dw", u_mix, dp_q, 'tn', tm=512, tn=q_rank),
        _mm("in_kvr_dw", u_mix, dp_kvr, 'tn', tm=512, tn=kvr_w)[:, :kv_rank + QK_ROPE]], axis=1)
    ds1, acc_n0 = _adaln_bwd("adaln_bwd_l0m", s1, du_mix, ds2, gains[0], 1, mods[0], 1, tm, h_tiles)
    put(0, 1, (acc_n0[:, 0], acc_n0[:, 1], acc_g0[:, 0]))
    dgain[0][1] = jnp.sum(acc_n0[:, 2], axis=0)
    d_w_uq = d_wq_p.reshape(q_rank, heads, HEAD_PAD)[..., :QK_NOPE + QK_ROPE].reshape(q_rank, -1)
    d_w_ukv = jnp.concatenate([d_wk_ext[:kv_rank].reshape(kv_rank, heads, HEAD_PAD)[..., :QK_NOPE],
                               d_wv_ext[:kv_rank].reshape(kv_rank, heads, HEAD_PAD)[..., :V_HEAD]],
                              axis=-1).reshape(kv_rank, -1)
    d_w_out = jnp.concatenate([d_w_out_p[:pool_dim],
                               d_w_out_p[pool_dim:].reshape(heads, HEAD_PAD, d)[:, :V_HEAD].reshape(-1, d)], axis=0)
    sums[3] = landed_sums(3, scatter[3], ds1)
    scatter[2] = chips_start(2, pair[2], ds1)
    pair[1] = pair_start(1, [by_shard_cols(d_w_in), by_shard_rows(d_w_uq), by_shard_cols(d_w_ukv),
                             by_shard_rows(d_w_out)])
    ds0, ffn_g, tr, dgain[0][0] = _ffn_half_bwd("l0a", ds1, sav_f00, gains[0], mods[0], 0, ffn_w[0][0],
                                                empty_ffn_grads(), tm, h_tiles, tm_l0,
                                                scatter[2]['token'] + pair[1]['token'])
    put(0, 0, tr)
    grad_x = ds0[:t_len][None]
    sums[2] = landed_sums(2, scatter[2], ds0)
    pair[0] = pair_start(0, list(ffn_g))

    dmh = jnp.stack([jnp.stack([dmod[l][k][0] for k in range(N_MOD)]) for l in range(2)])
    dmg0 = jnp.stack([dmod[0][k][1] for k in range(N_MOD)])
    dg_rows = jnp.stack([dgain[l][k] for l in range(2) for k in range(3)])
    pieces = [dmh.reshape(2 * N_MOD, d), dmg0, dg_rows, d_cw[:3], d_final_g[None],
              (dpw_h + dpw_g).reshape(-1, d), jnp.pad((dps_h + dps_g)[0], (0, d - pool_dim))[None],
              jnp.pad(acc_qg[0, 0], (0, d - q_rank))[None], jnp.pad(acc_kvg[0, 0], (0, d - kv_rank))[None],
              acc_loss[0, 1][None]]
    n_piece = [p.shape[0] for p in pieces]
    pieces = [jnp.pad(p, ((0, (-p.shape[0]) % 8), (0, 0))) for p in pieces]
    small_g = jnp.concatenate(pieces, axis=0) + pair[0]['token'][0, 0]
    sg_all = _gather_all("gather_small_grads", [small_g])[0]
    sg_sum = _sum_lead("sum_small_grads", sg_all)
    scatter[1] = chips_start(1, pair[1], sg_sum)
    offs = [0]
    for p in pieces:
        offs.append(offs[-1] + p.shape[0])
    part = lambda j: sg_sum[offs[j]:offs[j] + n_piece[j]]
    sum_dmh, sum_dmg0, g_norm_full, g_conv_w_full = part(0).reshape(2, N_MOD * d), part(1).reshape(N_MOD * d), part(2), part(3)
    g_final = part(4)[0]
    loss = part(9)[0, 0]
    g_pool_w = part(5).reshape(pool_w.shape)
    g_pool_scale = part(6)[:, :pool_dim]
    g_q_norm = part(7)[:, :q_rank]
    g_kv_norm = part(8)[:, :kv_rank]
    col0 = shard * (d // N_SHARD)
    g_norm_g = lax.dynamic_slice_in_dim(g_norm_full.reshape(2, 3, d), col0, d // N_SHARD, axis=2)
    g_conv_w = lax.dynamic_slice_in_dim(g_conv_w_full, col0, d // N_SHARD, axis=1)[None]
    g_b_mod = _sum_lead("sum_b_mod", jnp.stack([sum_dmh, jnp.stack([sum_dmg0, jnp.zeros_like(sum_dmg0)])]))

    dm16 = []
    for l in range(2):
        per_dev = sg_all[:, l * N_MOD:(l + 1) * N_MOD].reshape(N_DEV, N_MOD * d)
        row8 = (sum_dmg0 if l == 0 else jnp.zeros_like(sum_dmg0)) + scatter[1]['token'][0, 0]
        full = jnp.concatenate([per_dev, row8[None], jnp.zeros((7, N_MOD * d), F32)], axis=0)
        dm16.append(lax.dynamic_slice_in_dim(full, shard * n_col, n_col, axis=1))
    g_w_mod = jnp.stack([_mm(f"mod_dw_{l}", c16, dm16[l], 'tn', tm=512, tn=768, a_pre=_silu) for l in range(2)])
    dc16 = _mm("mod_dx", dm16[0], w_mod, 'nt', tm=16, tn=512, b_lead=0, epi=lambda acc, i, cv: (acc * _dsilu(cv),),
               epi_args=(c16,), epi_kinds=('mn',))
    dc_all = _gather_all("gather_dc", [dc16])[0]
    g_c_ctx = _sum_lead("sum_dc", dc_all[::2])[8]

    grads = dict(c_ctx=g_c_ctx, norm_g=g_norm_g, w_mod=g_w_mod, b_mod=g_b_mod, pool_w=g_pool_w,
                 pool_scale=g_pool_scale, q_norm_g=g_q_norm, kv_norm_g=g_kv_norm, conv_w=g_conv_w, final_norm_g=g_final)
    names = list(weights)

    sums[1] = landed_sums(1, scatter[1], g_c_ctx)
    scatter[0] = chips_start(0, pair[0], g_c_ctx)
    upd = {n: _adamw(f"adamw_{n}", weights[n], grads[n].reshape(weights[n].shape), mom_m[n], mom_v[n],
                     scatter[0]['token']) for n in names if n not in big_names}
    sums[0] = landed_sums(0, scatter[0], upd["w_mod"][0])

    halves_sum = [s for k in range(6) for s in sums[k]]
    swap = _split_start("swap_start", halves_sum + [pltpu.with_memory_space_constraint(lax.empty(s.shape, s.dtype),
                                                                                        pltpu.HBM) for s in halves_sum],
                        len(halves_sum), _sibling_whole_build(len(halves_sum)))
    both = _split_wait("swap_wait", swap, swap['token'])
    south = ci == 0
    swapped = iter([jnp.where(south, jnp.stack([a, g]), jnp.stack([g, a]))
                    for a, g in zip(both[:len(halves_sum)], both[len(halves_sum):])])
    by_stage = [{nm: next(swapped) for nm in stage_names(k)} for k in range(6)]
    for nm in ffn_names:
        grads[nm] = jnp.stack([jnp.stack([by_stage[3 * l + 2 * f][nm].reshape(weights[nm].shape[2:])
                                          for f in range(2)]) for l in range(2)])
    for l in range(2):
        for nm in mixer_names[l]:
            grads[nm] = by_stage[3 * l + 1][nm].reshape(weights[nm].shape)
    upd.update({n: _adamw(f"adamw_{n}", weights[n], grads[n], mom_m[n], mom_v[n]) for n in big_names})
    return (loss, grad_x, *[grads[n].reshape(weights[n].shape) for n in names], *[upd[n][0] for n in names],
            *[upd[n][1] for n in names], *[upd[n][2] for n in names])
```

```python
import functools
import math

import jax
import jax.numpy as jnp
from jax import lax
from jax.experimental import pallas as pl
from jax.experimental.pallas import tpu as pltpu

F32 = jnp.float32
BF = jnp.bfloat16
MESH = pl.DeviceIdType.MESH

N_DEV = 8
N_SHARD = 4
RMS_EPS = 1e-6
N_MOD = 9
POOL_WINDOWS = (2, 4, 8, 16)
QK_NOPE = 64
QK_ROPE = 32
V_HEAD = 64
HEAD_PAD = 128
GRID_W = 64
ROPE_THETA = 10000.0
POOL_PAD = 16
ADAM_LR, ADAM_B1, ADAM_B2, ADAM_EPS, ADAM_WD, ADAM_STEP = 0.001, 0.9, 0.999, 1e-08, 0.01, 10
VMEM_LIMIT = 56 * 1024 * 1024
MM_ROWS = 1024


def _pcall(body, **kw):
    return pl.pallas_call(body, **kw)


def _params(sem=None):
    return pltpu.CompilerParams(dimension_semantics=sem, vmem_limit_bytes=VMEM_LIMIT)


def _pick(n, pref, mult=128):
    best = None
    d = mult
    while d <= min(n, pref):
        if n % d == 0:
            best = d
        d += mult
    return best if best is not None else n


def _silu(z):
    return z * jax.nn.sigmoid(z)


def _dsilu(z):
    s = jax.nn.sigmoid(z)
    return s * (1.0 + z * (1.0 - s))


def _dot(a, b, dims):
    return lax.dot_general(a.astype(BF), b.astype(BF), (dims, ((), ())), preferred_element_type=F32)


NN = ((1,), (0,))
NT = ((1,), (1,))
TN = ((0,), (0,))


ALL_FLIPS = [(kx, ky, kc) for kx in (0, 1) for ky in (0, 1) for kc in (0, 1) if (kx, ky, kc) != (0, 0, 0)]
CHIP_FLIPS = [(1, 0, 0), (0, 1, 0), (1, 1, 0)]
SIBLING = (0, 0, 1)
COMM_SPLIT = 8
SPLIT_MIN_ROWS = 256


def _exchange(name, arrays, plan, lead, whole_src, split=COMM_SPLIT):
    n = len(arrays)
    blk_shapes = [tuple(a.shape) if whole_src else tuple(a.shape[1:]) for a in arrays]
    splits = []
    for shp in blk_shapes:
        s = 1
        while s * 2 <= split and shp[0] % (s * 2) == 0 and (shp[0] // (s * 2)) % 16 == 0 \
                and shp[0] // (s * 2) >= SPLIT_MIN_ROWS:
            s *= 2
        splits.append(s)
    items = plan(0, 0, 0)
    n_items = len(items)
    remote_ids = [k for k, it in enumerate(items) if it[0] is not None]
    local_ids = [k for k, it in enumerate(items) if it[0] is None]
    slots = [(a, s) for s in range(max(splits)) for a in range(n) if s < splits[a]]
    n_slot = len(slots)

    def body(*refs):
        ins, outs = refs[:n], refs[n:2 * n]
        send_sems, recv_sems, loc_sems = refs[2 * n:]
        x, y, c = lax.axis_index("x"), lax.axis_index("y"), lax.axis_index("c")
        plan_here = plan(x, y, c)

        def rows(ref, a, s):
            rc = blk_shapes[a][0] // splits[a]
            return ref.at[pl.ds(s * rc, rc)]

        def make(si, k):
            a, s = slots[si]
            flip, src, dst, _ = plan_here[k]
            base = outs[a] if src[0] == 'out' else ins[a]
            src_ref = rows(base if src[1] is None else base.at[src[1]], a, s)
            dst_ref = rows(outs[a].at[dst], a, s)
            if flip is None:
                return pltpu.make_async_copy(src_ref, dst_ref, loc_sems.at[si * max(1, len(local_ids)) + local_ids.index(k)])
            peer = (1 - x if flip[0] else x, 1 - y if flip[1] else y, 1 - c if flip[2] else c)
            sem = si * len(remote_ids) + remote_ids.index(k)
            return pltpu.make_async_remote_copy(src_ref=src_ref, dst_ref=dst_ref, send_sem=send_sems.at[sem],
                                                recv_sem=recv_sems.at[sem], device_id=peer, device_id_type=MESH)

        copies = {}
        for si in range(n_slot):
            for k in range(n_items):
                if plan_here[k][3] is None:
                    copies[si, k] = make(si, k)
                    copies[si, k].start()
        arrived = set()
        for si in range(n_slot):
            for k in range(n_items):
                after = plan_here[k][3]
                if after is not None:
                    if (si, after) not in arrived:
                        copies[si, after].wait_recv()
                        arrived.add((si, after))
                    copies[si, k] = make(si, k)
                    copies[si, k].start()
        for (si, k), cp in copies.items():
            if plan_here[k][0] is None:
                cp.wait()
            else:
                cp.wait_send()
                if (si, k) not in arrived:
                    cp.wait_recv()

    any_spec = pl.BlockSpec(memory_space=pl.ANY)
    n_rem = max(1, n_slot * len(remote_ids))
    outs = _pcall(
        body, name=name,
        out_shape=[jax.ShapeDtypeStruct((lead,) + s, a.dtype) for s, a in zip(blk_shapes, arrays)],
        in_specs=[any_spec] * n, out_specs=[any_spec] * n,
        scratch_shapes=[pltpu.SemaphoreType.DMA((n_rem,)), pltpu.SemaphoreType.DMA((n_rem,)),
                        pltpu.SemaphoreType.DMA((max(1, n_slot * len(local_ids)),))],
    )(*arrays)
    return list(outs)


def _place(x, y, c):
    return 4 * x + 2 * y + c


def _flip(v, f):
    return 1 - v if f else v


def _gather_all(name, arrays):
    def plan(x, y, c):
        me = _place(x, y, c)
        return [(None, ('in', None), me, None)] + [(f, ('in', None), me, None) for f in ALL_FLIPS]
    return _exchange(name, arrays, plan, N_DEV, True)


HBM_SPEC = pl.BlockSpec(memory_space=pltpu.HBM)
SEM_SPEC = pl.BlockSpec(memory_space=pltpu.SEMAPHORE)
SIDE_EFFECT = pltpu.SideEffectType.DATAFLOW_SIDE_EFFECTING


def _split_start(name, bufs, n_copies, build):
    n = len(bufs)

    def body(*refs):
        for cp in build(refs[:n], refs[n], refs[n + 1]):
            cp.start()
        token = refs[-1]
        token[...] = jnp.zeros_like(token)

    res = _pcall(
        body, name=name,
        out_shape=(pltpu.SemaphoreType.DMA((n_copies,)), pltpu.SemaphoreType.DMA((n_copies,)),
                   *[pltpu.HBM(b.shape, b.dtype) for b in bufs], jax.ShapeDtypeStruct((8, 128), F32)),
        in_specs=[HBM_SPEC] * n,
        out_specs=(SEM_SPEC, SEM_SPEC, *[HBM_SPEC] * n, pl.BlockSpec(memory_space=pltpu.VMEM)),
        input_output_aliases={i: 2 + i for i in range(n)},
        compiler_params=pltpu.CompilerParams(has_side_effects=SIDE_EFFECT),
    )(*[pltpu.with_memory_space_constraint(b, pltpu.HBM) for b in bufs])
    return dict(send=res[0], recv=res[1], bufs=list(res[2:2 + n]), token=res[-1], build=build)


def _split_wait(name, handle, after):
    n = len(handle['bufs'])
    build = handle['build']

    def body(*refs):
        for cp in build(refs[:n], refs[n], refs[n + 1]):
            cp.wait_send()
            cp.wait_recv()

    res = _pcall(
        body, name=name, out_shape=tuple(pltpu.HBM(b.shape, b.dtype) for b in handle['bufs']),
        in_specs=[HBM_SPEC] * n + [SEM_SPEC, SEM_SPEC, pl.BlockSpec(memory_space=pl.ANY)],
        out_specs=tuple([HBM_SPEC] * n), input_output_aliases={i: i for i in range(n)},
        compiler_params=pltpu.CompilerParams(has_side_effects=SIDE_EFFECT),
    )(*handle['bufs'], handle['send'], handle['recv'], after)
    return list(res)


def _landing(lead, arrays):
    return [pltpu.with_memory_space_constraint(lax.empty((lead,) + tuple(a.shape[1:]), a.dtype), pltpu.HBM)
            for a in arrays]


def _copy_list(n, per_array, make):
    def build(refs, send_sems, recv_sems):
        copies = []
        for a in range(n):
            for j in range(per_array):
                src, dst, peer = make(refs, a, j)
                k = a * per_array + j
                copies.append(pltpu.make_async_remote_copy(src_ref=src, dst_ref=dst, send_sem=send_sems.at[k],
                                                           recv_sem=recv_sems.at[k], device_id=peer,
                                                           device_id_type=MESH))
        return copies
    return build


def _mesh_place():
    x, y, c = lax.axis_index("x"), lax.axis_index("y"), lax.axis_index("c")
    return x, y, c, 2 * x + y


def _chips_gather_build(n):
    def make(refs, a, j):
        x, y, c, chip = _mesh_place()
        px, py = _flip(x, CHIP_FLIPS[j][0]), _flip(y, CHIP_FLIPS[j][1])
        return refs[a].at[c], refs[n + a].at[2 * chip + c], (px, py, c)
    return _copy_list(n, len(CHIP_FLIPS), make)


def _chips_scatter_build(n):
    def make(refs, a, j):
        x, y, c, chip = _mesh_place()
        px, py = _flip(x, CHIP_FLIPS[j][0]), _flip(y, CHIP_FLIPS[j][1])
        return refs[a].at[2 * px + py], refs[n + a].at[chip], (px, py, c)
    return _copy_list(n, len(CHIP_FLIPS), make)


def _sibling_forward_build(n):
    def make(refs, a, j):
        x, y, c, _ = _mesh_place()
        blk = 2 * (2 * _flip(x, CHIP_FLIPS[j][0]) + _flip(y, CHIP_FLIPS[j][1])) + c
        return refs[a].at[blk], refs[a].at[blk], (x, y, 1 - c)
    return _copy_list(n, len(CHIP_FLIPS), make)


def _sibling_halves_build(n):
    def make(refs, a, j):
        x, y, c, _ = _mesh_place()
        return refs[a].at[2 * j + 1 - c], refs[n + a].at[j], (x, y, 1 - c)
    return _copy_list(n, N_SHARD, make)


def _sibling_whole_build(n):
    def make(refs, a, j):
        x, y, c, _ = _mesh_place()
        return refs[a], refs[n + a], (x, y, 1 - c)
    return _copy_list(n, 1, make)


def _add_halves(name, send, land):
    _, r, cdim = send.shape
    tr = _pick(r, max(16, (1 << 20) // (cdim * 2)), 16)

    def body(c_ref, own_ref, got_ref, o_ref):
        o_ref[...] = (own_ref[...].astype(F32) + got_ref[...].astype(F32)).astype(BF)

    grid_spec = pltpu.PrefetchScalarGridSpec(
        num_scalar_prefetch=1, grid=(N_SHARD, r // tr),
        in_specs=[pl.BlockSpec((None, tr, cdim), lambda sh, i, cr: (2 * sh + cr[0], i, 0)),
                  pl.BlockSpec((None, tr, cdim), lambda sh, i, cr: (sh, i, 0))],
        out_specs=pl.BlockSpec((None, tr, cdim), lambda sh, i, cr: (sh, i, 0)))
    core = lax.axis_index("c").astype(jnp.int32).reshape(1)
    return _pcall(body, name=name, grid_spec=grid_spec, out_shape=jax.ShapeDtypeStruct((N_SHARD, r, cdim), BF),
                  compiler_params=_params(("arbitrary", "arbitrary")))(core, send, land)


def _sum_lead(name, arr, out_dtype=F32):
    n, r, cdim = arr.shape
    tr = r
    limit = (4 << 20) // (n * cdim * arr.dtype.itemsize)
    if r > limit:
        tr = _pick(r, max(limit, 16), 16)

    def body(x_ref, o_ref):
        acc = x_ref[0].astype(F32)
        for d in range(1, n):
            acc = acc + x_ref[d].astype(F32)
        o_ref[...] = acc.astype(out_dtype)

    return _pcall(body, name=name, grid=(r // tr,),
                  in_specs=[pl.BlockSpec((n, tr, cdim), lambda i: (0, i, 0))],
                  out_specs=pl.BlockSpec((tr, cdim), lambda i: (i, 0)),
                  out_shape=jax.ShapeDtypeStruct((r, cdim), out_dtype),
                  compiler_params=_params(("arbitrary",)))(arr)


def _rows_call(name, fn, n_rows, tm, rows, consts, mod, outs, acc_w=None, h_tiles=None):
    nt = n_rows // tm
    ht = nt if h_tiles is None else h_tiles
    ng = 1 if mod is None else mod.shape[0]
    n_r, n_c, n_o = len(rows), len(consts), len(outs)
    has_mod = mod is not None

    def body(*refs):
        i = pl.program_id(0)
        first = (i % ht) == 0
        row_refs, const_refs = refs[:n_r], refs[n_r:n_r + n_c]
        p = n_r + n_c
        mod_tile = refs[p][...] if has_mod else None
        p += int(has_mod)
        out_refs = refs[p:p + n_o]
        o, acc = fn([r[...] for r in row_refs], [r[...] for r in const_refs], mod_tile)
        for r, v in zip(out_refs, o):
            r[...] = v.astype(r.dtype)
        if acc_w is not None:
            acc_ref = refs[p + n_o]

            @pl.when(first)
            def _():
                acc_ref[...] = jnp.zeros_like(acc_ref)

            for k, v in acc.items():
                acc_ref[k:k + 1, :] += v

    in_specs = [pl.BlockSpec((tm, r.shape[1]), lambda i: (i, 0)) for r in rows]
    in_specs += [pl.BlockSpec(cst.shape, lambda i, nd=cst.ndim: (0,) * nd) for cst in consts]
    args = list(rows) + list(consts)
    if has_mod:
        in_specs.append(pl.BlockSpec((None,) + mod.shape[1:], lambda i: (i // ht, 0, 0)))
        args.append(mod)
    out_shape = [jax.ShapeDtypeStruct((n_rows, w), dt) for w, dt in outs]
    out_specs = [pl.BlockSpec((tm, w), lambda i: (i, 0)) for w, _ in outs]
    if acc_w is not None:
        out_shape.append(jax.ShapeDtypeStruct((ng, 8, acc_w), F32))
        out_specs.append(pl.BlockSpec((None, 8, acc_w), lambda i: (i // ht, 0, 0)))
    res = _pcall(body, name=name, grid=(nt,), in_specs=in_specs, out_specs=out_specs, out_shape=out_shape,
                 compiler_params=_params(("arbitrary",)))(*args)
    return list(res)


def _rms(s):
    r = lax.rsqrt(jnp.mean(s * s, axis=1, keepdims=True) + RMS_EPS)
    return s * r, r


def _rms_bwd(dn, n, r):
    return r * (dn - n * jnp.mean(dn * n, axis=1, keepdims=True))


def _adaln_fwd(name, s, gains, gain_row, mod, k, tm, h_tiles, after=None):
    def fn(rows, consts, m):
        n, _ = _rms(rows[0])
        y = n * consts[0][gain_row:gain_row + 1, :]
        return [y * (1.0 + m[3 * k + 1:3 * k + 2, :]) + m[3 * k:3 * k + 1, :]], {}

    d = s.shape[1]
    consts = [gains] if after is None else [gains, after]
    return _rows_call(name, fn, s.shape[0], tm, [s], consts, mod, [(d, BF)], h_tiles=h_tiles)[0]


def _adaln_bwd(name, s, du, ds_res, gains, gain_row, mod, k, tm, h_tiles):
    def fn(rows, consts, m):
        sv, duv, res = rows
        gain = consts[0][gain_row:gain_row + 1, :]
        n, r = _rms(sv)
        y = n * gain
        dy = duv * (1.0 + m[3 * k + 1:3 * k + 2, :])
        acc = {0: jnp.sum(duv, axis=0, keepdims=True), 1: jnp.sum(duv * y, axis=0, keepdims=True),
               2: jnp.sum(dy * n, axis=0, keepdims=True)}
        return [_rms_bwd(dy * gain, n, r) + res], acc

    d = s.shape[1]
    return _rows_call(name, fn, s.shape[0], tm, [s, du, ds_res], [gains], mod, [(d, F32)], acc_w=d, h_tiles=h_tiles)


def _resid_bwd(name, ds_out, o, mod, k, cst, tm, h_tiles, after=None):
    def fn(rows, consts, m):
        dsv, ov = rows
        gate = m[3 * k + 2:3 * k + 3, :]
        return [cst * gate * dsv], {0: jnp.sum(cst * ov * dsv, axis=0, keepdims=True)}

    d = o.shape[1]
    consts = [] if after is None else [after]
    return _rows_call(name, fn, o.shape[0], tm, [ds_out, o], consts, mod, [(d, BF)], acc_w=d, h_tiles=h_tiles)


def _mm(name, a, b, mode, tm=256, tn=512, out_dtypes=(F32,), epi=None, epi_args=(), epi_kinds=(), a_pre=None,
        b_lead=None):
    bshape = b.shape if b_lead is None else b.shape[1:]
    if mode == 'nn':
        (m, kd), nd = a.shape, bshape[1]
    elif mode == 'nt':
        (m, kd), nd = a.shape, bshape[0]
    else:
        (kd, m), nd = a.shape, bshape[1]
    tm = _pick(m, tm, 16) if m % tm else tm
    tn = _pick(nd, tn, 128) if nd % tn else tn
    dims = {'nn': NN, 'nt': NT, 'tn': TN}[mode]
    n_e, n_o = len(epi_args), len(out_dtypes)

    def body(*refs):
        i = pl.program_id(1)
        av = refs[0][...]
        if a_pre is not None:
            av = a_pre(av)
        acc = _dot(av, refs[1][...], dims)
        res = (acc,) if epi is None else epi(acc, i, *[r[...] for r in refs[2:2 + n_e]])
        for r, v in zip(refs[2 + n_e:], res):
            r[...] = v.astype(r.dtype)

    if mode == 'nn':
        specs = [pl.BlockSpec((tm, kd), lambda j, i: (i, 0)), pl.BlockSpec((kd, tn), lambda j, i: (0, j))]
    elif mode == 'nt':
        specs = [pl.BlockSpec((tm, kd), lambda j, i: (i, 0)), pl.BlockSpec((tn, kd), lambda j, i: (j, 0))]
    else:
        specs = [pl.BlockSpec((kd, tm), lambda j, i: (0, i)), pl.BlockSpec((kd, tn), lambda j, i: (0, j))]
    if b_lead is not None:
        shape2, at2 = specs[1].block_shape, specs[1].index_map
        specs[1] = pl.BlockSpec((None,) + tuple(shape2), lambda j, i: (b_lead,) + tuple(at2(j, i)))
    for arr, kind in zip(epi_args, epi_kinds):
        if kind == 'mn':
            specs.append(pl.BlockSpec((tm, tn), lambda j, i: (i, j)))
        elif kind == 'n':
            specs.append(pl.BlockSpec((1, tn), lambda j, i: (0, j)))
        elif kind == 'mt':
            specs.append(pl.BlockSpec((tm, arr.shape[1]), lambda j, i: (i, 0)))
        else:
            specs.append(pl.BlockSpec(arr.shape, lambda j, i, nd_=arr.ndim: (0,) * nd_))
    res = _pcall(body, name=name, grid=(nd // tn, m // tm), in_specs=specs,
                 out_specs=[pl.BlockSpec((tm, tn), lambda j, i: (i, j))] * n_o,
                 out_shape=[jax.ShapeDtypeStruct((m, nd), dt) for dt in out_dtypes],
                 compiler_params=_params(("arbitrary", "arbitrary")))(a, b, *epi_args)
    return res[0] if n_o == 1 else list(res)


def _row_gate(mod, k3, i, tm, n_lat):
    g0 = mod[0, k3:k3 + 1, :]
    if mod.shape[0] == 1:
        return g0
    rid = i * tm + lax.broadcasted_iota(jnp.int32, (tm, 1), 0)
    return jnp.where(rid < n_lat, g0, mod[1, k3:k3 + 1, :])


def _ffn_up(name, u, wg, wu, base, tm):
    r, d = u.shape
    nch, _, _, fc = wg.shape

    def body(u_ref, wg_ref, wu_ref, a_ref, b_ref, h_ref):
        uv = u_ref[...]
        a = _dot(uv, wg_ref[...], NN)
        b = _dot(uv, wu_ref[...], NN)
        a_ref[...] = a.astype(BF)
        b_ref[...] = b.astype(BF)
        h_ref[...] = (_silu(a) * b).astype(BF)

    chunk = pl.BlockSpec((None, tm, fc), lambda j, i: (j, i, 0))
    return _pcall(body, name=name, grid=(nch, r // tm),
                  in_specs=[pl.BlockSpec((tm, d), lambda j, i: (i, 0)),
                            pl.BlockSpec((None, None, d, fc), lambda j, i: (j, base, 0, 0)),
                            pl.BlockSpec((None, None, d, fc), lambda j, i: (j, base, 0, 0))],
                  out_specs=[chunk] * 3, out_shape=[jax.ShapeDtypeStruct((nch, r, fc), BF)] * 3,
                  compiler_params=_params(("arbitrary", "arbitrary")))(u, wg, wu)


def _ffn_down(name, hid, wd, wd_blk, s, mod, k, n_lat, tm):
    nch, r, fc = hid.shape
    d = wd.shape[2]

    def body(h_ref, w_ref, s_ref, m_ref, so_ref, o_ref, acc_ref):
        i, j = pl.program_id(0), pl.program_id(1)
        part = _dot(h_ref[...], w_ref[...], NN)

        @pl.when(j == 0)
        def _():
            acc_ref[...] = part

        @pl.when(j > 0)
        def _():
            acc_ref[...] += part

        @pl.when(j == nch - 1)
        def _():
            o = acc_ref[...]
            o_ref[...] = o
            so_ref[...] = s_ref[...] + 0.5 * _row_gate(m_ref[...], 3 * k + 2, i, tm, n_lat) * o

    row = pl.BlockSpec((tm, d), lambda i, j: (i, 0))
    return _pcall(body, name=name, grid=(r // tm, nch),
                  in_specs=[pl.BlockSpec((None, tm, fc), lambda i, j: (j, i, 0)),
                            pl.BlockSpec((None, fc, d), lambda i, j: (j, wd_blk, 0)), row,
                            pl.BlockSpec(mod.shape, lambda i, j: (0, 0, 0))],
                  out_specs=[row, row], out_shape=[jax.ShapeDtypeStruct((r, d), F32)] * 2,
                  scratch_shapes=[pltpu.VMEM((tm, d), F32)],
                  compiler_params=_params(("arbitrary", "arbitrary")))(hid, wd, s, mod)


def _ffn_dhid(name, d_o, wd, wd_blk, a, b, tm):
    r, d = d_o.shape
    nch, _, fc = a.shape

    def body(g_ref, w_ref, a_ref, b_ref, da_ref, db_ref):
        dh = _dot(g_ref[...], w_ref[...], NT)
        av, bv = a_ref[...].astype(F32), b_ref[...].astype(F32)
        da_ref[...] = (dh * bv * _dsilu(av)).astype(BF)
        db_ref[...] = (dh * _silu(av)).astype(BF)

    chunk = pl.BlockSpec((None, tm, fc), lambda j, i: (j, i, 0))
    return _pcall(body, name=name, grid=(nch, r // tm),
                  in_specs=[pl.BlockSpec((tm, d), lambda j, i: (i, 0)),
                            pl.BlockSpec((None, fc, d), lambda j, i: (j, wd_blk, 0)), chunk, chunk],
                  out_specs=[chunk] * 2, out_shape=[jax.ShapeDtypeStruct((nch, r, fc), BF)] * 2,
                  compiler_params=_params(("arbitrary", "arbitrary")))(d_o, wd, a, b)


def _ffn_du(name, da, db, wg, wu, base, tm):
    nch, r, fc = da.shape
    d = wg.shape[2]

    def body(da_ref, db_ref, wg_ref, wu_ref, o_ref, acc_ref):
        j = pl.program_id(1)
        part = _dot(da_ref[...], wg_ref[...], NT) + _dot(db_ref[...], wu_ref[...], NT)

        @pl.when(j == 0)
        def _():
            acc_ref[...] = part

        @pl.when(j > 0)
        def _():
            acc_ref[...] += part

        @pl.when(j == nch - 1)
        def _():
            o_ref[...] = acc_ref[...]

    chunk = pl.BlockSpec((None, tm, fc), lambda i, j: (j, i, 0))
    return _pcall(body, name=name, grid=(r // tm, nch),
                  in_specs=[chunk, chunk, pl.BlockSpec((None, None, d, fc), lambda i, j: (j, base, 0, 0)),
                            pl.BlockSpec((None, None, d, fc), lambda i, j: (j, base, 0, 0))],
                  out_specs=pl.BlockSpec((tm, d), lambda i, j: (i, 0)),
                  out_shape=jax.ShapeDtypeStruct((r, d), F32), scratch_shapes=[pltpu.VMEM((tm, d), F32)],
                  compiler_params=_params(("arbitrary", "arbitrary")))(da, db, wg, wu)


def _ffn_dw_in(name, u, dz, tmm, grads, idx):
    r, d = u.shape
    nch, _, fc = dz.shape
    nb = d // tmm

    def body(u_ref, z_ref, g_ref, o_ref):
        o_ref[...] = _dot(u_ref[...], z_ref[...], TN).astype(o_ref.dtype)

    return _pcall(body, name=name, grid=(nch, nb),
                  in_specs=[pl.BlockSpec((r, tmm), lambda j, mi: (0, mi)),
                            pl.BlockSpec((None, r, fc), lambda j, mi: (j, 0, 0)),
                            pl.BlockSpec(memory_space=pl.ANY)],
                  out_specs=pl.BlockSpec((None, tmm, fc), lambda j, mi: (j, idx * nb + mi, 0)),
                  out_shape=jax.ShapeDtypeStruct(grads.shape, grads.dtype), input_output_aliases={2: 0},
                  compiler_params=_params(("arbitrary", "arbitrary")))(u, dz, grads)


def _ffn_dw_down(name, hid, d_o, tn, grads, idx):
    nch, r, fc = hid.shape
    d = d_o.shape[1]

    def body(h_ref, g_ref, acc_ref, o_ref):
        o_ref[...] = _dot(h_ref[...], g_ref[...], TN).astype(o_ref.dtype)

    return _pcall(body, name=name, grid=(nch, d // tn),
                  in_specs=[pl.BlockSpec((None, r, fc), lambda j, ni: (j, 0, 0)),
                            pl.BlockSpec((r, tn), lambda j, ni: (0, ni)),
                            pl.BlockSpec(memory_space=pl.ANY)],
                  out_specs=pl.BlockSpec((None, fc, tn), lambda j, ni: (j, idx, ni)),
                  out_shape=jax.ShapeDtypeStruct(grads.shape, grads.dtype), input_output_aliases={2: 0},
                  compiler_params=_params(("arbitrary", "arbitrary")))(hid, d_o, grads)


def _partner(x):
    n = x.shape[1]
    lane = lax.broadcasted_iota(jnp.int32, x.shape, 1)
    return jnp.where((lane & 15) < 8, pltpu.roll(x, n - 8, 1), pltpu.roll(x, 8, 1))


def _rope(x, ct, st):
    reps = x.shape[1] // ct.shape[1]
    if reps > 1:
        ct, st = jnp.tile(ct, (1, reps)), jnp.tile(st, (1, reps))
    return x * ct + _partner(x) * st


def _rope_t(dy, ct, st):
    reps = dy.shape[1] // ct.shape[1]
    if reps > 1:
        ct, st = jnp.tile(ct, (1, reps)), jnp.tile(st, (1, reps))
    return dy * ct + _partner(dy * st)


def _rope_tables(t_len, g_len, lane0):
    half = QK_ROPE // 4
    pos = jnp.arange(t_len)
    row = (pos // GRID_W).astype(F32)
    col = (pos % GRID_W).astype(F32)
    freqs = jnp.power(ROPE_THETA, -jnp.arange(0, QK_ROPE // 2, 2, dtype=F32) / (QK_ROPE // 2))
    ang_r, ang_c = row[:, None] * freqs, col[:, None] * freqs
    cs = jnp.concatenate([jnp.cos(ang_r)] * 2 + [jnp.cos(ang_c)] * 2, axis=1)
    sn = jnp.concatenate([-jnp.sin(ang_r), jnp.sin(ang_r), -jnp.sin(ang_c), jnp.sin(ang_c)], axis=1)
    assert cs.shape[1] == 4 * half == QK_ROPE
    def place(tab, fill):
        rest = HEAD_PAD - lane0 - QK_ROPE
        rows = jnp.concatenate([jnp.full((t_len, lane0), fill, F32), tab, jnp.full((t_len, rest), fill, F32)], axis=1)
        return jnp.concatenate([rows, jnp.full((g_len, HEAD_PAD), fill, F32)], axis=0)

    return place(cs, 1.0), place(sn, 0.0)


def _attn_fwd(name, q, kp, vp, n_q, q_off, n_k, k_blk, heads, tq, scale):
    qb = q_off // tq

    def body(q_ref, k_ref, v_ref, o_ref, l_ref):
        s = _dot(q_ref[...], k_ref[...], NT) * scale
        m = jnp.max(s, axis=1, keepdims=True)
        p = jnp.exp(s - m)
        l = jnp.sum(p, axis=1, keepdims=True)
        o_ref[...] = (_dot(p, v_ref[...], NN) / l).astype(BF)
        l_ref[...] = jnp.broadcast_to(m + jnp.log(l), l_ref.shape)

    hw = heads * HEAD_PAD
    blk = pl.BlockSpec((tq, HEAD_PAD), lambda h, i: (i, h))
    kv = pl.BlockSpec((n_k, HEAD_PAD), lambda h, i: (k_blk, h))
    return _pcall(body, name=name, grid=(heads, n_q // tq),
                  in_specs=[pl.BlockSpec((tq, HEAD_PAD), lambda h, i: (i + qb, h)), kv, kv],
                  out_specs=[blk, blk],
                  out_shape=[jax.ShapeDtypeStruct((n_q, hw), BF), jax.ShapeDtypeStruct((n_q, hw), F32)],
                  compiler_params=_params(("arbitrary", "arbitrary")))(q, kp, vp)


def _attn_bwd(name, q, kp, vp, cat, dcat, lse, n_q, q_off, n_k, k_blk, heads, tq, scale, col_blk, onto=None):
    qb = q_off // tq

    def body(q_ref, k_ref, v_ref, o_ref, do_ref, l_ref, *rest):
        dq_ref, dk_ref, dv_ref = rest[-3:]
        i = pl.program_id(1)
        qv, kv_, vv = q_ref[...], k_ref[...], v_ref[...]
        dov = do_ref[...]
        s = _dot(qv, kv_, NT) * scale
        p = jnp.exp(s - l_ref[...][:, 0:1])
        dp = _dot(dov, vv, NT)
        delta = jnp.sum(dov * o_ref[...].astype(F32), axis=1, keepdims=True)
        ds = (p * (dp - delta) * scale).astype(BF)
        dq_ref[...] = _dot(ds, kv_, NN)
        dk = _dot(ds, qv, TN)
        dv = _dot(p, dov, TN)

        @pl.when(i == 0)
        def _():
            if onto is None:
                dk_ref[...] = dk
                dv_ref[...] = dv
            else:
                dk_ref[...] = rest[0][...] + dk
                dv_ref[...] = rest[1][...] + dv

        @pl.when(i > 0)
        def _():
            dk_ref[...] += dk
            dv_ref[...] += dv

    hw = heads * HEAD_PAD
    qspec = pl.BlockSpec((tq, HEAD_PAD), lambda h, i: (i + qb, h))
    cspec = pl.BlockSpec((tq, HEAD_PAD), lambda h, i: (i + qb, col_blk + h))
    kv = pl.BlockSpec((n_k, HEAD_PAD), lambda h, i: (k_blk, h))
    blk = pl.BlockSpec((tq, HEAD_PAD), lambda h, i: (i, h))
    if onto is None:
        acc = pl.BlockSpec((n_k, HEAD_PAD), lambda h, i: (0, h))
        return _pcall(body, name=name, grid=(heads, n_q // tq),
                      in_specs=[qspec, kv, kv, cspec, cspec, blk], out_specs=[blk, acc, acc],
                      out_shape=[jax.ShapeDtypeStruct((n_q, hw), F32), jax.ShapeDtypeStruct((n_k, hw), F32),
                                 jax.ShapeDtypeStruct((n_k, hw), F32)],
                      compiler_params=_params(("arbitrary", "arbitrary")))(q, kp, vp, cat, dcat, lse)
    return _pcall(body, name=name, grid=(heads, n_q // tq),
                  in_specs=[qspec, kv, kv, cspec, cspec, blk, kv, kv], out_specs=[blk, kv, kv],
                  out_shape=[jax.ShapeDtypeStruct((n_q, hw), F32)] + [jax.ShapeDtypeStruct(t.shape, F32) for t in onto],
                  input_output_aliases={6: 1, 7: 2},
                  compiler_params=_params(("arbitrary", "arbitrary")))(q, kp, vp, cat, dcat, lse, *onto)


def _shift(x, k):
    return pltpu.roll(x, k % x.shape[0], 0)


def _window_sum(v, w, mirrored):
    n, gd = v.shape
    pad = jnp.zeros((POOL_PAD, gd), F32)
    e = jnp.concatenate([pad, v, pad], axis=0)
    acc = e + _shift(e, -1 if mirrored else 1)
    step = 1
    while 2 * step < w:
        acc = _shift(acc, step) + _shift(acc, -step)
        step *= 2
    return acc[POOL_PAD:POOL_PAD + n]


def _window_count(n, w):
    t = lax.broadcasted_iota(jnp.int32, (n, 1), 0)
    lo = jnp.maximum(t - w // 2, 0)
    hi = jnp.minimum(t + (w - w // 2 - 1), n - 1)
    return (hi - lo + 1).astype(F32)


def _pool_fwd(name, u, pool_w, scale):
    n, pd = u.shape
    ng = len(POOL_WINDOWS)
    gd = pd // ng

    def body(u_ref, w_ref, s_ref, y_ref):
        for g, w in enumerate(POOL_WINDOWS):
            sl = slice(g * gd, (g + 1) * gd)
            ug = u_ref[:, sl]
            p = _window_sum(ug, w, False) / _window_count(n, w) - ug
            y_ref[:, sl] = (_dot(p, w_ref[g], NN) * s_ref[:, sl]).astype(BF)

    return _pcall(body, name=name, out_shape=jax.ShapeDtypeStruct((n, pd), BF),
                  compiler_params=_params())(u, pool_w, scale)


def _pool_bwd(name, u, dcat, pool_w, scale, row_off):
    n, pd = u.shape
    ng = len(POOL_WINDOWS)
    gd = pd // ng

    def body(u_ref, dy_ref, w_ref, s_ref, du_ref, dw_ref, ds_ref):
        ds_ref[...] = jnp.zeros_like(ds_ref)
        for g, w in enumerate(POOL_WINDOWS):
            sl = slice(g * gd, (g + 1) * gd)
            ug, dy, wg = u_ref[:, sl], dy_ref[:, sl], w_ref[g]
            cnt = _window_count(n, w)
            p = _window_sum(ug, w, False) / cnt - ug
            ds_ref[0:1, sl] = jnp.sum(dy * _dot(p, wg, NN), axis=0, keepdims=True)
            dys = dy * s_ref[:, sl]
            dw_ref[g] = _dot(p, dys, TN)
            dp = _dot(dys, wg, NT)
            du_ref[:, sl] = (_window_sum(dp / cnt, w, True) - dp).astype(BF)

    rb = row_off // n
    return _pcall(body, name=name, grid=(1,),
                  in_specs=[pl.BlockSpec((n, pd), lambda i: (0, 0)), pl.BlockSpec((n, pd), lambda i: (rb, 0)),
                            pl.BlockSpec(pool_w.shape, lambda i: (0, 0, 0)), pl.BlockSpec(scale.shape, lambda i: (0, 0))],
                  out_specs=[pl.BlockSpec((n, pd), lambda i: (0, 0)), pl.BlockSpec((ng, gd, gd), lambda i: (0, 0, 0)),
                             pl.BlockSpec((8, pd), lambda i: (0, 0))],
                  out_shape=[jax.ShapeDtypeStruct((n, pd), BF), jax.ShapeDtypeStruct((ng, gd, gd), F32),
                             jax.ShapeDtypeStruct((8, pd), F32)],
                  compiler_params=_params(("arbitrary",)))(u, dcat, pool_w, scale)


def _edge_shift(z, k):
    n = z.shape[0]
    t = lax.broadcasted_iota(jnp.int32, (n, 1), 0)
    keep = (t >= k) if k > 0 else (t < n + k)
    return jnp.where(keep, pltpu.roll(z, k % n, 0), 0.0)


def _conv_fwd(name, p3, cw, tc):
    n, cd = p3.shape[0], p3.shape[1] // 3
    nb = cd // tc

    def body(b_ref, c_ref, v_ref, w_ref, y_ref):
        z = c_ref[...] * v_ref[...]
        w = w_ref[...]
        zc = w[0:1] * _edge_shift(z, 1) + w[1:2] * z + w[2:3] * _edge_shift(z, -1)
        y_ref[...] = (b_ref[...] * zc).astype(BF)

    return _pcall(body, name=name, grid=(nb,),
                  in_specs=[pl.BlockSpec((n, tc), lambda j: (0, j)), pl.BlockSpec((n, tc), lambda j: (0, nb + j)),
                            pl.BlockSpec((n, tc), lambda j: (0, 2 * nb + j)), pl.BlockSpec((3, tc), lambda j: (0, j))],
                  out_specs=pl.BlockSpec((n, tc), lambda j: (0, j)), out_shape=jax.ShapeDtypeStruct((n, cd), BF),
                  compiler_params=_params(("arbitrary",)))(p3, p3, p3, cw)


def _conv_bwd(name, p3, cw, dy, tc):
    n, cd = dy.shape
    nb = cd // tc

    def body(b_ref, c_ref, v_ref, w_ref, dy_ref, dp_ref, dw_ref):
        cv, vv, w, dyv = c_ref[...], v_ref[...], w_ref[...], dy_ref[...]
        z = cv * vv
        zl, zr = _edge_shift(z, 1), _edge_shift(z, -1)
        zc = w[0:1] * zl + w[1:2] * z + w[2:3] * zr
        dzc = dyv * b_ref[...]
        dz = w[0:1] * _edge_shift(dzc, -1) + w[1:2] * dzc + w[2:3] * _edge_shift(dzc, 1)
        dp_ref[0] = (dyv * zc).astype(BF)
        dp_ref[1] = (dz * vv).astype(BF)
        dp_ref[2] = (dz * cv).astype(BF)
        dw_ref[...] = jnp.zeros_like(dw_ref)
        dw_ref[0:1, :] = jnp.sum(dzc * zl, axis=0, keepdims=True)
        dw_ref[1:2, :] = jnp.sum(dzc * z, axis=0, keepdims=True)
        dw_ref[2:3, :] = jnp.sum(dzc * zr, axis=0, keepdims=True)

    col = pl.BlockSpec((n, tc), lambda j: (0, j))
    return _pcall(body, name=name, grid=(nb,),
                  in_specs=[col, pl.BlockSpec((n, tc), lambda j: (0, nb + j)),
                            pl.BlockSpec((n, tc), lambda j: (0, 2 * nb + j)), pl.BlockSpec((3, tc), lambda j: (0, j)), col],
                  out_specs=[pl.BlockSpec((3, n, tc), lambda j: (0, 0, j)), pl.BlockSpec((8, tc), lambda j: (0, j))],
                  out_shape=[jax.ShapeDtypeStruct((3, n, cd), BF), jax.ShapeDtypeStruct((8, cd), F32)],
                  compiler_params=_params(("arbitrary",)))(p3, p3, p3, cw, dy)


def _conv_din(name, dp3, w_in, tm):
    _, n, cd = dp3.shape
    d = w_in.shape[0]

    def body(a_ref, w_ref, o_ref, acc_ref):
        j = pl.program_id(1)
        part = _dot(a_ref[...], w_ref[...], NT)

        @pl.when(j == 0)
        def _():
            acc_ref[...] = part

        @pl.when(j > 0)
        def _():
            acc_ref[...] += part

        @pl.when(j == 2)
        def _():
            o_ref[...] = acc_ref[...]

    return _pcall(body, name=name, grid=(n // tm, 3),
                  in_specs=[pl.BlockSpec((None, tm, cd), lambda i, j: (j, i, 0)),
                            pl.BlockSpec((d, cd), lambda i, j: (0, j))],
                  out_specs=pl.BlockSpec((tm, d), lambda i, j: (i, 0)), out_shape=jax.ShapeDtypeStruct((n, d), F32),
                  scratch_shapes=[pltpu.VMEM((tm, d), F32)],
                  compiler_params=_params(("arbitrary", "arbitrary")))(dp3, w_in)


def _conv_dw_in(name, u, dp3, tmm, tn):
    n, d = u.shape
    cd = dp3.shape[2]
    nb = cd // tn

    def body(u_ref, z_ref, o_ref):
        o_ref[...] = _dot(u_ref[...], z_ref[...], TN)

    return _pcall(body, name=name, grid=(3 * nb, d // tmm),
                  in_specs=[pl.BlockSpec((n, tmm), lambda j, mi: (0, mi)),
                            pl.BlockSpec((None, n, tn), lambda j, mi: (j // nb, 0, j % nb))],
                  out_specs=pl.BlockSpec((tmm, tn), lambda j, mi: (mi, j)),
                  out_shape=jax.ShapeDtypeStruct((d, 3 * cd), F32),
                  compiler_params=_params(("arbitrary", "arbitrary")))(u, dp3)


def _loss_head(name, h, target, gain, tm):
    d = h.shape[1]

    def fn(rows, consts, m):
        hv, tv = rows
        g = consts[0][0:1, :]
        n, r = _rms(hv)
        err = n * g - tv
        dy = err / d
        loss = 0.5 * jnp.sum(err * err) / d
        acc = {0: jnp.sum(dy * n, axis=0, keepdims=True), 1: jnp.full((1, d), loss, F32)}
        return [_rms_bwd(dy * g, n, r)], acc

    return _rows_call(name, fn, h.shape[0], tm, [h, target], [gain], None, [(d, F32)], acc_w=d)


def _adamw(name, w, g, m, v, after=None):
    shape = w.shape
    if w.ndim == 1:
        shape2 = (1,) + shape
        res = _adamw(name, *[t.reshape(shape2) for t in (w, g, m, v)], after=after)
        return [t.reshape(shape) for t in res]
    if shape[-1] % 128 and shape[-2] % 128 == 0:
        res = _adamw(name, *[jnp.swapaxes(t, -1, -2) for t in (w, g, m, v)], after=after)
        return [jnp.swapaxes(t, -1, -2) for t in res]
    lead, (r, cdim) = shape[:-2], shape[-2:]
    tr = r
    if r * cdim * 4 > (3 << 19):
        tr = _pick(r, max(8, (3 << 19) // (cdim * 4)), 8)
    c1 = 1.0 / (1.0 - ADAM_B1 ** ADAM_STEP)
    c2 = 1.0 / (1.0 - ADAM_B2 ** ADAM_STEP)
    nl = len(lead)

    def body(w_ref, g_ref, m_ref, v_ref, *rest):
        d_ref, nm_ref, nv_ref = rest[-3:]
        gv = g_ref[...]
        nm = ADAM_B1 * m_ref[...] + (1.0 - ADAM_B1) * gv
        nv = ADAM_B2 * v_ref[...] + (1.0 - ADAM_B2) * (gv * gv)
        nm_ref[...] = nm
        nv_ref[...] = nv
        d_ref[...] = -ADAM_LR * ((nm * c1) / (jnp.sqrt(nv * c2) + ADAM_EPS) + ADAM_WD * w_ref[...])

    spec = pl.BlockSpec((None,) * nl + (tr, cdim), lambda *idx: idx + (0,))
    extra = [] if after is None else [after]
    res = _pcall(body, name=name, grid=lead + (r // tr,),
                 in_specs=[spec] * 4 + [pl.BlockSpec(memory_space=pl.ANY)] * len(extra), out_specs=[spec] * 3,
                 out_shape=[jax.ShapeDtypeStruct(shape, F32)] * 3,
                 compiler_params=_params(("arbitrary",) * (nl + 1)))(w, g, m, v, *extra)
    return list(res)


def _adamw_piece(name, w, g_piece, m, v, at, outs):
    shape = w.shape
    nl = len(at)
    r, cdim = shape[-2:]
    assert shape[nl:] == g_piece.shape and len(shape) == nl + 2
    tr = _pick(r, max(8, (3 << 19) // (cdim * 4)), 8) if r * cdim * 4 > (3 << 19) else r
    c1 = 1.0 / (1.0 - ADAM_B1 ** ADAM_STEP)
    c2 = 1.0 / (1.0 - ADAM_B2 ** ADAM_STEP)
    if outs is None:
        outs = [lax.empty(shape, F32) for _ in range(4)]

    def body(w_ref, g_ref, m_ref, v_ref, *rest):
        go_ref, d_ref, nm_ref, nv_ref = rest[-4:]
        gv = g_ref[...]
        nm = ADAM_B1 * m_ref[...] + (1.0 - ADAM_B1) * gv
        nv = ADAM_B2 * v_ref[...] + (1.0 - ADAM_B2) * (gv * gv)
        go_ref[...] = gv
        nm_ref[...] = nm
        nv_ref[...] = nv
        d_ref[...] = -ADAM_LR * ((nm * c1) / (jnp.sqrt(nv * c2) + ADAM_EPS) + ADAM_WD * w_ref[...])

    full = pl.BlockSpec((None,) * nl + (tr, cdim), lambda i: tuple(at) + (i, 0))
    res = _pcall(body, name=name, grid=(r // tr,),
                 in_specs=[full, pl.BlockSpec((tr, cdim), lambda i: (i, 0)), full, full]
                 + [pl.BlockSpec(memory_space=pl.ANY)] * 4,
                 out_specs=[full] * 4, out_shape=[jax.ShapeDtypeStruct(shape, F32)] * 4,
                 input_output_aliases={4 + j: j for j in range(4)},
                 compiler_params=_params(("arbitrary",)))(w, g_piece, m, v, *outs)
    return list(res)


def _ffn_half_fwd(tag, s, gains, mod, k, wts, n_lat, tm, h_tiles, tm_big, after=None):
    wg, wu, wd, idx = wts
    u = _adaln_fwd(f"adaln_{tag}", s, gains, k, mod, k, tm, h_tiles, after)
    a, b, hid = _ffn_up(f"ffn_up_{tag}", u, wg, wu, idx, tm_big)
    s_out, o = _ffn_down(f"ffn_down_{tag}", hid, wd, idx, s, mod, k, n_lat, tm_big)
    return s_out, (s, u, a, b, hid, o)


def _ffn_half_bwd(tag, ds_out, saved, gains, mod, k, wts, big_grads, tm, h_tiles, tm_big, after=None):
    wg, wu, wd, idx = wts
    g_gate, g_up, g_down = big_grads
    s, u, a, b, hid, o = saved
    d_o, acc_g = _resid_bwd(f"resid_bwd_{tag}", ds_out, o, mod, k, 0.5, tm, h_tiles, after)
    da, db = _ffn_dhid(f"ffn_dhid_{tag}", d_o, wd, idx, a, b, tm_big)
    du = _ffn_du(f"ffn_du_{tag}", da, db, wg, wu, idx, _pick(da.shape[1], 1152, 128))
    d = u.shape[1]
    g_gate = _ffn_dw_in(f"ffn_dwg_{tag}", u, da, _pick(d, MM_ROWS), g_gate, idx)
    g_up = _ffn_dw_in(f"ffn_dwu_{tag}", u, db, _pick(d, MM_ROWS), g_up, idx)
    g_down = _ffn_dw_down(f"ffn_dwd_{tag}", hid, d_o, _pick(d, 512), g_down, idx)
    ds, acc_n = _adaln_bwd(f"adaln_bwd_{tag}", s, du, ds_out, gains, k, mod, k, tm, h_tiles)
    return ds, (g_gate, g_up, g_down), (acc_n[:, 0], acc_n[:, 1], acc_g[:, 0]), jnp.sum(acc_n[:, 2], axis=0)


def kernel(x, c, ctx, c_ctx, norm_g, w_mod, b_mod, ffn_w_gate, ffn_w_up, ffn_w_down, ab_w_in, pool_w, pool_scale, q_norm_g, w_uq, kv_norm_g, w_ukv, ab_w_out, conv_w_in, conv_w, conv_w_out, final_norm_g, loss_target, m_c_ctx, m_norm_g, m_w_mod, m_b_mod, m_ffn_w_gate, m_ffn_w_up, m_ffn_w_down, m_ab_w_in, m_pool_w, m_pool_scale, m_q_norm_g, m_w_uq, m_kv_norm_g, m_w_ukv, m_ab_w_out, m_conv_w_in, m_conv_w, m_conv_w_out, m_final_norm_g, v_c_ctx, v_norm_g, v_w_mod, v_b_mod, v_ffn_w_gate, v_ffn_w_up, v_ffn_w_down, v_ab_w_in, v_pool_w, v_pool_scale, v_q_norm_g, v_w_uq, v_kv_norm_g, v_w_ukv, v_ab_w_out, v_conv_w_in, v_conv_w, v_conv_w_out, v_final_norm_g):
    weights = dict(c_ctx=c_ctx, norm_g=norm_g, w_mod=w_mod, b_mod=b_mod, ffn_w_gate=ffn_w_gate, ffn_w_up=ffn_w_up,
                   ffn_w_down=ffn_w_down, ab_w_in=ab_w_in, pool_w=pool_w, pool_scale=pool_scale, q_norm_g=q_norm_g,
                   w_uq=w_uq, kv_norm_g=kv_norm_g, w_ukv=w_ukv, ab_w_out=ab_w_out, conv_w_in=conv_w_in, conv_w=conv_w,
                   conv_w_out=conv_w_out, final_norm_g=final_norm_g)
    mom_m = dict(c_ctx=m_c_ctx, norm_g=m_norm_g, w_mod=m_w_mod, b_mod=m_b_mod, ffn_w_gate=m_ffn_w_gate,
                 ffn_w_up=m_ffn_w_up, ffn_w_down=m_ffn_w_down, ab_w_in=m_ab_w_in, pool_w=m_pool_w,
                 pool_scale=m_pool_scale, q_norm_g=m_q_norm_g, w_uq=m_w_uq, kv_norm_g=m_kv_norm_g, w_ukv=m_w_ukv,
                 ab_w_out=m_ab_w_out, conv_w_in=m_conv_w_in, conv_w=m_conv_w, conv_w_out=m_conv_w_out,
                 final_norm_g=m_final_norm_g)
    mom_v = dict(c_ctx=v_c_ctx, norm_g=v_norm_g, w_mod=v_w_mod, b_mod=v_b_mod, ffn_w_gate=v_ffn_w_gate,
                 ffn_w_up=v_ffn_w_up, ffn_w_down=v_ffn_w_down, ab_w_in=v_ab_w_in, pool_w=v_pool_w,
                 pool_scale=v_pool_scale, q_norm_g=v_q_norm_g, w_uq=v_w_uq, kv_norm_g=v_kv_norm_g, w_ukv=v_w_ukv,
                 ab_w_out=v_ab_w_out, conv_w_in=v_conv_w_in, conv_w=v_conv_w, conv_w_out=v_conv_w_out,
                 final_norm_g=v_final_norm_g)

    t_len, d = x.shape[1], x.shape[2]
    g_len = ctx.shape[1]
    r_len = t_len + g_len
    fc = ffn_w_gate.shape[3]
    heads = d // 128
    pool_dim = d // 2
    q_rank, kv_rank = q_norm_g.shape[1], kv_norm_g.shape[1]
    hw = heads * HEAD_PAD
    attn_scale = 1.0 / math.sqrt(QK_NOPE + QK_ROPE)
    kvr_w = kv_rank + HEAD_PAD
    in_w = pool_dim + q_rank + kvr_w
    tm = 256 if g_len % 256 == 0 else g_len
    assert t_len % tm == 0 and g_len % tm == 0 and t_len % g_len == 0 and pool_dim % 128 == 0
    h_tiles = t_len // tm
    tm_l0 = _pick(r_len, 768, tm)
    tm_l1 = _pick(t_len, 1024, tm)

    xi, yi, ci = lax.axis_index("x"), lax.axis_index("y"), lax.axis_index("c")
    me = 4 * xi + 2 * yi + ci
    shard = 2 * xi + yi

    def halves(w):
        return w.astype(BF).reshape(2, -1, w.shape[-1])

    ffn_names = ["ffn_w_gate", "ffn_w_up", "ffn_w_down"]
    mixer_names = [["ab_w_in", "w_uq", "w_ukv", "ab_w_out"], ["conv_w_in", "conv_w_out"]]
    big_names = ffn_names + mixer_names[0] + mixer_names[1]

    def stage_names(k):
        return mixer_names[k // 3] if k % 3 == 1 else ffn_names

    def stage_halves(k):
        l, f = k // 3, (k % 3) // 2
        if k % 3 == 1:
            return [halves(weights[nm]) for nm in mixer_names[l]]
        return [halves(weights[nm][l, f]) for nm in ffn_names]

    def gather_start(k, dep):
        own = lax.optimization_barrier((tuple(stage_halves(k)), dep))[0]
        n = len(own)
        return _split_start(f"gather_start_s{k}", list(own) + _landing(N_DEV, own), n * len(CHIP_FLIPS),
                            _chips_gather_build(n))

    gather0 = gather_start(0, c)

    small = jnp.concatenate([norm_g.reshape(6, -1), conv_w[0]], axis=0)
    small = jnp.pad(small, ((0, 7), (0, 0)))
    c_row = jnp.pad(c, ((0, 7), (0, 0))) + gather0['token'][0, 0]
    small_all, c_all = _gather_all("gather_small", [small, c_row])
    small_full = small_all[::2].transpose(1, 0, 2).reshape(16, d)
    gains = [jnp.pad(small_full[3 * l:3 * l + 3], ((0, 5), (0, 0))) for l in range(2)]
    conv_w_full = small_full[6:9]
    c16 = jnp.concatenate([c_all[:, 0], c_ctx[None], jnp.zeros((7, d), F32)], axis=0)

    n_col = w_mod.shape[2]
    b_sh = lax.dynamic_slice_in_dim(b_mod, shard * n_col, n_col, axis=1)
    m_sh = [_mm(f"mod_fwd_{l}", c16, w_mod, 'nn', tm=16, tn=768, a_pre=_silu, b_lead=l,
                epi=lambda acc, i, bv: (acc + bv,), epi_args=(b_sh[l:l + 1],), epi_kinds=('n',)) for l in range(2)]
    m_all = _gather_all("gather_mod", [jnp.concatenate(m_sh, axis=0)])[0]
    m_full = m_all[::2].reshape(N_SHARD, 2, 16, n_col).transpose(1, 2, 0, 3).reshape(2, 16, N_MOD * d)
    mod_h = [jnp.pad(lax.dynamic_index_in_dim(m_full[l], me, 0, keepdims=False).reshape(N_MOD, d), ((0, 7), (0, 0)))
             for l in range(2)]
    mod_g0 = jnp.pad(m_full[0, 8].reshape(N_MOD, d), ((0, 7), (0, 0)))
    mods = [jnp.stack([mod_h[0], mod_g0]), mod_h[1][None]]

    def stage_weights(k, handle, after):
        bufs = _split_wait(f"gather_wait_s{k}", handle, after)
        n = len(bufs) // 2
        own = bufs[:n]
        fwd = _split_start(f"forward_start_s{k}", bufs[n:], n * len(CHIP_FLIPS), _sibling_forward_build(n))
        nxt = gather_start(k + 1, fwd['token']) if k + 1 < 6 else None
        landed = _split_wait(f"forward_wait_s{k}", fwd, fwd['token'])
        full = [lax.dynamic_update_slice_in_dim(z, a, 2 * shard, 0) for z, a in zip(landed, own)]
        gw = {nm: g.reshape(N_SHARD, 2 * g.shape[1], g.shape[2]) for nm, g in zip(stage_names(k), full)}
        return gw, nxt, (fwd['token'] if nxt is None else nxt['token'])

    def ffn_weights(gw):
        return gw["ffn_w_gate"].reshape(N_SHARD, 1, d, fc), gw["ffn_w_up"].reshape(N_SHARD, 1, d, fc), \
            gw["ffn_w_down"], 0

    gw_s0, gather1, tok0 = stage_weights(0, gather0, m_all)
    ffn_w = [[ffn_weights(gw_s0), None], [None, None]]

    s0 = jnp.concatenate([x[0], ctx[0]], axis=0)
    s1, sav_f00 = _ffn_half_fwd("l0a", s0, gains[0], mods[0], 0, ffn_w[0][0], t_len, tm, h_tiles, tm_l0, tok0)

    gw_s1, gather2, tok1 = stage_weights(1, gather1, s1)
    w_out_full = gw_s1["ab_w_out"].reshape(-1, d)
    w_uq_full = gw_s1["w_uq"].reshape(q_rank, heads * (QK_NOPE + QK_ROPE))
    w_ukv_full = gw_s1["w_ukv"].transpose(1, 0, 2).reshape(kv_rank, heads * (QK_NOPE + V_HEAD))
    w_in_full = gw_s1["ab_w_in"].transpose(1, 0, 2).reshape(d, -1)

    wq_p = jnp.pad(w_uq_full.reshape(q_rank, heads, QK_NOPE + QK_ROPE),
                   ((0, 0), (0, 0), (0, HEAD_PAD - QK_NOPE - QK_ROPE))).reshape(q_rank, hw)
    ukv3 = w_ukv_full.reshape(kv_rank, heads, QK_NOPE + V_HEAD)
    wk_top = jnp.pad(ukv3[..., :QK_NOPE], ((0, 0), (0, 0), (0, HEAD_PAD - QK_NOPE))).reshape(kv_rank, hw)
    wv_top = jnp.pad(ukv3[..., QK_NOPE:], ((0, 0), (0, 0), (0, HEAD_PAD - V_HEAD))).reshape(kv_rank, hw)
    src_row = lax.broadcasted_iota(jnp.int32, (HEAD_PAD, hw), 0)
    dst_lane = lax.broadcasted_iota(jnp.int32, (HEAD_PAD, hw), 1) % HEAD_PAD
    spread = ((src_row < QK_ROPE) & (dst_lane == src_row + QK_NOPE)).astype(BF)
    wk_ext = jnp.concatenate([wk_top, spread], axis=0)
    wv_ext = jnp.concatenate([wv_top, jnp.zeros((HEAD_PAD, hw), BF)], axis=0)
    w_in_pool = w_in_full[:, :pool_dim]
    w_in_q = w_in_full[:, pool_dim:pool_dim + q_rank]
    w_in_kvr = jnp.pad(w_in_full[:, pool_dim + q_rank:], ((0, 0), (0, HEAD_PAD - QK_ROPE)))
    w_out_attn = jnp.pad(w_out_full[pool_dim:].reshape(heads, V_HEAD, d),
                         ((0, 0), (0, HEAD_PAD - V_HEAD), (0, 0))).reshape(hw, d)
    w_out_p = jnp.concatenate([w_out_full[:pool_dim], w_out_attn], axis=0)

    u_mix = _adaln_fwd("adaln_l0m", s1, gains[0], 1, mods[0], 1, tm, h_tiles, tok1)
    p_pool = _mm("in_pool", u_mix, w_in_pool, 'nn', tm=tm_l0, tn=pool_dim)
    p_q = _mm("in_q", u_mix, w_in_q, 'nn', tm=tm_l0, tn=q_rank)
    p_kvr = _mm("in_kvr", u_mix, w_in_kvr, 'nn', tm=tm_l0, tn=kvr_w)
    qg = jnp.pad(q_norm_g, ((0, 7), (0, 0)))
    kvg = jnp.pad(kv_norm_g, ((0, 7), (0, 0)))
    tq_c, tq_s = _rope_tables(t_len, g_len, QK_NOPE)
    tk_c, tk_s = _rope_tables(t_len, g_len, 0)

    def qn_fn(rows, consts, m):
        n, _ = _rms(rows[0])
        return [n * consts[0][0:1, :]], {}

    qn = _rows_call("q_norm", qn_fn, r_len, tm, [p_q], [qg], None, [(q_rank, BF)])[0]
    q_r = _mm("q_up", qn, wq_p, 'nn', tm=tm_l0, tn=hw, out_dtypes=(BF,),
              epi=lambda acc, i, ct, st: (_rope(acc, ct, st),), epi_args=(tq_c, tq_s), epi_kinds=('mt', 'mt'))

    def kvn_fn(rows, consts, m):
        pv, ct, st = rows
        n, _ = _rms(pv[:, :kv_rank])
        return [jnp.concatenate([n * consts[0][0:1, :], _rope(pv[:, kv_rank:], ct, st)], axis=1)], {}

    kvn = _rows_call("kv_norm", kvn_fn, r_len, tm, [p_kvr, tk_c, tk_s], [kvg], None, [(kvr_w, BF)])[0]
    k_p = _mm("k_up", kvn, wk_ext, 'nn', tm=tm_l0, tn=hw, out_dtypes=(BF,))
    v_p = _mm("v_up", kvn, wv_ext, 'nn', tm=tm_l0, tn=hw, out_dtypes=(BF,))
    tq_h = _pick(t_len, 512, tm)
    o_h, lse_h = _attn_fwd("attn_h", q_r, k_p, v_p, t_len, 0, r_len, 0, heads, tm, attn_scale)
    o_g, lse_g = _attn_fwd("attn_g", q_r, k_p, v_p, g_len, t_len, g_len, t_len // g_len, heads, tm, attn_scale)
    y_h = _pool_fwd("pool_h", p_pool[:t_len], pool_w[0], pool_scale)
    y_g = _pool_fwd("pool_g", p_pool[t_len:], pool_w[0], pool_scale)
    cat = jnp.concatenate([jnp.concatenate([y_h, y_g], axis=0), jnp.concatenate([o_h, o_g], axis=0)], axis=1)

    def resid_epi(k3, n_lat, tmr):
        def epi(acc, i, sv, mv):
            return sv + _row_gate(mv, k3, i, tmr, n_lat) * acc, acc
        return epi

    s2, o_mix0 = _mm("mix_out_l0", cat, w_out_p, 'nn', tm=tm_l0, tn=d, out_dtypes=(F32, F32),
                     epi=resid_epi(5, t_len, tm_l0), epi_args=(s1, mods[0]), epi_kinds=('mn', 'w'))
    gw_s2, gather3, tok2 = stage_weights(2, gather2, s2)
    ffn_w[0][1] = ffn_weights(gw_s2)
    s3, sav_f01 = _ffn_half_fwd("l0b", s2, gains[0], mods[0], 2, ffn_w[0][1], t_len, tm, h_tiles, tm_l0, tok2)

    gw_s3, gather4, tok3 = stage_weights(3, gather3, s3)
    ffn_w[1][0] = ffn_weights(gw_s3)
    tml = 256 if t_len % 256 == 0 else tm
    h3 = s3[:t_len]
    h4, sav_f10 = _ffn_half_fwd("l1a", h3, gains[1], mods[1], 0, ffn_w[1][0], t_len, tml, None, tm_l1, tok3)
    gw_s4, gather5, tok4 = stage_weights(4, gather4, h4)
    cw_out_full = gw_s4["conv_w_out"].reshape(-1, d)
    cw_in_full = gw_s4["conv_w_in"].transpose(1, 0, 2).reshape(d, -1)
    u_cv = _adaln_fwd("adaln_l1m", h4, gains[1], 1, mods[1], 1, tml, None, tok4)
    p3 = _mm("conv_in", u_cv, cw_in_full, 'nn', tm=tm_l1, tn=512)
    cwp = conv_w_full
    tc = _pick(d, 256)
    y_cv = _conv_fwd("conv_fwd", p3, cwp, tc)
    h5, o_mix1 = _mm("mix_out_l1", y_cv, cw_out_full, 'nn', tm=tm_l1, tn=d, out_dtypes=(F32, F32),
                     epi=resid_epi(5, t_len, tm_l1), epi_args=(h4, mods[1]), epi_kinds=('mn', 'w'))
    gw_s5, _, tok5 = stage_weights(5, gather5, h5)
    ffn_w[1][1] = ffn_weights(gw_s5)
    h6, sav_f11 = _ffn_half_fwd("l1b", h5, gains[1], mods[1], 2, ffn_w[1][1], t_len, tml, None, tm_l1, tok5)

    fg = jnp.pad(final_norm_g[None], ((0, 7), (0, 0)))
    dh6, acc_loss = _loss_head("loss_head", h6, loss_target[0], fg, tml)
    d_final_g = acc_loss[0, 0]

    dgain = [[None] * 3 for _ in range(2)]
    dmod = [[None] * N_MOD for _ in range(2)]

    def put(l, k, triple):
        dmod[l][3 * k], dmod[l][3 * k + 1], dmod[l][3 * k + 2] = triple

    def empty_ffn_grads():
        return (lax.empty((N_SHARD, d, fc), BF), lax.empty((N_SHARD, d, fc), BF), lax.empty((N_SHARD, fc, d), BF))

    def by_shard_rows(g):
        return g.reshape(N_SHARD, -1, g.shape[-1])

    def by_shard_cols(g):
        return g.reshape(g.shape[0], N_SHARD, -1).transpose(1, 0, 2)

    def pair_start(k, big):
        send = [b.astype(BF).reshape(N_DEV, b.shape[1] // 2, b.shape[2]) for b in big]
        n = len(send)
        return _split_start(f"grads_pair_start_s{k}", send + _landing(N_SHARD, send), n * N_SHARD,
                            _sibling_halves_build(n))

    def chips_start(k, handle, after):
        bufs = _split_wait(f"grads_pair_wait_s{k}", handle, after)
        n = len(bufs) // 2
        pre = [_add_halves(f"grads_add_s{k}_{nm}", s, z) for nm, s, z in zip(stage_names(k), bufs[:n], bufs[n:])]
        return _split_start(f"grads_start_s{k}", pre + _landing(N_SHARD, pre), n * len(CHIP_FLIPS),
                            _chips_scatter_build(n))

    def landed_sums(k, handle, after):
        bufs = _split_wait(f"grads_wait_s{k}", handle, after)
        n = len(bufs) // 2
        landed = [lax.dynamic_update_slice_in_dim(z, lax.dynamic_slice_in_dim(p, shard, 1, 0), shard, 0)
                  for p, z in zip(bufs[:n], bufs[n:])]
        return [_sum_lead(f"sum_grads_s{k}_{nm}", z) for nm, z in zip(stage_names(k), landed)]

    pair, scatter, sums = [None] * 6, [None] * 6, [None] * 6
    grads, upd = {}, {}
    ffn_out = {nm: None for nm in ffn_names}

    def lanes_last(nm, t):
        shp = weights[nm].shape
        return jnp.swapaxes(t, -1, -2) if shp[-1] % 128 and shp[-2] % 128 == 0 else t

    def finish_stage(k, halves):
        n = len(halves)
        lands = [pltpu.with_memory_space_constraint(lax.empty(h.shape, h.dtype), pltpu.HBM) for h in halves]
        swap = _split_start(f"swap_start_s{k}", list(halves) + lands, n, _sibling_whole_build(n))
        both = _split_wait(f"swap_wait_s{k}", swap, swap['token'])
        for nm, a, g in zip(stage_names(k), both[:n], both[n:]):
            piece = jnp.where(ci == 0, jnp.concatenate([a, g], axis=0), jnp.concatenate([g, a], axis=0))
            if k % 3 == 1:
                grads[nm] = piece.reshape(weights[nm].shape)
                upd[nm] = _adamw(f"adamw_{nm}", weights[nm], grads[nm], mom_m[nm], mom_v[nm])
            else:
                ffn_out[nm] = _adamw_piece(f"adamw_{nm}_s{k}", lanes_last(nm, weights[nm]), lanes_last(nm, piece),
                                           lanes_last(nm, mom_m[nm]), lanes_last(nm, mom_v[nm]),
                                           (k // 3, (k % 3) // 2), ffn_out[nm])
    dh5, ffn_g, tr, dgain[1][2] = _ffn_half_bwd("l1b", dh6, sav_f11, gains[1], mods[1], 2, ffn_w[1][1],
                                                empty_ffn_grads(), tml, None, tm_l1)
    put(1, 2, tr)
    pair[5] = pair_start(5, list(ffn_g))
    d_o1, acc_g1 = _resid_bwd("resid_bwd_l1m", dh5, o_mix1, mods[1], 1, 1.0, tml, None, pair[5]['token'])
    dy_cv = _mm("mix_out_l1_dx", d_o1, cw_out_full, 'nt', tm=tm_l1, tn=d)
    d_cw_out = _mm("mix_out_l1_dw", y_cv, d_o1, 'tn', tm=512, tn=512)
    dp3, d_cw = _conv_bwd("conv_bwd", p3, cwp, dy_cv, tc)
    du_cv = _conv_din("conv_in_dx", dp3, cw_in_full, tm_l1)
    d_cw_in = _conv_dw_in("conv_in_dw", u_cv, dp3, _pick(d, MM_ROWS), _pick(d, 512))
    dh4, acc_n1 = _adaln_bwd("adaln_bwd_l1m", h4, du_cv, dh5, gains[1], 1, mods[1], 1, tml, None)
    put(1, 1, (acc_n1[:, 0], acc_n1[:, 1], acc_g1[:, 0]))
    dgain[1][1] = acc_n1[0, 2]
    scatter[5] = chips_start(5, pair[5], dh4)
    pair[4] = pair_start(4, [by_shard_cols(d_cw_in), by_shard_rows(d_cw_out)])
    dh3, ffn_g, tr, dgain[1][0] = _ffn_half_bwd("l1a", dh4, sav_f10, gains[1], mods[1], 0, ffn_w[1][0],
                                                empty_ffn_grads(), tml, None, tm_l1,
                                                scatter[5]['token'] + pair[4]['token'])
    put(1, 0, tr)
    finish_stage(5, landed_sums(5, scatter[5], dh3))
    scatter[4] = chips_start(4, pair[4], dh3)
    pair[3] = pair_start(3, list(ffn_g))

    ds3 = jnp.concatenate([dh3, jnp.zeros((g_len, d), F32)], axis=0) \
        + (scatter[4]['token'][0, 0] + pair[3]['token'][0, 0])
    ds2, ffn_g, tr, dgain[0][2] = _ffn_half_bwd("l0b", ds3, sav_f01, gains[0], mods[0], 2, ffn_w[0][1],
                                                empty_ffn_grads(), tm, h_tiles, tm_l0)
    put(0, 2, tr)
    finish_stage(4, landed_sums(4, scatter[4], ds2))
    scatter[3] = chips_start(3, pair[3], ds2)
    pair[2] = pair_start(2, list(ffn_g))
    d_o0, acc_g0 = _resid_bwd("resid_bwd_l0m", ds2, o_mix0, mods[0], 1, 1.0, tm, h_tiles,
                              scatter[3]['token'] + pair[2]['token'])
    dcat = _mm("mix_out_l0_dx", d_o0, w_out_p, 'nt', tm=tm_l0, tn=pool_dim + hw)
    d_w_out_p = _mm("mix_out_l0_dw", cat, d_o0, 'tn', tm=512, tn=512)
    col_blk = pool_dim // HEAD_PAD
    dq_h, dk_h, dv_h = _attn_bwd("attn_bwd_h", q_r, k_p, v_p, cat, dcat, lse_h, t_len, 0, r_len, 0, heads, tq_h,
                                 attn_scale, col_blk)
    dq_g, dk_all, dv_all = _attn_bwd("attn_bwd_g", q_r, k_p, v_p, cat, dcat, lse_g, g_len, t_len, g_len,
                                     t_len // g_len, heads, tm, attn_scale, col_blk, onto=(dk_h, dv_h))
    dq_all = jnp.concatenate([dq_h, dq_g], axis=0)
    dkvn = _mm("k_up_dx", dk_all, wk_ext, 'nt', tm=tm_l0, tn=kvr_w)
    dkvn = _mm("v_up_dx", dv_all, wv_ext, 'nt', tm=tm_l0, tn=kvr_w, epi=lambda acc, i, prev: (acc + prev,),
               epi_args=(dkvn,), epi_kinds=('mn',))
    d_wk_ext = _mm("k_up_dw", kvn, dk_all, 'tn', tm=kvr_w, tn=512)
    d_wv_ext = _mm("v_up_dw", kvn, dv_all, 'tn', tm=kvr_w, tn=512)

    def kvn_bwd_fn(rows, consts, m):
        pv, dv_, ct, st = rows
        g = consts[0][0:1, :]
        n, r = _rms(pv[:, :kv_rank])
        dyn = dv_[:, :kv_rank]
        dckv = _rms_bwd(dyn * g, n, r)
        dkr = _rope_t(dv_[:, kv_rank:], ct, st)
        return [jnp.concatenate([dckv, dkr], axis=1)], {0: jnp.sum(dyn * n, axis=0, keepdims=True)}

    dp_kvr, acc_kvg = _rows_call("kv_norm_bwd", kvn_bwd_fn, r_len, tm, [p_kvr, dkvn, tk_c, tk_s], [kvg], None,
                                 [(kvr_w, BF)], acc_w=kv_rank)

    def qrope_bwd_fn(rows, consts, m):
        return [_rope_t(rows[0], rows[1], rows[2])], {}

    dq_pad = _rows_call("q_rope_bwd", qrope_bwd_fn, r_len, tm, [dq_all, tq_c, tq_s], [], None, [(hw, BF)])[0]
    dqn = _mm("q_up_dx", dq_pad, wq_p, 'nt', tm=tm_l0, tn=q_rank)
    d_wq_p = _mm("q_up_dw", qn, dq_pad, 'tn', tm=512, tn=512)

    def qn_bwd_fn(rows, consts, m):
        pv, dv_ = rows
        g = consts[0][0:1, :]
        n, r = _rms(pv)
        return [_rms_bwd(dv_ * g, n, r)], {0: jnp.sum(dv_ * n, axis=0, keepdims=True)}

    dp_q, acc_qg = _rows_call("q_norm_bwd", qn_bwd_fn, r_len, tm, [p_q, dqn], [qg], None, [(q_rank, BF)],
                              acc_w=q_rank)
    dpu_h, dpw_h, dps_h = _pool_bwd("pool_bwd_h", p_pool[:t_len], dcat, pool_w[0], pool_scale, 0)
    dpu_g, dpw_g, dps_g = _pool_bwd("pool_bwd_g", p_pool[t_len:], dcat, pool_w[0], pool_scale, t_len)
    dp_pool = jnp.concatenate([dpu_h, dpu_g], axis=0)
    add_prev = lambda acc, i, prev: (acc + prev,)
    du_mix = _mm("in_pool_dx", dp_pool, w_in_pool, 'nt', tm=tm_l0, tn=d)
    du_mix = _mm("in_q_dx", dp_q, w_in_q, 'nt', tm=tm_l0, tn=d, epi=add_prev, epi_args=(du_mix,), epi_kinds=('mn',))
    du_mix = _mm("in_kvr_dx", dp_kvr, w_in_kvr, 'nt', tm=tm_l0, tn=d, epi=add_prev, epi_args=(du_mix,), epi_kinds=('mn',))
    d_w_in = jnp.concatenate([
        _mm("in_pool_dw", u_mix, dp_pool, 'tn', tm=512, tn=pool_dim),
        _mm("in_q_dw", u_mix, dp_q, 'tn', tm=512, tn=q_rank),
        _mm("in_kvr_dw", u_mix, dp_kvr, 'tn', tm=512, tn=kvr_w)[:, :kv_rank + QK_ROPE]], axis=1)
    ds1, acc_n0 = _adaln_bwd("adaln_bwd_l0m", s1, du_mix, ds2, gains[0], 1, mods[0], 1, tm, h_tiles)
    put(0, 1, (acc_n0[:, 0], acc_n0[:, 1], acc_g0[:, 0]))
    dgain[0][1] = jnp.sum(acc_n0[:, 2], axis=0)
    d_w_uq = d_wq_p.reshape(q_rank, heads, HEAD_PAD)[..., :QK_NOPE + QK_ROPE].reshape(q_rank, -1)
    d_w_ukv = jnp.concatenate([d_wk_ext[:kv_rank].reshape(kv_rank, heads, HEAD_PAD)[..., :QK_NOPE],
                               d_wv_ext[:kv_rank].reshape(kv_rank, heads, HEAD_PAD)[..., :V_HEAD]],
                              axis=-1).reshape(kv_rank, -1)
    d_w_out = jnp.concatenate([d_w_out_p[:pool_dim],
                               d_w_out_p[pool_dim:].reshape(heads, HEAD_PAD, d)[:, :V_HEAD].reshape(-1, d)], axis=0)
    finish_stage(3, landed_sums(3, scatter[3], ds1))
    scatter[2] = chips_start(2, pair[2], ds1)
    pair[1] = pair_start(1, [by_shard_cols(d_w_in), by_shard_rows(d_w_uq), by_shard_cols(d_w_ukv),
                             by_shard_rows(d_w_out)])
    ds0, ffn_g, tr, dgain[0][0] = _ffn_half_bwd("l0a", ds1, sav_f00, gains[0], mods[0], 0, ffn_w[0][0],
                                                empty_ffn_grads(), tm, h_tiles, tm_l0,
                                                scatter[2]['token'] + pair[1]['token'])
    put(0, 0, tr)
    grad_x = ds0[:t_len][None]
    finish_stage(2, landed_sums(2, scatter[2], ds0))
    pair[0] = pair_start(0, list(ffn_g))

    dmh = jnp.stack([jnp.stack([dmod[l][k][0] for k in range(N_MOD)]) for l in range(2)])
    dmg0 = jnp.stack([dmod[0][k][1] for k in range(N_MOD)])
    dg_rows = jnp.stack([dgain[l][k] for l in range(2) for k in range(3)])
    pieces = [dmh.reshape(2 * N_MOD, d), dmg0, dg_rows, d_cw[:3], d_final_g[None],
              (dpw_h + dpw_g).reshape(-1, d), jnp.pad((dps_h + dps_g)[0], (0, d - pool_dim))[None],
              jnp.pad(acc_qg[0, 0], (0, d - q_rank))[None], jnp.pad(acc_kvg[0, 0], (0, d - kv_rank))[None],
              acc_loss[0, 1][None]]
    n_piece = [p.shape[0] for p in pieces]
    pieces = [jnp.pad(p, ((0, (-p.shape[0]) % 8), (0, 0))) for p in pieces]
    small_g = jnp.concatenate(pieces, axis=0) + pair[0]['token'][0, 0]
    sg_all = _gather_all("gather_small_grads", [small_g])[0]
    sg_sum = _sum_lead("sum_small_grads", sg_all)
    scatter[1] = chips_start(1, pair[1], sg_sum)
    offs = [0]
    for p in pieces:
        offs.append(offs[-1] + p.shape[0])
    part = lambda j: sg_sum[offs[j]:offs[j] + n_piece[j]]
    sum_dmh, sum_dmg0, g_norm_full, g_conv_w_full = part(0).reshape(2, N_MOD * d), part(1).reshape(N_MOD * d), part(2), part(3)
    g_final = part(4)[0]
    loss = part(9)[0, 0]
    g_pool_w = part(5).reshape(pool_w.shape)
    g_pool_scale = part(6)[:, :pool_dim]
    g_q_norm = part(7)[:, :q_rank]
    g_kv_norm = part(8)[:, :kv_rank]
    col0 = shard * (d // N_SHARD)
    g_norm_g = lax.dynamic_slice_in_dim(g_norm_full.reshape(2, 3, d), col0, d // N_SHARD, axis=2)
    g_conv_w = lax.dynamic_slice_in_dim(g_conv_w_full, col0, d // N_SHARD, axis=1)[None]
    g_b_mod = _sum_lead("sum_b_mod", jnp.stack([sum_dmh, jnp.stack([sum_dmg0, jnp.zeros_like(sum_dmg0)])]))

    dm16 = []
    for l in range(2):
        per_dev = sg_all[:, l * N_MOD:(l + 1) * N_MOD].reshape(N_DEV, N_MOD * d)
        row8 = (sum_dmg0 if l == 0 else jnp.zeros_like(sum_dmg0)) + scatter[1]['token'][0, 0]
        full = jnp.concatenate([per_dev, row8[None], jnp.zeros((7, N_MOD * d), F32)], axis=0)
        dm16.append(lax.dynamic_slice_in_dim(full, shard * n_col, n_col, axis=1))
    g_w_mod = jnp.stack([_mm(f"mod_dw_{l}", c16, dm16[l], 'tn', tm=512, tn=768, a_pre=_silu) for l in range(2)])
    dc16 = _mm("mod_dx", dm16[0], w_mod, 'nt', tm=16, tn=512, b_lead=0, epi=lambda acc, i, cv: (acc * _dsilu(cv),),
               epi_args=(c16,), epi_kinds=('mn',))
    dc_all = _gather_all("gather_dc", [dc16])[0]
    g_c_ctx = _sum_lead("sum_dc", dc_all[::2])[8]

    grads.update(c_ctx=g_c_ctx, norm_g=g_norm_g, w_mod=g_w_mod, b_mod=g_b_mod, pool_w=g_pool_w,
                 pool_scale=g_pool_scale, q_norm_g=g_q_norm, kv_norm_g=g_kv_norm, conv_w=g_conv_w, final_norm_g=g_final)
    names = list(weights)

    scatter[0] = chips_start(0, pair[0], g_c_ctx)
    upd.update({n: _adamw(f"adamw_{n}", weights[n], grads[n].reshape(weights[n].shape), mom_m[n], mom_v[n],
                          scatter[0]['token']) for n in names if n not in big_names})
    finish_stage(1, landed_sums(1, scatter[1], upd["w_mod"][0]))
    finish_stage(0, landed_sums(0, scatter[0], upd[mixer_names[0][-1]][0]))
    for nm in ffn_names:
        done = [lanes_last(nm, t) for t in ffn_out[nm]]
        grads[nm], upd[nm] = done[0], done[1:]
    return (loss, grad_x, *[grads[n].reshape(weights[n].shape) for n in names], *[upd[n][0] for n in names],
            *[upd[n][1] for n in names], *[upd[n][2] for n in names])
```

```python
import functools
import math

import jax
import jax.numpy as jnp
from jax import lax
from jax.experimental import pallas as pl
from jax.experimental.pallas import tpu as pltpu

F32 = jnp.float32
BF = jnp.bfloat16
MESH = pl.DeviceIdType.MESH

N_DEV = 8
N_SHARD = 4
RMS_EPS = 1e-6
N_MOD = 9
POOL_WINDOWS = (2, 4, 8, 16)
QK_NOPE = 64
QK_ROPE = 32
V_HEAD = 64
HEAD_PAD = 128
GRID_W = 64
ROPE_THETA = 10000.0
POOL_PAD = 16
ADAM_LR, ADAM_B1, ADAM_B2, ADAM_EPS, ADAM_WD, ADAM_STEP = 0.001, 0.9, 0.999, 1e-08, 0.01, 10
VMEM_LIMIT = 56 * 1024 * 1024
MM_ROWS = 1024


def _pcall(body, **kw):
    return pl.pallas_call(body, **kw)


def _params(sem=None):
    return pltpu.CompilerParams(dimension_semantics=sem, vmem_limit_bytes=VMEM_LIMIT)


def _pick(n, pref, mult=128):
    best = None
    d = mult
    while d <= min(n, pref):
        if n % d == 0:
            best = d
        d += mult
    return best if best is not None else n


def _silu(z):
    return z * jax.nn.sigmoid(z)


def _dsilu(z):
    s = jax.nn.sigmoid(z)
    return s * (1.0 + z * (1.0 - s))


def _dot(a, b, dims):
    return lax.dot_general(a.astype(BF), b.astype(BF), (dims, ((), ())), preferred_element_type=F32)


NN = ((1,), (0,))
NT = ((1,), (1,))
TN = ((0,), (0,))


ALL_FLIPS = [(kx, ky, kc) for kx in (0, 1) for ky in (0, 1) for kc in (0, 1) if (kx, ky, kc) != (0, 0, 0)]
CHIP_FLIPS = [(1, 0, 0), (0, 1, 0), (1, 1, 0)]
SIBLING = (0, 0, 1)
COMM_SPLIT = 8
SPLIT_MIN_ROWS = 256


def _exchange(name, arrays, plan, lead, whole_src, split=COMM_SPLIT):
    n = len(arrays)
    blk_shapes = [tuple(a.shape) if whole_src else tuple(a.shape[1:]) for a in arrays]
    splits = []
    for shp in blk_shapes:
        s = 1
        while s * 2 <= split and shp[0] % (s * 2) == 0 and (shp[0] // (s * 2)) % 16 == 0 \
                and shp[0] // (s * 2) >= SPLIT_MIN_ROWS:
            s *= 2
        splits.append(s)
    items = plan(0, 0, 0)
    n_items = len(items)
    remote_ids = [k for k, it in enumerate(items) if it[0] is not None]
    local_ids = [k for k, it in enumerate(items) if it[0] is None]
    slots = [(a, s) for s in range(max(splits)) for a in range(n) if s < splits[a]]
    n_slot = len(slots)

    def body(*refs):
        ins, outs = refs[:n], refs[n:2 * n]
        send_sems, recv_sems, loc_sems = refs[2 * n:]
        x, y, c = lax.axis_index("x"), lax.axis_index("y"), lax.axis_index("c")
        plan_here = plan(x, y, c)

        def rows(ref, a, s):
            rc = blk_shapes[a][0] // splits[a]
            return ref.at[pl.ds(s * rc, rc)]

        def make(si, k):
            a, s = slots[si]
            flip, src, dst, _ = plan_here[k]
            base = outs[a] if src[0] == 'out' else ins[a]
            src_ref = rows(base if src[1] is None else base.at[src[1]], a, s)
            dst_ref = rows(outs[a].at[dst], a, s)
            if flip is None:
                return pltpu.make_async_copy(src_ref, dst_ref, loc_sems.at[si * max(1, len(local_ids)) + local_ids.index(k)])
            peer = (1 - x if flip[0] else x, 1 - y if flip[1] else y, 1 - c if flip[2] else c)
            sem = si * len(remote_ids) + remote_ids.index(k)
            return pltpu.make_async_remote_copy(src_ref=src_ref, dst_ref=dst_ref, send_sem=send_sems.at[sem],
                                                recv_sem=recv_sems.at[sem], device_id=peer, device_id_type=MESH)

        copies = {}
        for si in range(n_slot):
            for k in range(n_items):
                if plan_here[k][3] is None:
                    copies[si, k] = make(si, k)
                    copies[si, k].start()
        arrived = set()
        for si in range(n_slot):
            for k in range(n_items):
                after = plan_here[k][3]
                if after is not None:
                    if (si, after) not in arrived:
                        copies[si, after].wait_recv()
                        arrived.add((si, after))
                    copies[si, k] = make(si, k)
                    copies[si, k].start()
        for (si, k), cp in copies.items():
            if plan_here[k][0] is None:
                cp.wait()
            else:
                cp.wait_send()
                if (si, k) not in arrived:
                    cp.wait_recv()

    any_spec = pl.BlockSpec(memory_space=pl.ANY)
    n_rem = max(1, n_slot * len(remote_ids))
    outs = _pcall(
        body, name=name,
        out_shape=[jax.ShapeDtypeStruct((lead,) + s, a.dtype) for s, a in zip(blk_shapes, arrays)],
        in_specs=[any_spec] * n, out_specs=[any_spec] * n,
        scratch_shapes=[pltpu.SemaphoreType.DMA((n_rem,)), pltpu.SemaphoreType.DMA((n_rem,)),
                        pltpu.SemaphoreType.DMA((max(1, n_slot * len(local_ids)),))],
    )(*arrays)
    return list(outs)


def _place(x, y, c):
    return 4 * x + 2 * y + c


def _flip(v, f):
    return 1 - v if f else v


def _gather_all(name, arrays):
    def plan(x, y, c):
        me = _place(x, y, c)
        return [(None, ('in', None), me, None)] + [(f, ('in', None), me, None) for f in ALL_FLIPS]
    return _exchange(name, arrays, plan, N_DEV, True)


HBM_SPEC = pl.BlockSpec(memory_space=pltpu.HBM)
SEM_SPEC = pl.BlockSpec(memory_space=pltpu.SEMAPHORE)
SIDE_EFFECT = pltpu.SideEffectType.DATAFLOW_SIDE_EFFECTING


def _split_start(name, bufs, n_copies, build):
    n = len(bufs)

    def body(*refs):
        for cp in build(refs[:n], refs[n], refs[n + 1]):
            cp.start()
        token = refs[-1]
        token[...] = jnp.zeros_like(token)

    res = _pcall(
        body, name=name,
        out_shape=(pltpu.SemaphoreType.DMA((n_copies,)), pltpu.SemaphoreType.DMA((n_copies,)),
                   *[pltpu.HBM(b.shape, b.dtype) for b in bufs], jax.ShapeDtypeStruct((8, 128), F32)),
        in_specs=[HBM_SPEC] * n,
        out_specs=(SEM_SPEC, SEM_SPEC, *[HBM_SPEC] * n, pl.BlockSpec(memory_space=pltpu.VMEM)),
        input_output_aliases={i: 2 + i for i in range(n)},
        compiler_params=pltpu.CompilerParams(has_side_effects=SIDE_EFFECT),
    )(*[pltpu.with_memory_space_constraint(b, pltpu.HBM) for b in bufs])
    return dict(send=res[0], recv=res[1], bufs=list(res[2:2 + n]), token=res[-1], build=build)


def _split_wait(name, handle, after):
    n = len(handle['bufs'])
    build = handle['build']

    def body(*refs):
        for cp in build(refs[:n], refs[n], refs[n + 1]):
            cp.wait_send()
            cp.wait_recv()

    res = _pcall(
        body, name=name, out_shape=tuple(pltpu.HBM(b.shape, b.dtype) for b in handle['bufs']),
        in_specs=[HBM_SPEC] * n + [SEM_SPEC, SEM_SPEC, pl.BlockSpec(memory_space=pl.ANY)],
        out_specs=tuple([HBM_SPEC] * n), input_output_aliases={i: i for i in range(n)},
        compiler_params=pltpu.CompilerParams(has_side_effects=SIDE_EFFECT),
    )(*handle['bufs'], handle['send'], handle['recv'], after)
    return list(res)


def _landing(lead, arrays):
    return [pltpu.with_memory_space_constraint(lax.empty((lead,) + tuple(a.shape[1:]), a.dtype), pltpu.HBM)
            for a in arrays]


def _copy_list(n, per_array, make):
    def build(refs, send_sems, recv_sems):
        copies = []
        for a in range(n):
            for j in range(per_array):
                src, dst, peer = make(refs, a, j)
                k = a * per_array + j
                copies.append(pltpu.make_async_remote_copy(src_ref=src, dst_ref=dst, send_sem=send_sems.at[k],
                                                           recv_sem=recv_sems.at[k], device_id=peer,
                                                           device_id_type=MESH))
        return copies
    return build


def _mesh_place():
    x, y, c = lax.axis_index("x"), lax.axis_index("y"), lax.axis_index("c")
    return x, y, c, 2 * x + y


def _chips_gather_build(n):
    def make(refs, a, j):
        x, y, c, chip = _mesh_place()
        px, py = _flip(x, CHIP_FLIPS[j][0]), _flip(y, CHIP_FLIPS[j][1])
        return refs[a].at[c], refs[n + a].at[2 * chip + c], (px, py, c)
    return _copy_list(n, len(CHIP_FLIPS), make)


def _chips_scatter_build(n):
    def make(refs, a, j):
        x, y, c, chip = _mesh_place()
        px, py = _flip(x, CHIP_FLIPS[j][0]), _flip(y, CHIP_FLIPS[j][1])
        return refs[a].at[2 * px + py], refs[n + a].at[chip], (px, py, c)
    return _copy_list(n, len(CHIP_FLIPS), make)


def _sibling_forward_build(n):
    def make(refs, a, j):
        x, y, c, _ = _mesh_place()
        blk = 2 * (2 * _flip(x, CHIP_FLIPS[j][0]) + _flip(y, CHIP_FLIPS[j][1])) + c
        return refs[a].at[blk], refs[a].at[blk], (x, y, 1 - c)
    return _copy_list(n, len(CHIP_FLIPS), make)


def _sibling_halves_build(n):
    def make(refs, a, j):
        x, y, c, _ = _mesh_place()
        return refs[a].at[2 * j + 1 - c], refs[n + a].at[j], (x, y, 1 - c)
    return _copy_list(n, N_SHARD, make)


def _sibling_whole_build(n):
    def make(refs, a, j):
        x, y, c, _ = _mesh_place()
        return refs[a], refs[n + a], (x, y, 1 - c)
    return _copy_list(n, 1, make)


def _add_halves(name, send, land):
    _, r, cdim = send.shape
    tr = _pick(r, max(16, (1 << 20) // (cdim * 2)), 16)

    def body(c_ref, own_ref, got_ref, o_ref):
        o_ref[...] = (own_ref[...].astype(F32) + got_ref[...].astype(F32)).astype(BF)

    grid_spec = pltpu.PrefetchScalarGridSpec(
        num_scalar_prefetch=1, grid=(N_SHARD, r // tr),
        in_specs=[pl.BlockSpec((None, tr, cdim), lambda sh, i, cr: (2 * sh + cr[0], i, 0)),
                  pl.BlockSpec((None, tr, cdim), lambda sh, i, cr: (sh, i, 0))],
        out_specs=pl.BlockSpec((None, tr, cdim), lambda sh, i, cr: (sh, i, 0)))
    core = lax.axis_index("c").astype(jnp.int32).reshape(1)
    return _pcall(body, name=name, grid_spec=grid_spec, out_shape=jax.ShapeDtypeStruct((N_SHARD, r, cdim), BF),
                  compiler_params=_params(("arbitrary", "arbitrary")))(core, send, land)


def _sum_lead(name, arr, out_dtype=F32):
    n, r, cdim = arr.shape
    tr = r
    limit = (4 << 20) // (n * cdim * arr.dtype.itemsize)
    if r > limit:
        tr = _pick(r, max(limit, 16), 16)

    def body(x_ref, o_ref):
        acc = x_ref[0].astype(F32)
        for d in range(1, n):
            acc = acc + x_ref[d].astype(F32)
        o_ref[...] = acc.astype(out_dtype)

    return _pcall(body, name=name, grid=(r // tr,),
                  in_specs=[pl.BlockSpec((n, tr, cdim), lambda i: (0, i, 0))],
                  out_specs=pl.BlockSpec((tr, cdim), lambda i: (i, 0)),
                  out_shape=jax.ShapeDtypeStruct((r, cdim), out_dtype),
                  compiler_params=_params(("arbitrary",)))(arr)


def _rows_call(name, fn, n_rows, tm, rows, consts, mod, outs, acc_w=None, h_tiles=None):
    nt = n_rows // tm
    ht = nt if h_tiles is None else h_tiles
    ng = 1 if mod is None else mod.shape[0]
    n_r, n_c, n_o = len(rows), len(consts), len(outs)
    has_mod = mod is not None

    def body(*refs):
        i = pl.program_id(0)
        first = (i % ht) == 0
        row_refs, const_refs = refs[:n_r], refs[n_r:n_r + n_c]
        p = n_r + n_c
        mod_tile = refs[p][...] if has_mod else None
        p += int(has_mod)
        out_refs = refs[p:p + n_o]
        o, acc = fn([r[...] for r in row_refs], [r[...] for r in const_refs], mod_tile)
        for r, v in zip(out_refs, o):
            r[...] = v.astype(r.dtype)
        if acc_w is not None:
            acc_ref = refs[p + n_o]

            @pl.when(first)
            def _():
                acc_ref[...] = jnp.zeros_like(acc_ref)

            for k, v in acc.items():
                acc_ref[k:k + 1, :] += v

    in_specs = [pl.BlockSpec((tm, r.shape[1]), lambda i: (i, 0)) for r in rows]
    in_specs += [pl.BlockSpec(cst.shape, lambda i, nd=cst.ndim: (0,) * nd) for cst in consts]
    args = list(rows) + list(consts)
    if has_mod:
        in_specs.append(pl.BlockSpec((None,) + mod.shape[1:], lambda i: (i // ht, 0, 0)))
        args.append(mod)
    out_shape = [jax.ShapeDtypeStruct((n_rows, w), dt) for w, dt in outs]
    out_specs = [pl.BlockSpec((tm, w), lambda i: (i, 0)) for w, _ in outs]
    if acc_w is not None:
        out_shape.append(jax.ShapeDtypeStruct((ng, 8, acc_w), F32))
        out_specs.append(pl.BlockSpec((None, 8, acc_w), lambda i: (i // ht, 0, 0)))
    res = _pcall(body, name=name, grid=(nt,), in_specs=in_specs, out_specs=out_specs, out_shape=out_shape,
                 compiler_params=_params(("arbitrary",)))(*args)
    return list(res)


def _rms(s):
    r = lax.rsqrt(jnp.mean(s * s, axis=1, keepdims=True) + RMS_EPS)
    return s * r, r


def _rms_bwd(dn, n, r):
    return r * (dn - n * jnp.mean(dn * n, axis=1, keepdims=True))


def _adaln_fwd(name, s, gains, gain_row, mod, k, tm, h_tiles, after=None):
    def fn(rows, consts, m):
        n, _ = _rms(rows[0])
        y = n * consts[0][gain_row:gain_row + 1, :]
        return [y * (1.0 + m[3 * k + 1:3 * k + 2, :]) + m[3 * k:3 * k + 1, :]], {}

    d = s.shape[1]
    consts = [gains] if after is None else [gains, after]
    return _rows_call(name, fn, s.shape[0], tm, [s], consts, mod, [(d, BF)], h_tiles=h_tiles)[0]


def _adaln_bwd(name, s, du, ds_res, gains, gain_row, mod, k, tm, h_tiles):
    def fn(rows, consts, m):
        sv, duv, res = rows
        gain = consts[0][gain_row:gain_row + 1, :]
        n, r = _rms(sv)
        y = n * gain
        dy = duv * (1.0 + m[3 * k + 1:3 * k + 2, :])
        acc = {0: jnp.sum(duv, axis=0, keepdims=True), 1: jnp.sum(duv * y, axis=0, keepdims=True),
               2: jnp.sum(dy * n, axis=0, keepdims=True)}
        return [_rms_bwd(dy * gain, n, r) + res], acc

    d = s.shape[1]
    return _rows_call(name, fn, s.shape[0], tm, [s, du, ds_res], [gains], mod, [(d, F32)], acc_w=d, h_tiles=h_tiles)


def _resid_bwd(name, ds_out, o, mod, k, cst, tm, h_tiles, after=None):
    def fn(rows, consts, m):
        dsv, ov = rows
        gate = m[3 * k + 2:3 * k + 3, :]
        return [cst * gate * dsv], {0: jnp.sum(cst * ov * dsv, axis=0, keepdims=True)}

    d = o.shape[1]
    consts = [] if after is None else [after]
    return _rows_call(name, fn, o.shape[0], tm, [ds_out, o], consts, mod, [(d, BF)], acc_w=d, h_tiles=h_tiles)


def _mm(name, a, b, mode, tm=256, tn=512, out_dtypes=(F32,), epi=None, epi_args=(), epi_kinds=(), a_pre=None,
        b_lead=None):
    bshape = b.shape if b_lead is None else b.shape[1:]
    if mode == 'nn':
        (m, kd), nd = a.shape, bshape[1]
    elif mode == 'nt':
        (m, kd), nd = a.shape, bshape[0]
    else:
        (kd, m), nd = a.shape, bshape[1]
    tm = _pick(m, tm, 16) if m % tm else tm
    tn = _pick(nd, tn, 128) if nd % tn else tn
    dims = {'nn': NN, 'nt': NT, 'tn': TN}[mode]
    n_e, n_o = len(epi_args), len(out_dtypes)

    def body(*refs):
        i = pl.program_id(1)
        av = refs[0][...]
        if a_pre is not None:
            av = a_pre(av)
        acc = _dot(av, refs[1][...], dims)
        res = (acc,) if epi is None else epi(acc, i, *[r[...] for r in refs[2:2 + n_e]])
        for r, v in zip(refs[2 + n_e:], res):
            r[...] = v.astype(r.dtype)

    if mode == 'nn':
        specs = [pl.BlockSpec((tm, kd), lambda j, i: (i, 0)), pl.BlockSpec((kd, tn), lambda j, i: (0, j))]
    elif mode == 'nt':
        specs = [pl.BlockSpec((tm, kd), lambda j, i: (i, 0)), pl.BlockSpec((tn, kd), lambda j, i: (j, 0))]
    else:
        specs = [pl.BlockSpec((kd, tm), lambda j, i: (0, i)), pl.BlockSpec((kd, tn), lambda j, i: (0, j))]
    if b_lead is not None:
        shape2, at2 = specs[1].block_shape, specs[1].index_map
        specs[1] = pl.BlockSpec((None,) + tuple(shape2), lambda j, i: (b_lead,) + tuple(at2(j, i)))
    for arr, kind in zip(epi_args, epi_kinds):
        if kind == 'mn':
            specs.append(pl.BlockSpec((tm, tn), lambda j, i: (i, j)))
        elif kind == 'n':
            specs.append(pl.BlockSpec((1, tn), lambda j, i: (0, j)))
        elif kind == 'mt':
            specs.append(pl.BlockSpec((tm, arr.shape[1]), lambda j, i: (i, 0)))
        else:
            specs.append(pl.BlockSpec(arr.shape, lambda j, i, nd_=arr.ndim: (0,) * nd_))
    res = _pcall(body, name=name, grid=(nd // tn, m // tm), in_specs=specs,
                 out_specs=[pl.BlockSpec((tm, tn), lambda j, i: (i, j))] * n_o,
                 out_shape=[jax.ShapeDtypeStruct((m, nd), dt) for dt in out_dtypes],
                 compiler_params=_params(("arbitrary", "arbitrary")))(a, b, *epi_args)
    return res[0] if n_o == 1 else list(res)


def _row_gate(mod, k3, i, tm, n_lat):
    g0 = mod[0, k3:k3 + 1, :]
    if mod.shape[0] == 1:
        return g0
    rid = i * tm + lax.broadcasted_iota(jnp.int32, (tm, 1), 0)
    return jnp.where(rid < n_lat, g0, mod[1, k3:k3 + 1, :])


def _ffn_up(name, u, wg, wu, base, tm):
    r, d = u.shape
    nch, _, _, fc = wg.shape

    def body(u_ref, wg_ref, wu_ref, a_ref, b_ref, h_ref):
        uv = u_ref[...]
        a = _dot(uv, wg_ref[...], NN)
        b = _dot(uv, wu_ref[...], NN)
        a_ref[...] = a.astype(BF)
        b_ref[...] = b.astype(BF)
        h_ref[...] = (_silu(a) * b).astype(BF)

    chunk = pl.BlockSpec((None, tm, fc), lambda j, i: (j, i, 0))
    return _pcall(body, name=name, grid=(nch, r // tm),
                  in_specs=[pl.BlockSpec((tm, d), lambda j, i: (i, 0)),
                            pl.BlockSpec((None, None, d, fc), lambda j, i: (j, base, 0, 0)),
                            pl.BlockSpec((None, None, d, fc), lambda j, i: (j, base, 0, 0))],
                  out_specs=[chunk] * 3, out_shape=[jax.ShapeDtypeStruct((nch, r, fc), BF)] * 3,
                  compiler_params=_params(("arbitrary", "arbitrary")))(u, wg, wu)


def _ffn_down(name, hid, wd, wd_blk, s, mod, k, n_lat, tm):
    nch, r, fc = hid.shape
    d = wd.shape[2]

    def body(h_ref, w_ref, s_ref, m_ref, so_ref, o_ref):
        i = pl.program_id(0)
        o = _dot(h_ref[0], w_ref[0], NN)
        for j in range(1, nch):
            o = o + _dot(h_ref[j], w_ref[j], NN)
        o_ref[...] = o
        so_ref[...] = s_ref[...] + 0.5 * _row_gate(m_ref[...], 3 * k + 2, i, tm, n_lat) * o

    row = pl.BlockSpec((tm, d), lambda i: (i, 0))
    return _pcall(body, name=name, grid=(r // tm,),
                  in_specs=[pl.BlockSpec((nch, tm, fc), lambda i: (0, i, 0)),
                            pl.BlockSpec((nch, fc, d), lambda i: (0, wd_blk, 0)), row,
                            pl.BlockSpec(mod.shape, lambda i: (0, 0, 0))],
                  out_specs=[row, row], out_shape=[jax.ShapeDtypeStruct((r, d), F32)] * 2,
                  compiler_params=_params(("arbitrary",)))(hid, wd, s, mod)


def _ffn_dhid(name, d_o, wd, wd_blk, a, b, tm):
    r, d = d_o.shape
    nch, _, fc = a.shape

    def body(g_ref, w_ref, a_ref, b_ref, da_ref, db_ref):
        dh = _dot(g_ref[...], w_ref[...], NT)
        av, bv = a_ref[...].astype(F32), b_ref[...].astype(F32)
        da_ref[...] = (dh * bv * _dsilu(av)).astype(BF)
        db_ref[...] = (dh * _silu(av)).astype(BF)

    chunk = pl.BlockSpec((None, tm, fc), lambda j, i: (j, i, 0))
    return _pcall(body, name=name, grid=(nch, r // tm),
                  in_specs=[pl.BlockSpec((tm, d), lambda j, i: (i, 0)),
                            pl.BlockSpec((None, fc, d), lambda j, i: (j, wd_blk, 0)), chunk, chunk],
                  out_specs=[chunk] * 2, out_shape=[jax.ShapeDtypeStruct((nch, r, fc), BF)] * 2,
                  compiler_params=_params(("arbitrary", "arbitrary")))(d_o, wd, a, b)


def _ffn_du(name, da, db, wg, wu, base, tm):
    nch, r, fc = da.shape
    d = wg.shape[2]

    def body(da_ref, db_ref, wg_ref, wu_ref, o_ref):
        acc = _dot(da_ref[0], wg_ref[0], NT) + _dot(db_ref[0], wu_ref[0], NT)
        for j in range(1, nch):
            acc = acc + _dot(da_ref[j], wg_ref[j], NT) + _dot(db_ref[j], wu_ref[j], NT)
        o_ref[...] = acc

    chunks = pl.BlockSpec((nch, tm, fc), lambda i: (0, i, 0))
    held = pl.BlockSpec((nch, None, d, fc), lambda i: (0, base, 0, 0), pipeline_mode=pl.Buffered(1))
    return _pcall(body, name=name, grid=(r // tm,), in_specs=[chunks, chunks, held, held],
                  out_specs=pl.BlockSpec((tm, d), lambda i: (i, 0)), out_shape=jax.ShapeDtypeStruct((r, d), F32),
                  compiler_params=_params(("arbitrary",)))(da, db, wg, wu)


def _ffn_dw_in(name, u, da, db, tmm, g_gate, g_up, idx):
    r, d = u.shape
    nch, _, fc = da.shape
    nb = d // tmm

    def body(u_ref, a_ref, b_ref, gg_ref, gu_ref, og_ref, ou_ref):
        ut = u_ref[...].T
        og_ref[...] = _dot(ut, a_ref[...], NN).astype(og_ref.dtype)
        ou_ref[...] = _dot(ut, b_ref[...], NN).astype(ou_ref.dtype)

    chunk = pl.BlockSpec((None, r, fc), lambda j, mi: (j, 0, 0))
    out = pl.BlockSpec((None, tmm, fc), lambda j, mi: (j, idx * nb + mi, 0))
    return _pcall(body, name=name, grid=(nch, nb),
                  in_specs=[pl.BlockSpec((r, tmm), lambda j, mi: (0, mi)), chunk, chunk,
                            pl.BlockSpec(memory_space=pl.ANY), pl.BlockSpec(memory_space=pl.ANY)],
                  out_specs=[out, out],
                  out_shape=[jax.ShapeDtypeStruct(g_gate.shape, g_gate.dtype),
                             jax.ShapeDtypeStruct(g_up.shape, g_up.dtype)],
                  input_output_aliases={3: 0, 4: 1},
                  compiler_params=_params(("arbitrary", "arbitrary")))(u, da, db, g_gate, g_up)


def _ffn_dw_down(name, hid, d_o, tn, grads, idx):
    nch, r, fc = hid.shape
    d = d_o.shape[1]

    def body(h_ref, g_ref, acc_ref, o_ref):
        o_ref[...] = _dot(h_ref[...], g_ref[...], TN).astype(o_ref.dtype)

    return _pcall(body, name=name, grid=(nch, d // tn),
                  in_specs=[pl.BlockSpec((None, r, fc), lambda j, ni: (j, 0, 0)),
                            pl.BlockSpec((r, tn), lambda j, ni: (0, ni)),
                            pl.BlockSpec(memory_space=pl.ANY)],
                  out_specs=pl.BlockSpec((None, fc, tn), lambda j, ni: (j, idx, ni)),
                  out_shape=jax.ShapeDtypeStruct(grads.shape, grads.dtype), input_output_aliases={2: 0},
                  compiler_params=_params(("arbitrary", "arbitrary")))(hid, d_o, grads)


def _partner(x):
    n = x.shape[1]
    lane = lax.broadcasted_iota(jnp.int32, x.shape, 1)
    return jnp.where((lane & 15) < 8, pltpu.roll(x, n - 8, 1), pltpu.roll(x, 8, 1))


def _rope(x, ct, st):
    reps = x.shape[1] // ct.shape[1]
    if reps > 1:
        ct, st = jnp.tile(ct, (1, reps)), jnp.tile(st, (1, reps))
    return x * ct + _partner(x) * st


def _rope_t(dy, ct, st):
    reps = dy.shape[1] // ct.shape[1]
    if reps > 1:
        ct, st = jnp.tile(ct, (1, reps)), jnp.tile(st, (1, reps))
    return dy * ct + _partner(dy * st)


def _rope_tables(t_len, g_len, lane0):
    half = QK_ROPE // 4
    pos = jnp.arange(t_len)
    row = (pos // GRID_W).astype(F32)
    col = (pos % GRID_W).astype(F32)
    freqs = jnp.power(ROPE_THETA, -jnp.arange(0, QK_ROPE // 2, 2, dtype=F32) / (QK_ROPE // 2))
    ang_r, ang_c = row[:, None] * freqs, col[:, None] * freqs
    cs = jnp.concatenate([jnp.cos(ang_r)] * 2 + [jnp.cos(ang_c)] * 2, axis=1)
    sn = jnp.concatenate([-jnp.sin(ang_r), jnp.sin(ang_r), -jnp.sin(ang_c), jnp.sin(ang_c)], axis=1)
    assert cs.shape[1] == 4 * half == QK_ROPE
    def place(tab, fill):
        rest = HEAD_PAD - lane0 - QK_ROPE
        rows = jnp.concatenate([jnp.full((t_len, lane0), fill, F32), tab, jnp.full((t_len, rest), fill, F32)], axis=1)
        return jnp.concatenate([rows, jnp.full((g_len, HEAD_PAD), fill, F32)], axis=0)

    return place(cs, 1.0), place(sn, 0.0)


def _attn_fwd(name, q, kp, vp, n_q, q_off, n_k, k_blk, heads, tq, scale):
    qb = q_off // tq
    per = 2 if heads % 2 == 0 else 1
    wide = per * HEAD_PAD

    def body(q_ref, k_ref, v_ref, o_ref, l_ref):
        for e in range(per):
            sl = slice(e * HEAD_PAD, (e + 1) * HEAD_PAD)
            s = _dot(q_ref[:, sl], k_ref[:, sl], NT) * scale
            m = jnp.max(s, axis=1, keepdims=True)
            p = jnp.exp(s - m)
            l = jnp.sum(p, axis=1, keepdims=True)
            o_ref[:, sl] = (_dot(p, v_ref[:, sl], NN) / l).astype(BF)
            l_ref[:, sl] = jnp.broadcast_to(m + jnp.log(l), (tq, HEAD_PAD))

    hw = heads * HEAD_PAD
    blk = pl.BlockSpec((tq, wide), lambda h, i: (i, h))
    kv = pl.BlockSpec((n_k, wide), lambda h, i: (k_blk, h))
    return _pcall(body, name=name, grid=(heads // per, n_q // tq),
                  in_specs=[pl.BlockSpec((tq, wide), lambda h, i: (i + qb, h)), kv, kv],
                  out_specs=[blk, blk],
                  out_shape=[jax.ShapeDtypeStruct((n_q, hw), BF), jax.ShapeDtypeStruct((n_q, hw), F32)],
                  compiler_params=_params(("arbitrary", "arbitrary")))(q, kp, vp)


def _attn_bwd(name, q, kp, vp, cat, dcat, lse, n_q, q_off, n_k, k_blk, heads, tq, scale, col_blk, onto=None):
    qb = q_off // tq

    per = 1
    wide = per * HEAD_PAD

    def body(q_ref, k_ref, v_ref, o_ref, do_ref, l_ref, *rest):
        dq_ref, dk_ref, dv_ref = rest[-3:]
        i = pl.program_id(1)
        for e in range(per):
            sl = slice(e * HEAD_PAD, (e + 1) * HEAD_PAD)
            qv, kv_, vv = q_ref[:, sl], k_ref[:, sl], v_ref[:, sl]
            dov = do_ref[:, sl]
            s = _dot(qv, kv_, NT) * scale
            p = jnp.exp(s - l_ref[:, e * HEAD_PAD:e * HEAD_PAD + 1])
            dp = _dot(dov, vv, NT)
            delta = jnp.sum(dov * o_ref[:, sl].astype(F32), axis=1, keepdims=True)
            ds = (p * (dp - delta) * scale).astype(BF)
            dq_ref[:, sl] = _dot(ds, kv_, NN)
            dk = _dot(ds, qv, TN)
            dv = _dot(p, dov, TN)

            @pl.when(i == 0)
            def _():
                if onto is None:
                    dk_ref[:, sl] = dk
                    dv_ref[:, sl] = dv
                else:
                    dk_ref[:, sl] = rest[0][:, sl] + dk
                    dv_ref[:, sl] = rest[1][:, sl] + dv

            @pl.when(i > 0)
            def _():
                dk_ref[:, sl] += dk
                dv_ref[:, sl] += dv

    hw = heads * HEAD_PAD
    heads = heads // per
    col_blk = col_blk // per
    qspec = pl.BlockSpec((tq, wide), lambda h, i: (i + qb, h))
    cspec = pl.BlockSpec((tq, wide), lambda h, i: (i + qb, col_blk + h))
    kv = pl.BlockSpec((n_k, wide), lambda h, i: (k_blk, h))
    blk = pl.BlockSpec((tq, wide), lambda h, i: (i, h))
    if onto is None:
        acc = pl.BlockSpec((n_k, wide), lambda h, i: (0, h))
        return _pcall(body, name=name, grid=(heads, n_q // tq),
                      in_specs=[qspec, kv, kv, cspec, cspec, blk], out_specs=[blk, acc, acc],
                      out_shape=[jax.ShapeDtypeStruct((n_q, hw), F32), jax.ShapeDtypeStruct((n_k, hw), F32),
                                 jax.ShapeDtypeStruct((n_k, hw), F32)],
                      compiler_params=_params(("arbitrary", "arbitrary")))(q, kp, vp, cat, dcat, lse)
    return _pcall(body, name=name, grid=(heads, n_q // tq),
                  in_specs=[qspec, kv, kv, cspec, cspec, blk, kv, kv], out_specs=[blk, kv, kv],
                  out_shape=[jax.ShapeDtypeStruct((n_q, hw), F32)] + [jax.ShapeDtypeStruct(t.shape, F32) for t in onto],
                  input_output_aliases={6: 1, 7: 2},
                  compiler_params=_params(("arbitrary", "arbitrary")))(q, kp, vp, cat, dcat, lse, *onto)


def _shift(x, k):
    return pltpu.roll(x, k % x.shape[0], 0)


def _window_sum(v, w, mirrored):
    n, gd = v.shape
    pad = jnp.zeros((POOL_PAD, gd), F32)
    e = jnp.concatenate([pad, v, pad], axis=0)
    acc = e + _shift(e, -1 if mirrored else 1)
    step = 1
    while 2 * step < w:
        acc = _shift(acc, step) + _shift(acc, -step)
        step *= 2
    return acc[POOL_PAD:POOL_PAD + n]


def _window_count(n, w):
    t = lax.broadcasted_iota(jnp.int32, (n, 1), 0)
    lo = jnp.maximum(t - w // 2, 0)
    hi = jnp.minimum(t + (w - w // 2 - 1), n - 1)
    return (hi - lo + 1).astype(F32)


def _pool_fwd(name, u, pool_w, scale):
    n, pd = u.shape
    ng = len(POOL_WINDOWS)
    gd = pd // ng

    def body(u_ref, w_ref, s_ref, y_ref):
        for g, w in enumerate(POOL_WINDOWS):
            sl = slice(g * gd, (g + 1) * gd)
            ug = u_ref[:, sl]
            p = _window_sum(ug, w, False) / _window_count(n, w) - ug
            y_ref[:, sl] = (_dot(p, w_ref[g], NN) * s_ref[:, sl]).astype(BF)

    return _pcall(body, name=name, out_shape=jax.ShapeDtypeStruct((n, pd), BF),
                  compiler_params=_params())(u, pool_w, scale)


def _pool_bwd(name, u, dcat, pool_w, scale, row_off):
    n, pd = u.shape
    ng = len(POOL_WINDOWS)
    gd = pd // ng

    def body(u_ref, dy_ref, w_ref, s_ref, du_ref, dw_ref, ds_ref):
        ds_ref[...] = jnp.zeros_like(ds_ref)
        for g, w in enumerate(POOL_WINDOWS):
            sl = slice(g * gd, (g + 1) * gd)
            ug, dy, wg = u_ref[:, sl], dy_ref[:, sl], w_ref[g]
            cnt = _window_count(n, w)
            p = _window_sum(ug, w, False) / cnt - ug
            ds_ref[0:1, sl] = jnp.sum(dy * _dot(p, wg, NN), axis=0, keepdims=True)
            dys = dy * s_ref[:, sl]
            dw_ref[g] = _dot(p, dys, TN)
            dp = _dot(dys, wg, NT)
            du_ref[:, sl] = (_window_sum(dp / cnt, w, True) - dp).astype(BF)

    rb = row_off // n
    return _pcall(body, name=name, grid=(1,),
                  in_specs=[pl.BlockSpec((n, pd), lambda i: (0, 0)), pl.BlockSpec((n, pd), lambda i: (rb, 0)),
                            pl.BlockSpec(pool_w.shape, lambda i: (0, 0, 0)), pl.BlockSpec(scale.shape, lambda i: (0, 0))],
                  out_specs=[pl.BlockSpec((n, pd), lambda i: (0, 0)), pl.BlockSpec((ng, gd, gd), lambda i: (0, 0, 0)),
                             pl.BlockSpec((8, pd), lambda i: (0, 0))],
                  out_shape=[jax.ShapeDtypeStruct((n, pd), BF), jax.ShapeDtypeStruct((ng, gd, gd), F32),
                             jax.ShapeDtypeStruct((8, pd), F32)],
                  compiler_params=_params(("arbitrary",)))(u, dcat, pool_w, scale)


def _edge_shift(z, k):
    n = z.shape[0]
    t = lax.broadcasted_iota(jnp.int32, (n, 1), 0)
    keep = (t >= k) if k > 0 else (t < n + k)
    return jnp.where(keep, pltpu.roll(z, k % n, 0), 0.0)


def _conv_fwd(name, p3, cw, tc):
    n, cd = p3.shape[0], p3.shape[1] // 3
    nb = cd // tc

    def body(b_ref, c_ref, v_ref, w_ref, y_ref):
        z = c_ref[...] * v_ref[...]
        w = w_ref[...]
        zc = w[0:1] * _edge_shift(z, 1) + w[1:2] * z + w[2:3] * _edge_shift(z, -1)
        y_ref[...] = (b_ref[...] * zc).astype(BF)

    return _pcall(body, name=name, grid=(nb,),
                  in_specs=[pl.BlockSpec((n, tc), lambda j: (0, j)), pl.BlockSpec((n, tc), lambda j: (0, nb + j)),
                            pl.BlockSpec((n, tc), lambda j: (0, 2 * nb + j)), pl.BlockSpec((3, tc), lambda j: (0, j))],
                  out_specs=pl.BlockSpec((n, tc), lambda j: (0, j)), out_shape=jax.ShapeDtypeStruct((n, cd), BF),
                  compiler_params=_params(("arbitrary",)))(p3, p3, p3, cw)


def _conv_bwd(name, p3, cw, dy, tc):
    n, cd = dy.shape
    nb = cd // tc

    def body(b_ref, c_ref, v_ref, w_ref, dy_ref, dp_ref, dw_ref):
        cv, vv, w, dyv = c_ref[...], v_ref[...], w_ref[...], dy_ref[...]
        z = cv * vv
        zl, zr = _edge_shift(z, 1), _edge_shift(z, -1)
        zc = w[0:1] * zl + w[1:2] * z + w[2:3] * zr
        dzc = dyv * b_ref[...]
        dz = w[0:1] * _edge_shift(dzc, -1) + w[1:2] * dzc + w[2:3] * _edge_shift(dzc, 1)
        dp_ref[0] = (dyv * zc).astype(BF)
        dp_ref[1] = (dz * vv).astype(BF)
        dp_ref[2] = (dz * cv).astype(BF)
        dw_ref[...] = jnp.zeros_like(dw_ref)
        dw_ref[0:1, :] = jnp.sum(dzc * zl, axis=0, keepdims=True)
        dw_ref[1:2, :] = jnp.sum(dzc * z, axis=0, keepdims=True)
        dw_ref[2:3, :] = jnp.sum(dzc * zr, axis=0, keepdims=True)

    col = pl.BlockSpec((n, tc), lambda j: (0, j))
    return _pcall(body, name=name, grid=(nb,),
                  in_specs=[col, pl.BlockSpec((n, tc), lambda j: (0, nb + j)),
                            pl.BlockSpec((n, tc), lambda j: (0, 2 * nb + j)), pl.BlockSpec((3, tc), lambda j: (0, j)), col],
                  out_specs=[pl.BlockSpec((3, n, tc), lambda j: (0, 0, j)), pl.BlockSpec((8, tc), lambda j: (0, j))],
                  out_shape=[jax.ShapeDtypeStruct((3, n, cd), BF), jax.ShapeDtypeStruct((8, cd), F32)],
                  compiler_params=_params(("arbitrary",)))(p3, p3, p3, cw, dy)


def _conv_din(name, dp3, w_in, tm):
    _, n, cd = dp3.shape
    d = w_in.shape[0]

    def body(a_ref, w_ref, o_ref, acc_ref):
        j = pl.program_id(1)
        part = _dot(a_ref[...], w_ref[...], NT)

        @pl.when(j == 0)
        def _():
            acc_ref[...] = part

        @pl.when(j > 0)
        def _():
            acc_ref[...] += part

        @pl.when(j == 2)
        def _():
            o_ref[...] = acc_ref[...]

    return _pcall(body, name=name, grid=(n // tm, 3),
                  in_specs=[pl.BlockSpec((None, tm, cd), lambda i, j: (j, i, 0)),
                            pl.BlockSpec((d, cd), lambda i, j: (0, j))],
                  out_specs=pl.BlockSpec((tm, d), lambda i, j: (i, 0)), out_shape=jax.ShapeDtypeStruct((n, d), F32),
                  scratch_shapes=[pltpu.VMEM((tm, d), F32)],
                  compiler_params=_params(("arbitrary", "arbitrary")))(dp3, w_in)


def _conv_dw_in(name, u, dp3, tmm, tn):
    n, d = u.shape
    cd = dp3.shape[2]
    nb = cd // tn

    def body(u_ref, z_ref, o_ref):
        o_ref[...] = _dot(u_ref[...], z_ref[...], TN)

    return _pcall(body, name=name, grid=(3 * nb, d // tmm),
                  in_specs=[pl.BlockSpec((n, tmm), lambda j, mi: (0, mi)),
                            pl.BlockSpec((None, n, tn), lambda j, mi: (j // nb, 0, j % nb))],
                  out_specs=pl.BlockSpec((tmm, tn), lambda j, mi: (mi, j)),
                  out_shape=jax.ShapeDtypeStruct((d, 3 * cd), F32),
                  compiler_params=_params(("arbitrary", "arbitrary")))(u, dp3)


def _loss_head(name, h, target, gain, tm):
    d = h.shape[1]

    def fn(rows, consts, m):
        hv, tv = rows
        g = consts[0][0:1, :]
        n, r = _rms(hv)
        err = n * g - tv
        dy = err / d
        loss = 0.5 * jnp.sum(err * err) / d
        acc = {0: jnp.sum(dy * n, axis=0, keepdims=True), 1: jnp.full((1, d), loss, F32)}
        return [_rms_bwd(dy * g, n, r)], acc

    return _rows_call(name, fn, h.shape[0], tm, [h, target], [gain], None, [(d, F32)], acc_w=d)


def _adamw(name, w, g, m, v, after=None):
    shape = w.shape
    if w.ndim == 1:
        shape2 = (1,) + shape
        res = _adamw(name, *[t.reshape(shape2) for t in (w, g, m, v)], after=after)
        return [t.reshape(shape) for t in res]
    if shape[-1] % 128 and shape[-2] % 128 == 0:
        res = _adamw(name, *[jnp.swapaxes(t, -1, -2) for t in (w, g, m, v)], after=after)
        return [jnp.swapaxes(t, -1, -2) for t in res]
    lead, (r, cdim) = shape[:-2], shape[-2:]
    tr = r
    if r * cdim * 4 > (3 << 19):
        tr = _pick(r, max(8, (3 << 19) // (cdim * 4)), 8)
    c1 = 1.0 / (1.0 - ADAM_B1 ** ADAM_STEP)
    c2 = 1.0 / (1.0 - ADAM_B2 ** ADAM_STEP)
    nl = len(lead)

    def body(w_ref, g_ref, m_ref, v_ref, *rest):
        d_ref, nm_ref, nv_ref = rest[-3:]
        gv = g_ref[...]
        nm = ADAM_B1 * m_ref[...] + (1.0 - ADAM_B1) * gv
        nv = ADAM_B2 * v_ref[...] + (1.0 - ADAM_B2) * (gv * gv)
        nm_ref[...] = nm
        nv_ref[...] = nv
        d_ref[...] = -ADAM_LR * ((nm * c1) / (jnp.sqrt(nv * c2) + ADAM_EPS) + ADAM_WD * w_ref[...])

    spec = pl.BlockSpec((None,) * nl + (tr, cdim), lambda *idx: idx + (0,))
    extra = [] if after is None else [after]
    res = _pcall(body, name=name, grid=lead + (r // tr,),
                 in_specs=[spec] * 4 + [pl.BlockSpec(memory_space=pl.ANY)] * len(extra), out_specs=[spec] * 3,
                 out_shape=[jax.ShapeDtypeStruct(shape, F32)] * 3,
                 compiler_params=_params(("arbitrary",) * (nl + 1)))(w, g, m, v, *extra)
    return list(res)


def _adamw_piece(name, w, g_piece, m, v, at, outs):
    shape = w.shape
    nl = len(at)
    r, cdim = shape[-2:]
    assert shape[nl:] == g_piece.shape and len(shape) == nl + 2
    tr = _pick(r, max(8, (3 << 19) // (cdim * 4)), 8) if r * cdim * 4 > (3 << 19) else r
    c1 = 1.0 / (1.0 - ADAM_B1 ** ADAM_STEP)
    c2 = 1.0 / (1.0 - ADAM_B2 ** ADAM_STEP)
    if outs is None:
        outs = [lax.empty(shape, F32) for _ in range(4)]

    def body(w_ref, g_ref, m_ref, v_ref, *rest):
        go_ref, d_ref, nm_ref, nv_ref = rest[-4:]
        gv = g_ref[...]
        nm = ADAM_B1 * m_ref[...] + (1.0 - ADAM_B1) * gv
        nv = ADAM_B2 * v_ref[...] + (1.0 - ADAM_B2) * (gv * gv)
        go_ref[...] = gv
        nm_ref[...] = nm
        nv_ref[...] = nv
        d_ref[...] = -ADAM_LR * ((nm * c1) / (jnp.sqrt(nv * c2) + ADAM_EPS) + ADAM_WD * w_ref[...])

    full = pl.BlockSpec((None,) * nl + (tr, cdim), lambda i: tuple(at) + (i, 0))
    res = _pcall(body, name=name, grid=(r // tr,),
                 in_specs=[full, pl.BlockSpec((tr, cdim), lambda i: (i, 0)), full, full]
                 + [pl.BlockSpec(memory_space=pl.ANY)] * 4,
                 out_specs=[full] * 4, out_shape=[jax.ShapeDtypeStruct(shape, F32)] * 4,
                 input_output_aliases={4 + j: j for j in range(4)},
                 compiler_params=_params(("arbitrary",)))(w, g_piece, m, v, *outs)
    return list(res)


def _ffn_half_fwd(tag, s, gains, mod, k, wts, n_lat, tm, h_tiles, tm_big, after=None):
    wg, wu, wd, idx = wts
    u = _adaln_fwd(f"adaln_{tag}", s, gains, k, mod, k, tm, h_tiles, after)
    a, b, hid = _ffn_up(f"ffn_up_{tag}", u, wg, wu, idx, tm_big)
    s_out, o = _ffn_down(f"ffn_down_{tag}", hid, wd, idx, s, mod, k, n_lat, tm_big)
    return s_out, (s, u, a, b, hid, o)


def _ffn_half_bwd(tag, ds_out, saved, gains, mod, k, wts, big_grads, tm, h_tiles, tm_big, after=None):
    wg, wu, wd, idx = wts
    g_gate, g_up, g_down = big_grads
    s, u, a, b, hid, o = saved
    d_o, acc_g = _resid_bwd(f"resid_bwd_{tag}", ds_out, o, mod, k, 0.5, tm, h_tiles, after)
    da, db = _ffn_dhid(f"ffn_dhid_{tag}", d_o, wd, idx, a, b, tm_big)
    du = _ffn_du(f"ffn_du_{tag}", da, db, wg, wu, idx, tm_big)
    d = u.shape[1]
    g_gate, g_up = _ffn_dw_in(f"ffn_dwgu_{tag}", u, da, db, _pick(d, MM_ROWS), g_gate, g_up, idx)
    g_down = _ffn_dw_down(f"ffn_dwd_{tag}", hid, d_o, _pick(d, 512), g_down, idx)
    ds, acc_n = _adaln_bwd(f"adaln_bwd_{tag}", s, du, ds_out, gains, k, mod, k, tm, h_tiles)
    return ds, (g_gate, g_up, g_down), (acc_n[:, 0], acc_n[:, 1], acc_g[:, 0]), jnp.sum(acc_n[:, 2], axis=0)


def kernel(x, c, ctx, c_ctx, norm_g, w_mod, b_mod, ffn_w_gate, ffn_w_up, ffn_w_down, ab_w_in, pool_w, pool_scale, q_norm_g, w_uq, kv_norm_g, w_ukv, ab_w_out, conv_w_in, conv_w, conv_w_out, final_norm_g, loss_target, m_c_ctx, m_norm_g, m_w_mod, m_b_mod, m_ffn_w_gate, m_ffn_w_up, m_ffn_w_down, m_ab_w_in, m_pool_w, m_pool_scale, m_q_norm_g, m_w_uq, m_kv_norm_g, m_w_ukv, m_ab_w_out, m_conv_w_in, m_conv_w, m_conv_w_out, m_final_norm_g, v_c_ctx, v_norm_g, v_w_mod, v_b_mod, v_ffn_w_gate, v_ffn_w_up, v_ffn_w_down, v_ab_w_in, v_pool_w, v_pool_scale, v_q_norm_g, v_w_uq, v_kv_norm_g, v_w_ukv, v_ab_w_out, v_conv_w_in, v_conv_w, v_conv_w_out, v_final_norm_g):
    weights = dict(c_ctx=c_ctx, norm_g=norm_g, w_mod=w_mod, b_mod=b_mod, ffn_w_gate=ffn_w_gate, ffn_w_up=ffn_w_up,
                   ffn_w_down=ffn_w_down, ab_w_in=ab_w_in, pool_w=pool_w, pool_scale=pool_scale, q_norm_g=q_norm_g,
                   w_uq=w_uq, kv_norm_g=kv_norm_g, w_ukv=w_ukv, ab_w_out=ab_w_out, conv_w_in=conv_w_in, conv_w=conv_w,
                   conv_w_out=conv_w_out, final_norm_g=final_norm_g)
    mom_m = dict(c_ctx=m_c_ctx, norm_g=m_norm_g, w_mod=m_w_mod, b_mod=m_b_mod, ffn_w_gate=m_ffn_w_gate,
                 ffn_w_up=m_ffn_w_up, ffn_w_down=m_ffn_w_down, ab_w_in=m_ab_w_in, pool_w=m_pool_w,
                 pool_scale=m_pool_scale, q_norm_g=m_q_norm_g, w_uq=m_w_uq, kv_norm_g=m_kv_norm_g, w_ukv=m_w_ukv,
                 ab_w_out=m_ab_w_out, conv_w_in=m_conv_w_in, conv_w=m_conv_w, conv_w_out=m_conv_w_out,
                 final_norm_g=m_final_norm_g)
    mom_v = dict(c_ctx=v_c_ctx, norm_g=v_norm_g, w_mod=v_w_mod, b_mod=v_b_mod, ffn_w_gate=v_ffn_w_gate,
                 ffn_w_up=v_ffn_w_up, ffn_w_down=v_ffn_w_down, ab_w_in=v_ab_w_in, pool_w=v_pool_w,
                 pool_scale=v_pool_scale, q_norm_g=v_q_norm_g, w_uq=v_w_uq, kv_norm_g=v_kv_norm_g, w_ukv=v_w_ukv,
                 ab_w_out=v_ab_w_out, conv_w_in=v_conv_w_in, conv_w=v_conv_w, conv_w_out=v_conv_w_out,
                 final_norm_g=v_final_norm_g)

    t_len, d = x.shape[1], x.shape[2]
    g_len = ctx.shape[1]
    r_len = t_len + g_len
    fc = ffn_w_gate.shape[3]
    heads = d // 128
    pool_dim = d // 2
    q_rank, kv_rank = q_norm_g.shape[1], kv_norm_g.shape[1]
    hw = heads * HEAD_PAD
    attn_scale = 1.0 / math.sqrt(QK_NOPE + QK_ROPE)
    kvr_w = kv_rank + HEAD_PAD
    in_w = pool_dim + q_rank + kvr_w
    tm = 256 if g_len % 256 == 0 else g_len
    assert t_len % tm == 0 and g_len % tm == 0 and t_len % g_len == 0 and pool_dim % 128 == 0
    h_tiles = t_len // tm
    tm_l0 = _pick(r_len, 768, tm)
    tm_l1 = _pick(t_len, 1024, tm)

    xi, yi, ci = lax.axis_index("x"), lax.axis_index("y"), lax.axis_index("c")
    me = 4 * xi + 2 * yi + ci
    shard = 2 * xi + yi

    def halves(w):
        return w.astype(BF).reshape(2, -1, w.shape[-1])

    ffn_names = ["ffn_w_gate", "ffn_w_up", "ffn_w_down"]
    mixer_names = [["ab_w_in", "w_uq", "w_ukv", "ab_w_out"], ["conv_w_in", "conv_w_out"]]
    big_names = ffn_names + mixer_names[0] + mixer_names[1]

    def stage_names(k):
        return mixer_names[k // 3] if k % 3 == 1 else ffn_names

    def stage_halves(k):
        l, f = k // 3, (k % 3) // 2
        if k % 3 == 1:
            return [halves(weights[nm]) for nm in mixer_names[l]]
        return [halves(weights[nm][l, f]) for nm in ffn_names]

    def gather_start(k, dep):
        own = lax.optimization_barrier((tuple(stage_halves(k)), dep))[0]
        n = len(own)
        return _split_start(f"gather_start_s{k}", list(own) + _landing(N_DEV, own), n * len(CHIP_FLIPS),
                            _chips_gather_build(n))

    gather0 = gather_start(0, c)

    small = jnp.concatenate([norm_g.reshape(6, -1), conv_w[0]], axis=0)
    small = jnp.pad(small, ((0, 7), (0, 0)))
    c_row = jnp.pad(c, ((0, 7), (0, 0))) + gather0['token'][0, 0]
    small_all, c_all = _gather_all("gather_small", [small, c_row])
    small_full = small_all[::2].transpose(1, 0, 2).reshape(16, d)
    gains = [jnp.pad(small_full[3 * l:3 * l + 3], ((0, 5), (0, 0))) for l in range(2)]
    conv_w_full = small_full[6:9]
    c16 = jnp.concatenate([c_all[:, 0], c_ctx[None], jnp.zeros((7, d), F32)], axis=0)

    n_col = w_mod.shape[2]
    b_sh = lax.dynamic_slice_in_dim(b_mod, shard * n_col, n_col, axis=1)
    m_sh = [_mm(f"mod_fwd_{l}", c16, w_mod, 'nn', tm=16, tn=768, a_pre=_silu, b_lead=l,
                epi=lambda acc, i, bv: (acc + bv,), epi_args=(b_sh[l:l + 1],), epi_kinds=('n',)) for l in range(2)]
    m_all = _gather_all("gather_mod", [jnp.concatenate(m_sh, axis=0)])[0]
    m_full = m_all[::2].reshape(N_SHARD, 2, 16, n_col).transpose(1, 2, 0, 3).reshape(2, 16, N_MOD * d)
    mod_h = [jnp.pad(lax.dynamic_index_in_dim(m_full[l], me, 0, keepdims=False).reshape(N_MOD, d), ((0, 7), (0, 0)))
             for l in range(2)]
    mod_g0 = jnp.pad(m_full[0, 8].reshape(N_MOD, d), ((0, 7), (0, 0)))
    mods = [jnp.stack([mod_h[0], mod_g0]), mod_h[1][None]]

    def stage_weights(k, handle, after):
        bufs = _split_wait(f"gather_wait_s{k}", handle, after)
        n = len(bufs) // 2
        own = bufs[:n]
        fwd = _split_start(f"forward_start_s{k}", bufs[n:], n * len(CHIP_FLIPS), _sibling_forward_build(n))
        nxt = gather_start(k + 1, fwd['token']) if k + 1 < 6 else None
        landed = _split_wait(f"forward_wait_s{k}", fwd, fwd['token'])
        full = [lax.dynamic_update_slice_in_dim(z, a, 2 * shard, 0) for z, a in zip(landed, own)]
        gw = {nm: g.reshape(N_SHARD, 2 * g.shape[1], g.shape[2]) for nm, g in zip(stage_names(k), full)}
        return gw, nxt, (fwd['token'] if nxt is None else nxt['token'])

    def ffn_weights(gw):
        return gw["ffn_w_gate"].reshape(N_SHARD, 1, d, fc), gw["ffn_w_up"].reshape(N_SHARD, 1, d, fc), \
            gw["ffn_w_down"], 0

    gw_s0, gather1, tok0 = stage_weights(0, gather0, m_all)
    ffn_w = [[ffn_weights(gw_s0), None], [None, None]]

    s0 = jnp.concatenate([x[0], ctx[0]], axis=0)
    s1, sav_f00 = _ffn_half_fwd("l0a", s0, gains[0], mods[0], 0, ffn_w[0][0], t_len, tm, h_tiles, tm_l0, tok0)

    gw_s1, gather2, tok1 = stage_weights(1, gather1, s1)
    w_out_full = gw_s1["ab_w_out"].reshape(-1, d)
    w_uq_full = gw_s1["w_uq"].reshape(q_rank, heads * (QK_NOPE + QK_ROPE))
    w_ukv_full = gw_s1["w_ukv"].transpose(1, 0, 2).reshape(kv_rank, heads * (QK_NOPE + V_HEAD))
    w_in_full = gw_s1["ab_w_in"].transpose(1, 0, 2).reshape(d, -1)

    wq_p = jnp.pad(w_uq_full.reshape(q_rank, heads, QK_NOPE + QK_ROPE),
                   ((0, 0), (0, 0), (0, HEAD_PAD - QK_NOPE - QK_ROPE))).reshape(q_rank, hw)
    ukv3 = w_ukv_full.reshape(kv_rank, heads, QK_NOPE + V_HEAD)
    wk_top = jnp.pad(ukv3[..., :QK_NOPE], ((0, 0), (0, 0), (0, HEAD_PAD - QK_NOPE))).reshape(kv_rank, hw)
    wv_top = jnp.pad(ukv3[..., QK_NOPE:], ((0, 0), (0, 0), (0, HEAD_PAD - V_HEAD))).reshape(kv_rank, hw)
    src_row = lax.broadcasted_iota(jnp.int32, (HEAD_PAD, hw), 0)
    dst_lane = lax.broadcasted_iota(jnp.int32, (HEAD_PAD, hw), 1) % HEAD_PAD
    spread = ((src_row < QK_ROPE) & (dst_lane == src_row + QK_NOPE)).astype(BF)
    wk_ext = jnp.concatenate([wk_top, spread], axis=0)
    wv_ext = jnp.concatenate([wv_top, jnp.zeros((HEAD_PAD, hw), BF)], axis=0)
    w_in_pool = w_in_full[:, :pool_dim]
    w_in_q = w_in_full[:, pool_dim:pool_dim + q_rank]
    w_in_kvr = jnp.pad(w_in_full[:, pool_dim + q_rank:], ((0, 0), (0, HEAD_PAD - QK_ROPE)))
    w_out_attn = jnp.pad(w_out_full[pool_dim:].reshape(heads, V_HEAD, d),
                         ((0, 0), (0, HEAD_PAD - V_HEAD), (0, 0))).reshape(hw, d)
    w_out_p = jnp.concatenate([w_out_full[:pool_dim], w_out_attn], axis=0)

    u_mix = _adaln_fwd("adaln_l0m", s1, gains[0], 1, mods[0], 1, tm, h_tiles, tok1)
    p_pool = _mm("in_pool", u_mix, w_in_pool, 'nn', tm=tm_l0, tn=pool_dim)
    p_q = _mm("in_q", u_mix, w_in_q, 'nn', tm=tm_l0, tn=q_rank)
    p_kvr = _mm("in_kvr", u_mix, w_in_kvr, 'nn', tm=tm_l0, tn=kvr_w)
    qg = jnp.pad(q_norm_g, ((0, 7), (0, 0)))
    kvg = jnp.pad(kv_norm_g, ((0, 7), (0, 0)))
    tq_c, tq_s = _rope_tables(t_len, g_len, QK_NOPE)
    tk_c, tk_s = _rope_tables(t_len, g_len, 0)

    def qn_fn(rows, consts, m):
        n, _ = _rms(rows[0])
        return [n * consts[0][0:1, :]], {}

    qn = _rows_call("q_norm", qn_fn, r_len, tm, [p_q], [qg], None, [(q_rank, BF)])[0]
    q_r = _mm("q_up", qn, wq_p, 'nn', tm=tm_l0, tn=hw, out_dtypes=(BF,),
              epi=lambda acc, i, ct, st: (_rope(acc, ct, st),), epi_args=(tq_c, tq_s), epi_kinds=('mt', 'mt'))

    def kvn_fn(rows, consts, m):
        pv, ct, st = rows
        n, _ = _rms(pv[:, :kv_rank])
        return [jnp.concatenate([n * consts[0][0:1, :], _rope(pv[:, kv_rank:], ct, st)], axis=1)], {}

    kvn = _rows_call("kv_norm", kvn_fn, r_len, tm, [p_kvr, tk_c, tk_s], [kvg], None, [(kvr_w, BF)])[0]
    k_p = _mm("k_up", kvn, wk_ext, 'nn', tm=tm_l0, tn=hw, out_dtypes=(BF,))
    v_p = _mm("v_up", kvn, wv_ext, 'nn', tm=tm_l0, tn=hw, out_dtypes=(BF,))
    tq_h = _pick(t_len, 512, tm)
    o_h, lse_h = _attn_fwd("attn_h", q_r, k_p, v_p, t_len, 0, r_len, 0, heads, tm, attn_scale)
    o_g, lse_g = _attn_fwd("attn_g", q_r, k_p, v_p, g_len, t_len, g_len, t_len // g_len, heads, tm, attn_scale)
    y_h = _pool_fwd("pool_h", p_pool[:t_len], pool_w[0], pool_scale)
    y_g = _pool_fwd("pool_g", p_pool[t_len:], pool_w[0], pool_scale)
    cat = jnp.concatenate([jnp.concatenate([y_h, y_g], axis=0), jnp.concatenate([o_h, o_g], axis=0)], axis=1)

    def resid_epi(k3, n_lat, tmr):
        def epi(acc, i, sv, mv):
            return sv + _row_gate(mv, k3, i, tmr, n_lat) * acc, acc
        return epi

    s2, o_mix0 = _mm("mix_out_l0", cat, w_out_p, 'nn', tm=tm_l0, tn=d, out_dtypes=(F32, F32),
                     epi=resid_epi(5, t_len, tm_l0), epi_args=(s1, mods[0]), epi_kinds=('mn', 'w'))
    gw_s2, gather3, tok2 = stage_weights(2, gather2, s2)
    ffn_w[0][1] = ffn_weights(gw_s2)
    s3, sav_f01 = _ffn_half_fwd("l0b", s2, gains[0], mods[0], 2, ffn_w[0][1], t_len, tm, h_tiles, tm_l0, tok2)

    gw_s3, gather4, tok3 = stage_weights(3, gather3, s3)
    ffn_w[1][0] = ffn_weights(gw_s3)
    tml = 256 if t_len % 256 == 0 else tm
    h3 = s3[:t_len]
    h4, sav_f10 = _ffn_half_fwd("l1a", h3, gains[1], mods[1], 0, ffn_w[1][0], t_len, tml, None, tm_l1, tok3)
    gw_s4, gather5, tok4 = stage_weights(4, gather4, h4)
    cw_out_full = gw_s4["conv_w_out"].reshape(-1, d)
    cw_in_full = gw_s4["conv_w_in"].transpose(1, 0, 2).reshape(d, -1)
    u_cv = _adaln_fwd("adaln_l1m", h4, gains[1], 1, mods[1], 1, tml, None, tok4)
    p3 = _mm("conv_in", u_cv, cw_in_full, 'nn', tm=tm_l1, tn=512)
    cwp = conv_w_full
    tc = _pick(d, 256)
    y_cv = _conv_fwd("conv_fwd", p3, cwp, tc)
    h5, o_mix1 = _mm("mix_out_l1", y_cv, cw_out_full, 'nn', tm=tm_l1, tn=d, out_dtypes=(F32, F32),
                     epi=resid_epi(5, t_len, tm_l1), epi_args=(h4, mods[1]), epi_kinds=('mn', 'w'))
    gw_s5, _, tok5 = stage_weights(5, gather5, h5)
    ffn_w[1][1] = ffn_weights(gw_s5)
    h6, sav_f11 = _ffn_half_fwd("l1b", h5, gains[1], mods[1], 2, ffn_w[1][1], t_len, tml, None, tm_l1, tok5)

    fg = jnp.pad(final_norm_g[None], ((0, 7), (0, 0)))
    dh6, acc_loss = _loss_head("loss_head", h6, loss_target[0], fg, tml)
    d_final_g = acc_loss[0, 0]

    dgain = [[None] * 3 for _ in range(2)]
    dmod = [[None] * N_MOD for _ in range(2)]

    def put(l, k, triple):
        dmod[l][3 * k], dmod[l][3 * k + 1], dmod[l][3 * k + 2] = triple

    def empty_ffn_grads():
        return (lax.empty((N_SHARD, d, fc), BF), lax.empty((N_SHARD, d, fc), BF), lax.empty((N_SHARD, fc, d), BF))

    def by_shard_rows(g):
        return g.reshape(N_SHARD, -1, g.shape[-1])

    def by_shard_cols(g):
        return g.reshape(g.shape[0], N_SHARD, -1).transpose(1, 0, 2)

    def pair_start(k, big):
        send = [b.astype(BF).reshape(N_DEV, b.shape[1] // 2, b.shape[2]) for b in big]
        n = len(send)
        return _split_start(f"grads_pair_start_s{k}", send + _landing(N_SHARD, send), n * N_SHARD,
                            _sibling_halves_build(n))

    def chips_start(k, handle, after):
        bufs = _split_wait(f"grads_pair_wait_s{k}", handle, after)
        n = len(bufs) // 2
        pre = [_add_halves(f"grads_add_s{k}_{nm}", s, z) for nm, s, z in zip(stage_names(k), bufs[:n], bufs[n:])]
        return _split_start(f"grads_start_s{k}", pre + _landing(N_SHARD, pre), n * len(CHIP_FLIPS),
                            _chips_scatter_build(n))

    def landed_sums(k, handle, after):
        bufs = _split_wait(f"grads_wait_s{k}", handle, after)
        n = len(bufs) // 2
        landed = [lax.dynamic_update_slice_in_dim(z, lax.dynamic_slice_in_dim(p, shard, 1, 0), shard, 0)
                  for p, z in zip(bufs[:n], bufs[n:])]
        return [_sum_lead(f"sum_grads_s{k}_{nm}", z) for nm, z in zip(stage_names(k), landed)]

    pair, scatter, sums = [None] * 6, [None] * 6, [None] * 6
    grads, upd = {}, {}
    ffn_out = {nm: None for nm in ffn_names}

    def lanes_last(nm, t):
        shp = weights[nm].shape
        return jnp.swapaxes(t, -1, -2) if shp[-1] % 128 and shp[-2] % 128 == 0 else t

    def finish_stage(k, halves):
        n = len(halves)
        lands = [pltpu.with_memory_space_constraint(lax.empty(h.shape, h.dtype), pltpu.HBM) for h in halves]
        swap = _split_start(f"swap_start_s{k}", list(halves) + lands, n, _sibling_whole_build(n))
        both = _split_wait(f"swap_wait_s{k}", swap, swap['token'])
        for nm, a, g in zip(stage_names(k), both[:n], both[n:]):
            piece = jnp.where(ci == 0, jnp.concatenate([a, g], axis=0), jnp.concatenate([g, a], axis=0))
            if k % 3 == 1:
                grads[nm] = piece.reshape(weights[nm].shape)
                upd[nm] = _adamw(f"adamw_{nm}", weights[nm], grads[nm], mom_m[nm], mom_v[nm])
            else:
                ffn_out[nm] = _adamw_piece(f"adamw_{nm}_s{k}", lanes_last(nm, weights[nm]), lanes_last(nm, piece),
                                           lanes_last(nm, mom_m[nm]), lanes_last(nm, mom_v[nm]),
                                           (k // 3, (k % 3) // 2), ffn_out[nm])
    dh5, ffn_g, tr, dgain[1][2] = _ffn_half_bwd("l1b", dh6, sav_f11, gains[1], mods[1], 2, ffn_w[1][1],
                                                empty_ffn_grads(), tml, None, tm_l1)
    put(1, 2, tr)
    pair[5] = pair_start(5, list(ffn_g))
    d_o1, acc_g1 = _resid_bwd("resid_bwd_l1m", dh5, o_mix1, mods[1], 1, 1.0, tml, None, pair[5]['token'])
    dy_cv = _mm("mix_out_l1_dx", d_o1, cw_out_full, 'nt', tm=tm_l1, tn=d)
    d_cw_out = _mm("mix_out_l1_dw", y_cv, d_o1, 'tn', tm=512, tn=512)
    dp3, d_cw = _conv_bwd("conv_bwd", p3, cwp, dy_cv, tc)
    du_cv = _conv_din("conv_in_dx", dp3, cw_in_full, tm_l1)
    d_cw_in = _conv_dw_in("conv_in_dw", u_cv, dp3, _pick(d, MM_ROWS), _pick(d, 512))
    dh4, acc_n1 = _adaln_bwd("adaln_bwd_l1m", h4, du_cv, dh5, gains[1], 1, mods[1], 1, tml, None)
    put(1, 1, (acc_n1[:, 0], acc_n1[:, 1], acc_g1[:, 0]))
    dgain[1][1] = acc_n1[0, 2]
    scatter[5] = chips_start(5, pair[5], dh4)
    pair[4] = pair_start(4, [by_shard_cols(d_cw_in), by_shard_rows(d_cw_out)])
    dh3, ffn_g, tr, dgain[1][0] = _ffn_half_bwd("l1a", dh4, sav_f10, gains[1], mods[1], 0, ffn_w[1][0],
                                                empty_ffn_grads(), tml, None, tm_l1,
                                                scatter[5]['token'] + pair[4]['token'])
    put(1, 0, tr)
    finish_stage(5, landed_sums(5, scatter[5], dh3))
    scatter[4] = chips_start(4, pair[4], dh3)
    pair[3] = pair_start(3, list(ffn_g))

    ds3 = jnp.concatenate([dh3, jnp.zeros((g_len, d), F32)], axis=0) \
        + (scatter[4]['token'][0, 0] + pair[3]['token'][0, 0])
    ds2, ffn_g, tr, dgain[0][2] = _ffn_half_bwd("l0b", ds3, sav_f01, gains[0], mods[0], 2, ffn_w[0][1],
                                                empty_ffn_grads(), tm, h_tiles, tm_l0)
    put(0, 2, tr)
    finish_stage(4, landed_sums(4, scatter[4], ds2))
    scatter[3] = chips_start(3, pair[3], ds2)
    pair[2] = pair_start(2, list(ffn_g))
    d_o0, acc_g0 = _resid_bwd("resid_bwd_l0m", ds2, o_mix0, mods[0], 1, 1.0, tm, h_tiles,
                              scatter[3]['token'] + pair[2]['token'])
    dcat = _mm("mix_out_l0_dx", d_o0, w_out_p, 'nt', tm=tm_l0, tn=pool_dim + hw)
    d_w_out_p = _mm("mix_out_l0_dw", cat, d_o0, 'tn', tm=512, tn=512)
    col_blk = pool_dim // HEAD_PAD
    dq_h, dk_h, dv_h = _attn_bwd("attn_bwd_h", q_r, k_p, v_p, cat, dcat, lse_h, t_len, 0, r_len, 0, heads, tq_h,
                                 attn_scale, col_blk)
    dq_g, dk_all, dv_all = _attn_bwd("attn_bwd_g", q_r, k_p, v_p, cat, dcat, lse_g, g_len, t_len, g_len,
                                     t_len // g_len, heads, tm, attn_scale, col_blk, onto=(dk_h, dv_h))
    dq_all = jnp.concatenate([dq_h, dq_g], axis=0)
    dkvn = _mm("k_up_dx", dk_all, wk_ext, 'nt', tm=tm_l0, tn=kvr_w)
    dkvn = _mm("v_up_dx", dv_all, wv_ext, 'nt', tm=tm_l0, tn=kvr_w, epi=lambda acc, i, prev: (acc + prev,),
               epi_args=(dkvn,), epi_kinds=('mn',))
    d_wk_ext = _mm("k_up_dw", kvn, dk_all, 'tn', tm=kvr_w, tn=512)
    d_wv_ext = _mm("v_up_dw", kvn, dv_all, 'tn', tm=kvr_w, tn=512)

    def kvn_bwd_fn(rows, consts, m):
        pv, dv_, ct, st = rows
        g = consts[0][0:1, :]
        n, r = _rms(pv[:, :kv_rank])
        dyn = dv_[:, :kv_rank]
        dckv = _rms_bwd(dyn * g, n, r)
        dkr = _rope_t(dv_[:, kv_rank:], ct, st)
        return [jnp.concatenate([dckv, dkr], axis=1)], {0: jnp.sum(dyn * n, axis=0, keepdims=True)}

    dp_kvr, acc_kvg = _rows_call("kv_norm_bwd", kvn_bwd_fn, r_len, tm, [p_kvr, dkvn, tk_c, tk_s], [kvg], None,
                                 [(kvr_w, BF)], acc_w=kv_rank)

    def qrope_bwd_fn(rows, consts, m):
        return [_rope_t(rows[0], rows[1], rows[2])], {}

    dq_pad = _rows_call("q_rope_bwd", qrope_bwd_fn, r_len, tm, [dq_all, tq_c, tq_s], [], None, [(hw, BF)])[0]
    dqn = _mm("q_up_dx", dq_pad, wq_p, 'nt', tm=tm_l0, tn=q_rank)
    d_wq_p = _mm("q_up_dw", qn, dq_pad, 'tn', tm=512, tn=512)

    def qn_bwd_fn(rows, consts, m):
        pv, dv_ = rows
        g = consts[0][0:1, :]
        n, r = _rms(pv)
        return [_rms_bwd(dv_ * g, n, r)], {0: jnp.sum(dv_ * n, axis=0, keepdims=True)}

    dp_q, acc_qg = _rows_call("q_norm_bwd", qn_bwd_fn, r_len, tm, [p_q, dqn], [qg], None, [(q_rank, BF)],
                              acc_w=q_rank)
    dpu_h, dpw_h, dps_h = _pool_bwd("pool_bwd_h", p_pool[:t_len], dcat, pool_w[0], pool_scale, 0)
    dpu_g, dpw_g, dps_g = _pool_bwd("pool_bwd_g", p_pool[t_len:], dcat, pool_w[0], pool_scale, t_len)
    dp_pool = jnp.concatenate([dpu_h, dpu_g], axis=0)
    add_prev = lambda acc, i, prev: (acc + prev,)
    du_mix = _mm("in_pool_dx", dp_pool, w_in_pool, 'nt', tm=tm_l0, tn=d)
    du_mix = _mm("in_q_dx", dp_q, w_in_q, 'nt', tm=tm_l0, tn=d, epi=add_prev, epi_args=(du_mix,), epi_kinds=('mn',))
    du_mix = _mm("in_kvr_dx", dp_kvr, w_in_kvr, 'nt', tm=tm_l0, tn=d, epi=add_prev, epi_args=(du_mix,), epi_kinds=('mn',))
    d_w_in = jnp.concatenate([
        _mm("in_pool_dw", u_mix, dp_pool, 'tn', tm=512, tn=pool_dim),
        _mm("in_q_dw", u_mix, dp_q, 'tn', tm=512, tn=q_rank),
        _mm("in_kvr_dw", u_mix, dp_kvr, 'tn', tm=512, tn=kvr_w)[:, :kv_rank + QK_ROPE]], axis=1)
    ds1, acc_n0 = _adaln_bwd("adaln_bwd_l0m", s1, du_mix, ds2, gains[0], 1, mods[0], 1, tm, h_tiles)
    put(0, 1, (acc_n0[:, 0], acc_n0[:, 1], acc_g0[:, 0]))
    dgain[0][1] = jnp.sum(acc_n0[:, 2], axis=0)
    d_w_uq = d_wq_p.reshape(q_rank, heads, HEAD_PAD)[..., :QK_NOPE + QK_ROPE].reshape(q_rank, -1)
    d_w_ukv = jnp.concatenate([d_wk_ext[:kv_rank].reshape(kv_rank, heads, HEAD_PAD)[..., :QK_NOPE],
                               d_wv_ext[:kv_rank].reshape(kv_rank, heads, HEAD_PAD)[..., :V_HEAD]],
                              axis=-1).reshape(kv_rank, -1)
    d_w_out = jnp.concatenate([d_w_out_p[:pool_dim],
                               d_w_out_p[pool_dim:].reshape(heads, HEAD_PAD, d)[:, :V_HEAD].reshape(-1, d)], axis=0)
    finish_stage(3, landed_sums(3, scatter[3], ds1))
    scatter[2] = chips_start(2, pair[2], ds1)
    pair[1] = pair_start(1, [by_shard_cols(d_w_in), by_shard_rows(d_w_uq), by_shard_cols(d_w_ukv),
                             by_shard_rows(d_w_out)])
    ds0, ffn_g, tr, dgain[0][0] = _ffn_half_bwd("l0a", ds1, sav_f00, gains[0], mods[0], 0, ffn_w[0][0],
                                                empty_ffn_grads(), tm, h_tiles, tm_l0,
                                                scatter[2]['token'] + pair[1]['token'])
    put(0, 0, tr)
    grad_x = ds0[:t_len][None]
    finish_stage(2, landed_sums(2, scatter[2], ds0))
    pair[0] = pair_start(0, list(ffn_g))

    dmh = jnp.stack([jnp.stack([dmod[l][k][0] for k in range(N_MOD)]) for l in range(2)])
    dmg0 = jnp.stack([dmod[0][k][1] for k in range(N_MOD)])
    dg_rows = jnp.stack([dgain[l][k] for l in range(2) for k in range(3)])
    pieces = [dmh.reshape(2 * N_MOD, d), dmg0, dg_rows, d_cw[:3], d_final_g[None],
              (dpw_h + dpw_g).reshape(-1, d), jnp.pad((dps_h + dps_g)[0], (0, d - pool_dim))[None],
              jnp.pad(acc_qg[0, 0], (0, d - q_rank))[None], jnp.pad(acc_kvg[0, 0], (0, d - kv_rank))[None],
              acc_loss[0, 1][None]]
    n_piece = [p.shape[0] for p in pieces]
    pieces = [jnp.pad(p, ((0, (-p.shape[0]) % 8), (0, 0))) for p in pieces]
    small_g = jnp.concatenate(pieces, axis=0) + pair[0]['token'][0, 0]
    sg_all = _gather_all("gather_small_grads", [small_g])[0]
    sg_sum = _sum_lead("sum_small_grads", sg_all)
    scatter[1] = chips_start(1, pair[1], sg_sum)
    offs = [0]
    for p in pieces:
        offs.append(offs[-1] + p.shape[0])
    part = lambda j: sg_sum[offs[j]:offs[j] + n_piece[j]]
    sum_dmh, sum_dmg0, g_norm_full, g_conv_w_full = part(0).reshape(2, N_MOD * d), part(1).reshape(N_MOD * d), part(2), part(3)
    g_final = part(4)[0]
    loss = part(9)[0, 0]
    g_pool_w = part(5).reshape(pool_w.shape)
    g_pool_scale = part(6)[:, :pool_dim]
    g_q_norm = part(7)[:, :q_rank]
    g_kv_norm = part(8)[:, :kv_rank]
    col0 = shard * (d // N_SHARD)
    g_norm_g = lax.dynamic_slice_in_dim(g_norm_full.reshape(2, 3, d), col0, d // N_SHARD, axis=2)
    g_conv_w = lax.dynamic_slice_in_dim(g_conv_w_full, col0, d // N_SHARD, axis=1)[None]
    g_b_mod = _sum_lead("sum_b_mod", jnp.stack([sum_dmh, jnp.stack([sum_dmg0, jnp.zeros_like(sum_dmg0)])]))

    dm16 = []
    for l in range(2):
        per_dev = sg_all[:, l * N_MOD:(l + 1) * N_MOD].reshape(N_DEV, N_MOD * d)
        row8 = (sum_dmg0 if l == 0 else jnp.zeros_like(sum_dmg0)) + scatter[1]['token'][0, 0]
        full = jnp.concatenate([per_dev, row8[None], jnp.zeros((7, N_MOD * d), F32)], axis=0)
        dm16.append(lax.dynamic_slice_in_dim(full, shard * n_col, n_col, axis=1))
    g_w_mod = jnp.stack([_mm(f"mod_dw_{l}", c16, dm16[l], 'tn', tm=512, tn=768, a_pre=_silu) for l in range(2)])
    dc16 = _mm("mod_dx", dm16[0], w_mod, 'nt', tm=16, tn=512, b_lead=0, epi=lambda acc, i, cv: (acc * _dsilu(cv),),
               epi_args=(c16,), epi_kinds=('mn',))
    dc_all = _gather_all("gather_dc", [dc16])[0]
    g_c_ctx = _sum_lead("sum_dc", dc_all[::2])[8]

    grads.update(c_ctx=g_c_ctx, norm_g=g_norm_g, w_mod=g_w_mod, b_mod=g_b_mod, pool_w=g_pool_w,
                 pool_scale=g_pool_scale, q_norm_g=g_q_norm, kv_norm_g=g_kv_norm, conv_w=g_conv_w, final_norm_g=g_final)
    names = list(weights)

    scatter[0] = chips_start(0, pair[0], g_c_ctx)
    upd.update({n: _adamw(f"adamw_{n}", weights[n], grads[n].reshape(weights[n].shape), mom_m[n], mom_v[n],
                          scatter[0]['token']) for n in names if n not in big_names})
    finish_stage(1, landed_sums(1, scatter[1], upd["w_mod"][0]))
    finish_stage(0, landed_sums(0, scatter[0], upd[mixer_names[0][-1]][0]))
    for nm in ffn_names:
        done = [lanes_last(nm, t) for t in ffn_out[nm]]
        grads[nm], upd[nm] = done[0], done[1:]
    return (loss, grad_x, *[grads[n].reshape(weights[n].shape) for n in names], *[upd[n][0] for n in names],
            *[upd[n][1] for n in names], *[upd[n][2] for n in names])
```

```python
import functools
import math

import jax
import jax.numpy as jnp
from jax import lax
from jax.experimental import pallas as pl
from jax.experimental.pallas import tpu as pltpu

F32 = jnp.float32
BF = jnp.bfloat16
MESH = pl.DeviceIdType.MESH

N_DEV = 8
N_SHARD = 4
RMS_EPS = 1e-6
N_MOD = 9
POOL_WINDOWS = (2, 4, 8, 16)
QK_NOPE = 64
QK_ROPE = 32
V_HEAD = 64
HEAD_PAD = 128
GRID_W = 64
ROPE_THETA = 10000.0
POOL_PAD = 16
ADAM_LR, ADAM_B1, ADAM_B2, ADAM_EPS, ADAM_WD, ADAM_STEP = 0.001, 0.9, 0.999, 1e-08, 0.01, 10
VMEM_LIMIT = 56 * 1024 * 1024
MM_ROWS = 1024


def _pcall(body, **kw):
    return pl.pallas_call(body, **kw)


def _params(sem=None):
    return pltpu.CompilerParams(dimension_semantics=sem, vmem_limit_bytes=VMEM_LIMIT)


def _pick(n, pref, mult=128):
    best = None
    d = mult
    while d <= min(n, pref):
        if n % d == 0:
            best = d
        d += mult
    return best if best is not None else n


def _silu(z):
    return z * jax.nn.sigmoid(z)


def _dsilu(z):
    s = jax.nn.sigmoid(z)
    return s * (1.0 + z * (1.0 - s))


def _dot(a, b, dims):
    return lax.dot_general(a.astype(BF), b.astype(BF), (dims, ((), ())), preferred_element_type=F32)


NN = ((1,), (0,))
NT = ((1,), (1,))
TN = ((0,), (0,))


ALL_FLIPS = [(kx, ky, kc) for kx in (0, 1) for ky in (0, 1) for kc in (0, 1) if (kx, ky, kc) != (0, 0, 0)]
CHIP_FLIPS = [(1, 0, 0), (0, 1, 0), (1, 1, 0)]
SIBLING = (0, 0, 1)
COMM_SPLIT = 8
SPLIT_MIN_ROWS = 256


def _exchange(name, arrays, plan, lead, whole_src, split=COMM_SPLIT):
    n = len(arrays)
    blk_shapes = [tuple(a.shape) if whole_src else tuple(a.shape[1:]) for a in arrays]
    splits = []
    for shp in blk_shapes:
        s = 1
        while s * 2 <= split and shp[0] % (s * 2) == 0 and (shp[0] // (s * 2)) % 16 == 0 \
                and shp[0] // (s * 2) >= SPLIT_MIN_ROWS:
            s *= 2
        splits.append(s)
    items = plan(0, 0, 0)
    n_items = len(items)
    remote_ids = [k for k, it in enumerate(items) if it[0] is not None]
    local_ids = [k for k, it in enumerate(items) if it[0] is None]
    slots = [(a, s) for s in range(max(splits)) for a in range(n) if s < splits[a]]
    n_slot = len(slots)

    def body(*refs):
        ins, outs = refs[:n], refs[n:2 * n]
        send_sems, recv_sems, loc_sems = refs[2 * n:]
        x, y, c = lax.axis_index("x"), lax.axis_index("y"), lax.axis_index("c")
        plan_here = plan(x, y, c)

        def rows(ref, a, s):
            rc = blk_shapes[a][0] // splits[a]
            return ref.at[pl.ds(s * rc, rc)]

        def make(si, k):
            a, s = slots[si]
            flip, src, dst, _ = plan_here[k]
            base = outs[a] if src[0] == 'out' else ins[a]
            src_ref = rows(base if src[1] is None else base.at[src[1]], a, s)
            dst_ref = rows(outs[a].at[dst], a, s)
            if flip is None:
                return pltpu.make_async_copy(src_ref, dst_ref, loc_sems.at[si * max(1, len(local_ids)) + local_ids.index(k)])
            peer = (1 - x if flip[0] else x, 1 - y if flip[1] else y, 1 - c if flip[2] else c)
            sem = si * len(remote_ids) + remote_ids.index(k)
            return pltpu.make_async_remote_copy(src_ref=src_ref, dst_ref=dst_ref, send_sem=send_sems.at[sem],
                                                recv_sem=recv_sems.at[sem], device_id=peer, device_id_type=MESH)

        copies = {}
        for si in range(n_slot):
            for k in range(n_items):
                if plan_here[k][3] is None:
                    copies[si, k] = make(si, k)
                    copies[si, k].start()
        arrived = set()
        for si in range(n_slot):
            for k in range(n_items):
                after = plan_here[k][3]
                if after is not None:
                    if (si, after) not in arrived:
                        copies[si, after].wait_recv()
                        arrived.add((si, after))
                    copies[si, k] = make(si, k)
                    copies[si, k].start()
        for (si, k), cp in copies.items():
            if plan_here[k][0] is None:
                cp.wait()
            else:
                cp.wait_send()
                if (si, k) not in arrived:
                    cp.wait_recv()

    any_spec = pl.BlockSpec(memory_space=pl.ANY)
    n_rem = max(1, n_slot * len(remote_ids))
    outs = _pcall(
        body, name=name,
        out_shape=[jax.ShapeDtypeStruct((lead,) + s, a.dtype) for s, a in zip(blk_shapes, arrays)],
        in_specs=[any_spec] * n, out_specs=[any_spec] * n,
        scratch_shapes=[pltpu.SemaphoreType.DMA((n_rem,)), pltpu.SemaphoreType.DMA((n_rem,)),
                        pltpu.SemaphoreType.DMA((max(1, n_slot * len(local_ids)),))],
    )(*arrays)
    return list(outs)


def _place(x, y, c):
    return 4 * x + 2 * y + c


def _flip(v, f):
    return 1 - v if f else v


def _gather_all(name, arrays, two_level=False):
    def plan(x, y, c):
        me = _place(x, y, c)
        if not two_level:
            return [(None, ('in', None), me, None)] + [(f, ('in', None), me, None) for f in ALL_FLIPS]
        items = [(None, ('in', None), me, None), (SIBLING, ('in', None), me, None)]
        items += [(f, ('in', None), me, None) for f in CHIP_FLIPS]
        for j, f in enumerate(CHIP_FLIPS):
            got = _place(_flip(x, f[0]), _flip(y, f[1]), c)
            items.append((SIBLING, ('out', got), got, 2 + j))
        return items
    return _exchange(name, arrays, plan, N_DEV, True, split=1)


HBM_SPEC = pl.BlockSpec(memory_space=pltpu.HBM)
SEM_SPEC = pl.BlockSpec(memory_space=pltpu.SEMAPHORE)
SIDE_EFFECT = pltpu.SideEffectType.DATAFLOW_SIDE_EFFECTING


def _split_start(name, bufs, n_copies, build):
    n = len(bufs)

    def body(*refs):
        for cp in build(refs[:n], refs[n], refs[n + 1]):
            cp.start()
        token = refs[-1]
        token[...] = jnp.zeros_like(token)

    res = _pcall(
        body, name=name,
        out_shape=(pltpu.SemaphoreType.DMA((n_copies,)), pltpu.SemaphoreType.DMA((n_copies,)),
                   *[pltpu.HBM(b.shape, b.dtype) for b in bufs], jax.ShapeDtypeStruct((8, 128), F32)),
        in_specs=[HBM_SPEC] * n,
        out_specs=(SEM_SPEC, SEM_SPEC, *[HBM_SPEC] * n, pl.BlockSpec(memory_space=pltpu.VMEM)),
        input_output_aliases={i: 2 + i for i in range(n)},
        compiler_params=pltpu.CompilerParams(has_side_effects=SIDE_EFFECT),
    )(*[pltpu.with_memory_space_constraint(b, pltpu.HBM) for b in bufs])
    return dict(send=res[0], recv=res[1], bufs=list(res[2:2 + n]), token=res[-1], build=build)


def _split_wait(name, handle, after):
    n = len(handle['bufs'])
    build = handle['build']

    def body(*refs):
        for cp in build(refs[:n], refs[n], refs[n + 1]):
            cp.wait_send()
            cp.wait_recv()

    res = _pcall(
        body, name=name, out_shape=tuple(pltpu.HBM(b.shape, b.dtype) for b in handle['bufs']),
        in_specs=[HBM_SPEC] * n + [SEM_SPEC, SEM_SPEC, pl.BlockSpec(memory_space=pl.ANY)],
        out_specs=tuple([HBM_SPEC] * n), input_output_aliases={i: i for i in range(n)},
        compiler_params=pltpu.CompilerParams(has_side_effects=SIDE_EFFECT),
    )(*handle['bufs'], handle['send'], handle['recv'], after)
    return list(res)


def _landing(lead, arrays):
    return [pltpu.with_memory_space_constraint(lax.empty((lead,) + tuple(a.shape[1:]), a.dtype), pltpu.HBM)
            for a in arrays]


def _copy_list(n, per_array, make):
    def build(refs, send_sems, recv_sems):
        copies = []
        for a in range(n):
            for j in range(per_array):
                src, dst, peer = make(refs, a, j)
                k = a * per_array + j
                copies.append(pltpu.make_async_remote_copy(src_ref=src, dst_ref=dst, send_sem=send_sems.at[k],
                                                           recv_sem=recv_sems.at[k], device_id=peer,
                                                           device_id_type=MESH))
        return copies
    return build


def _mesh_place():
    x, y, c = lax.axis_index("x"), lax.axis_index("y"), lax.axis_index("c")
    return x, y, c, 2 * x + y


def _chips_gather_build(n):
    def make(refs, a, j):
        x, y, c, chip = _mesh_place()
        px, py = _flip(x, CHIP_FLIPS[j][0]), _flip(y, CHIP_FLIPS[j][1])
        return refs[a].at[c], refs[n + a].at[2 * chip + c], (px, py, c)
    return _copy_list(n, len(CHIP_FLIPS), make)


def _chips_scatter_build(n):
    def make(refs, a, j):
        x, y, c, chip = _mesh_place()
        px, py = _flip(x, CHIP_FLIPS[j][0]), _flip(y, CHIP_FLIPS[j][1])
        return refs[a].at[2 * px + py], refs[n + a].at[chip], (px, py, c)
    return _copy_list(n, len(CHIP_FLIPS), make)


def _sibling_forward_build(n):
    def make(refs, a, j):
        x, y, c, _ = _mesh_place()
        blk = 2 * (2 * _flip(x, CHIP_FLIPS[j][0]) + _flip(y, CHIP_FLIPS[j][1])) + c
        return refs[a].at[blk], refs[a].at[blk], (x, y, 1 - c)
    return _copy_list(n, len(CHIP_FLIPS), make)


def _sibling_halves_build(n):
    def make(refs, a, j):
        x, y, c, _ = _mesh_place()
        return refs[a].at[2 * j + 1 - c], refs[n + a].at[j], (x, y, 1 - c)
    return _copy_list(n, N_SHARD, make)


def _sibling_whole_build(n):
    def make(refs, a, j):
        x, y, c, _ = _mesh_place()
        return refs[a], refs[n + a], (x, y, 1 - c)
    return _copy_list(n, 1, make)


def _add_halves(name, send, land):
    _, r, cdim = send.shape
    tr = _pick(r, max(16, (1 << 20) // (cdim * 2)), 16)

    def body(c_ref, own_ref, got_ref, o_ref):
        o_ref[...] = (own_ref[...].astype(F32) + got_ref[...].astype(F32)).astype(BF)

    grid_spec = pltpu.PrefetchScalarGridSpec(
        num_scalar_prefetch=1, grid=(N_SHARD, r // tr),
        in_specs=[pl.BlockSpec((None, tr, cdim), lambda sh, i, cr: (2 * sh + cr[0], i, 0)),
                  pl.BlockSpec((None, tr, cdim), lambda sh, i, cr: (sh, i, 0))],
        out_specs=pl.BlockSpec((None, tr, cdim), lambda sh, i, cr: (sh, i, 0)))
    core = lax.axis_index("c").astype(jnp.int32).reshape(1)
    return _pcall(body, name=name, grid_spec=grid_spec, out_shape=jax.ShapeDtypeStruct((N_SHARD, r, cdim), BF),
                  compiler_params=_params(("arbitrary", "arbitrary")))(core, send, land)


def _sum_lead(name, arr, out_dtype=F32):
    n, r, cdim = arr.shape
    tr = r
    limit = (4 << 20) // (n * cdim * arr.dtype.itemsize)
    if r > limit:
        tr = _pick(r, max(limit, 16), 16)

    def body(x_ref, o_ref):
        acc = x_ref[0].astype(F32)
        for d in range(1, n):
            acc = acc + x_ref[d].astype(F32)
        o_ref[...] = acc.astype(out_dtype)

    return _pcall(body, name=name, grid=(r // tr,),
                  in_specs=[pl.BlockSpec((n, tr, cdim), lambda i: (0, i, 0))],
                  out_specs=pl.BlockSpec((tr, cdim), lambda i: (i, 0)),
                  out_shape=jax.ShapeDtypeStruct((r, cdim), out_dtype),
                  compiler_params=_params(("arbitrary",)))(arr)


def _rows_call(name, fn, n_rows, tm, rows, consts, mod, outs, acc_w=None, h_tiles=None):
    nt = n_rows // tm
    ht = nt if h_tiles is None else h_tiles
    ng = 1 if mod is None else mod.shape[0]
    n_r, n_c, n_o = len(rows), len(consts), len(outs)
    has_mod = mod is not None

    def body(*refs):
        i = pl.program_id(0)
        first = (i % ht) == 0
        row_refs, const_refs = refs[:n_r], refs[n_r:n_r + n_c]
        p = n_r + n_c
        mod_tile = refs[p][...] if has_mod else None
        p += int(has_mod)
        out_refs = refs[p:p + n_o]
        o, acc = fn([r[...] for r in row_refs], [r[...] for r in const_refs], mod_tile)
        for r, v in zip(out_refs, o):
            r[...] = v.astype(r.dtype)
        if acc_w is not None:
            acc_ref = refs[p + n_o]

            @pl.when(first)
            def _():
                acc_ref[...] = jnp.zeros_like(acc_ref)

            for k, v in acc.items():
                acc_ref[k:k + 1, :] += v

    in_specs = [pl.BlockSpec((tm, r.shape[1]), lambda i: (i, 0)) for r in rows]
    in_specs += [pl.BlockSpec(cst.shape, lambda i, nd=cst.ndim: (0,) * nd) for cst in consts]
    args = list(rows) + list(consts)
    if has_mod:
        in_specs.append(pl.BlockSpec((None,) + mod.shape[1:], lambda i: (i // ht, 0, 0)))
        args.append(mod)
    out_shape = [jax.ShapeDtypeStruct((n_rows, w), dt) for w, dt in outs]
    out_specs = [pl.BlockSpec((tm, w), lambda i: (i, 0)) for w, _ in outs]
    if acc_w is not None:
        out_shape.append(jax.ShapeDtypeStruct((ng, 8, acc_w), F32))
        out_specs.append(pl.BlockSpec((None, 8, acc_w), lambda i: (i // ht, 0, 0)))
    res = _pcall(body, name=name, grid=(nt,), in_specs=in_specs, out_specs=out_specs, out_shape=out_shape,
                 compiler_params=_params(("arbitrary",)))(*args)
    return list(res)


def _rms(s):
    r = lax.rsqrt(jnp.mean(s * s, axis=1, keepdims=True) + RMS_EPS)
    return s * r, r


def _rms_bwd(dn, n, r):
    return r * (dn - n * jnp.mean(dn * n, axis=1, keepdims=True))


def _adaln_fwd(name, s, gains, gain_row, mod, k, tm, h_tiles, after=None):
    def fn(rows, consts, m):
        n, _ = _rms(rows[0])
        y = n * consts[0][gain_row:gain_row + 1, :]
        return [y * (1.0 + m[3 * k + 1:3 * k + 2, :]) + m[3 * k:3 * k + 1, :]], {}

    d = s.shape[1]
    consts = [gains] if after is None else [gains, after]
    return _rows_call(name, fn, s.shape[0], tm, [s], consts, mod, [(d, BF)], h_tiles=h_tiles)[0]


def _adaln_bwd(name, s, du, ds_res, gains, gain_row, mod, k, tm, h_tiles):
    def fn(rows, consts, m):
        sv, duv, res = rows
        gain = consts[0][gain_row:gain_row + 1, :]
        n, r = _rms(sv)
        y = n * gain
        dy = duv * (1.0 + m[3 * k + 1:3 * k + 2, :])
        acc = {0: jnp.sum(duv, axis=0, keepdims=True), 1: jnp.sum(duv * y, axis=0, keepdims=True),
               2: jnp.sum(dy * n, axis=0, keepdims=True)}
        return [_rms_bwd(dy * gain, n, r) + res], acc

    d = s.shape[1]
    return _rows_call(name, fn, s.shape[0], tm, [s, du, ds_res], [gains], mod, [(d, F32)], acc_w=d, h_tiles=h_tiles)


def _resid_bwd(name, ds_out, o, mod, k, cst, tm, h_tiles, after=None):
    def fn(rows, consts, m):
        dsv, ov = rows
        gate = m[3 * k + 2:3 * k + 3, :]
        return [cst * gate * dsv], {0: jnp.sum(cst * ov * dsv, axis=0, keepdims=True)}

    d = o.shape[1]
    consts = [] if after is None else [after]
    return _rows_call(name, fn, o.shape[0], tm, [ds_out, o], consts, mod, [(d, BF)], acc_w=d, h_tiles=h_tiles)


def _mm(name, a, b, mode, tm=256, tn=512, out_dtypes=(F32,), epi=None, epi_args=(), epi_kinds=(), a_pre=None,
        b_lead=None):
    bshape = b.shape if b_lead is None else b.shape[1:]
    if mode == 'nn':
        (m, kd), nd = a.shape, bshape[1]
    elif mode == 'nt':
        (m, kd), nd = a.shape, bshape[0]
    else:
        (kd, m), nd = a.shape, bshape[1]
    tm = _pick(m, tm, 16) if m % tm else tm
    tn = _pick(nd, tn, 128) if nd % tn else tn
    dims = {'nn': NN, 'nt': NT, 'tn': TN}[mode]
    n_e, n_o = len(epi_args), len(out_dtypes)

    def body(*refs):
        i = pl.program_id(1)
        av = refs[0][...]
        if a_pre is not None:
            av = a_pre(av)
        acc = _dot(av, refs[1][...], dims)
        res = (acc,) if epi is None else epi(acc, i, *[r[...] for r in refs[2:2 + n_e]])
        for r, v in zip(refs[2 + n_e:], res):
            r[...] = v.astype(r.dtype)

    if mode == 'nn':
        specs = [pl.BlockSpec((tm, kd), lambda j, i: (i, 0)), pl.BlockSpec((kd, tn), lambda j, i: (0, j))]
    elif mode == 'nt':
        specs = [pl.BlockSpec((tm, kd), lambda j, i: (i, 0)), pl.BlockSpec((tn, kd), lambda j, i: (j, 0))]
    else:
        specs = [pl.BlockSpec((kd, tm), lambda j, i: (0, i)), pl.BlockSpec((kd, tn), lambda j, i: (0, j))]
    if b_lead is not None:
        shape2, at2 = specs[1].block_shape, specs[1].index_map
        specs[1] = pl.BlockSpec((None,) + tuple(shape2), lambda j, i: (b_lead,) + tuple(at2(j, i)))
    for arr, kind in zip(epi_args, epi_kinds):
        if kind == 'mn':
            specs.append(pl.BlockSpec((tm, tn), lambda j, i: (i, j)))
        elif kind == 'n':
            specs.append(pl.BlockSpec((1, tn), lambda j, i: (0, j)))
        elif kind == 'mt':
            specs.append(pl.BlockSpec((tm, arr.shape[1]), lambda j, i: (i, 0)))
        else:
            specs.append(pl.BlockSpec(arr.shape, lambda j, i, nd_=arr.ndim: (0,) * nd_))
    res = _pcall(body, name=name, grid=(nd // tn, m // tm), in_specs=specs,
                 out_specs=[pl.BlockSpec((tm, tn), lambda j, i: (i, j))] * n_o,
                 out_shape=[jax.ShapeDtypeStruct((m, nd), dt) for dt in out_dtypes],
                 compiler_params=_params(("arbitrary", "arbitrary")))(a, b, *epi_args)
    return res[0] if n_o == 1 else list(res)


def _row_gate(mod, k3, i, tm, n_lat):
    g0 = mod[0, k3:k3 + 1, :]
    if mod.shape[0] == 1:
        return g0
    rid = i * tm + lax.broadcasted_iota(jnp.int32, (tm, 1), 0)
    return jnp.where(rid < n_lat, g0, mod[1, k3:k3 + 1, :])


def _ffn_up(name, u, wg, wu, base, tm):
    r, d = u.shape
    nch, _, _, fc = wg.shape

    def body(u_ref, wg_ref, wu_ref, a_ref, b_ref, h_ref):
        uv = u_ref[...]
        a = _dot(uv, wg_ref[...], NN)
        b = _dot(uv, wu_ref[...], NN)
        a_ref[...] = a.astype(BF)
        b_ref[...] = b.astype(BF)
        h_ref[...] = (_silu(a) * b).astype(BF)

    chunk = pl.BlockSpec((None, tm, fc), lambda j, i: (j, i, 0))
    return _pcall(body, name=name, grid=(nch, r // tm),
                  in_specs=[pl.BlockSpec((tm, d), lambda j, i: (i, 0)),
                            pl.BlockSpec((None, None, d, fc), lambda j, i: (j, base, 0, 0)),
                            pl.BlockSpec((None, None, d, fc), lambda j, i: (j, base, 0, 0))],
                  out_specs=[chunk] * 3, out_shape=[jax.ShapeDtypeStruct((nch, r, fc), BF)] * 3,
                  compiler_params=_params(("arbitrary", "arbitrary")))(u, wg, wu)


def _ffn_down(name, hid, wd, wd_blk, s, mod, k, n_lat, tm):
    nch, r, fc = hid.shape
    d = wd.shape[2]

    def body(h_ref, w_ref, s_ref, m_ref, so_ref, o_ref):
        i = pl.program_id(0)
        o = _dot(h_ref[0], w_ref[0], NN)
        for j in range(1, nch):
            o = o + _dot(h_ref[j], w_ref[j], NN)
        o_ref[...] = o
        so_ref[...] = s_ref[...] + 0.5 * _row_gate(m_ref[...], 3 * k + 2, i, tm, n_lat) * o

    row = pl.BlockSpec((tm, d), lambda i: (i, 0))
    return _pcall(body, name=name, grid=(r // tm,),
                  in_specs=[pl.BlockSpec((nch, tm, fc), lambda i: (0, i, 0)),
                            pl.BlockSpec((nch, fc, d), lambda i: (0, wd_blk, 0)), row,
                            pl.BlockSpec(mod.shape, lambda i: (0, 0, 0))],
                  out_specs=[row, row], out_shape=[jax.ShapeDtypeStruct((r, d), F32)] * 2,
                  compiler_params=_params(("arbitrary",)))(hid, wd, s, mod)


def _ffn_dhid(name, d_o, wd, wd_blk, a, b, tm):
    r, d = d_o.shape
    nch, _, fc = a.shape

    def body(g_ref, w_ref, a_ref, b_ref, da_ref, db_ref):
        dh = _dot(g_ref[...], w_ref[...], NT)
        av, bv = a_ref[...].astype(F32), b_ref[...].astype(F32)
        da_ref[...] = (dh * bv * _dsilu(av)).astype(BF)
        db_ref[...] = (dh * _silu(av)).astype(BF)

    chunk = pl.BlockSpec((None, tm, fc), lambda j, i: (j, i, 0))
    return _pcall(body, name=name, grid=(nch, r // tm),
                  in_specs=[pl.BlockSpec((tm, d), lambda j, i: (i, 0)),
                            pl.BlockSpec((None, fc, d), lambda j, i: (j, wd_blk, 0)), chunk, chunk],
                  out_specs=[chunk] * 2, out_shape=[jax.ShapeDtypeStruct((nch, r, fc), BF)] * 2,
                  compiler_params=_params(("arbitrary", "arbitrary")))(d_o, wd, a, b)


def _ffn_du(name, da, db, wg, wu, base, tm):
    nch, r, fc = da.shape
    d = wg.shape[2]

    def body(da_ref, db_ref, wg_ref, wu_ref, o_ref):
        acc = _dot(da_ref[0], wg_ref[0], NT) + _dot(db_ref[0], wu_ref[0], NT)
        for j in range(1, nch):
            acc = acc + _dot(da_ref[j], wg_ref[j], NT) + _dot(db_ref[j], wu_ref[j], NT)
        o_ref[...] = acc

    chunks = pl.BlockSpec((nch, tm, fc), lambda i: (0, i, 0))
    held = pl.BlockSpec((nch, None, d, fc), lambda i: (0, base, 0, 0), pipeline_mode=pl.Buffered(1))
    return _pcall(body, name=name, grid=(r // tm,), in_specs=[chunks, chunks, held, held],
                  out_specs=pl.BlockSpec((tm, d), lambda i: (i, 0)), out_shape=jax.ShapeDtypeStruct((r, d), F32),
                  compiler_params=_params(("arbitrary",)))(da, db, wg, wu)


def _ffn_dw_in(name, u, da, db, tmm, g_gate, g_up, idx):
    r, d = u.shape
    nch, _, fc = da.shape
    nb = d // tmm

    def body(u_ref, a_ref, b_ref, gg_ref, gu_ref, og_ref, ou_ref):
        ut = u_ref[...].T
        og_ref[...] = _dot(ut, a_ref[...], NN).astype(og_ref.dtype)
        ou_ref[...] = _dot(ut, b_ref[...], NN).astype(ou_ref.dtype)

    chunk = pl.BlockSpec((None, r, fc), lambda j, mi: (j, 0, 0))
    out = pl.BlockSpec((None, tmm, fc), lambda j, mi: (j, idx * nb + mi, 0))
    return _pcall(body, name=name, grid=(nch, nb),
                  in_specs=[pl.BlockSpec((r, tmm), lambda j, mi: (0, mi)), chunk, chunk,
                            pl.BlockSpec(memory_space=pl.ANY), pl.BlockSpec(memory_space=pl.ANY)],
                  out_specs=[out, out],
                  out_shape=[jax.ShapeDtypeStruct(g_gate.shape, g_gate.dtype),
                             jax.ShapeDtypeStruct(g_up.shape, g_up.dtype)],
                  input_output_aliases={3: 0, 4: 1},
                  compiler_params=_params(("arbitrary", "arbitrary")))(u, da, db, g_gate, g_up)


def _ffn_dw_down(name, hid, d_o, tn, grads, idx):
    nch, r, fc = hid.shape
    d = d_o.shape[1]

    def body(h_ref, g_ref, acc_ref, o_ref):
        o_ref[...] = _dot(h_ref[...], g_ref[...], TN).astype(o_ref.dtype)

    return _pcall(body, name=name, grid=(nch, d // tn),
                  in_specs=[pl.BlockSpec((None, r, fc), lambda j, ni: (j, 0, 0)),
                            pl.BlockSpec((r, tn), lambda j, ni: (0, ni)),
                            pl.BlockSpec(memory_space=pl.ANY)],
                  out_specs=pl.BlockSpec((None, fc, tn), lambda j, ni: (j, idx, ni)),
                  out_shape=jax.ShapeDtypeStruct(grads.shape, grads.dtype), input_output_aliases={2: 0},
                  compiler_params=_params(("arbitrary", "arbitrary")))(hid, d_o, grads)


def _partner(x):
    n = x.shape[1]
    lane = lax.broadcasted_iota(jnp.int32, x.shape, 1)
    return jnp.where((lane & 15) < 8, pltpu.roll(x, n - 8, 1), pltpu.roll(x, 8, 1))


def _rope(x, ct, st):
    reps = x.shape[1] // ct.shape[1]
    if reps > 1:
        ct, st = jnp.tile(ct, (1, reps)), jnp.tile(st, (1, reps))
    return x * ct + _partner(x) * st


def _rope_t(dy, ct, st):
    reps = dy.shape[1] // ct.shape[1]
    if reps > 1:
        ct, st = jnp.tile(ct, (1, reps)), jnp.tile(st, (1, reps))
    return dy * ct + _partner(dy * st)


def _rope_tables(t_len, g_len, lane0):
    half = QK_ROPE // 4
    pos = jnp.arange(t_len)
    row = (pos // GRID_W).astype(F32)
    col = (pos % GRID_W).astype(F32)
    freqs = jnp.power(ROPE_THETA, -jnp.arange(0, QK_ROPE // 2, 2, dtype=F32) / (QK_ROPE // 2))
    ang_r, ang_c = row[:, None] * freqs, col[:, None] * freqs
    cs = jnp.concatenate([jnp.cos(ang_r)] * 2 + [jnp.cos(ang_c)] * 2, axis=1)
    sn = jnp.concatenate([-jnp.sin(ang_r), jnp.sin(ang_r), -jnp.sin(ang_c), jnp.sin(ang_c)], axis=1)
    assert cs.shape[1] == 4 * half == QK_ROPE
    def place(tab, fill):
        rest = HEAD_PAD - lane0 - QK_ROPE
        rows = jnp.concatenate([jnp.full((t_len, lane0), fill, F32), tab, jnp.full((t_len, rest), fill, F32)], axis=1)
        return jnp.concatenate([rows, jnp.full((g_len, HEAD_PAD), fill, F32)], axis=0)

    return place(cs, 1.0), place(sn, 0.0)


def _attn_fwd(name, q, kp, vp, n_q, q_off, n_k, k_blk, heads, tq, scale):
    qb = q_off // tq
    per = 2 if heads % 2 == 0 else 1
    wide = per * HEAD_PAD

    def body(q_ref, k_ref, v_ref, o_ref, l_ref):
        for e in range(per):
            sl = slice(e * HEAD_PAD, (e + 1) * HEAD_PAD)
            s = _dot(q_ref[:, sl], k_ref[:, sl], NT) * scale
            m = jnp.max(s, axis=1, keepdims=True)
            p = jnp.exp(s - m)
            l = jnp.sum(p, axis=1, keepdims=True)
            o_ref[:, sl] = (_dot(p, v_ref[:, sl], NN) / l).astype(BF)
            l_ref[:, sl] = jnp.broadcast_to(m + jnp.log(l), (tq, HEAD_PAD))

    hw = heads * HEAD_PAD
    blk = pl.BlockSpec((tq, wide), lambda h, i: (i, h))
    kv = pl.BlockSpec((n_k, wide), lambda h, i: (k_blk, h))
    return _pcall(body, name=name, grid=(heads // per, n_q // tq),
                  in_specs=[pl.BlockSpec((tq, wide), lambda h, i: (i + qb, h)), kv, kv],
                  out_specs=[blk, blk],
                  out_shape=[jax.ShapeDtypeStruct((n_q, hw), BF), jax.ShapeDtypeStruct((n_q, hw), F32)],
                  compiler_params=_params(("arbitrary", "arbitrary")))(q, kp, vp)


def _attn_bwd(name, q, kp, vp, cat, dcat, lse, n_q, q_off, n_k, k_blk, heads, tq, scale, col_blk, onto=None):
    qb = q_off // tq

    per = 1
    wide = per * HEAD_PAD

    def body(q_ref, k_ref, v_ref, o_ref, do_ref, l_ref, *rest):
        dq_ref, dk_ref, dv_ref = rest[-3:]
        i = pl.program_id(1)
        for e in range(per):
            sl = slice(e * HEAD_PAD, (e + 1) * HEAD_PAD)
            qv, kv_, vv = q_ref[:, sl], k_ref[:, sl], v_ref[:, sl]
            dov = do_ref[:, sl]
            s = _dot(qv, kv_, NT) * scale
            p = jnp.exp(s - l_ref[:, e * HEAD_PAD:e * HEAD_PAD + 1])
            dp = _dot(dov, vv, NT)
            delta = jnp.sum(dov * o_ref[:, sl].astype(F32), axis=1, keepdims=True)
            ds = (p * (dp - delta) * scale).astype(BF)
            dq_ref[:, sl] = _dot(ds, kv_, NN)
            dk = _dot(ds, qv, TN)
            dv = _dot(p, dov, TN)

            @pl.when(i == 0)
            def _():
                if onto is None:
                    dk_ref[:, sl] = dk
                    dv_ref[:, sl] = dv
                else:
                    dk_ref[:, sl] = rest[0][:, sl] + dk
                    dv_ref[:, sl] = rest[1][:, sl] + dv

            @pl.when(i > 0)
            def _():
                dk_ref[:, sl] += dk
                dv_ref[:, sl] += dv

    hw = heads * HEAD_PAD
    heads = heads // per
    col_blk = col_blk // per
    qspec = pl.BlockSpec((tq, wide), lambda h, i: (i + qb, h))
    cspec = pl.BlockSpec((tq, wide), lambda h, i: (i + qb, col_blk + h))
    kv = pl.BlockSpec((n_k, wide), lambda h, i: (k_blk, h))
    blk = pl.BlockSpec((tq, wide), lambda h, i: (i, h))
    if onto is None:
        acc = pl.BlockSpec((n_k, wide), lambda h, i: (0, h))
        return _pcall(body, name=name, grid=(heads, n_q // tq),
                      in_specs=[qspec, kv, kv, cspec, cspec, blk], out_specs=[blk, acc, acc],
                      out_shape=[jax.ShapeDtypeStruct((n_q, hw), F32), jax.ShapeDtypeStruct((n_k, hw), F32),
                                 jax.ShapeDtypeStruct((n_k, hw), F32)],
                      compiler_params=_params(("arbitrary", "arbitrary")))(q, kp, vp, cat, dcat, lse)
    return _pcall(body, name=name, grid=(heads, n_q // tq),
                  in_specs=[qspec, kv, kv, cspec, cspec, blk, kv, kv], out_specs=[blk, kv, kv],
                  out_shape=[jax.ShapeDtypeStruct((n_q, hw), F32)] + [jax.ShapeDtypeStruct(t.shape, F32) for t in onto],
                  input_output_aliases={6: 1, 7: 2},
                  compiler_params=_params(("arbitrary", "arbitrary")))(q, kp, vp, cat, dcat, lse, *onto)


def _shift(x, k):
    return pltpu.roll(x, k % x.shape[0], 0)


def _window_sum(v, w, mirrored):
    n, gd = v.shape
    pad = jnp.zeros((POOL_PAD, gd), F32)
    e = jnp.concatenate([pad, v, pad], axis=0)
    acc = e + _shift(e, -1 if mirrored else 1)
    step = 1
    while 2 * step < w:
        acc = _shift(acc, step) + _shift(acc, -step)
        step *= 2
    return acc[POOL_PAD:POOL_PAD + n]


def _window_count(n, w):
    t = lax.broadcasted_iota(jnp.int32, (n, 1), 0)
    lo = jnp.maximum(t - w // 2, 0)
    hi = jnp.minimum(t + (w - w // 2 - 1), n - 1)
    return (hi - lo + 1).astype(F32)


def _pool_fwd(name, u, pool_w, scale):
    n, pd = u.shape
    ng = len(POOL_WINDOWS)
    gd = pd // ng

    def body(u_ref, w_ref, s_ref, y_ref):
        for g, w in enumerate(POOL_WINDOWS):
            sl = slice(g * gd, (g + 1) * gd)
            ug = u_ref[:, sl]
            p = _window_sum(ug, w, False) / _window_count(n, w) - ug
            y_ref[:, sl] = (_dot(p, w_ref[g], NN) * s_ref[:, sl]).astype(BF)

    return _pcall(body, name=name, out_shape=jax.ShapeDtypeStruct((n, pd), BF),
                  compiler_params=_params())(u, pool_w, scale)


def _pool_bwd(name, u, dcat, pool_w, scale, row_off):
    n, pd = u.shape
    ng = len(POOL_WINDOWS)
    gd = pd // ng

    def body(u_ref, dy_ref, w_ref, s_ref, du_ref, dw_ref, ds_ref):
        ds_ref[...] = jnp.zeros_like(ds_ref)
        for g, w in enumerate(POOL_WINDOWS):
            sl = slice(g * gd, (g + 1) * gd)
            ug, dy, wg = u_ref[:, sl], dy_ref[:, sl], w_ref[g]
            cnt = _window_count(n, w)
            p = _window_sum(ug, w, False) / cnt - ug
            ds_ref[0:1, sl] = jnp.sum(dy * _dot(p, wg, NN), axis=0, keepdims=True)
            dys = dy * s_ref[:, sl]
            dw_ref[g] = _dot(p, dys, TN)
            dp = _dot(dys, wg, NT)
            du_ref[:, sl] = (_window_sum(dp / cnt, w, True) - dp).astype(BF)

    rb = row_off // n
    return _pcall(body, name=name, grid=(1,),
                  in_specs=[pl.BlockSpec((n, pd), lambda i: (0, 0)), pl.BlockSpec((n, pd), lambda i: (rb, 0)),
                            pl.BlockSpec(pool_w.shape, lambda i: (0, 0, 0)), pl.BlockSpec(scale.shape, lambda i: (0, 0))],
                  out_specs=[pl.BlockSpec((n, pd), lambda i: (0, 0)), pl.BlockSpec((ng, gd, gd), lambda i: (0, 0, 0)),
                             pl.BlockSpec((8, pd), lambda i: (0, 0))],
                  out_shape=[jax.ShapeDtypeStruct((n, pd), BF), jax.ShapeDtypeStruct((ng, gd, gd), F32),
                             jax.ShapeDtypeStruct((8, pd), F32)],
                  compiler_params=_params(("arbitrary",)))(u, dcat, pool_w, scale)


def _edge_shift(z, k):
    n = z.shape[0]
    t = lax.broadcasted_iota(jnp.int32, (n, 1), 0)
    keep = (t >= k) if k > 0 else (t < n + k)
    return jnp.where(keep, pltpu.roll(z, k % n, 0), 0.0)


def _conv_fwd(name, p3, cw, tc):
    n, cd = p3.shape[0], p3.shape[1] // 3
    nb = cd // tc

    def body(b_ref, c_ref, v_ref, w_ref, y_ref):
        z = c_ref[...] * v_ref[...]
        w = w_ref[...]
        zc = w[0:1] * _edge_shift(z, 1) + w[1:2] * z + w[2:3] * _edge_shift(z, -1)
        y_ref[...] = (b_ref[...] * zc).astype(BF)

    return _pcall(body, name=name, grid=(nb,),
                  in_specs=[pl.BlockSpec((n, tc), lambda j: (0, j)), pl.BlockSpec((n, tc), lambda j: (0, nb + j)),
                            pl.BlockSpec((n, tc), lambda j: (0, 2 * nb + j)), pl.BlockSpec((3, tc), lambda j: (0, j))],
                  out_specs=pl.BlockSpec((n, tc), lambda j: (0, j)), out_shape=jax.ShapeDtypeStruct((n, cd), BF),
                  compiler_params=_params(("arbitrary",)))(p3, p3, p3, cw)


def _conv_bwd(name, p3, cw, dy, tc):
    n, cd = dy.shape
    nb = cd // tc

    def body(b_ref, c_ref, v_ref, w_ref, dy_ref, dp_ref, dw_ref):
        cv, vv, w, dyv = c_ref[...], v_ref[...], w_ref[...], dy_ref[...]
        z = cv * vv
        zl, zr = _edge_shift(z, 1), _edge_shift(z, -1)
        zc = w[0:1] * zl + w[1:2] * z + w[2:3] * zr
        dzc = dyv * b_ref[...]
        dz = w[0:1] * _edge_shift(dzc, -1) + w[1:2] * dzc + w[2:3] * _edge_shift(dzc, 1)
        dp_ref[0] = (dyv * zc).astype(BF)
        dp_ref[1] = (dz * vv).astype(BF)
        dp_ref[2] = (dz * cv).astype(BF)
        dw_ref[...] = jnp.zeros_like(dw_ref)
        dw_ref[0:1, :] = jnp.sum(dzc * zl, axis=0, keepdims=True)
        dw_ref[1:2, :] = jnp.sum(dzc * z, axis=0, keepdims=True)
        dw_ref[2:3, :] = jnp.sum(dzc * zr, axis=0, keepdims=True)

    col = pl.BlockSpec((n, tc), lambda j: (0, j))
    return _pcall(body, name=name, grid=(nb,),
                  in_specs=[col, pl.BlockSpec((n, tc), lambda j: (0, nb + j)),
                            pl.BlockSpec((n, tc), lambda j: (0, 2 * nb + j)), pl.BlockSpec((3, tc), lambda j: (0, j)), col],
                  out_specs=[pl.BlockSpec((3, n, tc), lambda j: (0, 0, j)), pl.BlockSpec((8, tc), lambda j: (0, j))],
                  out_shape=[jax.ShapeDtypeStruct((3, n, cd), BF), jax.ShapeDtypeStruct((8, cd), F32)],
                  compiler_params=_params(("arbitrary",)))(p3, p3, p3, cw, dy)


def _conv_din(name, dp3, w_in, tm):
    _, n, cd = dp3.shape
    d = w_in.shape[0]

    def body(a_ref, w_ref, o_ref, acc_ref):
        j = pl.program_id(1)
        part = _dot(a_ref[...], w_ref[...], NT)

        @pl.when(j == 0)
        def _():
            acc_ref[...] = part

        @pl.when(j > 0)
        def _():
            acc_ref[...] += part

        @pl.when(j == 2)
        def _():
            o_ref[...] = acc_ref[...]

    return _pcall(body, name=name, grid=(n // tm, 3),
                  in_specs=[pl.BlockSpec((None, tm, cd), lambda i, j: (j, i, 0)),
                            pl.BlockSpec((d, cd), lambda i, j: (0, j))],
                  out_specs=pl.BlockSpec((tm, d), lambda i, j: (i, 0)), out_shape=jax.ShapeDtypeStruct((n, d), F32),
                  scratch_shapes=[pltpu.VMEM((tm, d), F32)],
                  compiler_params=_params(("arbitrary", "arbitrary")))(dp3, w_in)


def _conv_dw_in(name, u, dp3, tmm, tn):
    n, d = u.shape
    cd = dp3.shape[2]
    nb = cd // tn

    def body(u_ref, z_ref, o_ref):
        o_ref[...] = _dot(u_ref[...], z_ref[...], TN)

    return _pcall(body, name=name, grid=(3 * nb, d // tmm),
                  in_specs=[pl.BlockSpec((n, tmm), lambda j, mi: (0, mi)),
                            pl.BlockSpec((None, n, tn), lambda j, mi: (j // nb, 0, j % nb))],
                  out_specs=pl.BlockSpec((tmm, tn), lambda j, mi: (mi, j)),
                  out_shape=jax.ShapeDtypeStruct((d, 3 * cd), F32),
                  compiler_params=_params(("arbitrary", "arbitrary")))(u, dp3)


def _loss_head(name, h, target, gain, tm):
    d = h.shape[1]

    def fn(rows, consts, m):
        hv, tv = rows
        g = consts[0][0:1, :]
        n, r = _rms(hv)
        err = n * g - tv
        dy = err / d
        loss = 0.5 * jnp.sum(err * err) / d
        acc = {0: jnp.sum(dy * n, axis=0, keepdims=True), 1: jnp.full((1, d), loss, F32)}
        return [_rms_bwd(dy * g, n, r)], acc

    return _rows_call(name, fn, h.shape[0], tm, [h, target], [gain], None, [(d, F32)], acc_w=d)


def _adamw(name, w, g, m, v, after=None):
    shape = w.shape
    if w.ndim == 1:
        shape2 = (1,) + shape
        res = _adamw(name, *[t.reshape(shape2) for t in (w, g, m, v)], after=after)
        return [t.reshape(shape) for t in res]
    if shape[-1] % 128 and shape[-2] % 128 == 0:
        res = _adamw(name, *[jnp.swapaxes(t, -1, -2) for t in (w, g, m, v)], after=after)
        return [jnp.swapaxes(t, -1, -2) for t in res]
    lead, (r, cdim) = shape[:-2], shape[-2:]
    tr = r
    if r * cdim * 4 > (3 << 19):
        tr = _pick(r, max(8, (3 << 19) // (cdim * 4)), 8)
    c1 = 1.0 / (1.0 - ADAM_B1 ** ADAM_STEP)
    c2 = 1.0 / (1.0 - ADAM_B2 ** ADAM_STEP)
    nl = len(lead)

    def body(w_ref, g_ref, m_ref, v_ref, *rest):
        d_ref, nm_ref, nv_ref = rest[-3:]
        gv = g_ref[...]
        nm = ADAM_B1 * m_ref[...] + (1.0 - ADAM_B1) * gv
        nv = ADAM_B2 * v_ref[...] + (1.0 - ADAM_B2) * (gv * gv)
        nm_ref[...] = nm
        nv_ref[...] = nv
        d_ref[...] = -ADAM_LR * ((nm * c1) / (jnp.sqrt(nv * c2) + ADAM_EPS) + ADAM_WD * w_ref[...])

    spec = pl.BlockSpec((None,) * nl + (tr, cdim), lambda *idx: idx + (0,))
    extra = [] if after is None else [after]
    res = _pcall(body, name=name, grid=lead + (r // tr,),
                 in_specs=[spec] * 4 + [pl.BlockSpec(memory_space=pl.ANY)] * len(extra), out_specs=[spec] * 3,
                 out_shape=[jax.ShapeDtypeStruct(shape, F32)] * 3,
                 compiler_params=_params(("arbitrary",) * (nl + 1)))(w, g, m, v, *extra)
    return list(res)


def _adamw_piece(name, w, g_piece, m, v, at, outs):
    shape = w.shape
    nl = len(at)
    r, cdim = shape[-2:]
    assert shape[nl:] == g_piece.shape and len(shape) == nl + 2
    tr = _pick(r, max(8, (3 << 19) // (cdim * 4)), 8) if r * cdim * 4 > (3 << 19) else r
    c1 = 1.0 / (1.0 - ADAM_B1 ** ADAM_STEP)
    c2 = 1.0 / (1.0 - ADAM_B2 ** ADAM_STEP)
    if outs is None:
        outs = [lax.empty(shape, F32) for _ in range(4)]

    def body(w_ref, g_ref, m_ref, v_ref, *rest):
        go_ref, d_ref, nm_ref, nv_ref = rest[-4:]
        gv = g_ref[...]
        nm = ADAM_B1 * m_ref[...] + (1.0 - ADAM_B1) * gv
        nv = ADAM_B2 * v_ref[...] + (1.0 - ADAM_B2) * (gv * gv)
        go_ref[...] = gv
        nm_ref[...] = nm
        nv_ref[...] = nv
        d_ref[...] = -ADAM_LR * ((nm * c1) / (jnp.sqrt(nv * c2) + ADAM_EPS) + ADAM_WD * w_ref[...])

    full = pl.BlockSpec((None,) * nl + (tr, cdim), lambda i: tuple(at) + (i, 0))
    res = _pcall(body, name=name, grid=(r // tr,),
                 in_specs=[full, pl.BlockSpec((tr, cdim), lambda i: (i, 0)), full, full]
                 + [pl.BlockSpec(memory_space=pl.ANY)] * 4,
                 out_specs=[full] * 4, out_shape=[jax.ShapeDtypeStruct(shape, F32)] * 4,
                 input_output_aliases={4 + j: j for j in range(4)},
                 compiler_params=_params(("arbitrary",)))(w, g_piece, m, v, *outs)
    return list(res)


def _ffn_half_fwd(tag, s, gains, mod, k, wts, n_lat, tm, h_tiles, tm_big, after=None):
    wg, wu, wd, idx = wts
    u = _adaln_fwd(f"adaln_{tag}", s, gains, k, mod, k, tm, h_tiles, after)
    a, b, hid = _ffn_up(f"ffn_up_{tag}", u, wg, wu, idx, tm_big)
    s_out, o = _ffn_down(f"ffn_down_{tag}", hid, wd, idx, s, mod, k, n_lat, tm_big)
    return s_out, (s, u, a, b, hid, o)


def _ffn_half_bwd(tag, ds_out, saved, gains, mod, k, wts, big_grads, tm, h_tiles, tm_big, after=None):
    wg, wu, wd, idx = wts
    g_gate, g_up, g_down = big_grads
    s, u, a, b, hid, o = saved
    d_o, acc_g = _resid_bwd(f"resid_bwd_{tag}", ds_out, o, mod, k, 0.5, tm, h_tiles, after)
    da, db = _ffn_dhid(f"ffn_dhid_{tag}", d_o, wd, idx, a, b, tm_big)
    du = _ffn_du(f"ffn_du_{tag}", da, db, wg, wu, idx, tm_big)
    d = u.shape[1]
    g_gate, g_up = _ffn_dw_in(f"ffn_dwgu_{tag}", u, da, db, _pick(d, MM_ROWS), g_gate, g_up, idx)
    g_down = _ffn_dw_down(f"ffn_dwd_{tag}", hid, d_o, _pick(d, 512), g_down, idx)
    ds, acc_n = _adaln_bwd(f"adaln_bwd_{tag}", s, du, ds_out, gains, k, mod, k, tm, h_tiles)
    return ds, (g_gate, g_up, g_down), (acc_n[:, 0], acc_n[:, 1], acc_g[:, 0]), jnp.sum(acc_n[:, 2], axis=0)


def kernel(x, c, ctx, c_ctx, norm_g, w_mod, b_mod, ffn_w_gate, ffn_w_up, ffn_w_down, ab_w_in, pool_w, pool_scale, q_norm_g, w_uq, kv_norm_g, w_ukv, ab_w_out, conv_w_in, conv_w, conv_w_out, final_norm_g, loss_target, m_c_ctx, m_norm_g, m_w_mod, m_b_mod, m_ffn_w_gate, m_ffn_w_up, m_ffn_w_down, m_ab_w_in, m_pool_w, m_pool_scale, m_q_norm_g, m_w_uq, m_kv_norm_g, m_w_ukv, m_ab_w_out, m_conv_w_in, m_conv_w, m_conv_w_out, m_final_norm_g, v_c_ctx, v_norm_g, v_w_mod, v_b_mod, v_ffn_w_gate, v_ffn_w_up, v_ffn_w_down, v_ab_w_in, v_pool_w, v_pool_scale, v_q_norm_g, v_w_uq, v_kv_norm_g, v_w_ukv, v_ab_w_out, v_conv_w_in, v_conv_w, v_conv_w_out, v_final_norm_g):
    weights = dict(c_ctx=c_ctx, norm_g=norm_g, w_mod=w_mod, b_mod=b_mod, ffn_w_gate=ffn_w_gate, ffn_w_up=ffn_w_up,
                   ffn_w_down=ffn_w_down, ab_w_in=ab_w_in, pool_w=pool_w, pool_scale=pool_scale, q_norm_g=q_norm_g,
                   w_uq=w_uq, kv_norm_g=kv_norm_g, w_ukv=w_ukv, ab_w_out=ab_w_out, conv_w_in=conv_w_in, conv_w=conv_w,
                   conv_w_out=conv_w_out, final_norm_g=final_norm_g)
    mom_m = dict(c_ctx=m_c_ctx, norm_g=m_norm_g, w_mod=m_w_mod, b_mod=m_b_mod, ffn_w_gate=m_ffn_w_gate,
                 ffn_w_up=m_ffn_w_up, ffn_w_down=m_ffn_w_down, ab_w_in=m_ab_w_in, pool_w=m_pool_w,
                 pool_scale=m_pool_scale, q_norm_g=m_q_norm_g, w_uq=m_w_uq, kv_norm_g=m_kv_norm_g, w_ukv=m_w_ukv,
                 ab_w_out=m_ab_w_out, conv_w_in=m_conv_w_in, conv_w=m_conv_w, conv_w_out=m_conv_w_out,
                 final_norm_g=m_final_norm_g)
    mom_v = dict(c_ctx=v_c_ctx, norm_g=v_norm_g, w_mod=v_w_mod, b_mod=v_b_mod, ffn_w_gate=v_ffn_w_gate,
                 ffn_w_up=v_ffn_w_up, ffn_w_down=v_ffn_w_down, ab_w_in=v_ab_w_in, pool_w=v_pool_w,
                 pool_scale=v_pool_scale, q_norm_g=v_q_norm_g, w_uq=v_w_uq, kv_norm_g=v_kv_norm_g, w_ukv=v_w_ukv,
                 ab_w_out=v_ab_w_out, conv_w_in=v_conv_w_in, conv_w=v_conv_w, conv_w_out=v_conv_w_out,
                 final_norm_g=v_final_norm_g)

    t_len, d = x.shape[1], x.shape[2]
    g_len = ctx.shape[1]
    r_len = t_len + g_len
    fc = ffn_w_gate.shape[3]
    heads = d // 128
    pool_dim = d // 2
    q_rank, kv_rank = q_norm_g.shape[1], kv_norm_g.shape[1]
    hw = heads * HEAD_PAD
    attn_scale = 1.0 / math.sqrt(QK_NOPE + QK_ROPE)
    kvr_w = kv_rank + HEAD_PAD
    in_w = pool_dim + q_rank + kvr_w
    tm = 256 if g_len % 256 == 0 else g_len
    assert t_len % tm == 0 and g_len % tm == 0 and t_len % g_len == 0 and pool_dim % 128 == 0
    h_tiles = t_len // tm
    tm_l0 = _pick(r_len, 768, tm)
    tm_l1 = _pick(t_len, 1024, tm)

    xi, yi, ci = lax.axis_index("x"), lax.axis_index("y"), lax.axis_index("c")
    me = 4 * xi + 2 * yi + ci
    shard = 2 * xi + yi

    def halves(w):
        return w.astype(BF).reshape(2, -1, w.shape[-1])

    ffn_names = ["ffn_w_gate", "ffn_w_up", "ffn_w_down"]
    mixer_names = [["ab_w_in", "w_uq", "w_ukv", "ab_w_out"], ["conv_w_in", "conv_w_out"]]
    big_names = ffn_names + mixer_names[0] + mixer_names[1]

    def stage_names(k):
        return mixer_names[k // 3] if k % 3 == 1 else ffn_names

    def stage_halves(k):
        l, f = k // 3, (k % 3) // 2
        if k % 3 == 1:
            return [halves(weights[nm]) for nm in mixer_names[l]]
        return [halves(weights[nm][l, f]) for nm in ffn_names]

    def gather_start(k, dep):
        own = lax.optimization_barrier((tuple(stage_halves(k)), dep))[0]
        n = len(own)
        return _split_start(f"gather_start_s{k}", list(own) + _landing(N_DEV, own), n * len(CHIP_FLIPS),
                            _chips_gather_build(n))

    gather0 = gather_start(0, c)

    small = jnp.concatenate([norm_g.reshape(6, -1), conv_w[0]], axis=0)
    small = jnp.pad(small, ((0, 7), (0, 0)))
    c_row = jnp.pad(c, ((0, 7), (0, 0))) + gather0['token'][0, 0]
    small_all, c_all = _gather_all("gather_small", [small, c_row])
    small_full = small_all[::2].transpose(1, 0, 2).reshape(16, d)
    gains = [jnp.pad(small_full[3 * l:3 * l + 3], ((0, 5), (0, 0))) for l in range(2)]
    conv_w_full = small_full[6:9]
    c16 = jnp.concatenate([c_all[:, 0], c_ctx[None], jnp.zeros((7, d), F32)], axis=0)

    n_col = w_mod.shape[2]
    b_sh = lax.dynamic_slice_in_dim(b_mod, shard * n_col, n_col, axis=1)
    m_sh = [_mm(f"mod_fwd_{l}", c16, w_mod, 'nn', tm=16, tn=768, a_pre=_silu, b_lead=l,
                epi=lambda acc, i, bv: (acc + bv,), epi_args=(b_sh[l:l + 1],), epi_kinds=('n',)) for l in range(2)]
    m_all = _gather_all("gather_mod", [jnp.concatenate(m_sh, axis=0)], two_level=True)[0]
    m_full = m_all[::2].reshape(N_SHARD, 2, 16, n_col).transpose(1, 2, 0, 3).reshape(2, 16, N_MOD * d)
    mod_h = [jnp.pad(lax.dynamic_index_in_dim(m_full[l], me, 0, keepdims=False).reshape(N_MOD, d), ((0, 7), (0, 0)))
             for l in range(2)]
    mod_g0 = jnp.pad(m_full[0, 8].reshape(N_MOD, d), ((0, 7), (0, 0)))
    mods = [jnp.stack([mod_h[0], mod_g0]), mod_h[1][None]]

    def stage_weights(k, handle, after):
        bufs = _split_wait(f"gather_wait_s{k}", handle, after)
        n = len(bufs) // 2
        own = bufs[:n]
        fwd = _split_start(f"forward_start_s{k}", bufs[n:], n * len(CHIP_FLIPS), _sibling_forward_build(n))
        nxt = gather_start(k + 1, fwd['token']) if k + 1 < 6 else None
        landed = _split_wait(f"forward_wait_s{k}", fwd, fwd['token'])
        full = [lax.dynamic_update_slice_in_dim(z, a, 2 * shard, 0) for z, a in zip(landed, own)]
        gw = {nm: g.reshape(N_SHARD, 2 * g.shape[1], g.shape[2]) for nm, g in zip(stage_names(k), full)}
        return gw, nxt, (fwd['token'] if nxt is None else nxt['token'])

    def ffn_weights(gw):
        return gw["ffn_w_gate"].reshape(N_SHARD, 1, d, fc), gw["ffn_w_up"].reshape(N_SHARD, 1, d, fc), \
            gw["ffn_w_down"], 0

    gw_s0, gather1, tok0 = stage_weights(0, gather0, m_all)
    ffn_w = [[ffn_weights(gw_s0), None], [None, None]]

    s0 = jnp.concatenate([x[0], ctx[0]], axis=0)
    s1, sav_f00 = _ffn_half_fwd("l0a", s0, gains[0], mods[0], 0, ffn_w[0][0], t_len, tm, h_tiles, tm_l0, tok0)

    gw_s1, gather2, tok1 = stage_weights(1, gather1, s1)
    w_out_full = gw_s1["ab_w_out"].reshape(-1, d)
    w_uq_full = gw_s1["w_uq"].reshape(q_rank, heads * (QK_NOPE + QK_ROPE))
    w_ukv_full = gw_s1["w_ukv"].transpose(1, 0, 2).reshape(kv_rank, heads * (QK_NOPE + V_HEAD))
    w_in_full = gw_s1["ab_w_in"].transpose(1, 0, 2).reshape(d, -1)

    wq_p = jnp.pad(w_uq_full.reshape(q_rank, heads, QK_NOPE + QK_ROPE),
                   ((0, 0), (0, 0), (0, HEAD_PAD - QK_NOPE - QK_ROPE))).reshape(q_rank, hw)
    ukv3 = w_ukv_full.reshape(kv_rank, heads, QK_NOPE + V_HEAD)
    wk_top = jnp.pad(ukv3[..., :QK_NOPE], ((0, 0), (0, 0), (0, HEAD_PAD - QK_NOPE))).reshape(kv_rank, hw)
    wv_top = jnp.pad(ukv3[..., QK_NOPE:], ((0, 0), (0, 0), (0, HEAD_PAD - V_HEAD))).reshape(kv_rank, hw)
    src_row = lax.broadcasted_iota(jnp.int32, (HEAD_PAD, hw), 0)
    dst_lane = lax.broadcasted_iota(jnp.int32, (HEAD_PAD, hw), 1) % HEAD_PAD
    spread = ((src_row < QK_ROPE) & (dst_lane == src_row + QK_NOPE)).astype(BF)
    wk_ext = jnp.concatenate([wk_top, spread], axis=0)
    wv_ext = jnp.concatenate([wv_top, jnp.zeros((HEAD_PAD, hw), BF)], axis=0)
    w_in_pool = w_in_full[:, :pool_dim]
    w_in_q = w_in_full[:, pool_dim:pool_dim + q_rank]
    w_in_kvr = jnp.pad(w_in_full[:, pool_dim + q_rank:], ((0, 0), (0, HEAD_PAD - QK_ROPE)))
    w_out_attn = jnp.pad(w_out_full[pool_dim:].reshape(heads, V_HEAD, d),
                         ((0, 0), (0, HEAD_PAD - V_HEAD), (0, 0))).reshape(hw, d)
    w_out_p = jnp.concatenate([w_out_full[:pool_dim], w_out_attn], axis=0)

    u_mix = _adaln_fwd("adaln_l0m", s1, gains[0], 1, mods[0], 1, tm, h_tiles, tok1)
    p_pool = _mm("in_pool", u_mix, w_in_pool, 'nn', tm=tm_l0, tn=pool_dim)
    p_q = _mm("in_q", u_mix, w_in_q, 'nn', tm=tm_l0, tn=q_rank)
    p_kvr = _mm("in_kvr", u_mix, w_in_kvr, 'nn', tm=tm_l0, tn=kvr_w)
    qg = jnp.pad(q_norm_g, ((0, 7), (0, 0)))
    kvg = jnp.pad(kv_norm_g, ((0, 7), (0, 0)))
    tq_c, tq_s = _rope_tables(t_len, g_len, QK_NOPE)
    tk_c, tk_s = _rope_tables(t_len, g_len, 0)

    def qn_fn(rows, consts, m):
        n, _ = _rms(rows[0])
        return [n * consts[0][0:1, :]], {}

    qn = _rows_call("q_norm", qn_fn, r_len, tm, [p_q], [qg], None, [(q_rank, BF)])[0]
    q_r = _mm("q_up", qn, wq_p, 'nn', tm=tm_l0, tn=hw, out_dtypes=(BF,),
              epi=lambda acc, i, ct, st: (_rope(acc, ct, st),), epi_args=(tq_c, tq_s), epi_kinds=('mt', 'mt'))

    def kvn_fn(rows, consts, m):
        pv, ct, st = rows
        n, _ = _rms(pv[:, :kv_rank])
        return [jnp.concatenate([n * consts[0][0:1, :], _rope(pv[:, kv_rank:], ct, st)], axis=1)], {}

    kvn = _rows_call("kv_norm", kvn_fn, r_len, tm, [p_kvr, tk_c, tk_s], [kvg], None, [(kvr_w, BF)])[0]
    k_p = _mm("k_up", kvn, wk_ext, 'nn', tm=tm_l0, tn=hw, out_dtypes=(BF,))
    v_p = _mm("v_up", kvn, wv_ext, 'nn', tm=tm_l0, tn=hw, out_dtypes=(BF,))
    tq_h = _pick(t_len, 512, tm)
    o_h, lse_h = _attn_fwd("attn_h", q_r, k_p, v_p, t_len, 0, r_len, 0, heads, tm, attn_scale)
    o_g, lse_g = _attn_fwd("attn_g", q_r, k_p, v_p, g_len, t_len, g_len, t_len // g_len, heads, tm, attn_scale)
    y_h = _pool_fwd("pool_h", p_pool[:t_len], pool_w[0], pool_scale)
    y_g = _pool_fwd("pool_g", p_pool[t_len:], pool_w[0], pool_scale)
    cat = jnp.concatenate([jnp.concatenate([y_h, y_g], axis=0), jnp.concatenate([o_h, o_g], axis=0)], axis=1)

    def resid_epi(k3, n_lat, tmr):
        def epi(acc, i, sv, mv):
            return sv + _row_gate(mv, k3, i, tmr, n_lat) * acc, acc
        return epi

    s2, o_mix0 = _mm("mix_out_l0", cat, w_out_p, 'nn', tm=tm_l0, tn=d, out_dtypes=(F32, F32),
                     epi=resid_epi(5, t_len, tm_l0), epi_args=(s1, mods[0]), epi_kinds=('mn', 'w'))
    gw_s2, gather3, tok2 = stage_weights(2, gather2, s2)
    ffn_w[0][1] = ffn_weights(gw_s2)
    s3, sav_f01 = _ffn_half_fwd("l0b", s2, gains[0], mods[0], 2, ffn_w[0][1], t_len, tm, h_tiles, tm_l0, tok2)

    gw_s3, gather4, tok3 = stage_weights(3, gather3, s3)
    ffn_w[1][0] = ffn_weights(gw_s3)
    tml = 256 if t_len % 256 == 0 else tm
    h3 = s3[:t_len]
    h4, sav_f10 = _ffn_half_fwd("l1a", h3, gains[1], mods[1], 0, ffn_w[1][0], t_len, tml, None, tm_l1, tok3)
    gw_s4, gather5, tok4 = stage_weights(4, gather4, h4)
    cw_out_full = gw_s4["conv_w_out"].reshape(-1, d)
    cw_in_full = gw_s4["conv_w_in"].transpose(1, 0, 2).reshape(d, -1)
    u_cv = _adaln_fwd("adaln_l1m", h4, gains[1], 1, mods[1], 1, tml, None, tok4)
    p3 = _mm("conv_in", u_cv, cw_in_full, 'nn', tm=tm_l1, tn=512)
    cwp = conv_w_full
    tc = _pick(d, 256)
    y_cv = _conv_fwd("conv_fwd", p3, cwp, tc)
    h5, o_mix1 = _mm("mix_out_l1", y_cv, cw_out_full, 'nn', tm=tm_l1, tn=d, out_dtypes=(F32, F32),
                     epi=resid_epi(5, t_len, tm_l1), epi_args=(h4, mods[1]), epi_kinds=('mn', 'w'))
    gw_s5, _, tok5 = stage_weights(5, gather5, h5)
    ffn_w[1][1] = ffn_weights(gw_s5)
    h6, sav_f11 = _ffn_half_fwd("l1b", h5, gains[1], mods[1], 2, ffn_w[1][1], t_len, tml, None, tm_l1, tok5)

    fg = jnp.pad(final_norm_g[None], ((0, 7), (0, 0)))
    dh6, acc_loss = _loss_head("loss_head", h6, loss_target[0], fg, tml)
    d_final_g = acc_loss[0, 0]

    dgain = [[None] * 3 for _ in range(2)]
    dmod = [[None] * N_MOD for _ in range(2)]

    def put(l, k, triple):
        dmod[l][3 * k], dmod[l][3 * k + 1], dmod[l][3 * k + 2] = triple

    def empty_ffn_grads():
        return (lax.empty((N_SHARD, d, fc), BF), lax.empty((N_SHARD, d, fc), BF), lax.empty((N_SHARD, fc, d), BF))

    def by_shard_rows(g):
        return g.reshape(N_SHARD, -1, g.shape[-1])

    def by_shard_cols(g):
        return g.reshape(g.shape[0], N_SHARD, -1).transpose(1, 0, 2)

    def pair_start(k, big):
        send = [b.astype(BF).reshape(N_DEV, b.shape[1] // 2, b.shape[2]) for b in big]
        n = len(send)
        return _split_start(f"grads_pair_start_s{k}", send + _landing(N_SHARD, send), n * N_SHARD,
                            _sibling_halves_build(n))

    def chips_start(k, handle, after):
        bufs = _split_wait(f"grads_pair_wait_s{k}", handle, after)
        n = len(bufs) // 2
        pre = [_add_halves(f"grads_add_s{k}_{nm}", s, z) for nm, s, z in zip(stage_names(k), bufs[:n], bufs[n:])]
        return _split_start(f"grads_start_s{k}", pre + _landing(N_SHARD, pre), n * len(CHIP_FLIPS),
                            _chips_scatter_build(n))

    def landed_sums(k, handle, after):
        bufs = _split_wait(f"grads_wait_s{k}", handle, after)
        n = len(bufs) // 2
        landed = [lax.dynamic_update_slice_in_dim(z, lax.dynamic_slice_in_dim(p, shard, 1, 0), shard, 0)
                  for p, z in zip(bufs[:n], bufs[n:])]
        return [_sum_lead(f"sum_grads_s{k}_{nm}", z) for nm, z in zip(stage_names(k), landed)]

    pair, scatter, sums = [None] * 6, [None] * 6, [None] * 6
    grads, upd = {}, {}
    ffn_out = {nm: None for nm in ffn_names}

    def lanes_last(nm, t):
        shp = weights[nm].shape
        return jnp.swapaxes(t, -1, -2) if shp[-1] % 128 and shp[-2] % 128 == 0 else t

    def finish_stage(k, halves):
        n = len(halves)
        lands = [pltpu.with_memory_space_constraint(lax.empty(h.shape, h.dtype), pltpu.HBM) for h in halves]
        swap = _split_start(f"swap_start_s{k}", list(halves) + lands, n, _sibling_whole_build(n))
        both = _split_wait(f"swap_wait_s{k}", swap, swap['token'])
        for nm, a, g in zip(stage_names(k), both[:n], both[n:]):
            piece = jnp.where(ci == 0, jnp.concatenate([a, g], axis=0), jnp.concatenate([g, a], axis=0))
            if k % 3 == 1:
                grads[nm] = piece.reshape(weights[nm].shape)
                upd[nm] = _adamw(f"adamw_{nm}", weights[nm], grads[nm], mom_m[nm], mom_v[nm])
            else:
                ffn_out[nm] = _adamw_piece(f"adamw_{nm}_s{k}", lanes_last(nm, weights[nm]), lanes_last(nm, piece),
                                           lanes_last(nm, mom_m[nm]), lanes_last(nm, mom_v[nm]),
                                           (k // 3, (k % 3) // 2), ffn_out[nm])
    dh5, ffn_g, tr, dgain[1][2] = _ffn_half_bwd("l1b", dh6, sav_f11, gains[1], mods[1], 2, ffn_w[1][1],
                                                empty_ffn_grads(), tml, None, tm_l1)
    put(1, 2, tr)
    pair[5] = pair_start(5, list(ffn_g))
    d_o1, acc_g1 = _resid_bwd("resid_bwd_l1m", dh5, o_mix1, mods[1], 1, 1.0, tml, None, pair[5]['token'])
    dy_cv = _mm("mix_out_l1_dx", d_o1, cw_out_full, 'nt', tm=tm_l1, tn=d)
    d_cw_out = _mm("mix_out_l1_dw", y_cv, d_o1, 'tn', tm=512, tn=512)
    dp3, d_cw = _conv_bwd("conv_bwd", p3, cwp, dy_cv, tc)
    du_cv = _conv_din("conv_in_dx", dp3, cw_in_full, tm_l1)
    d_cw_in = _conv_dw_in("conv_in_dw", u_cv, dp3, _pick(d, MM_ROWS), _pick(d, 512))
    dh4, acc_n1 = _adaln_bwd("adaln_bwd_l1m", h4, du_cv, dh5, gains[1], 1, mods[1], 1, tml, None)
    put(1, 1, (acc_n1[:, 0], acc_n1[:, 1], acc_g1[:, 0]))
    dgain[1][1] = acc_n1[0, 2]
    scatter[5] = chips_start(5, pair[5], dh4)
    pair[4] = pair_start(4, [by_shard_cols(d_cw_in), by_shard_rows(d_cw_out)])
    dh3, ffn_g, tr, dgain[1][0] = _ffn_half_bwd("l1a", dh4, sav_f10, gains[1], mods[1], 0, ffn_w[1][0],
                                                empty_ffn_grads(), tml, None, tm_l1,
                                                scatter[5]['token'] + pair[4]['token'])
    put(1, 0, tr)
    finish_stage(5, landed_sums(5, scatter[5], dh3))
    scatter[4] = chips_start(4, pair[4], dh3)
    pair[3] = pair_start(3, list(ffn_g))

    ds3 = jnp.concatenate([dh3, jnp.zeros((g_len, d), F32)], axis=0) \
        + (scatter[4]['token'][0, 0] + pair[3]['token'][0, 0])
    ds2, ffn_g, tr, dgain[0][2] = _ffn_half_bwd("l0b", ds3, sav_f01, gains[0], mods[0], 2, ffn_w[0][1],
                                                empty_ffn_grads(), tm, h_tiles, tm_l0)
    put(0, 2, tr)
    finish_stage(4, landed_sums(4, scatter[4], ds2))
    scatter[3] = chips_start(3, pair[3], ds2)
    pair[2] = pair_start(2, list(ffn_g))
    d_o0, acc_g0 = _resid_bwd("resid_bwd_l0m", ds2, o_mix0, mods[0], 1, 1.0, tm, h_tiles,
                              scatter[3]['token'] + pair[2]['token'])
    dcat = _mm("mix_out_l0_dx", d_o0, w_out_p, 'nt', tm=tm_l0, tn=pool_dim + hw)
    d_w_out_p = _mm("mix_out_l0_dw", cat, d_o0, 'tn', tm=512, tn=512)
    col_blk = pool_dim // HEAD_PAD
    dq_h, dk_h, dv_h = _attn_bwd("attn_bwd_h", q_r, k_p, v_p, cat, dcat, lse_h, t_len, 0, r_len, 0, heads, tq_h,
                                 attn_scale, col_blk)
    dq_g, dk_all, dv_all = _attn_bwd("attn_bwd_g", q_r, k_p, v_p, cat, dcat, lse_g, g_len, t_len, g_len,
                                     t_len // g_len, heads, tm, attn_scale, col_blk, onto=(dk_h, dv_h))
    dq_all = jnp.concatenate([dq_h, dq_g], axis=0)
    dkvn = _mm("k_up_dx", dk_all, wk_ext, 'nt', tm=tm_l0, tn=kvr_w)
    dkvn = _mm("v_up_dx", dv_all, wv_ext, 'nt', tm=tm_l0, tn=kvr_w, epi=lambda acc, i, prev: (acc + prev,),
               epi_args=(dkvn,), epi_kinds=('mn',))
    d_wk_ext = _mm("k_up_dw", kvn, dk_all, 'tn', tm=kvr_w, tn=512)
    d_wv_ext = _mm("v_up_dw", kvn, dv_all, 'tn', tm=kvr_w, tn=512)

    def kvn_bwd_fn(rows, consts, m):
        pv, dv_, ct, st = rows
        g = consts[0][0:1, :]
        n, r = _rms(pv[:, :kv_rank])
        dyn = dv_[:, :kv_rank]
        dckv = _rms_bwd(dyn * g, n, r)
        dkr = _rope_t(dv_[:, kv_rank:], ct, st)
        return [jnp.concatenate([dckv, dkr], axis=1)], {0: jnp.sum(dyn * n, axis=0, keepdims=True)}

    dp_kvr, acc_kvg = _rows_call("kv_norm_bwd", kvn_bwd_fn, r_len, tm, [p_kvr, dkvn, tk_c, tk_s], [kvg], None,
                                 [(kvr_w, BF)], acc_w=kv_rank)

    def qrope_bwd_fn(rows, consts, m):
        return [_rope_t(rows[0], rows[1], rows[2])], {}

    dq_pad = _rows_call("q_rope_bwd", qrope_bwd_fn, r_len, tm, [dq_all, tq_c, tq_s], [], None, [(hw, BF)])[0]
    dqn = _mm("q_up_dx", dq_pad, wq_p, 'nt', tm=tm_l0, tn=q_rank)
    d_wq_p = _mm("q_up_dw", qn, dq_pad, 'tn', tm=512, tn=512)

    def qn_bwd_fn(rows, consts, m):
        pv, dv_ = rows
        g = consts[0][0:1, :]
        n, r = _rms(pv)
        return [_rms_bwd(dv_ * g, n, r)], {0: jnp.sum(dv_ * n, axis=0, keepdims=True)}

    dp_q, acc_qg = _rows_call("q_norm_bwd", qn_bwd_fn, r_len, tm, [p_q, dqn], [qg], None, [(q_rank, BF)],
                              acc_w=q_rank)
    dpu_h, dpw_h, dps_h = _pool_bwd("pool_bwd_h", p_pool[:t_len], dcat, pool_w[0], pool_scale, 0)
    dpu_g, dpw_g, dps_g = _pool_bwd("pool_bwd_g", p_pool[t_len:], dcat, pool_w[0], pool_scale, t_len)
    dp_pool = jnp.concatenate([dpu_h, dpu_g], axis=0)
    add_prev = lambda acc, i, prev: (acc + prev,)
    du_mix = _mm("in_pool_dx", dp_pool, w_in_pool, 'nt', tm=tm_l0, tn=d)
    du_mix = _mm("in_q_dx", dp_q, w_in_q, 'nt', tm=tm_l0, tn=d, epi=add_prev, epi_args=(du_mix,), epi_kinds=('mn',))
    du_mix = _mm("in_kvr_dx", dp_kvr, w_in_kvr, 'nt', tm=tm_l0, tn=d, epi=add_prev, epi_args=(du_mix,), epi_kinds=('mn',))
    d_w_in = jnp.concatenate([
        _mm("in_pool_dw", u_mix, dp_pool, 'tn', tm=512, tn=pool_dim),
        _mm("in_q_dw", u_mix, dp_q, 'tn', tm=512, tn=q_rank),
        _mm("in_kvr_dw", u_mix, dp_kvr, 'tn', tm=512, tn=kvr_w)[:, :kv_rank + QK_ROPE]], axis=1)
    ds1, acc_n0 = _adaln_bwd("adaln_bwd_l0m", s1, du_mix, ds2, gains[0], 1, mods[0], 1, tm, h_tiles)
    put(0, 1, (acc_n0[:, 0], acc_n0[:, 1], acc_g0[:, 0]))
    dgain[0][1] = jnp.sum(acc_n0[:, 2], axis=0)
    d_w_uq = d_wq_p.reshape(q_rank, heads, HEAD_PAD)[..., :QK_NOPE + QK_ROPE].reshape(q_rank, -1)
    d_w_ukv = jnp.concatenate([d_wk_ext[:kv_rank].reshape(kv_rank, heads, HEAD_PAD)[..., :QK_NOPE],
                               d_wv_ext[:kv_rank].reshape(kv_rank, heads, HEAD_PAD)[..., :V_HEAD]],
                              axis=-1).reshape(kv_rank, -1)
    d_w_out = jnp.concatenate([d_w_out_p[:pool_dim],
                               d_w_out_p[pool_dim:].reshape(heads, HEAD_PAD, d)[:, :V_HEAD].reshape(-1, d)], axis=0)
    finish_stage(3, landed_sums(3, scatter[3], ds1))
    scatter[2] = chips_start(2, pair[2], ds1)
    pair[1] = pair_start(1, [by_shard_cols(d_w_in), by_shard_rows(d_w_uq), by_shard_cols(d_w_ukv),
                             by_shard_rows(d_w_out)])
    ds0, ffn_g, tr, dgain[0][0] = _ffn_half_bwd("l0a", ds1, sav_f00, gains[0], mods[0], 0, ffn_w[0][0],
                                                empty_ffn_grads(), tm, h_tiles, tm_l0,
                                                scatter[2]['token'] + pair[1]['token'])
    put(0, 0, tr)
    grad_x = ds0[:t_len][None]
    finish_stage(2, landed_sums(2, scatter[2], ds0))
    pair[0] = pair_start(0, list(ffn_g))

    dmh = jnp.stack([jnp.stack([dmod[l][k][0] for k in range(N_MOD)]) for l in range(2)])
    dmg0 = jnp.stack([dmod[0][k][1] for k in range(N_MOD)])
    dg_rows = jnp.stack([dgain[l][k] for l in range(2) for k in range(3)])
    pieces = [dmh.reshape(2 * N_MOD, d), dmg0, dg_rows, d_cw[:3], d_final_g[None],
              (dpw_h + dpw_g).reshape(-1, d), jnp.pad((dps_h + dps_g)[0], (0, d - pool_dim))[None],
              jnp.pad(acc_qg[0, 0], (0, d - q_rank))[None], jnp.pad(acc_kvg[0, 0], (0, d - kv_rank))[None],
              acc_loss[0, 1][None]]
    n_piece = [p.shape[0] for p in pieces]
    pieces = [jnp.pad(p, ((0, (-p.shape[0]) % 8), (0, 0))) for p in pieces]
    small_g = jnp.concatenate(pieces, axis=0) + pair[0]['token'][0, 0]
    sg_all = _gather_all("gather_small_grads", [small_g], two_level=True)[0]
    sg_sum = _sum_lead("sum_small_grads", sg_all)
    scatter[1] = chips_start(1, pair[1], sg_sum)
    offs = [0]
    for p in pieces:
        offs.append(offs[-1] + p.shape[0])
    part = lambda j: sg_sum[offs[j]:offs[j] + n_piece[j]]
    sum_dmh, sum_dmg0, g_norm_full, g_conv_w_full = part(0).reshape(2, N_MOD * d), part(1).reshape(N_MOD * d), part(2), part(3)
    g_final = part(4)[0]
    loss = part(9)[0, 0]
    g_pool_w = part(5).reshape(pool_w.shape)
    g_pool_scale = part(6)[:, :pool_dim]
    g_q_norm = part(7)[:, :q_rank]
    g_kv_norm = part(8)[:, :kv_rank]
    col0 = shard * (d // N_SHARD)
    g_norm_g = lax.dynamic_slice_in_dim(g_norm_full.reshape(2, 3, d), col0, d // N_SHARD, axis=2)
    g_conv_w = lax.dynamic_slice_in_dim(g_conv_w_full, col0, d // N_SHARD, axis=1)[None]
    g_b_mod = _sum_lead("sum_b_mod", jnp.stack([sum_dmh, jnp.stack([sum_dmg0, jnp.zeros_like(sum_dmg0)])]))

    dm16 = []
    for l in range(2):
        per_dev = sg_all[:, l * N_MOD:(l + 1) * N_MOD].reshape(N_DEV, N_MOD * d)
        row8 = (sum_dmg0 if l == 0 else jnp.zeros_like(sum_dmg0)) + scatter[1]['token'][0, 0]
        full = jnp.concatenate([per_dev, row8[None], jnp.zeros((7, N_MOD * d), F32)], axis=0)
        dm16.append(lax.dynamic_slice_in_dim(full, shard * n_col, n_col, axis=1))
    g_w_mod = jnp.stack([_mm(f"mod_dw_{l}", c16, dm16[l], 'tn', tm=512, tn=768, a_pre=_silu) for l in range(2)])
    dc16 = _mm("mod_dx", dm16[0], w_mod, 'nt', tm=16, tn=512, b_lead=0, epi=lambda acc, i, cv: (acc * _dsilu(cv),),
               epi_args=(c16,), epi_kinds=('mn',))
    dc_all = _gather_all("gather_dc", [dc16])[0]
    g_c_ctx = _sum_lead("sum_dc", dc_all[::2])[8]

    grads.update(c_ctx=g_c_ctx, norm_g=g_norm_g, w_mod=g_w_mod, b_mod=g_b_mod, pool_w=g_pool_w,
                 pool_scale=g_pool_scale, q_norm_g=g_q_norm, kv_norm_g=g_kv_norm, conv_w=g_conv_w, final_norm_g=g_final)
    names = list(weights)

    scatter[0] = chips_start(0, pair[0], g_c_ctx)
    upd.update({n: _adamw(f"adamw_{n}", weights[n], grads[n].reshape(weights[n].shape), mom_m[n], mom_v[n],
                          scatter[0]['token']) for n in names if n not in big_names})
    finish_stage(1, landed_sums(1, scatter[1], upd["w_mod"][0]))
    finish_stage(0, landed_sums(0, scatter[0], upd[mixer_names[0][-1]][0]))
    for nm in ffn_names:
        done = [lanes_last(nm, t) for t in ffn_out[nm]]
        grads[nm], upd[nm] = done[0], done[1:]
    return (loss, grad_x, *[grads[n].reshape(weights[n].shape) for n in names], *[upd[n][0] for n in names],
            *[upd[n][1] for n in names], *[upd[n][2] for n in names])
```

```python
import functools
import math

import jax
import jax.numpy as jnp
from jax import lax
from jax.experimental import pallas as pl
from jax.experimental.pallas import tpu as pltpu

F32 = jnp.float32
BF = jnp.bfloat16
MESH = pl.DeviceIdType.MESH

N_DEV = 8
N_SHARD = 4
RMS_EPS = 1e-6
N_MOD = 9
POOL_WINDOWS = (2, 4, 8, 16)
QK_NOPE = 64
QK_ROPE = 32
V_HEAD = 64
HEAD_PAD = 128
GRID_W = 64
ROPE_THETA = 10000.0
POOL_PAD = 16
ADAM_LR, ADAM_B1, ADAM_B2, ADAM_EPS, ADAM_WD, ADAM_STEP = 0.001, 0.9, 0.999, 1e-08, 0.01, 10
VMEM_LIMIT = 56 * 1024 * 1024
MM_ROWS = 1024


def _pcall(body, **kw):
    return pl.pallas_call(body, **kw)


def _params(sem=None):
    return pltpu.CompilerParams(dimension_semantics=sem, vmem_limit_bytes=VMEM_LIMIT)


def _pick(n, pref, mult=128):
    best = None
    d = mult
    while d <= min(n, pref):
        if n % d == 0:
            best = d
        d += mult
    return best if best is not None else n


def _silu(z):
    return z * jax.nn.sigmoid(z)


def _dsilu(z):
    s = jax.nn.sigmoid(z)
    return s * (1.0 + z * (1.0 - s))


def _dot(a, b, dims):
    return lax.dot_general(a.astype(BF), b.astype(BF), (dims, ((), ())), preferred_element_type=F32)


NN = ((1,), (0,))
NT = ((1,), (1,))
TN = ((0,), (0,))


ALL_FLIPS = [(kx, ky, kc) for kx in (0, 1) for ky in (0, 1) for kc in (0, 1) if (kx, ky, kc) != (0, 0, 0)]
CHIP_FLIPS = [(1, 0, 0), (0, 1, 0), (1, 1, 0)]
SIBLING = (0, 0, 1)
COMM_SPLIT = 8
SPLIT_MIN_ROWS = 256


def _exchange(name, arrays, plan, lead, whole_src, split=COMM_SPLIT):
    n = len(arrays)
    blk_shapes = [tuple(a.shape) if whole_src else tuple(a.shape[1:]) for a in arrays]
    splits = []
    for shp in blk_shapes:
        s = 1
        while s * 2 <= split and shp[0] % (s * 2) == 0 and (shp[0] // (s * 2)) % 16 == 0 \
                and shp[0] // (s * 2) >= SPLIT_MIN_ROWS:
            s *= 2
        splits.append(s)
    items = plan(0, 0, 0)
    n_items = len(items)
    remote_ids = [k for k, it in enumerate(items) if it[0] is not None]
    local_ids = [k for k, it in enumerate(items) if it[0] is None]
    slots = [(a, s) for s in range(max(splits)) for a in range(n) if s < splits[a]]
    n_slot = len(slots)

    def body(*refs):
        ins, outs = refs[:n], refs[n:2 * n]
        send_sems, recv_sems, loc_sems = refs[2 * n:]
        x, y, c = lax.axis_index("x"), lax.axis_index("y"), lax.axis_index("c")
        plan_here = plan(x, y, c)

        def rows(ref, a, s):
            rc = blk_shapes[a][0] // splits[a]
            return ref.at[pl.ds(s * rc, rc)]

        def make(si, k):
            a, s = slots[si]
            flip, src, dst, _ = plan_here[k]
            base = outs[a] if src[0] == 'out' else ins[a]
            src_ref = rows(base if src[1] is None else base.at[src[1]], a, s)
            dst_ref = rows(outs[a].at[dst], a, s)
            if flip is None:
                return pltpu.make_async_copy(src_ref, dst_ref, loc_sems.at[si * max(1, len(local_ids)) + local_ids.index(k)])
            peer = (1 - x if flip[0] else x, 1 - y if flip[1] else y, 1 - c if flip[2] else c)
            sem = si * len(remote_ids) + remote_ids.index(k)
            return pltpu.make_async_remote_copy(src_ref=src_ref, dst_ref=dst_ref, send_sem=send_sems.at[sem],
                                                recv_sem=recv_sems.at[sem], device_id=peer, device_id_type=MESH)

        copies = {}
        for si in range(n_slot):
            for k in range(n_items):
                if plan_here[k][3] is None:
                    copies[si, k] = make(si, k)
                    copies[si, k].start()
        arrived = set()
        for si in range(n_slot):
            for k in range(n_items):
                after = plan_here[k][3]
                if after is not None:
                    if (si, after) not in arrived:
                        copies[si, after].wait_recv()
                        arrived.add((si, after))
                    copies[si, k] = make(si, k)
                    copies[si, k].start()
        for (si, k), cp in copies.items():
            if plan_here[k][0] is None:
                cp.wait()
            else:
                cp.wait_send()
                if (si, k) not in arrived:
                    cp.wait_recv()

    any_spec = pl.BlockSpec(memory_space=pl.ANY)
    n_rem = max(1, n_slot * len(remote_ids))
    outs = _pcall(
        body, name=name,
        out_shape=[jax.ShapeDtypeStruct((lead,) + s, a.dtype) for s, a in zip(blk_shapes, arrays)],
        in_specs=[any_spec] * n, out_specs=[any_spec] * n,
        scratch_shapes=[pltpu.SemaphoreType.DMA((n_rem,)), pltpu.SemaphoreType.DMA((n_rem,)),
                        pltpu.SemaphoreType.DMA((max(1, n_slot * len(local_ids)),))],
    )(*arrays)
    return list(outs)


def _place(x, y, c):
    return 4 * x + 2 * y + c


def _flip(v, f):
    return 1 - v if f else v


def _gather_all(name, arrays, two_level=False):
    def plan(x, y, c):
        me = _place(x, y, c)
        if not two_level:
            return [(None, ('in', None), me, None)] + [(f, ('in', None), me, None) for f in ALL_FLIPS]
        items = [(None, ('in', None), me, None), (SIBLING, ('in', None), me, None)]
        items += [(f, ('in', None), me, None) for f in CHIP_FLIPS]
        for j, f in enumerate(CHIP_FLIPS):
            got = _place(_flip(x, f[0]), _flip(y, f[1]), c)
            items.append((SIBLING, ('out', got), got, 2 + j))
        return items
    return _exchange(name, arrays, plan, N_DEV, True, split=1)


HBM_SPEC = pl.BlockSpec(memory_space=pltpu.HBM)
SEM_SPEC = pl.BlockSpec(memory_space=pltpu.SEMAPHORE)
SIDE_EFFECT = pltpu.SideEffectType.DATAFLOW_SIDE_EFFECTING


def _split_start(name, bufs, n_copies, build):
    n = len(bufs)

    def body(*refs):
        for cp in build(refs[:n], refs[n], refs[n + 1]):
            cp.start()
        token = refs[-1]
        token[...] = jnp.zeros_like(token)

    res = _pcall(
        body, name=name,
        out_shape=(pltpu.SemaphoreType.DMA((n_copies,)), pltpu.SemaphoreType.DMA((n_copies,)),
                   *[pltpu.HBM(b.shape, b.dtype) for b in bufs], jax.ShapeDtypeStruct((8, 128), F32)),
        in_specs=[HBM_SPEC] * n,
        out_specs=(SEM_SPEC, SEM_SPEC, *[HBM_SPEC] * n, pl.BlockSpec(memory_space=pltpu.VMEM)),
        input_output_aliases={i: 2 + i for i in range(n)},
        compiler_params=pltpu.CompilerParams(has_side_effects=SIDE_EFFECT),
    )(*[pltpu.with_memory_space_constraint(b, pltpu.HBM) for b in bufs])
    return dict(send=res[0], recv=res[1], bufs=list(res[2:2 + n]), token=res[-1], build=build)


def _split_wait(name, handle, after):
    n = len(handle['bufs'])
    build = handle['build']

    def body(*refs):
        for cp in build(refs[:n], refs[n], refs[n + 1]):
            cp.wait_send()
            cp.wait_recv()

    res = _pcall(
        body, name=name, out_shape=tuple(pltpu.HBM(b.shape, b.dtype) for b in handle['bufs']),
        in_specs=[HBM_SPEC] * n + [SEM_SPEC, SEM_SPEC, pl.BlockSpec(memory_space=pl.ANY)],
        out_specs=tuple([HBM_SPEC] * n), input_output_aliases={i: i for i in range(n)},
        compiler_params=pltpu.CompilerParams(has_side_effects=SIDE_EFFECT),
    )(*handle['bufs'], handle['send'], handle['recv'], after)
    return list(res)


def _landing(lead, arrays):
    return [pltpu.with_memory_space_constraint(lax.empty((lead,) + tuple(a.shape[1:]), a.dtype), pltpu.HBM)
            for a in arrays]


def _copy_list(n, per_array, make):
    def build(refs, send_sems, recv_sems):
        copies = []
        for a in range(n):
            for j in range(per_array):
                src, dst, peer = make(refs, a, j)
                k = a * per_array + j
                copies.append(pltpu.make_async_remote_copy(src_ref=src, dst_ref=dst, send_sem=send_sems.at[k],
                                                           recv_sem=recv_sems.at[k], device_id=peer,
                                                           device_id_type=MESH))
        return copies
    return build


def _mesh_place():
    x, y, c = lax.axis_index("x"), lax.axis_index("y"), lax.axis_index("c")
    return x, y, c, 2 * x + y


def _chips_gather_build(n):
    def make(refs, a, j):
        x, y, c, chip = _mesh_place()
        px, py = _flip(x, CHIP_FLIPS[j][0]), _flip(y, CHIP_FLIPS[j][1])
        return refs[a].at[c], refs[n + a].at[2 * chip + c], (px, py, c)
    return _copy_list(n, len(CHIP_FLIPS), make)


def _chips_scatter_build(n):
    def make(refs, a, j):
        x, y, c, chip = _mesh_place()
        px, py = _flip(x, CHIP_FLIPS[j][0]), _flip(y, CHIP_FLIPS[j][1])
        return refs[a].at[2 * px + py], refs[n + a].at[chip], (px, py, c)
    return _copy_list(n, len(CHIP_FLIPS), make)


def _sibling_forward_build(n):
    def make(refs, a, j):
        x, y, c, _ = _mesh_place()
        blk = 2 * (2 * _flip(x, CHIP_FLIPS[j][0]) + _flip(y, CHIP_FLIPS[j][1])) + c
        return refs[a].at[blk], refs[a].at[blk], (x, y, 1 - c)
    return _copy_list(n, len(CHIP_FLIPS), make)


def _sibling_halves_build(n):
    def make(refs, a, j):
        x, y, c, _ = _mesh_place()
        return refs[a].at[2 * j + 1 - c], refs[n + a].at[j], (x, y, 1 - c)
    return _copy_list(n, N_SHARD, make)


def _sibling_whole_build(n):
    def make(refs, a, j):
        x, y, c, _ = _mesh_place()
        return refs[a], refs[n + a], (x, y, 1 - c)
    return _copy_list(n, 1, make)


def _add_halves(name, send, land):
    _, r, cdim = send.shape
    tr = _pick(r, max(16, (1 << 20) // (cdim * 2)), 16)

    def body(c_ref, own_ref, got_ref, o_ref):
        o_ref[...] = (own_ref[...].astype(F32) + got_ref[...].astype(F32)).astype(BF)

    grid_spec = pltpu.PrefetchScalarGridSpec(
        num_scalar_prefetch=1, grid=(N_SHARD, r // tr),
        in_specs=[pl.BlockSpec((None, tr, cdim), lambda sh, i, cr: (2 * sh + cr[0], i, 0)),
                  pl.BlockSpec((None, tr, cdim), lambda sh, i, cr: (sh, i, 0))],
        out_specs=pl.BlockSpec((None, tr, cdim), lambda sh, i, cr: (sh, i, 0)))
    core = lax.axis_index("c").astype(jnp.int32).reshape(1)
    return _pcall(body, name=name, grid_spec=grid_spec, out_shape=jax.ShapeDtypeStruct((N_SHARD, r, cdim), BF),
                  compiler_params=_params(("arbitrary", "arbitrary")))(core, send, land)


def _sum_lead(name, arr, out_dtype=F32):
    n, r, cdim = arr.shape
    tr = r
    limit = (4 << 20) // (n * cdim * arr.dtype.itemsize)
    if r > limit:
        tr = _pick(r, max(limit, 16), 16)

    def body(x_ref, o_ref):
        acc = x_ref[0].astype(F32)
        for d in range(1, n):
            acc = acc + x_ref[d].astype(F32)
        o_ref[...] = acc.astype(out_dtype)

    return _pcall(body, name=name, grid=(r // tr,),
                  in_specs=[pl.BlockSpec((n, tr, cdim), lambda i: (0, i, 0))],
                  out_specs=pl.BlockSpec((tr, cdim), lambda i: (i, 0)),
                  out_shape=jax.ShapeDtypeStruct((r, cdim), out_dtype),
                  compiler_params=_params(("arbitrary",)))(arr)


def _rows_call(name, fn, n_rows, tm, rows, consts, mod, outs, acc_w=None, h_tiles=None):
    nt = n_rows // tm
    ht = nt if h_tiles is None else h_tiles
    ng = 1 if mod is None else mod.shape[0]
    n_r, n_c, n_o = len(rows), len(consts), len(outs)
    has_mod = mod is not None

    def body(*refs):
        i = pl.program_id(0)
        first = (i % ht) == 0
        row_refs, const_refs = refs[:n_r], refs[n_r:n_r + n_c]
        p = n_r + n_c
        mod_tile = refs[p][...] if has_mod else None
        p += int(has_mod)
        out_refs = refs[p:p + n_o]
        o, acc = fn([r[...] for r in row_refs], [r[...] for r in const_refs], mod_tile)
        for r, v in zip(out_refs, o):
            r[...] = v.astype(r.dtype)
        if acc_w is not None:
            acc_ref = refs[p + n_o]

            @pl.when(first)
            def _():
                acc_ref[...] = jnp.zeros_like(acc_ref)

            for k, v in acc.items():
                acc_ref[k:k + 1, :] += v

    in_specs = [pl.BlockSpec((tm, r.shape[1]), lambda i: (i, 0)) for r in rows]
    in_specs += [pl.BlockSpec(cst.shape, lambda i, nd=cst.ndim: (0,) * nd) for cst in consts]
    args = list(rows) + list(consts)
    if has_mod:
        in_specs.append(pl.BlockSpec((None,) + mod.shape[1:], lambda i: (i // ht, 0, 0)))
        args.append(mod)
    out_shape = [jax.ShapeDtypeStruct((n_rows, w), dt) for w, dt in outs]
    out_specs = [pl.BlockSpec((tm, w), lambda i: (i, 0)) for w, _ in outs]
    if acc_w is not None:
        out_shape.append(jax.ShapeDtypeStruct((ng, 8, acc_w), F32))
        out_specs.append(pl.BlockSpec((None, 8, acc_w), lambda i: (i // ht, 0, 0)))
    res = _pcall(body, name=name, grid=(nt,), in_specs=in_specs, out_specs=out_specs, out_shape=out_shape,
                 compiler_params=_params(("arbitrary",)))(*args)
    return list(res)


def _rms(s):
    r = lax.rsqrt(jnp.mean(s * s, axis=1, keepdims=True) + RMS_EPS)
    return s * r, r


def _rms_bwd(dn, n, r):
    return r * (dn - n * jnp.mean(dn * n, axis=1, keepdims=True))


def _adaln_fwd(name, s, gains, gain_row, mod, k, tm, h_tiles, after=None):
    def fn(rows, consts, m):
        n, _ = _rms(rows[0])
        y = n * consts[0][gain_row:gain_row + 1, :]
        return [y * (1.0 + m[3 * k + 1:3 * k + 2, :]) + m[3 * k:3 * k + 1, :]], {}

    d = s.shape[1]
    consts = [gains] if after is None else [gains, after]
    return _rows_call(name, fn, s.shape[0], tm, [s], consts, mod, [(d, BF)], h_tiles=h_tiles)[0]


def _adaln_bwd(name, s, du, ds_res, gains, gain_row, mod, k, tm, h_tiles):
    def fn(rows, consts, m):
        sv, duv, res = rows
        gain = consts[0][gain_row:gain_row + 1, :]
        n, r = _rms(sv)
        y = n * gain
        dy = duv * (1.0 + m[3 * k + 1:3 * k + 2, :])
        acc = {0: jnp.sum(duv, axis=0, keepdims=True), 1: jnp.sum(duv * y, axis=0, keepdims=True),
               2: jnp.sum(dy * n, axis=0, keepdims=True)}
        return [_rms_bwd(dy * gain, n, r) + res], acc

    d = s.shape[1]
    return _rows_call(name, fn, s.shape[0], tm, [s, du, ds_res], [gains], mod, [(d, F32)], acc_w=d, h_tiles=h_tiles)


def _resid_bwd(name, ds_out, o, mod, k, cst, tm, h_tiles, after=None):
    def fn(rows, consts, m):
        dsv, ov = rows
        gate = m[3 * k + 2:3 * k + 3, :]
        return [cst * gate * dsv], {0: jnp.sum(cst * ov * dsv, axis=0, keepdims=True)}

    d = o.shape[1]
    consts = [] if after is None else [after]
    return _rows_call(name, fn, o.shape[0], tm, [ds_out, o], consts, mod, [(d, BF)], acc_w=d, h_tiles=h_tiles)


def _mm(name, a, b, mode, tm=256, tn=512, out_dtypes=(F32,), epi=None, epi_args=(), epi_kinds=(), a_pre=None,
        b_lead=None):
    bshape = b.shape if b_lead is None else b.shape[1:]
    if mode == 'nn':
        (m, kd), nd = a.shape, bshape[1]
    elif mode == 'nt':
        (m, kd), nd = a.shape, bshape[0]
    else:
        (kd, m), nd = a.shape, bshape[1]
    tm = _pick(m, tm, 16) if m % tm else tm
    tn = _pick(nd, tn, 128) if nd % tn else tn
    dims = {'nn': NN, 'nt': NT, 'tn': TN}[mode]
    n_e, n_o = len(epi_args), len(out_dtypes)

    def body(*refs):
        i = pl.program_id(1)
        av = refs[0][...]
        if a_pre is not None:
            av = a_pre(av)
        acc = _dot(av, refs[1][...], dims)
        res = (acc,) if epi is None else epi(acc, i, *[r[...] for r in refs[2:2 + n_e]])
        for r, v in zip(refs[2 + n_e:], res):
            r[...] = v.astype(r.dtype)

    if mode == 'nn':
        specs = [pl.BlockSpec((tm, kd), lambda j, i: (i, 0)), pl.BlockSpec((kd, tn), lambda j, i: (0, j))]
    elif mode == 'nt':
        specs = [pl.BlockSpec((tm, kd), lambda j, i: (i, 0)), pl.BlockSpec((tn, kd), lambda j, i: (j, 0))]
    else:
        specs = [pl.BlockSpec((kd, tm), lambda j, i: (0, i)), pl.BlockSpec((kd, tn), lambda j, i: (0, j))]
    if b_lead is not None:
        shape2, at2 = specs[1].block_shape, specs[1].index_map
        specs[1] = pl.BlockSpec((None,) + tuple(shape2), lambda j, i: (b_lead,) + tuple(at2(j, i)))
    for arr, kind in zip(epi_args, epi_kinds):
        if kind == 'mn':
            specs.append(pl.BlockSpec((tm, tn), lambda j, i: (i, j)))
        elif kind == 'n':
            specs.append(pl.BlockSpec((1, tn), lambda j, i: (0, j)))
        elif kind == 'mt':
            specs.append(pl.BlockSpec((tm, arr.shape[1]), lambda j, i: (i, 0)))
        else:
            specs.append(pl.BlockSpec(arr.shape, lambda j, i, nd_=arr.ndim: (0,) * nd_))
    res = _pcall(body, name=name, grid=(nd // tn, m // tm), in_specs=specs,
                 out_specs=[pl.BlockSpec((tm, tn), lambda j, i: (i, j))] * n_o,
                 out_shape=[jax.ShapeDtypeStruct((m, nd), dt) for dt in out_dtypes],
                 compiler_params=_params(("arbitrary", "arbitrary")))(a, b, *epi_args)
    return res[0] if n_o == 1 else list(res)


def _row_gate(mod, k3, i, tm, n_lat):
    g0 = mod[0, k3:k3 + 1, :]
    if mod.shape[0] == 1:
        return g0
    rid = i * tm + lax.broadcasted_iota(jnp.int32, (tm, 1), 0)
    return jnp.where(rid < n_lat, g0, mod[1, k3:k3 + 1, :])


def _ffn_up(name, u, wg, wu, base, tm):
    r, d = u.shape
    nch, _, _, fc = wg.shape

    def body(u_ref, wg_ref, wu_ref, a_ref, b_ref, h_ref):
        uv = u_ref[...]
        a = _dot(uv, wg_ref[...], NN)
        b = _dot(uv, wu_ref[...], NN)
        a_ref[...] = a.astype(BF)
        b_ref[...] = b.astype(BF)
        h_ref[...] = (_silu(a) * b).astype(BF)

    chunk = pl.BlockSpec((None, tm, fc), lambda j, i: (j, i, 0))
    return _pcall(body, name=name, grid=(nch, r // tm),
                  in_specs=[pl.BlockSpec((tm, d), lambda j, i: (i, 0)),
                            pl.BlockSpec((None, None, d, fc), lambda j, i: (j, base, 0, 0)),
                            pl.BlockSpec((None, None, d, fc), lambda j, i: (j, base, 0, 0))],
                  out_specs=[chunk] * 3, out_shape=[jax.ShapeDtypeStruct((nch, r, fc), BF)] * 3,
                  compiler_params=_params(("arbitrary", "arbitrary")))(u, wg, wu)


def _ffn_down(name, hid, wd, wd_blk, s, mod, k, n_lat, tm):
    nch, r, fc = hid.shape
    d = wd.shape[2]

    def body(h_ref, w_ref, s_ref, m_ref, so_ref, o_ref):
        i = pl.program_id(0)
        o = _dot(h_ref[0], w_ref[0], NN)
        for j in range(1, nch):
            o = o + _dot(h_ref[j], w_ref[j], NN)
        o_ref[...] = o
        so_ref[...] = s_ref[...] + 0.5 * _row_gate(m_ref[...], 3 * k + 2, i, tm, n_lat) * o

    row = pl.BlockSpec((tm, d), lambda i: (i, 0))
    return _pcall(body, name=name, grid=(r // tm,),
                  in_specs=[pl.BlockSpec((nch, tm, fc), lambda i: (0, i, 0)),
                            pl.BlockSpec((nch, fc, d), lambda i: (0, wd_blk, 0)), row,
                            pl.BlockSpec(mod.shape, lambda i: (0, 0, 0))],
                  out_specs=[row, row], out_shape=[jax.ShapeDtypeStruct((r, d), F32)] * 2,
                  compiler_params=_params(("arbitrary",)))(hid, wd, s, mod)


def _ffn_dhid(name, d_o, wd, wd_blk, a, b, tm):
    r, d = d_o.shape
    nch, _, fc = a.shape

    def body(g_ref, w_ref, a_ref, b_ref, da_ref, db_ref):
        dh = _dot(g_ref[...], w_ref[...], NT)
        av, bv = a_ref[...].astype(F32), b_ref[...].astype(F32)
        da_ref[...] = (dh * bv * _dsilu(av)).astype(BF)
        db_ref[...] = (dh * _silu(av)).astype(BF)

    chunk = pl.BlockSpec((None, tm, fc), lambda j, i: (j, i, 0))
    return _pcall(body, name=name, grid=(nch, r // tm),
                  in_specs=[pl.BlockSpec((tm, d), lambda j, i: (i, 0)),
                            pl.BlockSpec((None, fc, d), lambda j, i: (j, wd_blk, 0)), chunk, chunk],
                  out_specs=[chunk] * 2, out_shape=[jax.ShapeDtypeStruct((nch, r, fc), BF)] * 2,
                  compiler_params=_params(("arbitrary", "arbitrary")))(d_o, wd, a, b)


def _ffn_du(name, da, db, wg, wu, base, tm):
    nch, r, fc = da.shape
    d = wg.shape[2]

    def body(da_ref, db_ref, wg_ref, wu_ref, o_ref):
        acc = _dot(da_ref[0], wg_ref[0], NT) + _dot(db_ref[0], wu_ref[0], NT)
        for j in range(1, nch):
            acc = acc + _dot(da_ref[j], wg_ref[j], NT) + _dot(db_ref[j], wu_ref[j], NT)
        o_ref[...] = acc

    chunks = pl.BlockSpec((nch, tm, fc), lambda i: (0, i, 0))
    held = pl.BlockSpec((nch, None, d, fc), lambda i: (0, base, 0, 0), pipeline_mode=pl.Buffered(1))
    return _pcall(body, name=name, grid=(r // tm,), in_specs=[chunks, chunks, held, held],
                  out_specs=pl.BlockSpec((tm, d), lambda i: (i, 0)), out_shape=jax.ShapeDtypeStruct((r, d), F32),
                  compiler_params=_params(("arbitrary",)))(da, db, wg, wu)


def _ffn_dw_in(name, u, da, db, tmm, g_gate, g_up, idx):
    r, d = u.shape
    nch, _, fc = da.shape
    nb = d // tmm

    def body(u_ref, a_ref, b_ref, gg_ref, gu_ref, og_ref, ou_ref):
        ut = u_ref[...].T
        og_ref[...] = _dot(ut, a_ref[...], NN).astype(og_ref.dtype)
        ou_ref[...] = _dot(ut, b_ref[...], NN).astype(ou_ref.dtype)

    chunk = pl.BlockSpec((None, r, fc), lambda j, mi: (j, 0, 0))
    out = pl.BlockSpec((None, tmm, fc), lambda j, mi: (j, idx * nb + mi, 0))
    return _pcall(body, name=name, grid=(nch, nb),
                  in_specs=[pl.BlockSpec((r, tmm), lambda j, mi: (0, mi)), chunk, chunk,
                            pl.BlockSpec(memory_space=pl.ANY), pl.BlockSpec(memory_space=pl.ANY)],
                  out_specs=[out, out],
                  out_shape=[jax.ShapeDtypeStruct(g_gate.shape, g_gate.dtype),
                             jax.ShapeDtypeStruct(g_up.shape, g_up.dtype)],
                  input_output_aliases={3: 0, 4: 1},
                  compiler_params=_params(("arbitrary", "arbitrary")))(u, da, db, g_gate, g_up)


def _ffn_dw_down(name, hid, d_o, tn, grads, idx):
    nch, r, fc = hid.shape
    d = d_o.shape[1]

    def body(h_ref, g_ref, acc_ref, o_ref):
        o_ref[...] = _dot(h_ref[...], g_ref[...], TN).astype(o_ref.dtype)

    return _pcall(body, name=name, grid=(nch, d // tn),
                  in_specs=[pl.BlockSpec((None, r, fc), lambda j, ni: (j, 0, 0)),
                            pl.BlockSpec((r, tn), lambda j, ni: (0, ni)),
                            pl.BlockSpec(memory_space=pl.ANY)],
                  out_specs=pl.BlockSpec((None, fc, tn), lambda j, ni: (j, idx, ni)),
                  out_shape=jax.ShapeDtypeStruct(grads.shape, grads.dtype), input_output_aliases={2: 0},
                  compiler_params=_params(("arbitrary", "arbitrary")))(hid, d_o, grads)


def _partner(x):
    n = x.shape[1]
    lane = lax.broadcasted_iota(jnp.int32, x.shape, 1)
    return jnp.where((lane & 15) < 8, pltpu.roll(x, n - 8, 1), pltpu.roll(x, 8, 1))


def _rope(x, ct, st):
    reps = x.shape[1] // ct.shape[1]
    if reps > 1:
        ct, st = jnp.tile(ct, (1, reps)), jnp.tile(st, (1, reps))
    return x * ct + _partner(x) * st


def _rope_t(dy, ct, st):
    reps = dy.shape[1] // ct.shape[1]
    if reps > 1:
        ct, st = jnp.tile(ct, (1, reps)), jnp.tile(st, (1, reps))
    return dy * ct + _partner(dy * st)


def _rope_tables(t_len, g_len, lane0):
    half = QK_ROPE // 4
    pos = jnp.arange(t_len)
    row = (pos // GRID_W).astype(F32)
    col = (pos % GRID_W).astype(F32)
    freqs = jnp.power(ROPE_THETA, -jnp.arange(0, QK_ROPE // 2, 2, dtype=F32) / (QK_ROPE // 2))
    ang_r, ang_c = row[:, None] * freqs, col[:, None] * freqs
    cs = jnp.concatenate([jnp.cos(ang_r)] * 2 + [jnp.cos(ang_c)] * 2, axis=1)
    sn = jnp.concatenate([-jnp.sin(ang_r), jnp.sin(ang_r), -jnp.sin(ang_c), jnp.sin(ang_c)], axis=1)
    assert cs.shape[1] == 4 * half == QK_ROPE
    def place(tab, fill):
        rest = HEAD_PAD - lane0 - QK_ROPE
        rows = jnp.concatenate([jnp.full((t_len, lane0), fill, F32), tab, jnp.full((t_len, rest), fill, F32)], axis=1)
        return jnp.concatenate([rows, jnp.full((g_len, HEAD_PAD), fill, F32)], axis=0)

    return place(cs, 1.0), place(sn, 0.0)


def _attn_fwd(name, q, kp, vp, n_q, q_off, n_k, k_blk, heads, tq, scale):
    qb = q_off // tq
    per = 2 if heads % 2 == 0 else 1
    wide = per * HEAD_PAD

    def body(q_ref, k_ref, v_ref, o_ref, l_ref):
        for e in range(per):
            sl = slice(e * HEAD_PAD, (e + 1) * HEAD_PAD)
            s = _dot(q_ref[:, sl], k_ref[:, sl], NT) * scale
            m = jnp.max(s, axis=1, keepdims=True)
            p = jnp.exp(s - m)
            l = jnp.sum(p, axis=1, keepdims=True)
            o_ref[:, sl] = (_dot(p, v_ref[:, sl], NN) / l).astype(BF)
            l_ref[:, sl] = jnp.broadcast_to(m + jnp.log(l), (tq, HEAD_PAD))

    hw = heads * HEAD_PAD
    blk = pl.BlockSpec((tq, wide), lambda h, i: (i, h))
    kv = pl.BlockSpec((n_k, wide), lambda h, i: (k_blk, h))
    return _pcall(body, name=name, grid=(heads // per, n_q // tq),
                  in_specs=[pl.BlockSpec((tq, wide), lambda h, i: (i + qb, h)), kv, kv],
                  out_specs=[blk, blk],
                  out_shape=[jax.ShapeDtypeStruct((n_q, hw), BF), jax.ShapeDtypeStruct((n_q, hw), F32)],
                  compiler_params=_params(("arbitrary", "arbitrary")))(q, kp, vp)


def _attn_bwd(name, q, kp, vp, cat, dcat, lse, n_q, q_off, n_k, k_blk, heads, tq, scale, col_blk, onto=None):
    qb = q_off // tq

    per = 1
    wide = per * HEAD_PAD

    def body(q_ref, k_ref, v_ref, o_ref, do_ref, l_ref, *rest):
        dq_ref, dk_ref, dv_ref = rest[-3:]
        i = pl.program_id(1)
        for e in range(per):
            sl = slice(e * HEAD_PAD, (e + 1) * HEAD_PAD)
            qv, kv_, vv = q_ref[:, sl], k_ref[:, sl], v_ref[:, sl]
            dov = do_ref[:, sl]
            s = _dot(qv, kv_, NT) * scale
            p = jnp.exp(s - l_ref[:, e * HEAD_PAD:e * HEAD_PAD + 1])
            dp = _dot(dov, vv, NT)
            delta = jnp.sum(dov * o_ref[:, sl].astype(F32), axis=1, keepdims=True)
            ds = (p * (dp - delta) * scale).astype(BF)
            dq_ref[:, sl] = _dot(ds, kv_, NN)
            dk = _dot(ds, qv, TN)
            dv = _dot(p, dov, TN)

            @pl.when(i == 0)
            def _():
                if onto is None:
                    dk_ref[:, sl] = dk
                    dv_ref[:, sl] = dv
                else:
                    dk_ref[:, sl] = rest[0][:, sl] + dk
                    dv_ref[:, sl] = rest[1][:, sl] + dv

            @pl.when(i > 0)
            def _():
                dk_ref[:, sl] += dk
                dv_ref[:, sl] += dv

    hw = heads * HEAD_PAD
    heads = heads // per
    col_blk = col_blk // per
    qspec = pl.BlockSpec((tq, wide), lambda h, i: (i + qb, h))
    cspec = pl.BlockSpec((tq, wide), lambda h, i: (i + qb, col_blk + h))
    kv = pl.BlockSpec((n_k, wide), lambda h, i: (k_blk, h))
    blk = pl.BlockSpec((tq, wide), lambda h, i: (i, h))
    if onto is None:
        acc = pl.BlockSpec((n_k, wide), lambda h, i: (0, h))
        return _pcall(body, name=name, grid=(heads, n_q // tq),
                      in_specs=[qspec, kv, kv, cspec, cspec, blk], out_specs=[blk, acc, acc],
                      out_shape=[jax.ShapeDtypeStruct((n_q, hw), F32), jax.ShapeDtypeStruct((n_k, hw), F32),
                                 jax.ShapeDtypeStruct((n_k, hw), F32)],
                      compiler_params=_params(("arbitrary", "arbitrary")))(q, kp, vp, cat, dcat, lse)
    return _pcall(body, name=name, grid=(heads, n_q // tq),
                  in_specs=[qspec, kv, kv, cspec, cspec, blk, kv, kv], out_specs=[blk, kv, kv],
                  out_shape=[jax.ShapeDtypeStruct((n_q, hw), F32)] + [jax.ShapeDtypeStruct(t.shape, F32) for t in onto],
                  input_output_aliases={6: 1, 7: 2},
                  compiler_params=_params(("arbitrary", "arbitrary")))(q, kp, vp, cat, dcat, lse, *onto)


def _shift(x, k):
    return pltpu.roll(x, k % x.shape[0], 0)


def _window_sum(v, w, mirrored):
    n, gd = v.shape
    pad = jnp.zeros((POOL_PAD, gd), F32)
    e = jnp.concatenate([pad, v, pad], axis=0)
    acc = e + _shift(e, -1 if mirrored else 1)
    step = 1
    while 2 * step < w:
        acc = _shift(acc, step) + _shift(acc, -step)
        step *= 2
    return acc[POOL_PAD:POOL_PAD + n]


def _window_count(n, w):
    t = lax.broadcasted_iota(jnp.int32, (n, 1), 0)
    lo = jnp.maximum(t - w // 2, 0)
    hi = jnp.minimum(t + (w - w // 2 - 1), n - 1)
    return (hi - lo + 1).astype(F32)


def _pool_fwd(name, u, pool_w, scale):
    n, pd = u.shape
    ng = len(POOL_WINDOWS)
    gd = pd // ng

    def body(u_ref, w_ref, s_ref, y_ref):
        for g, w in enumerate(POOL_WINDOWS):
            sl = slice(g * gd, (g + 1) * gd)
            ug = u_ref[:, sl]
            p = _window_sum(ug, w, False) / _window_count(n, w) - ug
            y_ref[:, sl] = (_dot(p, w_ref[g], NN) * s_ref[:, sl]).astype(BF)

    return _pcall(body, name=name, out_shape=jax.ShapeDtypeStruct((n, pd), BF),
                  compiler_params=_params())(u, pool_w, scale)


def _pool_bwd(name, u, dcat, pool_w, scale, row_off):
    n, pd = u.shape
    ng = len(POOL_WINDOWS)
    gd = pd // ng

    def body(u_ref, dy_ref, w_ref, s_ref, du_ref, dw_ref, ds_ref):
        ds_ref[...] = jnp.zeros_like(ds_ref)
        for g, w in enumerate(POOL_WINDOWS):
            sl = slice(g * gd, (g + 1) * gd)
            ug, dy, wg = u_ref[:, sl], dy_ref[:, sl], w_ref[g]
            cnt = _window_count(n, w)
            p = _window_sum(ug, w, False) / cnt - ug
            ds_ref[0:1, sl] = jnp.sum(dy * _dot(p, wg, NN), axis=0, keepdims=True)
            dys = dy * s_ref[:, sl]
            dw_ref[g] = _dot(p, dys, TN)
            dp = _dot(dys, wg, NT)
            du_ref[:, sl] = (_window_sum(dp / cnt, w, True) - dp).astype(BF)

    rb = row_off // n
    return _pcall(body, name=name, grid=(1,),
                  in_specs=[pl.BlockSpec((n, pd), lambda i: (0, 0)), pl.BlockSpec((n, pd), lambda i: (rb, 0)),
                            pl.BlockSpec(pool_w.shape, lambda i: (0, 0, 0)), pl.BlockSpec(scale.shape, lambda i: (0, 0))],
                  out_specs=[pl.BlockSpec((n, pd), lambda i: (0, 0)), pl.BlockSpec((ng, gd, gd), lambda i: (0, 0, 0)),
                             pl.BlockSpec((8, pd), lambda i: (0, 0))],
                  out_shape=[jax.ShapeDtypeStruct((n, pd), BF), jax.ShapeDtypeStruct((ng, gd, gd), F32),
                             jax.ShapeDtypeStruct((8, pd), F32)],
                  compiler_params=_params(("arbitrary",)))(u, dcat, pool_w, scale)


def _edge_shift(z, k):
    n = z.shape[0]
    t = lax.broadcasted_iota(jnp.int32, (n, 1), 0)
    keep = (t >= k) if k > 0 else (t < n + k)
    return jnp.where(keep, pltpu.roll(z, k % n, 0), 0.0)


def _conv_fwd(name, p3, cw, tc):
    n, cd = p3.shape[0], p3.shape[1] // 3
    nb = cd // tc

    def body(b_ref, c_ref, v_ref, w_ref, y_ref):
        z = c_ref[...] * v_ref[...]
        w = w_ref[...]
        zc = w[0:1] * _edge_shift(z, 1) + w[1:2] * z + w[2:3] * _edge_shift(z, -1)
        y_ref[...] = (b_ref[...] * zc).astype(BF)

    return _pcall(body, name=name, grid=(nb,),
                  in_specs=[pl.BlockSpec((n, tc), lambda j: (0, j)), pl.BlockSpec((n, tc), lambda j: (0, nb + j)),
                            pl.BlockSpec((n, tc), lambda j: (0, 2 * nb + j)), pl.BlockSpec((3, tc), lambda j: (0, j))],
                  out_specs=pl.BlockSpec((n, tc), lambda j: (0, j)), out_shape=jax.ShapeDtypeStruct((n, cd), BF),
                  compiler_params=_params(("arbitrary",)))(p3, p3, p3, cw)


def _conv_bwd(name, p3, cw, dy, tc):
    n, cd = dy.shape
    nb = cd // tc

    def body(b_ref, c_ref, v_ref, w_ref, dy_ref, dp_ref, dw_ref):
        cv, vv, w, dyv = c_ref[...], v_ref[...], w_ref[...], dy_ref[...]
        z = cv * vv
        zl, zr = _edge_shift(z, 1), _edge_shift(z, -1)
        zc = w[0:1] * zl + w[1:2] * z + w[2:3] * zr
        dzc = dyv * b_ref[...]
        dz = w[0:1] * _edge_shift(dzc, -1) + w[1:2] * dzc + w[2:3] * _edge_shift(dzc, 1)
        dp_ref[0] = (dyv * zc).astype(BF)
        dp_ref[1] = (dz * vv).astype(BF)
        dp_ref[2] = (dz * cv).astype(BF)
        dw_ref[...] = jnp.zeros_like(dw_ref)
        dw_ref[0:1, :] = jnp.sum(dzc * zl, axis=0, keepdims=True)
        dw_ref[1:2, :] = jnp.sum(dzc * z, axis=0, keepdims=True)
        dw_ref[2:3, :] = jnp.sum(dzc * zr, axis=0, keepdims=True)

    col = pl.BlockSpec((n, tc), lambda j: (0, j))
    return _pcall(body, name=name, grid=(nb,),
                  in_specs=[col, pl.BlockSpec((n, tc), lambda j: (0, nb + j)),
                            pl.BlockSpec((n, tc), lambda j: (0, 2 * nb + j)), pl.BlockSpec((3, tc), lambda j: (0, j)), col],
                  out_specs=[pl.BlockSpec((3, n, tc), lambda j: (0, 0, j)), pl.BlockSpec((8, tc), lambda j: (0, j))],
                  out_shape=[jax.ShapeDtypeStruct((3, n, cd), BF), jax.ShapeDtypeStruct((8, cd), F32)],
                  compiler_params=_params(("arbitrary",)))(p3, p3, p3, cw, dy)


def _conv_din(name, dp3, w_in, tm):
    _, n, cd = dp3.shape
    d = w_in.shape[0]

    def body(a_ref, w_ref, o_ref, acc_ref):
        j = pl.program_id(1)
        part = _dot(a_ref[...], w_ref[...], NT)

        @pl.when(j == 0)
        def _():
            acc_ref[...] = part

        @pl.when(j > 0)
        def _():
            acc_ref[...] += part

        @pl.when(j == 2)
        def _():
            o_ref[...] = acc_ref[...]

    return _pcall(body, name=name, grid=(n // tm, 3),
                  in_specs=[pl.BlockSpec((None, tm, cd), lambda i, j: (j, i, 0)),
                            pl.BlockSpec((d, cd), lambda i, j: (0, j))],
                  out_specs=pl.BlockSpec((tm, d), lambda i, j: (i, 0)), out_shape=jax.ShapeDtypeStruct((n, d), F32),
                  scratch_shapes=[pltpu.VMEM((tm, d), F32)],
                  compiler_params=_params(("arbitrary", "arbitrary")))(dp3, w_in)


def _conv_dw_in(name, u, dp3, tmm, tn):
    n, d = u.shape
    cd = dp3.shape[2]
    nb = cd // tn

    def body(u_ref, z_ref, o_ref):
        o_ref[...] = _dot(u_ref[...], z_ref[...], TN)

    return _pcall(body, name=name, grid=(3 * nb, d // tmm),
                  in_specs=[pl.BlockSpec((n, tmm), lambda j, mi: (0, mi)),
                            pl.BlockSpec((None, n, tn), lambda j, mi: (j // nb, 0, j % nb))],
                  out_specs=pl.BlockSpec((tmm, tn), lambda j, mi: (mi, j)),
                  out_shape=jax.ShapeDtypeStruct((d, 3 * cd), F32),
                  compiler_params=_params(("arbitrary", "arbitrary")))(u, dp3)


def _loss_head(name, h, target, gain, tm):
    d = h.shape[1]

    def fn(rows, consts, m):
        hv, tv = rows
        g = consts[0][0:1, :]
        n, r = _rms(hv)
        err = n * g - tv
        dy = err / d
        loss = 0.5 * jnp.sum(err * err) / d
        acc = {0: jnp.sum(dy * n, axis=0, keepdims=True), 1: jnp.full((1, d), loss, F32)}
        return [_rms_bwd(dy * g, n, r)], acc

    return _rows_call(name, fn, h.shape[0], tm, [h, target], [gain], None, [(d, F32)], acc_w=d)


def _adamw(name, w, g, m, v, after=None):
    shape = w.shape
    if w.ndim == 1:
        shape2 = (1,) + shape
        res = _adamw(name, *[t.reshape(shape2) for t in (w, g, m, v)], after=after)
        return [t.reshape(shape) for t in res]
    if shape[-1] % 128 and shape[-2] % 128 == 0:
        res = _adamw(name, *[jnp.swapaxes(t, -1, -2) for t in (w, g, m, v)], after=after)
        return [jnp.swapaxes(t, -1, -2) for t in res]
    lead, (r, cdim) = shape[:-2], shape[-2:]
    tr = r
    if r * cdim * 4 > (3 << 19):
        tr = _pick(r, max(8, (3 << 19) // (cdim * 4)), 8)
    c1 = 1.0 / (1.0 - ADAM_B1 ** ADAM_STEP)
    c2 = 1.0 / (1.0 - ADAM_B2 ** ADAM_STEP)
    nl = len(lead)

    def body(w_ref, g_ref, m_ref, v_ref, *rest):
        d_ref, nm_ref, nv_ref = rest[-3:]
        gv = g_ref[...]
        nm = ADAM_B1 * m_ref[...] + (1.0 - ADAM_B1) * gv
        nv = ADAM_B2 * v_ref[...] + (1.0 - ADAM_B2) * (gv * gv)
        nm_ref[...] = nm
        nv_ref[...] = nv
        d_ref[...] = -ADAM_LR * ((nm * c1) / (jnp.sqrt(nv * c2) + ADAM_EPS) + ADAM_WD * w_ref[...])

    spec = pl.BlockSpec((None,) * nl + (tr, cdim), lambda *idx: idx + (0,))
    extra = [] if after is None else [after]
    res = _pcall(body, name=name, grid=lead + (r // tr,),
                 in_specs=[spec] * 4 + [pl.BlockSpec(memory_space=pl.ANY)] * len(extra), out_specs=[spec] * 3,
                 out_shape=[jax.ShapeDtypeStruct(shape, F32)] * 3,
                 compiler_params=_params(("arbitrary",) * (nl + 1)))(w, g, m, v, *extra)
    return list(res)


def _adamw_piece(name, w, g_piece, m, v, at, outs, after=None):
    shape = w.shape
    nl = len(at)
    r, cdim = shape[-2:]
    assert shape[nl:] == g_piece.shape and len(shape) == nl + 2
    tr = _pick(r, max(8, (3 << 19) // (cdim * 4)), 8) if r * cdim * 4 > (3 << 19) else r
    c1 = 1.0 / (1.0 - ADAM_B1 ** ADAM_STEP)
    c2 = 1.0 / (1.0 - ADAM_B2 ** ADAM_STEP)
    if outs is None:
        outs = [lax.empty(shape, F32) for _ in range(4)]

    def body(w_ref, g_ref, m_ref, v_ref, *rest):
        go_ref, d_ref, nm_ref, nv_ref = rest[-4:]
        gv = g_ref[...]
        nm = ADAM_B1 * m_ref[...] + (1.0 - ADAM_B1) * gv
        nv = ADAM_B2 * v_ref[...] + (1.0 - ADAM_B2) * (gv * gv)
        go_ref[...] = gv
        nm_ref[...] = nm
        nv_ref[...] = nv
        d_ref[...] = -ADAM_LR * ((nm * c1) / (jnp.sqrt(nv * c2) + ADAM_EPS) + ADAM_WD * w_ref[...])

    full = pl.BlockSpec((None,) * nl + (tr, cdim), lambda i: tuple(at) + (i, 0))
    extra = [] if after is None else [after]
    res = _pcall(body, name=name, grid=(r // tr,),
                 in_specs=[full, pl.BlockSpec((tr, cdim), lambda i: (i, 0)), full, full]
                 + [pl.BlockSpec(memory_space=pl.ANY)] * (4 + len(extra)),
                 out_specs=[full] * 4, out_shape=[jax.ShapeDtypeStruct(shape, F32)] * 4,
                 input_output_aliases={4 + j: j for j in range(4)},
                 compiler_params=_params(("arbitrary",)))(w, g_piece, m, v, *outs, *extra)
    return list(res)


def _ffn_half_fwd(tag, s, gains, mod, k, wts, n_lat, tm, h_tiles, tm_big, after=None):
    wg, wu, wd, idx = wts
    u = _adaln_fwd(f"adaln_{tag}", s, gains, k, mod, k, tm, h_tiles, after)
    a, b, hid = _ffn_up(f"ffn_up_{tag}", u, wg, wu, idx, tm_big)
    s_out, o = _ffn_down(f"ffn_down_{tag}", hid, wd, idx, s, mod, k, n_lat, tm_big)
    return s_out, (s, u, a, b, hid, o)


def _ffn_half_bwd(tag, ds_out, saved, gains, mod, k, wts, big_grads, tm, h_tiles, tm_big, after=None):
    wg, wu, wd, idx = wts
    g_gate, g_up, g_down = big_grads
    s, u, a, b, hid, o = saved
    d_o, acc_g = _resid_bwd(f"resid_bwd_{tag}", ds_out, o, mod, k, 0.5, tm, h_tiles, after)
    da, db = _ffn_dhid(f"ffn_dhid_{tag}", d_o, wd, idx, a, b, tm_big)
    du = _ffn_du(f"ffn_du_{tag}", da, db, wg, wu, idx, tm_big)
    d = u.shape[1]
    g_gate, g_up = _ffn_dw_in(f"ffn_dwgu_{tag}", u, da, db, _pick(d, MM_ROWS), g_gate, g_up, idx)
    g_down = _ffn_dw_down(f"ffn_dwd_{tag}", hid, d_o, _pick(d, 512), g_down, idx)
    ds, acc_n = _adaln_bwd(f"adaln_bwd_{tag}", s, du, ds_out, gains, k, mod, k, tm, h_tiles)
    return ds, (g_gate, g_up, g_down), (acc_n[:, 0], acc_n[:, 1], acc_g[:, 0]), jnp.sum(acc_n[:, 2], axis=0)


def kernel(x, c, ctx, c_ctx, norm_g, w_mod, b_mod, ffn_w_gate, ffn_w_up, ffn_w_down, ab_w_in, pool_w, pool_scale, q_norm_g, w_uq, kv_norm_g, w_ukv, ab_w_out, conv_w_in, conv_w, conv_w_out, final_norm_g, loss_target, m_c_ctx, m_norm_g, m_w_mod, m_b_mod, m_ffn_w_gate, m_ffn_w_up, m_ffn_w_down, m_ab_w_in, m_pool_w, m_pool_scale, m_q_norm_g, m_w_uq, m_kv_norm_g, m_w_ukv, m_ab_w_out, m_conv_w_in, m_conv_w, m_conv_w_out, m_final_norm_g, v_c_ctx, v_norm_g, v_w_mod, v_b_mod, v_ffn_w_gate, v_ffn_w_up, v_ffn_w_down, v_ab_w_in, v_pool_w, v_pool_scale, v_q_norm_g, v_w_uq, v_kv_norm_g, v_w_ukv, v_ab_w_out, v_conv_w_in, v_conv_w, v_conv_w_out, v_final_norm_g):
    weights = dict(c_ctx=c_ctx, norm_g=norm_g, w_mod=w_mod, b_mod=b_mod, ffn_w_gate=ffn_w_gate, ffn_w_up=ffn_w_up,
                   ffn_w_down=ffn_w_down, ab_w_in=ab_w_in, pool_w=pool_w, pool_scale=pool_scale, q_norm_g=q_norm_g,
                   w_uq=w_uq, kv_norm_g=kv_norm_g, w_ukv=w_ukv, ab_w_out=ab_w_out, conv_w_in=conv_w_in, conv_w=conv_w,
                   conv_w_out=conv_w_out, final_norm_g=final_norm_g)
    mom_m = dict(c_ctx=m_c_ctx, norm_g=m_norm_g, w_mod=m_w_mod, b_mod=m_b_mod, ffn_w_gate=m_ffn_w_gate,
                 ffn_w_up=m_ffn_w_up, ffn_w_down=m_ffn_w_down, ab_w_in=m_ab_w_in, pool_w=m_pool_w,
                 pool_scale=m_pool_scale, q_norm_g=m_q_norm_g, w_uq=m_w_uq, kv_norm_g=m_kv_norm_g, w_ukv=m_w_ukv,
                 ab_w_out=m_ab_w_out, conv_w_in=m_conv_w_in, conv_w=m_conv_w, conv_w_out=m_conv_w_out,
                 final_norm_g=m_final_norm_g)
    mom_v = dict(c_ctx=v_c_ctx, norm_g=v_norm_g, w_mod=v_w_mod, b_mod=v_b_mod, ffn_w_gate=v_ffn_w_gate,
                 ffn_w_up=v_ffn_w_up, ffn_w_down=v_ffn_w_down, ab_w_in=v_ab_w_in, pool_w=v_pool_w,
                 pool_scale=v_pool_scale, q_norm_g=v_q_norm_g, w_uq=v_w_uq, kv_norm_g=v_kv_norm_g, w_ukv=v_w_ukv,
                 ab_w_out=v_ab_w_out, conv_w_in=v_conv_w_in, conv_w=v_conv_w, conv_w_out=v_conv_w_out,
                 final_norm_g=v_final_norm_g)

    t_len, d = x.shape[1], x.shape[2]
    g_len = ctx.shape[1]
    r_len = t_len + g_len
    fc = ffn_w_gate.shape[3]
    heads = d // 128
    pool_dim = d // 2
    q_rank, kv_rank = q_norm_g.shape[1], kv_norm_g.shape[1]
    hw = heads * HEAD_PAD
    attn_scale = 1.0 / math.sqrt(QK_NOPE + QK_ROPE)
    kvr_w = kv_rank + HEAD_PAD
    in_w = pool_dim + q_rank + kvr_w
    tm = 256 if g_len % 256 == 0 else g_len
    assert t_len % tm == 0 and g_len % tm == 0 and t_len % g_len == 0 and pool_dim % 128 == 0
    h_tiles = t_len // tm
    tm_l0 = _pick(r_len, 768, tm)
    tm_l1 = _pick(t_len, 1024, tm)

    xi, yi, ci = lax.axis_index("x"), lax.axis_index("y"), lax.axis_index("c")
    me = 4 * xi + 2 * yi + ci
    shard = 2 * xi + yi

    def halves(w):
        return w.astype(BF).reshape(2, -1, w.shape[-1])

    ffn_names = ["ffn_w_gate", "ffn_w_up", "ffn_w_down"]
    mixer_names = [["ab_w_in", "w_uq", "w_ukv", "ab_w_out"], ["conv_w_in", "conv_w_out"]]
    big_names = ffn_names + mixer_names[0] + mixer_names[1]

    def stage_names(k):
        return mixer_names[k // 3] if k % 3 == 1 else ffn_names

    def stage_halves(k):
        l, f = k // 3, (k % 3) // 2
        if k % 3 == 1:
            return [halves(weights[nm]) for nm in mixer_names[l]]
        return [halves(weights[nm][l, f]) for nm in ffn_names]

    def gather_start(k, dep):
        own = lax.optimization_barrier((tuple(stage_halves(k)), dep))[0]
        n = len(own)
        return _split_start(f"gather_start_s{k}", list(own) + _landing(N_DEV, own), n * len(CHIP_FLIPS),
                            _chips_gather_build(n))

    gather0 = gather_start(0, c)

    small = jnp.concatenate([norm_g.reshape(6, -1), conv_w[0]], axis=0)
    small = jnp.pad(small, ((0, 7), (0, 0)))
    c_row = jnp.pad(c, ((0, 7), (0, 0))) + gather0['token'][0, 0]
    small_all, c_all = _gather_all("gather_small", [small, c_row])
    small_full = small_all[::2].transpose(1, 0, 2).reshape(16, d)
    gains = [jnp.pad(small_full[3 * l:3 * l + 3], ((0, 5), (0, 0))) for l in range(2)]
    conv_w_full = small_full[6:9]
    c16 = jnp.concatenate([c_all[:, 0], c_ctx[None], jnp.zeros((7, d), F32)], axis=0)

    n_col = w_mod.shape[2]
    b_sh = lax.dynamic_slice_in_dim(b_mod, shard * n_col, n_col, axis=1)
    m_sh = [_mm(f"mod_fwd_{l}", c16, w_mod, 'nn', tm=16, tn=768, a_pre=_silu, b_lead=l,
                epi=lambda acc, i, bv: (acc + bv,), epi_args=(b_sh[l:l + 1],), epi_kinds=('n',)) for l in range(2)]
    m_all = _gather_all("gather_mod", [jnp.concatenate(m_sh, axis=0)], two_level=True)[0]
    m_full = m_all[::2].reshape(N_SHARD, 2, 16, n_col).transpose(1, 2, 0, 3).reshape(2, 16, N_MOD * d)
    mod_h = [jnp.pad(lax.dynamic_index_in_dim(m_full[l], me, 0, keepdims=False).reshape(N_MOD, d), ((0, 7), (0, 0)))
             for l in range(2)]
    mod_g0 = jnp.pad(m_full[0, 8].reshape(N_MOD, d), ((0, 7), (0, 0)))
    mods = [jnp.stack([mod_h[0], mod_g0]), mod_h[1][None]]

    def stage_weights(k, handle, after):
        bufs = _split_wait(f"gather_wait_s{k}", handle, after)
        n = len(bufs) // 2
        own = bufs[:n]
        fwd = _split_start(f"forward_start_s{k}", bufs[n:], n * len(CHIP_FLIPS), _sibling_forward_build(n))
        nxt = gather_start(k + 1, fwd['token']) if k + 1 < 6 else None
        landed = _split_wait(f"forward_wait_s{k}", fwd, fwd['token'])
        full = [lax.dynamic_update_slice_in_dim(z, a, 2 * shard, 0) for z, a in zip(landed, own)]
        gw = {nm: g.reshape(N_SHARD, 2 * g.shape[1], g.shape[2]) for nm, g in zip(stage_names(k), full)}
        return gw, nxt, (fwd['token'] if nxt is None else nxt['token'])

    def ffn_weights(gw):
        return gw["ffn_w_gate"].reshape(N_SHARD, 1, d, fc), gw["ffn_w_up"].reshape(N_SHARD, 1, d, fc), \
            gw["ffn_w_down"], 0

    gw_s0, gather1, tok0 = stage_weights(0, gather0, m_all)
    ffn_w = [[ffn_weights(gw_s0), None], [None, None]]

    s0 = jnp.concatenate([x[0], ctx[0]], axis=0)
    s1, sav_f00 = _ffn_half_fwd("l0a", s0, gains[0], mods[0], 0, ffn_w[0][0], t_len, tm, h_tiles, tm_l0, tok0)

    gw_s1, gather2, tok1 = stage_weights(1, gather1, s1)
    w_out_full = gw_s1["ab_w_out"].reshape(-1, d)
    w_uq_full = gw_s1["w_uq"].reshape(q_rank, heads * (QK_NOPE + QK_ROPE))
    w_ukv_full = gw_s1["w_ukv"].transpose(1, 0, 2).reshape(kv_rank, heads * (QK_NOPE + V_HEAD))
    w_in_full = gw_s1["ab_w_in"].transpose(1, 0, 2).reshape(d, -1)

    wq_p = jnp.pad(w_uq_full.reshape(q_rank, heads, QK_NOPE + QK_ROPE),
                   ((0, 0), (0, 0), (0, HEAD_PAD - QK_NOPE - QK_ROPE))).reshape(q_rank, hw)
    ukv3 = w_ukv_full.reshape(kv_rank, heads, QK_NOPE + V_HEAD)
    wk_top = jnp.pad(ukv3[..., :QK_NOPE], ((0, 0), (0, 0), (0, HEAD_PAD - QK_NOPE))).reshape(kv_rank, hw)
    wv_top = jnp.pad(ukv3[..., QK_NOPE:], ((0, 0), (0, 0), (0, HEAD_PAD - V_HEAD))).reshape(kv_rank, hw)
    src_row = lax.broadcasted_iota(jnp.int32, (HEAD_PAD, hw), 0)
    dst_lane = lax.broadcasted_iota(jnp.int32, (HEAD_PAD, hw), 1) % HEAD_PAD
    spread = ((src_row < QK_ROPE) & (dst_lane == src_row + QK_NOPE)).astype(BF)
    wk_ext = jnp.concatenate([wk_top, spread], axis=0)
    wv_ext = jnp.concatenate([wv_top, jnp.zeros((HEAD_PAD, hw), BF)], axis=0)
    w_in_pool = w_in_full[:, :pool_dim]
    w_in_q = w_in_full[:, pool_dim:pool_dim + q_rank]
    w_in_kvr = jnp.pad(w_in_full[:, pool_dim + q_rank:], ((0, 0), (0, HEAD_PAD - QK_ROPE)))
    w_out_attn = jnp.pad(w_out_full[pool_dim:].reshape(heads, V_HEAD, d),
                         ((0, 0), (0, HEAD_PAD - V_HEAD), (0, 0))).reshape(hw, d)
    w_out_p = jnp.concatenate([w_out_full[:pool_dim], w_out_attn], axis=0)

    u_mix = _adaln_fwd("adaln_l0m", s1, gains[0], 1, mods[0], 1, tm, h_tiles, tok1)
    p_pool = _mm("in_pool", u_mix, w_in_pool, 'nn', tm=tm_l0, tn=pool_dim)
    p_q = _mm("in_q", u_mix, w_in_q, 'nn', tm=tm_l0, tn=q_rank)
    p_kvr = _mm("in_kvr", u_mix, w_in_kvr, 'nn', tm=tm_l0, tn=kvr_w)
    qg = jnp.pad(q_norm_g, ((0, 7), (0, 0)))
    kvg = jnp.pad(kv_norm_g, ((0, 7), (0, 0)))
    tq_c, tq_s = _rope_tables(t_len, g_len, QK_NOPE)
    tk_c, tk_s = _rope_tables(t_len, g_len, 0)

    def qn_fn(rows, consts, m):
        n, _ = _rms(rows[0])
        return [n * consts[0][0:1, :]], {}

    qn = _rows_call("q_norm", qn_fn, r_len, tm, [p_q], [qg], None, [(q_rank, BF)])[0]
    q_r = _mm("q_up", qn, wq_p, 'nn', tm=tm_l0, tn=hw, out_dtypes=(BF,),
              epi=lambda acc, i, ct, st: (_rope(acc, ct, st),), epi_args=(tq_c, tq_s), epi_kinds=('mt', 'mt'))

    def kvn_fn(rows, consts, m):
        pv, ct, st = rows
        n, _ = _rms(pv[:, :kv_rank])
        return [jnp.concatenate([n * consts[0][0:1, :], _rope(pv[:, kv_rank:], ct, st)], axis=1)], {}

    kvn = _rows_call("kv_norm", kvn_fn, r_len, tm, [p_kvr, tk_c, tk_s], [kvg], None, [(kvr_w, BF)])[0]
    k_p = _mm("k_up", kvn, wk_ext, 'nn', tm=tm_l0, tn=hw, out_dtypes=(BF,))
    v_p = _mm("v_up", kvn, wv_ext, 'nn', tm=tm_l0, tn=hw, out_dtypes=(BF,))
    tq_h = _pick(t_len, 512, tm)
    o_h, lse_h = _attn_fwd("attn_h", q_r, k_p, v_p, t_len, 0, r_len, 0, heads, tm, attn_scale)
    o_g, lse_g = _attn_fwd("attn_g", q_r, k_p, v_p, g_len, t_len, g_len, t_len // g_len, heads, tm, attn_scale)
    y_h = _pool_fwd("pool_h", p_pool[:t_len], pool_w[0], pool_scale)
    y_g = _pool_fwd("pool_g", p_pool[t_len:], pool_w[0], pool_scale)
    cat = jnp.concatenate([jnp.concatenate([y_h, y_g], axis=0), jnp.concatenate([o_h, o_g], axis=0)], axis=1)

    def resid_epi(k3, n_lat, tmr):
        def epi(acc, i, sv, mv):
            return sv + _row_gate(mv, k3, i, tmr, n_lat) * acc, acc
        return epi

    s2, o_mix0 = _mm("mix_out_l0", cat, w_out_p, 'nn', tm=tm_l0, tn=d, out_dtypes=(F32, F32),
                     epi=resid_epi(5, t_len, tm_l0), epi_args=(s1, mods[0]), epi_kinds=('mn', 'w'))
    gw_s2, gather3, tok2 = stage_weights(2, gather2, s2)
    ffn_w[0][1] = ffn_weights(gw_s2)
    s3, sav_f01 = _ffn_half_fwd("l0b", s2, gains[0], mods[0], 2, ffn_w[0][1], t_len, tm, h_tiles, tm_l0, tok2)

    gw_s3, gather4, tok3 = stage_weights(3, gather3, s3)
    ffn_w[1][0] = ffn_weights(gw_s3)
    tml = 256 if t_len % 256 == 0 else tm
    h3 = s3[:t_len]
    h4, sav_f10 = _ffn_half_fwd("l1a", h3, gains[1], mods[1], 0, ffn_w[1][0], t_len, tml, None, tm_l1, tok3)
    gw_s4, gather5, tok4 = stage_weights(4, gather4, h4)
    cw_out_full = gw_s4["conv_w_out"].reshape(-1, d)
    cw_in_full = gw_s4["conv_w_in"].transpose(1, 0, 2).reshape(d, -1)
    u_cv = _adaln_fwd("adaln_l1m", h4, gains[1], 1, mods[1], 1, tml, None, tok4)
    p3 = _mm("conv_in", u_cv, cw_in_full, 'nn', tm=tm_l1, tn=512)
    cwp = conv_w_full
    tc = _pick(d, 256)
    y_cv = _conv_fwd("conv_fwd", p3, cwp, tc)
    h5, o_mix1 = _mm("mix_out_l1", y_cv, cw_out_full, 'nn', tm=tm_l1, tn=d, out_dtypes=(F32, F32),
                     epi=resid_epi(5, t_len, tm_l1), epi_args=(h4, mods[1]), epi_kinds=('mn', 'w'))
    gw_s5, _, tok5 = stage_weights(5, gather5, h5)
    ffn_w[1][1] = ffn_weights(gw_s5)
    h6, sav_f11 = _ffn_half_fwd("l1b", h5, gains[1], mods[1], 2, ffn_w[1][1], t_len, tml, None, tm_l1, tok5)

    fg = jnp.pad(final_norm_g[None], ((0, 7), (0, 0)))
    dh6, acc_loss = _loss_head("loss_head", h6, loss_target[0], fg, tml)
    d_final_g = acc_loss[0, 0]

    dgain = [[None] * 3 for _ in range(2)]
    dmod = [[None] * N_MOD for _ in range(2)]

    def put(l, k, triple):
        dmod[l][3 * k], dmod[l][3 * k + 1], dmod[l][3 * k + 2] = triple

    def empty_ffn_grads():
        return (lax.empty((N_SHARD, d, fc), BF), lax.empty((N_SHARD, d, fc), BF), lax.empty((N_SHARD, fc, d), BF))

    def by_shard_rows(g):
        return g.reshape(N_SHARD, -1, g.shape[-1])

    def by_shard_cols(g):
        return g.reshape(g.shape[0], N_SHARD, -1).transpose(1, 0, 2)

    def pair_start(k, big):
        send = [b.astype(BF).reshape(N_DEV, b.shape[1] // 2, b.shape[2]) for b in big]
        n = len(send)
        return _split_start(f"grads_pair_start_s{k}", send + _landing(N_SHARD, send), n * N_SHARD,
                            _sibling_halves_build(n))

    def chips_start(k, handle, after):
        bufs = _split_wait(f"grads_pair_wait_s{k}", handle, after)
        n = len(bufs) // 2
        pre = [_add_halves(f"grads_add_s{k}_{nm}", s, z) for nm, s, z in zip(stage_names(k), bufs[:n], bufs[n:])]
        return _split_start(f"grads_start_s{k}", pre + _landing(N_SHARD, pre), n * len(CHIP_FLIPS),
                            _chips_scatter_build(n))

    def landed_sums(k, handle, after):
        bufs = _split_wait(f"grads_wait_s{k}", handle, after)
        n = len(bufs) // 2
        landed = [lax.dynamic_update_slice_in_dim(z, lax.dynamic_slice_in_dim(p, shard, 1, 0), shard, 0)
                  for p, z in zip(bufs[:n], bufs[n:])]
        return [_sum_lead(f"sum_grads_s{k}_{nm}", z) for nm, z in zip(stage_names(k), landed)]

    pair, scatter, sums = [None] * 6, [None] * 6, [None] * 6
    grads, upd = {}, {}
    ffn_out = {nm: None for nm in ffn_names}

    def lanes_last(nm, t):
        shp = weights[nm].shape
        return jnp.swapaxes(t, -1, -2) if shp[-1] % 128 and shp[-2] % 128 == 0 else t

    def finish_stage(k, halves):
        n = len(halves)
        lands = [pltpu.with_memory_space_constraint(lax.empty(h.shape, h.dtype), pltpu.HBM) for h in halves]
        swap = _split_start(f"swap_start_s{k}", list(halves) + lands, n, _sibling_whole_build(n))
        both = _split_wait(f"swap_wait_s{k}", swap, swap['token'])
        for nm, a, g in zip(stage_names(k), both[:n], both[n:]):
            piece = jnp.where(ci == 0, jnp.concatenate([a, g], axis=0), jnp.concatenate([g, a], axis=0))
            if k % 3 == 1:
                grads[nm] = piece.reshape(weights[nm].shape)
                upd[nm] = _adamw(f"adamw_{nm}", weights[nm], grads[nm], mom_m[nm], mom_v[nm])
            else:
                ffn_out[nm] = _adamw_piece(f"adamw_{nm}_s{k}", lanes_last(nm, weights[nm]), lanes_last(nm, piece),
                                           lanes_last(nm, mom_m[nm]), lanes_last(nm, mom_v[nm]),
                                           (k // 3, (k % 3) // 2), ffn_out[nm])
    dh5, ffn_g, tr, dgain[1][2] = _ffn_half_bwd("l1b", dh6, sav_f11, gains[1], mods[1], 2, ffn_w[1][1],
                                                empty_ffn_grads(), tml, None, tm_l1)
    put(1, 2, tr)
    pair[5] = pair_start(5, list(ffn_g))
    d_o1, acc_g1 = _resid_bwd("resid_bwd_l1m", dh5, o_mix1, mods[1], 1, 1.0, tml, None, pair[5]['token'])
    dy_cv = _mm("mix_out_l1_dx", d_o1, cw_out_full, 'nt', tm=tm_l1, tn=d)
    d_cw_out = _mm("mix_out_l1_dw", y_cv, d_o1, 'tn', tm=512, tn=512)
    dp3, d_cw = _conv_bwd("conv_bwd", p3, cwp, dy_cv, tc)
    du_cv = _conv_din("conv_in_dx", dp3, cw_in_full, tm_l1)
    d_cw_in = _conv_dw_in("conv_in_dw", u_cv, dp3, _pick(d, MM_ROWS), _pick(d, 512))
    dh4, acc_n1 = _adaln_bwd("adaln_bwd_l1m", h4, du_cv, dh5, gains[1], 1, mods[1], 1, tml, None)
    put(1, 1, (acc_n1[:, 0], acc_n1[:, 1], acc_g1[:, 0]))
    dgain[1][1] = acc_n1[0, 2]
    scatter[5] = chips_start(5, pair[5], dh4)
    pair[4] = pair_start(4, [by_shard_cols(d_cw_in), by_shard_rows(d_cw_out)])
    dh3, ffn_g, tr, dgain[1][0] = _ffn_half_bwd("l1a", dh4, sav_f10, gains[1], mods[1], 0, ffn_w[1][0],
                                                empty_ffn_grads(), tml, None, tm_l1,
                                                scatter[5]['token'] + pair[4]['token'])
    put(1, 0, tr)
    finish_stage(5, landed_sums(5, scatter[5], dh3))
    scatter[4] = chips_start(4, pair[4], dh3)
    pair[3] = pair_start(3, list(ffn_g))

    ds3 = jnp.concatenate([dh3, jnp.zeros((g_len, d), F32)], axis=0) \
        + (scatter[4]['token'][0, 0] + pair[3]['token'][0, 0])
    ds2, ffn_g, tr, dgain[0][2] = _ffn_half_bwd("l0b", ds3, sav_f01, gains[0], mods[0], 2, ffn_w[0][1],
                                                empty_ffn_grads(), tm, h_tiles, tm_l0)
    put(0, 2, tr)
    finish_stage(4, landed_sums(4, scatter[4], ds2))
    scatter[3] = chips_start(3, pair[3], ds2)
    pair[2] = pair_start(2, list(ffn_g))
    d_o0, acc_g0 = _resid_bwd("resid_bwd_l0m", ds2, o_mix0, mods[0], 1, 1.0, tm, h_tiles,
                              scatter[3]['token'] + pair[2]['token'])
    dcat = _mm("mix_out_l0_dx", d_o0, w_out_p, 'nt', tm=tm_l0, tn=pool_dim + hw)
    d_w_out_p = _mm("mix_out_l0_dw", cat, d_o0, 'tn', tm=512, tn=512)
    col_blk = pool_dim // HEAD_PAD
    dq_h, dk_h, dv_h = _attn_bwd("attn_bwd_h", q_r, k_p, v_p, cat, dcat, lse_h, t_len, 0, r_len, 0, heads, tq_h,
                                 attn_scale, col_blk)
    dq_g, dk_all, dv_all = _attn_bwd("attn_bwd_g", q_r, k_p, v_p, cat, dcat, lse_g, g_len, t_len, g_len,
                                     t_len // g_len, heads, tm, attn_scale, col_blk, onto=(dk_h, dv_h))
    dq_all = jnp.concatenate([dq_h, dq_g], axis=0)
    dkvn = _mm("k_up_dx", dk_all, wk_ext, 'nt', tm=tm_l0, tn=kvr_w)
    dkvn = _mm("v_up_dx", dv_all, wv_ext, 'nt', tm=tm_l0, tn=kvr_w, epi=lambda acc, i, prev: (acc + prev,),
               epi_args=(dkvn,), epi_kinds=('mn',))
    d_wk_ext = _mm("k_up_dw", kvn, dk_all, 'tn', tm=kvr_w, tn=512)
    d_wv_ext = _mm("v_up_dw", kvn, dv_all, 'tn', tm=kvr_w, tn=512)

    def kvn_bwd_fn(rows, consts, m):
        pv, dv_, ct, st = rows
        g = consts[0][0:1, :]
        n, r = _rms(pv[:, :kv_rank])
        dyn = dv_[:, :kv_rank]
        dckv = _rms_bwd(dyn * g, n, r)
        dkr = _rope_t(dv_[:, kv_rank:], ct, st)
        return [jnp.concatenate([dckv, dkr], axis=1)], {0: jnp.sum(dyn * n, axis=0, keepdims=True)}

    dp_kvr, acc_kvg = _rows_call("kv_norm_bwd", kvn_bwd_fn, r_len, tm, [p_kvr, dkvn, tk_c, tk_s], [kvg], None,
                                 [(kvr_w, BF)], acc_w=kv_rank)

    def qrope_bwd_fn(rows, consts, m):
        return [_rope_t(rows[0], rows[1], rows[2])], {}

    dq_pad = _rows_call("q_rope_bwd", qrope_bwd_fn, r_len, tm, [dq_all, tq_c, tq_s], [], None, [(hw, BF)])[0]
    dqn = _mm("q_up_dx", dq_pad, wq_p, 'nt', tm=tm_l0, tn=q_rank)
    d_wq_p = _mm("q_up_dw", qn, dq_pad, 'tn', tm=512, tn=512)

    def qn_bwd_fn(rows, consts, m):
        pv, dv_ = rows
        g = consts[0][0:1, :]
        n, r = _rms(pv)
        return [_rms_bwd(dv_ * g, n, r)], {0: jnp.sum(dv_ * n, axis=0, keepdims=True)}

    dp_q, acc_qg = _rows_call("q_norm_bwd", qn_bwd_fn, r_len, tm, [p_q, dqn], [qg], None, [(q_rank, BF)],
                              acc_w=q_rank)
    dpu_h, dpw_h, dps_h = _pool_bwd("pool_bwd_h", p_pool[:t_len], dcat, pool_w[0], pool_scale, 0)
    dpu_g, dpw_g, dps_g = _pool_bwd("pool_bwd_g", p_pool[t_len:], dcat, pool_w[0], pool_scale, t_len)
    dp_pool = jnp.concatenate([dpu_h, dpu_g], axis=0)
    add_prev = lambda acc, i, prev: (acc + prev,)
    du_mix = _mm("in_pool_dx", dp_pool, w_in_pool, 'nt', tm=tm_l0, tn=d)
    du_mix = _mm("in_q_dx", dp_q, w_in_q, 'nt', tm=tm_l0, tn=d, epi=add_prev, epi_args=(du_mix,), epi_kinds=('mn',))
    du_mix = _mm("in_kvr_dx", dp_kvr, w_in_kvr, 'nt', tm=tm_l0, tn=d, epi=add_prev, epi_args=(du_mix,), epi_kinds=('mn',))
    d_w_in = jnp.concatenate([
        _mm("in_pool_dw", u_mix, dp_pool, 'tn', tm=512, tn=pool_dim),
        _mm("in_q_dw", u_mix, dp_q, 'tn', tm=512, tn=q_rank),
        _mm("in_kvr_dw", u_mix, dp_kvr, 'tn', tm=512, tn=kvr_w)[:, :kv_rank + QK_ROPE]], axis=1)
    ds1, acc_n0 = _adaln_bwd("adaln_bwd_l0m", s1, du_mix, ds2, gains[0], 1, mods[0], 1, tm, h_tiles)
    put(0, 1, (acc_n0[:, 0], acc_n0[:, 1], acc_g0[:, 0]))
    dgain[0][1] = jnp.sum(acc_n0[:, 2], axis=0)
    d_w_uq = d_wq_p.reshape(q_rank, heads, HEAD_PAD)[..., :QK_NOPE + QK_ROPE].reshape(q_rank, -1)
    d_w_ukv = jnp.concatenate([d_wk_ext[:kv_rank].reshape(kv_rank, heads, HEAD_PAD)[..., :QK_NOPE],
                               d_wv_ext[:kv_rank].reshape(kv_rank, heads, HEAD_PAD)[..., :V_HEAD]],
                              axis=-1).reshape(kv_rank, -1)
    d_w_out = jnp.concatenate([d_w_out_p[:pool_dim],
                               d_w_out_p[pool_dim:].reshape(heads, HEAD_PAD, d)[:, :V_HEAD].reshape(-1, d)], axis=0)
    finish_stage(3, landed_sums(3, scatter[3], ds1))
    scatter[2] = chips_start(2, pair[2], ds1)
    pair[1] = pair_start(1, [by_shard_cols(d_w_in), by_shard_rows(d_w_uq), by_shard_cols(d_w_ukv),
                             by_shard_rows(d_w_out)])
    ds0, ffn_g, tr, dgain[0][0] = _ffn_half_bwd("l0a", ds1, sav_f00, gains[0], mods[0], 0, ffn_w[0][0],
                                                empty_ffn_grads(), tm, h_tiles, tm_l0,
                                                scatter[2]['token'] + pair[1]['token'])
    put(0, 0, tr)
    grad_x = ds0[:t_len][None]
    finish_stage(2, landed_sums(2, scatter[2], ds0))
    pair[0] = pair_start(0, list(ffn_g))

    dmh = jnp.stack([jnp.stack([dmod[l][k][0] for k in range(N_MOD)]) for l in range(2)])
    dmg0 = jnp.stack([dmod[0][k][1] for k in range(N_MOD)])
    dg_rows = jnp.stack([dgain[l][k] for l in range(2) for k in range(3)])
    pieces = [dmh.reshape(2 * N_MOD, d), dmg0, dg_rows, d_cw[:3], d_final_g[None],
              (dpw_h + dpw_g).reshape(-1, d), jnp.pad((dps_h + dps_g)[0], (0, d - pool_dim))[None],
              jnp.pad(acc_qg[0, 0], (0, d - q_rank))[None], jnp.pad(acc_kvg[0, 0], (0, d - kv_rank))[None],
              acc_loss[0, 1][None]]
    n_piece = [p.shape[0] for p in pieces]
    pieces = [jnp.pad(p, ((0, (-p.shape[0]) % 8), (0, 0))) for p in pieces]
    small_g = jnp.concatenate(pieces, axis=0) + pair[0]['token'][0, 0]
    sg_all = _gather_all("gather_small_grads", [small_g], two_level=True)[0]
    sg_sum = _sum_lead("sum_small_grads", sg_all)
    scatter[1] = chips_start(1, pair[1], sg_sum)
    offs = [0]
    for p in pieces:
        offs.append(offs[-1] + p.shape[0])
    part = lambda j: sg_sum[offs[j]:offs[j] + n_piece[j]]
    sum_dmh, sum_dmg0, g_norm_full, g_conv_w_full = part(0).reshape(2, N_MOD * d), part(1).reshape(N_MOD * d), part(2), part(3)
    g_final = part(4)[0]
    loss = part(9)[0, 0]
    g_pool_w = part(5).reshape(pool_w.shape)
    g_pool_scale = part(6)[:, :pool_dim]
    g_q_norm = part(7)[:, :q_rank]
    g_kv_norm = part(8)[:, :kv_rank]
    col0 = shard * (d // N_SHARD)
    g_norm_g = lax.dynamic_slice_in_dim(g_norm_full.reshape(2, 3, d), col0, d // N_SHARD, axis=2)
    g_conv_w = lax.dynamic_slice_in_dim(g_conv_w_full, col0, d // N_SHARD, axis=1)[None]
    g_b_mod = _sum_lead("sum_b_mod", jnp.stack([sum_dmh, jnp.stack([sum_dmg0, jnp.zeros_like(sum_dmg0)])]))

    dm16 = []
    for l in range(2):
        per_dev = sg_all[:, l * N_MOD:(l + 1) * N_MOD].reshape(N_DEV, N_MOD * d)
        row8 = (sum_dmg0 if l == 0 else jnp.zeros_like(sum_dmg0)) + scatter[1]['token'][0, 0]
        full = jnp.concatenate([per_dev, row8[None], jnp.zeros((7, N_MOD * d), F32)], axis=0)
        dm16.append(lax.dynamic_slice_in_dim(full, shard * n_col, n_col, axis=1))
    g_w_mod = [_mm(f"mod_dw_{l}", c16, dm16[l], 'tn', tm=512, tn=768, a_pre=_silu) for l in range(2)]
    dc16 = _mm("mod_dx", dm16[0], w_mod, 'nt', tm=16, tn=512, b_lead=0, epi=lambda acc, i, cv: (acc * _dsilu(cv),),
               epi_args=(c16,), epi_kinds=('mn',))
    dc_all = _gather_all("gather_dc", [dc16])[0]
    g_c_ctx = _sum_lead("sum_dc", dc_all[::2])[8]

    grads.update(c_ctx=g_c_ctx, norm_g=g_norm_g, b_mod=g_b_mod, pool_w=g_pool_w,
                 pool_scale=g_pool_scale, q_norm_g=g_q_norm, kv_norm_g=g_kv_norm, conv_w=g_conv_w, final_norm_g=g_final)
    names = list(weights)

    scatter[0] = chips_start(0, pair[0], g_c_ctx)
    upd.update({n: _adamw(f"adamw_{n}", weights[n], grads[n].reshape(weights[n].shape), mom_m[n], mom_v[n],
                          scatter[0]['token']) for n in names if n not in big_names and n != "w_mod"})
    mod_out = None
    for l in range(2):
        mod_out = _adamw_piece(f"adamw_w_mod_{l}", w_mod, g_w_mod[l], mom_m["w_mod"], mom_v["w_mod"], (l,), mod_out,
                               scatter[0]['token'])
    grads["w_mod"], upd["w_mod"] = mod_out[0], mod_out[1:]
    finish_stage(1, landed_sums(1, scatter[1], upd["w_mod"][0]))
    finish_stage(0, landed_sums(0, scatter[0], upd[mixer_names[0][-1]][0]))
    for nm in ffn_names:
        done = [lanes_last(nm, t) for t in ffn_out[nm]]
        grads[nm], upd[nm] = done[0], done[1:]
    return (loss, grad_x, *[grads[n].reshape(weights[n].shape) for n in names], *[upd[n][0] for n in names],
            *[upd[n][1] for n in names], *[upd[n][2] for n in names])
```

```python
import math

import jax
import jax.numpy as jnp
from jax import lax
from jax.experimental import pallas as pl
from jax.experimental.pallas import tpu as pltpu

F32 = jnp.float32
BF = jnp.bfloat16
MESH = pl.DeviceIdType.MESH

N_DEV = 8
N_SHARD = 4
RMS_EPS = 1e-6
N_MOD = 9
POOL_WINDOWS = (2, 4, 8, 16)
QK_NOPE = 64
QK_ROPE = 32
V_HEAD = 64
HEAD_PAD = 128
GRID_W = 64
ROPE_THETA = 10000.0
POOL_PAD = 16
ADAM_LR, ADAM_B1, ADAM_B2, ADAM_EPS, ADAM_WD, ADAM_STEP = 0.001, 0.9, 0.999, 1e-08, 0.01, 10
VMEM_LIMIT = 56 * 1024 * 1024
MM_ROWS = 1024


def _pcall(body, **kw):
    return pl.pallas_call(body, **kw)


def _params(sem=None):
    return pltpu.CompilerParams(dimension_semantics=sem, vmem_limit_bytes=VMEM_LIMIT)


def _pick(n, pref, mult=128):
    best = None
    d = mult
    while d <= min(n, pref):
        if n % d == 0:
            best = d
        d += mult
    return best if best is not None else n


def _silu(z):
    return z * jax.nn.sigmoid(z)


def _dsilu(z):
    s = jax.nn.sigmoid(z)
    return s * (1.0 + z * (1.0 - s))


def _dot(a, b, dims):
    return lax.dot_general(a.astype(BF), b.astype(BF), (dims, ((), ())), preferred_element_type=F32)


NN = ((1,), (0,))
NT = ((1,), (1,))
TN = ((0,), (0,))


ALL_FLIPS = [(kx, ky, kc) for kx in (0, 1) for ky in (0, 1) for kc in (0, 1) if (kx, ky, kc) != (0, 0, 0)]
CHIP_FLIPS = [(1, 0, 0), (0, 1, 0), (1, 1, 0)]
SIBLING = (0, 0, 1)
COMM_SPLIT = 8
SPLIT_MIN_ROWS = 256


def _exchange(name, arrays, plan, lead, whole_src, split=COMM_SPLIT):
    n = len(arrays)
    blk_shapes = [tuple(a.shape) if whole_src else tuple(a.shape[1:]) for a in arrays]
    splits = []
    for shp in blk_shapes:
        s = 1
        while s * 2 <= split and shp[0] % (s * 2) == 0 and (shp[0] // (s * 2)) % 16 == 0 \
                and shp[0] // (s * 2) >= SPLIT_MIN_ROWS:
            s *= 2
        splits.append(s)
    items = plan(0, 0, 0)
    n_items = len(items)
    remote_ids = [k for k, it in enumerate(items) if it[0] is not None]
    local_ids = [k for k, it in enumerate(items) if it[0] is None]
    slots = [(a, s) for s in range(max(splits)) for a in range(n) if s < splits[a]]
    n_slot = len(slots)

    def body(*refs):
        ins, outs = refs[:n], refs[n:2 * n]
        send_sems, recv_sems, loc_sems = refs[2 * n:]
        x, y, c = lax.axis_index("x"), lax.axis_index("y"), lax.axis_index("c")
        plan_here = plan(x, y, c)

        def rows(ref, a, s):
            rc = blk_shapes[a][0] // splits[a]
            return ref.at[pl.ds(s * rc, rc)]

        def make(si, k):
            a, s = slots[si]
            flip, src, dst, _ = plan_here[k]
            base = outs[a] if src[0] == 'out' else ins[a]
            src_ref = rows(base if src[1] is None else base.at[src[1]], a, s)
            dst_ref = rows(outs[a].at[dst], a, s)
            if flip is None:
                return pltpu.make_async_copy(src_ref, dst_ref, loc_sems.at[si * max(1, len(local_ids)) + local_ids.index(k)])
            peer = (1 - x if flip[0] else x, 1 - y if flip[1] else y, 1 - c if flip[2] else c)
            sem = si * len(remote_ids) + remote_ids.index(k)
            return pltpu.make_async_remote_copy(src_ref=src_ref, dst_ref=dst_ref, send_sem=send_sems.at[sem],
                                                recv_sem=recv_sems.at[sem], device_id=peer, device_id_type=MESH)

        copies = {}
        for si in range(n_slot):
            for k in range(n_items):
                if plan_here[k][3] is None:
                    copies[si, k] = make(si, k)
                    copies[si, k].start()
        arrived = set()
        for si in range(n_slot):
            for k in range(n_items):
                after = plan_here[k][3]
                if after is not None:
                    if (si, after) not in arrived:
                        copies[si, after].wait_recv()
                        arrived.add((si, after))
                    copies[si, k] = make(si, k)
                    copies[si, k].start()
        for (si, k), cp in copies.items():
            if plan_here[k][0] is None:
                cp.wait()
            else:
                cp.wait_send()
                if (si, k) not in arrived:
                    cp.wait_recv()

    any_spec = pl.BlockSpec(memory_space=pl.ANY)
    n_rem = max(1, n_slot * len(remote_ids))
    outs = _pcall(
        body, name=name,
        out_shape=[jax.ShapeDtypeStruct((lead,) + s, a.dtype) for s, a in zip(blk_shapes, arrays)],
        in_specs=[any_spec] * n, out_specs=[any_spec] * n,
        scratch_shapes=[pltpu.SemaphoreType.DMA((n_rem,)), pltpu.SemaphoreType.DMA((n_rem,)),
                        pltpu.SemaphoreType.DMA((max(1, n_slot * len(local_ids)),))],
    )(*arrays)
    return list(outs)


def _place(x, y, c):
    return 4 * x + 2 * y + c


def _flip(v, f):
    return 1 - v if f else v


def _gather_all(name, arrays, two_level=False):
    def plan(x, y, c):
        me = _place(x, y, c)
        if not two_level:
            return [(None, ('in', None), me, None)] + [(f, ('in', None), me, None) for f in ALL_FLIPS]
        items = [(None, ('in', None), me, None), (SIBLING, ('in', None), me, None)]
        items += [(f, ('in', None), me, None) for f in CHIP_FLIPS]
        for j, f in enumerate(CHIP_FLIPS):
            got = _place(_flip(x, f[0]), _flip(y, f[1]), c)
            items.append((SIBLING, ('out', got), got, 2 + j))
        return items
    return _exchange(name, arrays, plan, N_DEV, True, split=1)


HBM_SPEC = pl.BlockSpec(memory_space=pltpu.HBM)
SEM_SPEC = pl.BlockSpec(memory_space=pltpu.SEMAPHORE)
SIDE_EFFECT = pltpu.SideEffectType.DATAFLOW_SIDE_EFFECTING


def _split_start(name, bufs, n_copies, build):
    n = len(bufs)

    def body(*refs):
        for cp in build(refs[:n], refs[n], refs[n + 1]):
            cp.start()
        token = refs[-1]
        token[...] = jnp.zeros_like(token)

    res = _pcall(
        body, name=name,
        out_shape=(pltpu.SemaphoreType.DMA((n_copies,)), pltpu.SemaphoreType.DMA((n_copies,)),
                   *[pltpu.HBM(b.shape, b.dtype) for b in bufs], jax.ShapeDtypeStruct((8, 128), F32)),
        in_specs=[HBM_SPEC] * n,
        out_specs=(SEM_SPEC, SEM_SPEC, *[HBM_SPEC] * n, pl.BlockSpec(memory_space=pltpu.VMEM)),
        input_output_aliases={i: 2 + i for i in range(n)},
        compiler_params=pltpu.CompilerParams(has_side_effects=SIDE_EFFECT),
    )(*[pltpu.with_memory_space_constraint(b, pltpu.HBM) for b in bufs])
    return dict(send=res[0], recv=res[1], bufs=list(res[2:2 + n]), token=res[-1], build=build)


def _split_wait(name, handle, after):
    n = len(handle['bufs'])
    build = handle['build']

    def body(*refs):
        for cp in build(refs[:n], refs[n], refs[n + 1]):
            cp.wait_send()
            cp.wait_recv()

    res = _pcall(
        body, name=name, out_shape=tuple(pltpu.HBM(b.shape, b.dtype) for b in handle['bufs']),
        in_specs=[HBM_SPEC] * n + [SEM_SPEC, SEM_SPEC, pl.BlockSpec(memory_space=pl.ANY)],
        out_specs=tuple([HBM_SPEC] * n), input_output_aliases={i: i for i in range(n)},
        compiler_params=pltpu.CompilerParams(has_side_effects=SIDE_EFFECT),
    )(*handle['bufs'], handle['send'], handle['recv'], after)
    return list(res)


def _landing(lead, arrays):
    return [pltpu.with_memory_space_constraint(lax.empty((lead,) + tuple(a.shape[1:]), a.dtype), pltpu.HBM)
            for a in arrays]


def _copy_list(n, per_array, make):
    def build(refs, send_sems, recv_sems):
        copies = []
        for a in range(n):
            for j in range(per_array):
                src, dst, peer = make(refs, a, j)
                k = a * per_array + j
                copies.append(pltpu.make_async_remote_copy(src_ref=src, dst_ref=dst, send_sem=send_sems.at[k],
                                                           recv_sem=recv_sems.at[k], device_id=peer,
                                                           device_id_type=MESH))
        return copies
    return build


def _mesh_place():
    x, y, c = lax.axis_index("x"), lax.axis_index("y"), lax.axis_index("c")
    return x, y, c, 2 * x + y


def _chips_gather_build(n):
    def make(refs, a, j):
        x, y, c, chip = _mesh_place()
        px, py = _flip(x, CHIP_FLIPS[j][0]), _flip(y, CHIP_FLIPS[j][1])
        return refs[a].at[c], refs[n + a].at[2 * chip + c], (px, py, c)
    return _copy_list(n, len(CHIP_FLIPS), make)


def _chips_scatter_build(n):
    def make(refs, a, j):
        x, y, c, chip = _mesh_place()
        px, py = _flip(x, CHIP_FLIPS[j][0]), _flip(y, CHIP_FLIPS[j][1])
        return refs[a].at[2 * px + py], refs[n + a].at[chip], (px, py, c)
    return _copy_list(n, len(CHIP_FLIPS), make)


def _sibling_forward_build(n):
    def make(refs, a, j):
        x, y, c, _ = _mesh_place()
        blk = 2 * (2 * _flip(x, CHIP_FLIPS[j][0]) + _flip(y, CHIP_FLIPS[j][1])) + c
        return refs[a].at[blk], refs[a].at[blk], (x, y, 1 - c)
    return _copy_list(n, len(CHIP_FLIPS), make)


def _sibling_halves_build(n):
    def make(refs, a, j):
        x, y, c, _ = _mesh_place()
        return refs[a].at[2 * j + 1 - c], refs[n + a].at[j], (x, y, 1 - c)
    return _copy_list(n, N_SHARD, make)


def _sibling_whole_build(n):
    def make(refs, a, j):
        x, y, c, _ = _mesh_place()
        return refs[a], refs[n + a], (x, y, 1 - c)
    return _copy_list(n, 1, make)


def _add_halves(name, send, land):
    _, r, cdim = send.shape
    tr = _pick(r, max(16, (1 << 20) // (cdim * 2)), 16)

    def body(c_ref, own_ref, got_ref, o_ref):
        o_ref[...] = (own_ref[...].astype(F32) + got_ref[...].astype(F32)).astype(BF)

    grid_spec = pltpu.PrefetchScalarGridSpec(
        num_scalar_prefetch=1, grid=(N_SHARD, r // tr),
        in_specs=[pl.BlockSpec((None, tr, cdim), lambda sh, i, cr: (2 * sh + cr[0], i, 0)),
                  pl.BlockSpec((None, tr, cdim), lambda sh, i, cr: (sh, i, 0))],
        out_specs=pl.BlockSpec((None, tr, cdim), lambda sh, i, cr: (sh, i, 0)))
    core = lax.axis_index("c").astype(jnp.int32).reshape(1)
    return _pcall(body, name=name, grid_spec=grid_spec, out_shape=jax.ShapeDtypeStruct((N_SHARD, r, cdim), BF),
                  compiler_params=_params(("arbitrary", "arbitrary")))(core, send, land)


def _sum_lead(name, arr, out_dtype=F32):
    n, r, cdim = arr.shape
    tr = r
    limit = (4 << 20) // (n * cdim * arr.dtype.itemsize)
    if r > limit:
        tr = _pick(r, max(limit, 16), 16)

    def body(x_ref, o_ref):
        acc = x_ref[0].astype(F32)
        for d in range(1, n):
            acc = acc + x_ref[d].astype(F32)
        o_ref[...] = acc.astype(out_dtype)

    return _pcall(body, name=name, grid=(r // tr,),
                  in_specs=[pl.BlockSpec((n, tr, cdim), lambda i: (0, i, 0))],
                  out_specs=pl.BlockSpec((tr, cdim), lambda i: (i, 0)),
                  out_shape=jax.ShapeDtypeStruct((r, cdim), out_dtype),
                  compiler_params=_params(("arbitrary",)))(arr)


def _rows_call(name, fn, n_rows, tm, rows, consts, mod, outs, acc_w=None, h_tiles=None):
    nt = n_rows // tm
    ht = nt if h_tiles is None else h_tiles
    ng = 1 if mod is None else mod.shape[0]
    n_r, n_c, n_o = len(rows), len(consts), len(outs)
    has_mod = mod is not None

    def body(*refs):
        i = pl.program_id(0)
        first = (i % ht) == 0
        row_refs, const_refs = refs[:n_r], refs[n_r:n_r + n_c]
        p = n_r + n_c
        mod_tile = refs[p][...] if has_mod else None
        p += int(has_mod)
        out_refs = refs[p:p + n_o]
        o, acc = fn([r[...] for r in row_refs], [r[...] for r in const_refs], mod_tile)
        for r, v in zip(out_refs, o):
            r[...] = v.astype(r.dtype)
        if acc_w is not None:
            acc_ref = refs[p + n_o]

            @pl.when(first)
            def _():
                acc_ref[...] = jnp.zeros_like(acc_ref)

            for k, v in acc.items():
                acc_ref[k:k + 1, :] += v

    in_specs = [pl.BlockSpec((tm, r.shape[1]), lambda i: (i, 0)) for r in rows]
    in_specs += [pl.BlockSpec(cst.shape, lambda i, nd=cst.ndim: (0,) * nd) for cst in consts]
    args = list(rows) + list(consts)
    if has_mod:
        in_specs.append(pl.BlockSpec((None,) + mod.shape[1:], lambda i: (i // ht, 0, 0)))
        args.append(mod)
    out_shape = [jax.ShapeDtypeStruct((n_rows, w), dt) for w, dt in outs]
    out_specs = [pl.BlockSpec((tm, w), lambda i: (i, 0)) for w, _ in outs]
    if acc_w is not None:
        out_shape.append(jax.ShapeDtypeStruct((ng, 8, acc_w), F32))
        out_specs.append(pl.BlockSpec((None, 8, acc_w), lambda i: (i // ht, 0, 0)))
    res = _pcall(body, name=name, grid=(nt,), in_specs=in_specs, out_specs=out_specs, out_shape=out_shape,
                 compiler_params=_params(("arbitrary",)))(*args)
    return list(res)


def _rms(s):
    r = lax.rsqrt(jnp.mean(s * s, axis=1, keepdims=True) + RMS_EPS)
    return s * r, r


def _rms_bwd(dn, n, r):
    return r * (dn - n * jnp.mean(dn * n, axis=1, keepdims=True))


def _adaln_fwd(name, s, gains, gain_row, mod, k, tm, h_tiles, after=None):
    def fn(rows, consts, m):
        n, _ = _rms(rows[0])
        y = n * consts[0][gain_row:gain_row + 1, :]
        return [y * (1.0 + m[3 * k + 1:3 * k + 2, :]) + m[3 * k:3 * k + 1, :]], {}

    d = s.shape[1]
    consts = [gains] if after is None else [gains, after]
    return _rows_call(name, fn, s.shape[0], tm, [s], consts, mod, [(d, BF)], h_tiles=h_tiles)[0]


def _adaln_bwd(name, s, du, ds_res, gains, gain_row, mod, k, tm, h_tiles):
    def fn(rows, consts, m):
        sv, duv, res = rows
        gain = consts[0][gain_row:gain_row + 1, :]
        n, r = _rms(sv)
        y = n * gain
        dy = duv * (1.0 + m[3 * k + 1:3 * k + 2, :])
        acc = {0: jnp.sum(duv, axis=0, keepdims=True), 1: jnp.sum(duv * y, axis=0, keepdims=True),
               2: jnp.sum(dy * n, axis=0, keepdims=True)}
        return [_rms_bwd(dy * gain, n, r) + res], acc

    d = s.shape[1]
    return _rows_call(name, fn, s.shape[0], tm, [s, du, ds_res], [gains], mod, [(d, F32)], acc_w=d, h_tiles=h_tiles)


def _resid_bwd(name, ds_out, o, mod, k, cst, tm, h_tiles, after=None):
    def fn(rows, consts, m):
        dsv, ov = rows
        gate = m[3 * k + 2:3 * k + 3, :]
        return [cst * gate * dsv], {0: jnp.sum(cst * ov * dsv, axis=0, keepdims=True)}

    d = o.shape[1]
    consts = [] if after is None else [after]
    return _rows_call(name, fn, o.shape[0], tm, [ds_out, o], consts, mod, [(d, BF)], acc_w=d, h_tiles=h_tiles)


def _mm(name, a, b, mode, tm=256, tn=512, out_dtypes=(F32,), epi=None, epi_args=(), epi_kinds=(), a_pre=None,
        b_lead=None):
    bshape = b.shape if b_lead is None else b.shape[1:]
    if mode == 'nn':
        (m, kd), nd = a.shape, bshape[1]
    elif mode == 'nt':
        (m, kd), nd = a.shape, bshape[0]
    else:
        (kd, m), nd = a.shape, bshape[1]
    tm = _pick(m, tm, 16) if m % tm else tm
    tn = _pick(nd, tn, 128) if nd % tn else tn
    dims = {'nn': NN, 'nt': NT, 'tn': TN}[mode]
    n_e, n_o = len(epi_args), len(out_dtypes)

    def body(*refs):
        i = pl.program_id(1)
        av = refs[0][...]
        if a_pre is not None:
            av = a_pre(av)
        acc = _dot(av, refs[1][...], dims)
        res = (acc,) if epi is None else epi(acc, i, *[r[...] for r in refs[2:2 + n_e]])
        for r, v in zip(refs[2 + n_e:], res):
            r[...] = v.astype(r.dtype)

    if mode == 'nn':
        specs = [pl.BlockSpec((tm, kd), lambda j, i: (i, 0)), pl.BlockSpec((kd, tn), lambda j, i: (0, j))]
    elif mode == 'nt':
        specs = [pl.BlockSpec((tm, kd), lambda j, i: (i, 0)), pl.BlockSpec((tn, kd), lambda j, i: (j, 0))]
    else:
        specs = [pl.BlockSpec((kd, tm), lambda j, i: (0, i)), pl.BlockSpec((kd, tn), lambda j, i: (0, j))]
    if b_lead is not None:
        shape2, at2 = specs[1].block_shape, specs[1].index_map
        specs[1] = pl.BlockSpec((None,) + tuple(shape2), lambda j, i: (b_lead,) + tuple(at2(j, i)))
    for arr, kind in zip(epi_args, epi_kinds):
        if kind == 'mn':
            specs.append(pl.BlockSpec((tm, tn), lambda j, i: (i, j)))
        elif kind == 'n':
            specs.append(pl.BlockSpec((1, tn), lambda j, i: (0, j)))
        elif kind == 'mt':
            specs.append(pl.BlockSpec((tm, arr.shape[1]), lambda j, i: (i, 0)))
        else:
            specs.append(pl.BlockSpec(arr.shape, lambda j, i, nd_=arr.ndim: (0,) * nd_))
    res = _pcall(body, name=name, grid=(nd // tn, m // tm), in_specs=specs,
                 out_specs=[pl.BlockSpec((tm, tn), lambda j, i: (i, j))] * n_o,
                 out_shape=[jax.ShapeDtypeStruct((m, nd), dt) for dt in out_dtypes],
                 compiler_params=_params(("arbitrary", "arbitrary")))(a, b, *epi_args)
    return res[0] if n_o == 1 else list(res)


def _row_gate(mod, k3, i, tm, n_lat):
    g0 = mod[0, k3:k3 + 1, :]
    if mod.shape[0] == 1:
        return g0
    rid = i * tm + lax.broadcasted_iota(jnp.int32, (tm, 1), 0)
    return jnp.where(rid < n_lat, g0, mod[1, k3:k3 + 1, :])


def _ffn_up(name, u, wg, wu, base, tm):
    r, d = u.shape
    nch, _, _, fc = wg.shape

    def body(u_ref, wg_ref, wu_ref, a_ref, b_ref, h_ref):
        uv = u_ref[...]
        a = _dot(uv, wg_ref[...], NN)
        b = _dot(uv, wu_ref[...], NN)
        a_ref[...] = a.astype(BF)
        b_ref[...] = b.astype(BF)
        h_ref[...] = (_silu(a) * b).astype(BF)

    chunk = pl.BlockSpec((None, tm, fc), lambda j, i: (j, i, 0))
    return _pcall(body, name=name, grid=(nch, r // tm),
                  in_specs=[pl.BlockSpec((tm, d), lambda j, i: (i, 0)),
                            pl.BlockSpec((None, None, d, fc), lambda j, i: (j, base, 0, 0)),
                            pl.BlockSpec((None, None, d, fc), lambda j, i: (j, base, 0, 0))],
                  out_specs=[chunk] * 3, out_shape=[jax.ShapeDtypeStruct((nch, r, fc), BF)] * 3,
                  compiler_params=_params(("arbitrary", "arbitrary")))(u, wg, wu)


def _ffn_down(name, hid, wd, wd_blk, s, mod, k, n_lat, tm):
    nch, r, fc = hid.shape
    d = wd.shape[2]

    def body(h_ref, w_ref, s_ref, m_ref, so_ref, o_ref):
        i = pl.program_id(0)
        o = _dot(h_ref[0], w_ref[0], NN)
        for j in range(1, nch):
            o = o + _dot(h_ref[j], w_ref[j], NN)
        o_ref[...] = o
        so_ref[...] = s_ref[...] + 0.5 * _row_gate(m_ref[...], 3 * k + 2, i, tm, n_lat) * o

    row = pl.BlockSpec((tm, d), lambda i: (i, 0))
    return _pcall(body, name=name, grid=(r // tm,),
                  in_specs=[pl.BlockSpec((nch, tm, fc), lambda i: (0, i, 0)),
                            pl.BlockSpec((nch, fc, d), lambda i: (0, wd_blk, 0)), row,
                            pl.BlockSpec(mod.shape, lambda i: (0, 0, 0))],
                  out_specs=[row, row], out_shape=[jax.ShapeDtypeStruct((r, d), F32)] * 2,
                  compiler_params=_params(("arbitrary",)))(hid, wd, s, mod)


def _ffn_dhid(name, d_o, wd, wd_blk, a, b, tm):
    r, d = d_o.shape
    nch, _, fc = a.shape

    def body(g_ref, w_ref, a_ref, b_ref, da_ref, db_ref):
        dh = _dot(g_ref[...], w_ref[...], NT)
        av, bv = a_ref[...].astype(F32), b_ref[...].astype(F32)
        sg = jax.nn.sigmoid(av)
        da_ref[...] = (dh * bv * (sg * (1.0 + av * (1.0 - sg)))).astype(BF)
        db_ref[...] = (dh * (av * sg)).astype(BF)

    chunk = pl.BlockSpec((None, tm, fc), lambda j, i: (j, i, 0))
    return _pcall(body, name=name, grid=(nch, r // tm),
                  in_specs=[pl.BlockSpec((tm, d), lambda j, i: (i, 0)),
                            pl.BlockSpec((None, fc, d), lambda j, i: (j, wd_blk, 0)), chunk, chunk],
                  out_specs=[chunk] * 2, out_shape=[jax.ShapeDtypeStruct((nch, r, fc), BF)] * 2,
                  compiler_params=_params(("arbitrary", "arbitrary")))(d_o, wd, a, b)


def _ffn_du(name, da, db, wg, wu, base, tm):
    nch, r, fc = da.shape
    d = wg.shape[2]

    def body(da_ref, db_ref, wg_ref, wu_ref, o_ref):
        acc = _dot(da_ref[0], wg_ref[0], NT) + _dot(db_ref[0], wu_ref[0], NT)
        for j in range(1, nch):
            acc = acc + _dot(da_ref[j], wg_ref[j], NT) + _dot(db_ref[j], wu_ref[j], NT)
        o_ref[...] = acc

    chunks = pl.BlockSpec((nch, tm, fc), lambda i: (0, i, 0))
    held = pl.BlockSpec((nch, None, d, fc), lambda i: (0, base, 0, 0), pipeline_mode=pl.Buffered(1))
    return _pcall(body, name=name, grid=(r // tm,), in_specs=[chunks, chunks, held, held],
                  out_specs=pl.BlockSpec((tm, d), lambda i: (i, 0)), out_shape=jax.ShapeDtypeStruct((r, d), F32),
                  compiler_params=_params(("arbitrary",)))(da, db, wg, wu)


def _ffn_dw_in(name, u, da, db, tmm, g_gate, g_up, idx):
    r, d = u.shape
    nch, _, fc = da.shape
    nb = d // tmm

    def body(u_ref, a_ref, b_ref, gg_ref, gu_ref, og_ref, ou_ref):
        ut = u_ref[...].T
        og_ref[...] = _dot(ut, a_ref[...], NN).astype(og_ref.dtype)
        ou_ref[...] = _dot(ut, b_ref[...], NN).astype(ou_ref.dtype)

    chunk = pl.BlockSpec((None, r, fc), lambda j, mi: (j, 0, 0))
    out = pl.BlockSpec((None, tmm, fc), lambda j, mi: (j, idx * nb + mi, 0))
    return _pcall(body, name=name, grid=(nch, nb),
                  in_specs=[pl.BlockSpec((r, tmm), lambda j, mi: (0, mi)), chunk, chunk,
                            pl.BlockSpec(memory_space=pl.ANY), pl.BlockSpec(memory_space=pl.ANY)],
                  out_specs=[out, out],
                  out_shape=[jax.ShapeDtypeStruct(g_gate.shape, g_gate.dtype),
                             jax.ShapeDtypeStruct(g_up.shape, g_up.dtype)],
                  input_output_aliases={3: 0, 4: 1},
                  compiler_params=_params(("arbitrary", "arbitrary")))(u, da, db, g_gate, g_up)


def _ffn_dw_down(name, hid, d_o, tn, grads, idx):
    nch, r, fc = hid.shape
    d = d_o.shape[1]

    def body(h_ref, g_ref, acc_ref, o_ref):
        o_ref[...] = _dot(h_ref[...], g_ref[...], TN).astype(o_ref.dtype)

    return _pcall(body, name=name, grid=(nch, d // tn),
                  in_specs=[pl.BlockSpec((None, r, fc), lambda j, ni: (j, 0, 0)),
                            pl.BlockSpec((r, tn), lambda j, ni: (0, ni)),
                            pl.BlockSpec(memory_space=pl.ANY)],
                  out_specs=pl.BlockSpec((None, fc, tn), lambda j, ni: (j, idx, ni)),
                  out_shape=jax.ShapeDtypeStruct(grads.shape, grads.dtype), input_output_aliases={2: 0},
                  compiler_params=_params(("arbitrary", "arbitrary")))(hid, d_o, grads)


def _partner(x):
    n = x.shape[1]
    lane = lax.broadcasted_iota(jnp.int32, x.shape, 1)
    return jnp.where((lane & 15) < 8, pltpu.roll(x, n - 8, 1), pltpu.roll(x, 8, 1))


def _rope(x, ct, st):
    reps = x.shape[1] // ct.shape[1]
    if reps > 1:
        ct, st = jnp.tile(ct, (1, reps)), jnp.tile(st, (1, reps))
    return x * ct + _partner(x) * st


def _rope_t(dy, ct, st):
    reps = dy.shape[1] // ct.shape[1]
    if reps > 1:
        ct, st = jnp.tile(ct, (1, reps)), jnp.tile(st, (1, reps))
    return dy * ct + _partner(dy * st)


def _rope_tables(t_len, g_len, lane0):
    half = QK_ROPE // 4
    pos = jnp.arange(t_len)
    row = (pos // GRID_W).astype(F32)
    col = (pos % GRID_W).astype(F32)
    freqs = jnp.power(ROPE_THETA, -jnp.arange(0, QK_ROPE // 2, 2, dtype=F32) / (QK_ROPE // 2))
    ang_r, ang_c = row[:, None] * freqs, col[:, None] * freqs
    cs = jnp.concatenate([jnp.cos(ang_r)] * 2 + [jnp.cos(ang_c)] * 2, axis=1)
    sn = jnp.concatenate([-jnp.sin(ang_r), jnp.sin(ang_r), -jnp.sin(ang_c), jnp.sin(ang_c)], axis=1)
    assert cs.shape[1] == 4 * half == QK_ROPE
    def place(tab, fill):
        rest = HEAD_PAD - lane0 - QK_ROPE
        rows = jnp.concatenate([jnp.full((t_len, lane0), fill, F32), tab, jnp.full((t_len, rest), fill, F32)], axis=1)
        return jnp.concatenate([rows, jnp.full((g_len, HEAD_PAD), fill, F32)], axis=0)

    return place(cs, 1.0), place(sn, 0.0)


def _attn_fwd(name, q, kp, vp, n_q, q_off, n_k, k_blk, heads, tq, scale):
    qb = q_off // tq
    per = 2 if heads % 2 == 0 else 1
    wide = per * HEAD_PAD

    def body(q_ref, k_ref, v_ref, o_ref, l_ref):
        for e in range(per):
            sl = slice(e * HEAD_PAD, (e + 1) * HEAD_PAD)
            s = _dot(q_ref[:, sl], k_ref[:, sl], NT) * scale
            m = jnp.max(s, axis=1, keepdims=True)
            p = jnp.exp(s - m)
            l = jnp.sum(p, axis=1, keepdims=True)
            o_ref[:, sl] = (_dot(p, v_ref[:, sl], NN) / l).astype(BF)
            l_ref[:, sl] = jnp.broadcast_to(m + jnp.log(l), (tq, HEAD_PAD))

    hw = heads * HEAD_PAD
    blk = pl.BlockSpec((tq, wide), lambda h, i: (i, h))
    kv = pl.BlockSpec((n_k, wide), lambda h, i: (k_blk, h))
    return _pcall(body, name=name, grid=(heads // per, n_q // tq),
                  in_specs=[pl.BlockSpec((tq, wide), lambda h, i: (i + qb, h)), kv, kv],
                  out_specs=[blk, blk],
                  out_shape=[jax.ShapeDtypeStruct((n_q, hw), BF), jax.ShapeDtypeStruct((n_q, hw), F32)],
                  compiler_params=_params(("arbitrary", "arbitrary")))(q, kp, vp)


def _attn_bwd(name, q, kp, vp, cat, dcat, lse, n_q, q_off, n_k, k_blk, heads, tq, scale, col_blk, onto=None):
    qb = q_off // tq

    per = 1
    wide = per * HEAD_PAD

    def body(q_ref, k_ref, v_ref, o_ref, do_ref, l_ref, *rest):
        dq_ref, dk_ref, dv_ref = rest[-3:]
        i = pl.program_id(1)
        for e in range(per):
            sl = slice(e * HEAD_PAD, (e + 1) * HEAD_PAD)
            qv, kv_, vv = q_ref[:, sl], k_ref[:, sl], v_ref[:, sl]
            dov = do_ref[:, sl]
            s = _dot(qv, kv_, NT) * scale
            p = jnp.exp(s - l_ref[:, e * HEAD_PAD:e * HEAD_PAD + 1])
            dp = _dot(dov, vv, NT)
            delta = jnp.sum(dov * o_ref[:, sl].astype(F32), axis=1, keepdims=True)
            ds = (p * (dp - delta) * scale).astype(BF)
            dq_ref[:, sl] = _dot(ds, kv_, NN)
            dk = _dot(ds, qv, TN)
            dv = _dot(p, dov, TN)

            @pl.when(i == 0)
            def _():
                if onto is None:
                    dk_ref[:, sl] = dk
                    dv_ref[:, sl] = dv
                else:
                    dk_ref[:, sl] = rest[0][:, sl] + dk
                    dv_ref[:, sl] = rest[1][:, sl] + dv

            @pl.when(i > 0)
            def _():
                dk_ref[:, sl] += dk
                dv_ref[:, sl] += dv

    hw = heads * HEAD_PAD
    heads = heads // per
    col_blk = col_blk // per
    qspec = pl.BlockSpec((tq, wide), lambda h, i: (i + qb, h))
    cspec = pl.BlockSpec((tq, wide), lambda h, i: (i + qb, col_blk + h))
    kv = pl.BlockSpec((n_k, wide), lambda h, i: (k_blk, h))
    blk = pl.BlockSpec((tq, wide), lambda h, i: (i, h))
    if onto is None:
        acc = pl.BlockSpec((n_k, wide), lambda h, i: (0, h))
        return _pcall(body, name=name, grid=(heads, n_q // tq),
                      in_specs=[qspec, kv, kv, cspec, cspec, blk], out_specs=[blk, acc, acc],
                      out_shape=[jax.ShapeDtypeStruct((n_q, hw), F32), jax.ShapeDtypeStruct((n_k, hw), F32),
                                 jax.ShapeDtypeStruct((n_k, hw), F32)],
                      compiler_params=_params(("arbitrary", "arbitrary")))(q, kp, vp, cat, dcat, lse)
    return _pcall(body, name=name, grid=(heads, n_q // tq),
                  in_specs=[qspec, kv, kv, cspec, cspec, blk, kv, kv], out_specs=[blk, kv, kv],
                  out_shape=[jax.ShapeDtypeStruct((n_q, hw), F32)] + [jax.ShapeDtypeStruct(t.shape, F32) for t in onto],
                  input_output_aliases={6: 1, 7: 2},
                  compiler_params=_params(("arbitrary", "arbitrary")))(q, kp, vp, cat, dcat, lse, *onto)


def _shift(x, k):
    return pltpu.roll(x, k % x.shape[0], 0)


def _window_sum(v, w, mirrored):
    n, gd = v.shape
    pad = jnp.zeros((POOL_PAD, gd), F32)
    e = jnp.concatenate([pad, v, pad], axis=0)
    acc = e + _shift(e, -1 if mirrored else 1)
    step = 1
    while 2 * step < w:
        acc = _shift(acc, step) + _shift(acc, -step)
        step *= 2
    return acc[POOL_PAD:POOL_PAD + n]


def _window_count(n, w):
    t = lax.broadcasted_iota(jnp.int32, (n, 1), 0)
    lo = jnp.maximum(t - w // 2, 0)
    hi = jnp.minimum(t + (w - w // 2 - 1), n - 1)
    return (hi - lo + 1).astype(F32)


def _pool_fwd(name, u, pool_w, scale):
    n, pd = u.shape
    ng = len(POOL_WINDOWS)
    gd = pd // ng

    def body(u_ref, w_ref, s_ref, y_ref):
        for g, w in enumerate(POOL_WINDOWS):
            sl = slice(g * gd, (g + 1) * gd)
            ug = u_ref[:, sl]
            p = _window_sum(ug, w, False) / _window_count(n, w) - ug
            y_ref[:, sl] = (_dot(p, w_ref[g], NN) * s_ref[:, sl]).astype(BF)

    return _pcall(body, name=name, out_shape=jax.ShapeDtypeStruct((n, pd), BF),
                  compiler_params=_params())(u, pool_w, scale)


def _pool_bwd(name, u, dcat, pool_w, scale, row_off):
    n, pd = u.shape
    ng = len(POOL_WINDOWS)
    gd = pd // ng

    def body(u_ref, dy_ref, w_ref, s_ref, du_ref, dw_ref, ds_ref):
        ds_ref[...] = jnp.zeros_like(ds_ref)
        for g, w in enumerate(POOL_WINDOWS):
            sl = slice(g * gd, (g + 1) * gd)
            ug, dy, wg = u_ref[:, sl], dy_ref[:, sl], w_ref[g]
            cnt = _window_count(n, w)
            p = _window_sum(ug, w, False) / cnt - ug
            ds_ref[0:1, sl] = jnp.sum(dy * _dot(p, wg, NN), axis=0, keepdims=True)
            dys = dy * s_ref[:, sl]
            dw_ref[g] = _dot(p, dys, TN)
            dp = _dot(dys, wg, NT)
            du_ref[:, sl] = (_window_sum(dp / cnt, w, True) - dp).astype(BF)

    rb = row_off // n
    return _pcall(body, name=name, grid=(1,),
                  in_specs=[pl.BlockSpec((n, pd), lambda i: (0, 0)), pl.BlockSpec((n, pd), lambda i: (rb, 0)),
                            pl.BlockSpec(pool_w.shape, lambda i: (0, 0, 0)), pl.BlockSpec(scale.shape, lambda i: (0, 0))],
                  out_specs=[pl.BlockSpec((n, pd), lambda i: (0, 0)), pl.BlockSpec((ng, gd, gd), lambda i: (0, 0, 0)),
                             pl.BlockSpec((8, pd), lambda i: (0, 0))],
                  out_shape=[jax.ShapeDtypeStruct((n, pd), BF), jax.ShapeDtypeStruct((ng, gd, gd), F32),
                             jax.ShapeDtypeStruct((8, pd), F32)],
                  compiler_params=_params(("arbitrary",)))(u, dcat, pool_w, scale)


def _edge_shift(z, k):
    n = z.shape[0]
    t = lax.broadcasted_iota(jnp.int32, (n, 1), 0)
    keep = (t >= k) if k > 0 else (t < n + k)
    return jnp.where(keep, pltpu.roll(z, k % n, 0), 0.0)


def _conv_fwd(name, p3, cw, tc):
    n, cd = p3.shape[0], p3.shape[1] // 3
    nb = cd // tc

    def body(b_ref, c_ref, v_ref, w_ref, y_ref):
        z = c_ref[...] * v_ref[...]
        w = w_ref[...]
        zc = w[0:1] * _edge_shift(z, 1) + w[1:2] * z + w[2:3] * _edge_shift(z, -1)
        y_ref[...] = (b_ref[...] * zc).astype(BF)

    return _pcall(body, name=name, grid=(nb,),
                  in_specs=[pl.BlockSpec((n, tc), lambda j: (0, j)), pl.BlockSpec((n, tc), lambda j: (0, nb + j)),
                            pl.BlockSpec((n, tc), lambda j: (0, 2 * nb + j)), pl.BlockSpec((3, tc), lambda j: (0, j))],
                  out_specs=pl.BlockSpec((n, tc), lambda j: (0, j)), out_shape=jax.ShapeDtypeStruct((n, cd), BF),
                  compiler_params=_params(("arbitrary",)))(p3, p3, p3, cw)


def _conv_bwd(name, p3, cw, dy, tc):
    n, cd = dy.shape
    nb = cd // tc

    def body(b_ref, c_ref, v_ref, w_ref, dy_ref, dp_ref, dw_ref):
        cv, vv, w, dyv = c_ref[...], v_ref[...], w_ref[...], dy_ref[...]
        z = cv * vv
        zl, zr = _edge_shift(z, 1), _edge_shift(z, -1)
        zc = w[0:1] * zl + w[1:2] * z + w[2:3] * zr
        dzc = dyv * b_ref[...]
        dz = w[0:1] * _edge_shift(dzc, -1) + w[1:2] * dzc + w[2:3] * _edge_shift(dzc, 1)
        dp_ref[0] = (dyv * zc).astype(BF)
        dp_ref[1] = (dz * vv).astype(BF)
        dp_ref[2] = (dz * cv).astype(BF)
        dw_ref[...] = jnp.zeros_like(dw_ref)
        dw_ref[0:1, :] = jnp.sum(dzc * zl, axis=0, keepdims=True)
        dw_ref[1:2, :] = jnp.sum(dzc * z, axis=0, keepdims=True)
        dw_ref[2:3, :] = jnp.sum(dzc * zr, axis=0, keepdims=True)

    col = pl.BlockSpec((n, tc), lambda j: (0, j))
    return _pcall(body, name=name, grid=(nb,),
                  in_specs=[col, pl.BlockSpec((n, tc), lambda j: (0, nb + j)),
                            pl.BlockSpec((n, tc), lambda j: (0, 2 * nb + j)), pl.BlockSpec((3, tc), lambda j: (0, j)), col],
                  out_specs=[pl.BlockSpec((3, n, tc), lambda j: (0, 0, j)), pl.BlockSpec((8, tc), lambda j: (0, j))],
                  out_shape=[jax.ShapeDtypeStruct((3, n, cd), BF), jax.ShapeDtypeStruct((8, cd), F32)],
                  compiler_params=_params(("arbitrary",)))(p3, p3, p3, cw, dy)


def _conv_din(name, dp3, w_in, tm):
    _, n, cd = dp3.shape
    d = w_in.shape[0]

    def body(a_ref, w_ref, o_ref, acc_ref):
        j = pl.program_id(1)
        part = _dot(a_ref[...], w_ref[...], NT)

        @pl.when(j == 0)
        def _():
            acc_ref[...] = part

        @pl.when(j > 0)
        def _():
            acc_ref[...] += part

        @pl.when(j == 2)
        def _():
            o_ref[...] = acc_ref[...]

    return _pcall(body, name=name, grid=(n // tm, 3),
                  in_specs=[pl.BlockSpec((None, tm, cd), lambda i, j: (j, i, 0)),
                            pl.BlockSpec((d, cd), lambda i, j: (0, j))],
                  out_specs=pl.BlockSpec((tm, d), lambda i, j: (i, 0)), out_shape=jax.ShapeDtypeStruct((n, d), F32),
                  scratch_shapes=[pltpu.VMEM((tm, d), F32)],
                  compiler_params=_params(("arbitrary", "arbitrary")))(dp3, w_in)


def _conv_dw_in(name, u, dp3, tmm, tn):
    n, d = u.shape
    cd = dp3.shape[2]
    nb = cd // tn

    def body(u_ref, z_ref, o_ref):
        o_ref[...] = _dot(u_ref[...], z_ref[...], TN)

    return _pcall(body, name=name, grid=(3 * nb, d // tmm),
                  in_specs=[pl.BlockSpec((n, tmm), lambda j, mi: (0, mi)),
                            pl.BlockSpec((None, n, tn), lambda j, mi: (j // nb, 0, j % nb))],
                  out_specs=pl.BlockSpec((tmm, tn), lambda j, mi: (mi, j)),
                  out_shape=jax.ShapeDtypeStruct((d, 3 * cd), F32),
                  compiler_params=_params(("arbitrary", "arbitrary")))(u, dp3)


def _loss_head(name, h, target, gain, tm):
    d = h.shape[1]

    def fn(rows, consts, m):
        hv, tv = rows
        g = consts[0][0:1, :]
        n, r = _rms(hv)
        err = n * g - tv
        dy = err / d
        loss = 0.5 * jnp.sum(err * err) / d
        acc = {0: jnp.sum(dy * n, axis=0, keepdims=True), 1: jnp.full((1, d), loss, F32)}
        return [_rms_bwd(dy * g, n, r)], acc

    return _rows_call(name, fn, h.shape[0], tm, [h, target], [gain], None, [(d, F32)], acc_w=d)


def _adamw(name, w, g, m, v, after=None):
    shape = w.shape
    if w.ndim == 1:
        shape2 = (1,) + shape
        res = _adamw(name, *[t.reshape(shape2) for t in (w, g, m, v)], after=after)
        return [t.reshape(shape) for t in res]
    if shape[-1] % 128 and shape[-2] % 128 == 0:
        res = _adamw(name, *[jnp.swapaxes(t, -1, -2) for t in (w, g, m, v)], after=after)
        return [jnp.swapaxes(t, -1, -2) for t in res]
    lead, (r, cdim) = shape[:-2], shape[-2:]
    tr = r
    if r * cdim * 4 > (3 << 19):
        tr = _pick(r, max(8, (3 << 19) // (cdim * 4)), 8)
    c1 = 1.0 / (1.0 - ADAM_B1 ** ADAM_STEP)
    c2 = 1.0 / (1.0 - ADAM_B2 ** ADAM_STEP)
    nl = len(lead)

    def body(w_ref, g_ref, m_ref, v_ref, *rest):
        d_ref, nm_ref, nv_ref = rest[-3:]
        gv = g_ref[...]
        nm = ADAM_B1 * m_ref[...] + (1.0 - ADAM_B1) * gv
        nv = ADAM_B2 * v_ref[...] + (1.0 - ADAM_B2) * (gv * gv)
        nm_ref[...] = nm
        nv_ref[...] = nv
        d_ref[...] = -ADAM_LR * ((nm * c1) / (jnp.sqrt(nv * c2) + ADAM_EPS) + ADAM_WD * w_ref[...])

    spec = pl.BlockSpec((None,) * nl + (tr, cdim), lambda *idx: idx + (0,))
    extra = [] if after is None else [after]
    res = _pcall(body, name=name, grid=lead + (r // tr,),
                 in_specs=[spec] * 4 + [pl.BlockSpec(memory_space=pl.ANY)] * len(extra), out_specs=[spec] * 3,
                 out_shape=[jax.ShapeDtypeStruct(shape, F32)] * 3,
                 compiler_params=_params(("arbitrary",) * (nl + 1)))(w, g, m, v, *extra)
    return list(res)


def _adamw_piece(name, w, g_piece, m, v, at, outs, after=None):
    shape = w.shape
    nl = len(at)
    r, cdim = shape[-2:]
    assert shape[nl:] == g_piece.shape and len(shape) == nl + 2
    tr = _pick(r, max(8, (3 << 19) // (cdim * 4)), 8) if r * cdim * 4 > (3 << 19) else r
    c1 = 1.0 / (1.0 - ADAM_B1 ** ADAM_STEP)
    c2 = 1.0 / (1.0 - ADAM_B2 ** ADAM_STEP)
    if outs is None:
        outs = [lax.empty(shape, F32) for _ in range(4)]

    def body(w_ref, g_ref, m_ref, v_ref, *rest):
        go_ref, d_ref, nm_ref, nv_ref = rest[-4:]
        gv = g_ref[...]
        nm = ADAM_B1 * m_ref[...] + (1.0 - ADAM_B1) * gv
        nv = ADAM_B2 * v_ref[...] + (1.0 - ADAM_B2) * (gv * gv)
        go_ref[...] = gv
        nm_ref[...] = nm
        nv_ref[...] = nv
        d_ref[...] = -ADAM_LR * ((nm * c1) / (jnp.sqrt(nv * c2) + ADAM_EPS) + ADAM_WD * w_ref[...])

    full = pl.BlockSpec((None,) * nl + (tr, cdim), lambda i: tuple(at) + (i, 0))
    extra = [] if after is None else [after]
    res = _pcall(body, name=name, grid=(r // tr,),
                 in_specs=[full, pl.BlockSpec((tr, cdim), lambda i: (i, 0)), full, full]
                 + [pl.BlockSpec(memory_space=pl.ANY)] * (4 + len(extra)),
                 out_specs=[full] * 4, out_shape=[jax.ShapeDtypeStruct(shape, F32)] * 4,
                 input_output_aliases={4 + j: j for j in range(4)},
                 compiler_params=_params(("arbitrary",)))(w, g_piece, m, v, *outs, *extra)
    return list(res)


def _ffn_half_fwd(tag, s, gains, mod, k, wts, n_lat, tm, h_tiles, tm_big, after=None):
    wg, wu, wd, idx = wts
    u = _adaln_fwd(f"adaln_{tag}", s, gains, k, mod, k, tm, h_tiles, after)
    a, b, hid = _ffn_up(f"ffn_up_{tag}", u, wg, wu, idx, tm_big)
    s_out, o = _ffn_down(f"ffn_down_{tag}", hid, wd, idx, s, mod, k, n_lat, tm_big)
    return s_out, (s, u, a, b, hid, o)


def _ffn_half_bwd(tag, ds_out, saved, gains, mod, k, wts, big_grads, tm, h_tiles, tm_big, after=None):
    wg, wu, wd, idx = wts
    g_gate, g_up, g_down = big_grads
    s, u, a, b, hid, o = saved
    d_o, acc_g = _resid_bwd(f"resid_bwd_{tag}", ds_out, o, mod, k, 0.5, tm, h_tiles, after)
    da, db = _ffn_dhid(f"ffn_dhid_{tag}", d_o, wd, idx, a, b, tm_big)
    du = _ffn_du(f"ffn_du_{tag}", da, db, wg, wu, idx, tm_big)
    d = u.shape[1]
    g_gate, g_up = _ffn_dw_in(f"ffn_dwgu_{tag}", u, da, db, _pick(d, MM_ROWS), g_gate, g_up, idx)
    g_down = _ffn_dw_down(f"ffn_dwd_{tag}", hid, d_o, _pick(d, 512), g_down, idx)
    ds, acc_n = _adaln_bwd(f"adaln_bwd_{tag}", s, du, ds_out, gains, k, mod, k, tm, h_tiles)
    return ds, (g_gate, g_up, g_down), (acc_n[:, 0], acc_n[:, 1], acc_g[:, 0]), jnp.sum(acc_n[:, 2], axis=0)


def kernel(x, c, ctx, c_ctx, norm_g, w_mod, b_mod, ffn_w_gate, ffn_w_up, ffn_w_down, ab_w_in, pool_w, pool_scale, q_norm_g, w_uq, kv_norm_g, w_ukv, ab_w_out, conv_w_in, conv_w, conv_w_out, final_norm_g, loss_target, m_c_ctx, m_norm_g, m_w_mod, m_b_mod, m_ffn_w_gate, m_ffn_w_up, m_ffn_w_down, m_ab_w_in, m_pool_w, m_pool_scale, m_q_norm_g, m_w_uq, m_kv_norm_g, m_w_ukv, m_ab_w_out, m_conv_w_in, m_conv_w, m_conv_w_out, m_final_norm_g, v_c_ctx, v_norm_g, v_w_mod, v_b_mod, v_ffn_w_gate, v_ffn_w_up, v_ffn_w_down, v_ab_w_in, v_pool_w, v_pool_scale, v_q_norm_g, v_w_uq, v_kv_norm_g, v_w_ukv, v_ab_w_out, v_conv_w_in, v_conv_w, v_conv_w_out, v_final_norm_g):
    weights = dict(c_ctx=c_ctx, norm_g=norm_g, w_mod=w_mod, b_mod=b_mod, ffn_w_gate=ffn_w_gate, ffn_w_up=ffn_w_up,
                   ffn_w_down=ffn_w_down, ab_w_in=ab_w_in, pool_w=pool_w, pool_scale=pool_scale, q_norm_g=q_norm_g,
                   w_uq=w_uq, kv_norm_g=kv_norm_g, w_ukv=w_ukv, ab_w_out=ab_w_out, conv_w_in=conv_w_in, conv_w=conv_w,
                   conv_w_out=conv_w_out, final_norm_g=final_norm_g)
    mom_m = dict(c_ctx=m_c_ctx, norm_g=m_norm_g, w_mod=m_w_mod, b_mod=m_b_mod, ffn_w_gate=m_ffn_w_gate,
                 ffn_w_up=m_ffn_w_up, ffn_w_down=m_ffn_w_down, ab_w_in=m_ab_w_in, pool_w=m_pool_w,
                 pool_scale=m_pool_scale, q_norm_g=m_q_norm_g, w_uq=m_w_uq, kv_norm_g=m_kv_norm_g, w_ukv=m_w_ukv,
                 ab_w_out=m_ab_w_out, conv_w_in=m_conv_w_in, conv_w=m_conv_w, conv_w_out=m_conv_w_out,
                 final_norm_g=m_final_norm_g)
    mom_v = dict(c_ctx=v_c_ctx, norm_g=v_norm_g, w_mod=v_w_mod, b_mod=v_b_mod, ffn_w_gate=v_ffn_w_gate,
                 ffn_w_up=v_ffn_w_up, ffn_w_down=v_ffn_w_down, ab_w_in=v_ab_w_in, pool_w=v_pool_w,
                 pool_scale=v_pool_scale, q_norm_g=v_q_norm_g, w_uq=v_w_uq, kv_norm_g=v_kv_norm_g, w_ukv=v_w_ukv,
                 ab_w_out=v_ab_w_out, conv_w_in=v_conv_w_in, conv_w=v_conv_w, conv_w_out=v_conv_w_out,
                 final_norm_g=v_final_norm_g)

    t_len, d = x.shape[1], x.shape[2]
    g_len = ctx.shape[1]
    r_len = t_len + g_len
    fc = ffn_w_gate.shape[3]
    heads = d // 128
    pool_dim = d // 2
    q_rank, kv_rank = q_norm_g.shape[1], kv_norm_g.shape[1]
    hw = heads * HEAD_PAD
    attn_scale = 1.0 / math.sqrt(QK_NOPE + QK_ROPE)
    kvr_w = kv_rank + HEAD_PAD
    tm = 256 if g_len % 256 == 0 else g_len
    assert t_len % tm == 0 and g_len % tm == 0 and t_len % g_len == 0 and pool_dim % 128 == 0
    h_tiles = t_len // tm
    tm_l0 = _pick(r_len, 768, tm)
    tm_l1 = _pick(t_len, 1024, tm)

    xi, yi, ci = lax.axis_index("x"), lax.axis_index("y"), lax.axis_index("c")
    me = 4 * xi + 2 * yi + ci
    shard = 2 * xi + yi

    def halves(w):
        return w.astype(BF).reshape(2, -1, w.shape[-1])

    ffn_names = ["ffn_w_gate", "ffn_w_up", "ffn_w_down"]
    mixer_names = [["ab_w_in", "w_uq", "w_ukv", "ab_w_out"], ["conv_w_in", "conv_w_out"]]
    big_names = ffn_names + mixer_names[0] + mixer_names[1]

    def stage_names(k):
        return mixer_names[k // 3] if k % 3 == 1 else ffn_names

    def stage_halves(k):
        l, f = k // 3, (k % 3) // 2
        if k % 3 == 1:
            return [halves(weights[nm]) for nm in mixer_names[l]]
        return [halves(weights[nm][l, f]) for nm in ffn_names]

    def gather_start(k, dep):
        own = lax.optimization_barrier((tuple(stage_halves(k)), dep))[0]
        n = len(own)
        return _split_start(f"gather_start_s{k}", list(own) + _landing(N_DEV, own), n * len(CHIP_FLIPS),
                            _chips_gather_build(n))

    gather0 = gather_start(0, c)

    small = jnp.concatenate([norm_g.reshape(6, -1), conv_w[0]], axis=0)
    small = jnp.pad(small, ((0, 7), (0, 0)))
    c_row = jnp.pad(c, ((0, 7), (0, 0))) + gather0['token'][0, 0]
    small_all, c_all = _gather_all("gather_small", [small, c_row])
    small_full = small_all[::2].transpose(1, 0, 2).reshape(16, d)
    gains = [jnp.pad(small_full[3 * l:3 * l + 3], ((0, 5), (0, 0))) for l in range(2)]
    conv_w_full = small_full[6:9]
    c16 = jnp.concatenate([c_all[:, 0], c_ctx[None], jnp.zeros((7, d), F32)], axis=0)

    n_col = w_mod.shape[2]
    b_sh = lax.dynamic_slice_in_dim(b_mod, shard * n_col, n_col, axis=1)
    m_sh = [_mm(f"mod_fwd_{l}", c16, w_mod, 'nn', tm=16, tn=768, a_pre=_silu, b_lead=l,
                epi=lambda acc, i, bv: (acc + bv,), epi_args=(b_sh[l:l + 1],), epi_kinds=('n',)) for l in range(2)]
    m_all = _gather_all("gather_mod", [jnp.concatenate(m_sh, axis=0)], two_level=True)[0]
    m_full = m_all[::2].reshape(N_SHARD, 2, 16, n_col).transpose(1, 2, 0, 3).reshape(2, 16, N_MOD * d)
    mod_h = [jnp.pad(lax.dynamic_index_in_dim(m_full[l], me, 0, keepdims=False).reshape(N_MOD, d), ((0, 7), (0, 0)))
             for l in range(2)]
    mod_g0 = jnp.pad(m_full[0, 8].reshape(N_MOD, d), ((0, 7), (0, 0)))
    mods = [jnp.stack([mod_h[0], mod_g0]), mod_h[1][None]]

    def stage_weights(k, handle, after):
        bufs = _split_wait(f"gather_wait_s{k}", handle, after)
        n = len(bufs) // 2
        own = bufs[:n]
        fwd = _split_start(f"forward_start_s{k}", bufs[n:], n * len(CHIP_FLIPS), _sibling_forward_build(n))
        nxt = gather_start(k + 1, fwd['token']) if k + 1 < 6 else None
        landed = _split_wait(f"forward_wait_s{k}", fwd, fwd['token'])
        full = [lax.dynamic_update_slice_in_dim(z, a, 2 * shard, 0) for z, a in zip(landed, own)]
        gw = {nm: g.reshape(N_SHARD, 2 * g.shape[1], g.shape[2]) for nm, g in zip(stage_names(k), full)}
        return gw, nxt, (fwd['token'] if nxt is None else nxt['token'])

    def ffn_weights(gw):
        return gw["ffn_w_gate"].reshape(N_SHARD, 1, d, fc), gw["ffn_w_up"].reshape(N_SHARD, 1, d, fc), \
            gw["ffn_w_down"], 0

    gw_s0, gather1, tok0 = stage_weights(0, gather0, m_all)
    ffn_w = [[ffn_weights(gw_s0), None], [None, None]]

    s0 = jnp.concatenate([x[0], ctx[0]], axis=0)
    s1, sav_f00 = _ffn_half_fwd("l0a", s0, gains[0], mods[0], 0, ffn_w[0][0], t_len, tm, h_tiles, tm_l0, tok0)

    gw_s1, gather2, tok1 = stage_weights(1, gather1, s1)
    w_out_full = gw_s1["ab_w_out"].reshape(-1, d)
    w_uq_full = gw_s1["w_uq"].reshape(q_rank, heads * (QK_NOPE + QK_ROPE))
    w_ukv_full = gw_s1["w_ukv"].transpose(1, 0, 2).reshape(kv_rank, heads * (QK_NOPE + V_HEAD))
    w_in_full = gw_s1["ab_w_in"].transpose(1, 0, 2).reshape(d, -1)

    wq_p = jnp.pad(w_uq_full.reshape(q_rank, heads, QK_NOPE + QK_ROPE),
                   ((0, 0), (0, 0), (0, HEAD_PAD - QK_NOPE - QK_ROPE))).reshape(q_rank, hw)
    ukv3 = w_ukv_full.reshape(kv_rank, heads, QK_NOPE + V_HEAD)
    wk_top = jnp.pad(ukv3[..., :QK_NOPE], ((0, 0), (0, 0), (0, HEAD_PAD - QK_NOPE))).reshape(kv_rank, hw)
    wv_top = jnp.pad(ukv3[..., QK_NOPE:], ((0, 0), (0, 0), (0, HEAD_PAD - V_HEAD))).reshape(kv_rank, hw)
    src_row = lax.broadcasted_iota(jnp.int32, (HEAD_PAD, hw), 0)
    dst_lane = lax.broadcasted_iota(jnp.int32, (HEAD_PAD, hw), 1) % HEAD_PAD
    spread = ((src_row < QK_ROPE) & (dst_lane == src_row + QK_NOPE)).astype(BF)
    wk_ext = jnp.concatenate([wk_top, spread], axis=0)
    wv_ext = jnp.concatenate([wv_top, jnp.zeros((HEAD_PAD, hw), BF)], axis=0)
    w_in_pool = w_in_full[:, :pool_dim]
    w_in_q = w_in_full[:, pool_dim:pool_dim + q_rank]
    w_in_kvr = jnp.pad(w_in_full[:, pool_dim + q_rank:], ((0, 0), (0, HEAD_PAD - QK_ROPE)))
    w_out_attn = jnp.pad(w_out_full[pool_dim:].reshape(heads, V_HEAD, d),
                         ((0, 0), (0, HEAD_PAD - V_HEAD), (0, 0))).reshape(hw, d)
    w_out_p = jnp.concatenate([w_out_full[:pool_dim], w_out_attn], axis=0)

    u_mix = _adaln_fwd("adaln_l0m", s1, gains[0], 1, mods[0], 1, tm, h_tiles, tok1)
    p_pool = _mm("in_pool", u_mix, w_in_pool, 'nn', tm=tm_l0, tn=pool_dim)
    p_q = _mm("in_q", u_mix, w_in_q, 'nn', tm=tm_l0, tn=q_rank)
    p_kvr = _mm("in_kvr", u_mix, w_in_kvr, 'nn', tm=tm_l0, tn=kvr_w)
    qg = jnp.pad(q_norm_g, ((0, 7), (0, 0)))
    kvg = jnp.pad(kv_norm_g, ((0, 7), (0, 0)))
    tq_c, tq_s = _rope_tables(t_len, g_len, QK_NOPE)
    tk_c, tk_s = _rope_tables(t_len, g_len, 0)

    def qn_fn(rows, consts, m):
        n, _ = _rms(rows[0])
        return [n * consts[0][0:1, :]], {}

    qn = _rows_call("q_norm", qn_fn, r_len, tm, [p_q], [qg], None, [(q_rank, BF)])[0]
    q_r = _mm("q_up", qn, wq_p, 'nn', tm=tm_l0, tn=hw, out_dtypes=(BF,),
              epi=lambda acc, i, ct, st: (_rope(acc, ct, st),), epi_args=(tq_c, tq_s), epi_kinds=('mt', 'mt'))

    def kvn_fn(rows, consts, m):
        pv, ct, st = rows
        n, _ = _rms(pv[:, :kv_rank])
        return [jnp.concatenate([n * consts[0][0:1, :], _rope(pv[:, kv_rank:], ct, st)], axis=1)], {}

    kvn = _rows_call("kv_norm", kvn_fn, r_len, tm, [p_kvr, tk_c, tk_s], [kvg], None, [(kvr_w, BF)])[0]
    k_p = _mm("k_up", kvn, wk_ext, 'nn', tm=tm_l0, tn=hw, out_dtypes=(BF,))
    v_p = _mm("v_up", kvn, wv_ext, 'nn', tm=tm_l0, tn=hw, out_dtypes=(BF,))
    tq_h = _pick(t_len, 512, tm)
    o_h, lse_h = _attn_fwd("attn_h", q_r, k_p, v_p, t_len, 0, r_len, 0, heads, tm, attn_scale)
    o_g, lse_g = _attn_fwd("attn_g", q_r, k_p, v_p, g_len, t_len, g_len, t_len // g_len, heads, tm, attn_scale)
    y_h = _pool_fwd("pool_h", p_pool[:t_len], pool_w[0], pool_scale)
    y_g = _pool_fwd("pool_g", p_pool[t_len:], pool_w[0], pool_scale)
    cat = jnp.concatenate([jnp.concatenate([y_h, y_g], axis=0), jnp.concatenate([o_h, o_g], axis=0)], axis=1)

    def resid_epi(k3, n_lat, tmr):
        def epi(acc, i, sv, mv):
            return sv + _row_gate(mv, k3, i, tmr, n_lat) * acc, acc
        return epi

    s2, o_mix0 = _mm("mix_out_l0", cat, w_out_p, 'nn', tm=tm_l0, tn=d, out_dtypes=(F32, F32),
                     epi=resid_epi(5, t_len, tm_l0), epi_args=(s1, mods[0]), epi_kinds=('mn', 'w'))
    gw_s2, gather3, tok2 = stage_weights(2, gather2, s2)
    ffn_w[0][1] = ffn_weights(gw_s2)
    s3, sav_f01 = _ffn_half_fwd("l0b", s2, gains[0], mods[0], 2, ffn_w[0][1], t_len, tm, h_tiles, tm_l0, tok2)

    gw_s3, gather4, tok3 = stage_weights(3, gather3, s3)
    ffn_w[1][0] = ffn_weights(gw_s3)
    tml = 256 if t_len % 256 == 0 else tm
    h3 = s3[:t_len]
    h4, sav_f10 = _ffn_half_fwd("l1a", h3, gains[1], mods[1], 0, ffn_w[1][0], t_len, tml, None, tm_l1, tok3)
    gw_s4, gather5, tok4 = stage_weights(4, gather4, h4)
    cw_out_full = gw_s4["conv_w_out"].reshape(-1, d)
    cw_in_full = gw_s4["conv_w_in"].transpose(1, 0, 2).reshape(d, -1)
    u_cv = _adaln_fwd("adaln_l1m", h4, gains[1], 1, mods[1], 1, tml, None, tok4)
    p3 = _mm("conv_in", u_cv, cw_in_full, 'nn', tm=tm_l1, tn=512)
    cwp = conv_w_full
    tc = _pick(d, 256)
    y_cv = _conv_fwd("conv_fwd", p3, cwp, tc)
    h5, o_mix1 = _mm("mix_out_l1", y_cv, cw_out_full, 'nn', tm=tm_l1, tn=d, out_dtypes=(F32, F32),
                     epi=resid_epi(5, t_len, tm_l1), epi_args=(h4, mods[1]), epi_kinds=('mn', 'w'))
    gw_s5, _, tok5 = stage_weights(5, gather5, h5)
    ffn_w[1][1] = ffn_weights(gw_s5)
    h6, sav_f11 = _ffn_half_fwd("l1b", h5, gains[1], mods[1], 2, ffn_w[1][1], t_len, tml, None, tm_l1, tok5)

    fg = jnp.pad(final_norm_g[None], ((0, 7), (0, 0)))
    dh6, acc_loss = _loss_head("loss_head", h6, loss_target[0], fg, tml)
    d_final_g = acc_loss[0, 0]

    dgain = [[None] * 3 for _ in range(2)]
    dmod = [[None] * N_MOD for _ in range(2)]

    def put(l, k, triple):
        dmod[l][3 * k], dmod[l][3 * k + 1], dmod[l][3 * k + 2] = triple

    def empty_ffn_grads():
        return (lax.empty((N_SHARD, d, fc), BF), lax.empty((N_SHARD, d, fc), BF), lax.empty((N_SHARD, fc, d), BF))

    def by_shard_rows(g):
        return g.reshape(N_SHARD, -1, g.shape[-1])

    def by_shard_cols(g):
        return g.reshape(g.shape[0], N_SHARD, -1).transpose(1, 0, 2)

    def pair_start(k, big):
        send = [b.astype(BF).reshape(N_DEV, b.shape[1] // 2, b.shape[2]) for b in big]
        n = len(send)
        return _split_start(f"grads_pair_start_s{k}", send + _landing(N_SHARD, send), n * N_SHARD,
                            _sibling_halves_build(n))

    def chips_start(k, handle, after):
        bufs = _split_wait(f"grads_pair_wait_s{k}", handle, after)
        n = len(bufs) // 2
        pre = [_add_halves(f"grads_add_s{k}_{nm}", s, z) for nm, s, z in zip(stage_names(k), bufs[:n], bufs[n:])]
        return _split_start(f"grads_start_s{k}", pre + _landing(N_SHARD, pre), n * len(CHIP_FLIPS),
                            _chips_scatter_build(n))

    def landed_sums(k, handle, after):
        bufs = _split_wait(f"grads_wait_s{k}", handle, after)
        n = len(bufs) // 2
        landed = [lax.dynamic_update_slice_in_dim(z, lax.dynamic_slice_in_dim(p, shard, 1, 0), shard, 0)
                  for p, z in zip(bufs[:n], bufs[n:])]
        return [_sum_lead(f"sum_grads_s{k}_{nm}", z) for nm, z in zip(stage_names(k), landed)]

    pair, scatter, sums = [None] * 6, [None] * 6, [None] * 6
    grads, upd = {}, {}
    ffn_out = {nm: None for nm in ffn_names}

    def lanes_last(nm, t):
        shp = weights[nm].shape
        return jnp.swapaxes(t, -1, -2) if shp[-1] % 128 and shp[-2] % 128 == 0 else t

    def finish_stage(k, halves):
        n = len(halves)
        lands = [pltpu.with_memory_space_constraint(lax.empty(h.shape, h.dtype), pltpu.HBM) for h in halves]
        swap = _split_start(f"swap_start_s{k}", list(halves) + lands, n, _sibling_whole_build(n))
        both = _split_wait(f"swap_wait_s{k}", swap, swap['token'])
        for nm, a, g in zip(stage_names(k), both[:n], both[n:]):
            piece = jnp.where(ci == 0, jnp.concatenate([a, g], axis=0), jnp.concatenate([g, a], axis=0))
            if k % 3 == 1:
                grads[nm] = piece.reshape(weights[nm].shape)
                upd[nm] = _adamw(f"adamw_{nm}", weights[nm], grads[nm], mom_m[nm], mom_v[nm])
            else:
                ffn_out[nm] = _adamw_piece(f"adamw_{nm}_s{k}", lanes_last(nm, weights[nm]), lanes_last(nm, piece),
                                           lanes_last(nm, mom_m[nm]), lanes_last(nm, mom_v[nm]),
                                           (k // 3, (k % 3) // 2), ffn_out[nm])
    dh5, ffn_g, tr, dgain[1][2] = _ffn_half_bwd("l1b", dh6, sav_f11, gains[1], mods[1], 2, ffn_w[1][1],
                                                empty_ffn_grads(), tml, None, tm_l1)
    put(1, 2, tr)
    pair[5] = pair_start(5, list(ffn_g))
    d_o1, acc_g1 = _resid_bwd("resid_bwd_l1m", dh5, o_mix1, mods[1], 1, 1.0, tml, None, pair[5]['token'])
    dy_cv = _mm("mix_out_l1_dx", d_o1, cw_out_full, 'nt', tm=tm_l1, tn=d)
    d_cw_out = _mm("mix_out_l1_dw", y_cv, d_o1, 'tn', tm=512, tn=512)
    dp3, d_cw = _conv_bwd("conv_bwd", p3, cwp, dy_cv, tc)
    du_cv = _conv_din("conv_in_dx", dp3, cw_in_full, tm_l1)
    d_cw_in = _conv_dw_in("conv_in_dw", u_cv, dp3, _pick(d, MM_ROWS), _pick(d, 512))
    dh4, acc_n1 = _adaln_bwd("adaln_bwd_l1m", h4, du_cv, dh5, gains[1], 1, mods[1], 1, tml, None)
    put(1, 1, (acc_n1[:, 0], acc_n1[:, 1], acc_g1[:, 0]))
    dgain[1][1] = acc_n1[0, 2]
    scatter[5] = chips_start(5, pair[5], dh4)
    pair[4] = pair_start(4, [by_shard_cols(d_cw_in), by_shard_rows(d_cw_out)])
    dh3, ffn_g, tr, dgain[1][0] = _ffn_half_bwd("l1a", dh4, sav_f10, gains[1], mods[1], 0, ffn_w[1][0],
                                                empty_ffn_grads(), tml, None, tm_l1,
                                                scatter[5]['token'] + pair[4]['token'])
    put(1, 0, tr)
    finish_stage(5, landed_sums(5, scatter[5], dh3))
    scatter[4] = chips_start(4, pair[4], dh3)
    pair[3] = pair_start(3, list(ffn_g))

    ds3 = jnp.concatenate([dh3, jnp.zeros((g_len, d), F32)], axis=0) \
        + (scatter[4]['token'][0, 0] + pair[3]['token'][0, 0])
    ds2, ffn_g, tr, dgain[0][2] = _ffn_half_bwd("l0b", ds3, sav_f01, gains[0], mods[0], 2, ffn_w[0][1],
                                                empty_ffn_grads(), tm, h_tiles, tm_l0)
    put(0, 2, tr)
    finish_stage(4, landed_sums(4, scatter[4], ds2))
    scatter[3] = chips_start(3, pair[3], ds2)
    pair[2] = pair_start(2, list(ffn_g))
    d_o0, acc_g0 = _resid_bwd("resid_bwd_l0m", ds2, o_mix0, mods[0], 1, 1.0, tm, h_tiles,
                              scatter[3]['token'] + pair[2]['token'])
    dcat = _mm("mix_out_l0_dx", d_o0, w_out_p, 'nt', tm=tm_l0, tn=pool_dim + hw)
    d_w_out_p = _mm("mix_out_l0_dw", cat, d_o0, 'tn', tm=512, tn=512)
    col_blk = pool_dim // HEAD_PAD
    dq_h, dk_h, dv_h = _attn_bwd("attn_bwd_h", q_r, k_p, v_p, cat, dcat, lse_h, t_len, 0, r_len, 0, heads, tq_h,
                                 attn_scale, col_blk)
    dq_g, dk_all, dv_all = _attn_bwd("attn_bwd_g", q_r, k_p, v_p, cat, dcat, lse_g, g_len, t_len, g_len,
                                     t_len // g_len, heads, tm, attn_scale, col_blk, onto=(dk_h, dv_h))
    dq_all = jnp.concatenate([dq_h, dq_g], axis=0)
    dkvn = _mm("k_up_dx", dk_all, wk_ext, 'nt', tm=tm_l0, tn=kvr_w)
    dkvn = _mm("v_up_dx", dv_all, wv_ext, 'nt', tm=tm_l0, tn=kvr_w, epi=lambda acc, i, prev: (acc + prev,),
               epi_args=(dkvn,), epi_kinds=('mn',))
    d_wk_ext = _mm("k_up_dw", kvn, dk_all, 'tn', tm=kvr_w, tn=512)
    d_wv_ext = _mm("v_up_dw", kvn, dv_all, 'tn', tm=kvr_w, tn=512)

    def kvn_bwd_fn(rows, consts, m):
        pv, dv_, ct, st = rows
        g = consts[0][0:1, :]
        n, r = _rms(pv[:, :kv_rank])
        dyn = dv_[:, :kv_rank]
        dckv = _rms_bwd(dyn * g, n, r)
        dkr = _rope_t(dv_[:, kv_rank:], ct, st)
        return [jnp.concatenate([dckv, dkr], axis=1)], {0: jnp.sum(dyn * n, axis=0, keepdims=True)}

    dp_kvr, acc_kvg = _rows_call("kv_norm_bwd", kvn_bwd_fn, r_len, tm, [p_kvr, dkvn, tk_c, tk_s], [kvg], None,
                                 [(kvr_w, BF)], acc_w=kv_rank)

    def qrope_bwd_fn(rows, consts, m):
        return [_rope_t(rows[0], rows[1], rows[2])], {}

    dq_pad = _rows_call("q_rope_bwd", qrope_bwd_fn, r_len, tm, [dq_all, tq_c, tq_s], [], None, [(hw, BF)])[0]
    dqn = _mm("q_up_dx", dq_pad, wq_p, 'nt', tm=tm_l0, tn=q_rank)
    d_wq_p = _mm("q_up_dw", qn, dq_pad, 'tn', tm=512, tn=512)

    def qn_bwd_fn(rows, consts, m):
        pv, dv_ = rows
        g = consts[0][0:1, :]
        n, r = _rms(pv)
        return [_rms_bwd(dv_ * g, n, r)], {0: jnp.sum(dv_ * n, axis=0, keepdims=True)}

    dp_q, acc_qg = _rows_call("q_norm_bwd", qn_bwd_fn, r_len, tm, [p_q, dqn], [qg], None, [(q_rank, BF)],
                              acc_w=q_rank)
    dpu_h, dpw_h, dps_h = _pool_bwd("pool_bwd_h", p_pool[:t_len], dcat, pool_w[0], pool_scale, 0)
    dpu_g, dpw_g, dps_g = _pool_bwd("pool_bwd_g", p_pool[t_len:], dcat, pool_w[0], pool_scale, t_len)
    dp_pool = jnp.concatenate([dpu_h, dpu_g], axis=0)
    add_prev = lambda acc, i, prev: (acc + prev,)
    du_mix = _mm("in_pool_dx", dp_pool, w_in_pool, 'nt', tm=tm_l0, tn=d)
    du_mix = _mm("in_q_dx", dp_q, w_in_q, 'nt', tm=tm_l0, tn=d, epi=add_prev, epi_args=(du_mix,), epi_kinds=('mn',))
    du_mix = _mm("in_kvr_dx", dp_kvr, w_in_kvr, 'nt', tm=tm_l0, tn=d, epi=add_prev, epi_args=(du_mix,), epi_kinds=('mn',))
    d_w_in = jnp.concatenate([
        _mm("in_pool_dw", u_mix, dp_pool, 'tn', tm=512, tn=pool_dim),
        _mm("in_q_dw", u_mix, dp_q, 'tn', tm=512, tn=q_rank),
        _mm("in_kvr_dw", u_mix, dp_kvr, 'tn', tm=512, tn=kvr_w)[:, :kv_rank + QK_ROPE]], axis=1)
    ds1, acc_n0 = _adaln_bwd("adaln_bwd_l0m", s1, du_mix, ds2, gains[0], 1, mods[0], 1, tm, h_tiles)
    put(0, 1, (acc_n0[:, 0], acc_n0[:, 1], acc_g0[:, 0]))
    dgain[0][1] = jnp.sum(acc_n0[:, 2], axis=0)
    d_w_uq = d_wq_p.reshape(q_rank, heads, HEAD_PAD)[..., :QK_NOPE + QK_ROPE].reshape(q_rank, -1)
    d_w_ukv = jnp.concatenate([d_wk_ext[:kv_rank].reshape(kv_rank, heads, HEAD_PAD)[..., :QK_NOPE],
                               d_wv_ext[:kv_rank].reshape(kv_rank, heads, HEAD_PAD)[..., :V_HEAD]],
                              axis=-1).reshape(kv_rank, -1)
    d_w_out = jnp.concatenate([d_w_out_p[:pool_dim],
                               d_w_out_p[pool_dim:].reshape(heads, HEAD_PAD, d)[:, :V_HEAD].reshape(-1, d)], axis=0)
    finish_stage(3, landed_sums(3, scatter[3], ds1))
    scatter[2] = chips_start(2, pair[2], ds1)
    pair[1] = pair_start(1, [by_shard_cols(d_w_in), by_shard_rows(d_w_uq), by_shard_cols(d_w_ukv),
                             by_shard_rows(d_w_out)])
    ds0, ffn_g, tr, dgain[0][0] = _ffn_half_bwd("l0a", ds1, sav_f00, gains[0], mods[0], 0, ffn_w[0][0],
                                                empty_ffn_grads(), tm, h_tiles, tm_l0,
                                                scatter[2]['token'] + pair[1]['token'])
    put(0, 0, tr)
    grad_x = ds0[:t_len][None]
    finish_stage(2, landed_sums(2, scatter[2], ds0))
    pair[0] = pair_start(0, list(ffn_g))

    dmh = jnp.stack([jnp.stack([dmod[l][k][0] for k in range(N_MOD)]) for l in range(2)])
    dmg0 = jnp.stack([dmod[0][k][1] for k in range(N_MOD)])
    dg_rows = jnp.stack([dgain[l][k] for l in range(2) for k in range(3)])
    pieces = [dmh.reshape(2 * N_MOD, d), dmg0, dg_rows, d_cw[:3], d_final_g[None],
              (dpw_h + dpw_g).reshape(-1, d), jnp.pad((dps_h + dps_g)[0], (0, d - pool_dim))[None],
              jnp.pad(acc_qg[0, 0], (0, d - q_rank))[None], jnp.pad(acc_kvg[0, 0], (0, d - kv_rank))[None],
              acc_loss[0, 1][None]]
    n_piece = [p.shape[0] for p in pieces]
    pieces = [jnp.pad(p, ((0, (-p.shape[0]) % 8), (0, 0))) for p in pieces]
    small_g = jnp.concatenate(pieces, axis=0) + pair[0]['token'][0, 0]
    sg_all = _gather_all("gather_small_grads", [small_g], two_level=True)[0]
    sg_sum = _sum_lead("sum_small_grads", sg_all)
    scatter[1] = chips_start(1, pair[1], sg_sum)
    offs = [0]
    for p in pieces:
        offs.append(offs[-1] + p.shape[0])
    part = lambda j: sg_sum[offs[j]:offs[j] + n_piece[j]]
    sum_dmh, sum_dmg0, g_norm_full, g_conv_w_full = part(0).reshape(2, N_MOD * d), part(1).reshape(N_MOD * d), part(2), part(3)
    g_final = part(4)[0]
    loss = part(9)[0, 0]
    g_pool_w = part(5).reshape(pool_w.shape)
    g_pool_scale = part(6)[:, :pool_dim]
    g_q_norm = part(7)[:, :q_rank]
    g_kv_norm = part(8)[:, :kv_rank]
    col0 = shard * (d // N_SHARD)
    g_norm_g = lax.dynamic_slice_in_dim(g_norm_full.reshape(2, 3, d), col0, d // N_SHARD, axis=2)
    g_conv_w = lax.dynamic_slice_in_dim(g_conv_w_full, col0, d // N_SHARD, axis=1)[None]
    g_b_mod = _sum_lead("sum_b_mod", jnp.stack([sum_dmh, jnp.stack([sum_dmg0, jnp.zeros_like(sum_dmg0)])]))

    dm16 = []
    for l in range(2):
        per_dev = sg_all[:, l * N_MOD:(l + 1) * N_MOD].reshape(N_DEV, N_MOD * d)
        row8 = (sum_dmg0 if l == 0 else jnp.zeros_like(sum_dmg0)) + scatter[1]['token'][0, 0]
        full = jnp.concatenate([per_dev, row8[None], jnp.zeros((7, N_MOD * d), F32)], axis=0)
        dm16.append(lax.dynamic_slice_in_dim(full, shard * n_col, n_col, axis=1))
    g_w_mod = [_mm(f"mod_dw_{l}", c16, dm16[l], 'tn', tm=512, tn=768, a_pre=_silu) for l in range(2)]
    dc16 = _mm("mod_dx", dm16[0], w_mod, 'nt', tm=16, tn=512, b_lead=0, epi=lambda acc, i, cv: (acc * _dsilu(cv),),
               epi_args=(c16,), epi_kinds=('mn',))
    dc_all = _gather_all("gather_dc", [dc16])[0]
    g_c_ctx = _sum_lead("sum_dc", dc_all[::2])[8]

    grads.update(c_ctx=g_c_ctx, norm_g=g_norm_g, b_mod=g_b_mod, pool_w=g_pool_w,
                 pool_scale=g_pool_scale, q_norm_g=g_q_norm, kv_norm_g=g_kv_norm, conv_w=g_conv_w, final_norm_g=g_final)
    names = list(weights)

    scatter[0] = chips_start(0, pair[0], g_c_ctx)
    upd.update({n: _adamw(f"adamw_{n}", weights[n], grads[n].reshape(weights[n].shape), mom_m[n], mom_v[n],
                          scatter[0]['token']) for n in names if n not in big_names and n != "w_mod"})
    mod_out = None
    for l in range(2):
        mod_out = _adamw_piece(f"adamw_w_mod_{l}", w_mod, g_w_mod[l], mom_m["w_mod"], mom_v["w_mod"], (l,), mod_out,
                               scatter[0]['token'])
    grads["w_mod"], upd["w_mod"] = mod_out[0], mod_out[1:]
    finish_stage(1, landed_sums(1, scatter[1], upd["w_mod"][0]))
    finish_stage(0, landed_sums(0, scatter[0], upd[mixer_names[0][-1]][0]))
    for nm in ffn_names:
        done = [lanes_last(nm, t) for t in ffn_out[nm]]
        grads[nm], upd[nm] = done[0], done[1:]
    return (loss, grad_x, *[grads[n].reshape(weights[n].shape) for n in names], *[upd[n][0] for n in names],
            *[upd[n][1] for n in names], *[upd[n][2] for n in names])
```

```python
import math

import jax
import jax.numpy as jnp
from jax import lax
from jax.experimental import pallas as pl
from jax.experimental.pallas import tpu as pltpu

F32 = jnp.float32
BF = jnp.bfloat16
MESH = pl.DeviceIdType.MESH

N_DEV = 8
N_SHARD = 4
RMS_EPS = 1e-6
N_MOD = 9
POOL_WINDOWS = (2, 4, 8, 16)
QK_NOPE = 64
QK_ROPE = 32
V_HEAD = 64
HEAD_PAD = 128
GRID_W = 64
ROPE_THETA = 10000.0
POOL_PAD = 16
ADAM_LR, ADAM_B1, ADAM_B2, ADAM_EPS, ADAM_WD, ADAM_STEP = 0.001, 0.9, 0.999, 1e-08, 0.01, 10
VMEM_LIMIT = 56 * 1024 * 1024
MM_ROWS = 1024


def _pcall(body, **kw):
    return pl.pallas_call(body, **kw)


def _params(sem=None):
    return pltpu.CompilerParams(dimension_semantics=sem, vmem_limit_bytes=VMEM_LIMIT)


def _pick(n, pref, mult=128):
    best = None
    d = mult
    while d <= min(n, pref):
        if n % d == 0:
            best = d
        d += mult
    return best if best is not None else n


def _silu(z):
    return z * jax.nn.sigmoid(z)


def _dsilu(z):
    s = jax.nn.sigmoid(z)
    return s * (1.0 + z * (1.0 - s))


def _dot(a, b, dims):
    return lax.dot_general(a.astype(BF), b.astype(BF), (dims, ((), ())), preferred_element_type=F32)


NN = ((1,), (0,))
NT = ((1,), (1,))
TN = ((0,), (0,))


ALL_FLIPS = [(kx, ky, kc) for kx in (0, 1) for ky in (0, 1) for kc in (0, 1) if (kx, ky, kc) != (0, 0, 0)]
CHIP_FLIPS = [(1, 0, 0), (0, 1, 0), (1, 1, 0)]
SIBLING = (0, 0, 1)
COMM_SPLIT = 8
SPLIT_MIN_ROWS = 256


def _exchange(name, arrays, plan, lead, whole_src, split=COMM_SPLIT):
    n = len(arrays)
    blk_shapes = [tuple(a.shape) if whole_src else tuple(a.shape[1:]) for a in arrays]
    splits = []
    for shp in blk_shapes:
        s = 1
        while s * 2 <= split and shp[0] % (s * 2) == 0 and (shp[0] // (s * 2)) % 16 == 0 \
                and shp[0] // (s * 2) >= SPLIT_MIN_ROWS:
            s *= 2
        splits.append(s)
    items = plan(0, 0, 0)
    n_items = len(items)
    remote_ids = [k for k, it in enumerate(items) if it[0] is not None]
    local_ids = [k for k, it in enumerate(items) if it[0] is None]
    slots = [(a, s) for s in range(max(splits)) for a in range(n) if s < splits[a]]
    n_slot = len(slots)

    def body(*refs):
        ins, outs = refs[:n], refs[n:2 * n]
        send_sems, recv_sems, loc_sems = refs[2 * n:]
        x, y, c = lax.axis_index("x"), lax.axis_index("y"), lax.axis_index("c")
        plan_here = plan(x, y, c)

        def rows(ref, a, s):
            rc = blk_shapes[a][0] // splits[a]
            return ref.at[pl.ds(s * rc, rc)]

        def make(si, k):
            a, s = slots[si]
            flip, src, dst, _ = plan_here[k]
            base = outs[a] if src[0] == 'out' else ins[a]
            src_ref = rows(base if src[1] is None else base.at[src[1]], a, s)
            dst_ref = rows(outs[a].at[dst], a, s)
            if flip is None:
                return pltpu.make_async_copy(src_ref, dst_ref, loc_sems.at[si * max(1, len(local_ids)) + local_ids.index(k)])
            peer = (1 - x if flip[0] else x, 1 - y if flip[1] else y, 1 - c if flip[2] else c)
            sem = si * len(remote_ids) + remote_ids.index(k)
            return pltpu.make_async_remote_copy(src_ref=src_ref, dst_ref=dst_ref, send_sem=send_sems.at[sem],
                                                recv_sem=recv_sems.at[sem], device_id=peer, device_id_type=MESH)

        copies = {}
        for si in range(n_slot):
            for k in range(n_items):
                if plan_here[k][3] is None:
                    copies[si, k] = make(si, k)
                    copies[si, k].start()
        arrived = set()
        for si in range(n_slot):
            for k in range(n_items):
                after = plan_here[k][3]
                if after is not None:
                    if (si, after) not in arrived:
                        copies[si, after].wait_recv()
                        arrived.add((si, after))
                    copies[si, k] = make(si, k)
                    copies[si, k].start()
        for (si, k), cp in copies.items():
            if plan_here[k][0] is None:
                cp.wait()
            else:
                cp.wait_send()
                if (si, k) not in arrived:
                    cp.wait_recv()

    any_spec = pl.BlockSpec(memory_space=pl.ANY)
    n_rem = max(1, n_slot * len(remote_ids))
    outs = _pcall(
        body, name=name,
        out_shape=[jax.ShapeDtypeStruct((lead,) + s, a.dtype) for s, a in zip(blk_shapes, arrays)],
        in_specs=[any_spec] * n, out_specs=[any_spec] * n,
        scratch_shapes=[pltpu.SemaphoreType.DMA((n_rem,)), pltpu.SemaphoreType.DMA((n_rem,)),
                        pltpu.SemaphoreType.DMA((max(1, n_slot * len(local_ids)),))],
    )(*arrays)
    return list(outs)


def _place(x, y, c):
    return 4 * x + 2 * y + c


def _flip(v, f):
    return 1 - v if f else v


def _gather_all(name, arrays, two_level=False):
    def plan(x, y, c):
        me = _place(x, y, c)
        if not two_level:
            return [(None, ('in', None), me, None)] + [(f, ('in', None), me, None) for f in ALL_FLIPS]
        items = [(None, ('in', None), me, None), (SIBLING, ('in', None), me, None)]
        items += [(f, ('in', None), me, None) for f in CHIP_FLIPS]
        for j, f in enumerate(CHIP_FLIPS):
            got = _place(_flip(x, f[0]), _flip(y, f[1]), c)
            items.append((SIBLING, ('out', got), got, 2 + j))
        return items
    return _exchange(name, arrays, plan, N_DEV, True, split=1)


HBM_SPEC = pl.BlockSpec(memory_space=pltpu.HBM)
SEM_SPEC = pl.BlockSpec(memory_space=pltpu.SEMAPHORE)
SIDE_EFFECT = pltpu.SideEffectType.DATAFLOW_SIDE_EFFECTING


def _split_start(name, bufs, n_copies, build):
    n = len(bufs)

    def body(*refs):
        for cp in build(refs[:n], refs[n], refs[n + 1]):
            cp.start()
        token = refs[-1]
        token[...] = jnp.zeros_like(token)

    res = _pcall(
        body, name=name,
        out_shape=(pltpu.SemaphoreType.DMA((n_copies,)), pltpu.SemaphoreType.DMA((n_copies,)),
                   *[pltpu.HBM(b.shape, b.dtype) for b in bufs], jax.ShapeDtypeStruct((8, 128), F32)),
        in_specs=[HBM_SPEC] * n,
        out_specs=(SEM_SPEC, SEM_SPEC, *[HBM_SPEC] * n, pl.BlockSpec(memory_space=pltpu.VMEM)),
        input_output_aliases={i: 2 + i for i in range(n)},
        compiler_params=pltpu.CompilerParams(has_side_effects=SIDE_EFFECT),
    )(*[pltpu.with_memory_space_constraint(b, pltpu.HBM) for b in bufs])
    return dict(send=res[0], recv=res[1], bufs=list(res[2:2 + n]), token=res[-1], build=build)


def _split_wait(name, handle, after):
    n = len(handle['bufs'])
    build = handle['build']

    def body(*refs):
        for cp in build(refs[:n], refs[n], refs[n + 1]):
            cp.wait_send()
            cp.wait_recv()

    res = _pcall(
        body, name=name, out_shape=tuple(pltpu.HBM(b.shape, b.dtype) for b in handle['bufs']),
        in_specs=[HBM_SPEC] * n + [SEM_SPEC, SEM_SPEC, pl.BlockSpec(memory_space=pl.ANY)],
        out_specs=tuple([HBM_SPEC] * n), input_output_aliases={i: i for i in range(n)},
        compiler_params=pltpu.CompilerParams(has_side_effects=SIDE_EFFECT),
    )(*handle['bufs'], handle['send'], handle['recv'], after)
    return list(res)


def _landing(lead, arrays):
    return [pltpu.with_memory_space_constraint(lax.empty((lead,) + tuple(a.shape[1:]), a.dtype), pltpu.HBM)
            for a in arrays]


def _copy_list(n, per_array, make):
    def build(refs, send_sems, recv_sems):
        copies = []
        for a in range(n):
            for j in range(per_array):
                src, dst, peer = make(refs, a, j)
                k = a * per_array + j
                copies.append(pltpu.make_async_remote_copy(src_ref=src, dst_ref=dst, send_sem=send_sems.at[k],
                                                           recv_sem=recv_sems.at[k], device_id=peer,
                                                           device_id_type=MESH))
        return copies
    return build


def _mesh_place():
    x, y, c = lax.axis_index("x"), lax.axis_index("y"), lax.axis_index("c")
    return x, y, c, 2 * x + y


def _chips_gather_build(n):
    def make(refs, a, j):
        x, y, c, chip = _mesh_place()
        px, py = _flip(x, CHIP_FLIPS[j][0]), _flip(y, CHIP_FLIPS[j][1])
        return refs[a].at[c], refs[n + a].at[2 * chip + c], (px, py, c)
    return _copy_list(n, len(CHIP_FLIPS), make)


def _chips_scatter_build(n):
    def make(refs, a, j):
        x, y, c, chip = _mesh_place()
        px, py = _flip(x, CHIP_FLIPS[j][0]), _flip(y, CHIP_FLIPS[j][1])
        return refs[a].at[2 * px + py], refs[n + a].at[chip], (px, py, c)
    return _copy_list(n, len(CHIP_FLIPS), make)


def _sibling_forward_build(n):
    def make(refs, a, j):
        x, y, c, _ = _mesh_place()
        blk = 2 * (2 * _flip(x, CHIP_FLIPS[j][0]) + _flip(y, CHIP_FLIPS[j][1])) + c
        return refs[a].at[blk], refs[a].at[blk], (x, y, 1 - c)
    return _copy_list(n, len(CHIP_FLIPS), make)


def _sibling_halves_build(n):
    def make(refs, a, j):
        x, y, c, _ = _mesh_place()
        return refs[a].at[2 * j + 1 - c], refs[n + a].at[j], (x, y, 1 - c)
    return _copy_list(n, N_SHARD, make)


def _sibling_whole_build(n):
    def make(refs, a, j):
        x, y, c, _ = _mesh_place()
        return refs[a], refs[n + a], (x, y, 1 - c)
    return _copy_list(n, 1, make)


def _add_halves(name, send, land):
    _, r, cdim = send.shape
    tr = _pick(r, max(16, (1 << 20) // (cdim * 2)), 16)

    def body(c_ref, own_ref, got_ref, o_ref):
        o_ref[...] = (own_ref[...].astype(F32) + got_ref[...].astype(F32)).astype(BF)

    grid_spec = pltpu.PrefetchScalarGridSpec(
        num_scalar_prefetch=1, grid=(N_SHARD, r // tr),
        in_specs=[pl.BlockSpec((None, tr, cdim), lambda sh, i, cr: (2 * sh + cr[0], i, 0)),
                  pl.BlockSpec((None, tr, cdim), lambda sh, i, cr: (sh, i, 0))],
        out_specs=pl.BlockSpec((None, tr, cdim), lambda sh, i, cr: (sh, i, 0)))
    core = lax.axis_index("c").astype(jnp.int32).reshape(1)
    return _pcall(body, name=name, grid_spec=grid_spec, out_shape=jax.ShapeDtypeStruct((N_SHARD, r, cdim), BF),
                  compiler_params=_params(("arbitrary", "arbitrary")))(core, send, land)


def _sum_lead(name, arr, out_dtype=F32):
    n, r, cdim = arr.shape
    tr = r
    limit = (4 << 20) // (n * cdim * arr.dtype.itemsize)
    if r > limit:
        tr = _pick(r, max(limit, 16), 16)

    def body(x_ref, o_ref):
        acc = x_ref[0].astype(F32)
        for d in range(1, n):
            acc = acc + x_ref[d].astype(F32)
        o_ref[...] = acc.astype(out_dtype)

    return _pcall(body, name=name, grid=(r // tr,),
                  in_specs=[pl.BlockSpec((n, tr, cdim), lambda i: (0, i, 0))],
                  out_specs=pl.BlockSpec((tr, cdim), lambda i: (i, 0)),
                  out_shape=jax.ShapeDtypeStruct((r, cdim), out_dtype),
                  compiler_params=_params(("arbitrary",)))(arr)


def _rows_call(name, fn, n_rows, tm, rows, consts, mod, outs, acc_w=None, h_tiles=None):
    nt = n_rows // tm
    ht = nt if h_tiles is None else h_tiles
    ng = 1 if mod is None else mod.shape[0]
    n_r, n_c, n_o = len(rows), len(consts), len(outs)
    has_mod = mod is not None

    def body(*refs):
        i = pl.program_id(0)
        first = (i % ht) == 0
        row_refs, const_refs = refs[:n_r], refs[n_r:n_r + n_c]
        p = n_r + n_c
        mod_tile = refs[p][...] if has_mod else None
        p += int(has_mod)
        out_refs = refs[p:p + n_o]
        o, acc = fn([r[...] for r in row_refs], [r[...] for r in const_refs], mod_tile)
        for r, v in zip(out_refs, o):
            r[...] = v.astype(r.dtype)
        if acc_w is not None:
            acc_ref = refs[p + n_o]

            @pl.when(first)
            def _():
                acc_ref[...] = jnp.zeros_like(acc_ref)

            for k, v in acc.items():
                acc_ref[k:k + 1, :] += v

    in_specs = [pl.BlockSpec((tm, r.shape[1]), lambda i: (i, 0)) for r in rows]
    in_specs += [pl.BlockSpec(cst.shape, lambda i, nd=cst.ndim: (0,) * nd) for cst in consts]
    args = list(rows) + list(consts)
    if has_mod:
        in_specs.append(pl.BlockSpec((None,) + mod.shape[1:], lambda i: (i // ht, 0, 0)))
        args.append(mod)
    out_shape = [jax.ShapeDtypeStruct((n_rows, w), dt) for w, dt in outs]
    out_specs = [pl.BlockSpec((tm, w), lambda i: (i, 0)) for w, _ in outs]
    if acc_w is not None:
        out_shape.append(jax.ShapeDtypeStruct((ng, 8, acc_w), F32))
        out_specs.append(pl.BlockSpec((None, 8, acc_w), lambda i: (i // ht, 0, 0)))
    res = _pcall(body, name=name, grid=(nt,), in_specs=in_specs, out_specs=out_specs, out_shape=out_shape,
                 compiler_params=_params(("arbitrary",)))(*args)
    return list(res)


def _rms(s):
    r = lax.rsqrt(jnp.mean(s * s, axis=1, keepdims=True) + RMS_EPS)
    return s * r, r


def _rms_bwd(dn, n, r):
    return r * (dn - n * jnp.mean(dn * n, axis=1, keepdims=True))


def _adaln_fwd(name, s, gains, gain_row, mod, k, tm, h_tiles, after=None):
    def fn(rows, consts, m):
        n, _ = _rms(rows[0])
        y = n * consts[0][gain_row:gain_row + 1, :]
        return [y * (1.0 + m[3 * k + 1:3 * k + 2, :]) + m[3 * k:3 * k + 1, :]], {}

    d = s.shape[1]
    consts = [gains] if after is None else [gains, after]
    return _rows_call(name, fn, s.shape[0], tm, [s], consts, mod, [(d, BF)], h_tiles=h_tiles)[0]


def _adaln_bwd(name, s, du, ds_res, gains, gain_row, mod, k, tm, h_tiles):
    def fn(rows, consts, m):
        sv, duv, res = rows
        gain = consts[0][gain_row:gain_row + 1, :]
        n, r = _rms(sv)
        y = n * gain
        dy = duv * (1.0 + m[3 * k + 1:3 * k + 2, :])
        acc = {0: jnp.sum(duv, axis=0, keepdims=True), 1: jnp.sum(duv * y, axis=0, keepdims=True),
               2: jnp.sum(dy * n, axis=0, keepdims=True)}
        return [_rms_bwd(dy * gain, n, r) + res], acc

    d = s.shape[1]
    return _rows_call(name, fn, s.shape[0], tm, [s, du, ds_res], [gains], mod, [(d, F32)], acc_w=d, h_tiles=h_tiles)


def _resid_bwd(name, ds_out, o, mod, k, cst, tm, h_tiles, after=None):
    def fn(rows, consts, m):
        dsv, ov = rows
        gate = m[3 * k + 2:3 * k + 3, :]
        return [cst * gate * dsv], {0: jnp.sum(cst * ov * dsv, axis=0, keepdims=True)}

    d = o.shape[1]
    consts = [] if after is None else [after]
    return _rows_call(name, fn, o.shape[0], tm, [ds_out, o], consts, mod, [(d, BF)], acc_w=d, h_tiles=h_tiles)


def _mm(name, a, b, mode, tm=256, tn=512, out_dtypes=(F32,), epi=None, epi_args=(), epi_kinds=(), a_pre=None,
        b_lead=None):
    bshape = b.shape if b_lead is None else b.shape[1:]
    if mode == 'nn':
        (m, kd), nd = a.shape, bshape[1]
    elif mode == 'nt':
        (m, kd), nd = a.shape, bshape[0]
    else:
        (kd, m), nd = a.shape, bshape[1]
    tm = _pick(m, tm, 16) if m % tm else tm
    tn = _pick(nd, tn, 128) if nd % tn else tn
    dims = {'nn': NN, 'nt': NT, 'tn': TN}[mode]
    n_e, n_o = len(epi_args), len(out_dtypes)

    def body(*refs):
        i = pl.program_id(1)
        av = refs[0][...]
        if a_pre is not None:
            av = a_pre(av)
        acc = _dot(av, refs[1][...], dims)
        res = (acc,) if epi is None else epi(acc, i, *[r[...] for r in refs[2:2 + n_e]])
        for r, v in zip(refs[2 + n_e:], res):
            r[...] = v.astype(r.dtype)

    if mode == 'nn':
        specs = [pl.BlockSpec((tm, kd), lambda j, i: (i, 0)), pl.BlockSpec((kd, tn), lambda j, i: (0, j))]
    elif mode == 'nt':
        specs = [pl.BlockSpec((tm, kd), lambda j, i: (i, 0)), pl.BlockSpec((tn, kd), lambda j, i: (j, 0))]
    else:
        specs = [pl.BlockSpec((kd, tm), lambda j, i: (0, i)), pl.BlockSpec((kd, tn), lambda j, i: (0, j))]
    if b_lead is not None:
        shape2, at2 = specs[1].block_shape, specs[1].index_map
        specs[1] = pl.BlockSpec((None,) + tuple(shape2), lambda j, i: (b_lead,) + tuple(at2(j, i)))
    for arr, kind in zip(epi_args, epi_kinds):
        if kind == 'mn':
            specs.append(pl.BlockSpec((tm, tn), lambda j, i: (i, j)))
        elif kind == 'n':
            specs.append(pl.BlockSpec((1, tn), lambda j, i: (0, j)))
        elif kind == 'mt':
            specs.append(pl.BlockSpec((tm, arr.shape[1]), lambda j, i: (i, 0)))
        else:
            specs.append(pl.BlockSpec(arr.shape, lambda j, i, nd_=arr.ndim: (0,) * nd_))
    res = _pcall(body, name=name, grid=(nd // tn, m // tm), in_specs=specs,
                 out_specs=[pl.BlockSpec((tm, tn), lambda j, i: (i, j))] * n_o,
                 out_shape=[jax.ShapeDtypeStruct((m, nd), dt) for dt in out_dtypes],
                 compiler_params=_params(("arbitrary", "arbitrary")))(a, b, *epi_args)
    return res[0] if n_o == 1 else list(res)


def _row_gate(mod, k3, i, tm, n_lat):
    g0 = mod[0, k3:k3 + 1, :]
    if mod.shape[0] == 1:
        return g0
    rid = i * tm + lax.broadcasted_iota(jnp.int32, (tm, 1), 0)
    return jnp.where(rid < n_lat, g0, mod[1, k3:k3 + 1, :])


def _ffn_up(name, u, wg, wu, base, tm):
    r, d = u.shape
    nch, _, _, fc = wg.shape

    def body(u_ref, wg_ref, wu_ref, a_ref, b_ref, h_ref):
        uv = u_ref[...]
        a = _dot(uv, wg_ref[...], NN)
        b = _dot(uv, wu_ref[...], NN)
        a_ref[...] = a.astype(BF)
        b_ref[...] = b.astype(BF)
        h_ref[...] = (_silu(a) * b).astype(BF)

    chunk = pl.BlockSpec((None, tm, fc), lambda j, i: (j, i, 0))
    return _pcall(body, name=name, grid=(nch, r // tm),
                  in_specs=[pl.BlockSpec((tm, d), lambda j, i: (i, 0)),
                            pl.BlockSpec((None, None, d, fc), lambda j, i: (j, base, 0, 0)),
                            pl.BlockSpec((None, None, d, fc), lambda j, i: (j, base, 0, 0))],
                  out_specs=[chunk] * 3, out_shape=[jax.ShapeDtypeStruct((nch, r, fc), BF)] * 3,
                  compiler_params=_params(("arbitrary", "arbitrary")))(u, wg, wu)


def _ffn_down(name, hid, wd, wd_blk, s, mod, k, n_lat, tm):
    nch, r, fc = hid.shape
    d = wd.shape[2]

    def body(h_ref, w_ref, s_ref, m_ref, so_ref, o_ref):
        i = pl.program_id(0)
        o = _dot(h_ref[0], w_ref[0], NN)
        for j in range(1, nch):
            o = o + _dot(h_ref[j], w_ref[j], NN)
        o_ref[...] = o
        so_ref[...] = s_ref[...] + 0.5 * _row_gate(m_ref[...], 3 * k + 2, i, tm, n_lat) * o

    row = pl.BlockSpec((tm, d), lambda i: (i, 0))
    return _pcall(body, name=name, grid=(r // tm,),
                  in_specs=[pl.BlockSpec((nch, tm, fc), lambda i: (0, i, 0)),
                            pl.BlockSpec((nch, fc, d), lambda i: (0, wd_blk, 0)), row,
                            pl.BlockSpec(mod.shape, lambda i: (0, 0, 0))],
                  out_specs=[row, row], out_shape=[jax.ShapeDtypeStruct((r, d), F32)] * 2,
                  compiler_params=_params(("arbitrary",)))(hid, wd, s, mod)


def _ffn_dhid(name, d_o, wd, wd_blk, a, b, tm):
    r, d = d_o.shape
    nch, _, fc = a.shape

    def body(g_ref, w_ref, a_ref, b_ref, da_ref, db_ref):
        dh = _dot(g_ref[...], w_ref[...], NT)
        av, bv = a_ref[...].astype(F32), b_ref[...].astype(F32)
        sg = jax.nn.sigmoid(av)
        da_ref[...] = (dh * bv * (sg * (1.0 + av * (1.0 - sg)))).astype(BF)
        db_ref[...] = (dh * (av * sg)).astype(BF)

    chunk = pl.BlockSpec((None, tm, fc), lambda j, i: (j, i, 0))
    return _pcall(body, name=name, grid=(nch, r // tm),
                  in_specs=[pl.BlockSpec((tm, d), lambda j, i: (i, 0)),
                            pl.BlockSpec((None, fc, d), lambda j, i: (j, wd_blk, 0)), chunk, chunk],
                  out_specs=[chunk] * 2, out_shape=[jax.ShapeDtypeStruct((nch, r, fc), BF)] * 2,
                  compiler_params=_params(("arbitrary", "arbitrary")))(d_o, wd, a, b)


def _ffn_du(name, da, db, wg, wu, base, tm):
    nch, r, fc = da.shape
    d = wg.shape[2]

    def body(da_ref, db_ref, wg_ref, wu_ref, o_ref):
        acc = _dot(da_ref[0], wg_ref[0], NT) + _dot(db_ref[0], wu_ref[0], NT)
        for j in range(1, nch):
            acc = acc + _dot(da_ref[j], wg_ref[j], NT) + _dot(db_ref[j], wu_ref[j], NT)
        o_ref[...] = acc

    chunks = pl.BlockSpec((nch, tm, fc), lambda i: (0, i, 0))
    held = pl.BlockSpec((nch, None, d, fc), lambda i: (0, base, 0, 0), pipeline_mode=pl.Buffered(1))
    return _pcall(body, name=name, grid=(r // tm,), in_specs=[chunks, chunks, held, held],
                  out_specs=pl.BlockSpec((tm, d), lambda i: (i, 0)), out_shape=jax.ShapeDtypeStruct((r, d), F32),
                  compiler_params=_params(("arbitrary",)))(da, db, wg, wu)


def _ffn_dw_in(name, u, da, db, tmm, g_gate, g_up, idx):
    r, d = u.shape
    nch, _, fc = da.shape
    nb = d // tmm

    def body(u_ref, a_ref, b_ref, gg_ref, gu_ref, og_ref, ou_ref):
        ut = u_ref[...].T
        og_ref[...] = _dot(ut, a_ref[...], NN).astype(og_ref.dtype)
        ou_ref[...] = _dot(ut, b_ref[...], NN).astype(ou_ref.dtype)

    chunk = pl.BlockSpec((None, r, fc), lambda j, mi: (j, 0, 0))
    out = pl.BlockSpec((None, tmm, fc), lambda j, mi: (j, idx * nb + mi, 0))
    return _pcall(body, name=name, grid=(nch, nb),
                  in_specs=[pl.BlockSpec((r, tmm), lambda j, mi: (0, mi)), chunk, chunk,
                            pl.BlockSpec(memory_space=pl.ANY), pl.BlockSpec(memory_space=pl.ANY)],
                  out_specs=[out, out],
                  out_shape=[jax.ShapeDtypeStruct(g_gate.shape, g_gate.dtype),
                             jax.ShapeDtypeStruct(g_up.shape, g_up.dtype)],
                  input_output_aliases={3: 0, 4: 1},
                  compiler_params=_params(("arbitrary", "arbitrary")))(u, da, db, g_gate, g_up)


def _ffn_dw_down(name, hid, d_o, tn, grads, idx):
    nch, r, fc = hid.shape
    d = d_o.shape[1]

    def body(h_ref, g_ref, acc_ref, o_ref):
        o_ref[...] = _dot(h_ref[...], g_ref[...], TN).astype(o_ref.dtype)

    return _pcall(body, name=name, grid=(nch, d // tn),
                  in_specs=[pl.BlockSpec((None, r, fc), lambda j, ni: (j, 0, 0)),
                            pl.BlockSpec((r, tn), lambda j, ni: (0, ni)),
                            pl.BlockSpec(memory_space=pl.ANY)],
                  out_specs=pl.BlockSpec((None, fc, tn), lambda j, ni: (j, idx, ni)),
                  out_shape=jax.ShapeDtypeStruct(grads.shape, grads.dtype), input_output_aliases={2: 0},
                  compiler_params=_params(("arbitrary", "arbitrary")))(hid, d_o, grads)


def _partner(x):
    n = x.shape[1]
    lane = lax.broadcasted_iota(jnp.int32, x.shape, 1)
    return jnp.where((lane & 15) < 8, pltpu.roll(x, n - 8, 1), pltpu.roll(x, 8, 1))


def _rope(x, ct, st):
    reps = x.shape[1] // ct.shape[1]
    if reps > 1:
        ct, st = jnp.tile(ct, (1, reps)), jnp.tile(st, (1, reps))
    return x * ct + _partner(x) * st


def _rope_t(dy, ct, st):
    reps = dy.shape[1] // ct.shape[1]
    if reps > 1:
        ct, st = jnp.tile(ct, (1, reps)), jnp.tile(st, (1, reps))
    return dy * ct + _partner(dy * st)


def _rope_tables(t_len, g_len, lane0):
    half = QK_ROPE // 4
    pos = jnp.arange(t_len)
    row = (pos // GRID_W).astype(F32)
    col = (pos % GRID_W).astype(F32)
    freqs = jnp.power(ROPE_THETA, -jnp.arange(0, QK_ROPE // 2, 2, dtype=F32) / (QK_ROPE // 2))
    ang_r, ang_c = row[:, None] * freqs, col[:, None] * freqs
    cs = jnp.concatenate([jnp.cos(ang_r)] * 2 + [jnp.cos(ang_c)] * 2, axis=1)
    sn = jnp.concatenate([-jnp.sin(ang_r), jnp.sin(ang_r), -jnp.sin(ang_c), jnp.sin(ang_c)], axis=1)
    assert cs.shape[1] == 4 * half == QK_ROPE
    def place(tab, fill):
        rest = HEAD_PAD - lane0 - QK_ROPE
        rows = jnp.concatenate([jnp.full((t_len, lane0), fill, F32), tab, jnp.full((t_len, rest), fill, F32)], axis=1)
        return jnp.concatenate([rows, jnp.full((g_len, HEAD_PAD), fill, F32)], axis=0)

    return place(cs, 1.0), place(sn, 0.0)


def _attn_fwd(name, q, kp, vp, n_q, q_off, n_k, k_blk, heads, tq, scale):
    qb = q_off // tq
    per = 4 if heads % 4 == 0 else (2 if heads % 2 == 0 else 1)
    wide = per * HEAD_PAD

    def body(q_ref, k_ref, v_ref, o_ref, l_ref):
        for e in range(per):
            sl = slice(e * HEAD_PAD, (e + 1) * HEAD_PAD)
            s = _dot(q_ref[:, sl], k_ref[:, sl], NT) * scale
            m = jnp.max(s, axis=1, keepdims=True)
            p = jnp.exp(s - m)
            l = jnp.sum(p, axis=1, keepdims=True)
            o_ref[:, sl] = (_dot(p, v_ref[:, sl], NN) / l).astype(BF)
            l_ref[:, sl] = jnp.broadcast_to(m + jnp.log(l), (tq, HEAD_PAD))

    hw = heads * HEAD_PAD
    blk = pl.BlockSpec((tq, wide), lambda h, i: (i, h))
    kv = pl.BlockSpec((n_k, wide), lambda h, i: (k_blk, h))
    return _pcall(body, name=name, grid=(heads // per, n_q // tq),
                  in_specs=[pl.BlockSpec((tq, wide), lambda h, i: (i + qb, h)), kv, kv],
                  out_specs=[blk, blk],
                  out_shape=[jax.ShapeDtypeStruct((n_q, hw), BF), jax.ShapeDtypeStruct((n_q, hw), F32)],
                  compiler_params=_params(("arbitrary", "arbitrary")))(q, kp, vp)


def _attn_bwd(name, q, kp, vp, cat, dcat, lse, n_q, q_off, n_k, k_blk, heads, tq, scale, col_blk, onto=None):
    qb = q_off // tq

    per = 1
    wide = per * HEAD_PAD

    def body(q_ref, k_ref, v_ref, o_ref, do_ref, l_ref, *rest):
        dq_ref, dk_ref, dv_ref = rest[-3:]
        i = pl.program_id(1)
        for e in range(per):
            sl = slice(e * HEAD_PAD, (e + 1) * HEAD_PAD)
            qv, kv_, vv = q_ref[:, sl], k_ref[:, sl], v_ref[:, sl]
            dov = do_ref[:, sl]
            s = _dot(qv, kv_, NT) * scale
            p = jnp.exp(s - l_ref[:, e * HEAD_PAD:e * HEAD_PAD + 1])
            dp = _dot(dov, vv, NT)
            delta = jnp.sum(dov * o_ref[:, sl].astype(F32), axis=1, keepdims=True)
            ds = (p * (dp - delta) * scale).astype(BF)
            dq_ref[:, sl] = _dot(ds, kv_, NN)
            dk = _dot(ds, qv, TN)
            dv = _dot(p, dov, TN)

            @pl.when(i == 0)
            def _():
                if onto is None:
                    dk_ref[:, sl] = dk
                    dv_ref[:, sl] = dv
                else:
                    dk_ref[:, sl] = rest[0][:, sl] + dk
                    dv_ref[:, sl] = rest[1][:, sl] + dv

            @pl.when(i > 0)
            def _():
                dk_ref[:, sl] += dk
                dv_ref[:, sl] += dv

    hw = heads * HEAD_PAD
    heads = heads // per
    col_blk = col_blk // per
    qspec = pl.BlockSpec((tq, wide), lambda h, i: (i + qb, h))
    cspec = pl.BlockSpec((tq, wide), lambda h, i: (i + qb, col_blk + h))
    kv = pl.BlockSpec((n_k, wide), lambda h, i: (k_blk, h))
    blk = pl.BlockSpec((tq, wide), lambda h, i: (i, h))
    if onto is None:
        acc = pl.BlockSpec((n_k, wide), lambda h, i: (0, h))
        return _pcall(body, name=name, grid=(heads, n_q // tq),
                      in_specs=[qspec, kv, kv, cspec, cspec, blk], out_specs=[blk, acc, acc],
                      out_shape=[jax.ShapeDtypeStruct((n_q, hw), F32), jax.ShapeDtypeStruct((n_k, hw), F32),
                                 jax.ShapeDtypeStruct((n_k, hw), F32)],
                      compiler_params=_params(("arbitrary", "arbitrary")))(q, kp, vp, cat, dcat, lse)
    return _pcall(body, name=name, grid=(heads, n_q // tq),
                  in_specs=[qspec, kv, kv, cspec, cspec, blk, kv, kv], out_specs=[blk, kv, kv],
                  out_shape=[jax.ShapeDtypeStruct((n_q, hw), F32)] + [jax.ShapeDtypeStruct(t.shape, F32) for t in onto],
                  input_output_aliases={6: 1, 7: 2},
                  compiler_params=_params(("arbitrary", "arbitrary")))(q, kp, vp, cat, dcat, lse, *onto)


def _shift(x, k):
    return pltpu.roll(x, k % x.shape[0], 0)


def _window_sum(v, w, mirrored):
    n, gd = v.shape
    pad = jnp.zeros((POOL_PAD, gd), F32)
    e = jnp.concatenate([pad, v, pad], axis=0)
    acc = e + _shift(e, -1 if mirrored else 1)
    step = 1
    while 2 * step < w:
        acc = _shift(acc, step) + _shift(acc, -step)
        step *= 2
    return acc[POOL_PAD:POOL_PAD + n]


def _window_count(n, w):
    t = lax.broadcasted_iota(jnp.int32, (n, 1), 0)
    lo = jnp.maximum(t - w // 2, 0)
    hi = jnp.minimum(t + (w - w // 2 - 1), n - 1)
    return (hi - lo + 1).astype(F32)


def _pool_fwd(name, u, pool_w, scale):
    n, pd = u.shape
    ng = len(POOL_WINDOWS)
    gd = pd // ng

    def body(u_ref, w_ref, s_ref, y_ref):
        for g, w in enumerate(POOL_WINDOWS):
            sl = slice(g * gd, (g + 1) * gd)
            ug = u_ref[:, sl]
            p = _window_sum(ug, w, False) / _window_count(n, w) - ug
            y_ref[:, sl] = (_dot(p, w_ref[g], NN) * s_ref[:, sl]).astype(BF)

    return _pcall(body, name=name, out_shape=jax.ShapeDtypeStruct((n, pd), BF),
                  compiler_params=_params())(u, pool_w, scale)


def _pool_bwd(name, u, dcat, pool_w, scale, row_off):
    n, pd = u.shape
    ng = len(POOL_WINDOWS)
    gd = pd // ng

    def body(u_ref, dy_ref, w_ref, s_ref, du_ref, dw_ref, ds_ref):
        ds_ref[...] = jnp.zeros_like(ds_ref)
        for g, w in enumerate(POOL_WINDOWS):
            sl = slice(g * gd, (g + 1) * gd)
            ug, dy, wg = u_ref[:, sl], dy_ref[:, sl], w_ref[g]
            cnt = _window_count(n, w)
            p = _window_sum(ug, w, False) / cnt - ug
            ds_ref[0:1, sl] = jnp.sum(dy * _dot(p, wg, NN), axis=0, keepdims=True)
            dys = dy * s_ref[:, sl]
            dw_ref[g] = _dot(p, dys, TN)
            dp = _dot(dys, wg, NT)
            du_ref[:, sl] = (_window_sum(dp / cnt, w, True) - dp).astype(BF)

    rb = row_off // n
    return _pcall(body, name=name, grid=(1,),
                  in_specs=[pl.BlockSpec((n, pd), lambda i: (0, 0)), pl.BlockSpec((n, pd), lambda i: (rb, 0)),
                            pl.BlockSpec(pool_w.shape, lambda i: (0, 0, 0)), pl.BlockSpec(scale.shape, lambda i: (0, 0))],
                  out_specs=[pl.BlockSpec((n, pd), lambda i: (0, 0)), pl.BlockSpec((ng, gd, gd), lambda i: (0, 0, 0)),
                             pl.BlockSpec((8, pd), lambda i: (0, 0))],
                  out_shape=[jax.ShapeDtypeStruct((n, pd), BF), jax.ShapeDtypeStruct((ng, gd, gd), F32),
                             jax.ShapeDtypeStruct((8, pd), F32)],
                  compiler_params=_params(("arbitrary",)))(u, dcat, pool_w, scale)


def _edge_shift(z, k):
    n = z.shape[0]
    t = lax.broadcasted_iota(jnp.int32, (n, 1), 0)
    keep = (t >= k) if k > 0 else (t < n + k)
    return jnp.where(keep, pltpu.roll(z, k % n, 0), 0.0)


def _conv_fwd(name, p3, cw, tc):
    n, cd = p3.shape[0], p3.shape[1] // 3
    nb = cd // tc

    def body(b_ref, c_ref, v_ref, w_ref, y_ref):
        z = c_ref[...] * v_ref[...]
        w = w_ref[...]
        zc = w[0:1] * _edge_shift(z, 1) + w[1:2] * z + w[2:3] * _edge_shift(z, -1)
        y_ref[...] = (b_ref[...] * zc).astype(BF)

    return _pcall(body, name=name, grid=(nb,),
                  in_specs=[pl.BlockSpec((n, tc), lambda j: (0, j)), pl.BlockSpec((n, tc), lambda j: (0, nb + j)),
                            pl.BlockSpec((n, tc), lambda j: (0, 2 * nb + j)), pl.BlockSpec((3, tc), lambda j: (0, j))],
                  out_specs=pl.BlockSpec((n, tc), lambda j: (0, j)), out_shape=jax.ShapeDtypeStruct((n, cd), BF),
                  compiler_params=_params(("arbitrary",)))(p3, p3, p3, cw)


def _conv_bwd(name, p3, cw, dy, tc):
    n, cd = dy.shape
    nb = cd // tc

    def body(b_ref, c_ref, v_ref, w_ref, dy_ref, dp_ref, dw_ref):
        cv, vv, w, dyv = c_ref[...], v_ref[...], w_ref[...], dy_ref[...]
        z = cv * vv
        zl, zr = _edge_shift(z, 1), _edge_shift(z, -1)
        zc = w[0:1] * zl + w[1:2] * z + w[2:3] * zr
        dzc = dyv * b_ref[...]
        dz = w[0:1] * _edge_shift(dzc, -1) + w[1:2] * dzc + w[2:3] * _edge_shift(dzc, 1)
        dp_ref[0] = (dyv * zc).astype(BF)
        dp_ref[1] = (dz * vv).astype(BF)
        dp_ref[2] = (dz * cv).astype(BF)
        dw_ref[...] = jnp.zeros_like(dw_ref)
        dw_ref[0:1, :] = jnp.sum(dzc * zl, axis=0, keepdims=True)
        dw_ref[1:2, :] = jnp.sum(dzc * z, axis=0, keepdims=True)
        dw_ref[2:3, :] = jnp.sum(dzc * zr, axis=0, keepdims=True)

    col = pl.BlockSpec((n, tc), lambda j: (0, j))
    return _pcall(body, name=name, grid=(nb,),
                  in_specs=[col, pl.BlockSpec((n, tc), lambda j: (0, nb + j)),
                            pl.BlockSpec((n, tc), lambda j: (0, 2 * nb + j)), pl.BlockSpec((3, tc), lambda j: (0, j)), col],
                  out_specs=[pl.BlockSpec((3, n, tc), lambda j: (0, 0, j)), pl.BlockSpec((8, tc), lambda j: (0, j))],
                  out_shape=[jax.ShapeDtypeStruct((3, n, cd), BF), jax.ShapeDtypeStruct((8, cd), F32)],
                  compiler_params=_params(("arbitrary",)))(p3, p3, p3, cw, dy)


def _conv_din(name, dp3, w_in, tm):
    _, n, cd = dp3.shape
    d = w_in.shape[0]

    def body(a_ref, w_ref, o_ref, acc_ref):
        j = pl.program_id(1)
        part = _dot(a_ref[...], w_ref[...], NT)

        @pl.when(j == 0)
        def _():
            acc_ref[...] = part

        @pl.when(j > 0)
        def _():
            acc_ref[...] += part

        @pl.when(j == 2)
        def _():
            o_ref[...] = acc_ref[...]

    return _pcall(body, name=name, grid=(n // tm, 3),
                  in_specs=[pl.BlockSpec((None, tm, cd), lambda i, j: (j, i, 0)),
                            pl.BlockSpec((d, cd), lambda i, j: (0, j))],
                  out_specs=pl.BlockSpec((tm, d), lambda i, j: (i, 0)), out_shape=jax.ShapeDtypeStruct((n, d), F32),
                  scratch_shapes=[pltpu.VMEM((tm, d), F32)],
                  compiler_params=_params(("arbitrary", "arbitrary")))(dp3, w_in)


def _conv_dw_in(name, u, dp3, tmm, tn):
    n, d = u.shape
    cd = dp3.shape[2]
    nb = cd // tn

    def body(u_ref, z_ref, o_ref):
        o_ref[...] = _dot(u_ref[...], z_ref[...], TN)

    return _pcall(body, name=name, grid=(3 * nb, d // tmm),
                  in_specs=[pl.BlockSpec((n, tmm), lambda j, mi: (0, mi)),
                            pl.BlockSpec((None, n, tn), lambda j, mi: (j // nb, 0, j % nb))],
                  out_specs=pl.BlockSpec((tmm, tn), lambda j, mi: (mi, j)),
                  out_shape=jax.ShapeDtypeStruct((d, 3 * cd), F32),
                  compiler_params=_params(("arbitrary", "arbitrary")))(u, dp3)


def _loss_head(name, h, target, gain, tm):
    d = h.shape[1]

    def fn(rows, consts, m):
        hv, tv = rows
        g = consts[0][0:1, :]
        n, r = _rms(hv)
        err = n * g - tv
        dy = err / d
        loss = 0.5 * jnp.sum(err * err) / d
        acc = {0: jnp.sum(dy * n, axis=0, keepdims=True), 1: jnp.full((1, d), loss, F32)}
        return [_rms_bwd(dy * g, n, r)], acc

    return _rows_call(name, fn, h.shape[0], tm, [h, target], [gain], None, [(d, F32)], acc_w=d)


def _adamw(name, w, g, m, v, after=None):
    shape = w.shape
    if w.ndim == 1:
        shape2 = (1,) + shape
        res = _adamw(name, *[t.reshape(shape2) for t in (w, g, m, v)], after=after)
        return [t.reshape(shape) for t in res]
    if shape[-1] % 128 and shape[-2] % 128 == 0:
        res = _adamw(name, *[jnp.swapaxes(t, -1, -2) for t in (w, g, m, v)], after=after)
        return [jnp.swapaxes(t, -1, -2) for t in res]
    lead, (r, cdim) = shape[:-2], shape[-2:]
    tr = r
    if r * cdim * 4 > (3 << 19):
        tr = _pick(r, max(8, (3 << 19) // (cdim * 4)), 8)
    c1 = 1.0 / (1.0 - ADAM_B1 ** ADAM_STEP)
    c2 = 1.0 / (1.0 - ADAM_B2 ** ADAM_STEP)
    nl = len(lead)

    def body(w_ref, g_ref, m_ref, v_ref, *rest):
        d_ref, nm_ref, nv_ref = rest[-3:]
        gv = g_ref[...]
        nm = ADAM_B1 * m_ref[...] + (1.0 - ADAM_B1) * gv
        nv = ADAM_B2 * v_ref[...] + (1.0 - ADAM_B2) * (gv * gv)
        nm_ref[...] = nm
        nv_ref[...] = nv
        d_ref[...] = -ADAM_LR * ((nm * c1) / (jnp.sqrt(nv * c2) + ADAM_EPS) + ADAM_WD * w_ref[...])

    spec = pl.BlockSpec((None,) * nl + (tr, cdim), lambda *idx: idx + (0,))
    extra = [] if after is None else [after]
    res = _pcall(body, name=name, grid=lead + (r // tr,),
                 in_specs=[spec] * 4 + [pl.BlockSpec(memory_space=pl.ANY)] * len(extra), out_specs=[spec] * 3,
                 out_shape=[jax.ShapeDtypeStruct(shape, F32)] * 3,
                 compiler_params=_params(("arbitrary",) * (nl + 1)))(w, g, m, v, *extra)
    return list(res)


def _adamw_piece(name, w, g_piece, m, v, at, outs, after=None):
    shape = w.shape
    nl = len(at)
    r, cdim = shape[-2:]
    assert shape[nl:] == g_piece.shape and len(shape) == nl + 2
    tr = _pick(r, max(8, (3 << 19) // (cdim * 4)), 8) if r * cdim * 4 > (3 << 19) else r
    c1 = 1.0 / (1.0 - ADAM_B1 ** ADAM_STEP)
    c2 = 1.0 / (1.0 - ADAM_B2 ** ADAM_STEP)
    if outs is None:
        outs = [lax.empty(shape, F32) for _ in range(4)]

    def body(w_ref, g_ref, m_ref, v_ref, *rest):
        go_ref, d_ref, nm_ref, nv_ref = rest[-4:]
        gv = g_ref[...]
        nm = ADAM_B1 * m_ref[...] + (1.0 - ADAM_B1) * gv
        nv = ADAM_B2 * v_ref[...] + (1.0 - ADAM_B2) * (gv * gv)
        go_ref[...] = gv
        nm_ref[...] = nm
        nv_ref[...] = nv
        d_ref[...] = -ADAM_LR * ((nm * c1) / (jnp.sqrt(nv * c2) + ADAM_EPS) + ADAM_WD * w_ref[...])

    full = pl.BlockSpec((None,) * nl + (tr, cdim), lambda i: tuple(at) + (i, 0))
    extra = [] if after is None else [after]
    res = _pcall(body, name=name, grid=(r // tr,),
                 in_specs=[full, pl.BlockSpec((tr, cdim), lambda i: (i, 0)), full, full]
                 + [pl.BlockSpec(memory_space=pl.ANY)] * (4 + len(extra)),
                 out_specs=[full] * 4, out_shape=[jax.ShapeDtypeStruct(shape, F32)] * 4,
                 input_output_aliases={4 + j: j for j in range(4)},
                 compiler_params=_params(("arbitrary",)))(w, g_piece, m, v, *outs, *extra)
    return list(res)


def _ffn_half_fwd(tag, s, gains, mod, k, wts, n_lat, tm, h_tiles, tm_big, after=None):
    wg, wu, wd, idx = wts
    u = _adaln_fwd(f"adaln_{tag}", s, gains, k, mod, k, tm, h_tiles, after)
    a, b, hid = _ffn_up(f"ffn_up_{tag}", u, wg, wu, idx, _pick(u.shape[0], 1152, 128))
    s_out, o = _ffn_down(f"ffn_down_{tag}", hid, wd, idx, s, mod, k, n_lat, tm_big)
    return s_out, (s, u, a, b, hid, o)


def _ffn_half_bwd(tag, ds_out, saved, gains, mod, k, wts, big_grads, tm, h_tiles, tm_big, after=None):
    wg, wu, wd, idx = wts
    g_gate, g_up, g_down = big_grads
    s, u, a, b, hid, o = saved
    d_o, acc_g = _resid_bwd(f"resid_bwd_{tag}", ds_out, o, mod, k, 0.5, tm, h_tiles, after)
    da, db = _ffn_dhid(f"ffn_dhid_{tag}", d_o, wd, idx, a, b, _pick(d_o.shape[0], 1152, 128))
    du = _ffn_du(f"ffn_du_{tag}", da, db, wg, wu, idx, tm_big)
    d = u.shape[1]
    g_gate, g_up = _ffn_dw_in(f"ffn_dwgu_{tag}", u, da, db, _pick(d, MM_ROWS), g_gate, g_up, idx)
    g_down = _ffn_dw_down(f"ffn_dwd_{tag}", hid, d_o, _pick(d, 512), g_down, idx)
    ds, acc_n = _adaln_bwd(f"adaln_bwd_{tag}", s, du, ds_out, gains, k, mod, k, tm, h_tiles)
    return ds, (g_gate, g_up, g_down), (acc_n[:, 0], acc_n[:, 1], acc_g[:, 0]), jnp.sum(acc_n[:, 2], axis=0)


def kernel(x, c, ctx, c_ctx, norm_g, w_mod, b_mod, ffn_w_gate, ffn_w_up, ffn_w_down, ab_w_in, pool_w, pool_scale, q_norm_g, w_uq, kv_norm_g, w_ukv, ab_w_out, conv_w_in, conv_w, conv_w_out, final_norm_g, loss_target, m_c_ctx, m_norm_g, m_w_mod, m_b_mod, m_ffn_w_gate, m_ffn_w_up, m_ffn_w_down, m_ab_w_in, m_pool_w, m_pool_scale, m_q_norm_g, m_w_uq, m_kv_norm_g, m_w_ukv, m_ab_w_out, m_conv_w_in, m_conv_w, m_conv_w_out, m_final_norm_g, v_c_ctx, v_norm_g, v_w_mod, v_b_mod, v_ffn_w_gate, v_ffn_w_up, v_ffn_w_down, v_ab_w_in, v_pool_w, v_pool_scale, v_q_norm_g, v_w_uq, v_kv_norm_g, v_w_ukv, v_ab_w_out, v_conv_w_in, v_conv_w, v_conv_w_out, v_final_norm_g):
    weights = dict(c_ctx=c_ctx, norm_g=norm_g, w_mod=w_mod, b_mod=b_mod, ffn_w_gate=ffn_w_gate, ffn_w_up=ffn_w_up,
                   ffn_w_down=ffn_w_down, ab_w_in=ab_w_in, pool_w=pool_w, pool_scale=pool_scale, q_norm_g=q_norm_g,
                   w_uq=w_uq, kv_norm_g=kv_norm_g, w_ukv=w_ukv, ab_w_out=ab_w_out, conv_w_in=conv_w_in, conv_w=conv_w,
                   conv_w_out=conv_w_out, final_norm_g=final_norm_g)
    mom_m = dict(c_ctx=m_c_ctx, norm_g=m_norm_g, w_mod=m_w_mod, b_mod=m_b_mod, ffn_w_gate=m_ffn_w_gate,
                 ffn_w_up=m_ffn_w_up, ffn_w_down=m_ffn_w_down, ab_w_in=m_ab_w_in, pool_w=m_pool_w,
                 pool_scale=m_pool_scale, q_norm_g=m_q_norm_g, w_uq=m_w_uq, kv_norm_g=m_kv_norm_g, w_ukv=m_w_ukv,
                 ab_w_out=m_ab_w_out, conv_w_in=m_conv_w_in, conv_w=m_conv_w, conv_w_out=m_conv_w_out,
                 final_norm_g=m_final_norm_g)
    mom_v = dict(c_ctx=v_c_ctx, norm_g=v_norm_g, w_mod=v_w_mod, b_mod=v_b_mod, ffn_w_gate=v_ffn_w_gate,
                 ffn_w_up=v_ffn_w_up, ffn_w_down=v_ffn_w_down, ab_w_in=v_ab_w_in, pool_w=v_pool_w,
                 pool_scale=v_pool_scale, q_norm_g=v_q_norm_g, w_uq=v_w_uq, kv_norm_g=v_kv_norm_g, w_ukv=v_w_ukv,
                 ab_w_out=v_ab_w_out, conv_w_in=v_conv_w_in, conv_w=v_conv_w, conv_w_out=v_conv_w_out,
                 final_norm_g=v_final_norm_g)

    t_len, d = x.shape[1], x.shape[2]
    g_len = ctx.shape[1]
    r_len = t_len + g_len
    fc = ffn_w_gate.shape[3]
    heads = d // 128
    pool_dim = d // 2
    q_rank, kv_rank = q_norm_g.shape[1], kv_norm_g.shape[1]
    hw = heads * HEAD_PAD
    attn_scale = 1.0 / math.sqrt(QK_NOPE + QK_ROPE)
    kvr_w = kv_rank + HEAD_PAD
    tm = 256 if g_len % 256 == 0 else g_len
    assert t_len % tm == 0 and g_len % tm == 0 and t_len % g_len == 0 and pool_dim % 128 == 0
    h_tiles = t_len // tm
    tm_l0 = _pick(r_len, 768, tm)
    tm_l1 = _pick(t_len, 1024, tm)

    xi, yi, ci = lax.axis_index("x"), lax.axis_index("y"), lax.axis_index("c")
    me = 4 * xi + 2 * yi + ci
    shard = 2 * xi + yi

    def halves(w):
        return w.astype(BF).reshape(2, -1, w.shape[-1])

    ffn_names = ["ffn_w_gate", "ffn_w_up", "ffn_w_down"]
    mixer_names = [["ab_w_in", "w_uq", "w_ukv", "ab_w_out"], ["conv_w_in", "conv_w_out"]]
    big_names = ffn_names + mixer_names[0] + mixer_names[1]

    def stage_names(k):
        return mixer_names[k // 3] if k % 3 == 1 else ffn_names

    def stage_halves(k):
        l, f = k // 3, (k % 3) // 2
        if k % 3 == 1:
            return [halves(weights[nm]) for nm in mixer_names[l]]
        return [halves(weights[nm][l, f]) for nm in ffn_names]

    def gather_start(k, dep):
        own = lax.optimization_barrier((tuple(stage_halves(k)), dep))[0]
        n = len(own)
        return _split_start(f"gather_start_s{k}", list(own) + _landing(N_DEV, own), n * len(CHIP_FLIPS),
                            _chips_gather_build(n))

    gather0 = gather_start(0, c)

    small = jnp.concatenate([norm_g.reshape(6, -1), conv_w[0]], axis=0)
    small = jnp.pad(small, ((0, 7), (0, 0)))
    c_row = jnp.pad(c, ((0, 7), (0, 0))) + gather0['token'][0, 0]
    small_all, c_all = _gather_all("gather_small", [small, c_row])
    small_full = small_all[::2].transpose(1, 0, 2).reshape(16, d)
    gains = [jnp.pad(small_full[3 * l:3 * l + 3], ((0, 5), (0, 0))) for l in range(2)]
    conv_w_full = small_full[6:9]
    c16 = jnp.concatenate([c_all[:, 0], c_ctx[None], jnp.zeros((7, d), F32)], axis=0)

    n_col = w_mod.shape[2]
    b_sh = lax.dynamic_slice_in_dim(b_mod, shard * n_col, n_col, axis=1)
    m_sh = [_mm(f"mod_fwd_{l}", c16, w_mod, 'nn', tm=16, tn=768, a_pre=_silu, b_lead=l,
                epi=lambda acc, i, bv: (acc + bv,), epi_args=(b_sh[l:l + 1],), epi_kinds=('n',)) for l in range(2)]
    m_all = _gather_all("gather_mod", [jnp.concatenate(m_sh, axis=0)], two_level=True)[0]
    m_full = m_all[::2].reshape(N_SHARD, 2, 16, n_col).transpose(1, 2, 0, 3).reshape(2, 16, N_MOD * d)
    mod_h = [jnp.pad(lax.dynamic_index_in_dim(m_full[l], me, 0, keepdims=False).reshape(N_MOD, d), ((0, 7), (0, 0)))
             for l in range(2)]
    mod_g0 = jnp.pad(m_full[0, 8].reshape(N_MOD, d), ((0, 7), (0, 0)))
    mods = [jnp.stack([mod_h[0], mod_g0]), mod_h[1][None]]

    def stage_weights(k, handle, after):
        bufs = _split_wait(f"gather_wait_s{k}", handle, after)
        n = len(bufs) // 2
        own = bufs[:n]
        fwd = _split_start(f"forward_start_s{k}", bufs[n:], n * len(CHIP_FLIPS), _sibling_forward_build(n))
        nxt = gather_start(k + 1, fwd['token']) if k + 1 < 6 else None
        landed = _split_wait(f"forward_wait_s{k}", fwd, fwd['token'])
        full = [lax.dynamic_update_slice_in_dim(z, a, 2 * shard, 0) for z, a in zip(landed, own)]
        gw = {nm: g.reshape(N_SHARD, 2 * g.shape[1], g.shape[2]) for nm, g in zip(stage_names(k), full)}
        return gw, nxt, (fwd['token'] if nxt is None else nxt['token'])

    def ffn_weights(gw):
        return gw["ffn_w_gate"].reshape(N_SHARD, 1, d, fc), gw["ffn_w_up"].reshape(N_SHARD, 1, d, fc), \
            gw["ffn_w_down"], 0

    gw_s0, gather1, tok0 = stage_weights(0, gather0, m_all)
    ffn_w = [[ffn_weights(gw_s0), None], [None, None]]

    s0 = jnp.concatenate([x[0], ctx[0]], axis=0)
    s1, sav_f00 = _ffn_half_fwd("l0a", s0, gains[0], mods[0], 0, ffn_w[0][0], t_len, tm, h_tiles, tm_l0, tok0)

    gw_s1, gather2, tok1 = stage_weights(1, gather1, s1)
    w_out_full = gw_s1["ab_w_out"].reshape(-1, d)
    w_uq_full = gw_s1["w_uq"].reshape(q_rank, heads * (QK_NOPE + QK_ROPE))
    w_ukv_full = gw_s1["w_ukv"].transpose(1, 0, 2).reshape(kv_rank, heads * (QK_NOPE + V_HEAD))
    w_in_full = gw_s1["ab_w_in"].transpose(1, 0, 2).reshape(d, -1)

    wq_p = jnp.pad(w_uq_full.reshape(q_rank, heads, QK_NOPE + QK_ROPE),
                   ((0, 0), (0, 0), (0, HEAD_PAD - QK_NOPE - QK_ROPE))).reshape(q_rank, hw)
    ukv3 = w_ukv_full.reshape(kv_rank, heads, QK_NOPE + V_HEAD)
    wk_top = jnp.pad(ukv3[..., :QK_NOPE], ((0, 0), (0, 0), (0, HEAD_PAD - QK_NOPE))).reshape(kv_rank, hw)
    wv_top = jnp.pad(ukv3[..., QK_NOPE:], ((0, 0), (0, 0), (0, HEAD_PAD - V_HEAD))).reshape(kv_rank, hw)
    src_row = lax.broadcasted_iota(jnp.int32, (HEAD_PAD, hw), 0)
    dst_lane = lax.broadcasted_iota(jnp.int32, (HEAD_PAD, hw), 1) % HEAD_PAD
    spread = ((src_row < QK_ROPE) & (dst_lane == src_row + QK_NOPE)).astype(BF)
    wk_ext = jnp.concatenate([wk_top, spread], axis=0)
    wv_ext = jnp.concatenate([wv_top, jnp.zeros((HEAD_PAD, hw), BF)], axis=0)
    w_in_pool = w_in_full[:, :pool_dim]
    w_in_q = w_in_full[:, pool_dim:pool_dim + q_rank]
    w_in_kvr = jnp.pad(w_in_full[:, pool_dim + q_rank:], ((0, 0), (0, HEAD_PAD - QK_ROPE)))
    w_out_attn = jnp.pad(w_out_full[pool_dim:].reshape(heads, V_HEAD, d),
                         ((0, 0), (0, HEAD_PAD - V_HEAD), (0, 0))).reshape(hw, d)
    w_out_p = jnp.concatenate([w_out_full[:pool_dim], w_out_attn], axis=0)

    u_mix = _adaln_fwd("adaln_l0m", s1, gains[0], 1, mods[0], 1, tm, h_tiles, tok1)
    p_pool = _mm("in_pool", u_mix, w_in_pool, 'nn', tm=tm_l0, tn=pool_dim)
    p_q = _mm("in_q", u_mix, w_in_q, 'nn', tm=tm_l0, tn=q_rank)
    p_kvr = _mm("in_kvr", u_mix, w_in_kvr, 'nn', tm=tm_l0, tn=kvr_w)
    qg = jnp.pad(q_norm_g, ((0, 7), (0, 0)))
    kvg = jnp.pad(kv_norm_g, ((0, 7), (0, 0)))
    tq_c, tq_s = _rope_tables(t_len, g_len, QK_NOPE)
    tk_c, tk_s = _rope_tables(t_len, g_len, 0)

    def qn_fn(rows, consts, m):
        n, _ = _rms(rows[0])
        return [n * consts[0][0:1, :]], {}

    qn = _rows_call("q_norm", qn_fn, r_len, tm, [p_q], [qg], None, [(q_rank, BF)])[0]
    q_r = _mm("q_up", qn, wq_p, 'nn', tm=tm_l0, tn=hw, out_dtypes=(BF,),
              epi=lambda acc, i, ct, st: (_rope(acc, ct, st),), epi_args=(tq_c, tq_s), epi_kinds=('mt', 'mt'))

    def kvn_fn(rows, consts, m):
        pv, ct, st = rows
        n, _ = _rms(pv[:, :kv_rank])
        return [jnp.concatenate([n * consts[0][0:1, :], _rope(pv[:, kv_rank:], ct, st)], axis=1)], {}

    kvn = _rows_call("kv_norm", kvn_fn, r_len, tm, [p_kvr, tk_c, tk_s], [kvg], None, [(kvr_w, BF)])[0]
    k_p = _mm("k_up", kvn, wk_ext, 'nn', tm=tm_l0, tn=hw, out_dtypes=(BF,))
    v_p = _mm("v_up", kvn, wv_ext, 'nn', tm=tm_l0, tn=hw, out_dtypes=(BF,))
    tq_h = _pick(t_len, 512, tm)
    o_h, lse_h = _attn_fwd("attn_h", q_r, k_p, v_p, t_len, 0, r_len, 0, heads, tm, attn_scale)
    o_g, lse_g = _attn_fwd("attn_g", q_r, k_p, v_p, g_len, t_len, g_len, t_len // g_len, heads, tm, attn_scale)
    y_h = _pool_fwd("pool_h", p_pool[:t_len], pool_w[0], pool_scale)
    y_g = _pool_fwd("pool_g", p_pool[t_len:], pool_w[0], pool_scale)
    cat = jnp.concatenate([jnp.concatenate([y_h, y_g], axis=0), jnp.concatenate([o_h, o_g], axis=0)], axis=1)

    def resid_epi(k3, n_lat, tmr):
        def epi(acc, i, sv, mv):
            return sv + _row_gate(mv, k3, i, tmr, n_lat) * acc, acc
        return epi

    s2, o_mix0 = _mm("mix_out_l0", cat, w_out_p, 'nn', tm=tm_l0, tn=d, out_dtypes=(F32, F32),
                     epi=resid_epi(5, t_len, tm_l0), epi_args=(s1, mods[0]), epi_kinds=('mn', 'w'))
    gw_s2, gather3, tok2 = stage_weights(2, gather2, s2)
    ffn_w[0][1] = ffn_weights(gw_s2)
    s3, sav_f01 = _ffn_half_fwd("l0b", s2, gains[0], mods[0], 2, ffn_w[0][1], t_len, tm, h_tiles, tm_l0, tok2)

    gw_s3, gather4, tok3 = stage_weights(3, gather3, s3)
    ffn_w[1][0] = ffn_weights(gw_s3)
    tml = 256 if t_len % 256 == 0 else tm
    h3 = s3[:t_len]
    h4, sav_f10 = _ffn_half_fwd("l1a", h3, gains[1], mods[1], 0, ffn_w[1][0], t_len, tml, None, tm_l1, tok3)
    gw_s4, gather5, tok4 = stage_weights(4, gather4, h4)
    cw_out_full = gw_s4["conv_w_out"].reshape(-1, d)
    cw_in_full = gw_s4["conv_w_in"].transpose(1, 0, 2).reshape(d, -1)
    u_cv = _adaln_fwd("adaln_l1m", h4, gains[1], 1, mods[1], 1, tml, None, tok4)
    p3 = _mm("conv_in", u_cv, cw_in_full, 'nn', tm=tm_l1, tn=512)
    cwp = conv_w_full
    tc = _pick(d, 256)
    y_cv = _conv_fwd("conv_fwd", p3, cwp, tc)
    h5, o_mix1 = _mm("mix_out_l1", y_cv, cw_out_full, 'nn', tm=tm_l1, tn=d, out_dtypes=(F32, F32),
                     epi=resid_epi(5, t_len, tm_l1), epi_args=(h4, mods[1]), epi_kinds=('mn', 'w'))
    gw_s5, _, tok5 = stage_weights(5, gather5, h5)
    ffn_w[1][1] = ffn_weights(gw_s5)
    h6, sav_f11 = _ffn_half_fwd("l1b", h5, gains[1], mods[1], 2, ffn_w[1][1], t_len, tml, None, tm_l1, tok5)

    fg = jnp.pad(final_norm_g[None], ((0, 7), (0, 0)))
    dh6, acc_loss = _loss_head("loss_head", h6, loss_target[0], fg, tml)
    d_final_g = acc_loss[0, 0]

    dgain = [[None] * 3 for _ in range(2)]
    dmod = [[None] * N_MOD for _ in range(2)]

    def put(l, k, triple):
        dmod[l][3 * k], dmod[l][3 * k + 1], dmod[l][3 * k + 2] = triple

    def empty_ffn_grads():
        return (lax.empty((N_SHARD, d, fc), BF), lax.empty((N_SHARD, d, fc), BF), lax.empty((N_SHARD, fc, d), BF))

    def by_shard_rows(g):
        return g.reshape(N_SHARD, -1, g.shape[-1])

    def by_shard_cols(g):
        return g.reshape(g.shape[0], N_SHARD, -1).transpose(1, 0, 2)

    def pair_start(k, big):
        send = [b.astype(BF).reshape(N_DEV, b.shape[1] // 2, b.shape[2]) for b in big]
        n = len(send)
        return _split_start(f"grads_pair_start_s{k}", send + _landing(N_SHARD, send), n * N_SHARD,
                            _sibling_halves_build(n))

    def chips_start(k, handle, after):
        bufs = _split_wait(f"grads_pair_wait_s{k}", handle, after)
        n = len(bufs) // 2
        pre = [_add_halves(f"grads_add_s{k}_{nm}", s, z) for nm, s, z in zip(stage_names(k), bufs[:n], bufs[n:])]
        return _split_start(f"grads_start_s{k}", pre + _landing(N_SHARD, pre), n * len(CHIP_FLIPS),
                            _chips_scatter_build(n))

    def landed_sums(k, handle, after):
        bufs = _split_wait(f"grads_wait_s{k}", handle, after)
        n = len(bufs) // 2
        landed = [lax.dynamic_update_slice_in_dim(z, lax.dynamic_slice_in_dim(p, shard, 1, 0), shard, 0)
                  for p, z in zip(bufs[:n], bufs[n:])]
        return [_sum_lead(f"sum_grads_s{k}_{nm}", z) for nm, z in zip(stage_names(k), landed)]

    pair, scatter, sums = [None] * 6, [None] * 6, [None] * 6
    grads, upd = {}, {}
    ffn_out = {nm: None for nm in ffn_names}

    def lanes_last(nm, t):
        shp = weights[nm].shape
        return jnp.swapaxes(t, -1, -2) if shp[-1] % 128 and shp[-2] % 128 == 0 else t

    def finish_stage(k, halves):
        n = len(halves)
        lands = [pltpu.with_memory_space_constraint(lax.empty(h.shape, h.dtype), pltpu.HBM) for h in halves]
        swap = _split_start(f"swap_start_s{k}", list(halves) + lands, n, _sibling_whole_build(n))
        both = _split_wait(f"swap_wait_s{k}", swap, swap['token'])
        for nm, a, g in zip(stage_names(k), both[:n], both[n:]):
            piece = jnp.where(ci == 0, jnp.concatenate([a, g], axis=0), jnp.concatenate([g, a], axis=0))
            if k % 3 == 1:
                grads[nm] = piece.reshape(weights[nm].shape)
                upd[nm] = _adamw(f"adamw_{nm}", weights[nm], grads[nm], mom_m[nm], mom_v[nm])
            else:
                ffn_out[nm] = _adamw_piece(f"adamw_{nm}_s{k}", lanes_last(nm, weights[nm]), lanes_last(nm, piece),
                                           lanes_last(nm, mom_m[nm]), lanes_last(nm, mom_v[nm]),
                                           (k // 3, (k % 3) // 2), ffn_out[nm])
    dh5, ffn_g, tr, dgain[1][2] = _ffn_half_bwd("l1b", dh6, sav_f11, gains[1], mods[1], 2, ffn_w[1][1],
                                                empty_ffn_grads(), tml, None, tm_l1)
    put(1, 2, tr)
    pair[5] = pair_start(5, list(ffn_g))
    d_o1, acc_g1 = _resid_bwd("resid_bwd_l1m", dh5, o_mix1, mods[1], 1, 1.0, tml, None, pair[5]['token'])
    dy_cv = _mm("mix_out_l1_dx", d_o1, cw_out_full, 'nt', tm=tm_l1, tn=d)
    d_cw_out = _mm("mix_out_l1_dw", y_cv, d_o1, 'tn', tm=512, tn=512)
    dp3, d_cw = _conv_bwd("conv_bwd", p3, cwp, dy_cv, tc)
    du_cv = _conv_din("conv_in_dx", dp3, cw_in_full, tm_l1)
    d_cw_in = _conv_dw_in("conv_in_dw", u_cv, dp3, _pick(d, MM_ROWS), _pick(d, 512))
    dh4, acc_n1 = _adaln_bwd("adaln_bwd_l1m", h4, du_cv, dh5, gains[1], 1, mods[1], 1, tml, None)
    put(1, 1, (acc_n1[:, 0], acc_n1[:, 1], acc_g1[:, 0]))
    dgain[1][1] = acc_n1[0, 2]
    scatter[5] = chips_start(5, pair[5], dh4)
    pair[4] = pair_start(4, [by_shard_cols(d_cw_in), by_shard_rows(d_cw_out)])
    dh3, ffn_g, tr, dgain[1][0] = _ffn_half_bwd("l1a", dh4, sav_f10, gains[1], mods[1], 0, ffn_w[1][0],
                                                empty_ffn_grads(), tml, None, tm_l1,
                                                scatter[5]['token'] + pair[4]['token'])
    put(1, 0, tr)
    finish_stage(5, landed_sums(5, scatter[5], dh3))
    scatter[4] = chips_start(4, pair[4], dh3)
    pair[3] = pair_start(3, list(ffn_g))

    ds3 = jnp.concatenate([dh3, jnp.zeros((g_len, d), F32)], axis=0) \
        + (scatter[4]['token'][0, 0] + pair[3]['token'][0, 0])
    ds2, ffn_g, tr, dgain[0][2] = _ffn_half_bwd("l0b", ds3, sav_f01, gains[0], mods[0], 2, ffn_w[0][1],
                                                empty_ffn_grads(), tm, h_tiles, tm_l0)
    put(0, 2, tr)
    finish_stage(4, landed_sums(4, scatter[4], ds2))
    scatter[3] = chips_start(3, pair[3], ds2)
    pair[2] = pair_start(2, list(ffn_g))
    d_o0, acc_g0 = _resid_bwd("resid_bwd_l0m", ds2, o_mix0, mods[0], 1, 1.0, tm, h_tiles,
                              scatter[3]['token'] + pair[2]['token'])
    dcat = _mm("mix_out_l0_dx", d_o0, w_out_p, 'nt', tm=tm_l0, tn=pool_dim + hw)
    d_w_out_p = _mm("mix_out_l0_dw", cat, d_o0, 'tn', tm=512, tn=512)
    col_blk = pool_dim // HEAD_PAD
    dq_h, dk_h, dv_h = _attn_bwd("attn_bwd_h", q_r, k_p, v_p, cat, dcat, lse_h, t_len, 0, r_len, 0, heads, tq_h,
                                 attn_scale, col_blk)
    dq_g, dk_all, dv_all = _attn_bwd("attn_bwd_g", q_r, k_p, v_p, cat, dcat, lse_g, g_len, t_len, g_len,
                                     t_len // g_len, heads, tm, attn_scale, col_blk, onto=(dk_h, dv_h))
    dq_all = jnp.concatenate([dq_h, dq_g], axis=0)
    dkvn = _mm("k_up_dx", dk_all, wk_ext, 'nt', tm=tm_l0, tn=kvr_w)
    dkvn = _mm("v_up_dx", dv_all, wv_ext, 'nt', tm=tm_l0, tn=kvr_w, epi=lambda acc, i, prev: (acc + prev,),
               epi_args=(dkvn,), epi_kinds=('mn',))
    d_wk_ext = _mm("k_up_dw", kvn, dk_all, 'tn', tm=kvr_w, tn=512)
    d_wv_ext = _mm("v_up_dw", kvn, dv_all, 'tn', tm=kvr_w, tn=512)

    def kvn_bwd_fn(rows, consts, m):
        pv, dv_, ct, st = rows
        g = consts[0][0:1, :]
        n, r = _rms(pv[:, :kv_rank])
        dyn = dv_[:, :kv_rank]
        dckv = _rms_bwd(dyn * g, n, r)
        dkr = _rope_t(dv_[:, kv_rank:], ct, st)
        return [jnp.concatenate([dckv, dkr], axis=1)], {0: jnp.sum(dyn * n, axis=0, keepdims=True)}

    dp_kvr, acc_kvg = _rows_call("kv_norm_bwd", kvn_bwd_fn, r_len, tm, [p_kvr, dkvn, tk_c, tk_s], [kvg], None,
                                 [(kvr_w, BF)], acc_w=kv_rank)

    def qrope_bwd_fn(rows, consts, m):
        return [_rope_t(rows[0], rows[1], rows[2])], {}

    dq_pad = _rows_call("q_rope_bwd", qrope_bwd_fn, r_len, tm, [dq_all, tq_c, tq_s], [], None, [(hw, BF)])[0]
    dqn = _mm("q_up_dx", dq_pad, wq_p, 'nt', tm=tm_l0, tn=q_rank)
    d_wq_p = _mm("q_up_dw", qn, dq_pad, 'tn', tm=512, tn=512)

    def qn_bwd_fn(rows, consts, m):
        pv, dv_ = rows
        g = consts[0][0:1, :]
        n, r = _rms(pv)
        return [_rms_bwd(dv_ * g, n, r)], {0: jnp.sum(dv_ * n, axis=0, keepdims=True)}

    dp_q, acc_qg = _rows_call("q_norm_bwd", qn_bwd_fn, r_len, tm, [p_q, dqn], [qg], None, [(q_rank, BF)],
                              acc_w=q_rank)
    dpu_h, dpw_h, dps_h = _pool_bwd("pool_bwd_h", p_pool[:t_len], dcat, pool_w[0], pool_scale, 0)
    dpu_g, dpw_g, dps_g = _pool_bwd("pool_bwd_g", p_pool[t_len:], dcat, pool_w[0], pool_scale, t_len)
    dp_pool = jnp.concatenate([dpu_h, dpu_g], axis=0)
    add_prev = lambda acc, i, prev: (acc + prev,)
    du_mix = _mm("in_pool_dx", dp_pool, w_in_pool, 'nt', tm=tm_l0, tn=d)
    du_mix = _mm("in_q_dx", dp_q, w_in_q, 'nt', tm=tm_l0, tn=d, epi=add_prev, epi_args=(du_mix,), epi_kinds=('mn',))
    du_mix = _mm("in_kvr_dx", dp_kvr, w_in_kvr, 'nt', tm=tm_l0, tn=d, epi=add_prev, epi_args=(du_mix,), epi_kinds=('mn',))
    d_w_in = jnp.concatenate([
        _mm("in_pool_dw", u_mix, dp_pool, 'tn', tm=512, tn=pool_dim),
        _mm("in_q_dw", u_mix, dp_q, 'tn', tm=512, tn=q_rank),
        _mm("in_kvr_dw", u_mix, dp_kvr, 'tn', tm=512, tn=kvr_w)[:, :kv_rank + QK_ROPE]], axis=1)
    ds1, acc_n0 = _adaln_bwd("adaln_bwd_l0m", s1, du_mix, ds2, gains[0], 1, mods[0], 1, tm, h_tiles)
    put(0, 1, (acc_n0[:, 0], acc_n0[:, 1], acc_g0[:, 0]))
    dgain[0][1] = jnp.sum(acc_n0[:, 2], axis=0)
    d_w_uq = d_wq_p.reshape(q_rank, heads, HEAD_PAD)[..., :QK_NOPE + QK_ROPE].reshape(q_rank, -1)
    d_w_ukv = jnp.concatenate([d_wk_ext[:kv_rank].reshape(kv_rank, heads, HEAD_PAD)[..., :QK_NOPE],
                               d_wv_ext[:kv_rank].reshape(kv_rank, heads, HEAD_PAD)[..., :V_HEAD]],
                              axis=-1).reshape(kv_rank, -1)
    d_w_out = jnp.concatenate([d_w_out_p[:pool_dim],
                               d_w_out_p[pool_dim:].reshape(heads, HEAD_PAD, d)[:, :V_HEAD].reshape(-1, d)], axis=0)
    finish_stage(3, landed_sums(3, scatter[3], ds1))
    scatter[2] = chips_start(2, pair[2], ds1)
    pair[1] = pair_start(1, [by_shard_cols(d_w_in), by_shard_rows(d_w_uq), by_shard_cols(d_w_ukv),
                             by_shard_rows(d_w_out)])
    ds0, ffn_g, tr, dgain[0][0] = _ffn_half_bwd("l0a", ds1, sav_f00, gains[0], mods[0], 0, ffn_w[0][0],
                                                empty_ffn_grads(), tm, h_tiles, tm_l0,
                                                scatter[2]['token'] + pair[1]['token'])
    put(0, 0, tr)
    grad_x = ds0[:t_len][None]
    finish_stage(2, landed_sums(2, scatter[2], ds0))
    pair[0] = pair_start(0, list(ffn_g))

    dmh = jnp.stack([jnp.stack([dmod[l][k][0] for k in range(N_MOD)]) for l in range(2)])
    dmg0 = jnp.stack([dmod[0][k][1] for k in range(N_MOD)])
    dg_rows = jnp.stack([dgain[l][k] for l in range(2) for k in range(3)])
    pieces = [dmh.reshape(2 * N_MOD, d), dmg0, dg_rows, d_cw[:3], d_final_g[None],
              (dpw_h + dpw_g).reshape(-1, d), jnp.pad((dps_h + dps_g)[0], (0, d - pool_dim))[None],
              jnp.pad(acc_qg[0, 0], (0, d - q_rank))[None], jnp.pad(acc_kvg[0, 0], (0, d - kv_rank))[None],
              acc_loss[0, 1][None]]
    n_piece = [p.shape[0] for p in pieces]
    pieces = [jnp.pad(p, ((0, (-p.shape[0]) % 8), (0, 0))) for p in pieces]
    small_g = jnp.concatenate(pieces, axis=0) + pair[0]['token'][0, 0]
    sg_all = _gather_all("gather_small_grads", [small_g], two_level=True)[0]
    sg_sum = _sum_lead("sum_small_grads", sg_all)
    scatter[1] = chips_start(1, pair[1], sg_sum)
    offs = [0]
    for p in pieces:
        offs.append(offs[-1] + p.shape[0])
    part = lambda j: sg_sum[offs[j]:offs[j] + n_piece[j]]
    sum_dmh, sum_dmg0, g_norm_full, g_conv_w_full = part(0).reshape(2, N_MOD * d), part(1).reshape(N_MOD * d), part(2), part(3)
    g_final = part(4)[0]
    loss = part(9)[0, 0]
    g_pool_w = part(5).reshape(pool_w.shape)
    g_pool_scale = part(6)[:, :pool_dim]
    g_q_norm = part(7)[:, :q_rank]
    g_kv_norm = part(8)[:, :kv_rank]
    col0 = shard * (d // N_SHARD)
    g_norm_g = lax.dynamic_slice_in_dim(g_norm_full.reshape(2, 3, d), col0, d // N_SHARD, axis=2)
    g_conv_w = lax.dynamic_slice_in_dim(g_conv_w_full, col0, d // N_SHARD, axis=1)[None]
    g_b_mod = _sum_lead("sum_b_mod", jnp.stack([sum_dmh, jnp.stack([sum_dmg0, jnp.zeros_like(sum_dmg0)])]))

    dm16 = []
    for l in range(2):
        per_dev = sg_all[:, l * N_MOD:(l + 1) * N_MOD].reshape(N_DEV, N_MOD * d)
        row8 = (sum_dmg0 if l == 0 else jnp.zeros_like(sum_dmg0)) + scatter[1]['token'][0, 0]
        full = jnp.concatenate([per_dev, row8[None], jnp.zeros((7, N_MOD * d), F32)], axis=0)
        dm16.append(lax.dynamic_slice_in_dim(full, shard * n_col, n_col, axis=1))
    g_w_mod = [_mm(f"mod_dw_{l}", c16, dm16[l], 'tn', tm=512, tn=768, a_pre=_silu) for l in range(2)]
    dc16 = _mm("mod_dx", dm16[0], w_mod, 'nt', tm=16, tn=512, b_lead=0, epi=lambda acc, i, cv: (acc * _dsilu(cv),),
               epi_args=(c16,), epi_kinds=('mn',))
    dc_all = _gather_all("gather_dc", [dc16])[0]
    g_c_ctx = _sum_lead("sum_dc", dc_all[::2])[8]

    grads.update(c_ctx=g_c_ctx, norm_g=g_norm_g, b_mod=g_b_mod, pool_w=g_pool_w,
                 pool_scale=g_pool_scale, q_norm_g=g_q_norm, kv_norm_g=g_kv_norm, conv_w=g_conv_w, final_norm_g=g_final)
    names = list(weights)

    scatter[0] = chips_start(0, pair[0], g_c_ctx)
    upd.update({n: _adamw(f"adamw_{n}", weights[n], grads[n].reshape(weights[n].shape), mom_m[n], mom_v[n],
                          scatter[0]['token']) for n in names if n not in big_names and n != "w_mod"})
    mod_out = None
    for l in range(2):
        mod_out = _adamw_piece(f"adamw_w_mod_{l}", w_mod, g_w_mod[l], mom_m["w_mod"], mom_v["w_mod"], (l,), mod_out,
                               scatter[0]['token'])
    grads["w_mod"], upd["w_mod"] = mod_out[0], mod_out[1:]
    finish_stage(1, landed_sums(1, scatter[1], upd["w_mod"][0]))
    finish_stage(0, landed_sums(0, scatter[0], upd[mixer_names[0][-1]][0]))
    for nm in ffn_names:
        done = [lanes_last(nm, t) for t in ffn_out[nm]]
        grads[nm], upd[nm] = done[0], done[1:]
    return (loss, grad_x, *[grads[n].reshape(weights[n].shape) for n in names], *[upd[n][0] for n in names],
            *[upd[n][1] for n in names], *[upd[n][2] for n in names])
```

```python
import math

import jax
import jax.numpy as jnp
from jax import lax
from jax.experimental import pallas as pl
from jax.experimental.pallas import tpu as pltpu

F32 = jnp.float32
BF = jnp.bfloat16
MESH = pl.DeviceIdType.MESH

N_DEV = 8
N_SHARD = 4
RMS_EPS = 1e-6
N_MOD = 9
POOL_WINDOWS = (2, 4, 8, 16)
QK_NOPE = 64
QK_ROPE = 32
V_HEAD = 64
HEAD_PAD = 128
GRID_W = 64
ROPE_THETA = 10000.0
POOL_PAD = 16
ADAM_LR, ADAM_B1, ADAM_B2, ADAM_EPS, ADAM_WD, ADAM_STEP = 0.001, 0.9, 0.999, 1e-08, 0.01, 10
VMEM_LIMIT = 56 * 1024 * 1024
MM_ROWS = 1024


def _pcall(body, **kw):
    return pl.pallas_call(body, **kw)


def _params(sem=None):
    return pltpu.CompilerParams(dimension_semantics=sem, vmem_limit_bytes=VMEM_LIMIT)


def _pick(n, pref, mult=128):
    best = None
    d = mult
    while d <= min(n, pref):
        if n % d == 0:
            best = d
        d += mult
    return best if best is not None else n


def _silu(z):
    return z * jax.nn.sigmoid(z)


def _dsilu(z):
    s = jax.nn.sigmoid(z)
    return s * (1.0 + z * (1.0 - s))


def _dot(a, b, dims):
    return lax.dot_general(a.astype(BF), b.astype(BF), (dims, ((), ())), preferred_element_type=F32)


NN = ((1,), (0,))
NT = ((1,), (1,))
TN = ((0,), (0,))


ALL_FLIPS = [(kx, ky, kc) for kx in (0, 1) for ky in (0, 1) for kc in (0, 1) if (kx, ky, kc) != (0, 0, 0)]
CHIP_FLIPS = [(1, 0, 0), (0, 1, 0), (1, 1, 0)]
SIBLING = (0, 0, 1)
COMM_SPLIT = 8
SPLIT_MIN_ROWS = 256


def _exchange(name, arrays, plan, lead, whole_src, split=COMM_SPLIT):
    n = len(arrays)
    blk_shapes = [tuple(a.shape) if whole_src else tuple(a.shape[1:]) for a in arrays]
    splits = []
    for shp in blk_shapes:
        s = 1
        while s * 2 <= split and shp[0] % (s * 2) == 0 and (shp[0] // (s * 2)) % 16 == 0 \
                and shp[0] // (s * 2) >= SPLIT_MIN_ROWS:
            s *= 2
        splits.append(s)
    items = plan(0, 0, 0)
    n_items = len(items)
    remote_ids = [k for k, it in enumerate(items) if it[0] is not None]
    local_ids = [k for k, it in enumerate(items) if it[0] is None]
    slots = [(a, s) for s in range(max(splits)) for a in range(n) if s < splits[a]]
    n_slot = len(slots)

    def body(*refs):
        ins, outs = refs[:n], refs[n:2 * n]
        send_sems, recv_sems, loc_sems = refs[2 * n:]
        x, y, c = lax.axis_index("x"), lax.axis_index("y"), lax.axis_index("c")
        plan_here = plan(x, y, c)

        def rows(ref, a, s):
            rc = blk_shapes[a][0] // splits[a]
            return ref.at[pl.ds(s * rc, rc)]

        def make(si, k):
            a, s = slots[si]
            flip, src, dst, _ = plan_here[k]
            base = outs[a] if src[0] == 'out' else ins[a]
            src_ref = rows(base if src[1] is None else base.at[src[1]], a, s)
            dst_ref = rows(outs[a].at[dst], a, s)
            if flip is None:
                return pltpu.make_async_copy(src_ref, dst_ref, loc_sems.at[si * max(1, len(local_ids)) + local_ids.index(k)])
            peer = (1 - x if flip[0] else x, 1 - y if flip[1] else y, 1 - c if flip[2] else c)
            sem = si * len(remote_ids) + remote_ids.index(k)
            return pltpu.make_async_remote_copy(src_ref=src_ref, dst_ref=dst_ref, send_sem=send_sems.at[sem],
                                                recv_sem=recv_sems.at[sem], device_id=peer, device_id_type=MESH)

        copies = {}
        for si in range(n_slot):
            for k in range(n_items):
                if plan_here[k][3] is None:
                    copies[si, k] = make(si, k)
                    copies[si, k].start()
        arrived = set()
        for si in range(n_slot):
            for k in range(n_items):
                after = plan_here[k][3]
                if after is not None:
                    if (si, after) not in arrived:
                        copies[si, after].wait_recv()
                        arrived.add((si, after))
                    copies[si, k] = make(si, k)
                    copies[si, k].start()
        for (si, k), cp in copies.items():
            if plan_here[k][0] is None:
                cp.wait()
            else:
                cp.wait_send()
                if (si, k) not in arrived:
                    cp.wait_recv()

    any_spec = pl.BlockSpec(memory_space=pl.ANY)
    n_rem = max(1, n_slot * len(remote_ids))
    outs = _pcall(
        body, name=name,
        out_shape=[jax.ShapeDtypeStruct((lead,) + s, a.dtype) for s, a in zip(blk_shapes, arrays)],
        in_specs=[any_spec] * n, out_specs=[any_spec] * n,
        scratch_shapes=[pltpu.SemaphoreType.DMA((n_rem,)), pltpu.SemaphoreType.DMA((n_rem,)),
                        pltpu.SemaphoreType.DMA((max(1, n_slot * len(local_ids)),))],
    )(*arrays)
    return list(outs)


def _place(x, y, c):
    return 4 * x + 2 * y + c


def _flip(v, f):
    return 1 - v if f else v


def _gather_all(name, arrays, two_level=False):
    def plan(x, y, c):
        me = _place(x, y, c)
        if not two_level:
            return [(None, ('in', None), me, None)] + [(f, ('in', None), me, None) for f in ALL_FLIPS]
        items = [(None, ('in', None), me, None), (SIBLING, ('in', None), me, None)]
        items += [(f, ('in', None), me, None) for f in CHIP_FLIPS]
        for j, f in enumerate(CHIP_FLIPS):
            got = _place(_flip(x, f[0]), _flip(y, f[1]), c)
            items.append((SIBLING, ('out', got), got, 2 + j))
        return items
    return _exchange(name, arrays, plan, N_DEV, True, split=1)


HBM_SPEC = pl.BlockSpec(memory_space=pltpu.HBM)
SEM_SPEC = pl.BlockSpec(memory_space=pltpu.SEMAPHORE)
SIDE_EFFECT = pltpu.SideEffectType.DATAFLOW_SIDE_EFFECTING


def _split_start(name, bufs, n_copies, build):
    n = len(bufs)

    def body(*refs):
        for cp in build(refs[:n], refs[n], refs[n + 1]):
            cp.start()
        token = refs[-1]
        token[...] = jnp.zeros_like(token)

    res = _pcall(
        body, name=name,
        out_shape=(pltpu.SemaphoreType.DMA((n_copies,)), pltpu.SemaphoreType.DMA((n_copies,)),
                   *[pltpu.HBM(b.shape, b.dtype) for b in bufs], jax.ShapeDtypeStruct((8, 128), F32)),
        in_specs=[HBM_SPEC] * n,
        out_specs=(SEM_SPEC, SEM_SPEC, *[HBM_SPEC] * n, pl.BlockSpec(memory_space=pltpu.VMEM)),
        input_output_aliases={i: 2 + i for i in range(n)},
        compiler_params=pltpu.CompilerParams(has_side_effects=SIDE_EFFECT),
    )(*[pltpu.with_memory_space_constraint(b, pltpu.HBM) for b in bufs])
    return dict(send=res[0], recv=res[1], bufs=list(res[2:2 + n]), token=res[-1], build=build)


def _split_wait(name, handle, after):
    n = len(handle['bufs'])
    build = handle['build']

    def body(*refs):
        for cp in build(refs[:n], refs[n], refs[n + 1]):
            cp.wait_send()
            cp.wait_recv()

    res = _pcall(
        body, name=name, out_shape=tuple(pltpu.HBM(b.shape, b.dtype) for b in handle['bufs']),
        in_specs=[HBM_SPEC] * n + [SEM_SPEC, SEM_SPEC, pl.BlockSpec(memory_space=pl.ANY)],
        out_specs=tuple([HBM_SPEC] * n), input_output_aliases={i: i for i in range(n)},
        compiler_params=pltpu.CompilerParams(has_side_effects=SIDE_EFFECT),
    )(*handle['bufs'], handle['send'], handle['recv'], after)
    return list(res)


def _landing(lead, arrays):
    return [pltpu.with_memory_space_constraint(lax.empty((lead,) + tuple(a.shape[1:]), a.dtype), pltpu.HBM)
            for a in arrays]


def _copy_list(n, per_array, make):
    def build(refs, send_sems, recv_sems):
        copies = []
        for a in range(n):
            for j in range(per_array):
                src, dst, peer = make(refs, a, j)
                k = a * per_array + j
                copies.append(pltpu.make_async_remote_copy(src_ref=src, dst_ref=dst, send_sem=send_sems.at[k],
                                                           recv_sem=recv_sems.at[k], device_id=peer,
                                                           device_id_type=MESH))
        return copies
    return build


def _mesh_place():
    x, y, c = lax.axis_index("x"), lax.axis_index("y"), lax.axis_index("c")
    return x, y, c, 2 * x + y


def _chips_gather_build(n):
    def make(refs, a, j):
        x, y, c, chip = _mesh_place()
        px, py = _flip(x, CHIP_FLIPS[j][0]), _flip(y, CHIP_FLIPS[j][1])
        return refs[a].at[c], refs[n + a].at[2 * chip + c], (px, py, c)
    return _copy_list(n, len(CHIP_FLIPS), make)


def _chips_scatter_build(n):
    def make(refs, a, j):
        x, y, c, chip = _mesh_place()
        px, py = _flip(x, CHIP_FLIPS[j][0]), _flip(y, CHIP_FLIPS[j][1])
        return refs[a].at[2 * px + py], refs[n + a].at[chip], (px, py, c)
    return _copy_list(n, len(CHIP_FLIPS), make)


def _sibling_forward_build(n):
    def make(refs, a, j):
        x, y, c, _ = _mesh_place()
        blk = 2 * (2 * _flip(x, CHIP_FLIPS[j][0]) + _flip(y, CHIP_FLIPS[j][1])) + c
        return refs[a].at[blk], refs[a].at[blk], (x, y, 1 - c)
    return _copy_list(n, len(CHIP_FLIPS), make)


def _sibling_halves_build(n):
    def make(refs, a, j):
        x, y, c, _ = _mesh_place()
        return refs[a].at[2 * j + 1 - c], refs[n + a].at[j], (x, y, 1 - c)
    return _copy_list(n, N_SHARD, make)


def _sibling_whole_build(n):
    def make(refs, a, j):
        x, y, c, _ = _mesh_place()
        return refs[a], refs[n + a], (x, y, 1 - c)
    return _copy_list(n, 1, make)


def _add_halves(name, send, land):
    _, r, cdim = send.shape
    tr = _pick(r, max(16, (1 << 20) // (cdim * 2)), 16)

    def body(c_ref, own_ref, got_ref, o_ref):
        o_ref[...] = (own_ref[...].astype(F32) + got_ref[...].astype(F32)).astype(BF)

    grid_spec = pltpu.PrefetchScalarGridSpec(
        num_scalar_prefetch=1, grid=(N_SHARD, r // tr),
        in_specs=[pl.BlockSpec((None, tr, cdim), lambda sh, i, cr: (2 * sh + cr[0], i, 0)),
                  pl.BlockSpec((None, tr, cdim), lambda sh, i, cr: (sh, i, 0))],
        out_specs=pl.BlockSpec((None, tr, cdim), lambda sh, i, cr: (sh, i, 0)))
    core = lax.axis_index("c").astype(jnp.int32).reshape(1)
    return _pcall(body, name=name, grid_spec=grid_spec, out_shape=jax.ShapeDtypeStruct((N_SHARD, r, cdim), BF),
                  compiler_params=_params(("arbitrary", "arbitrary")))(core, send, land)


def _sum_lead(name, arr, out_dtype=F32):
    n, r, cdim = arr.shape
    tr = r
    limit = (4 << 20) // (n * cdim * arr.dtype.itemsize)
    if r > limit:
        tr = _pick(r, max(limit, 16), 16)

    def body(x_ref, o_ref):
        acc = x_ref[0].astype(F32)
        for d in range(1, n):
            acc = acc + x_ref[d].astype(F32)
        o_ref[...] = acc.astype(out_dtype)

    return _pcall(body, name=name, grid=(r // tr,),
                  in_specs=[pl.BlockSpec((n, tr, cdim), lambda i: (0, i, 0))],
                  out_specs=pl.BlockSpec((tr, cdim), lambda i: (i, 0)),
                  out_shape=jax.ShapeDtypeStruct((r, cdim), out_dtype),
                  compiler_params=_params(("arbitrary",)))(arr)


def _rows_call(name, fn, n_rows, tm, rows, consts, mod, outs, acc_w=None, h_tiles=None):
    nt = n_rows // tm
    ht = nt if h_tiles is None else h_tiles
    ng = 1 if mod is None else mod.shape[0]
    n_r, n_c, n_o = len(rows), len(consts), len(outs)
    has_mod = mod is not None

    def body(*refs):
        i = pl.program_id(0)
        first = (i % ht) == 0
        row_refs, const_refs = refs[:n_r], refs[n_r:n_r + n_c]
        p = n_r + n_c
        mod_tile = refs[p][...] if has_mod else None
        p += int(has_mod)
        out_refs = refs[p:p + n_o]
        o, acc = fn([r[...] for r in row_refs], [r[...] for r in const_refs], mod_tile)
        for r, v in zip(out_refs, o):
            r[...] = v.astype(r.dtype)
        if acc_w is not None:
            acc_ref = refs[p + n_o]

            @pl.when(first)
            def _():
                acc_ref[...] = jnp.zeros_like(acc_ref)

            for k, v in acc.items():
                acc_ref[k:k + 1, :] += v

    in_specs = [pl.BlockSpec((tm, r.shape[1]), lambda i: (i, 0)) for r in rows]
    in_specs += [pl.BlockSpec(cst.shape, lambda i, nd=cst.ndim: (0,) * nd) for cst in consts]
    args = list(rows) + list(consts)
    if has_mod:
        in_specs.append(pl.BlockSpec((None,) + mod.shape[1:], lambda i: (i // ht, 0, 0)))
        args.append(mod)
    out_shape = [jax.ShapeDtypeStruct((n_rows, w), dt) for w, dt in outs]
    out_specs = [pl.BlockSpec((tm, w), lambda i: (i, 0)) for w, _ in outs]
    if acc_w is not None:
        out_shape.append(jax.ShapeDtypeStruct((ng, 8, acc_w), F32))
        out_specs.append(pl.BlockSpec((None, 8, acc_w), lambda i: (i // ht, 0, 0)))
    res = _pcall(body, name=name, grid=(nt,), in_specs=in_specs, out_specs=out_specs, out_shape=out_shape,
                 compiler_params=_params(("arbitrary",)))(*args)
    return list(res)


def _rms(s):
    r = lax.rsqrt(jnp.mean(s * s, axis=1, keepdims=True) + RMS_EPS)
    return s * r, r


def _rms_bwd(dn, n, r):
    return r * (dn - n * jnp.mean(dn * n, axis=1, keepdims=True))


def _adaln_fwd(name, s, gains, gain_row, mod, k, tm, h_tiles, after=None):
    def fn(rows, consts, m):
        n, _ = _rms(rows[0])
        y = n * consts[0][gain_row:gain_row + 1, :]
        return [y * (1.0 + m[3 * k + 1:3 * k + 2, :]) + m[3 * k:3 * k + 1, :]], {}

    d = s.shape[1]
    consts = [gains] if after is None else [gains, after]
    return _rows_call(name, fn, s.shape[0], tm, [s], consts, mod, [(d, BF)], h_tiles=h_tiles)[0]


def _adaln_bwd(name, s, du, ds_res, gains, gain_row, mod, k, tm, h_tiles):
    def fn(rows, consts, m):
        sv, duv, res = rows
        gain = consts[0][gain_row:gain_row + 1, :]
        n, r = _rms(sv)
        y = n * gain
        dy = duv * (1.0 + m[3 * k + 1:3 * k + 2, :])
        acc = {0: jnp.sum(duv, axis=0, keepdims=True), 1: jnp.sum(duv * y, axis=0, keepdims=True),
               2: jnp.sum(dy * n, axis=0, keepdims=True)}
        return [_rms_bwd(dy * gain, n, r) + res], acc

    d = s.shape[1]
    return _rows_call(name, fn, s.shape[0], tm, [s, du, ds_res], [gains], mod, [(d, F32)], acc_w=d, h_tiles=h_tiles)


def _resid_bwd(name, ds_out, o, mod, k, cst, tm, h_tiles, after=None):
    def fn(rows, consts, m):
        dsv, ov = rows
        gate = m[3 * k + 2:3 * k + 3, :]
        return [cst * gate * dsv], {0: jnp.sum(cst * ov * dsv, axis=0, keepdims=True)}

    d = o.shape[1]
    consts = [] if after is None else [after]
    return _rows_call(name, fn, o.shape[0], tm, [ds_out, o], consts, mod, [(d, BF)], acc_w=d, h_tiles=h_tiles)


def _mm(name, a, b, mode, tm=256, tn=512, out_dtypes=(F32,), epi=None, epi_args=(), epi_kinds=(), a_pre=None,
        b_lead=None):
    bshape = b.shape if b_lead is None else b.shape[1:]
    if mode == 'nn':
        (m, kd), nd = a.shape, bshape[1]
    elif mode == 'nt':
        (m, kd), nd = a.shape, bshape[0]
    else:
        (kd, m), nd = a.shape, bshape[1]
    tm = _pick(m, tm, 16) if m % tm else tm
    tn = _pick(nd, tn, 128) if nd % tn else tn
    dims = {'nn': NN, 'nt': NT, 'tn': TN}[mode]
    n_e, n_o = len(epi_args), len(out_dtypes)

    def body(*refs):
        i = pl.program_id(1)
        av = refs[0][...]
        if a_pre is not None:
            av = a_pre(av)
        acc = _dot(av, refs[1][...], dims)
        res = (acc,) if epi is None else epi(acc, i, *[r[...] for r in refs[2:2 + n_e]])
        for r, v in zip(refs[2 + n_e:], res):
            r[...] = v.astype(r.dtype)

    if mode == 'nn':
        specs = [pl.BlockSpec((tm, kd), lambda j, i: (i, 0)), pl.BlockSpec((kd, tn), lambda j, i: (0, j))]
    elif mode == 'nt':
        specs = [pl.BlockSpec((tm, kd), lambda j, i: (i, 0)), pl.BlockSpec((tn, kd), lambda j, i: (j, 0))]
    else:
        specs = [pl.BlockSpec((kd, tm), lambda j, i: (0, i)), pl.BlockSpec((kd, tn), lambda j, i: (0, j))]
    if b_lead is not None:
        shape2, at2 = specs[1].block_shape, specs[1].index_map
        specs[1] = pl.BlockSpec((None,) + tuple(shape2), lambda j, i: (b_lead,) + tuple(at2(j, i)))
    for arr, kind in zip(epi_args, epi_kinds):
        if kind == 'mn':
            specs.append(pl.BlockSpec((tm, tn), lambda j, i: (i, j)))
        elif kind == 'n':
            specs.append(pl.BlockSpec((1, tn), lambda j, i: (0, j)))
        elif kind == 'mt':
            specs.append(pl.BlockSpec((tm, arr.shape[1]), lambda j, i: (i, 0)))
        else:
            specs.append(pl.BlockSpec(arr.shape, lambda j, i, nd_=arr.ndim: (0,) * nd_))
    res = _pcall(body, name=name, grid=(nd // tn, m // tm), in_specs=specs,
                 out_specs=[pl.BlockSpec((tm, tn), lambda j, i: (i, j))] * n_o,
                 out_shape=[jax.ShapeDtypeStruct((m, nd), dt) for dt in out_dtypes],
                 compiler_params=_params(("arbitrary", "arbitrary")))(a, b, *epi_args)
    return res[0] if n_o == 1 else list(res)


def _row_gate(mod, k3, i, tm, n_lat):
    g0 = mod[0, k3:k3 + 1, :]
    if mod.shape[0] == 1:
        return g0
    rid = i * tm + lax.broadcasted_iota(jnp.int32, (tm, 1), 0)
    return jnp.where(rid < n_lat, g0, mod[1, k3:k3 + 1, :])


def _ffn_up(name, u, wg, wu, base, tm):
    r, d = u.shape
    nch, _, _, fc = wg.shape

    def body(u_ref, wg_ref, wu_ref, a_ref, b_ref, h_ref):
        uv = u_ref[...]
        a = _dot(uv, wg_ref[...], NN)
        b = _dot(uv, wu_ref[...], NN)
        a_ref[...] = a.astype(BF)
        b_ref[...] = b.astype(BF)
        h_ref[...] = (_silu(a) * b).astype(BF)

    chunk = pl.BlockSpec((None, tm, fc), lambda j, i: (j, i, 0))
    return _pcall(body, name=name, grid=(nch, r // tm),
                  in_specs=[pl.BlockSpec((tm, d), lambda j, i: (i, 0)),
                            pl.BlockSpec((None, None, d, fc), lambda j, i: (j, base, 0, 0)),
                            pl.BlockSpec((None, None, d, fc), lambda j, i: (j, base, 0, 0))],
                  out_specs=[chunk] * 3, out_shape=[jax.ShapeDtypeStruct((nch, r, fc), BF)] * 3,
                  compiler_params=_params(("arbitrary", "arbitrary")))(u, wg, wu)


def _ffn_down(name, hid, wd, wd_blk, s, mod, k, n_lat, tm):
    nch, r, fc = hid.shape
    d = wd.shape[2]

    def body(h_ref, w_ref, s_ref, m_ref, so_ref, o_ref):
        i = pl.program_id(0)
        o = _dot(h_ref[0], w_ref[0], NN)
        for j in range(1, nch):
            o = o + _dot(h_ref[j], w_ref[j], NN)
        o_ref[...] = o
        so_ref[...] = s_ref[...] + 0.5 * _row_gate(m_ref[...], 3 * k + 2, i, tm, n_lat) * o

    row = pl.BlockSpec((tm, d), lambda i: (i, 0))
    return _pcall(body, name=name, grid=(r // tm,),
                  in_specs=[pl.BlockSpec((nch, tm, fc), lambda i: (0, i, 0)),
                            pl.BlockSpec((nch, fc, d), lambda i: (0, wd_blk, 0)), row,
                            pl.BlockSpec(mod.shape, lambda i: (0, 0, 0))],
                  out_specs=[row, row], out_shape=[jax.ShapeDtypeStruct((r, d), F32)] * 2,
                  compiler_params=_params(("arbitrary",)))(hid, wd, s, mod)


def _ffn_dhid(name, d_o, wd, wd_blk, a, b, tm):
    r, d = d_o.shape
    nch, _, fc = a.shape

    def body(g_ref, w_ref, a_ref, b_ref, da_ref, db_ref):
        dh = _dot(g_ref[...], w_ref[...], NT)
        av, bv = a_ref[...].astype(F32), b_ref[...].astype(F32)
        sg = jax.nn.sigmoid(av)
        da_ref[...] = (dh * bv * (sg * (1.0 + av * (1.0 - sg)))).astype(BF)
        db_ref[...] = (dh * (av * sg)).astype(BF)

    chunk = pl.BlockSpec((None, tm, fc), lambda j, i: (j, i, 0))
    return _pcall(body, name=name, grid=(nch, r // tm),
                  in_specs=[pl.BlockSpec((tm, d), lambda j, i: (i, 0)),
                            pl.BlockSpec((None, fc, d), lambda j, i: (j, wd_blk, 0)), chunk, chunk],
                  out_specs=[chunk] * 2, out_shape=[jax.ShapeDtypeStruct((nch, r, fc), BF)] * 2,
                  compiler_params=_params(("arbitrary", "arbitrary")))(d_o, wd, a, b)


def _ffn_du(name, da, db, wg, wu, base, tm):
    nch, r, fc = da.shape
    d = wg.shape[2]

    def body(da_ref, db_ref, wg_ref, wu_ref, o_ref):
        acc = _dot(da_ref[0], wg_ref[0], NT) + _dot(db_ref[0], wu_ref[0], NT)
        for j in range(1, nch):
            acc = acc + _dot(da_ref[j], wg_ref[j], NT) + _dot(db_ref[j], wu_ref[j], NT)
        o_ref[...] = acc

    chunks = pl.BlockSpec((nch, tm, fc), lambda i: (0, i, 0))
    held = pl.BlockSpec((nch, None, d, fc), lambda i: (0, base, 0, 0), pipeline_mode=pl.Buffered(1))
    return _pcall(body, name=name, grid=(r // tm,), in_specs=[chunks, chunks, held, held],
                  out_specs=pl.BlockSpec((tm, d), lambda i: (i, 0)), out_shape=jax.ShapeDtypeStruct((r, d), F32),
                  compiler_params=_params(("arbitrary",)))(da, db, wg, wu)


def _ffn_dw_in(name, u, da, db, tmm, g_gate, g_up, idx):
    r, d = u.shape
    nch, _, fc = da.shape
    nb = d // tmm

    def body(u_ref, a_ref, b_ref, gg_ref, gu_ref, og_ref, ou_ref):
        ut = u_ref[...].T
        og_ref[...] = _dot(ut, a_ref[...], NN).astype(og_ref.dtype)
        ou_ref[...] = _dot(ut, b_ref[...], NN).astype(ou_ref.dtype)

    chunk = pl.BlockSpec((None, r, fc), lambda j, mi: (j, 0, 0))
    out = pl.BlockSpec((None, tmm, fc), lambda j, mi: (j, idx * nb + mi, 0))
    return _pcall(body, name=name, grid=(nch, nb),
                  in_specs=[pl.BlockSpec((r, tmm), lambda j, mi: (0, mi)), chunk, chunk,
                            pl.BlockSpec(memory_space=pl.ANY), pl.BlockSpec(memory_space=pl.ANY)],
                  out_specs=[out, out],
                  out_shape=[jax.ShapeDtypeStruct(g_gate.shape, g_gate.dtype),
                             jax.ShapeDtypeStruct(g_up.shape, g_up.dtype)],
                  input_output_aliases={3: 0, 4: 1},
                  compiler_params=_params(("arbitrary", "arbitrary")))(u, da, db, g_gate, g_up)


def _ffn_dw_down(name, hid, d_o, tn, grads, idx):
    nch, r, fc = hid.shape
    d = d_o.shape[1]

    def body(h_ref, g_ref, acc_ref, o_ref):
        o_ref[...] = _dot(h_ref[...], g_ref[...], TN).astype(o_ref.dtype)

    return _pcall(body, name=name, grid=(nch, d // tn),
                  in_specs=[pl.BlockSpec((None, r, fc), lambda j, ni: (j, 0, 0)),
                            pl.BlockSpec((r, tn), lambda j, ni: (0, ni)),
                            pl.BlockSpec(memory_space=pl.ANY)],
                  out_specs=pl.BlockSpec((None, fc, tn), lambda j, ni: (j, idx, ni)),
                  out_shape=jax.ShapeDtypeStruct(grads.shape, grads.dtype), input_output_aliases={2: 0},
                  compiler_params=_params(("arbitrary", "arbitrary")))(hid, d_o, grads)


def _partner(x):
    n = x.shape[1]
    lane = lax.broadcasted_iota(jnp.int32, x.shape, 1)
    return jnp.where((lane & 15) < 8, pltpu.roll(x, n - 8, 1), pltpu.roll(x, 8, 1))


def _rope(x, ct, st):
    reps = x.shape[1] // ct.shape[1]
    if reps > 1:
        ct, st = jnp.tile(ct, (1, reps)), jnp.tile(st, (1, reps))
    return x * ct + _partner(x) * st


def _rope_t(dy, ct, st):
    reps = dy.shape[1] // ct.shape[1]
    if reps > 1:
        ct, st = jnp.tile(ct, (1, reps)), jnp.tile(st, (1, reps))
    return dy * ct + _partner(dy * st)


def _rope_tables(t_len, g_len, lane0):
    half = QK_ROPE // 4
    pos = jnp.arange(t_len)
    row = (pos // GRID_W).astype(F32)
    col = (pos % GRID_W).astype(F32)
    freqs = jnp.power(ROPE_THETA, -jnp.arange(0, QK_ROPE // 2, 2, dtype=F32) / (QK_ROPE // 2))
    ang_r, ang_c = row[:, None] * freqs, col[:, None] * freqs
    cs = jnp.concatenate([jnp.cos(ang_r)] * 2 + [jnp.cos(ang_c)] * 2, axis=1)
    sn = jnp.concatenate([-jnp.sin(ang_r), jnp.sin(ang_r), -jnp.sin(ang_c), jnp.sin(ang_c)], axis=1)
    assert cs.shape[1] == 4 * half == QK_ROPE
    def place(tab, fill):
        rest = HEAD_PAD - lane0 - QK_ROPE
        rows = jnp.concatenate([jnp.full((t_len, lane0), fill, F32), tab, jnp.full((t_len, rest), fill, F32)], axis=1)
        return jnp.concatenate([rows, jnp.full((g_len, HEAD_PAD), fill, F32)], axis=0)

    return place(cs, 1.0), place(sn, 0.0)


def _attn_fwd(name, q, kp, vp, n_q, q_off, n_k, k_blk, heads, tq, scale):
    qb = q_off // tq
    per = 4 if heads % 4 == 0 else (2 if heads % 2 == 0 else 1)
    wide = per * HEAD_PAD

    def body(q_ref, k_ref, v_ref, o_ref, l_ref):
        for e in range(per):
            sl = slice(e * HEAD_PAD, (e + 1) * HEAD_PAD)
            s = _dot(q_ref[:, sl], k_ref[:, sl], NT) * scale
            m = jnp.max(s, axis=1, keepdims=True)
            p = jnp.exp(s - m)
            l = jnp.sum(p, axis=1, keepdims=True)
            o_ref[:, sl] = (_dot(p, v_ref[:, sl], NN) / l).astype(BF)
            l_ref[:, sl] = jnp.broadcast_to(m + jnp.log(l), (tq, HEAD_PAD))

    hw = heads * HEAD_PAD
    blk = pl.BlockSpec((tq, wide), lambda h, i: (i, h))
    kv = pl.BlockSpec((n_k, wide), lambda h, i: (k_blk, h))
    return _pcall(body, name=name, grid=(heads // per, n_q // tq),
                  in_specs=[pl.BlockSpec((tq, wide), lambda h, i: (i + qb, h)), kv, kv],
                  out_specs=[blk, blk],
                  out_shape=[jax.ShapeDtypeStruct((n_q, hw), BF), jax.ShapeDtypeStruct((n_q, hw), F32)],
                  compiler_params=_params(("arbitrary", "arbitrary")))(q, kp, vp)


def _attn_bwd(name, q, kp, vp, cat, dcat, lse, n_q, q_off, n_k, k_blk, heads, tq, scale, col_blk, onto=None):
    qb = q_off // tq

    per = 1
    wide = per * HEAD_PAD

    def body(q_ref, k_ref, v_ref, o_ref, do_ref, l_ref, *rest):
        dq_ref, dk_ref, dv_ref = rest[-3:]
        i = pl.program_id(1)
        for e in range(per):
            sl = slice(e * HEAD_PAD, (e + 1) * HEAD_PAD)
            qv, kv_, vv = q_ref[:, sl], k_ref[:, sl], v_ref[:, sl]
            dov = do_ref[:, sl]
            s = _dot(qv, kv_, NT) * scale
            p = jnp.exp(s - l_ref[:, e * HEAD_PAD:e * HEAD_PAD + 1])
            dp = _dot(dov, vv, NT)
            delta = jnp.sum(dov * o_ref[:, sl].astype(F32), axis=1, keepdims=True)
            ds = (p * (dp - delta) * scale).astype(BF)
            dq_ref[:, sl] = _dot(ds, kv_, NN)
            dk = _dot(ds, qv, TN)
            dv = _dot(p, dov, TN)

            @pl.when(i == 0)
            def _():
                if onto is None:
                    dk_ref[:, sl] = dk
                    dv_ref[:, sl] = dv
                else:
                    dk_ref[:, sl] = rest[0][:, sl] + dk
                    dv_ref[:, sl] = rest[1][:, sl] + dv

            @pl.when(i > 0)
            def _():
                dk_ref[:, sl] += dk
                dv_ref[:, sl] += dv

    hw = heads * HEAD_PAD
    heads = heads // per
    col_blk = col_blk // per
    qspec = pl.BlockSpec((tq, wide), lambda h, i: (i + qb, h))
    cspec = pl.BlockSpec((tq, wide), lambda h, i: (i + qb, col_blk + h))
    kv = pl.BlockSpec((n_k, wide), lambda h, i: (k_blk, h))
    blk = pl.BlockSpec((tq, wide), lambda h, i: (i, h))
    if onto is None:
        acc = pl.BlockSpec((n_k, wide), lambda h, i: (0, h))
        return _pcall(body, name=name, grid=(heads, n_q // tq),
                      in_specs=[qspec, kv, kv, cspec, cspec, blk], out_specs=[blk, acc, acc],
                      out_shape=[jax.ShapeDtypeStruct((n_q, hw), F32), jax.ShapeDtypeStruct((n_k, hw), F32),
                                 jax.ShapeDtypeStruct((n_k, hw), F32)],
                      compiler_params=_params(("arbitrary", "arbitrary")))(q, kp, vp, cat, dcat, lse)
    return _pcall(body, name=name, grid=(heads, n_q // tq),
                  in_specs=[qspec, kv, kv, cspec, cspec, blk, kv, kv], out_specs=[blk, kv, kv],
                  out_shape=[jax.ShapeDtypeStruct((n_q, hw), F32)] + [jax.ShapeDtypeStruct(t.shape, F32) for t in onto],
                  input_output_aliases={6: 1, 7: 2},
                  compiler_params=_params(("arbitrary", "arbitrary")))(q, kp, vp, cat, dcat, lse, *onto)


def _shift(x, k):
    return pltpu.roll(x, k % x.shape[0], 0)


def _window_sum(v, w, mirrored):
    n, gd = v.shape
    pad = jnp.zeros((POOL_PAD, gd), F32)
    e = jnp.concatenate([pad, v, pad], axis=0)
    acc = e + _shift(e, -1 if mirrored else 1)
    step = 1
    while 2 * step < w:
        acc = _shift(acc, step) + _shift(acc, -step)
        step *= 2
    return acc[POOL_PAD:POOL_PAD + n]


def _window_count(n, w):
    t = lax.broadcasted_iota(jnp.int32, (n, 1), 0)
    lo = jnp.maximum(t - w // 2, 0)
    hi = jnp.minimum(t + (w - w // 2 - 1), n - 1)
    return (hi - lo + 1).astype(F32)


def _pool_fwd(name, u, pool_w, scale):
    n, pd = u.shape
    ng = len(POOL_WINDOWS)
    gd = pd // ng

    def body(u_ref, w_ref, s_ref, y_ref):
        for g, w in enumerate(POOL_WINDOWS):
            sl = slice(g * gd, (g + 1) * gd)
            ug = u_ref[:, sl]
            p = _window_sum(ug, w, False) / _window_count(n, w) - ug
            y_ref[:, sl] = (_dot(p, w_ref[g], NN) * s_ref[:, sl]).astype(BF)

    return _pcall(body, name=name, out_shape=jax.ShapeDtypeStruct((n, pd), BF),
                  compiler_params=_params())(u, pool_w, scale)


def _pool_bwd(name, u, dcat, pool_w, scale, row_off):
    n, pd = u.shape
    ng = len(POOL_WINDOWS)
    gd = pd // ng

    def body(u_ref, dy_ref, w_ref, s_ref, du_ref, dw_ref, ds_ref):
        ds_ref[...] = jnp.zeros_like(ds_ref)
        for g, w in enumerate(POOL_WINDOWS):
            sl = slice(g * gd, (g + 1) * gd)
            ug, dy, wg = u_ref[:, sl], dy_ref[:, sl], w_ref[g]
            cnt = _window_count(n, w)
            p = _window_sum(ug, w, False) / cnt - ug
            ds_ref[0:1, sl] = jnp.sum(dy * _dot(p, wg, NN), axis=0, keepdims=True)
            dys = dy * s_ref[:, sl]
            dw_ref[g] = _dot(p, dys, TN)
            dp = _dot(dys, wg, NT)
            du_ref[:, sl] = (_window_sum(dp / cnt, w, True) - dp).astype(BF)

    rb = row_off // n
    return _pcall(body, name=name, grid=(1,),
                  in_specs=[pl.BlockSpec((n, pd), lambda i: (0, 0)), pl.BlockSpec((n, pd), lambda i: (rb, 0)),
                            pl.BlockSpec(pool_w.shape, lambda i: (0, 0, 0)), pl.BlockSpec(scale.shape, lambda i: (0, 0))],
                  out_specs=[pl.BlockSpec((n, pd), lambda i: (0, 0)), pl.BlockSpec((ng, gd, gd), lambda i: (0, 0, 0)),
                             pl.BlockSpec((8, pd), lambda i: (0, 0))],
                  out_shape=[jax.ShapeDtypeStruct((n, pd), BF), jax.ShapeDtypeStruct((ng, gd, gd), F32),
                             jax.ShapeDtypeStruct((8, pd), F32)],
                  compiler_params=_params(("arbitrary",)))(u, dcat, pool_w, scale)


def _edge_shift(z, k):
    n = z.shape[0]
    t = lax.broadcasted_iota(jnp.int32, (n, 1), 0)
    keep = (t >= k) if k > 0 else (t < n + k)
    return jnp.where(keep, pltpu.roll(z, k % n, 0), 0.0)


def _conv_fwd(name, p3, cw, tc):
    n, cd = p3.shape[0], p3.shape[1] // 3
    nb = cd // tc

    def body(b_ref, c_ref, v_ref, w_ref, y_ref):
        z = c_ref[...] * v_ref[...]
        w = w_ref[...]
        zc = w[0:1] * _edge_shift(z, 1) + w[1:2] * z + w[2:3] * _edge_shift(z, -1)
        y_ref[...] = (b_ref[...] * zc).astype(BF)

    return _pcall(body, name=name, grid=(nb,),
                  in_specs=[pl.BlockSpec((n, tc), lambda j: (0, j)), pl.BlockSpec((n, tc), lambda j: (0, nb + j)),
                            pl.BlockSpec((n, tc), lambda j: (0, 2 * nb + j)), pl.BlockSpec((3, tc), lambda j: (0, j))],
                  out_specs=pl.BlockSpec((n, tc), lambda j: (0, j)), out_shape=jax.ShapeDtypeStruct((n, cd), BF),
                  compiler_params=_params(("arbitrary",)))(p3, p3, p3, cw)


def _conv_bwd(name, p3, cw, dy, tc):
    n, cd = dy.shape
    nb = cd // tc

    def body(b_ref, c_ref, v_ref, w_ref, dy_ref, dp_ref, dw_ref):
        cv, vv, w, dyv = c_ref[...], v_ref[...], w_ref[...], dy_ref[...]
        z = cv * vv
        zl, zr = _edge_shift(z, 1), _edge_shift(z, -1)
        zc = w[0:1] * zl + w[1:2] * z + w[2:3] * zr
        dzc = dyv * b_ref[...]
        dz = w[0:1] * _edge_shift(dzc, -1) + w[1:2] * dzc + w[2:3] * _edge_shift(dzc, 1)
        dp_ref[0] = (dyv * zc).astype(BF)
        dp_ref[1] = (dz * vv).astype(BF)
        dp_ref[2] = (dz * cv).astype(BF)
        dw_ref[...] = jnp.zeros_like(dw_ref)
        dw_ref[0:1, :] = jnp.sum(dzc * zl, axis=0, keepdims=True)
        dw_ref[1:2, :] = jnp.sum(dzc * z, axis=0, keepdims=True)
        dw_ref[2:3, :] = jnp.sum(dzc * zr, axis=0, keepdims=True)

    col = pl.BlockSpec((n, tc), lambda j: (0, j))
    return _pcall(body, name=name, grid=(nb,),
                  in_specs=[col, pl.BlockSpec((n, tc), lambda j: (0, nb + j)),
                            pl.BlockSpec((n, tc), lambda j: (0, 2 * nb + j)), pl.BlockSpec((3, tc), lambda j: (0, j)), col],
                  out_specs=[pl.BlockSpec((3, n, tc), lambda j: (0, 0, j)), pl.BlockSpec((8, tc), lambda j: (0, j))],
                  out_shape=[jax.ShapeDtypeStruct((3, n, cd), BF), jax.ShapeDtypeStruct((8, cd), F32)],
                  compiler_params=_params(("arbitrary",)))(p3, p3, p3, cw, dy)


def _conv_din(name, dp3, w_in, tm):
    _, n, cd = dp3.shape
    d = w_in.shape[0]

    def body(a_ref, w_ref, o_ref, acc_ref):
        j = pl.program_id(1)
        part = _dot(a_ref[...], w_ref[...], NT)

        @pl.when(j == 0)
        def _():
            acc_ref[...] = part

        @pl.when(j > 0)
        def _():
            acc_ref[...] += part

        @pl.when(j == 2)
        def _():
            o_ref[...] = acc_ref[...]

    return _pcall(body, name=name, grid=(n // tm, 3),
                  in_specs=[pl.BlockSpec((None, tm, cd), lambda i, j: (j, i, 0)),
                            pl.BlockSpec((d, cd), lambda i, j: (0, j))],
                  out_specs=pl.BlockSpec((tm, d), lambda i, j: (i, 0)), out_shape=jax.ShapeDtypeStruct((n, d), F32),
                  scratch_shapes=[pltpu.VMEM((tm, d), F32)],
                  compiler_params=_params(("arbitrary", "arbitrary")))(dp3, w_in)


def _conv_dw_in(name, u, dp3, tmm, tn):
    n, d = u.shape
    cd = dp3.shape[2]
    nb = cd // tn

    def body(u_ref, z_ref, o_ref):
        o_ref[...] = _dot(u_ref[...], z_ref[...], TN)

    return _pcall(body, name=name, grid=(3 * nb, d // tmm),
                  in_specs=[pl.BlockSpec((n, tmm), lambda j, mi: (0, mi)),
                            pl.BlockSpec((None, n, tn), lambda j, mi: (j // nb, 0, j % nb))],
                  out_specs=pl.BlockSpec((tmm, tn), lambda j, mi: (mi, j)),
                  out_shape=jax.ShapeDtypeStruct((d, 3 * cd), F32),
                  compiler_params=_params(("arbitrary", "arbitrary")))(u, dp3)


def _loss_head(name, h, target, gain, tm):
    d = h.shape[1]

    def fn(rows, consts, m):
        hv, tv = rows
        g = consts[0][0:1, :]
        n, r = _rms(hv)
        err = n * g - tv
        dy = err / d
        loss = 0.5 * jnp.sum(err * err) / d
        acc = {0: jnp.sum(dy * n, axis=0, keepdims=True), 1: jnp.full((1, d), loss, F32)}
        return [_rms_bwd(dy * g, n, r)], acc

    return _rows_call(name, fn, h.shape[0], tm, [h, target], [gain], None, [(d, F32)], acc_w=d)


def _adamw(name, w, g, m, v, after=None):
    shape = w.shape
    if w.ndim == 1:
        shape2 = (1,) + shape
        res = _adamw(name, *[t.reshape(shape2) for t in (w, g, m, v)], after=after)
        return [t.reshape(shape) for t in res]
    if shape[-1] % 128 and shape[-2] % 128 == 0:
        res = _adamw(name, *[jnp.swapaxes(t, -1, -2) for t in (w, g, m, v)], after=after)
        return [jnp.swapaxes(t, -1, -2) for t in res]
    lead, (r, cdim) = shape[:-2], shape[-2:]
    tr = r
    if r * cdim * 4 > (3 << 19):
        tr = _pick(r, max(8, (3 << 19) // (cdim * 4)), 8)
    c1 = 1.0 / (1.0 - ADAM_B1 ** ADAM_STEP)
    c2 = 1.0 / (1.0 - ADAM_B2 ** ADAM_STEP)
    nl = len(lead)

    def body(w_ref, g_ref, m_ref, v_ref, *rest):
        d_ref, nm_ref, nv_ref = rest[-3:]
        gv = g_ref[...]
        nm = ADAM_B1 * m_ref[...] + (1.0 - ADAM_B1) * gv
        nv = ADAM_B2 * v_ref[...] + (1.0 - ADAM_B2) * (gv * gv)
        nm_ref[...] = nm
        nv_ref[...] = nv
        d_ref[...] = -ADAM_LR * ((nm * c1) / (jnp.sqrt(nv * c2) + ADAM_EPS) + ADAM_WD * w_ref[...])

    spec = pl.BlockSpec((None,) * nl + (tr, cdim), lambda *idx: idx + (0,))
    extra = [] if after is None else [after]
    res = _pcall(body, name=name, grid=lead + (r // tr,),
                 in_specs=[spec] * 4 + [pl.BlockSpec(memory_space=pl.ANY)] * len(extra), out_specs=[spec] * 3,
                 out_shape=[jax.ShapeDtypeStruct(shape, F32)] * 3,
                 compiler_params=_params(("arbitrary",) * (nl + 1)))(w, g, m, v, *extra)
    return list(res)


def _adamw_piece(name, w, g_piece, m, v, at, outs, after=None):
    shape = w.shape
    nl = len(at)
    r, cdim = shape[-2:]
    assert shape[nl:] == g_piece.shape and len(shape) == nl + 2
    tr = _pick(r, max(8, (3 << 19) // (cdim * 4)), 8) if r * cdim * 4 > (3 << 19) else r
    c1 = 1.0 / (1.0 - ADAM_B1 ** ADAM_STEP)
    c2 = 1.0 / (1.0 - ADAM_B2 ** ADAM_STEP)
    if outs is None:
        outs = [lax.empty(shape, F32) for _ in range(4)]

    def body(w_ref, g_ref, m_ref, v_ref, *rest):
        go_ref, d_ref, nm_ref, nv_ref = rest[-4:]
        gv = g_ref[...]
        nm = ADAM_B1 * m_ref[...] + (1.0 - ADAM_B1) * gv
        nv = ADAM_B2 * v_ref[...] + (1.0 - ADAM_B2) * (gv * gv)
        go_ref[...] = gv
        nm_ref[...] = nm
        nv_ref[...] = nv
        d_ref[...] = -ADAM_LR * ((nm * c1) / (jnp.sqrt(nv * c2) + ADAM_EPS) + ADAM_WD * w_ref[...])

    full = pl.BlockSpec((None,) * nl + (tr, cdim), lambda i: tuple(at) + (i, 0))
    extra = [] if after is None else [after]
    res = _pcall(body, name=name, grid=(r // tr,),
                 in_specs=[full, pl.BlockSpec((tr, cdim), lambda i: (i, 0)), full, full]
                 + [pl.BlockSpec(memory_space=pl.ANY)] * (4 + len(extra)),
                 out_specs=[full] * 4, out_shape=[jax.ShapeDtypeStruct(shape, F32)] * 4,
                 input_output_aliases={4 + j: j for j in range(4)},
                 compiler_params=_params(("arbitrary",)))(w, g_piece, m, v, *outs, *extra)
    return list(res)


def _ffn_half_fwd(tag, s, gains, mod, k, wts, n_lat, tm, h_tiles, tm_big, after=None):
    wg, wu, wd, idx = wts
    u = _adaln_fwd(f"adaln_{tag}", s, gains, k, mod, k, tm, h_tiles, after)
    a, b, hid = _ffn_up(f"ffn_up_{tag}", u, wg, wu, idx, _pick(u.shape[0], 1152, 128))
    s_out, o = _ffn_down(f"ffn_down_{tag}", hid, wd, idx, s, mod, k, n_lat, tm_big)
    return s_out, (s, u, a, b, hid, o)


def _ffn_half_bwd(tag, ds_out, saved, gains, mod, k, wts, big_grads, tm, h_tiles, tm_big, after=None):
    wg, wu, wd, idx = wts
    g_gate, g_up, g_down = big_grads
    s, u, a, b, hid, o = saved
    d_o, acc_g = _resid_bwd(f"resid_bwd_{tag}", ds_out, o, mod, k, 0.5, tm, h_tiles, after)
    da, db = _ffn_dhid(f"ffn_dhid_{tag}", d_o, wd, idx, a, b, _pick(d_o.shape[0], 1152, 128))
    du = _ffn_du(f"ffn_du_{tag}", da, db, wg, wu, idx, tm_big)
    d = u.shape[1]
    g_gate, g_up = _ffn_dw_in(f"ffn_dwgu_{tag}", u, da, db, _pick(d, MM_ROWS), g_gate, g_up, idx)
    g_down = _ffn_dw_down(f"ffn_dwd_{tag}", hid, d_o, _pick(d, 512), g_down, idx)
    ds, acc_n = _adaln_bwd(f"adaln_bwd_{tag}", s, du, ds_out, gains, k, mod, k, tm, h_tiles)
    return ds, (g_gate, g_up, g_down), (acc_n[:, 0], acc_n[:, 1], acc_g[:, 0]), jnp.sum(acc_n[:, 2], axis=0)


def kernel(x, c, ctx, c_ctx, norm_g, w_mod, b_mod, ffn_w_gate, ffn_w_up, ffn_w_down, ab_w_in, pool_w, pool_scale, q_norm_g, w_uq, kv_norm_g, w_ukv, ab_w_out, conv_w_in, conv_w, conv_w_out, final_norm_g, loss_target, m_c_ctx, m_norm_g, m_w_mod, m_b_mod, m_ffn_w_gate, m_ffn_w_up, m_ffn_w_down, m_ab_w_in, m_pool_w, m_pool_scale, m_q_norm_g, m_w_uq, m_kv_norm_g, m_w_ukv, m_ab_w_out, m_conv_w_in, m_conv_w, m_conv_w_out, m_final_norm_g, v_c_ctx, v_norm_g, v_w_mod, v_b_mod, v_ffn_w_gate, v_ffn_w_up, v_ffn_w_down, v_ab_w_in, v_pool_w, v_pool_scale, v_q_norm_g, v_w_uq, v_kv_norm_g, v_w_ukv, v_ab_w_out, v_conv_w_in, v_conv_w, v_conv_w_out, v_final_norm_g):
    weights = dict(c_ctx=c_ctx, norm_g=norm_g, w_mod=w_mod, b_mod=b_mod, ffn_w_gate=ffn_w_gate, ffn_w_up=ffn_w_up,
                   ffn_w_down=ffn_w_down, ab_w_in=ab_w_in, pool_w=pool_w, pool_scale=pool_scale, q_norm_g=q_norm_g,
                   w_uq=w_uq, kv_norm_g=kv_norm_g, w_ukv=w_ukv, ab_w_out=ab_w_out, conv_w_in=conv_w_in, conv_w=conv_w,
                   conv_w_out=conv_w_out, final_norm_g=final_norm_g)
    mom_m = dict(c_ctx=m_c_ctx, norm_g=m_norm_g, w_mod=m_w_mod, b_mod=m_b_mod, ffn_w_gate=m_ffn_w_gate,
                 ffn_w_up=m_ffn_w_up, ffn_w_down=m_ffn_w_down, ab_w_in=m_ab_w_in, pool_w=m_pool_w,
                 pool_scale=m_pool_scale, q_norm_g=m_q_norm_g, w_uq=m_w_uq, kv_norm_g=m_kv_norm_g, w_ukv=m_w_ukv,
                 ab_w_out=m_ab_w_out, conv_w_in=m_conv_w_in, conv_w=m_conv_w, conv_w_out=m_conv_w_out,
                 final_norm_g=m_final_norm_g)
    mom_v = dict(c_ctx=v_c_ctx, norm_g=v_norm_g, w_mod=v_w_mod, b_mod=v_b_mod, ffn_w_gate=v_ffn_w_gate,
                 ffn_w_up=v_ffn_w_up, ffn_w_down=v_ffn_w_down, ab_w_in=v_ab_w_in, pool_w=v_pool_w,
                 pool_scale=v_pool_scale, q_norm_g=v_q_norm_g, w_uq=v_w_uq, kv_norm_g=v_kv_norm_g, w_ukv=v_w_ukv,
                 ab_w_out=v_ab_w_out, conv_w_in=v_conv_w_in, conv_w=v_conv_w, conv_w_out=v_conv_w_out,
                 final_norm_g=v_final_norm_g)

    t_len, d = x.shape[1], x.shape[2]
    g_len = ctx.shape[1]
    r_len = t_len + g_len
    fc = ffn_w_gate.shape[3]
    heads = d // 128
    pool_dim = d // 2
    q_rank, kv_rank = q_norm_g.shape[1], kv_norm_g.shape[1]
    hw = heads * HEAD_PAD
    attn_scale = 1.0 / math.sqrt(QK_NOPE + QK_ROPE)
    kvr_w = kv_rank + HEAD_PAD
    tm = 256 if g_len % 256 == 0 else g_len
    assert t_len % tm == 0 and g_len % tm == 0 and t_len % g_len == 0 and pool_dim % 128 == 0
    h_tiles = t_len // tm
    tm_l0 = _pick(r_len, 768, tm)
    tm_l1 = _pick(t_len, 1024, tm)

    xi, yi, ci = lax.axis_index("x"), lax.axis_index("y"), lax.axis_index("c")
    me = 4 * xi + 2 * yi + ci
    shard = 2 * xi + yi

    def halves(w):
        return w.astype(BF).reshape(2, -1, w.shape[-1])

    ffn_names = ["ffn_w_gate", "ffn_w_up", "ffn_w_down"]
    mixer_names = [["ab_w_in", "w_uq", "w_ukv", "ab_w_out"], ["conv_w_in", "conv_w_out"]]
    big_names = ffn_names + mixer_names[0] + mixer_names[1]

    def stage_names(k):
        return mixer_names[k // 3] if k % 3 == 1 else ffn_names

    def stage_halves(k):
        l, f = k // 3, (k % 3) // 2
        if k % 3 == 1:
            return [halves(weights[nm]) for nm in mixer_names[l]]
        return [halves(weights[nm][l, f]) for nm in ffn_names]

    def gather_start(k, dep):
        own = lax.optimization_barrier((tuple(stage_halves(k)), dep))[0]
        n = len(own)
        return _split_start(f"gather_start_s{k}", list(own) + _landing(N_DEV, own), n * len(CHIP_FLIPS),
                            _chips_gather_build(n))

    gather0 = gather_start(0, c)

    small = jnp.concatenate([norm_g.reshape(6, -1), conv_w[0]], axis=0)
    small = jnp.pad(small, ((0, 7), (0, 0)))
    c_row = jnp.pad(c, ((0, 7), (0, 0))) + gather0['token'][0, 0]
    small_all, c_all = _gather_all("gather_small", [small, c_row], two_level=True)
    small_full = small_all[::2].transpose(1, 0, 2).reshape(16, d)
    gains = [jnp.pad(small_full[3 * l:3 * l + 3], ((0, 5), (0, 0))) for l in range(2)]
    conv_w_full = small_full[6:9]
    c16 = jnp.concatenate([c_all[:, 0], c_ctx[None], jnp.zeros((7, d), F32)], axis=0)

    n_col = w_mod.shape[2]
    b_sh = lax.dynamic_slice_in_dim(b_mod, shard * n_col, n_col, axis=1)
    m_sh = [_mm(f"mod_fwd_{l}", c16, w_mod, 'nn', tm=16, tn=768, a_pre=_silu, b_lead=l,
                epi=lambda acc, i, bv: (acc + bv,), epi_args=(b_sh[l:l + 1],), epi_kinds=('n',)) for l in range(2)]
    m_all = _gather_all("gather_mod", [jnp.concatenate(m_sh, axis=0)], two_level=True)[0]
    m_full = m_all[::2].reshape(N_SHARD, 2, 16, n_col).transpose(1, 2, 0, 3).reshape(2, 16, N_MOD * d)
    mod_h = [jnp.pad(lax.dynamic_index_in_dim(m_full[l], me, 0, keepdims=False).reshape(N_MOD, d), ((0, 7), (0, 0)))
             for l in range(2)]
    mod_g0 = jnp.pad(m_full[0, 8].reshape(N_MOD, d), ((0, 7), (0, 0)))
    mods = [jnp.stack([mod_h[0], mod_g0]), mod_h[1][None]]

    def stage_weights(k, handle, after):
        bufs = _split_wait(f"gather_wait_s{k}", handle, after)
        n = len(bufs) // 2
        own = bufs[:n]
        fwd = _split_start(f"forward_start_s{k}", bufs[n:], n * len(CHIP_FLIPS), _sibling_forward_build(n))
        nxt = gather_start(k + 1, fwd['token']) if k + 1 < 6 else None
        landed = _split_wait(f"forward_wait_s{k}", fwd, fwd['token'])
        full = [lax.dynamic_update_slice_in_dim(z, a, 2 * shard, 0) for z, a in zip(landed, own)]
        gw = {nm: g.reshape(N_SHARD, 2 * g.shape[1], g.shape[2]) for nm, g in zip(stage_names(k), full)}
        return gw, nxt, (fwd['token'] if nxt is None else nxt['token'])

    def ffn_weights(gw):
        return gw["ffn_w_gate"].reshape(N_SHARD, 1, d, fc), gw["ffn_w_up"].reshape(N_SHARD, 1, d, fc), \
            gw["ffn_w_down"], 0

    gw_s0, gather1, tok0 = stage_weights(0, gather0, m_all)
    ffn_w = [[ffn_weights(gw_s0), None], [None, None]]

    s0 = jnp.concatenate([x[0], ctx[0]], axis=0)
    s1, sav_f00 = _ffn_half_fwd("l0a", s0, gains[0], mods[0], 0, ffn_w[0][0], t_len, tm, h_tiles, tm_l0, tok0)

    gw_s1, gather2, tok1 = stage_weights(1, gather1, s1)
    w_out_full = gw_s1["ab_w_out"].reshape(-1, d)
    w_uq_full = gw_s1["w_uq"].reshape(q_rank, heads * (QK_NOPE + QK_ROPE))
    w_ukv_full = gw_s1["w_ukv"].transpose(1, 0, 2).reshape(kv_rank, heads * (QK_NOPE + V_HEAD))
    w_in_full = gw_s1["ab_w_in"].transpose(1, 0, 2).reshape(d, -1)

    wq_p = jnp.pad(w_uq_full.reshape(q_rank, heads, QK_NOPE + QK_ROPE),
                   ((0, 0), (0, 0), (0, HEAD_PAD - QK_NOPE - QK_ROPE))).reshape(q_rank, hw)
    ukv3 = w_ukv_full.reshape(kv_rank, heads, QK_NOPE + V_HEAD)
    wk_top = jnp.pad(ukv3[..., :QK_NOPE], ((0, 0), (0, 0), (0, HEAD_PAD - QK_NOPE))).reshape(kv_rank, hw)
    wv_top = jnp.pad(ukv3[..., QK_NOPE:], ((0, 0), (0, 0), (0, HEAD_PAD - V_HEAD))).reshape(kv_rank, hw)
    src_row = lax.broadcasted_iota(jnp.int32, (HEAD_PAD, hw), 0)
    dst_lane = lax.broadcasted_iota(jnp.int32, (HEAD_PAD, hw), 1) % HEAD_PAD
    spread = ((src_row < QK_ROPE) & (dst_lane == src_row + QK_NOPE)).astype(BF)
    wk_ext = jnp.concatenate([wk_top, spread], axis=0)
    wv_ext = jnp.concatenate([wv_top, jnp.zeros((HEAD_PAD, hw), BF)], axis=0)
    w_in_pool = w_in_full[:, :pool_dim]
    w_in_q = w_in_full[:, pool_dim:pool_dim + q_rank]
    w_in_kvr = jnp.pad(w_in_full[:, pool_dim + q_rank:], ((0, 0), (0, HEAD_PAD - QK_ROPE)))
    w_out_attn = jnp.pad(w_out_full[pool_dim:].reshape(heads, V_HEAD, d),
                         ((0, 0), (0, HEAD_PAD - V_HEAD), (0, 0))).reshape(hw, d)
    w_out_p = jnp.concatenate([w_out_full[:pool_dim], w_out_attn], axis=0)

    u_mix = _adaln_fwd("adaln_l0m", s1, gains[0], 1, mods[0], 1, tm, h_tiles, tok1)
    p_pool = _mm("in_pool", u_mix, w_in_pool, 'nn', tm=tm_l0, tn=pool_dim)
    p_q = _mm("in_q", u_mix, w_in_q, 'nn', tm=tm_l0, tn=q_rank)
    p_kvr = _mm("in_kvr", u_mix, w_in_kvr, 'nn', tm=tm_l0, tn=kvr_w)
    qg = jnp.pad(q_norm_g, ((0, 7), (0, 0)))
    kvg = jnp.pad(kv_norm_g, ((0, 7), (0, 0)))
    tq_c, tq_s = _rope_tables(t_len, g_len, QK_NOPE)
    tk_c, tk_s = _rope_tables(t_len, g_len, 0)

    def qn_fn(rows, consts, m):
        n, _ = _rms(rows[0])
        return [n * consts[0][0:1, :]], {}

    qn = _rows_call("q_norm", qn_fn, r_len, tm, [p_q], [qg], None, [(q_rank, BF)])[0]
    q_r = _mm("q_up", qn, wq_p, 'nn', tm=tm_l0, tn=hw, out_dtypes=(BF,),
              epi=lambda acc, i, ct, st: (_rope(acc, ct, st),), epi_args=(tq_c, tq_s), epi_kinds=('mt', 'mt'))

    def kvn_fn(rows, consts, m):
        pv, ct, st = rows
        n, _ = _rms(pv[:, :kv_rank])
        return [jnp.concatenate([n * consts[0][0:1, :], _rope(pv[:, kv_rank:], ct, st)], axis=1)], {}

    kvn = _rows_call("kv_norm", kvn_fn, r_len, tm, [p_kvr, tk_c, tk_s], [kvg], None, [(kvr_w, BF)])[0]
    k_p = _mm("k_up", kvn, wk_ext, 'nn', tm=tm_l0, tn=hw, out_dtypes=(BF,))
    v_p = _mm("v_up", kvn, wv_ext, 'nn', tm=tm_l0, tn=hw, out_dtypes=(BF,))
    tq_h = _pick(t_len, 512, tm)
    o_h, lse_h = _attn_fwd("attn_h", q_r, k_p, v_p, t_len, 0, r_len, 0, heads, tm, attn_scale)
    o_g, lse_g = _attn_fwd("attn_g", q_r, k_p, v_p, g_len, t_len, g_len, t_len // g_len, heads, tm, attn_scale)
    y_h = _pool_fwd("pool_h", p_pool[:t_len], pool_w[0], pool_scale)
    y_g = _pool_fwd("pool_g", p_pool[t_len:], pool_w[0], pool_scale)
    cat = jnp.concatenate([jnp.concatenate([y_h, y_g], axis=0), jnp.concatenate([o_h, o_g], axis=0)], axis=1)

    def resid_epi(k3, n_lat, tmr):
        def epi(acc, i, sv, mv):
            return sv + _row_gate(mv, k3, i, tmr, n_lat) * acc, acc
        return epi

    s2, o_mix0 = _mm("mix_out_l0", cat, w_out_p, 'nn', tm=tm_l0, tn=d, out_dtypes=(F32, F32),
                     epi=resid_epi(5, t_len, tm_l0), epi_args=(s1, mods[0]), epi_kinds=('mn', 'w'))
    gw_s2, gather3, tok2 = stage_weights(2, gather2, s2)
    ffn_w[0][1] = ffn_weights(gw_s2)
    s3, sav_f01 = _ffn_half_fwd("l0b", s2, gains[0], mods[0], 2, ffn_w[0][1], t_len, tm, h_tiles, tm_l0, tok2)

    gw_s3, gather4, tok3 = stage_weights(3, gather3, s3)
    ffn_w[1][0] = ffn_weights(gw_s3)
    tml = 256 if t_len % 256 == 0 else tm
    h3 = s3[:t_len]
    h4, sav_f10 = _ffn_half_fwd("l1a", h3, gains[1], mods[1], 0, ffn_w[1][0], t_len, tml, None, tm_l1, tok3)
    gw_s4, gather5, tok4 = stage_weights(4, gather4, h4)
    cw_out_full = gw_s4["conv_w_out"].reshape(-1, d)
    cw_in_full = gw_s4["conv_w_in"].transpose(1, 0, 2).reshape(d, -1)
    u_cv = _adaln_fwd("adaln_l1m", h4, gains[1], 1, mods[1], 1, tml, None, tok4)
    p3 = _mm("conv_in", u_cv, cw_in_full, 'nn', tm=tm_l1, tn=512)
    cwp = conv_w_full
    tc = _pick(d, 256)
    y_cv = _conv_fwd("conv_fwd", p3, cwp, tc)
    h5, o_mix1 = _mm("mix_out_l1", y_cv, cw_out_full, 'nn', tm=tm_l1, tn=d, out_dtypes=(F32, F32),
                     epi=resid_epi(5, t_len, tm_l1), epi_args=(h4, mods[1]), epi_kinds=('mn', 'w'))
    gw_s5, _, tok5 = stage_weights(5, gather5, h5)
    ffn_w[1][1] = ffn_weights(gw_s5)
    h6, sav_f11 = _ffn_half_fwd("l1b", h5, gains[1], mods[1], 2, ffn_w[1][1], t_len, tml, None, tm_l1, tok5)

    fg = jnp.pad(final_norm_g[None], ((0, 7), (0, 0)))
    dh6, acc_loss = _loss_head("loss_head", h6, loss_target[0], fg, tml)
    d_final_g = acc_loss[0, 0]

    dgain = [[None] * 3 for _ in range(2)]
    dmod = [[None] * N_MOD for _ in range(2)]

    def put(l, k, triple):
        dmod[l][3 * k], dmod[l][3 * k + 1], dmod[l][3 * k + 2] = triple

    def empty_ffn_grads():
        return (lax.empty((N_SHARD, d, fc), BF), lax.empty((N_SHARD, d, fc), BF), lax.empty((N_SHARD, fc, d), BF))

    def by_shard_rows(g):
        return g.reshape(N_SHARD, -1, g.shape[-1])

    def by_shard_cols(g):
        return g.reshape(g.shape[0], N_SHARD, -1).transpose(1, 0, 2)

    def pair_start(k, big):
        send = [b.astype(BF).reshape(N_DEV, b.shape[1] // 2, b.shape[2]) for b in big]
        n = len(send)
        return _split_start(f"grads_pair_start_s{k}", send + _landing(N_SHARD, send), n * N_SHARD,
                            _sibling_halves_build(n))

    def chips_start(k, handle, after):
        bufs = _split_wait(f"grads_pair_wait_s{k}", handle, after)
        n = len(bufs) // 2
        pre = [_add_halves(f"grads_add_s{k}_{nm}", s, z) for nm, s, z in zip(stage_names(k), bufs[:n], bufs[n:])]
        return _split_start(f"grads_start_s{k}", pre + _landing(N_SHARD, pre), n * len(CHIP_FLIPS),
                            _chips_scatter_build(n))

    def landed_sums(k, handle, after):
        bufs = _split_wait(f"grads_wait_s{k}", handle, after)
        n = len(bufs) // 2
        landed = [lax.dynamic_update_slice_in_dim(z, lax.dynamic_slice_in_dim(p, shard, 1, 0), shard, 0)
                  for p, z in zip(bufs[:n], bufs[n:])]
        return [_sum_lead(f"sum_grads_s{k}_{nm}", z) for nm, z in zip(stage_names(k), landed)]

    pair, scatter, sums = [None] * 6, [None] * 6, [None] * 6
    grads, upd = {}, {}
    ffn_out = {nm: None for nm in ffn_names}

    def lanes_last(nm, t):
        shp = weights[nm].shape
        return jnp.swapaxes(t, -1, -2) if shp[-1] % 128 and shp[-2] % 128 == 0 else t

    def finish_stage(k, halves):
        n = len(halves)
        lands = [pltpu.with_memory_space_constraint(lax.empty(h.shape, h.dtype), pltpu.HBM) for h in halves]
        swap = _split_start(f"swap_start_s{k}", list(halves) + lands, n, _sibling_whole_build(n))
        both = _split_wait(f"swap_wait_s{k}", swap, swap['token'])
        for nm, a, g in zip(stage_names(k), both[:n], both[n:]):
            piece = jnp.where(ci == 0, jnp.concatenate([a, g], axis=0), jnp.concatenate([g, a], axis=0))
            if k % 3 == 1:
                grads[nm] = piece.reshape(weights[nm].shape)
                upd[nm] = _adamw(f"adamw_{nm}", weights[nm], grads[nm], mom_m[nm], mom_v[nm])
            else:
                ffn_out[nm] = _adamw_piece(f"adamw_{nm}_s{k}", lanes_last(nm, weights[nm]), lanes_last(nm, piece),
                                           lanes_last(nm, mom_m[nm]), lanes_last(nm, mom_v[nm]),
                                           (k // 3, (k % 3) // 2), ffn_out[nm])
    dh5, ffn_g, tr, dgain[1][2] = _ffn_half_bwd("l1b", dh6, sav_f11, gains[1], mods[1], 2, ffn_w[1][1],
                                                empty_ffn_grads(), tml, None, tm_l1)
    put(1, 2, tr)
    pair[5] = pair_start(5, list(ffn_g))
    d_o1, acc_g1 = _resid_bwd("resid_bwd_l1m", dh5, o_mix1, mods[1], 1, 1.0, tml, None, pair[5]['token'])
    dy_cv = _mm("mix_out_l1_dx", d_o1, cw_out_full, 'nt', tm=tm_l1, tn=d)
    d_cw_out = _mm("mix_out_l1_dw", y_cv, d_o1, 'tn', tm=512, tn=512)
    dp3, d_cw = _conv_bwd("conv_bwd", p3, cwp, dy_cv, tc)
    du_cv = _conv_din("conv_in_dx", dp3, cw_in_full, tm_l1)
    d_cw_in = _conv_dw_in("conv_in_dw", u_cv, dp3, _pick(d, MM_ROWS), _pick(d, 512))
    dh4, acc_n1 = _adaln_bwd("adaln_bwd_l1m", h4, du_cv, dh5, gains[1], 1, mods[1], 1, tml, None)
    put(1, 1, (acc_n1[:, 0], acc_n1[:, 1], acc_g1[:, 0]))
    dgain[1][1] = acc_n1[0, 2]
    scatter[5] = chips_start(5, pair[5], dh4)
    pair[4] = pair_start(4, [by_shard_cols(d_cw_in), by_shard_rows(d_cw_out)])
    dh3, ffn_g, tr, dgain[1][0] = _ffn_half_bwd("l1a", dh4, sav_f10, gains[1], mods[1], 0, ffn_w[1][0],
                                                empty_ffn_grads(), tml, None, tm_l1,
                                                scatter[5]['token'] + pair[4]['token'])
    put(1, 0, tr)
    finish_stage(5, landed_sums(5, scatter[5], dh3))
    scatter[4] = chips_start(4, pair[4], dh3)
    pair[3] = pair_start(3, list(ffn_g))

    ds3 = jnp.concatenate([dh3, jnp.zeros((g_len, d), F32)], axis=0) \
        + (scatter[4]['token'][0, 0] + pair[3]['token'][0, 0])
    ds2, ffn_g, tr, dgain[0][2] = _ffn_half_bwd("l0b", ds3, sav_f01, gains[0], mods[0], 2, ffn_w[0][1],
                                                empty_ffn_grads(), tm, h_tiles, tm_l0)
    put(0, 2, tr)
    finish_stage(4, landed_sums(4, scatter[4], ds2))
    scatter[3] = chips_start(3, pair[3], ds2)
    pair[2] = pair_start(2, list(ffn_g))
    d_o0, acc_g0 = _resid_bwd("resid_bwd_l0m", ds2, o_mix0, mods[0], 1, 1.0, tm, h_tiles,
                              scatter[3]['token'] + pair[2]['token'])
    dcat = _mm("mix_out_l0_dx", d_o0, w_out_p, 'nt', tm=tm_l0, tn=pool_dim + hw)
    d_w_out_p = _mm("mix_out_l0_dw", cat, d_o0, 'tn', tm=512, tn=512)
    col_blk = pool_dim // HEAD_PAD
    dq_h, dk_h, dv_h = _attn_bwd("attn_bwd_h", q_r, k_p, v_p, cat, dcat, lse_h, t_len, 0, r_len, 0, heads, tq_h,
                                 attn_scale, col_blk)
    dq_g, dk_all, dv_all = _attn_bwd("attn_bwd_g", q_r, k_p, v_p, cat, dcat, lse_g, g_len, t_len, g_len,
                                     t_len // g_len, heads, tm, attn_scale, col_blk, onto=(dk_h, dv_h))
    dq_all = jnp.concatenate([dq_h, dq_g], axis=0)
    dkvn = _mm("k_up_dx", dk_all, wk_ext, 'nt', tm=tm_l0, tn=kvr_w)
    dkvn = _mm("v_up_dx", dv_all, wv_ext, 'nt', tm=tm_l0, tn=kvr_w, epi=lambda acc, i, prev: (acc + prev,),
               epi_args=(dkvn,), epi_kinds=('mn',))
    d_wk_ext = _mm("k_up_dw", kvn, dk_all, 'tn', tm=kvr_w, tn=512)
    d_wv_ext = _mm("v_up_dw", kvn, dv_all, 'tn', tm=kvr_w, tn=512)

    def kvn_bwd_fn(rows, consts, m):
        pv, dv_, ct, st = rows
        g = consts[0][0:1, :]
        n, r = _rms(pv[:, :kv_rank])
        dyn = dv_[:, :kv_rank]
        dckv = _rms_bwd(dyn * g, n, r)
        dkr = _rope_t(dv_[:, kv_rank:], ct, st)
        return [jnp.concatenate([dckv, dkr], axis=1)], {0: jnp.sum(dyn * n, axis=0, keepdims=True)}

    dp_kvr, acc_kvg = _rows_call("kv_norm_bwd", kvn_bwd_fn, r_len, tm, [p_kvr, dkvn, tk_c, tk_s], [kvg], None,
                                 [(kvr_w, BF)], acc_w=kv_rank)

    def qrope_bwd_fn(rows, consts, m):
        return [_rope_t(rows[0], rows[1], rows[2])], {}

    dq_pad = _rows_call("q_rope_bwd", qrope_bwd_fn, r_len, tm, [dq_all, tq_c, tq_s], [], None, [(hw, BF)])[0]
    dqn = _mm("q_up_dx", dq_pad, wq_p, 'nt', tm=tm_l0, tn=q_rank)
    d_wq_p = _mm("q_up_dw", qn, dq_pad, 'tn', tm=512, tn=512)

    def qn_bwd_fn(rows, consts, m):
        pv, dv_ = rows
        g = consts[0][0:1, :]
        n, r = _rms(pv)
        return [_rms_bwd(dv_ * g, n, r)], {0: jnp.sum(dv_ * n, axis=0, keepdims=True)}

    dp_q, acc_qg = _rows_call("q_norm_bwd", qn_bwd_fn, r_len, tm, [p_q, dqn], [qg], None, [(q_rank, BF)],
                              acc_w=q_rank)
    dpu_h, dpw_h, dps_h = _pool_bwd("pool_bwd_h", p_pool[:t_len], dcat, pool_w[0], pool_scale, 0)
    dpu_g, dpw_g, dps_g = _pool_bwd("pool_bwd_g", p_pool[t_len:], dcat, pool_w[0], pool_scale, t_len)
    dp_pool = jnp.concatenate([dpu_h, dpu_g], axis=0)
    add_prev = lambda acc, i, prev: (acc + prev,)
    du_mix = _mm("in_pool_dx", dp_pool, w_in_pool, 'nt', tm=tm_l0, tn=d)
    du_mix = _mm("in_q_dx", dp_q, w_in_q, 'nt', tm=tm_l0, tn=d, epi=add_prev, epi_args=(du_mix,), epi_kinds=('mn',))
    du_mix = _mm("in_kvr_dx", dp_kvr, w_in_kvr, 'nt', tm=tm_l0, tn=d, epi=add_prev, epi_args=(du_mix,), epi_kinds=('mn',))
    d_w_in = jnp.concatenate([
        _mm("in_pool_dw", u_mix, dp_pool, 'tn', tm=512, tn=pool_dim),
        _mm("in_q_dw", u_mix, dp_q, 'tn', tm=512, tn=q_rank),
        _mm("in_kvr_dw", u_mix, dp_kvr, 'tn', tm=512, tn=kvr_w)[:, :kv_rank + QK_ROPE]], axis=1)
    ds1, acc_n0 = _adaln_bwd("adaln_bwd_l0m", s1, du_mix, ds2, gains[0], 1, mods[0], 1, tm, h_tiles)
    put(0, 1, (acc_n0[:, 0], acc_n0[:, 1], acc_g0[:, 0]))
    dgain[0][1] = jnp.sum(acc_n0[:, 2], axis=0)
    d_w_uq = d_wq_p.reshape(q_rank, heads, HEAD_PAD)[..., :QK_NOPE + QK_ROPE].reshape(q_rank, -1)
    d_w_ukv = jnp.concatenate([d_wk_ext[:kv_rank].reshape(kv_rank, heads, HEAD_PAD)[..., :QK_NOPE],
                               d_wv_ext[:kv_rank].reshape(kv_rank, heads, HEAD_PAD)[..., :V_HEAD]],
                              axis=-1).reshape(kv_rank, -1)
    d_w_out = jnp.concatenate([d_w_out_p[:pool_dim],
                               d_w_out_p[pool_dim:].reshape(heads, HEAD_PAD, d)[:, :V_HEAD].reshape(-1, d)], axis=0)
    finish_stage(3, landed_sums(3, scatter[3], ds1))
    scatter[2] = chips_start(2, pair[2], ds1)
    pair[1] = pair_start(1, [by_shard_cols(d_w_in), by_shard_rows(d_w_uq), by_shard_cols(d_w_ukv),
                             by_shard_rows(d_w_out)])
    ds0, ffn_g, tr, dgain[0][0] = _ffn_half_bwd("l0a", ds1, sav_f00, gains[0], mods[0], 0, ffn_w[0][0],
                                                empty_ffn_grads(), tm, h_tiles, tm_l0,
                                                scatter[2]['token'] + pair[1]['token'])
    put(0, 0, tr)
    grad_x = ds0[:t_len][None]
    finish_stage(2, landed_sums(2, scatter[2], ds0))
    pair[0] = pair_start(0, list(ffn_g))

    dmh = jnp.stack([jnp.stack([dmod[l][k][0] for k in range(N_MOD)]) for l in range(2)])
    dmg0 = jnp.stack([dmod[0][k][1] for k in range(N_MOD)])
    dg_rows = jnp.stack([dgain[l][k] for l in range(2) for k in range(3)])
    pieces = [dmh.reshape(2 * N_MOD, d), dmg0, dg_rows, d_cw[:3], d_final_g[None],
              (dpw_h + dpw_g).reshape(-1, d), jnp.pad((dps_h + dps_g)[0], (0, d - pool_dim))[None],
              jnp.pad(acc_qg[0, 0], (0, d - q_rank))[None], jnp.pad(acc_kvg[0, 0], (0, d - kv_rank))[None],
              acc_loss[0, 1][None]]
    n_piece = [p.shape[0] for p in pieces]
    pieces = [jnp.pad(p, ((0, (-p.shape[0]) % 8), (0, 0))) for p in pieces]
    small_g = jnp.concatenate(pieces, axis=0) + pair[0]['token'][0, 0]
    sg_all = _gather_all("gather_small_grads", [small_g], two_level=True)[0]
    sg_sum = _sum_lead("sum_small_grads", sg_all)
    scatter[1] = chips_start(1, pair[1], sg_sum)
    offs = [0]
    for p in pieces:
        offs.append(offs[-1] + p.shape[0])
    part = lambda j: sg_sum[offs[j]:offs[j] + n_piece[j]]
    sum_dmh, sum_dmg0, g_norm_full, g_conv_w_full = part(0).reshape(2, N_MOD * d), part(1).reshape(N_MOD * d), part(2), part(3)
    g_final = part(4)[0]
    loss = part(9)[0, 0]
    g_pool_w = part(5).reshape(pool_w.shape)
    g_pool_scale = part(6)[:, :pool_dim]
    g_q_norm = part(7)[:, :q_rank]
    g_kv_norm = part(8)[:, :kv_rank]
    col0 = shard * (d // N_SHARD)
    g_norm_g = lax.dynamic_slice_in_dim(g_norm_full.reshape(2, 3, d), col0, d // N_SHARD, axis=2)
    g_conv_w = lax.dynamic_slice_in_dim(g_conv_w_full, col0, d // N_SHARD, axis=1)[None]
    g_b_mod = _sum_lead("sum_b_mod", jnp.stack([sum_dmh, jnp.stack([sum_dmg0, jnp.zeros_like(sum_dmg0)])]))

    dm16 = []
    for l in range(2):
        per_dev = sg_all[:, l * N_MOD:(l + 1) * N_MOD].reshape(N_DEV, N_MOD * d)
        row8 = (sum_dmg0 if l == 0 else jnp.zeros_like(sum_dmg0)) + scatter[1]['token'][0, 0]
        full = jnp.concatenate([per_dev, row8[None], jnp.zeros((7, N_MOD * d), F32)], axis=0)
        dm16.append(lax.dynamic_slice_in_dim(full, shard * n_col, n_col, axis=1))
    g_w_mod = [_mm(f"mod_dw_{l}", c16, dm16[l], 'tn', tm=512, tn=768, a_pre=_silu) for l in range(2)]
    dc16 = _mm("mod_dx", dm16[0], w_mod, 'nt', tm=16, tn=512, b_lead=0, epi=lambda acc, i, cv: (acc * _dsilu(cv),),
               epi_args=(c16,), epi_kinds=('mn',))
    dc_all = _gather_all("gather_dc", [dc16], two_level=True)[0]
    g_c_ctx = _sum_lead("sum_dc", dc_all[::2])[8]

    grads.update(c_ctx=g_c_ctx, norm_g=g_norm_g, b_mod=g_b_mod, pool_w=g_pool_w,
                 pool_scale=g_pool_scale, q_norm_g=g_q_norm, kv_norm_g=g_kv_norm, conv_w=g_conv_w, final_norm_g=g_final)
    names = list(weights)

    scatter[0] = chips_start(0, pair[0], g_c_ctx)
    upd.update({n: _adamw(f"adamw_{n}", weights[n], grads[n].reshape(weights[n].shape), mom_m[n], mom_v[n],
                          scatter[0]['token']) for n in names if n not in big_names and n != "w_mod"})
    mod_out = None
    for l in range(2):
        mod_out = _adamw_piece(f"adamw_w_mod_{l}", w_mod, g_w_mod[l], mom_m["w_mod"], mom_v["w_mod"], (l,), mod_out,
                               scatter[0]['token'])
    grads["w_mod"], upd["w_mod"] = mod_out[0], mod_out[1:]
    finish_stage(1, landed_sums(1, scatter[1], upd["w_mod"][0]))
    finish_stage(0, landed_sums(0, scatter[0], upd[mixer_names[0][-1]][0]))
    for nm in ffn_names:
        done = [lanes_last(nm, t) for t in ffn_out[nm]]
        grads[nm], upd[nm] = done[0], done[1:]
    return (loss, grad_x, *[grads[n].reshape(weights[n].shape) for n in names], *[upd[n][0] for n in names],
            *[upd[n][1] for n in names], *[upd[n][2] for n in names])
```
